```python
import math
import jax, jax.numpy as jnp
from jax import lax
import numpy as np

D_MODEL = 2048
BATCH = 4
SEQ = 2048
DEPTH = 1
DEC_BATCH = 8
DEC_SEQ = 1
PAST_LEN = 16384
PAGE_SIZE = 128

H_R = 16
HD_R = 64
C_R = H_R * HD_R
LORA_W = 64
LORA_A = 64
LORA_G = 128
C_RIN = 3 * C_R + LORA_W + LORA_A + LORA_G
LN_X_EPS = 64e-5
H_N = 16
HD_N = 64
N_KVG = 4
HPG = H_N // N_KVG
C_N = H_N * HD_N
KV_COLS = 2 * N_KVG * HD_N
C_NIN = C_N + 3 * KV_COLS + 3 * H_N
L_CMP = 32
STRIDE = 16
CMP_HID = 64
L_SEL = 64
N_TOP = 16
WINDOW = 512
Q_BLOCK = 32
N_BUCKETS = 32
MAX_DIST = 128
C_IN = C_RIN + C_NIN + 2 * D_MODEL
N_EGROUPS = 4
EXP_PER_GROUP = 8
N_EXP = N_EGROUPS * EXP_PER_GROUP
TOP_K = 2
D_EXP = 512
MOE_BLOCK = 128
RMS_EPS = 1e-6
NEG_INF = -1e30
FORCE = 1e9

kernel_name = 'hybrid_rwkv7_nsa_hmoe_step'


def rms_norm(x, g):
    xf = x.astype(jnp.float32)
    return xf * lax.rsqrt(jnp.mean(xf * xf, axis=-1, keepdims=True) + RMS_EPS) * g.astype(jnp.float32)


def t5_bucket(dist):
    n = jnp.maximum(dist, 0)
    max_exact = N_BUCKETS // 2
    nf = jnp.maximum(n, 1).astype(jnp.float32)
    large = max_exact + (jnp.log(nf / max_exact) / math.log(MAX_DIST / max_exact)
                         * (N_BUCKETS - max_exact)).astype(jnp.int32)
    large = jnp.minimum(large, N_BUCKETS - 1)
    return jnp.where(n < max_exact, n, large)


def split_projection(p):
    b, t = p.shape[:2]
    o = 0
    pr = p[..., o:o + C_RIN]
    o += C_RIN
    q = p[..., o:o + C_N].reshape(b, t, H_N, HD_N)
    o += C_N
    kvc = p[..., o:o + KV_COLS].reshape(b, t, 2, N_KVG, HD_N)
    o += KV_COLS
    kvs = p[..., o:o + KV_COLS].reshape(b, t, 2, N_KVG, HD_N)
    o += KV_COLS
    kvw = p[..., o:o + KV_COLS].reshape(b, t, 2, N_KVG, HD_N)
    o += KV_COLS
    gn = jax.nn.sigmoid(p[..., o:o + 3 * H_N]).reshape(b, t, H_N, 3)
    o += 3 * H_N
    gm = jax.nn.sigmoid(p[..., o:o + 2 * D_MODEL]).reshape(b, t, 2, D_MODEL)
    return pr, q, kvc, kvs, kvw, gn, gm


def rwkv_time_mix(p, prev_row, s0, mu, w0, w2, a0, a2, g2, k_k, k_a, r_k, ln_g, ln_b):
    b, t, _ = p.shape
    p = p.astype(jnp.float32)
    p_prev = jnp.concatenate([prev_row.astype(jnp.float32)[:, None], p[:, :-1]], axis=1)
    xm = p + (p_prev - p) * mu
    r = xm[..., :C_R]
    k = xm[..., C_R:2 * C_R]
    v = xm[..., 2 * C_R:3 * C_R]
    o = 3 * C_R
    wd = xm[..., o:o + LORA_W]
    ad = xm[..., o + LORA_W:o + LORA_W + LORA_A]
    gd = xm[..., o + LORA_W + LORA_A:]
    w_log = -jax.nn.softplus(-(w0 + jnp.tanh(wd) @ w2)) - 0.5
    decay = jnp.exp(-jnp.exp(w_log))
    a = jax.nn.sigmoid(a0 + ad @ a2)
    g = jax.nn.sigmoid(gd) @ g2
    heads = lambda z: z.reshape(b, t, H_R, HD_R)
    kk = heads(k * k_k)
    kk = kk / jnp.maximum(jnp.sqrt(jnp.sum(kk * kk, axis=-1, keepdims=True)), 1e-12)
    k = k * (1.0 + (a - 1.0) * k_a)
    r_h, k_h, v_h, w_h, a_h = heads(r), heads(k), heads(v), heads(decay), heads(a)
    b_h = kk * a_h

    def step(s, inp):
        r_t, k_t, v_t, w_t, kk_t, b_t = inp
        sa = jnp.einsum('bhij,bhj->bhi', s, -kk_t)
        s = s * w_t[:, :, None, :] + sa[..., None] * b_t[:, :, None, :] + v_t[..., None] * k_t[:, :, None, :]
        return s, jnp.einsum('bhij,bhj->bhi', s, r_t)

    tm = lambda z: jnp.moveaxis(z, 1, 0)
    s_fin, y = lax.scan(step, s0.astype(jnp.float32), (tm(r_h), tm(k_h), tm(v_h), tm(w_h), tm(kk), tm(b_h)))
    y = jnp.moveaxis(y, 0, 1)
    mean = jnp.mean(y, axis=-1, keepdims=True)
    var = jnp.mean(jnp.square(y - mean), axis=-1, keepdims=True)
    y = ((y - mean) * lax.rsqrt(var + LN_X_EPS)).reshape(b, t, C_R) * ln_g + ln_b
    bonus = jnp.sum(r_h * k_h * r_k, axis=-1, keepdims=True) * v_h
    out = (y + bonus.reshape(b, t, C_R)) * g
    return out, p[:, -1], s_fin


def compress_kv(kv, pos_emb, w1, w2):
    b, t = kv.shape[:2]
    nc = (t - L_CMP) // STRIDE + 1
    n_chunk = nc + 1
    rows = kv[:, :n_chunk * STRIDE].astype(jnp.float32).reshape(b, n_chunk, STRIDE, 2, N_KVG, HD_N)
    w1r = w1.reshape(2, 2, STRIDE, HD_N, CMP_HID)
    pos = pos_emb.reshape(2, STRIDE, HD_N)
    c = (jnp.einsum('bcpsgd,sfpde->bcfsge', rows, w1r)
         + jnp.einsum('fpd,sfpde->fse', pos, w1r)[None, None, :, :, None, :])
    hid = jax.nn.gelu(c[:, :-1, 0] + c[:, 1:, 1])
    out = jnp.einsum('bnsge,sed->bnsgd', hid, w2)
    ends = jnp.arange(nc, dtype=jnp.int32) * STRIDE + (L_CMP - 1)
    return out, ends


def cmp_sel_overlap(nc, nsb):
    s = np.arange(nc)[:, None] * STRIDE
    j = np.arange(nsb)[None, :] * L_SEL
    ov = np.clip(np.minimum(s + L_CMP, j + L_SEL) - np.maximum(s, j), 0, None) / L_CMP
    return jnp.asarray(ov, jnp.float32)


def nsa_attend(q, qpos, gates, kv_cmp, cmp_end, ov_mat, sel_gather, kv_win, win_pos, rel_bias):
    b, nq = q.shape[:2]
    nsb = ov_mat.shape[1]
    n_top = min(N_TOP, nsb)
    qg = q.astype(jnp.float32).reshape(b, nq, N_KVG, HPG, HD_N) * (HD_N ** -0.5)
    kc = kv_cmp[:, :, 0].astype(jnp.float32)
    vc = kv_cmp[:, :, 1].astype(jnp.float32)
    dc = qpos[:, None] - cmp_end[None, :]
    bc = rel_bias[t5_bucket(dc)].reshape(nq, -1, N_KVG, HPG).transpose(0, 2, 3, 1)
    valid_c = (dc >= 0)[:, None, None, :]
    lc = jnp.einsum('bqghd,bngd->bqghn', qg, kc) + bc
    pc = jax.nn.softmax(jnp.where(valid_c, lc, NEG_INF), axis=-1) * valid_c
    o_cmp = jnp.einsum('bqghn,bngd->bqghd', pc, vc)
    imp = jnp.einsum('bqgn,nj->bqgj', pc.sum(axis=3), ov_mat)
    blk = jnp.arange(nsb, dtype=jnp.int32)[None, :]
    cur = (qpos // L_SEL)[:, None]
    forced = ((blk == 0) | (blk == cur) | (blk == cur - 1))[None, :, None, :]
    future = (blk > cur)[None, :, None, :]
    score = jnp.where(forced, FORCE, jnp.where(future, -FORCE, imp))
    _, idx = lax.top_k(score, n_top)
    kv_sel = sel_gather(idx)
    ks = kv_sel[..., 0, :].astype(jnp.float32)
    vs = kv_sel[..., 1, :].astype(jnp.float32)
    ds = qpos[None, :, None, None, None] - (idx[..., None] * L_SEL + jnp.arange(L_SEL, dtype=jnp.int32))
    gsel = jnp.arange(N_KVG)[None, None, :, None, None]
    bs = rel_bias.reshape(N_BUCKETS, N_KVG, HPG)[t5_bucket(ds), gsel].transpose(0, 1, 2, 5, 3, 4)
    valid_s = (ds >= 0)[:, :, :, None]
    ls = jnp.einsum('bqghd,bqgnld->bqghnl', qg, ks) + bs
    ls = jnp.where(valid_s, ls, NEG_INF).reshape(b, nq, N_KVG, HPG, n_top * L_SEL)
    ps = jax.nn.softmax(ls, axis=-1).reshape(b, nq, N_KVG, HPG, n_top, L_SEL)
    o_sel = jnp.einsum('bqghnl,bqgnld->bqghd', ps, vs)
    dw = qpos[:, None] - win_pos[None, :]
    valid_w = ((dw >= 0) & (dw <= WINDOW) & (win_pos >= 0)[None, :])[:, None, None, :]
    bw = rel_bias[t5_bucket(dw)].reshape(nq, -1, N_KVG, HPG).transpose(0, 2, 3, 1)
    lw = jnp.einsum('bqghd,bkgd->bqghk', qg, kv_win[:, :, 0].astype(jnp.float32)) + bw
    pw = jax.nn.softmax(jnp.where(valid_w, lw, NEG_INF), axis=-1)
    o_win = jnp.einsum('bqghk,bkgd->bqghd', pw, kv_win[:, :, 1].astype(jnp.float32))
    g = gates.astype(jnp.float32).reshape(b, nq, N_KVG, HPG, 3)
    o = g[..., 0:1] * o_cmp + g[..., 1:2] * o_sel + g[..., 2:3] * o_win
    return o.reshape(b, nq, C_N)


def nsa_prompt(q, gates, kvc, kvs, kvw, pos_emb, cw1, cw2, rel_bias):
    b, t = q.shape[:2]
    kv_cmp, ends = compress_kv(kvc, pos_emb, cw1, cw2)
    nsb = t // L_SEL
    ov = cmp_sel_overlap(kv_cmp.shape[1], nsb)
    sel_blocks = kvs.reshape(b, nsb, L_SEL, 2, N_KVG, HD_N)
    bidx = jnp.arange(b)[:, None, None, None]
    gidx = jnp.arange(N_KVG)[None, None, :, None]

    def sel_gather(idx):
        return sel_blocks[bidx, idx, :, :, gidx]

    kvw_pad = jnp.pad(kvw, ((0, 0), (WINDOW, 0), (0, 0), (0, 0), (0, 0)))
    nqb = t // Q_BLOCK
    qb = q.reshape(b, nqb, Q_BLOCK, H_N, HD_N).transpose(1, 0, 2, 3, 4)
    gb = gates.reshape(b, nqb, Q_BLOCK, H_N, 3).transpose(1, 0, 2, 3, 4)

    def block(args):
        i, q_i, g_i = args
        q0 = i * Q_BLOCK
        qpos = q0 + jnp.arange(Q_BLOCK, dtype=jnp.int32)
        kv_w = lax.dynamic_slice_in_dim(kvw_pad, q0, WINDOW + Q_BLOCK, axis=1)
        wpos = q0 - WINDOW + jnp.arange(WINDOW + Q_BLOCK, dtype=jnp.int32)
        return nsa_attend(q_i, qpos, g_i, kv_cmp, ends, ov, sel_gather, kv_w, wpos, rel_bias)

    o = lax.map(block, (jnp.arange(nqb, dtype=jnp.int32), qb, gb))
    return o.transpose(1, 0, 2, 3).reshape(b, t, C_N)


def nsa_sample(q, gates, kvc_new, kvs_new, kvw_new, cache_cmp, cache_sel, win_buf, page_table,
               pos_emb, cw1, cw2, rel_bias):
    b, s = q.shape[:2]
    n_pages = page_table.shape[1]
    past = n_pages * PAGE_SIZE
    bpp = PAGE_SIZE // L_SEL
    past_cmp = cache_cmp[page_table].reshape(b, past, 2, N_KVG, HD_N)
    kv_all = jnp.concatenate([past_cmp.astype(jnp.float32), kvc_new.astype(jnp.float32)], axis=1)
    kv_cmp, ends = compress_kv(kv_all, pos_emb, cw1, cw2)
    t = past + s
    nsb = -(-t // L_SEL)
    npb = past // L_SEL
    nnb = nsb - npb
    ov = cmp_sel_overlap(kv_cmp.shape[1], nsb)
    pool_blocks = cache_sel.reshape(-1, L_SEL, 2, N_KVG, HD_N)
    new_blocks = jnp.pad(kvs_new, ((0, 0), (0, nnb * L_SEL - s), (0, 0), (0, 0), (0, 0)))
    new_blocks = new_blocks.reshape(b, nnb, L_SEL, 2, N_KVG, HD_N)
    bidx = jnp.arange(b)[:, None, None, None]
    gidx = jnp.arange(N_KVG)[None, None, :, None]

    def sel_gather(idx):
        in_past = idx < npb
        logical_page = jnp.minimum(idx // bpp, n_pages - 1)
        phys = page_table[bidx, logical_page] * bpp + idx % bpp
        from_past = pool_blocks[phys, :, :, gidx].astype(jnp.float32)
        from_new = new_blocks[bidx, jnp.clip(idx - npb, 0, nnb - 1), :, :, gidx].astype(jnp.float32)
        return jnp.where(in_past[..., None, None, None], from_past, from_new)

    wbuf = win_buf.shape[1]
    kv_w = jnp.concatenate([win_buf.astype(jnp.float32), kvw_new.astype(jnp.float32)], axis=1)
    wpos = past - wbuf + jnp.arange(wbuf + s, dtype=jnp.int32)
    qpos = past + jnp.arange(s, dtype=jnp.int32)
    o = nsa_attend(q, qpos, gates, kv_cmp, ends, ov, sel_gather, kv_w, wpos, rel_bias)
    return o, kv_w[:, -wbuf:]


def merge_branches(o_r, o_n, gm, w_o_rwkv, w_o_nsa, w_out):
    y = gm[..., 0, :] * (o_r @ w_o_rwkv) + gm[..., 1, :] * (o_n @ w_o_nsa)
    return y @ w_out


def moe_dispatch(x, eid, ew, w1, w3, w2):
    ntok, d = x.shape
    m = ntok * TOP_K
    blk = min(MOE_BLOCK, m)
    n_blocks = -(-m // blk) + N_EXP
    rows = n_blocks * blk
    e = eid.reshape(m)
    tok = jnp.repeat(jnp.arange(ntok, dtype=jnp.int32), TOP_K)
    wgt = ew.reshape(m)
    order = jnp.argsort(e)
    e_s, tok_s, w_s = e[order], tok[order], wgt[order]
    counts = jax.ops.segment_sum(jnp.ones((m,), jnp.int32), e, num_segments=N_EXP)
    start = jnp.cumsum(counts) - counts
    padded = (counts + blk - 1) // blk * blk
    pend = jnp.cumsum(padded)
    pstart = pend - padded
    dest = pstart[e_s] + jnp.arange(m, dtype=jnp.int32) - start[e_s]
    row_tok = jnp.full((rows,), ntok, jnp.int32).at[dest].set(tok_s)
    row_w = jnp.zeros((rows,), jnp.float32).at[dest].set(w_s.astype(jnp.float32))
    blk_exp = jnp.minimum(jnp.searchsorted(pend, jnp.arange(n_blocks, dtype=jnp.int32) * blk, side='right'),
                          N_EXP - 1).astype(jnp.int32)
    x_pad = jnp.concatenate([x.astype(jnp.float32), jnp.zeros((1, d), jnp.float32)], axis=0)

    def run(args):
        rt, rwt, ex = args
        xb = x_pad[rt]
        hid = jax.nn.silu(xb @ w1[ex]) * (xb @ w3[ex])
        return (hid @ w2[ex]) * rwt[:, None]

    y = lax.map(run, (row_tok.reshape(n_blocks, blk), row_w.reshape(n_blocks, blk), blk_exp))
    return jnp.zeros((ntok + 1, d), jnp.float32).at[row_tok].add(y.reshape(rows, d))[:ntok]


def hier_moe(h, wg, bg, we, be, w1, w3, w2):
    b, t, d = h.shape
    hf = h.reshape(b * t, d)
    pg = jax.nn.softmax((hf @ wg + bg).astype(jnp.float32), axis=-1)
    g_w, g_i = lax.top_k(pg, 1)
    le = (hf @ we + be).astype(jnp.float32).reshape(b * t, N_EGROUPS, EXP_PER_GROUP)
    le = jnp.take_along_axis(le, g_i[:, :, None], axis=1)[:, 0]
    e_w, e_i = lax.top_k(jax.nn.softmax(le, axis=-1), TOP_K)
    e_w = e_w / jnp.sum(e_w, axis=-1, keepdims=True) * g_w
    eid = g_i * EXP_PER_GROUP + e_i
    return moe_dispatch(hf, eid, e_w, w1, w3, w2).reshape(b, t, d)


def trunk_layer(x, c, mix_fn, g1, g2, w_ada, b_ada, moew):
    mod = (c @ w_ada + b_ada).reshape(c.shape[0], 6, D_MODEL)[:, :, None, :]
    h = rms_norm(x, g1) * (1.0 + mod[:, 1]) + mod[:, 0]
    mix, states = mix_fn(h)
    x = x + mod[:, 2] * mix
    h2 = rms_norm(x, g2) * (1.0 + mod[:, 4]) + mod[:, 3]
    x = x + mod[:, 5] * hier_moe(h2, *moew)
    return x, states


def setup_inputs(seed: int = 0) -> dict:
    key = jax.random.key(seed)
    ks = iter(jax.random.split(key, 64))
    nrm = lambda shape, scale: jax.random.normal(next(ks), shape, jnp.float32) * scale
    n_pages = PAST_LEN // PAGE_SIZE
    n_used = DEC_BATCH * n_pages
    n_pool = n_used + -(-n_used // 4)
    wbuf = min(WINDOW, PAST_LEN)
    x_prompt = nrm((BATCH, SEQ, D_MODEL), 1.0)
    x_sample = nrm((DEC_BATCH, DEC_SEQ, D_MODEL), 1.0)
    c_prompt = nrm((BATCH, D_MODEL), 1.0)
    c_sample = nrm((DEC_BATCH, D_MODEL), 1.0)
    cache_cmp_kv = nrm((DEPTH, n_pool, PAGE_SIZE, 2, N_KVG, HD_N), 1.0)
    cache_sel_kv = nrm((DEPTH, n_pool, PAGE_SIZE, 2, N_KVG, HD_N), 1.0)
    state_win_kv = nrm((DEPTH, DEC_BATCH, wbuf, 2, N_KVG, HD_N), 1.0)
    state_rwkv_shift = nrm((DEPTH, DEC_BATCH, C_RIN), 1.0)
    state_rwkv_wkv = nrm((DEPTH, DEC_BATCH, H_R, HD_R, HD_R), 0.3)
    page_table = jax.random.permutation(next(ks), n_pool)[:n_used].astype(jnp.int32).reshape(DEC_BATCH, n_pages)
    return {
        'x_prompt': x_prompt, 'x_sample': x_sample, 'c_prompt': c_prompt, 'c_sample': c_sample,
        'cache_cmp_kv': cache_cmp_kv, 'cache_sel_kv': cache_sel_kv, 'state_win_kv': state_win_kv,
        'state_rwkv_shift': state_rwkv_shift, 'state_rwkv_wkv': state_rwkv_wkv, 'page_table': page_table,
        'rel_bias': nrm((N_BUCKETS, H_N), 0.5),
        'norm_f': 1.0 + nrm((D_MODEL,), 0.05),
        'norm1': 1.0 + nrm((DEPTH, D_MODEL), 0.05),
        'norm2': 1.0 + nrm((DEPTH, D_MODEL), 0.05),
        'w_ada': nrm((DEPTH, D_MODEL, 6 * D_MODEL), 0.5 * D_MODEL ** -0.5),
        'b_ada': nrm((DEPTH, 6 * D_MODEL), 0.02),
        'w_in': nrm((DEPTH, D_MODEL, C_IN), D_MODEL ** -0.5),
        'rwkv_mu': jax.random.uniform(next(ks), (DEPTH, C_RIN), jnp.float32),
        'rwkv_w0': jax.random.uniform(next(ks), (DEPTH, C_R), jnp.float32, -3.0, 1.0),
        'rwkv_w2': nrm((DEPTH, LORA_W, C_R), 0.1),
        'rwkv_a0': nrm((DEPTH, C_R), 0.5),
        'rwkv_a2': nrm((DEPTH, LORA_A, C_R), 0.1),
        'rwkv_g2': nrm((DEPTH, LORA_G, C_R), LORA_G ** -0.5),
        'rwkv_kk': 0.85 + nrm((DEPTH, C_R), 0.05),
        'rwkv_ka': 1.0 + nrm((DEPTH, C_R), 0.05),
        'rwkv_rk': nrm((DEPTH, H_R, HD_R), 0.1),
        'rwkv_ln_g': 1.0 + nrm((DEPTH, C_R), 0.05),
        'rwkv_ln_b': nrm((DEPTH, C_R), 0.02),
        'cmp_pos': nrm((DEPTH, L_CMP, HD_N), 0.1),
        'cmp_w1': nrm((DEPTH, 2, L_CMP * HD_N, CMP_HID), (L_CMP * HD_N) ** -0.5),
        'cmp_w2': nrm((DEPTH, 2, CMP_HID, HD_N), CMP_HID ** -0.5),
        'w_o_rwkv': nrm((DEPTH, C_R, D_MODEL), C_R ** -0.5),
        'w_o_nsa': nrm((DEPTH, C_N, D_MODEL), C_N ** -0.5),
        'w_out': nrm((DEPTH, D_MODEL, D_MODEL), D_MODEL ** -0.5),
        'router_wg': nrm((DEPTH, D_MODEL, N_EGROUPS), D_MODEL ** -0.5),
        'router_bg': nrm((DEPTH, N_EGROUPS), 0.01),
        'router_we': nrm((DEPTH, D_MODEL, N_EXP), D_MODEL ** -0.5),
        'router_be': nrm((DEPTH, N_EXP), 0.01),
        'exp_w1': nrm((DEPTH, N_EXP, D_MODEL, D_EXP), D_MODEL ** -0.5),
        'exp_w3': nrm((DEPTH, N_EXP, D_MODEL, D_EXP), D_MODEL ** -0.5),
        'exp_w2': nrm((DEPTH, N_EXP, D_EXP, D_MODEL), D_EXP ** -0.5),
    }


def reference(x_prompt, x_sample, c_prompt, c_sample, cache_cmp_kv, cache_sel_kv, state_win_kv,
              state_rwkv_shift, state_rwkv_wkv, page_table, rel_bias, norm_f, norm1, norm2, w_ada, b_ada,
              w_in, rwkv_mu, rwkv_w0, rwkv_w2, rwkv_a0, rwkv_a2, rwkv_g2, rwkv_kk, rwkv_ka, rwkv_rk,
              rwkv_ln_g, rwkv_ln_b, cmp_pos, cmp_w1, cmp_w2, w_o_rwkv, w_o_nsa, w_out,
              router_wg, router_bg, router_we, router_be, exp_w1, exp_w3, exp_w2):
    xp, xs = x_prompt, x_sample
    st_p, st_s = [], []
    for l in range(DEPTH):
        rw = (rwkv_mu[l], rwkv_w0[l], rwkv_w2[l], rwkv_a0[l], rwkv_a2[l], rwkv_g2[l], rwkv_kk[l],
              rwkv_ka[l], rwkv_rk[l], rwkv_ln_g[l], rwkv_ln_b[l])
        cmpw = (cmp_pos[l], cmp_w1[l], cmp_w2[l])
        outw = (w_o_rwkv[l], w_o_nsa[l], w_out[l])
        moew = (router_wg[l], router_bg[l], router_we[l], router_be[l], exp_w1[l], exp_w3[l], exp_w2[l])
        w_in_l = w_in[l]

        def mix_prompt(h):
            pr, q, kvc, kvs, kvw, gn, gm = split_projection(h @ w_in_l)
            b = h.shape[0]
            o_r, shift, wkv = rwkv_time_mix(pr, jnp.zeros((b, C_RIN), jnp.float32),
                                            jnp.zeros((b, H_R, HD_R, HD_R), jnp.float32), *rw)
            o_n = nsa_prompt(q, gn, kvc, kvs, kvw, *cmpw, rel_bias)
            win = kvw[:, -min(WINDOW, kvw.shape[1]):]
            return merge_branches(o_r, o_n, gm, *outw), (kvc, kvs, win, shift, wkv)

        def mix_sample(h):
            pr, q, kvc, kvs, kvw, gn, gm = split_projection(h @ w_in_l)
            o_r, shift, wkv = rwkv_time_mix(pr, state_rwkv_shift[l], state_rwkv_wkv[l], *rw)
            o_n, win = nsa_sample(q, gn, kvc, kvs, kvw, cache_cmp_kv[l], cache_sel_kv[l], state_win_kv[l],
                                  page_table, *cmpw, rel_bias)
            return merge_branches(o_r, o_n, gm, *outw), (kvc, kvs, win, shift, wkv)

        xp, sp = trunk_layer(xp, c_prompt, mix_prompt, norm1[l], norm2[l], w_ada[l], b_ada[l], moew)
        xs, ss = trunk_layer(xs, c_sample, mix_sample, norm1[l], norm2[l], w_ada[l], b_ada[l], moew)
        st_p.append(sp)
        st_s.append(ss)
    y_prompt = rms_norm(xp, norm_f)
    y_sample = rms_norm(xs, norm_f)
    stack = lambda sts, i: jnp.stack([z[i] for z in sts], axis=0)
    return (y_prompt, y_sample,
            stack(st_p, 0), stack(st_s, 0),
            stack(st_p, 1), stack(st_s, 1),
            stack(st_p, 2), stack(st_s, 2),
            stack(st_p, 3), stack(st_s, 3),
            stack(st_p, 4), stack(st_s, 4))
```

```python
import functools
import math

import numpy as np
import jax
import jax.numpy as jnp
from jax import lax
from jax.experimental import pallas as pl
from jax.experimental.pallas import tpu as pltpu

D_MODEL = 2048
PAGE_SIZE = 128
H_R, HD_R = 16, 64
C_R = H_R * HD_R
LORA_W, LORA_A, LORA_G = 64, 64, 128
C_RIN = 3 * C_R + LORA_W + LORA_A + LORA_G
LN_X_EPS = 64e-5
H_N, HD_N, N_KVG = 16, 64, 4
HPG = H_N // N_KVG
C_N = H_N * HD_N
KV_COLS = 2 * N_KVG * HD_N
L_CMP, STRIDE, CMP_HID = 32, 16, 64
L_SEL, N_TOP, WINDOW = 64, 16, 512
N_BUCKETS, MAX_DIST = 32, 128
N_EGROUPS, EXP_PER_GROUP = 4, 8
N_EXP = N_EGROUPS * EXP_PER_GROUP
TOP_K, D_EXP = 2, 512
RMS_EPS = 1e-6
NEG_INF = -1e30
FORCE = 1e9

BF = jnp.bfloat16
F32 = jnp.float32
I32 = jnp.int32

VMEM_LIMIT_BYTES = 56 * 1024 * 1024
LANES = 128
RW_CHUNK = 64
ATT_TILE = 256
GN_GROUP_COLS = 128
NP_Q, NP_KVC, NP_KVS, NP_KVW, NP_GN = 0, C_N, C_N + KV_COLS, C_N + 2 * KV_COLS, C_N + 3 * KV_COLS
NP_COLS = NP_GN + N_KVG * GN_GROUP_COLS


def _cparams(*sem):
    return pltpu.CompilerParams(dimension_semantics=sem, vmem_limit_bytes=VMEM_LIMIT_BYTES)


def _dot(a, b):
    return jnp.dot(a.astype(BF), b.astype(BF), preferred_element_type=F32)


def _dot_nt(a, b):
    return lax.dot_general(a.astype(BF), b.astype(BF), (((1,), (1,)), ((), ())), preferred_element_type=F32)


def _dot_tn(a, b):
    return lax.dot_general(a.astype(BF), b.astype(BF), (((0,), (0,)), ((), ())), preferred_element_type=F32)


def _softplus(x):
    return jnp.maximum(x, 0.0) + jnp.log1p(jnp.exp(-jnp.abs(x)))


def _sigmoid(x):
    return 1.0 / (1.0 + jnp.exp(-x))


def _gelu_tanh(x):
    return 0.5 * x * (1.0 + jnp.tanh(math.sqrt(2.0 / math.pi) * (x + 0.044715 * x * x * x)))


def _t5_bucket(dist):
    n = jnp.maximum(dist, 0)
    max_exact = N_BUCKETS // 2
    nf = jnp.maximum(n, 1).astype(F32)
    large = max_exact + (jnp.log(nf / max_exact) / math.log(MAX_DIST / max_exact)
                         * (N_BUCKETS - max_exact)).astype(I32)
    large = jnp.minimum(large, N_BUCKETS - 1)
    return jnp.where(n < max_exact, n, large)


def _bias_rows(dist, rbt):
    bucket = _t5_bucket(dist)
    out = jnp.zeros((rbt.shape[0], dist.shape[1]), F32)
    for b in range(N_BUCKETS):
        out = jnp.where(bucket == b, rbt[:, b:b + 1], out)
    return out


def _ada_kernel(c_ref, w_ref, b_ref, o_ref):
    o_ref[...] = _dot(c_ref[...], w_ref[...]) + b_ref[...]


def _ada(c, w_ada, b_ada):
    r = c.shape[0]
    n = w_ada.shape[1]
    tn = 1024
    return pl.pallas_call(
        _ada_kernel,
        grid=(n // tn,),
        in_specs=[pl.BlockSpec((r, D_MODEL), lambda j: (0, 0)),
                  pl.BlockSpec((D_MODEL, tn), lambda j: (0, j)),
                  pl.BlockSpec((1, tn), lambda j: (0, j))],
        out_specs=pl.BlockSpec((r, tn), lambda j: (0, j)),
        out_shape=jax.ShapeDtypeStruct((r, n), F32),
        compiler_params=_cparams("arbitrary"),
        name="ada_mod",
    )(c, w_ada, b_ada.reshape(1, n))


def _rms(x, g):
    return x * lax.rsqrt(jnp.mean(x * x, axis=-1, keepdims=True) + RMS_EPS) * g


def _norm_mod_kernel(x_ref, g_ref, sh_ref, sc_ref, o_ref):
    o_ref[...] = (_rms(x_ref[...], g_ref[...]) * (1.0 + sc_ref[...]) + sh_ref[...]).astype(o_ref.dtype)


def _row_specs(m, tm, rpb):
    del m
    return (lambda tn: pl.BlockSpec((tm, tn), lambda i, j: (i, j)),
            lambda r, tn: pl.BlockSpec((None, r, tn), lambda i, j: ((i * tm) // rpb, 0, j)))


def _norm_mod(x, g, shift, scale, tm, rpb):
    m = x.shape[0]
    r = shift.shape[1]
    rows, mods = _row_specs(m, tm, rpb)
    return pl.pallas_call(
        _norm_mod_kernel,
        grid=(m // tm, 1),
        in_specs=[rows(D_MODEL), pl.BlockSpec((1, D_MODEL), lambda i, j: (0, 0)), mods(r, D_MODEL), mods(r, D_MODEL)],
        out_specs=rows(D_MODEL),
        out_shape=jax.ShapeDtypeStruct((m, D_MODEL), BF),
        compiler_params=_cparams("arbitrary", "arbitrary"),
        name="norm_mod",
    )(x, g.reshape(1, D_MODEL), shift, scale)


def _mm_kernel(a_ref, w_ref, o_ref):
    o_ref[...] = jnp.dot(a_ref[...], w_ref[...], preferred_element_type=F32).astype(o_ref.dtype)


def _matmul(a, w, tm, tn, out_dtype=F32):
    m, k = a.shape
    n = w.shape[1]
    return pl.pallas_call(
        _mm_kernel,
        grid=(m // tm, n // tn),
        in_specs=[pl.BlockSpec((tm, k), lambda i, j: (i, 0)), pl.BlockSpec((k, tn), lambda i, j: (0, j))],
        out_specs=pl.BlockSpec((tm, tn), lambda i, j: (i, j)),
        out_shape=jax.ShapeDtypeStruct((m, n), out_dtype),
        compiler_params=_cparams("arbitrary", "arbitrary"),
        name="matmul",
    )(a, w)


def _merge_kernel(or_ref, on_ref, wr_ref, wn_ref, g0_ref, g1_ref, o_ref):
    yr = jnp.dot(or_ref[...], wr_ref[...], preferred_element_type=F32)
    yn = jnp.dot(on_ref[...], wn_ref[...], preferred_element_type=F32)
    o_ref[...] = (_sigmoid(g0_ref[...]) * yr + _sigmoid(g1_ref[...]) * yn).astype(o_ref.dtype)


def _merge(o_r, o_n, w_r, w_n, p_g, tm):
    m = o_r.shape[0]
    tn = 1024
    nb = D_MODEL // tn
    return pl.pallas_call(
        _merge_kernel,
        grid=(m // tm, nb),
        in_specs=[pl.BlockSpec((tm, C_R), lambda i, j: (i, 0)), pl.BlockSpec((tm, C_N), lambda i, j: (i, 0)),
                  pl.BlockSpec((C_R, tn), lambda i, j: (0, j)), pl.BlockSpec((C_N, tn), lambda i, j: (0, j)),
                  pl.BlockSpec((tm, tn), lambda i, j: (i, j)), pl.BlockSpec((tm, tn), lambda i, j: (i, j + nb))],
        out_specs=pl.BlockSpec((tm, tn), lambda i, j: (i, j)),
        out_shape=jax.ShapeDtypeStruct((m, D_MODEL), BF),
        compiler_params=_cparams("arbitrary", "arbitrary"),
        name="merge_branches",
    )(o_r, o_n, w_r, w_n, p_g, p_g)


def _proj_res_kernel(y_ref, w_ref, x_ref, g_ref, o_ref):
    o_ref[...] = x_ref[...] + g_ref[...] * jnp.dot(y_ref[...], w_ref[...], preferred_element_type=F32)


def _proj_residual(y, w, x, gate, tm, rpb):
    m = y.shape[0]
    tn = 1024
    r = gate.shape[1]
    rows, mods = _row_specs(m, tm, rpb)
    return pl.pallas_call(
        _proj_res_kernel,
        grid=(m // tm, D_MODEL // tn),
        in_specs=[pl.BlockSpec((tm, D_MODEL), lambda i, j: (i, 0)), pl.BlockSpec((D_MODEL, tn), lambda i, j: (0, j)),
                  rows(tn), mods(r, tn)],
        out_specs=rows(tn),
        out_shape=jax.ShapeDtypeStruct((m, D_MODEL), F32),
        compiler_params=_cparams("arbitrary", "arbitrary"),
        name="out_proj_residual",
    )(y, w, x, gate)


def _rwkv_features(p, p_prev, mu, w0, w2, a0, a2, g2, k_k, k_a):
    xm = p + (p_prev - p) * mu
    r = xm[:, :C_R]
    k = xm[:, C_R:2 * C_R]
    v = xm[:, 2 * C_R:3 * C_R]
    o = 3 * C_R
    wd = xm[:, o:o + LORA_W]
    ad = xm[:, o + LORA_W:o + LORA_W + LORA_A]
    gd = xm[:, o + LORA_W + LORA_A:]
    w_log = -_softplus(-(w0 + _dot(jnp.tanh(wd), w2))) - 0.5
    lw = -jnp.exp(w_log)
    a = _sigmoid(a0 + _dot(ad, a2))
    g = _dot(_sigmoid(gd), g2)
    kk = k * k_k
    k = k * (1.0 + (a - 1.0) * k_a)
    return r, k, v, lw, a, g, kk


def _rwkv_head_out(y, r_h, k_h, v_h, g_h, rk_h, lng_h, lnb_h):
    mean = jnp.mean(y, axis=-1, keepdims=True)
    yc = y - mean
    var = jnp.mean(yc * yc, axis=-1, keepdims=True)
    yn = yc * lax.rsqrt(var + LN_X_EPS) * lng_h + lnb_h
    bonus = jnp.sum(r_h * k_h * rk_h, axis=-1, keepdims=True) * v_h
    return (yn + bonus) * g_h


def _rwkv_chunk_kernel(pr_ref, mu_ref, w0_ref, w2_ref, a0_ref, a2_ref, g2_ref, kk_ref, ka_ref, rk_ref,
                       lng_ref, lnb_ref, o_ref, shift_ref, state_ref):
    c = pl.program_id(1)
    C = RW_CHUNK

    @pl.when(c == 0)
    def _():
        shift_ref[...] = jnp.zeros_like(shift_ref)
        state_ref[...] = jnp.zeros_like(state_ref)

    p = pr_ref[...]
    row = lax.broadcasted_iota(I32, (C, 1), 0)
    p_prev = jnp.where(row == 0, shift_ref[...], pltpu.roll(p, 1, axis=0))
    shift_ref[...] = p[C - 1:C, :]
    r, k, v, lw, a, g, kk_all = _rwkv_features(p, p_prev, mu_ref[...], w0_ref[...], w2_ref[...], a0_ref[...],
                                               a2_ref[...], g2_ref[...], kk_ref[...], ka_ref[...])
    cl = lw
    s = 1
    while s < C:
        cl = cl + jnp.where(row >= s, pltpu.roll(cl, s, axis=0), 0.0)
        s *= 2
    ti = lax.broadcasted_iota(I32, (C, C), 0)
    si = lax.broadcasted_iota(I32, (C, C), 1)
    strict = ti > si
    incl = ti >= si
    eye = (ti == si).astype(F32)
    for h in range(H_R):
        sl = slice(h * HD_R, (h + 1) * HD_R)
        r_h, k_h, v_h, a_h = r[:, sl], k[:, sl], v[:, sl], a[:, sl]
        kk_h = kk_all[:, sl]
        nrm = jnp.sqrt(jnp.sum(kk_h * kk_h, axis=-1, keepdims=True))
        kk_h = kk_h / jnp.maximum(nrm, 1e-12)
        b_h = kk_h * a_h
        cl_h = cl[:, sl]
        cl_end = cl_h[C - 1:C, :]
        e_neg = jnp.exp(-cl_h)
        kkt = kk_h * jnp.exp(cl_h - lw[:, sl])
        kh = k_h * e_neg
        bh = b_h * e_neg
        rt = r_h * jnp.exp(cl_h)
        e_end = jnp.exp(cl_end - cl_h)
        kbar = k_h * e_end
        bbar = b_h * e_end
        lkk = jnp.where(strict, _dot_nt(kkt, kh), 0.0)
        lkb = jnp.where(strict, _dot_nt(kkt, bh), 0.0)
        grk = jnp.where(incl, _dot_nt(rt, kh), 0.0)
        grb = jnp.where(incl, _dot_nt(rt, bh), 0.0)
        n = -lkb
        tinv = eye + n
        m = 2
        while m < C:
            n = _dot(n, n)
            tinv = tinv + _dot(tinv, n)
            m *= 2
        s0 = state_ref[h]
        u = _dot(tinv, _dot_nt(kkt, s0) + _dot(lkk, v_h))
        y = _dot_nt(rt, s0) + _dot(grk, v_h) - _dot(grb, u)
        state_ref[h] = s0 * jnp.exp(cl_end) + _dot_tn(v_h, kbar) - _dot_tn(u, bbar)
        o_ref[:, sl] = _rwkv_head_out(y, r_h, k_h, v_h, g[:, sl], rk_ref[:, sl], lng_ref[:, sl],
                                      lnb_ref[:, sl]).astype(o_ref.dtype)


def _rwkv_weights(mu, w0, w2, a0, a2, g2, k_k, k_a, r_k, ln_g, ln_b):
    row = lambda z: z.reshape(1, -1).astype(F32)
    return (row(mu), row(w0), w2.astype(BF), row(a0), a2.astype(BF), g2.astype(BF), row(k_k), row(k_a), row(r_k),
            row(ln_g), row(ln_b))


_RWKV_W_SHAPES = ((1, C_RIN), (1, C_R), (LORA_W, C_R), (1, C_R), (LORA_A, C_R), (LORA_G, C_R), (1, C_R), (1, C_R),
                  (1, C_R), (1, C_R), (1, C_R))


def _rwkv_prompt(pr, rw):
    b, t, _ = pr.shape
    C = RW_CHUNK
    full = lambda shp: pl.BlockSpec(shp, lambda i, j: (0,) * len(shp))
    return pl.pallas_call(
        _rwkv_chunk_kernel,
        grid=(b, t // C),
        in_specs=[pl.BlockSpec((None, C, C_RIN), lambda i, j: (i, j, 0))] + [full(s) for s in _RWKV_W_SHAPES],
        out_specs=[pl.BlockSpec((None, C, C_R), lambda i, j: (i, j, 0)),
                   pl.BlockSpec((None, 1, C_RIN), lambda i, j: (i, 0, 0)),
                   pl.BlockSpec((None, H_R, HD_R, HD_R), lambda i, j: (i, 0, 0, 0))],
        out_shape=[jax.ShapeDtypeStruct((b, t, C_R), BF),
                   jax.ShapeDtypeStruct((b, 1, C_RIN), F32),
                   jax.ShapeDtypeStruct((b, H_R, HD_R, HD_R), F32)],
        compiler_params=_cparams("arbitrary", "arbitrary"),
        name="rwkv_chunk",
    )(pr, *rw)


def _rwkv_step_kernel(pr_ref, prev_ref, s0_ref, mu_ref, w0_ref, w2_ref, a0_ref, a2_ref, g2_ref, kk_ref, ka_ref,
                      rk_ref, lng_ref, lnb_ref, o_ref, state_ref):
    nb = pr_ref.shape[0]
    r, k, v, lw, a, g, kk_all = _rwkv_features(pr_ref[...], prev_ref[...], mu_ref[...], w0_ref[...], w2_ref[...],
                                               a0_ref[...], a2_ref[...], g2_ref[...], kk_ref[...], ka_ref[...])
    decay = jnp.exp(lw)
    ii = lax.broadcasted_iota(I32, (HD_R, HD_R), 0)
    jj = lax.broadcasted_iota(I32, (HD_R, HD_R), 1)
    eye = ii == jj
    col = lambda z: jnp.sum(jnp.where(eye, z, 0.0), axis=1, keepdims=True)
    for bi in range(nb):
        for h in range(H_R):
            sl = slice(h * HD_R, (h + 1) * HD_R)
            rows = lambda z: z[bi:bi + 1, sl]
            r_h, k_h, v_h, a_h, w_h = rows(r), rows(k), rows(v), rows(a), rows(decay)
            kk_h = rows(kk_all)
            kk_h = kk_h / jnp.maximum(jnp.sqrt(jnp.sum(kk_h * kk_h, axis=-1, keepdims=True)), 1e-12)
            b_h = kk_h * a_h
            s0 = s0_ref[bi, h]
            sa = jnp.sum(s0 * (-kk_h), axis=1, keepdims=True)
            s1 = s0 * w_h + sa * b_h + col(v_h) * k_h
            state_ref[bi, h] = s1
            y_col = jnp.sum(s1 * r_h, axis=1, keepdims=True)
            y = jnp.sum(jnp.where(eye, y_col, 0.0), axis=0, keepdims=True)
            o_ref[bi:bi + 1, sl] = _rwkv_head_out(y, r_h, k_h, v_h, rows(g), rk_ref[:, sl], lng_ref[:, sl],
                                                  lnb_ref[:, sl]).astype(o_ref.dtype)


def _rwkv_step(pr, prev, s0, rw):
    b = pr.shape[0]
    return pl.pallas_call(
        _rwkv_step_kernel,
        out_shape=[jax.ShapeDtypeStruct((b, C_R), BF), jax.ShapeDtypeStruct((b, H_R, HD_R, HD_R), F32)],
        compiler_params=pltpu.CompilerParams(vmem_limit_bytes=VMEM_LIMIT_BYTES),
        name="rwkv_step",
    )(pr, prev, s0, *rw)


def _cmp_partial_kernel(*refs):
    x_refs, w_ref, o_ref = refs[:-2], refs[-2], refs[-1]
    x = x_refs[0][...] if len(x_refs) == 1 else jnp.concatenate([r[...] for r in x_refs], axis=0)
    half = N_KVG * HD_N
    for s in range(2):
        acc = jnp.zeros((x.shape[0], N_KVG * 2 * CMP_HID), F32)
        for p in range(STRIDE):
            o = p * KV_COLS + s * half
            acc = acc + _dot(x[:, o:o + half], w_ref[p, s])
        o_ref[:, s * N_KVG * 2 * CMP_HID:(s + 1) * N_KVG * 2 * CMP_HID] = acc


def _cmp_partial_paged_kernel(pt_ref, *refs):
    del pt_ref
    _cmp_partial_kernel(*refs)


def _cmp_weights(cmp_w1):
    w1r = cmp_w1.reshape(2, 2, STRIDE, HD_N, CMP_HID)
    w = jnp.transpose(w1r, (2, 0, 3, 1, 4))
    eye = jnp.eye(N_KVG, dtype=w.dtype)
    wbd = jnp.einsum('gk,psdfe->psgdkfe', eye, w)
    return wbd.reshape(STRIDE, 2, N_KVG * HD_N, N_KVG * 2 * CMP_HID).astype(BF)


def _cmp_partial_rows(x, wbd, tr):
    r = x.shape[0]
    n = 2 * N_KVG * 2 * CMP_HID
    return pl.pallas_call(
        _cmp_partial_kernel,
        grid=(r // tr,),
        in_specs=[pl.BlockSpec((tr, STRIDE * KV_COLS), lambda i: (i, 0)),
                  pl.BlockSpec(wbd.shape, lambda i: (0, 0, 0, 0))],
        out_specs=pl.BlockSpec((tr, n), lambda i: (i, 0)),
        out_shape=jax.ShapeDtypeStruct((r, n), F32),
        compiler_params=_cparams("arbitrary"),
        name="cmp_partial",
    )(x, wbd)


PAGES_PER_STEP = 8


def _cmp_partial_paged(cache, page_table, wbd):
    b, npg = page_table.shape
    cpp = PAGE_SIZE // STRIDE
    n = 2 * N_KVG * 2 * CMP_HID
    steps = npg // PAGES_PER_STEP

    def page_spec(kpg):
        return pl.BlockSpec((None, cpp, STRIDE * KV_COLS), lambda i, j, pt: (pt[i, j * PAGES_PER_STEP + kpg], 0, 0))

    grid_spec = pltpu.PrefetchScalarGridSpec(
        num_scalar_prefetch=1,
        grid=(b, steps),
        in_specs=[page_spec(kpg) for kpg in range(PAGES_PER_STEP)]
        + [pl.BlockSpec(wbd.shape, lambda i, j, pt: (0, 0, 0, 0))],
        out_specs=pl.BlockSpec((None, PAGES_PER_STEP * cpp, n), lambda i, j, pt: (i, j, 0)),
    )
    return pl.pallas_call(
        _cmp_partial_paged_kernel,
        grid_spec=grid_spec,
        out_shape=jax.ShapeDtypeStruct((b, npg * cpp, n), F32),
        compiler_params=_cparams("arbitrary", "arbitrary"),
        name="cmp_partial_paged",
    )(page_table, *([cache] * PAGES_PER_STEP), wbd)


def _cmp_finish_kernel(c_ref, cpos_ref, w2_ref, o_ref):
    c = c_ref[...]
    nrow = c.shape[0]
    c_next = pltpu.roll(c, nrow - 1, axis=0)
    for s in range(2):
        for gi in range(N_KVG):
            o = (s * N_KVG + gi) * 2 * CMP_HID
            hid = (c[:, o:o + CMP_HID] + cpos_ref[0:1, o:o + CMP_HID]
                   + c_next[:, o + CMP_HID:o + 2 * CMP_HID] + cpos_ref[1:2, o + CMP_HID:o + 2 * CMP_HID])
            oo = (s * N_KVG + gi) * HD_N
            o_ref[:, oo:oo + HD_N] = _dot(_gelu_tanh(hid), w2_ref[s])


def _cmp_finish(c, cpos, w2):
    b, nch, n = c.shape
    return pl.pallas_call(
        _cmp_finish_kernel,
        grid=(b,),
        in_specs=[pl.BlockSpec((None, nch, n), lambda i: (i, 0, 0)), pl.BlockSpec(cpos.shape, lambda i: (0, 0)),
                  pl.BlockSpec(w2.shape, lambda i: (0, 0, 0))],
        out_specs=pl.BlockSpec((None, nch, KV_COLS), lambda i: (i, 0, 0)),
        out_shape=jax.ShapeDtypeStruct((b, nch, KV_COLS), F32),
        compiler_params=_cparams("arbitrary"),
        name="cmp_finish",
    )(c, cpos, w2.astype(BF))


def _cmp_pos_rows(cmp_pos):
    pos = cmp_pos.reshape(2, STRIDE, 1, 1, HD_N)
    rows = jnp.broadcast_to(pos, (2, STRIDE, 2, N_KVG, HD_N)).reshape(2, STRIDE * KV_COLS)
    return jnp.concatenate([rows, jnp.zeros((6, STRIDE * KV_COLS), F32)], axis=0)


def _bias_table_kernel(dist_ref, rb_ref, o_ref):
    bucket = _t5_bucket(dist_ref[...])
    for h in range(H_N):
        out = jnp.zeros(bucket.shape, F32)
        for b in range(N_BUCKETS):
            out = jnp.where(bucket == b, rb_ref[b, h], out)
        o_ref[h] = out


def _bias_table(dist, rel_bias, tr):
    r, n = dist.shape
    return pl.pallas_call(
        _bias_table_kernel,
        grid=(r // tr,),
        in_specs=[pl.BlockSpec((tr, n), lambda i: (i, 0)),
                  pl.BlockSpec(memory_space=pltpu.SMEM)],
        out_specs=pl.BlockSpec((H_N, tr, n), lambda i: (0, i, 0)),
        out_shape=jax.ShapeDtypeStruct((H_N, r, n), F32),
        compiler_params=_cparams("arbitrary"),
        name="bias_table",
    )(dist, rel_bias)


def _softmax_rows(logits, valid):
    lm = jnp.where(valid, logits, NEG_INF)
    e = jnp.exp(lm - jnp.max(lm, axis=-1, keepdims=True))
    return e / jnp.sum(e, axis=-1, keepdims=True)


def _cmp_attn_kernel(q_ref, kv_ref, bias_ref, ovt_ref, o_ref, sel_ref):
    tq = q_ref.shape[0]
    ncp = kv_ref.shape[0]
    nsb = ovt_ref.shape[0]
    q0 = pl.program_id(1) * tq
    qpos = q0 + lax.broadcasted_iota(I32, (tq, 1), 0)
    cend = lax.broadcasted_iota(I32, (1, ncp), 1) * STRIDE + (L_CMP - 1)
    valid = (qpos >= cend) & (lax.broadcasted_iota(I32, (1, ncp), 1) < ncp - 1)
    validf = valid.astype(F32)
    q = q_ref[...] * (HD_N ** -0.5)
    blk = lax.broadcasted_iota(I32, (nsb, tq), 0)
    cur = (q0 + lax.broadcasted_iota(I32, (1, tq), 1)) // L_SEL
    forced = (blk == 0) | (blk == cur) | (blk == cur - 1)
    future = blk > cur
    for gi in range(N_KVG):
        kc = kv_ref[:, gi * HD_N:(gi + 1) * HD_N]
        vc = kv_ref[:, (N_KVG + gi) * HD_N:(N_KVG + gi + 1) * HD_N]
        pcs = jnp.zeros((tq, ncp), F32)
        for hl in range(HPG):
            h = gi * HPG + hl
            sl = slice(h * HD_N, (h + 1) * HD_N)
            pc = _softmax_rows(_dot_nt(q[:, sl], kc) + bias_ref[h], valid) * validf
            pcs = pcs + pc
            o_ref[:, sl] = _dot(pc, vc)
        imp = _dot_nt(ovt_ref[...], pcs)
        score = jnp.where(forced, FORCE, jnp.where(future, -FORCE, imp))
        rank = jnp.zeros((nsb, tq), F32)
        for i in range(nsb):
            si = score[i:i + 1, :]
            rank = rank + ((si > score) | ((si == score) & (i < blk))).astype(F32)
        sel_ref[gi] = (rank < N_TOP).astype(sel_ref.dtype)


def _cmp_sel_overlap_t(nc, ncp, nsb):
    s = np.arange(nc)[None, :] * STRIDE
    j = np.arange(nsb)[:, None] * L_SEL
    ov = np.clip(np.minimum(s + L_CMP, j + L_SEL) - np.maximum(s, j), 0, None) / L_CMP
    return np.pad(ov, ((0, 0), (0, ncp - nc))).astype(np.float32)


def _cmp_attn_prompt(p_n, kv_cmp, bias_c, t):
    b = kv_cmp.shape[0]
    ncp = kv_cmp.shape[1]
    nsb = t // L_SEL
    tq = ATT_TILE
    nqt = t // tq
    ovt = jnp.asarray(_cmp_sel_overlap_t(ncp - 1, ncp, nsb), BF)
    return pl.pallas_call(
        _cmp_attn_kernel,
        grid=(b, nqt),
        in_specs=[pl.BlockSpec((tq, C_N), lambda i, j: (i * nqt + j, 0)),
                  pl.BlockSpec((None, ncp, KV_COLS), lambda i, j: (i, 0, 0)),
                  pl.BlockSpec((H_N, tq, ncp), lambda i, j: (0, j, 0)),
                  pl.BlockSpec((nsb, ncp), lambda i, j: (0, 0))],
        out_specs=[pl.BlockSpec((tq, C_N), lambda i, j: (i * nqt + j, 0)),
                   pl.BlockSpec((None, N_KVG, nsb, tq), lambda i, j: (i, 0, 0, j))],
        out_shape=[jax.ShapeDtypeStruct((b * t, C_N), F32), jax.ShapeDtypeStruct((b, N_KVG, nsb, t), BF)],
        compiler_params=_cparams("arbitrary", "arbitrary"),
        name="nsa_cmp_select",
    )(p_n, kv_cmp, bias_c, ovt)


N_BIAS_TILES = 3


def _swa_kernel(q_ref, ks_ref, vs_ref, kw_ref, vw_ref, sel_ref, exp_ref, bias_ref, gn_ref, oc_ref, o_ref):
    tq = q_ref.shape[0]
    tk = ATT_TILE
    qt = pl.program_id(2)
    q = q_ref[...] * (HD_N ** -0.5)
    q4 = jnp.concatenate([q[:, hl * HD_N:(hl + 1) * HD_N] for hl in range(HPG)], axis=0).astype(BF)
    sel_t = sel_ref[...]
    di = lax.broadcasted_iota(I32, (tq, tk), 0) - lax.broadcasted_iota(I32, (tq, tk), 1)

    def attend(k_ref, v_ref, lo, mask_fn):
        def body(ki, carry):
            m, l, acc = carry
            start = pl.multiple_of(ki * tk, tk)
            kt = k_ref[pl.ds(start, tk), :]
            vt = v_ref[pl.ds(start, tk), :]
            off = qt - ki
            s = lax.dot_general(q4, kt, (((1,), (1,)), ((), ())), preferred_element_type=F32)
            s = s + bias_ref[jnp.minimum(off, N_BIAS_TILES - 1)]
            ok = mask_fn(ki, off * tk + di)
            ok4 = jnp.concatenate([ok] * HPG, axis=0)
            s = jnp.where(ok4, s, NEG_INF)
            m_new = jnp.maximum(m, jnp.max(s, axis=-1, keepdims=True))
            alpha = jnp.exp(m - m_new)
            p = jnp.where(ok4, jnp.exp(s - m_new), 0.0)
            l = alpha * l + jnp.sum(p, axis=-1, keepdims=True)
            acc = alpha * acc + jnp.dot(p.astype(BF), vt, preferred_element_type=F32)
            return m_new, l, acc

        init = (jnp.full((HPG * tq, 1), NEG_INF, F32), jnp.zeros((HPG * tq, 1), F32),
                jnp.zeros((HPG * tq, HD_N), F32))
        m, l, acc = lax.fori_loop(lo, qt + 1, body, init)
        return acc / l

    def sel_mask(ki, dist):
        allowed = lax.dot_general(sel_t, exp_ref[ki], (((0,), (0,)), ((), ())), preferred_element_type=F32)
        return (allowed > 0.5) & (dist >= 0)

    def win_mask(ki, dist):
        del ki
        return (dist >= 0) & (dist <= WINDOW)

    o_sel = attend(ks_ref, vs_ref, 0, sel_mask)
    o_win = attend(kw_ref, vw_ref, jnp.maximum(qt - WINDOW // tk, 0), win_mask)
    gates = _sigmoid(gn_ref[...])
    for hl in range(HPG):
        rows = slice(hl * tq, (hl + 1) * tq)
        sl = slice(hl * HD_N, (hl + 1) * HD_N)
        o = (gates[:, 3 * hl:3 * hl + 1] * oc_ref[:, sl] + gates[:, 3 * hl + 1:3 * hl + 2] * o_sel[rows]
             + gates[:, 3 * hl + 2:3 * hl + 3] * o_win[rows])
        o_ref[:, sl] = o.astype(o_ref.dtype)


def _swa_prompt(p_n, kv_sel, kv_win, sel, o_cmp, bias_t, t):
    b = kv_sel.shape[0]
    tq = ATT_TILE
    nqt = t // tq
    nsb = t // L_SEL
    gw = HPG * HD_N
    expand = np.zeros((nqt, nsb, tq), np.float32)
    for ki in range(nqt):
        expand[ki, (ki * tq + np.arange(tq)) // L_SEL, np.arange(tq)] = 1.0
    kv_spec = lambda s: pl.BlockSpec((None, None, None, t, HD_N), lambda i, g, j: (i, s, g, 0, 0))
    return pl.pallas_call(
        _swa_kernel,
        grid=(b, N_KVG, nqt),
        in_specs=[pl.BlockSpec((tq, gw), lambda i, g, j: (i * nqt + j, g)),
                  kv_spec(0), kv_spec(1), kv_spec(0), kv_spec(1),
                  pl.BlockSpec((None, None, nsb, tq), lambda i, g, j: (i, g, 0, j)),
                  pl.BlockSpec((nqt, nsb, tq), lambda i, g, j: (0, 0, 0)),
                  pl.BlockSpec((N_BIAS_TILES, HPG * tq, tq), lambda i, g, j: (0, g, 0)),
                  pl.BlockSpec((tq, GN_GROUP_COLS), lambda i, g, j: (i * nqt + j, NP_GN // GN_GROUP_COLS + g)),
                  pl.BlockSpec((tq, gw), lambda i, g, j: (i * nqt + j, g))],
        out_specs=pl.BlockSpec((tq, gw), lambda i, g, j: (i * nqt + j, g)),
        out_shape=jax.ShapeDtypeStruct((b * t, C_N), BF),
        compiler_params=_cparams("arbitrary", "arbitrary", "arbitrary"),
        name="nsa_sel_win",
    )(p_n, kv_sel, kv_sel, kv_win, kv_win, sel, jnp.asarray(expand, BF), bias_t, p_n, o_cmp)


def _sample_cmp_kernel(q_ref, kv_ref, rbt_ref, ov_ref, o_ref, idx_ref, *, past):
    ncp = kv_ref.shape[0]
    nsbp = ov_ref.shape[1]
    nsb = -(-(past + 1) // L_SEL)
    q = q_ref[...] * (HD_N ** -0.5)
    hrow = lax.broadcasted_iota(I32, (H_N, 1), 0)
    nidx = lax.broadcasted_iota(I32, (1, ncp), 1)
    valid = nidx < ncp - 1
    bias = _bias_rows(past - (nidx * STRIDE + (L_CMP - 1)), rbt_ref[...])
    logits = jnp.zeros((H_N, ncp), F32)
    for gi in range(N_KVG):
        lg = _dot_nt(q, kv_ref[:, gi * HD_N:(gi + 1) * HD_N])
        logits = jnp.where(hrow // HPG == gi, lg, logits)
    pc = _softmax_rows(logits + bias, valid) * valid.astype(F32)
    o = jnp.zeros((H_N, HD_N), F32)
    for gi in range(N_KVG):
        og = _dot(pc, kv_ref[:, (N_KVG + gi) * HD_N:(N_KVG + gi + 1) * HD_N])
        o = jnp.where(hrow // HPG == gi, og, o)
    o_ref[...] = o
    imp_h = _dot(pc, ov_ref[...])
    blk = lax.broadcasted_iota(I32, (8, nsbp), 1)
    grow = lax.broadcasted_iota(I32, (8, 1), 0)
    cur = past // L_SEL
    score = jnp.full((8, nsbp), -3e38, F32)
    for gi in range(N_KVG):
        imp = jnp.sum(jnp.where(hrow // HPG == gi, imp_h, 0.0), axis=0, keepdims=True)
        score = jnp.where(grow == gi, imp, score)
    forced = (blk == 0) | (blk == cur) | (blk == cur - 1)
    score = jnp.where(forced, FORCE, jnp.where(blk > cur, -FORCE, score))
    score = jnp.where((blk < nsb) & (grow < N_KVG), score, -3e38)
    lane = lax.broadcasted_iota(I32, (8, LANES), 1)
    picks = jnp.zeros((8, LANES), I32)
    for it in range(N_TOP):
        mx = jnp.max(score, axis=-1, keepdims=True)
        pick = jnp.min(jnp.where(score == mx, blk, nsbp), axis=-1, keepdims=True)
        picks = jnp.where(lane == it, pick, picks)
        score = jnp.where(blk == pick, -3e38, score)
    idx_ref[...] = picks


def _sample_cmp(q, kv_cmp, rel_bias, past):
    b, ncp, _ = kv_cmp.shape
    nc = ncp - 1
    nsb = -(-(past + 1) // L_SEL)
    nsbp = -(-nsb // LANES) * LANES
    s = np.arange(nc)[:, None] * STRIDE
    j = np.arange(nsb)[None, :] * L_SEL
    ov = np.clip(np.minimum(s + L_CMP, j + L_SEL) - np.maximum(s, j), 0, None) / L_CMP
    ov = np.pad(ov, ((0, ncp - nc), (0, nsbp - nsb))).astype(np.float32)
    return pl.pallas_call(
        functools.partial(_sample_cmp_kernel, past=past),
        grid=(b,),
        in_specs=[pl.BlockSpec((None, H_N, HD_N), lambda i: (i, 0, 0)),
                  pl.BlockSpec((None, ncp, KV_COLS), lambda i: (i, 0, 0)),
                  pl.BlockSpec((H_N, N_BUCKETS), lambda i: (0, 0)),
                  pl.BlockSpec((ncp, nsbp), lambda i: (0, 0))],
        out_specs=[pl.BlockSpec((None, H_N, HD_N), lambda i: (i, 0, 0)),
                   pl.BlockSpec((None, 8, LANES), lambda i: (i, 0, 0))],
        out_shape=[jax.ShapeDtypeStruct((b, H_N, HD_N), F32), jax.ShapeDtypeStruct((b, 8, LANES), I32)],
        compiler_params=_cparams("arbitrary"),
        name="nsa_sample_cmp_select",
    )(q, kv_cmp, rel_bias.T, jnp.asarray(ov, BF))


def _row_copy_kernel(idx_ref, src_ref, *rest, rows, n_src, scatter):
    dst_ref, sem = rest[-2], rest[-1]
    base = pl.program_id(0) * rows

    def copy(i):
        r = base + i
        if scatter:
            return pltpu.make_async_copy(src_ref.at[r % n_src], dst_ref.at[idx_ref[r]], sem)
        return pltpu.make_async_copy(src_ref.at[idx_ref[r]], dst_ref.at[r], sem)

    def issue(i, c):
        copy(i).start()
        return c

    def drain(i, c):
        copy(i).wait()
        return c

    lax.fori_loop(0, rows, issue, 0)
    lax.fori_loop(0, rows, drain, 0)


def _row_copy(src, idx, rows_per_step, dst_init=None):
    n = idx.shape[0]
    scatter = dst_init is not None
    any_spec = pl.BlockSpec(memory_space=pl.ANY)
    out_shape = (jax.ShapeDtypeStruct(dst_init.shape, dst_init.dtype) if scatter
                 else jax.ShapeDtypeStruct((n,) + src.shape[1:], src.dtype))
    return pl.pallas_call(
        functools.partial(_row_copy_kernel, rows=rows_per_step, n_src=src.shape[0], scatter=scatter),
        grid_spec=pltpu.PrefetchScalarGridSpec(
            num_scalar_prefetch=1, grid=(n // rows_per_step,),
            in_specs=[any_spec] * (2 if scatter else 1),
            out_specs=any_spec,
            scratch_shapes=[pltpu.SemaphoreType.DMA(())]),
        out_shape=out_shape,
        input_output_aliases={2: 0} if scatter else {},
        compiler_params=_cparams("arbitrary"),
        name="scatter_rows" if scatter else "gather_rows",
    )(idx, src, *((dst_init,) if scatter else ()))


def _sample_swa_kernel(idx_ref, q_ref, blk_ref, win_ref, new_ref, rbt_ref, gate_ref, oc_ref, o_ref, *, past):
    bi = pl.program_id(0)
    q = q_ref[...] * (HD_N ** -0.5)
    rbt = rbt_ref[...]
    hrow = lax.broadcasted_iota(I32, (H_N, 1), 0)
    nk = N_TOP * L_SEL
    lane = lax.broadcasted_iota(I32, (1, nk), 1)
    new_blk = past // L_SEL
    bias_new = rbt[:, 0:1]
    gates = _sigmoid(gate_ref[...])
    nwin = win_ref.shape[0]
    wdist = nwin - lax.broadcasted_iota(I32, (1, nwin), 1)
    wbias = _bias_rows(wdist, rbt)
    wvalid = (wdist >= 0) & (wdist <= WINDOW)

    def with_new(logits, valid, vals, k_new, v_new):
        l_new = jnp.sum(q * k_new, axis=-1, keepdims=True) + bias_new
        lm = jnp.where(valid, logits, NEG_INF)
        m = jnp.maximum(jnp.max(lm, axis=-1, keepdims=True), l_new)
        e = jnp.where(valid, jnp.exp(lm - m), 0.0)
        e_new = jnp.exp(l_new - m)
        den = jnp.sum(e, axis=-1, keepdims=True) + e_new
        return (_dot(e, vals) + e_new * v_new) / den

    o_sel = jnp.zeros((H_N, HD_N), F32)
    o_win = jnp.zeros((H_N, HD_N), F32)
    for gi in range(N_KVG):
        ksl = slice(gi * HD_N, (gi + 1) * HD_N)
        vsl = slice((N_KVG + gi) * HD_N, (N_KVG + gi + 1) * HD_N)
        bid = jnp.zeros((1, nk), I32)
        for n in range(N_TOP):
            bid = jnp.where(lane // L_SEL == n, idx_ref[bi, gi, n], bid)
        dist = past - (bid * L_SEL + lane % L_SEL)
        valid = (bid != new_blk) & (dist >= 0)
        kv = blk_ref[gi]
        logits = _dot_nt(q, kv[:, ksl]) + _bias_rows(dist, rbt)
        og = with_new(logits, valid, kv[:, vsl], new_ref[0:1, ksl], new_ref[0:1, vsl])
        o_sel = jnp.where(hrow // HPG == gi, og, o_sel)
        logits = _dot_nt(q, win_ref[:, ksl]) + wbias
        og = with_new(logits, wvalid, win_ref[:, vsl], new_ref[1:2, ksl], new_ref[1:2, vsl])
        o_win = jnp.where(hrow // HPG == gi, og, o_win)
    o_ref[...] = gates[:, 0:1] * oc_ref[...] + gates[:, 1:2] * o_sel + gates[:, 2:3] * o_win


def _sample_swa(idx, q, blocks, win, new_kv, rel_bias, gates, o_cmp, past):
    b = q.shape[0]
    nk = N_TOP * L_SEL
    w = win.shape[1]
    grid_spec = pltpu.PrefetchScalarGridSpec(
        num_scalar_prefetch=1, grid=(b,),
        in_specs=[pl.BlockSpec((None, H_N, HD_N), lambda i, ix: (i, 0, 0)),
                  pl.BlockSpec((None, N_KVG, nk, KV_COLS), lambda i, ix: (i, 0, 0, 0)),
                  pl.BlockSpec((None, w, KV_COLS), lambda i, ix: (i, 0, 0)),
                  pl.BlockSpec((None, 2, KV_COLS), lambda i, ix: (i, 0, 0)),
                  pl.BlockSpec((H_N, N_BUCKETS), lambda i, ix: (0, 0)),
                  pl.BlockSpec((None, H_N, 3), lambda i, ix: (i, 0, 0)),
                  pl.BlockSpec((None, H_N, HD_N), lambda i, ix: (i, 0, 0))],
        out_specs=pl.BlockSpec((None, H_N, HD_N), lambda i, ix: (i, 0, 0)))
    return pl.pallas_call(
        functools.partial(_sample_swa_kernel, past=past),
        grid_spec=grid_spec,
        out_shape=jax.ShapeDtypeStruct((b, H_N, HD_N), F32),
        compiler_params=_cparams("arbitrary"),
        name="nsa_sample_sel_win",
    )(idx, q, blocks, win, new_kv, rel_bias.T, gates, o_cmp)


ROUTER_COLS = LANES


def _router_kernel(x_ref, g_ref, sh_ref, sc_ref, w_ref, b_ref, h_ref, e_ref, wt_ref, rk_ref, cnt_ref):
    i = pl.program_id(0)
    tm = x_ref.shape[0]

    @pl.when(i == 0)
    def _():
        cnt_ref[...] = jnp.zeros_like(cnt_ref)

    h = (_rms(x_ref[...], g_ref[...]) * (1.0 + sc_ref[...]) + sh_ref[...]).astype(BF)
    h_ref[:, 0, :] = h
    logits = jnp.dot(h, w_ref[...], preferred_element_type=F32) + b_ref[...]
    lane = lax.broadcasted_iota(I32, (tm, ROUTER_COLS), 1)

    def top1(vals, ok):
        vm = jnp.where(ok, vals, -3e38)
        mx = jnp.max(vm, axis=-1, keepdims=True)
        return mx, jnp.min(jnp.where(ok & (vm == mx), lane, ROUTER_COLS), axis=-1, keepdims=True)

    isg = lane < N_EGROUPS
    pg = _softmax_rows(logits, isg)
    g_w, g_i = top1(pg, isg)
    ise = (lane >= N_EGROUPS) & ((lane - N_EGROUPS) // EXP_PER_GROUP == g_i)
    pe = _softmax_rows(logits, ise)
    w0, l0 = top1(pe, ise)
    w1, l1 = top1(pe, ise & (lane != l0))
    den = w0 + w1
    e0 = l0 - N_EGROUPS
    e1 = l1 - N_EGROUPS
    e_ref[...] = jnp.where(lane == 0, e0, jnp.where(lane == 1, e1, 0))
    wt_ref[...] = jnp.where(lane == 0, w0 / den * g_w, jnp.where(lane == 1, w1 / den * g_w, 0.0))
    oh0 = (lane == e0).astype(F32)
    oh1 = (lane == e1).astype(F32)
    cnt = oh0 + oh1
    ti = lax.broadcasted_iota(I32, (tm, tm), 0)
    si = lax.broadcasted_iota(I32, (tm, tm), 1)
    before = _dot((ti > si).astype(F32), cnt) + cnt_ref[...]
    r0 = jnp.sum(before * oh0, axis=-1, keepdims=True)
    r1 = jnp.sum(before * oh1, axis=-1, keepdims=True)
    rk_ref[...] = jnp.where(lane == 0, r0, jnp.where(lane == 1, r1, 0.0)).astype(I32)
    cnt_ref[...] = cnt_ref[...] + jnp.sum(cnt, axis=0, keepdims=True)


def _router(x, g, shift, scale, w_r, b_r, tm, rpb):
    m = x.shape[0]
    r = shift.shape[1]
    rows = lambda tn: pl.BlockSpec((tm, tn), lambda i: (i, 0))
    mods = pl.BlockSpec((None, r, D_MODEL), lambda i: ((i * tm) // rpb, 0, 0))
    small = lambda dt: jax.ShapeDtypeStruct((m, ROUTER_COLS), dt)
    return pl.pallas_call(
        _router_kernel,
        grid=(m // tm,),
        in_specs=[rows(D_MODEL), pl.BlockSpec((1, D_MODEL), lambda i: (0, 0)), mods, mods,
                  pl.BlockSpec((D_MODEL, ROUTER_COLS), lambda i: (0, 0)),
                  pl.BlockSpec((1, ROUTER_COLS), lambda i: (0, 0))],
        out_specs=[pl.BlockSpec((tm, 1, D_MODEL), lambda i: (i, 0, 0)),
                   rows(ROUTER_COLS), rows(ROUTER_COLS), rows(ROUTER_COLS),
                   pl.BlockSpec((1, ROUTER_COLS), lambda i: (0, 0))],
        out_shape=[jax.ShapeDtypeStruct((m, 1, D_MODEL), BF), small(I32), small(F32), small(I32),
                   jax.ShapeDtypeStruct((1, ROUTER_COLS), F32)],
        compiler_params=_cparams("arbitrary"),
        name="moe_router",
    )(x, g.reshape(1, D_MODEL), shift, scale, w_r, b_r)


def _expert_kernel(be_ref, x_ref, w1_ref, w3_ref, w2_ref, o_ref):
    del be_ref
    x = x_ref[:, 0, :]
    a = jnp.dot(x, w1_ref[...].astype(BF), preferred_element_type=F32)
    b = jnp.dot(x, w3_ref[...].astype(BF), preferred_element_type=F32)
    hid = a * _sigmoid(a) * b
    o_ref[:, 0, :] = _dot(hid, w2_ref[...])


def _experts(xs, blk_exp, w1, w3, w2, blk):
    rows = xs.shape[0]
    nblk = rows // blk
    grid_spec = pltpu.PrefetchScalarGridSpec(
        num_scalar_prefetch=1, grid=(nblk,),
        in_specs=[pl.BlockSpec((blk, 1, D_MODEL), lambda i, be: (i, 0, 0)),
                  pl.BlockSpec((None, D_MODEL, D_EXP), lambda i, be: (be[i], 0, 0)),
                  pl.BlockSpec((None, D_MODEL, D_EXP), lambda i, be: (be[i], 0, 0)),
                  pl.BlockSpec((None, D_EXP, D_MODEL), lambda i, be: (be[i], 0, 0))],
        out_specs=pl.BlockSpec((blk, 1, D_MODEL), lambda i, be: (i, 0, 0)))
    return pl.pallas_call(
        _expert_kernel,
        grid_spec=grid_spec,
        out_shape=jax.ShapeDtypeStruct((rows, 1, D_MODEL), F32),
        compiler_params=_cparams("arbitrary"),
        name="moe_experts",
    )(blk_exp, xs, w1, w3, w2)


def _final_kernel(x_ref, g_ref, y0_ref, y1_ref, wt_ref, nf_ref, o_ref):
    wt = wt_ref[...]
    moe = wt[:, 0:1] * y0_ref[:, 0, :] + wt[:, 1:2] * y1_ref[:, 0, :]
    o_ref[...] = _rms(x_ref[...] + g_ref[...] * moe, nf_ref[...])


def _final(x, gate, yg, wts, norm_f, tm, rpb):
    m = x.shape[0]
    r = gate.shape[1]
    rows = lambda tn: pl.BlockSpec((tm, tn), lambda i: (i, 0))
    return pl.pallas_call(
        _final_kernel,
        grid=(m // tm,),
        in_specs=[rows(D_MODEL), pl.BlockSpec((None, r, D_MODEL), lambda i: ((i * tm) // rpb, 0, 0)),
                  pl.BlockSpec((None, tm, 1, D_MODEL), lambda i: (0, i, 0, 0)),
                  pl.BlockSpec((None, tm, 1, D_MODEL), lambda i: (1, i, 0, 0)),
                  rows(ROUTER_COLS), pl.BlockSpec((1, D_MODEL), lambda i: (0, 0))],
        out_specs=rows(D_MODEL),
        out_shape=jax.ShapeDtypeStruct((m, D_MODEL), F32),
        compiler_params=_cparams("arbitrary"),
        name="moe_combine_final_norm",
    )(x, gate, yg, yg, wts, norm_f.reshape(1, D_MODEL))


def _moe_and_final(x1, g2, shift, scale, gate, w_r, b_r, exp_w1, exp_w3, exp_w2, norm_f, tm, rpb, blk):
    m = x1.shape[0]
    h2, eid, wts, rank, counts = _router(x1, g2, shift, scale, w_r, b_r, tm, rpb)
    counts = counts[0, :N_EXP].astype(I32)
    padded = (counts + blk - 1) // blk * blk
    pend = jnp.cumsum(padded)
    pstart = pend - padded
    n_blocks = -(-(m * TOP_K) // blk) + N_EXP
    blk_exp = jnp.minimum(jnp.searchsorted(pend, jnp.arange(n_blocks, dtype=I32) * blk, side='right'),
                          N_EXP - 1).astype(I32)
    e = eid[:, :TOP_K]
    dest = (pstart[e] + rank[:, :TOP_K]).T.reshape(-1)
    step = math.gcd(m * TOP_K, 1024)
    xs = _row_copy(h2, dest, step, jnp.zeros((n_blocks * blk, 1, D_MODEL), BF))
    ys = _experts(xs, blk_exp, exp_w1, exp_w3, exp_w2, blk)
    yg = _row_copy(ys, dest, step).reshape(TOP_K, m, 1, D_MODEL)
    return _final(x1, gate, yg, wts, norm_f, tm, rpb)


def _pack_in_proj(w_in):
    o = C_RIN
    w_r = w_in[:, :o]
    w_q = w_in[:, o:o + C_N + 3 * KV_COLS]
    o += C_N + 3 * KV_COLS
    w_gn = w_in[:, o:o + 3 * H_N].reshape(D_MODEL, N_KVG, 3 * HPG)
    w_gn = jnp.pad(w_gn, ((0, 0), (0, 0), (0, GN_GROUP_COLS - 3 * HPG))).reshape(D_MODEL, N_KVG * GN_GROUP_COLS)
    o += 3 * H_N
    w_gm = w_in[:, o:]
    return w_r.astype(BF), jnp.concatenate([w_q, w_gn], axis=1).astype(BF), w_gm.astype(BF)


def _heads_major(kv, b, t):
    return jnp.transpose(kv.reshape(b, t, 2, N_KVG, HD_N), (0, 2, 3, 1, 4)).astype(BF)


def _prompt_bias_tables(rel_bias, t):
    tq = ATT_TILE
    i = np.arange(tq)[:, None]
    j = np.arange(tq)[None, :]
    dist = np.concatenate([d * tq + i - j for d in range(N_BIAS_TILES)], axis=0).astype(np.int32)
    tiles = _bias_table(jnp.asarray(dist), rel_bias, tq)
    tiles = jnp.transpose(tiles.reshape(H_N, N_BIAS_TILES, tq, tq), (1, 0, 2, 3)).reshape(N_BIAS_TILES, H_N * tq, tq)
    nc = (t - L_CMP) // STRIDE + 1
    ncp = nc + 1
    dc = (np.arange(t)[:, None] - (np.arange(ncp)[None, :] * STRIDE + L_CMP - 1)).astype(np.int32)
    return tiles, _bias_table(jnp.asarray(dc), rel_bias, tq)


def kernel(x_prompt, x_sample, c_prompt, c_sample, cache_cmp_kv, cache_sel_kv, state_win_kv, state_rwkv_shift,
           state_rwkv_wkv, page_table, rel_bias, norm_f, norm1, norm2, w_ada, b_ada, w_in, rwkv_mu, rwkv_w0, rwkv_w2,
           rwkv_a0, rwkv_a2, rwkv_g2, rwkv_kk, rwkv_ka, rwkv_rk, rwkv_ln_g, rwkv_ln_b, cmp_pos, cmp_w1, cmp_w2,
           w_o_rwkv, w_o_nsa, w_out, router_wg, router_bg, router_we, router_be, exp_w1, exp_w3, exp_w2):
    bp, t, _ = x_prompt.shape
    bs = x_sample.shape[0]
    mp = bp * t
    past = page_table.shape[1] * PAGE_SIZE

    nrow = -(-(bp + bs) // 8) * 8
    c_all = jnp.concatenate([c_prompt, c_sample, jnp.zeros((nrow - bp - bs, D_MODEL), F32)], axis=0)
    mod = _ada(c_all, w_ada[0], b_ada[0]).reshape(nrow, 6, D_MODEL)
    mod_p = [mod[:bp, i][:, None, :] for i in range(6)]
    mod_s = [mod[bp:bp + bs, i][None] for i in range(6)]

    w_r, w_n, w_gm = _pack_in_proj(w_in[0])
    rw = _rwkv_weights(rwkv_mu[0], rwkv_w0[0], rwkv_w2[0], rwkv_a0[0], rwkv_a2[0], rwkv_g2[0], rwkv_kk[0],
                       rwkv_ka[0], rwkv_rk[0], rwkv_ln_g[0], rwkv_ln_b[0])
    wbd = _cmp_weights(cmp_w1[0])
    cpos = _cmp_partial_rows(_cmp_pos_rows(cmp_pos[0]), wbd, 8)
    wo_r, wo_n, wo = w_o_rwkv[0].astype(BF), w_o_nsa[0].astype(BF), w_out[0].astype(BF)
    w_router = jnp.pad(jnp.concatenate([router_wg[0], router_we[0]], axis=1),
                       ((0, 0), (0, ROUTER_COLS - N_EGROUPS - N_EXP))).astype(BF)
    b_router = jnp.pad(jnp.concatenate([router_bg[0], router_be[0]]), (0, ROUTER_COLS - N_EGROUPS - N_EXP))[None]

    tm = 512
    xp = x_prompt.reshape(mp, D_MODEL)
    h = _norm_mod(xp, norm1[0], mod_p[0], mod_p[1], tm, t)
    p_r = _matmul(h, w_r, tm, C_RIN // 2)
    p_n = _matmul(h, w_n, tm, NP_COLS // 2)
    p_g = _matmul(h, w_gm, tm, 2048)
    o_r, shift_p, wkv_p = _rwkv_prompt(p_r.reshape(bp, t, C_RIN), rw)
    kvc = p_n[:, NP_KVC:NP_KVC + KV_COLS]
    kvs = p_n[:, NP_KVS:NP_KVS + KV_COLS]
    kvw = p_n[:, NP_KVW:NP_KVW + KV_COLS]
    nch = t // STRIDE
    c_part = _cmp_partial_rows(kvc.reshape(bp * nch, STRIDE * KV_COLS), wbd, nch)
    kv_cmp = _cmp_finish(c_part.reshape(bp, nch, -1), cpos, cmp_w2[0])
    bias_t, bias_c = _prompt_bias_tables(rel_bias, t)
    o_cmp, sel = _cmp_attn_prompt(p_n, kv_cmp, bias_c, t)
    o_n = _swa_prompt(p_n, _heads_major(kvs, bp, t), _heads_major(kvw, bp, t), sel, o_cmp, bias_t, t)
    y = _merge(o_r.reshape(mp, C_R), o_n, wo_r, wo_n, p_g, tm)
    x1 = _proj_residual(y, wo, xp, mod_p[2], tm, t)
    y_prompt = _moe_and_final(x1, norm2[0], mod_p[3], mod_p[4], mod_p[5], w_router, b_router, exp_w1[0], exp_w3[0],
                              exp_w2[0], norm_f, tm, t, 128).reshape(bp, t, D_MODEL)
    kv_shape = (1, bp, t, 2, N_KVG, HD_N)
    wlen = min(WINDOW, t)
    win_p = kvw.reshape(bp, t, KV_COLS)[:, t - wlen:].reshape(1, bp, wlen, 2, N_KVG, HD_N)

    xs = x_sample.reshape(bs, D_MODEL)
    hs = _norm_mod(xs, norm1[0], mod_s[0], mod_s[1], bs, bs)
    ps_r = _matmul(hs, w_r, bs, C_RIN // 2)
    ps_n = _matmul(hs, w_n, bs, NP_COLS // 2)
    ps_g = _matmul(hs, w_gm, bs, 2048)
    os_r, wkv_s = _rwkv_step(ps_r, state_rwkv_shift[0], state_rwkv_wkv[0], rw)
    kvc_s = ps_n[:, NP_KVC:NP_KVC + KV_COLS]
    kvs_s = ps_n[:, NP_KVS:NP_KVS + KV_COLS]
    kvw_s = ps_n[:, NP_KVW:NP_KVW + KV_COLS]
    n_pool = cache_cmp_kv.shape[1]
    cpp = PAGE_SIZE // STRIDE
    cs_part = _cmp_partial_paged(cache_cmp_kv[0].reshape(n_pool, cpp, STRIDE * KV_COLS), page_table, wbd)
    kv_cmp_s = _cmp_finish(cs_part, cpos, cmp_w2[0])
    q_s = ps_n[:, :C_N].reshape(bs, H_N, HD_N)
    o_cmp_s, picks = _sample_cmp(q_s, kv_cmp_s, rel_bias, past)
    idx = picks[:, :N_KVG, :N_TOP]
    bpp = PAGE_SIZE // L_SEL
    npb = past // L_SEL
    idc = jnp.minimum(idx, npb - 1)
    phys = jnp.take_along_axis(page_table, (idc // bpp).reshape(bs, -1), axis=1).reshape(idx.shape) * bpp + idc % bpp
    pool = cache_sel_kv.reshape(n_pool * bpp, L_SEL, 2, N_KVG, HD_N)
    blocks = _row_copy(pool, phys.reshape(-1), bs * N_KVG * N_TOP // 8)
    blocks = blocks.reshape(bs, N_KVG, N_TOP * L_SEL, KV_COLS)
    win_buf = state_win_kv[0].reshape(bs, -1, KV_COLS)
    gates_s = ps_n[:, NP_GN:].reshape(bs, N_KVG, GN_GROUP_COLS)[:, :, :3 * HPG].reshape(bs, H_N, 3)
    new_kv = jnp.stack([kvs_s, kvw_s], axis=1)
    os_n = _sample_swa(idx, q_s, blocks, win_buf, new_kv, rel_bias, gates_s, o_cmp_s, past)
    ys = _merge(os_r, os_n.reshape(bs, C_N).astype(BF), wo_r, wo_n, ps_g, bs)
    xs1 = _proj_residual(ys, wo, xs, mod_s[2], bs, bs)
    y_sample = _moe_and_final(xs1, norm2[0], mod_s[3], mod_s[4], mod_s[5], w_router, b_router, exp_w1[0], exp_w3[0],
                              exp_w2[0], norm_f, bs, bs, 16).reshape(bs, 1, D_MODEL)
    kv1 = (1, bs, 1, 2, N_KVG, HD_N)
    wbuf = win_buf.shape[1]
    win_s = jnp.concatenate([win_buf, kvw_s[:, None, :]], axis=1)[:, -wbuf:].reshape(1, bs, wbuf, 2, N_KVG, HD_N)

    return (y_prompt, y_sample,
            kvc.reshape(kv_shape), kvc_s.reshape(kv1),
            kvs.reshape(kv_shape), kvs_s.reshape(kv1),
            win_p, win_s,
            shift_p.reshape(1, bp, C_RIN), ps_r.reshape(1, bs, C_RIN),
            wkv_p[None], wkv_s[None])
```

```python
import functools
import math

import numpy as np
import jax
import jax.numpy as jnp
from jax import lax
from jax.experimental import pallas as pl
from jax.experimental.pallas import tpu as pltpu

D_MODEL = 2048
PAGE_SIZE = 128
H_R, HD_R = 16, 64
C_R = H_R * HD_R
LORA_W, LORA_A, LORA_G = 64, 64, 128
C_RIN = 3 * C_R + LORA_W + LORA_A + LORA_G
LN_X_EPS = 64e-5
H_N, HD_N, N_KVG = 16, 64, 4
HPG = H_N // N_KVG
C_N = H_N * HD_N
KV_COLS = 2 * N_KVG * HD_N
L_CMP, STRIDE, CMP_HID = 32, 16, 64
L_SEL, N_TOP, WINDOW = 64, 16, 512
N_BUCKETS, MAX_DIST = 32, 128
N_EGROUPS, EXP_PER_GROUP = 4, 8
N_EXP = N_EGROUPS * EXP_PER_GROUP
TOP_K, D_EXP = 2, 512
RMS_EPS = 1e-6
NEG_INF = -1e30
FORCE = 1e9

BF = jnp.bfloat16
F32 = jnp.float32
I32 = jnp.int32

VMEM_LIMIT_BYTES = 56 * 1024 * 1024
LANES = 128
RW_CHUNK = 32
ATT_TILE = 256
GN_GROUP_COLS = 128
NP_Q, NP_KVC, NP_KVS, NP_KVW, NP_GN = 0, C_N, C_N + KV_COLS, C_N + 2 * KV_COLS, C_N + 3 * KV_COLS
NP_COLS = NP_GN + N_KVG * GN_GROUP_COLS


def _cparams(*sem):
    return pltpu.CompilerParams(dimension_semantics=sem, vmem_limit_bytes=VMEM_LIMIT_BYTES)


def _dot(a, b):
    return jnp.dot(a.astype(BF), b.astype(BF), preferred_element_type=F32)


def _dot_nt(a, b):
    return lax.dot_general(a.astype(BF), b.astype(BF), (((1,), (1,)), ((), ())), preferred_element_type=F32)


def _dot_tn(a, b):
    return lax.dot_general(a.astype(BF), b.astype(BF), (((0,), (0,)), ((), ())), preferred_element_type=F32)


def _softplus(x):
    return jnp.maximum(x, 0.0) + jnp.log1p(jnp.exp(-jnp.abs(x)))


def _sigmoid(x):
    return 1.0 / (1.0 + jnp.exp(-x))


def _gelu_tanh(x):
    return 0.5 * x * (1.0 + jnp.tanh(math.sqrt(2.0 / math.pi) * (x + 0.044715 * x * x * x)))


def _t5_bucket(dist):
    n = jnp.maximum(dist, 0)
    max_exact = N_BUCKETS // 2
    nf = jnp.maximum(n, 1).astype(F32)
    large = max_exact + (jnp.log(nf / max_exact) / math.log(MAX_DIST / max_exact)
                         * (N_BUCKETS - max_exact)).astype(I32)
    large = jnp.minimum(large, N_BUCKETS - 1)
    return jnp.where(n < max_exact, n, large)


def _bias_rows(dist, rbt):
    bucket = _t5_bucket(dist)
    out = jnp.zeros((rbt.shape[0], dist.shape[1]), F32)
    for b in range(N_BUCKETS):
        out = jnp.where(bucket == b, rbt[:, b:b + 1], out)
    return out


def _ada_kernel(c_ref, w_ref, b_ref, o_ref):
    o_ref[...] = _dot(c_ref[...], w_ref[...]) + b_ref[...]


def _ada(c, w_ada, b_ada):
    r = c.shape[0]
    n = w_ada.shape[1]
    tn = 1024
    return pl.pallas_call(
        _ada_kernel,
        grid=(n // tn,),
        in_specs=[pl.BlockSpec((r, D_MODEL), lambda j: (0, 0)),
                  pl.BlockSpec((D_MODEL, tn), lambda j: (0, j)),
                  pl.BlockSpec((1, tn), lambda j: (0, j))],
        out_specs=pl.BlockSpec((r, tn), lambda j: (0, j)),
        out_shape=jax.ShapeDtypeStruct((r, n), F32),
        compiler_params=_cparams("arbitrary"),
        name="ada_mod",
    )(c, w_ada, b_ada.reshape(1, n))


def _rms(x, g):
    return x * lax.rsqrt(jnp.mean(x * x, axis=-1, keepdims=True) + RMS_EPS) * g


def _norm_mod_kernel(x_ref, g_ref, sh_ref, sc_ref, o_ref):
    o_ref[...] = (_rms(x_ref[...], g_ref[...]) * (1.0 + sc_ref[...]) + sh_ref[...]).astype(o_ref.dtype)


def _row_specs(m, tm, rpb):
    del m
    return (lambda tn: pl.BlockSpec((tm, tn), lambda i, j: (i, j)),
            lambda r, tn: pl.BlockSpec((None, r, tn), lambda i, j: ((i * tm) // rpb, 0, j)))


def _norm_mod(x, g, shift, scale, tm, rpb):
    m = x.shape[0]
    r = shift.shape[1]
    rows, mods = _row_specs(m, tm, rpb)
    return pl.pallas_call(
        _norm_mod_kernel,
        grid=(m // tm, 1),
        in_specs=[rows(D_MODEL), pl.BlockSpec((1, D_MODEL), lambda i, j: (0, 0)), mods(r, D_MODEL), mods(r, D_MODEL)],
        out_specs=rows(D_MODEL),
        out_shape=jax.ShapeDtypeStruct((m, D_MODEL), BF),
        compiler_params=_cparams("arbitrary", "arbitrary"),
        name="norm_mod",
    )(x, g.reshape(1, D_MODEL), shift, scale)


def _mm_kernel(a_ref, w_ref, o_ref):
    o_ref[...] = jnp.dot(a_ref[...], w_ref[...], preferred_element_type=F32).astype(o_ref.dtype)


def _matmul(a, w, tm, tn, out_dtype=F32):
    m, k = a.shape
    n = w.shape[1]
    return pl.pallas_call(
        _mm_kernel,
        grid=(m // tm, n // tn),
        in_specs=[pl.BlockSpec((tm, k), lambda i, j: (i, 0)), pl.BlockSpec((k, tn), lambda i, j: (0, j))],
        out_specs=pl.BlockSpec((tm, tn), lambda i, j: (i, j)),
        out_shape=jax.ShapeDtypeStruct((m, n), out_dtype),
        compiler_params=_cparams("arbitrary", "arbitrary"),
        name="matmul",
    )(a, w)


def _merge_kernel(or_ref, on_ref, wr_ref, wn_ref, g0_ref, g1_ref, o_ref):
    yr = jnp.dot(or_ref[...], wr_ref[...], preferred_element_type=F32)
    yn = jnp.dot(on_ref[...], wn_ref[...], preferred_element_type=F32)
    o_ref[...] = (_sigmoid(g0_ref[...]) * yr + _sigmoid(g1_ref[...]) * yn).astype(o_ref.dtype)


def _merge(o_r, o_n, w_r, w_n, p_g, tm):
    m = o_r.shape[0]
    tn = 1024
    nb = D_MODEL // tn
    return pl.pallas_call(
        _merge_kernel,
        grid=(m // tm, nb),
        in_specs=[pl.BlockSpec((tm, C_R), lambda i, j: (i, 0)), pl.BlockSpec((tm, C_N), lambda i, j: (i, 0)),
                  pl.BlockSpec((C_R, tn), lambda i, j: (0, j)), pl.BlockSpec((C_N, tn), lambda i, j: (0, j)),
                  pl.BlockSpec((tm, tn), lambda i, j: (i, j)), pl.BlockSpec((tm, tn), lambda i, j: (i, j + nb))],
        out_specs=pl.BlockSpec((tm, tn), lambda i, j: (i, j)),
        out_shape=jax.ShapeDtypeStruct((m, D_MODEL), BF),
        compiler_params=_cparams("arbitrary", "arbitrary"),
        name="merge_branches",
    )(o_r, o_n, w_r, w_n, p_g, p_g)


def _proj_res_kernel(y_ref, w_ref, x_ref, g_ref, o_ref):
    o_ref[...] = x_ref[...] + g_ref[...] * jnp.dot(y_ref[...], w_ref[...], preferred_element_type=F32)


def _proj_residual(y, w, x, gate, tm, rpb):
    m = y.shape[0]
    tn = 1024
    r = gate.shape[1]
    rows, mods = _row_specs(m, tm, rpb)
    return pl.pallas_call(
        _proj_res_kernel,
        grid=(m // tm, D_MODEL // tn),
        in_specs=[pl.BlockSpec((tm, D_MODEL), lambda i, j: (i, 0)), pl.BlockSpec((D_MODEL, tn), lambda i, j: (0, j)),
                  rows(tn), mods(r, tn)],
        out_specs=rows(tn),
        out_shape=jax.ShapeDtypeStruct((m, D_MODEL), F32),
        compiler_params=_cparams("arbitrary", "arbitrary"),
        name="out_proj_residual",
    )(y, w, x, gate)


def _rwkv_features(p, p_prev, mu, w0, w2, a0, a2, g2, k_k, k_a):
    xm = p + (p_prev - p) * mu
    r = xm[:, :C_R]
    k = xm[:, C_R:2 * C_R]
    v = xm[:, 2 * C_R:3 * C_R]
    o = 3 * C_R
    wd = xm[:, o:o + LORA_W]
    ad = xm[:, o + LORA_W:o + LORA_W + LORA_A]
    gd = xm[:, o + LORA_W + LORA_A:]
    w_log = -_softplus(-(w0 + _dot(jnp.tanh(wd), w2))) - 0.5
    lw = -jnp.exp(w_log)
    a = _sigmoid(a0 + _dot(ad, a2))
    g = _dot(_sigmoid(gd), g2)
    kk = k * k_k
    k = k * (1.0 + (a - 1.0) * k_a)
    return r, k, v, lw, a, g, kk


def _rwkv_head_out(y, r_h, k_h, v_h, g_h, rk_h, lng_h, lnb_h):
    mean = jnp.mean(y, axis=-1, keepdims=True)
    yc = y - mean
    var = jnp.mean(yc * yc, axis=-1, keepdims=True)
    yn = yc * lax.rsqrt(var + LN_X_EPS) * lng_h + lnb_h
    bonus = jnp.sum(r_h * k_h * rk_h, axis=-1, keepdims=True) * v_h
    return (yn + bonus) * g_h


def _rwkv_chunk_kernel(pr_ref, mu_ref, w0_ref, w2_ref, a0_ref, a2_ref, g2_ref, kk_ref, ka_ref, rk_ref,
                       lng_ref, lnb_ref, o_ref, shift_ref, state_ref):
    c = pl.program_id(1)
    C = RW_CHUNK

    @pl.when(c == 0)
    def _():
        shift_ref[...] = jnp.zeros_like(shift_ref)
        state_ref[...] = jnp.zeros_like(state_ref)

    p = pr_ref[...]
    row = lax.broadcasted_iota(I32, (C, 1), 0)
    p_prev = jnp.where(row == 0, shift_ref[...], pltpu.roll(p, 1, axis=0))
    shift_ref[...] = p[C - 1:C, :]
    r, k, v, lw, a, g, kk_all = _rwkv_features(p, p_prev, mu_ref[...], w0_ref[...], w2_ref[...], a0_ref[...],
                                               a2_ref[...], g2_ref[...], kk_ref[...], ka_ref[...])
    cl = lw
    s = 1
    while s < C:
        cl = cl + jnp.where(row >= s, pltpu.roll(cl, s, axis=0), 0.0)
        s *= 2
    ti = lax.broadcasted_iota(I32, (C, C), 0)
    si = lax.broadcasted_iota(I32, (C, C), 1)
    strict = ti > si
    incl = ti >= si
    eye = (ti == si).astype(F32)
    heads = range(H_R)
    sls = [slice(h * HD_R, (h + 1) * HD_R) for h in heads]
    kkt, kh, bh, rt, kbar, bbar, vb, g_end = [], [], [], [], [], [], [], []
    for sl in sls:
        kk_h = kk_all[:, sl]
        nrm = jnp.sqrt(jnp.sum(kk_h * kk_h, axis=-1, keepdims=True))
        kk_h = kk_h / jnp.maximum(nrm, 1e-12)
        b_h = kk_h * a[:, sl]
        cl_h = cl[:, sl]
        cl_end = cl_h[C - 1:C, :]
        e_neg = jnp.exp(-cl_h)
        e_end = jnp.exp(cl_end - cl_h)
        kkt.append((kk_h * jnp.exp(cl_h - lw[:, sl])).astype(BF))
        kh.append((k[:, sl] * e_neg).astype(BF))
        bh.append((b_h * e_neg).astype(BF))
        rt.append((r[:, sl] * jnp.exp(cl_h)).astype(BF))
        kbar.append((k[:, sl] * e_end).astype(BF))
        bbar.append((b_h * e_end).astype(BF))
        vb.append(v[:, sl].astype(BF))
        g_end.append(jnp.exp(cl_end))
    lkk = [jnp.where(strict, _dot_nt(kkt[h], kh[h]), 0.0).astype(BF) for h in heads]
    nil = [jnp.where(strict, -_dot_nt(kkt[h], bh[h]), 0.0) for h in heads]
    grk = [jnp.where(incl, _dot_nt(rt[h], kh[h]), 0.0).astype(BF) for h in heads]
    grb = [jnp.where(incl, _dot_nt(rt[h], bh[h]), 0.0).astype(BF) for h in heads]
    tinv = [eye + n for n in nil]
    m = 2
    while m < C:
        nil = [_dot(n, n) for n in nil]
        tinv = [t + _dot(t, n) for t, n in zip(tinv, nil)]
        m *= 2
    s0 = [state_ref[h] for h in heads]
    x = [_dot_nt(kkt[h], s0[h]) + _dot(lkk[h], vb[h]) for h in heads]
    u = [_dot(tinv[h], x[h]).astype(BF) for h in heads]
    y = [_dot_nt(rt[h], s0[h]) + _dot(grk[h], vb[h]) - _dot(grb[h], u[h]) for h in heads]
    for h in heads:
        state_ref[h] = s0[h] * g_end[h] + _dot_tn(vb[h], kbar[h]) - _dot_tn(u[h], bbar[h])
    for h, sl in enumerate(sls):
        o_ref[:, sl] = _rwkv_head_out(y[h], r[:, sl], k[:, sl], v[:, sl], g[:, sl], rk_ref[:, sl], lng_ref[:, sl],
                                      lnb_ref[:, sl]).astype(o_ref.dtype)


def _rwkv_weights(mu, w0, w2, a0, a2, g2, k_k, k_a, r_k, ln_g, ln_b):
    row = lambda z: z.reshape(1, -1).astype(F32)
    return (row(mu), row(w0), w2.astype(BF), row(a0), a2.astype(BF), g2.astype(BF), row(k_k), row(k_a), row(r_k),
            row(ln_g), row(ln_b))


_RWKV_W_SHAPES = ((1, C_RIN), (1, C_R), (LORA_W, C_R), (1, C_R), (LORA_A, C_R), (LORA_G, C_R), (1, C_R), (1, C_R),
                  (1, C_R), (1, C_R), (1, C_R))


def _rwkv_prompt(pr, rw):
    b, t, _ = pr.shape
    C = RW_CHUNK
    full = lambda shp: pl.BlockSpec(shp, lambda i, j: (0,) * len(shp))
    return pl.pallas_call(
        _rwkv_chunk_kernel,
        grid=(b, t // C),
        in_specs=[pl.BlockSpec((None, C, C_RIN), lambda i, j: (i, j, 0))] + [full(s) for s in _RWKV_W_SHAPES],
        out_specs=[pl.BlockSpec((None, C, C_R), lambda i, j: (i, j, 0)),
                   pl.BlockSpec((None, 1, C_RIN), lambda i, j: (i, 0, 0)),
                   pl.BlockSpec((None, H_R, HD_R, HD_R), lambda i, j: (i, 0, 0, 0))],
        out_shape=[jax.ShapeDtypeStruct((b, t, C_R), BF),
                   jax.ShapeDtypeStruct((b, 1, C_RIN), F32),
                   jax.ShapeDtypeStruct((b, H_R, HD_R, HD_R), F32)],
        compiler_params=_cparams("arbitrary", "arbitrary"),
        name="rwkv_chunk",
    )(pr, *rw)


def _rwkv_step_kernel(pr_ref, prev_ref, s0_ref, mu_ref, w0_ref, w2_ref, a0_ref, a2_ref, g2_ref, kk_ref, ka_ref,
                      rk_ref, lng_ref, lnb_ref, o_ref, state_ref):
    nb = pr_ref.shape[0]
    r, k, v, lw, a, g, kk_all = _rwkv_features(pr_ref[...], prev_ref[...], mu_ref[...], w0_ref[...], w2_ref[...],
                                               a0_ref[...], a2_ref[...], g2_ref[...], kk_ref[...], ka_ref[...])
    decay = jnp.exp(lw)
    ii = lax.broadcasted_iota(I32, (HD_R, HD_R), 0)
    jj = lax.broadcasted_iota(I32, (HD_R, HD_R), 1)
    eye = ii == jj
    col = lambda z: jnp.sum(jnp.where(eye, z, 0.0), axis=1, keepdims=True)
    for bi in range(nb):
        for h in range(H_R):
            sl = slice(h * HD_R, (h + 1) * HD_R)
            rows = lambda z: z[bi:bi + 1, sl]
            r_h, k_h, v_h, a_h, w_h = rows(r), rows(k), rows(v), rows(a), rows(decay)
            kk_h = rows(kk_all)
            kk_h = kk_h / jnp.maximum(jnp.sqrt(jnp.sum(kk_h * kk_h, axis=-1, keepdims=True)), 1e-12)
            b_h = kk_h * a_h
            s0 = s0_ref[bi, h]
            sa = jnp.sum(s0 * (-kk_h), axis=1, keepdims=True)
            s1 = s0 * w_h + sa * b_h + col(v_h) * k_h
            state_ref[bi, h] = s1
            y_col = jnp.sum(s1 * r_h, axis=1, keepdims=True)
            y = jnp.sum(jnp.where(eye, y_col, 0.0), axis=0, keepdims=True)
            o_ref[bi:bi + 1, sl] = _rwkv_head_out(y, r_h, k_h, v_h, rows(g), rk_ref[:, sl], lng_ref[:, sl],
                                                  lnb_ref[:, sl]).astype(o_ref.dtype)


def _rwkv_step(pr, prev, s0, rw):
    b = pr.shape[0]
    return pl.pallas_call(
        _rwkv_step_kernel,
        out_shape=[jax.ShapeDtypeStruct((b, C_R), BF), jax.ShapeDtypeStruct((b, H_R, HD_R, HD_R), F32)],
        compiler_params=pltpu.CompilerParams(vmem_limit_bytes=VMEM_LIMIT_BYTES),
        name="rwkv_step",
    )(pr, prev, s0, *rw)


def _cmp_partial_kernel(*refs):
    x_refs, w_ref, o_ref = refs[:-2], refs[-2], refs[-1]
    x = x_refs[0][...] if len(x_refs) == 1 else jnp.concatenate([r[...] for r in x_refs], axis=0)
    half = N_KVG * HD_N
    for s in range(2):
        acc = jnp.zeros((x.shape[0], N_KVG * 2 * CMP_HID), F32)
        for p in range(STRIDE):
            o = p * KV_COLS + s * half
            acc = acc + _dot(x[:, o:o + half], w_ref[p, s])
        o_ref[:, s * N_KVG * 2 * CMP_HID:(s + 1) * N_KVG * 2 * CMP_HID] = acc


def _cmp_partial_paged_kernel(pt_ref, *refs):
    del pt_ref
    _cmp_partial_kernel(*refs)


def _cmp_weights(cmp_w1):
    w1r = cmp_w1.reshape(2, 2, STRIDE, HD_N, CMP_HID)
    w = jnp.transpose(w1r, (2, 0, 3, 1, 4))
    eye = jnp.eye(N_KVG, dtype=w.dtype)
    wbd = jnp.einsum('gk,psdfe->psgdkfe', eye, w)
    return wbd.reshape(STRIDE, 2, N_KVG * HD_N, N_KVG * 2 * CMP_HID).astype(BF)


def _cmp_partial_rows(x, wbd, tr):
    r = x.shape[0]
    n = 2 * N_KVG * 2 * CMP_HID
    return pl.pallas_call(
        _cmp_partial_kernel,
        grid=(r // tr,),
        in_specs=[pl.BlockSpec((tr, STRIDE * KV_COLS), lambda i: (i, 0)),
                  pl.BlockSpec(wbd.shape, lambda i: (0, 0, 0, 0))],
        out_specs=pl.BlockSpec((tr, n), lambda i: (i, 0)),
        out_shape=jax.ShapeDtypeStruct((r, n), F32),
        compiler_params=_cparams("arbitrary"),
        name="cmp_partial",
    )(x, wbd)


PAGES_PER_STEP = 8


def _cmp_partial_paged(cache, page_table, wbd):
    b, npg = page_table.shape
    cpp = PAGE_SIZE // STRIDE
    n = 2 * N_KVG * 2 * CMP_HID
    steps = npg // PAGES_PER_STEP

    def page_spec(kpg):
        return pl.BlockSpec((None, cpp, STRIDE * KV_COLS), lambda i, j, pt: (pt[i, j * PAGES_PER_STEP + kpg], 0, 0))

    grid_spec = pltpu.PrefetchScalarGridSpec(
        num_scalar_prefetch=1,
        grid=(b, steps),
        in_specs=[page_spec(kpg) for kpg in range(PAGES_PER_STEP)]
        + [pl.BlockSpec(wbd.shape, lambda i, j, pt: (0, 0, 0, 0))],
        out_specs=pl.BlockSpec((None, PAGES_PER_STEP * cpp, n), lambda i, j, pt: (i, j, 0)),
    )
    return pl.pallas_call(
        _cmp_partial_paged_kernel,
        grid_spec=grid_spec,
        out_shape=jax.ShapeDtypeStruct((b, npg * cpp, n), F32),
        compiler_params=_cparams("arbitrary", "arbitrary"),
        name="cmp_partial_paged",
    )(page_table, *([cache] * PAGES_PER_STEP), wbd)


def _cmp_finish_kernel(c_ref, cpos_ref, w2_ref, o_ref):
    c = c_ref[...]
    nrow = c.shape[0]
    c_next = pltpu.roll(c, nrow - 1, axis=0)
    for s in range(2):
        for gi in range(N_KVG):
            o = (s * N_KVG + gi) * 2 * CMP_HID
            hid = (c[:, o:o + CMP_HID] + cpos_ref[0:1, o:o + CMP_HID]
                   + c_next[:, o + CMP_HID:o + 2 * CMP_HID] + cpos_ref[1:2, o + CMP_HID:o + 2 * CMP_HID])
            oo = (s * N_KVG + gi) * HD_N
            o_ref[:, oo:oo + HD_N] = _dot(_gelu_tanh(hid), w2_ref[s])


def _cmp_finish(c, cpos, w2):
    b, nch, n = c.shape
    return pl.pallas_call(
        _cmp_finish_kernel,
        grid=(b,),
        in_specs=[pl.BlockSpec((None, nch, n), lambda i: (i, 0, 0)), pl.BlockSpec(cpos.shape, lambda i: (0, 0)),
                  pl.BlockSpec(w2.shape, lambda i: (0, 0, 0))],
        out_specs=pl.BlockSpec((None, nch, KV_COLS), lambda i: (i, 0, 0)),
        out_shape=jax.ShapeDtypeStruct((b, nch, KV_COLS), F32),
        compiler_params=_cparams("arbitrary"),
        name="cmp_finish",
    )(c, cpos, w2.astype(BF))


def _cmp_pos_rows(cmp_pos):
    pos = cmp_pos.reshape(2, STRIDE, 1, 1, HD_N)
    rows = jnp.broadcast_to(pos, (2, STRIDE, 2, N_KVG, HD_N)).reshape(2, STRIDE * KV_COLS)
    return jnp.concatenate([rows, jnp.zeros((6, STRIDE * KV_COLS), F32)], axis=0)


def _bias_table_kernel(dist_ref, rb_ref, o_ref):
    bucket = _t5_bucket(dist_ref[...])
    for h in range(H_N):
        out = jnp.zeros(bucket.shape, F32)
        for b in range(N_BUCKETS):
            out = jnp.where(bucket == b, rb_ref[b, h], out)
        o_ref[h] = out


def _bias_table(dist, rel_bias, tr):
    r, n = dist.shape
    return pl.pallas_call(
        _bias_table_kernel,
        grid=(r // tr,),
        in_specs=[pl.BlockSpec((tr, n), lambda i: (i, 0)),
                  pl.BlockSpec(memory_space=pltpu.SMEM)],
        out_specs=pl.BlockSpec((H_N, tr, n), lambda i: (0, i, 0)),
        out_shape=jax.ShapeDtypeStruct((H_N, r, n), F32),
        compiler_params=_cparams("arbitrary"),
        name="bias_table",
    )(dist, rel_bias)


def _softmax_rows(logits, valid):
    lm = jnp.where(valid, logits, NEG_INF)
    e = jnp.exp(lm - jnp.max(lm, axis=-1, keepdims=True))
    return e / jnp.sum(e, axis=-1, keepdims=True)


def _cmp_attn_kernel(q_ref, kv_ref, bias_ref, ovt_ref, o_ref, sel_ref):
    tq = q_ref.shape[0]
    ncp = kv_ref.shape[0]
    nsb = ovt_ref.shape[0]
    q0 = pl.program_id(1) * tq
    qpos = q0 + lax.broadcasted_iota(I32, (tq, 1), 0)
    cend = lax.broadcasted_iota(I32, (1, ncp), 1) * STRIDE + (L_CMP - 1)
    valid = (qpos >= cend) & (lax.broadcasted_iota(I32, (1, ncp), 1) < ncp - 1)
    validf = valid.astype(F32)
    q = q_ref[...] * (HD_N ** -0.5)
    blk = lax.broadcasted_iota(I32, (nsb, tq), 0)
    cur = (q0 + lax.broadcasted_iota(I32, (1, tq), 1)) // L_SEL
    forced = (blk == 0) | (blk == cur) | (blk == cur - 1)
    future = blk > cur
    for gi in range(N_KVG):
        kc = kv_ref[:, gi * HD_N:(gi + 1) * HD_N]
        vc = kv_ref[:, (N_KVG + gi) * HD_N:(N_KVG + gi + 1) * HD_N]
        pcs = jnp.zeros((tq, ncp), F32)
        for hl in range(HPG):
            h = gi * HPG + hl
            sl = slice(h * HD_N, (h + 1) * HD_N)
            pc = _softmax_rows(_dot_nt(q[:, sl], kc) + bias_ref[h], valid) * validf
            pcs = pcs + pc
            o_ref[:, sl] = _dot(pc, vc)
        imp = _dot_nt(ovt_ref[...], pcs)
        score = jnp.where(forced, FORCE, jnp.where(future, -FORCE, imp))
        rank = jnp.zeros((nsb, tq), F32)
        for i in range(nsb):
            si = score[i:i + 1, :]
            rank = rank + ((si > score) | ((si == score) & (i < blk))).astype(F32)
        sel_ref[gi] = (rank < N_TOP).astype(sel_ref.dtype)


def _cmp_sel_overlap_t(nc, ncp, nsb):
    s = np.arange(nc)[None, :] * STRIDE
    j = np.arange(nsb)[:, None] * L_SEL
    ov = np.clip(np.minimum(s + L_CMP, j + L_SEL) - np.maximum(s, j), 0, None) / L_CMP
    return np.pad(ov, ((0, 0), (0, ncp - nc))).astype(np.float32)


def _cmp_attn_prompt(p_n, kv_cmp, bias_c, t):
    b = kv_cmp.shape[0]
    ncp = kv_cmp.shape[1]
    nsb = t // L_SEL
    tq = ATT_TILE
    nqt = t // tq
    ovt = jnp.asarray(_cmp_sel_overlap_t(ncp - 1, ncp, nsb), BF)
    return pl.pallas_call(
        _cmp_attn_kernel,
        grid=(b, nqt),
        in_specs=[pl.BlockSpec((tq, C_N), lambda i, j: (i * nqt + j, 0)),
                  pl.BlockSpec((None, ncp, KV_COLS), lambda i, j: (i, 0, 0)),
                  pl.BlockSpec((H_N, tq, ncp), lambda i, j: (0, j, 0)),
                  pl.BlockSpec((nsb, ncp), lambda i, j: (0, 0))],
        out_specs=[pl.BlockSpec((tq, C_N), lambda i, j: (i * nqt + j, 0)),
                   pl.BlockSpec((None, N_KVG, nsb, tq), lambda i, j: (i, 0, 0, j))],
        out_shape=[jax.ShapeDtypeStruct((b * t, C_N), F32), jax.ShapeDtypeStruct((b, N_KVG, nsb, t), BF)],
        compiler_params=_cparams("arbitrary", "arbitrary"),
        name="nsa_cmp_select",
    )(p_n, kv_cmp, bias_c, ovt)


N_BIAS_TILES = 3


def _swa_kernel(q_ref, ks_ref, vs_ref, kw_ref, vw_ref, sel_ref, exp_ref, bias_ref, gn_ref, oc_ref, o_ref):
    tq = q_ref.shape[0]
    tk = ATT_TILE
    qt = pl.program_id(2)
    q = q_ref[...] * (HD_N ** -0.5)
    q4 = jnp.concatenate([q[:, hl * HD_N:(hl + 1) * HD_N] for hl in range(HPG)], axis=0).astype(BF)
    sel_t = sel_ref[...]
    di = lax.broadcasted_iota(I32, (tq, tk), 0) - lax.broadcasted_iota(I32, (tq, tk), 1)

    def attend(k_ref, v_ref, lo, mask_fn):
        def body(ki, carry):
            m, l, acc = carry
            start = pl.multiple_of(ki * tk, tk)
            kt = k_ref[pl.ds(start, tk), :]
            vt = v_ref[pl.ds(start, tk), :]
            off = qt - ki
            s = lax.dot_general(q4, kt, (((1,), (1,)), ((), ())), preferred_element_type=F32)
            s = s + bias_ref[jnp.minimum(off, N_BIAS_TILES - 1)]
            ok = mask_fn(ki, off * tk + di)
            ok4 = jnp.concatenate([ok] * HPG, axis=0)
            s = jnp.where(ok4, s, NEG_INF)
            m_new = jnp.maximum(m, jnp.max(s, axis=-1, keepdims=True))
            alpha = jnp.exp(m - m_new)
            p = jnp.where(ok4, jnp.exp(s - m_new), 0.0)
            l = alpha * l + jnp.sum(p, axis=-1, keepdims=True)
            acc = alpha * acc + jnp.dot(p.astype(BF), vt, preferred_element_type=F32)
            return m_new, l, acc

        init = (jnp.full((HPG * tq, 1), NEG_INF, F32), jnp.zeros((HPG * tq, 1), F32),
                jnp.zeros((HPG * tq, HD_N), F32))
        m, l, acc = lax.fori_loop(lo, qt + 1, body, init)
        return acc / l

    def sel_mask(ki, dist):
        allowed = lax.dot_general(sel_t, exp_ref[ki], (((0,), (0,)), ((), ())), preferred_element_type=F32)
        return (allowed > 0.5) & (dist >= 0)

    def win_mask(ki, dist):
        del ki
        return (dist >= 0) & (dist <= WINDOW)

    o_sel = attend(ks_ref, vs_ref, 0, sel_mask)
    o_win = attend(kw_ref, vw_ref, jnp.maximum(qt - WINDOW // tk, 0), win_mask)
    gates = _sigmoid(gn_ref[...])
    for hl in range(HPG):
        rows = slice(hl * tq, (hl + 1) * tq)
        sl = slice(hl * HD_N, (hl + 1) * HD_N)
        o = (gates[:, 3 * hl:3 * hl + 1] * oc_ref[:, sl] + gates[:, 3 * hl + 1:3 * hl + 2] * o_sel[rows]
             + gates[:, 3 * hl + 2:3 * hl + 3] * o_win[rows])
        o_ref[:, sl] = o.astype(o_ref.dtype)


def _swa_prompt(p_n, kv_sel, kv_win, sel, o_cmp, bias_t, t):
    b = kv_sel.shape[0]
    tq = ATT_TILE
    nqt = t // tq
    nsb = t // L_SEL
    gw = HPG * HD_N
    expand = np.zeros((nqt, nsb, tq), np.float32)
    for ki in range(nqt):
        expand[ki, (ki * tq + np.arange(tq)) // L_SEL, np.arange(tq)] = 1.0
    kv_spec = lambda s: pl.BlockSpec((None, None, None, t, HD_N), lambda i, g, j: (i, s, g, 0, 0))
    return pl.pallas_call(
        _swa_kernel,
        grid=(b, N_KVG, nqt),
        in_specs=[pl.BlockSpec((tq, gw), lambda i, g, j: (i * nqt + j, g)),
                  kv_spec(0), kv_spec(1), kv_spec(0), kv_spec(1),
                  pl.BlockSpec((None, None, nsb, tq), lambda i, g, j: (i, g, 0, j)),
                  pl.BlockSpec((nqt, nsb, tq), lambda i, g, j: (0, 0, 0)),
                  pl.BlockSpec((N_BIAS_TILES, HPG * tq, tq), lambda i, g, j: (0, g, 0)),
                  pl.BlockSpec((tq, GN_GROUP_COLS), lambda i, g, j: (i * nqt + j, NP_GN // GN_GROUP_COLS + g)),
                  pl.BlockSpec((tq, gw), lambda i, g, j: (i * nqt + j, g))],
        out_specs=pl.BlockSpec((tq, gw), lambda i, g, j: (i * nqt + j, g)),
        out_shape=jax.ShapeDtypeStruct((b * t, C_N), BF),
        compiler_params=_cparams("arbitrary", "arbitrary", "arbitrary"),
        name="nsa_sel_win",
    )(p_n, kv_sel, kv_sel, kv_win, kv_win, sel, jnp.asarray(expand, BF), bias_t, p_n, o_cmp)


def _sample_cmp_kernel(q_ref, kv_ref, rbt_ref, ov_ref, o_ref, idx_ref, *, past):
    ncp = kv_ref.shape[0]
    nsbp = ov_ref.shape[1]
    nsb = -(-(past + 1) // L_SEL)
    q = q_ref[...] * (HD_N ** -0.5)
    hrow = lax.broadcasted_iota(I32, (H_N, 1), 0)
    nidx = lax.broadcasted_iota(I32, (1, ncp), 1)
    valid = nidx < ncp - 1
    bias = _bias_rows(past - (nidx * STRIDE + (L_CMP - 1)), rbt_ref[...])
    logits = jnp.zeros((H_N, ncp), F32)
    for gi in range(N_KVG):
        lg = _dot_nt(q, kv_ref[:, gi * HD_N:(gi + 1) * HD_N])
        logits = jnp.where(hrow // HPG == gi, lg, logits)
    pc = _softmax_rows(logits + bias, valid) * valid.astype(F32)
    o = jnp.zeros((H_N, HD_N), F32)
    for gi in range(N_KVG):
        og = _dot(pc, kv_ref[:, (N_KVG + gi) * HD_N:(N_KVG + gi + 1) * HD_N])
        o = jnp.where(hrow // HPG == gi, og, o)
    o_ref[...] = o
    imp_h = _dot(pc, ov_ref[...])
    blk = lax.broadcasted_iota(I32, (8, nsbp), 1)
    grow = lax.broadcasted_iota(I32, (8, 1), 0)
    cur = past // L_SEL
    score = jnp.full((8, nsbp), -3e38, F32)
    for gi in range(N_KVG):
        imp = jnp.sum(jnp.where(hrow // HPG == gi, imp_h, 0.0), axis=0, keepdims=True)
        score = jnp.where(grow == gi, imp, score)
    forced = (blk == 0) | (blk == cur) | (blk == cur - 1)
    score = jnp.where(forced, FORCE, jnp.where(blk > cur, -FORCE, score))
    score = jnp.where((blk < nsb) & (grow < N_KVG), score, -3e38)
    lane = lax.broadcasted_iota(I32, (8, LANES), 1)
    picks = jnp.zeros((8, LANES), I32)
    for it in range(N_TOP):
        mx = jnp.max(score, axis=-1, keepdims=True)
        pick = jnp.min(jnp.where(score == mx, blk, nsbp), axis=-1, keepdims=True)
        picks = jnp.where(lane == it, pick, picks)
        score = jnp.where(blk == pick, -3e38, score)
    idx_ref[...] = picks


def _sample_cmp(q, kv_cmp, rel_bias, past):
    b, ncp, _ = kv_cmp.shape
    nc = ncp - 1
    nsb = -(-(past + 1) // L_SEL)
    nsbp = -(-nsb // LANES) * LANES
    s = np.arange(nc)[:, None] * STRIDE
    j = np.arange(nsb)[None, :] * L_SEL
    ov = np.clip(np.minimum(s + L_CMP, j + L_SEL) - np.maximum(s, j), 0, None) / L_CMP
    ov = np.pad(ov, ((0, ncp - nc), (0, nsbp - nsb))).astype(np.float32)
    return pl.pallas_call(
        functools.partial(_sample_cmp_kernel, past=past),
        grid=(b,),
        in_specs=[pl.BlockSpec((None, H_N, HD_N), lambda i: (i, 0, 0)),
                  pl.BlockSpec((None, ncp, KV_COLS), lambda i: (i, 0, 0)),
                  pl.BlockSpec((H_N, N_BUCKETS), lambda i: (0, 0)),
                  pl.BlockSpec((ncp, nsbp), lambda i: (0, 0))],
        out_specs=[pl.BlockSpec((None, H_N, HD_N), lambda i: (i, 0, 0)),
                   pl.BlockSpec((None, 8, LANES), lambda i: (i, 0, 0))],
        out_shape=[jax.ShapeDtypeStruct((b, H_N, HD_N), F32), jax.ShapeDtypeStruct((b, 8, LANES), I32)],
        compiler_params=_cparams("arbitrary"),
        name="nsa_sample_cmp_select",
    )(q, kv_cmp, rel_bias.T, jnp.asarray(ov, BF))


def _block_copy_kernel(pg_ref, hf_ref, x_ref, o_ref):
    del pg_ref, hf_ref
    o_ref[...] = x_ref[...]


def _gather_sel_blocks(cache, page, half):
    n = page.shape[0]
    blk = (None, L_SEL, 2, N_KVG, HD_N)
    return pl.pallas_call(
        _block_copy_kernel,
        grid_spec=pltpu.PrefetchScalarGridSpec(
            num_scalar_prefetch=2, grid=(n,),
            in_specs=[pl.BlockSpec(blk, lambda i, pg, hf: (pg[i], hf[i], 0, 0, 0))],
            out_specs=pl.BlockSpec(blk, lambda i, pg, hf: (i, 0, 0, 0, 0))),
        out_shape=jax.ShapeDtypeStruct((n, L_SEL, 2, N_KVG, HD_N), cache.dtype),
        compiler_params=_cparams("arbitrary"),
        name="gather_sel_blocks",
    )(page, half, cache)


def _sample_swa_kernel(idx_ref, q_ref, blk_ref, win_ref, new_ref, rbt_ref, gate_ref, oc_ref, o_ref, *, past):
    bi = pl.program_id(0)
    q = q_ref[...] * (HD_N ** -0.5)
    rbt = rbt_ref[...]
    hrow = lax.broadcasted_iota(I32, (H_N, 1), 0)
    nk = N_TOP * L_SEL
    lane = lax.broadcasted_iota(I32, (1, nk), 1)
    new_blk = past // L_SEL
    bias_new = rbt[:, 0:1]
    gates = _sigmoid(gate_ref[...])
    nwin = win_ref.shape[0]
    wdist = nwin - lax.broadcasted_iota(I32, (1, nwin), 1)
    wbias = _bias_rows(wdist, rbt)
    wvalid = (wdist >= 0) & (wdist <= WINDOW)

    def with_new(logits, valid, vals, k_new, v_new):
        l_new = jnp.sum(q * k_new, axis=-1, keepdims=True) + bias_new
        lm = jnp.where(valid, logits, NEG_INF)
        m = jnp.maximum(jnp.max(lm, axis=-1, keepdims=True), l_new)
        e = jnp.where(valid, jnp.exp(lm - m), 0.0)
        e_new = jnp.exp(l_new - m)
        den = jnp.sum(e, axis=-1, keepdims=True) + e_new
        return (_dot(e, vals) + e_new * v_new) / den

    o_sel = jnp.zeros((H_N, HD_N), F32)
    o_win = jnp.zeros((H_N, HD_N), F32)
    for gi in range(N_KVG):
        ksl = slice(gi * HD_N, (gi + 1) * HD_N)
        vsl = slice((N_KVG + gi) * HD_N, (N_KVG + gi + 1) * HD_N)
        bid = jnp.zeros((1, nk), I32)
        for n in range(N_TOP):
            bid = jnp.where(lane // L_SEL == n, idx_ref[bi, gi, n], bid)
        dist = past - (bid * L_SEL + lane % L_SEL)
        valid = (bid != new_blk) & (dist >= 0)
        kv = blk_ref[gi]
        logits = _dot_nt(q, kv[:, ksl]) + _bias_rows(dist, rbt)
        og = with_new(logits, valid, kv[:, vsl], new_ref[0:1, ksl], new_ref[0:1, vsl])
        o_sel = jnp.where(hrow // HPG == gi, og, o_sel)
        logits = _dot_nt(q, win_ref[:, ksl]) + wbias
        og = with_new(logits, wvalid, win_ref[:, vsl], new_ref[1:2, ksl], new_ref[1:2, vsl])
        o_win = jnp.where(hrow // HPG == gi, og, o_win)
    o_ref[...] = gates[:, 0:1] * oc_ref[...] + gates[:, 1:2] * o_sel + gates[:, 2:3] * o_win


def _sample_swa(idx, q, blocks, win, new_kv, rel_bias, gates, o_cmp, past):
    b = q.shape[0]
    nk = N_TOP * L_SEL
    w = win.shape[1]
    grid_spec = pltpu.PrefetchScalarGridSpec(
        num_scalar_prefetch=1, grid=(b,),
        in_specs=[pl.BlockSpec((None, H_N, HD_N), lambda i, ix: (i, 0, 0)),
                  pl.BlockSpec((None, N_KVG, nk, KV_COLS), lambda i, ix: (i, 0, 0, 0)),
                  pl.BlockSpec((None, w, KV_COLS), lambda i, ix: (i, 0, 0)),
                  pl.BlockSpec((None, 2, KV_COLS), lambda i, ix: (i, 0, 0)),
                  pl.BlockSpec((H_N, N_BUCKETS), lambda i, ix: (0, 0)),
                  pl.BlockSpec((None, H_N, 3), lambda i, ix: (i, 0, 0)),
                  pl.BlockSpec((None, H_N, HD_N), lambda i, ix: (i, 0, 0))],
        out_specs=pl.BlockSpec((None, H_N, HD_N), lambda i, ix: (i, 0, 0)))
    return pl.pallas_call(
        functools.partial(_sample_swa_kernel, past=past),
        grid_spec=grid_spec,
        out_shape=jax.ShapeDtypeStruct((b, H_N, HD_N), F32),
        compiler_params=_cparams("arbitrary"),
        name="nsa_sample_sel_win",
    )(idx, q, blocks, win, new_kv, rel_bias.T, gates, o_cmp)


ROUTER_COLS = LANES


def _router_kernel(x_ref, g_ref, sh_ref, sc_ref, w_ref, b_ref, h_ref, e_ref, wt_ref, rk_ref, cnt_ref):
    i = pl.program_id(0)
    tm = x_ref.shape[0]

    @pl.when(i == 0)
    def _():
        cnt_ref[...] = jnp.zeros_like(cnt_ref)

    h = (_rms(x_ref[...], g_ref[...]) * (1.0 + sc_ref[...]) + sh_ref[...]).astype(BF)
    h_ref[:, 0, :] = h
    logits = jnp.dot(h, w_ref[...], preferred_element_type=F32) + b_ref[...]
    lane = lax.broadcasted_iota(I32, (tm, ROUTER_COLS), 1)

    def top1(vals, ok):
        vm = jnp.where(ok, vals, -3e38)
        mx = jnp.max(vm, axis=-1, keepdims=True)
        return mx, jnp.min(jnp.where(ok & (vm == mx), lane, ROUTER_COLS), axis=-1, keepdims=True)

    isg = lane < N_EGROUPS
    pg = _softmax_rows(logits, isg)
    g_w, g_i = top1(pg, isg)
    ise = (lane >= N_EGROUPS) & ((lane - N_EGROUPS) // EXP_PER_GROUP == g_i)
    pe = _softmax_rows(logits, ise)
    w0, l0 = top1(pe, ise)
    w1, l1 = top1(pe, ise & (lane != l0))
    den = w0 + w1
    e0 = l0 - N_EGROUPS
    e1 = l1 - N_EGROUPS
    e_ref[...] = jnp.where(lane == 0, e0, jnp.where(lane == 1, e1, 0))
    wt_ref[...] = jnp.where(lane == 0, w0 / den * g_w, jnp.where(lane == 1, w1 / den * g_w, 0.0))
    oh0 = (lane == e0).astype(F32)
    oh1 = (lane == e1).astype(F32)
    cnt = oh0 + oh1
    ti = lax.broadcasted_iota(I32, (tm, tm), 0)
    si = lax.broadcasted_iota(I32, (tm, tm), 1)
    before = _dot((ti > si).astype(F32), cnt) + cnt_ref[...]
    r0 = jnp.sum(before * oh0, axis=-1, keepdims=True)
    r1 = jnp.sum(before * oh1, axis=-1, keepdims=True)
    rk_ref[...] = jnp.where(lane == 0, r0, jnp.where(lane == 1, r1, 0.0)).astype(I32)
    cnt_ref[...] = cnt_ref[...] + jnp.sum(cnt, axis=0, keepdims=True)


def _router(x, g, shift, scale, w_r, b_r, tm, rpb):
    m = x.shape[0]
    r = shift.shape[1]
    rows = lambda tn: pl.BlockSpec((tm, tn), lambda i: (i, 0))
    mods = pl.BlockSpec((None, r, D_MODEL), lambda i: ((i * tm) // rpb, 0, 0))
    small = lambda dt: jax.ShapeDtypeStruct((m, ROUTER_COLS), dt)
    return pl.pallas_call(
        _router_kernel,
        grid=(m // tm,),
        in_specs=[rows(D_MODEL), pl.BlockSpec((1, D_MODEL), lambda i: (0, 0)), mods, mods,
                  pl.BlockSpec((D_MODEL, ROUTER_COLS), lambda i: (0, 0)),
                  pl.BlockSpec((1, ROUTER_COLS), lambda i: (0, 0))],
        out_specs=[pl.BlockSpec((tm, 1, D_MODEL), lambda i: (i, 0, 0)),
                   rows(ROUTER_COLS), rows(ROUTER_COLS), rows(ROUTER_COLS),
                   pl.BlockSpec((1, ROUTER_COLS), lambda i: (0, 0))],
        out_shape=[jax.ShapeDtypeStruct((m, 1, D_MODEL), BF), small(I32), small(F32), small(I32),
                   jax.ShapeDtypeStruct((1, ROUTER_COLS), F32)],
        compiler_params=_cparams("arbitrary"),
        name="moe_router",
    )(x, g.reshape(1, D_MODEL), shift, scale, w_r, b_r)


def _gather_rows_to_vmem(src_hbm, dst_vmem, sem, n, src_row, dst_row):
    def copy(i):
        return pltpu.make_async_copy(src_hbm.at[src_row(i)], dst_vmem.at[dst_row(i)], sem)

    def issue(i, c):
        copy(i).start()
        return c

    def drain(i, c):
        copy(i).wait()
        return c

    lax.fori_loop(0, n, issue, 0)
    lax.fori_loop(0, n, drain, 0)


def _expert_kernel(be_ref, rt_ref, h_hbm, w1_ref, w3_ref, w2_ref, o_ref, xbuf, sem):
    del be_ref
    blk = xbuf.shape[0]
    _gather_rows_to_vmem(h_hbm, xbuf, sem, blk, lambda r: rt_ref[0, r], lambda r: r)
    x = xbuf[:, 0, :]
    a = jnp.dot(x, w1_ref[...].astype(BF), preferred_element_type=F32)
    b = jnp.dot(x, w3_ref[...].astype(BF), preferred_element_type=F32)
    hid = a * _sigmoid(a) * b
    o_ref[:, 0, :] = _dot(hid, w2_ref[...])


def _experts(h2, row_tok, blk_exp, w1, w3, w2):
    nblk, _, blk = row_tok.shape
    grid_spec = pltpu.PrefetchScalarGridSpec(
        num_scalar_prefetch=1, grid=(nblk,),
        in_specs=[pl.BlockSpec((None, 1, blk), lambda i, be: (i, 0, 0), memory_space=pltpu.SMEM),
                  pl.BlockSpec(memory_space=pl.ANY),
                  pl.BlockSpec((None, D_MODEL, D_EXP), lambda i, be: (be[i], 0, 0)),
                  pl.BlockSpec((None, D_MODEL, D_EXP), lambda i, be: (be[i], 0, 0)),
                  pl.BlockSpec((None, D_EXP, D_MODEL), lambda i, be: (be[i], 0, 0))],
        out_specs=pl.BlockSpec((blk, 1, D_MODEL), lambda i, be: (i, 0, 0)),
        scratch_shapes=[pltpu.VMEM((blk, 1, D_MODEL), BF), pltpu.SemaphoreType.DMA(())])
    return pl.pallas_call(
        _expert_kernel,
        grid_spec=grid_spec,
        out_shape=jax.ShapeDtypeStruct((nblk * blk, 1, D_MODEL), F32),
        compiler_params=_cparams("arbitrary"),
        name="moe_experts",
    )(blk_exp, row_tok, h2, w1, w3, w2)


def _final_kernel(x_ref, g_ref, dest_ref, ys_hbm, wt_ref, nf_ref, o_ref, ybuf, sem):
    tm = x_ref.shape[0]
    for k in range(TOP_K):
        _gather_rows_to_vmem(ys_hbm, ybuf.at[k], sem, tm, lambda r, k=k: dest_ref[k, r], lambda r: r)
    wt = wt_ref[...]
    moe = wt[:, 0:1] * ybuf[0, :, 0, :] + wt[:, 1:2] * ybuf[1, :, 0, :]
    o_ref[...] = _rms(x_ref[...] + g_ref[...] * moe, nf_ref[...])


def _final(x, gate, ys, dest, wts, norm_f, tm, rpb):
    m = x.shape[0]
    r = gate.shape[1]
    rows = lambda tn: pl.BlockSpec((tm, tn), lambda i: (i, 0))
    return pl.pallas_call(
        _final_kernel,
        grid=(m // tm,),
        in_specs=[rows(D_MODEL), pl.BlockSpec((None, r, D_MODEL), lambda i: ((i * tm) // rpb, 0, 0)),
                  pl.BlockSpec((None, TOP_K, tm), lambda i: (i, 0, 0), memory_space=pltpu.SMEM),
                  pl.BlockSpec(memory_space=pl.ANY),
                  rows(ROUTER_COLS), pl.BlockSpec((1, D_MODEL), lambda i: (0, 0))],
        out_specs=rows(D_MODEL),
        out_shape=jax.ShapeDtypeStruct((m, D_MODEL), F32),
        scratch_shapes=[pltpu.VMEM((TOP_K, tm, 1, D_MODEL), F32), pltpu.SemaphoreType.DMA(())],
        compiler_params=_cparams("arbitrary"),
        name="moe_combine_final_norm",
    )(x, gate, dest, ys, wts, norm_f.reshape(1, D_MODEL))


def _moe_and_final(x1, g2, shift, scale, gate, w_r, b_r, exp_w1, exp_w3, exp_w2, norm_f, tm, rpb, blk):
    m = x1.shape[0]
    h2, eid, wts, rank, counts = _router(x1, g2, shift, scale, w_r, b_r, tm, rpb)
    counts = counts[0, :N_EXP].astype(I32)
    padded = (counts + blk - 1) // blk * blk
    pend = jnp.cumsum(padded)
    pstart = pend - padded
    n_blocks = -(-(m * TOP_K) // blk) + N_EXP
    blk_exp = jnp.minimum(jnp.searchsorted(pend, jnp.arange(n_blocks, dtype=I32) * blk, side='right'),
                          N_EXP - 1).astype(I32)
    e = eid[:, :TOP_K]
    dest = pstart[e] + rank[:, :TOP_K]
    tok = jnp.broadcast_to(jnp.arange(m, dtype=I32)[:, None], (m, TOP_K))
    row_tok = jnp.zeros((n_blocks * blk,), I32).at[dest.reshape(-1)].set(tok.reshape(-1))
    ys = _experts(h2, row_tok.reshape(n_blocks, 1, blk), blk_exp, exp_w1, exp_w3, exp_w2)
    dest_t = jnp.transpose(dest.reshape(m // tm, tm, TOP_K), (0, 2, 1))
    return _final(x1, gate, ys, dest_t, wts, norm_f, tm, rpb)


def _pack_in_proj(w_in):
    o = C_RIN
    w_r = w_in[:, :o]
    w_q = w_in[:, o:o + C_N + 3 * KV_COLS]
    o += C_N + 3 * KV_COLS
    w_gn = w_in[:, o:o + 3 * H_N].reshape(D_MODEL, N_KVG, 3 * HPG)
    w_gn = jnp.pad(w_gn, ((0, 0), (0, 0), (0, GN_GROUP_COLS - 3 * HPG))).reshape(D_MODEL, N_KVG * GN_GROUP_COLS)
    o += 3 * H_N
    w_gm = w_in[:, o:]
    return w_r.astype(BF), jnp.concatenate([w_q, w_gn], axis=1).astype(BF), w_gm.astype(BF)


def _heads_major(kv, b, t):
    return jnp.transpose(kv.reshape(b, t, 2, N_KVG, HD_N), (0, 2, 3, 1, 4)).astype(BF)


def _prompt_bias_tables(rel_bias, t):
    tq = ATT_TILE
    i = np.arange(tq)[:, None]
    j = np.arange(tq)[None, :]
    dist = np.concatenate([d * tq + i - j for d in range(N_BIAS_TILES)], axis=0).astype(np.int32)
    tiles = _bias_table(jnp.asarray(dist), rel_bias, tq)
    tiles = jnp.transpose(tiles.reshape(H_N, N_BIAS_TILES, tq, tq), (1, 0, 2, 3)).reshape(N_BIAS_TILES, H_N * tq, tq)
    nc = (t - L_CMP) // STRIDE + 1
    ncp = nc + 1
    dc = (np.arange(t)[:, None] - (np.arange(ncp)[None, :] * STRIDE + L_CMP - 1)).astype(np.int32)
    return tiles, _bias_table(jnp.asarray(dc), rel_bias, tq)


def kernel(x_prompt, x_sample, c_prompt, c_sample, cache_cmp_kv, cache_sel_kv, state_win_kv, state_rwkv_shift,
           state_rwkv_wkv, page_table, rel_bias, norm_f, norm1, norm2, w_ada, b_ada, w_in, rwkv_mu, rwkv_w0, rwkv_w2,
           rwkv_a0, rwkv_a2, rwkv_g2, rwkv_kk, rwkv_ka, rwkv_rk, rwkv_ln_g, rwkv_ln_b, cmp_pos, cmp_w1, cmp_w2,
           w_o_rwkv, w_o_nsa, w_out, router_wg, router_bg, router_we, router_be, exp_w1, exp_w3, exp_w2):
    bp, t, _ = x_prompt.shape
    bs = x_sample.shape[0]
    mp = bp * t
    past = page_table.shape[1] * PAGE_SIZE

    nrow = -(-(bp + bs) // 8) * 8
    c_all = jnp.concatenate([c_prompt, c_sample, jnp.zeros((nrow - bp - bs, D_MODEL), F32)], axis=0)
    mod = _ada(c_all, w_ada[0], b_ada[0]).reshape(nrow, 6, D_MODEL)
    mod_p = [mod[:bp, i][:, None, :] for i in range(6)]
    mod_s = [mod[bp:bp + bs, i][None] for i in range(6)]

    w_r, w_n, w_gm = _pack_in_proj(w_in[0])
    rw = _rwkv_weights(rwkv_mu[0], rwkv_w0[0], rwkv_w2[0], rwkv_a0[0], rwkv_a2[0], rwkv_g2[0], rwkv_kk[0],
                       rwkv_ka[0], rwkv_rk[0], rwkv_ln_g[0], rwkv_ln_b[0])
    wbd = _cmp_weights(cmp_w1[0])
    cpos = _cmp_partial_rows(_cmp_pos_rows(cmp_pos[0]), wbd, 8)
    wo_r, wo_n, wo = w_o_rwkv[0].astype(BF), w_o_nsa[0].astype(BF), w_out[0].astype(BF)
    w_router = jnp.pad(jnp.concatenate([router_wg[0], router_we[0]], axis=1),
                       ((0, 0), (0, ROUTER_COLS - N_EGROUPS - N_EXP))).astype(BF)
    b_router = jnp.pad(jnp.concatenate([router_bg[0], router_be[0]]), (0, ROUTER_COLS - N_EGROUPS - N_EXP))[None]

    tm = 512
    xp = x_prompt.reshape(mp, D_MODEL)
    h = _norm_mod(xp, norm1[0], mod_p[0], mod_p[1], tm, t)
    p_r = _matmul(h, w_r, tm, C_RIN // 2)
    p_n = _matmul(h, w_n, tm, NP_COLS // 2)
    p_g = _matmul(h, w_gm, tm, 2048)
    o_r, shift_p, wkv_p = _rwkv_prompt(p_r.reshape(bp, t, C_RIN), rw)
    kvc = p_n[:, NP_KVC:NP_KVC + KV_COLS]
    kvs = p_n[:, NP_KVS:NP_KVS + KV_COLS]
    kvw = p_n[:, NP_KVW:NP_KVW + KV_COLS]
    nch = t // STRIDE
    c_part = _cmp_partial_rows(kvc.reshape(bp * nch, STRIDE * KV_COLS), wbd, nch)
    kv_cmp = _cmp_finish(c_part.reshape(bp, nch, -1), cpos, cmp_w2[0])
    bias_t, bias_c = _prompt_bias_tables(rel_bias, t)
    o_cmp, sel = _cmp_attn_prompt(p_n, kv_cmp, bias_c, t)
    o_n = _swa_prompt(p_n, _heads_major(kvs, bp, t), _heads_major(kvw, bp, t), sel, o_cmp, bias_t, t)
    y = _merge(o_r.reshape(mp, C_R), o_n, wo_r, wo_n, p_g, tm)
    x1 = _proj_residual(y, wo, xp, mod_p[2], tm, t)
    y_prompt = _moe_and_final(x1, norm2[0], mod_p[3], mod_p[4], mod_p[5], w_router, b_router, exp_w1[0], exp_w3[0],
                              exp_w2[0], norm_f, tm, t, 128).reshape(bp, t, D_MODEL)
    kv_shape = (1, bp, t, 2, N_KVG, HD_N)
    wlen = min(WINDOW, t)
    win_p = kvw.reshape(bp, t, KV_COLS)[:, t - wlen:].reshape(1, bp, wlen, 2, N_KVG, HD_N)

    xs = x_sample.reshape(bs, D_MODEL)
    hs = _norm_mod(xs, norm1[0], mod_s[0], mod_s[1], bs, bs)
    ps_r = _matmul(hs, w_r, bs, C_RIN // 2)
    ps_n = _matmul(hs, w_n, bs, NP_COLS // 2)
    ps_g = _matmul(hs, w_gm, bs, 2048)
    os_r, wkv_s = _rwkv_step(ps_r, state_rwkv_shift[0], state_rwkv_wkv[0], rw)
    kvc_s = ps_n[:, NP_KVC:NP_KVC + KV_COLS]
    kvs_s = ps_n[:, NP_KVS:NP_KVS + KV_COLS]
    kvw_s = ps_n[:, NP_KVW:NP_KVW + KV_COLS]
    n_pool = cache_cmp_kv.shape[1]
    cpp = PAGE_SIZE // STRIDE
    cs_part = _cmp_partial_paged(cache_cmp_kv[0].reshape(n_pool, cpp, STRIDE * KV_COLS), page_table, wbd)
    kv_cmp_s = _cmp_finish(cs_part, cpos, cmp_w2[0])
    q_s = ps_n[:, :C_N].reshape(bs, H_N, HD_N)
    o_cmp_s, picks = _sample_cmp(q_s, kv_cmp_s, rel_bias, past)
    idx = picks[:, :N_KVG, :N_TOP]
    bpp = PAGE_SIZE // L_SEL
    npb = past // L_SEL
    idc = jnp.minimum(idx, npb - 1)
    page = jnp.take_along_axis(page_table, (idc // bpp).reshape(bs, -1), axis=1).reshape(-1)
    blocks = _gather_sel_blocks(cache_sel_kv[0], page, (idc % bpp).reshape(-1))
    blocks = blocks.reshape(bs, N_KVG, N_TOP * L_SEL, KV_COLS)
    win_buf = state_win_kv[0].reshape(bs, -1, KV_COLS)
    gates_s = ps_n[:, NP_GN:].reshape(bs, N_KVG, GN_GROUP_COLS)[:, :, :3 * HPG].reshape(bs, H_N, 3)
    new_kv = jnp.stack([kvs_s, kvw_s], axis=1)
    os_n = _sample_swa(idx, q_s, blocks, win_buf, new_kv, rel_bias, gates_s, o_cmp_s, past)
    ys = _merge(os_r, os_n.reshape(bs, C_N).astype(BF), wo_r, wo_n, ps_g, bs)
    xs1 = _proj_residual(ys, wo, xs, mod_s[2], bs, bs)
    y_sample = _moe_and_final(xs1, norm2[0], mod_s[3], mod_s[4], mod_s[5], w_router, b_router, exp_w1[0], exp_w3[0],
                              exp_w2[0], norm_f, bs, bs, 16).reshape(bs, 1, D_MODEL)
    kv1 = (1, bs, 1, 2, N_KVG, HD_N)
    wbuf = win_buf.shape[1]
    win_s = jnp.concatenate([win_buf, kvw_s[:, None, :]], axis=1)[:, -wbuf:].reshape(1, bs, wbuf, 2, N_KVG, HD_N)

    return (y_prompt, y_sample,
            kvc.reshape(kv_shape), kvc_s.reshape(kv1),
            kvs.reshape(kv_shape), kvs_s.reshape(kv1),
            win_p, win_s,
            shift_p.reshape(1, bp, C_RIN), ps_r.reshape(1, bs, C_RIN),
            wkv_p[None], wkv_s[None])
```

```python
import functools
import math

import numpy as np
import jax
import jax.numpy as jnp
from jax import lax
from jax.experimental import pallas as pl
from jax.experimental.pallas import tpu as pltpu

D_MODEL = 2048
PAGE_SIZE = 128
H_R, HD_R = 16, 64
C_R = H_R * HD_R
LORA_W, LORA_A, LORA_G = 64, 64, 128
C_RIN = 3 * C_R + LORA_W + LORA_A + LORA_G
LN_X_EPS = 64e-5
H_N, HD_N, N_KVG = 16, 64, 4
HPG = H_N // N_KVG
C_N = H_N * HD_N
KV_COLS = 2 * N_KVG * HD_N
L_CMP, STRIDE, CMP_HID = 32, 16, 64
L_SEL, N_TOP, WINDOW = 64, 16, 512
N_BUCKETS, MAX_DIST = 32, 128
N_EGROUPS, EXP_PER_GROUP = 4, 8
N_EXP = N_EGROUPS * EXP_PER_GROUP
TOP_K, D_EXP = 2, 512
RMS_EPS = 1e-6
NEG_INF = -1e30
FORCE = 1e9

BF = jnp.bfloat16
F32 = jnp.float32
I32 = jnp.int32

VMEM_LIMIT_BYTES = 56 * 1024 * 1024
LANES = 128
RW_CHUNK = 32
ATT_TILE = 256
GN_GROUP_COLS = 128
NP_Q, NP_KVC, NP_KVS, NP_KVW, NP_GN = 0, C_N, C_N + KV_COLS, C_N + 2 * KV_COLS, C_N + 3 * KV_COLS
NP_COLS = NP_GN + N_KVG * GN_GROUP_COLS


def _cparams(*sem):
    return pltpu.CompilerParams(dimension_semantics=sem, vmem_limit_bytes=VMEM_LIMIT_BYTES)


def _dot(a, b):
    return jnp.dot(a.astype(BF), b.astype(BF), preferred_element_type=F32)


def _dot_nt(a, b):
    return lax.dot_general(a.astype(BF), b.astype(BF), (((1,), (1,)), ((), ())), preferred_element_type=F32)


def _dot_tn(a, b):
    return lax.dot_general(a.astype(BF), b.astype(BF), (((0,), (0,)), ((), ())), preferred_element_type=F32)


def _softplus(x):
    return jnp.maximum(x, 0.0) + jnp.log1p(jnp.exp(-jnp.abs(x)))


def _sigmoid(x):
    return 1.0 / (1.0 + jnp.exp(-x))


def _gelu_tanh(x):
    return 0.5 * x * (1.0 + jnp.tanh(math.sqrt(2.0 / math.pi) * (x + 0.044715 * x * x * x)))


def _t5_bucket(dist):
    n = jnp.maximum(dist, 0)
    max_exact = N_BUCKETS // 2
    nf = jnp.maximum(n, 1).astype(F32)
    large = max_exact + (jnp.log(nf / max_exact) / math.log(MAX_DIST / max_exact)
                         * (N_BUCKETS - max_exact)).astype(I32)
    large = jnp.minimum(large, N_BUCKETS - 1)
    return jnp.where(n < max_exact, n, large)


def _bias_rows(dist, rbt):
    bucket = _t5_bucket(dist)
    out = jnp.zeros((rbt.shape[0], dist.shape[1]), F32)
    for b in range(N_BUCKETS):
        out = jnp.where(bucket == b, rbt[:, b:b + 1], out)
    return out


def _ada_kernel(c_ref, w_ref, b_ref, o_ref):
    o_ref[...] = _dot(c_ref[...], w_ref[...]) + b_ref[...]


def _ada(c, w_ada, b_ada):
    r = c.shape[0]
    n = w_ada.shape[1]
    tn = 1024
    return pl.pallas_call(
        _ada_kernel,
        grid=(n // tn,),
        in_specs=[pl.BlockSpec((r, D_MODEL), lambda j: (0, 0)),
                  pl.BlockSpec((D_MODEL, tn), lambda j: (0, j)),
                  pl.BlockSpec((1, tn), lambda j: (0, j))],
        out_specs=pl.BlockSpec((r, tn), lambda j: (0, j)),
        out_shape=jax.ShapeDtypeStruct((r, n), F32),
        compiler_params=_cparams("arbitrary"),
        name="ada_mod",
    )(c, w_ada, b_ada.reshape(1, n))


def _rms(x, g):
    return x * lax.rsqrt(jnp.mean(x * x, axis=-1, keepdims=True) + RMS_EPS) * g


def _norm_mod_kernel(x_ref, g_ref, sh_ref, sc_ref, o_ref):
    o_ref[...] = (_rms(x_ref[...], g_ref[...]) * (1.0 + sc_ref[...]) + sh_ref[...]).astype(o_ref.dtype)


def _row_specs(m, tm, rpb):
    del m
    return (lambda tn: pl.BlockSpec((tm, tn), lambda i, j: (i, j)),
            lambda r, tn: pl.BlockSpec((None, r, tn), lambda i, j: ((i * tm) // rpb, 0, j)))


def _norm_mod(x, g, shift, scale, tm, rpb):
    m = x.shape[0]
    r = shift.shape[1]
    rows, mods = _row_specs(m, tm, rpb)
    return pl.pallas_call(
        _norm_mod_kernel,
        grid=(m // tm, 1),
        in_specs=[rows(D_MODEL), pl.BlockSpec((1, D_MODEL), lambda i, j: (0, 0)), mods(r, D_MODEL), mods(r, D_MODEL)],
        out_specs=rows(D_MODEL),
        out_shape=jax.ShapeDtypeStruct((m, D_MODEL), BF),
        compiler_params=_cparams("arbitrary", "arbitrary"),
        name="norm_mod",
    )(x, g.reshape(1, D_MODEL), shift, scale)


def _mm_kernel(a_ref, w_ref, o_ref):
    o_ref[...] = jnp.dot(a_ref[...], w_ref[...], preferred_element_type=F32).astype(o_ref.dtype)


def _matmul(a, w, tm, tn, out_dtype=F32):
    m, k = a.shape
    n = w.shape[1]
    return pl.pallas_call(
        _mm_kernel,
        grid=(m // tm, n // tn),
        in_specs=[pl.BlockSpec((tm, k), lambda i, j: (i, 0)), pl.BlockSpec((k, tn), lambda i, j: (0, j))],
        out_specs=pl.BlockSpec((tm, tn), lambda i, j: (i, j)),
        out_shape=jax.ShapeDtypeStruct((m, n), out_dtype),
        compiler_params=_cparams("arbitrary", "arbitrary"),
        name="matmul",
    )(a, w)


def _merge_kernel(or_ref, on_ref, wr_ref, wn_ref, g0_ref, g1_ref, o_ref):
    yr = jnp.dot(or_ref[...], wr_ref[...], preferred_element_type=F32)
    yn = jnp.dot(on_ref[...], wn_ref[...], preferred_element_type=F32)
    o_ref[...] = (_sigmoid(g0_ref[...]) * yr + _sigmoid(g1_ref[...]) * yn).astype(o_ref.dtype)


def _merge(o_r, o_n, w_r, w_n, p_g, tm):
    m = o_r.shape[0]
    tn = 1024
    nb = D_MODEL // tn
    return pl.pallas_call(
        _merge_kernel,
        grid=(m // tm, nb),
        in_specs=[pl.BlockSpec((tm, C_R), lambda i, j: (i, 0)), pl.BlockSpec((tm, C_N), lambda i, j: (i, 0)),
                  pl.BlockSpec((C_R, tn), lambda i, j: (0, j)), pl.BlockSpec((C_N, tn), lambda i, j: (0, j)),
                  pl.BlockSpec((tm, tn), lambda i, j: (i, j)), pl.BlockSpec((tm, tn), lambda i, j: (i, j + nb))],
        out_specs=pl.BlockSpec((tm, tn), lambda i, j: (i, j)),
        out_shape=jax.ShapeDtypeStruct((m, D_MODEL), BF),
        compiler_params=_cparams("arbitrary", "arbitrary"),
        name="merge_branches",
    )(o_r, o_n, w_r, w_n, p_g, p_g)


def _proj_res_kernel(y_ref, w_ref, x_ref, g_ref, o_ref):
    o_ref[...] = x_ref[...] + g_ref[...] * jnp.dot(y_ref[...], w_ref[...], preferred_element_type=F32)


def _proj_residual(y, w, x, gate, tm, rpb):
    m = y.shape[0]
    tn = 1024
    r = gate.shape[1]
    rows, mods = _row_specs(m, tm, rpb)
    return pl.pallas_call(
        _proj_res_kernel,
        grid=(m // tm, D_MODEL // tn),
        in_specs=[pl.BlockSpec((tm, D_MODEL), lambda i, j: (i, 0)), pl.BlockSpec((D_MODEL, tn), lambda i, j: (0, j)),
                  rows(tn), mods(r, tn)],
        out_specs=rows(tn),
        out_shape=jax.ShapeDtypeStruct((m, D_MODEL), F32),
        compiler_params=_cparams("arbitrary", "arbitrary"),
        name="out_proj_residual",
    )(y, w, x, gate)


def _rwkv_features(p, p_prev, mu, w0, w2, a0, a2, g2, k_k, k_a):
    xm = p + (p_prev - p) * mu
    r = xm[:, :C_R]
    k = xm[:, C_R:2 * C_R]
    v = xm[:, 2 * C_R:3 * C_R]
    o = 3 * C_R
    wd = xm[:, o:o + LORA_W]
    ad = xm[:, o + LORA_W:o + LORA_W + LORA_A]
    gd = xm[:, o + LORA_W + LORA_A:]
    w_log = -_softplus(-(w0 + _dot(jnp.tanh(wd), w2))) - 0.5
    lw = -jnp.exp(w_log)
    a = _sigmoid(a0 + _dot(ad, a2))
    g = _dot(_sigmoid(gd), g2)
    kk = k * k_k
    k = k * (1.0 + (a - 1.0) * k_a)
    return r, k, v, lw, a, g, kk


def _rwkv_head_out(y, r_h, k_h, v_h, g_h, rk_h, lng_h, lnb_h):
    mean = jnp.mean(y, axis=-1, keepdims=True)
    yc = y - mean
    var = jnp.mean(yc * yc, axis=-1, keepdims=True)
    yn = yc * lax.rsqrt(var + LN_X_EPS) * lng_h + lnb_h
    bonus = jnp.sum(r_h * k_h * rk_h, axis=-1, keepdims=True) * v_h
    return (yn + bonus) * g_h


def _rwkv_chunk_kernel(pr_ref, mu_ref, w0_ref, w2_ref, a0_ref, a2_ref, g2_ref, kk_ref, ka_ref, rk_ref,
                       lng_ref, lnb_ref, o_ref, shift_ref, state_ref):
    c = pl.program_id(1)
    C = RW_CHUNK

    @pl.when(c == 0)
    def _():
        shift_ref[...] = jnp.zeros_like(shift_ref)
        state_ref[...] = jnp.zeros_like(state_ref)

    p = pr_ref[...]
    row = lax.broadcasted_iota(I32, (C, 1), 0)
    p_prev = jnp.where(row == 0, shift_ref[...], pltpu.roll(p, 1, axis=0))
    shift_ref[...] = p[C - 1:C, :]
    r, k, v, lw, a, g, kk_all = _rwkv_features(p, p_prev, mu_ref[...], w0_ref[...], w2_ref[...], a0_ref[...],
                                               a2_ref[...], g2_ref[...], kk_ref[...], ka_ref[...])
    cl = lw
    s = 1
    while s < C:
        cl = cl + jnp.where(row >= s, pltpu.roll(cl, s, axis=0), 0.0)
        s *= 2
    ti = lax.broadcasted_iota(I32, (C, C), 0)
    si = lax.broadcasted_iota(I32, (C, C), 1)
    strict = ti > si
    incl = ti >= si
    eye = (ti == si).astype(F32)
    heads = range(H_R)
    sls = [slice(h * HD_R, (h + 1) * HD_R) for h in heads]
    kkt, kh, bh, rt, kbar, bbar, vb, g_end = [], [], [], [], [], [], [], []
    for sl in sls:
        kk_h = kk_all[:, sl]
        nrm = jnp.sqrt(jnp.sum(kk_h * kk_h, axis=-1, keepdims=True))
        kk_h = kk_h / jnp.maximum(nrm, 1e-12)
        b_h = kk_h * a[:, sl]
        cl_h = cl[:, sl]
        cl_end = cl_h[C - 1:C, :]
        e_neg = jnp.exp(-cl_h)
        e_end = jnp.exp(cl_end - cl_h)
        kkt.append((kk_h * jnp.exp(cl_h - lw[:, sl])).astype(BF))
        kh.append((k[:, sl] * e_neg).astype(BF))
        bh.append((b_h * e_neg).astype(BF))
        rt.append((r[:, sl] * jnp.exp(cl_h)).astype(BF))
        kbar.append((k[:, sl] * e_end).astype(BF))
        bbar.append((b_h * e_end).astype(BF))
        vb.append(v[:, sl].astype(BF))
        g_end.append(jnp.exp(cl_end))
    lkk = [jnp.where(strict, _dot_nt(kkt[h], kh[h]), 0.0).astype(BF) for h in heads]
    nil = [jnp.where(strict, -_dot_nt(kkt[h], bh[h]), 0.0) for h in heads]
    grk = [jnp.where(incl, _dot_nt(rt[h], kh[h]), 0.0).astype(BF) for h in heads]
    grb = [jnp.where(incl, _dot_nt(rt[h], bh[h]), 0.0).astype(BF) for h in heads]
    tinv = [eye + n for n in nil]
    m = 2
    while m < C:
        nil = [_dot(n, n) for n in nil]
        tinv = [t + _dot(t, n) for t, n in zip(tinv, nil)]
        m *= 2
    s0 = [state_ref[h] for h in heads]
    x = [_dot_nt(kkt[h], s0[h]) + _dot(lkk[h], vb[h]) for h in heads]
    u = [_dot(tinv[h], x[h]).astype(BF) for h in heads]
    y = [_dot_nt(rt[h], s0[h]) + _dot(grk[h], vb[h]) - _dot(grb[h], u[h]) for h in heads]
    for h in heads:
        state_ref[h] = s0[h] * g_end[h] + _dot_tn(vb[h], kbar[h]) - _dot_tn(u[h], bbar[h])
    for h, sl in enumerate(sls):
        o_ref[:, sl] = _rwkv_head_out(y[h], r[:, sl], k[:, sl], v[:, sl], g[:, sl], rk_ref[:, sl], lng_ref[:, sl],
                                      lnb_ref[:, sl]).astype(o_ref.dtype)


def _rwkv_weights(mu, w0, w2, a0, a2, g2, k_k, k_a, r_k, ln_g, ln_b):
    row = lambda z: z.reshape(1, -1).astype(F32)
    return (row(mu), row(w0), w2.astype(BF), row(a0), a2.astype(BF), g2.astype(BF), row(k_k), row(k_a), row(r_k),
            row(ln_g), row(ln_b))


_RWKV_W_SHAPES = ((1, C_RIN), (1, C_R), (LORA_W, C_R), (1, C_R), (LORA_A, C_R), (LORA_G, C_R), (1, C_R), (1, C_R),
                  (1, C_R), (1, C_R), (1, C_R))


def _rwkv_prompt(pr, rw):
    b, t, _ = pr.shape
    C = RW_CHUNK
    full = lambda shp: pl.BlockSpec(shp, lambda i, j: (0,) * len(shp))
    return pl.pallas_call(
        _rwkv_chunk_kernel,
        grid=(b, t // C),
        in_specs=[pl.BlockSpec((None, C, C_RIN), lambda i, j: (i, j, 0))] + [full(s) for s in _RWKV_W_SHAPES],
        out_specs=[pl.BlockSpec((None, C, C_R), lambda i, j: (i, j, 0)),
                   pl.BlockSpec((None, 1, C_RIN), lambda i, j: (i, 0, 0)),
                   pl.BlockSpec((None, H_R, HD_R, HD_R), lambda i, j: (i, 0, 0, 0))],
        out_shape=[jax.ShapeDtypeStruct((b, t, C_R), BF),
                   jax.ShapeDtypeStruct((b, 1, C_RIN), F32),
                   jax.ShapeDtypeStruct((b, H_R, HD_R, HD_R), F32)],
        compiler_params=_cparams("arbitrary", "arbitrary"),
        name="rwkv_chunk",
    )(pr, *rw)


def _rwkv_step_kernel(pr_ref, prev_ref, s0_ref, mu_ref, w0_ref, w2_ref, a0_ref, a2_ref, g2_ref, kk_ref, ka_ref,
                      rk_ref, lng_ref, lnb_ref, o_ref, state_ref):
    nb = pr_ref.shape[0]
    r, k, v, lw, a, g, kk_all = _rwkv_features(pr_ref[...], prev_ref[...], mu_ref[...], w0_ref[...], w2_ref[...],
                                               a0_ref[...], a2_ref[...], g2_ref[...], kk_ref[...], ka_ref[...])
    decay = jnp.exp(lw)
    ii = lax.broadcasted_iota(I32, (HD_R, HD_R), 0)
    jj = lax.broadcasted_iota(I32, (HD_R, HD_R), 1)
    eye = ii == jj
    col = lambda z: jnp.sum(jnp.where(eye, z, 0.0), axis=1, keepdims=True)
    for bi in range(nb):
        for h in range(H_R):
            sl = slice(h * HD_R, (h + 1) * HD_R)
            rows = lambda z: z[bi:bi + 1, sl]
            r_h, k_h, v_h, a_h, w_h = rows(r), rows(k), rows(v), rows(a), rows(decay)
            kk_h = rows(kk_all)
            kk_h = kk_h / jnp.maximum(jnp.sqrt(jnp.sum(kk_h * kk_h, axis=-1, keepdims=True)), 1e-12)
            b_h = kk_h * a_h
            s0 = s0_ref[bi, h]
            sa = jnp.sum(s0 * (-kk_h), axis=1, keepdims=True)
            s1 = s0 * w_h + sa * b_h + col(v_h) * k_h
            state_ref[bi, h] = s1
            y_col = jnp.sum(s1 * r_h, axis=1, keepdims=True)
            y = jnp.sum(jnp.where(eye, y_col, 0.0), axis=0, keepdims=True)
            o_ref[bi:bi + 1, sl] = _rwkv_head_out(y, r_h, k_h, v_h, rows(g), rk_ref[:, sl], lng_ref[:, sl],
                                                  lnb_ref[:, sl]).astype(o_ref.dtype)


def _rwkv_step(pr, prev, s0, rw):
    b = pr.shape[0]
    return pl.pallas_call(
        _rwkv_step_kernel,
        out_shape=[jax.ShapeDtypeStruct((b, C_R), BF), jax.ShapeDtypeStruct((b, H_R, HD_R, HD_R), F32)],
        compiler_params=pltpu.CompilerParams(vmem_limit_bytes=VMEM_LIMIT_BYTES),
        name="rwkv_step",
    )(pr, prev, s0, *rw)


def _cmp_partial_kernel(*refs):
    x_refs, w_ref, o_ref = refs[:-2], refs[-2], refs[-1]
    x = x_refs[0][...] if len(x_refs) == 1 else jnp.concatenate([r[...] for r in x_refs], axis=0)
    half = N_KVG * HD_N
    for s in range(2):
        acc = jnp.zeros((x.shape[0], N_KVG * 2 * CMP_HID), F32)
        for p in range(STRIDE):
            o = p * KV_COLS + s * half
            acc = acc + _dot(x[:, o:o + half], w_ref[p, s])
        o_ref[:, s * N_KVG * 2 * CMP_HID:(s + 1) * N_KVG * 2 * CMP_HID] = acc


def _cmp_partial_paged_kernel(pt_ref, *refs):
    del pt_ref
    _cmp_partial_kernel(*refs)


def _cmp_weights(cmp_w1):
    w1r = cmp_w1.reshape(2, 2, STRIDE, HD_N, CMP_HID)
    w = jnp.transpose(w1r, (2, 0, 3, 1, 4))
    w = w.reshape(STRIDE, 2, 1, HD_N, 1, 2 * CMP_HID)
    eye = jnp.eye(N_KVG, dtype=w.dtype).reshape(1, 1, N_KVG, 1, N_KVG, 1)
    wbd = eye * w
    return wbd.reshape(STRIDE, 2, N_KVG * HD_N, N_KVG * 2 * CMP_HID).astype(BF)


def _cmp_partial_rows(x, wbd, tr):
    r = x.shape[0]
    n = 2 * N_KVG * 2 * CMP_HID
    return pl.pallas_call(
        _cmp_partial_kernel,
        grid=(r // tr,),
        in_specs=[pl.BlockSpec((tr, STRIDE * KV_COLS), lambda i: (i, 0)),
                  pl.BlockSpec(wbd.shape, lambda i: (0, 0, 0, 0))],
        out_specs=pl.BlockSpec((tr, n), lambda i: (i, 0)),
        out_shape=jax.ShapeDtypeStruct((r, n), F32),
        compiler_params=_cparams("arbitrary"),
        name="cmp_partial",
    )(x, wbd)


PAGES_PER_STEP = 8


def _cmp_partial_paged(cache, page_table, wbd):
    b, npg = page_table.shape
    cpp = PAGE_SIZE // STRIDE
    n = 2 * N_KVG * 2 * CMP_HID
    steps = npg // PAGES_PER_STEP

    def page_spec(kpg):
        return pl.BlockSpec((None, cpp, STRIDE * KV_COLS), lambda i, j, pt: (pt[i, j * PAGES_PER_STEP + kpg], 0, 0))

    grid_spec = pltpu.PrefetchScalarGridSpec(
        num_scalar_prefetch=1,
        grid=(b, steps),
        in_specs=[page_spec(kpg) for kpg in range(PAGES_PER_STEP)]
        + [pl.BlockSpec(wbd.shape, lambda i, j, pt: (0, 0, 0, 0))],
        out_specs=pl.BlockSpec((None, PAGES_PER_STEP * cpp, n), lambda i, j, pt: (i, j, 0)),
    )
    return pl.pallas_call(
        _cmp_partial_paged_kernel,
        grid_spec=grid_spec,
        out_shape=jax.ShapeDtypeStruct((b, npg * cpp, n), F32),
        compiler_params=_cparams("arbitrary", "arbitrary"),
        name="cmp_partial_paged",
    )(page_table, *([cache] * PAGES_PER_STEP), wbd)


def _cmp_finish_kernel(c_ref, cpos_ref, w2_ref, o_ref):
    c = c_ref[...]
    nrow = c.shape[0]
    c_next = pltpu.roll(c, nrow - 1, axis=0)
    for s in range(2):
        for gi in range(N_KVG):
            o = (s * N_KVG + gi) * 2 * CMP_HID
            hid = (c[:, o:o + CMP_HID] + cpos_ref[0:1, o:o + CMP_HID]
                   + c_next[:, o + CMP_HID:o + 2 * CMP_HID] + cpos_ref[1:2, o + CMP_HID:o + 2 * CMP_HID])
            oo = (s * N_KVG + gi) * HD_N
            o_ref[:, oo:oo + HD_N] = _dot(_gelu_tanh(hid), w2_ref[s])


def _cmp_finish(c, cpos, w2):
    b, nch, n = c.shape
    return pl.pallas_call(
        _cmp_finish_kernel,
        grid=(b,),
        in_specs=[pl.BlockSpec((None, nch, n), lambda i: (i, 0, 0)), pl.BlockSpec(cpos.shape, lambda i: (0, 0)),
                  pl.BlockSpec(w2.shape, lambda i: (0, 0, 0))],
        out_specs=pl.BlockSpec((None, nch, KV_COLS), lambda i: (i, 0, 0)),
        out_shape=jax.ShapeDtypeStruct((b, nch, KV_COLS), F32),
        compiler_params=_cparams("arbitrary"),
        name="cmp_finish",
    )(c, cpos, w2.astype(BF))


def _cmp_pos_rows(cmp_pos):
    pos = cmp_pos.reshape(2, STRIDE, 1, 1, HD_N)
    rows = jnp.broadcast_to(pos, (2, STRIDE, 2, N_KVG, HD_N)).reshape(2, STRIDE * KV_COLS)
    return jnp.concatenate([rows, jnp.zeros((6, STRIDE * KV_COLS), F32)], axis=0)


def _bias_table_kernel(dist_ref, rb_ref, o_ref):
    bucket = _t5_bucket(dist_ref[...])
    for h in range(H_N):
        out = jnp.zeros(bucket.shape, F32)
        for b in range(N_BUCKETS):
            out = jnp.where(bucket == b, rb_ref[b, h], out)
        o_ref[h] = out


def _bias_table(dist, rel_bias, tr):
    r, n = dist.shape
    return pl.pallas_call(
        _bias_table_kernel,
        grid=(r // tr,),
        in_specs=[pl.BlockSpec((tr, n), lambda i: (i, 0)),
                  pl.BlockSpec(memory_space=pltpu.SMEM)],
        out_specs=pl.BlockSpec((H_N, tr, n), lambda i: (0, i, 0)),
        out_shape=jax.ShapeDtypeStruct((H_N, r, n), F32),
        compiler_params=_cparams("arbitrary"),
        name="bias_table",
    )(dist, rel_bias)


def _softmax_rows(logits, valid):
    lm = jnp.where(valid, logits, NEG_INF)
    e = jnp.exp(lm - jnp.max(lm, axis=-1, keepdims=True))
    return e / jnp.sum(e, axis=-1, keepdims=True)


def _cmp_attn_kernel(q_ref, kv_ref, bias_ref, ovt_ref, o_ref, sel_ref):
    tq = q_ref.shape[0]
    ncp = kv_ref.shape[0]
    nsb = ovt_ref.shape[0]
    q0 = pl.program_id(1) * tq
    qpos = q0 + lax.broadcasted_iota(I32, (tq, 1), 0)
    cend = lax.broadcasted_iota(I32, (1, ncp), 1) * STRIDE + (L_CMP - 1)
    valid = (qpos >= cend) & (lax.broadcasted_iota(I32, (1, ncp), 1) < ncp - 1)
    validf = valid.astype(F32)
    q = q_ref[...] * (HD_N ** -0.5)
    blk = lax.broadcasted_iota(I32, (nsb, tq), 0)
    cur = (q0 + lax.broadcasted_iota(I32, (1, tq), 1)) // L_SEL
    forced = (blk == 0) | (blk == cur) | (blk == cur - 1)
    future = blk > cur
    for gi in range(N_KVG):
        kc = kv_ref[:, gi * HD_N:(gi + 1) * HD_N]
        vc = kv_ref[:, (N_KVG + gi) * HD_N:(N_KVG + gi + 1) * HD_N]
        pcs = jnp.zeros((tq, ncp), F32)
        for hl in range(HPG):
            h = gi * HPG + hl
            sl = slice(h * HD_N, (h + 1) * HD_N)
            pc = _softmax_rows(_dot_nt(q[:, sl], kc) + bias_ref[h], valid) * validf
            pcs = pcs + pc
            o_ref[:, sl] = _dot(pc, vc)
        imp = _dot_nt(ovt_ref[...], pcs)
        score = jnp.where(forced, FORCE, jnp.where(future, -FORCE, imp))
        rank = jnp.zeros((nsb, tq), F32)
        for i in range(nsb):
            si = score[i:i + 1, :]
            rank = rank + ((si > score) | ((si == score) & (i < blk))).astype(F32)
        sel_ref[gi] = (rank < N_TOP).astype(sel_ref.dtype)


def _cmp_sel_overlap_t(nc, ncp, nsb):
    s = np.arange(nc)[None, :] * STRIDE
    j = np.arange(nsb)[:, None] * L_SEL
    ov = np.clip(np.minimum(s + L_CMP, j + L_SEL) - np.maximum(s, j), 0, None) / L_CMP
    return np.pad(ov, ((0, 0), (0, ncp - nc))).astype(np.float32)


def _cmp_attn_prompt(p_n, kv_cmp, bias_c, t):
    b = kv_cmp.shape[0]
    ncp = kv_cmp.shape[1]
    nsb = t // L_SEL
    tq = ATT_TILE
    nqt = t // tq
    ovt = jnp.asarray(_cmp_sel_overlap_t(ncp - 1, ncp, nsb), BF)
    return pl.pallas_call(
        _cmp_attn_kernel,
        grid=(b, nqt),
        in_specs=[pl.BlockSpec((tq, C_N), lambda i, j: (i * nqt + j, 0)),
                  pl.BlockSpec((None, ncp, KV_COLS), lambda i, j: (i, 0, 0)),
                  pl.BlockSpec((H_N, tq, ncp), lambda i, j: (0, j, 0)),
                  pl.BlockSpec((nsb, ncp), lambda i, j: (0, 0))],
        out_specs=[pl.BlockSpec((tq, C_N), lambda i, j: (i * nqt + j, 0)),
                   pl.BlockSpec((None, N_KVG, nsb, tq), lambda i, j: (i, 0, 0, j))],
        out_shape=[jax.ShapeDtypeStruct((b * t, C_N), F32), jax.ShapeDtypeStruct((b, N_KVG, nsb, t), BF)],
        compiler_params=_cparams("arbitrary", "arbitrary"),
        name="nsa_cmp_select",
    )(p_n, kv_cmp, bias_c, ovt)


ATT_TK = 128
ATT_R = ATT_TILE // ATT_TK
N_SEL_OFFS = ATT_R + (MAX_DIST + ATT_TK - 1) // ATT_TK + 1
N_WIN_OFFS = ATT_R + WINDOW // ATT_TK
QA_COLS = HD_N + 32


def _swa_kernel(qa_ref, ks_ref, vs_ref, kw_ref, vw_ref, tabs_ref, tabw_ref, gn_ref, oc_ref, o_ref):
    tq = qa_ref.shape[1]
    qt = pl.program_id(2)
    top = ATT_R * qt + ATT_R - 1

    heads = range(HPG)
    tiles = range(ATT_R)

    def attend(k_ref, v_ref, tab_ref, lo, n_off):
        def body(kp, carry):
            ki = [ATT_R * kp + j for j in tiles]
            kt = [k_ref[i] for i in ki]
            vt = [v_ref[i] for i in ki]
            off = [jnp.minimum(top - i, n_off - 1) for i in ki]
            s = [[lax.dot_general(kt[j], qa_ref[hl], (((1,), (1,)), ((), ())), preferred_element_type=F32)
                  + tab_ref[off[j], hl * ATT_TK:(hl + 1) * ATT_TK, :] for j in tiles] for hl in heads]
            m_new = [functools.reduce(jnp.maximum, [carry[hl][0]] + [jnp.max(s[hl][j], axis=0, keepdims=True)
                                                                    for j in tiles]) for hl in heads]
            alpha = [jnp.exp(carry[hl][0] - m_new[hl]) for hl in heads]
            p = [[jnp.exp(s[hl][j] - m_new[hl]) for j in tiles] for hl in heads]
            l = [alpha[hl] * carry[hl][1] + sum(jnp.sum(p[hl][j], axis=0, keepdims=True) for j in tiles)
                 for hl in heads]
            acc = [alpha[hl] * carry[hl][2] + sum(jnp.dot(vt[j], p[hl][j].astype(BF), preferred_element_type=F32)
                                                  for j in tiles) for hl in heads]
            return tuple((m_new[hl], l[hl], acc[hl]) for hl in heads)

        init = tuple((jnp.full((1, tq), NEG_INF, F32), jnp.zeros((1, tq), F32), jnp.zeros((HD_N, tq), F32))
                     for _ in heads)
        res = lax.fori_loop(lo // ATT_R, qt + 1, body, init)
        return [acc / l for _, l, acc in res]

    o_sel = attend(ks_ref, vs_ref, tabs_ref, 0, N_SEL_OFFS)
    o_win = attend(kw_ref, vw_ref, tabw_ref, jnp.maximum(top + 1 - N_WIN_OFFS, 0), N_WIN_OFFS)
    gates = _sigmoid(gn_ref[...])
    gates_t = gates.T
    for hl in heads:
        sl = slice(hl * HD_N, (hl + 1) * HD_N)
        o_t = gates_t[3 * hl + 1:3 * hl + 2, :] * o_sel[hl] + gates_t[3 * hl + 2:3 * hl + 3, :] * o_win[hl]
        o_ref[:, sl] = (gates[:, 3 * hl:3 * hl + 1] * oc_ref[:, sl] + o_t.T).astype(o_ref.dtype)


def _swa_prompt(p_n, qa, ks, vs, kw, vw, tab_s, tab_w, o_cmp, t):
    b = qa.shape[0]
    tq = ATT_TILE
    nqt = t // tq
    nkt = t // ATT_TK
    gw = HPG * HD_N
    k_spec = pl.BlockSpec((None, None, nkt, ATT_TK, QA_COLS), lambda i, g, j: (i, g, 0, 0, 0))
    v_spec = pl.BlockSpec((None, None, nkt, HD_N, ATT_TK), lambda i, g, j: (i, g, 0, 0, 0))
    tab_spec = lambda n: pl.BlockSpec((n, HPG * ATT_TK, tq), lambda i, g, j: (0, g, 0))
    return pl.pallas_call(
        _swa_kernel,
        grid=(b, N_KVG, nqt),
        in_specs=[pl.BlockSpec((None, HPG, tq, QA_COLS), lambda i, g, j: (i, g, j, 0)),
                  k_spec, v_spec, k_spec, v_spec, tab_spec(N_SEL_OFFS), tab_spec(N_WIN_OFFS),
                  pl.BlockSpec((tq, GN_GROUP_COLS), lambda i, g, j: (i * nqt + j, NP_GN // GN_GROUP_COLS + g)),
                  pl.BlockSpec((tq, gw), lambda i, g, j: (i * nqt + j, g))],
        out_specs=pl.BlockSpec((tq, gw), lambda i, g, j: (i * nqt + j, g)),
        out_shape=jax.ShapeDtypeStruct((b * t, C_N), BF),
        compiler_params=_cparams("arbitrary", "arbitrary", "arbitrary"),
        name="nsa_sel_win",
    )(qa, ks, vs, kw, vw, tab_s, tab_w, p_n, o_cmp)


def _swa_operands(p_n, sel, b, t):
    nsb = t // L_SEL
    nkt = t // ATT_TK
    q = (p_n[:, :C_N] * (HD_N ** -0.5)).reshape(b, t, H_N, HD_N).transpose(0, 2, 1, 3)
    pen = jnp.where(jnp.transpose(sel, (0, 1, 3, 2)) > 0.5, 0.0, NEG_INF).astype(F32)
    pen = jnp.broadcast_to(pen[:, :, None], (b, N_KVG, HPG, t, nsb)).reshape(b, H_N, t, nsb)
    qa = jnp.concatenate([q, pen], axis=-1).astype(BF)

    def split(cols):
        kv = p_n[:, cols:cols + KV_COLS].reshape(b, t, 2, N_KVG, HD_N)
        k = jnp.transpose(kv[:, :, 0], (0, 2, 1, 3))
        v = jnp.transpose(kv[:, :, 1], (0, 2, 3, 1)).reshape(b, N_KVG, HD_N, nkt, ATT_TK)
        return k, jnp.transpose(v, (0, 1, 3, 2, 4)).astype(BF)

    onehot = jnp.asarray(np.arange(t)[:, None] // L_SEL == np.arange(nsb)[None, :], F32)
    k_s, v_s = split(NP_KVS)
    k_w, v_w = split(NP_KVW)
    ext = lambda k, e: jnp.concatenate([k, jnp.broadcast_to(e, (b, N_KVG, t, nsb))], axis=-1).astype(BF) \
        .reshape(b, N_KVG, nkt, ATT_TK, QA_COLS)
    return qa, ext(k_s, onehot), v_s, ext(k_w, jnp.zeros((t, nsb), F32)), v_w


def _sample_cmp_kernel(q_ref, kv_ref, rbt_ref, ov_ref, o_ref, idx_ref, *, past):
    ncp = kv_ref.shape[0]
    nsbp = ov_ref.shape[1]
    nsb = -(-(past + 1) // L_SEL)
    q = q_ref[...] * (HD_N ** -0.5)
    hrow = lax.broadcasted_iota(I32, (H_N, 1), 0)
    nidx = lax.broadcasted_iota(I32, (1, ncp), 1)
    valid = nidx < ncp - 1
    bias = _bias_rows(past - (nidx * STRIDE + (L_CMP - 1)), rbt_ref[...])
    logits = jnp.zeros((H_N, ncp), F32)
    for gi in range(N_KVG):
        lg = _dot_nt(q, kv_ref[:, gi * HD_N:(gi + 1) * HD_N])
        logits = jnp.where(hrow // HPG == gi, lg, logits)
    pc = _softmax_rows(logits + bias, valid) * valid.astype(F32)
    o = jnp.zeros((H_N, HD_N), F32)
    for gi in range(N_KVG):
        og = _dot(pc, kv_ref[:, (N_KVG + gi) * HD_N:(N_KVG + gi + 1) * HD_N])
        o = jnp.where(hrow // HPG == gi, og, o)
    o_ref[...] = o
    imp_h = _dot(pc, ov_ref[...])
    blk = lax.broadcasted_iota(I32, (8, nsbp), 1)
    grow = lax.broadcasted_iota(I32, (8, 1), 0)
    cur = past // L_SEL
    score = jnp.full((8, nsbp), -3e38, F32)
    for gi in range(N_KVG):
        imp = jnp.sum(jnp.where(hrow // HPG == gi, imp_h, 0.0), axis=0, keepdims=True)
        score = jnp.where(grow == gi, imp, score)
    forced = (blk == 0) | (blk == cur) | (blk == cur - 1)
    score = jnp.where(forced, FORCE, jnp.where(blk > cur, -FORCE, score))
    score = jnp.where((blk < nsb) & (grow < N_KVG), score, -3e38)
    lane = lax.broadcasted_iota(I32, (8, LANES), 1)
    picks = jnp.zeros((8, LANES), I32)
    for it in range(N_TOP):
        mx = jnp.max(score, axis=-1, keepdims=True)
        pick = jnp.min(jnp.where(score == mx, blk, nsbp), axis=-1, keepdims=True)
        picks = jnp.where(lane == it, pick, picks)
        score = jnp.where(blk == pick, -3e38, score)
    idx_ref[...] = picks


def _sample_cmp(q, kv_cmp, rel_bias, past):
    b, ncp, _ = kv_cmp.shape
    nc = ncp - 1
    nsb = -(-(past + 1) // L_SEL)
    nsbp = -(-nsb // LANES) * LANES
    s = np.arange(nc)[:, None] * STRIDE
    j = np.arange(nsb)[None, :] * L_SEL
    ov = np.clip(np.minimum(s + L_CMP, j + L_SEL) - np.maximum(s, j), 0, None) / L_CMP
    ov = np.pad(ov, ((0, ncp - nc), (0, nsbp - nsb))).astype(np.float32)
    return pl.pallas_call(
        functools.partial(_sample_cmp_kernel, past=past),
        grid=(b,),
        in_specs=[pl.BlockSpec((None, H_N, HD_N), lambda i: (i, 0, 0)),
                  pl.BlockSpec((None, ncp, KV_COLS), lambda i: (i, 0, 0)),
                  pl.BlockSpec((H_N, N_BUCKETS), lambda i: (0, 0)),
                  pl.BlockSpec((ncp, nsbp), lambda i: (0, 0))],
        out_specs=[pl.BlockSpec((None, H_N, HD_N), lambda i: (i, 0, 0)),
                   pl.BlockSpec((None, 8, LANES), lambda i: (i, 0, 0))],
        out_shape=[jax.ShapeDtypeStruct((b, H_N, HD_N), F32), jax.ShapeDtypeStruct((b, 8, LANES), I32)],
        compiler_params=_cparams("arbitrary"),
        name="nsa_sample_cmp_select",
    )(q, kv_cmp, rel_bias.T, jnp.asarray(ov, BF))


def _block_copy_kernel(pg_ref, gr_ref, x_ref, o_ref):
    del pg_ref, gr_ref
    o_ref[...] = x_ref[...]


def _gather_sel_pages(cache_t, page, grp):
    n = page.shape[0]
    return pl.pallas_call(
        _block_copy_kernel,
        grid_spec=pltpu.PrefetchScalarGridSpec(
            num_scalar_prefetch=2, grid=(n,),
            in_specs=[pl.BlockSpec((None, 2, None, HD_N, PAGE_SIZE), lambda i, pg, gr: (pg[i], 0, gr[i], 0, 0))],
            out_specs=pl.BlockSpec((None, 2, HD_N, PAGE_SIZE), lambda i, pg, gr: (i, 0, 0, 0))),
        out_shape=jax.ShapeDtypeStruct((n, 2, HD_N, PAGE_SIZE), cache_t.dtype),
        compiler_params=_cparams("arbitrary"),
        name="gather_sel_pages",
    )(page, grp, cache_t)


def _sample_swa_kernel(idx_ref, q_ref, blk_ref, win_ref, new_ref, rbt_ref, gate_ref, oc_ref, o_ref, *, past):
    bi = pl.program_id(0)
    q = q_ref[...] * (HD_N ** -0.5)
    rbt = rbt_ref[...]
    hrow = lax.broadcasted_iota(I32, (H_N, 1), 0)
    nk = N_TOP * PAGE_SIZE
    lane = lax.broadcasted_iota(I32, (1, nk), 1)
    new_blk = past // L_SEL
    bias_new = rbt[:, 0:1]
    gates = _sigmoid(gate_ref[...])
    nwin = win_ref.shape[0]
    wdist = nwin - lax.broadcasted_iota(I32, (1, nwin), 1)
    wbias = _bias_rows(wdist, rbt)
    wvalid = (wdist >= 0) & (wdist <= WINDOW)

    def with_new(logits, valid, weigh, k_new, v_new):
        l_new = jnp.sum(q * k_new, axis=-1, keepdims=True) + bias_new
        lm = jnp.where(valid, logits, NEG_INF)
        m = jnp.maximum(jnp.max(lm, axis=-1, keepdims=True), l_new)
        e = jnp.where(valid, jnp.exp(lm - m), 0.0)
        e_new = jnp.exp(l_new - m)
        den = jnp.sum(e, axis=-1, keepdims=True) + e_new
        return (weigh(e) + e_new * v_new) / den

    o_sel = jnp.zeros((H_N, HD_N), F32)
    o_win = jnp.zeros((H_N, HD_N), F32)
    bpp = PAGE_SIZE // L_SEL
    tok = lane % PAGE_SIZE
    for gi in range(N_KVG):
        ksl = slice(gi * HD_N, (gi + 1) * HD_N)
        vsl = slice((N_KVG + gi) * HD_N, (N_KVG + gi + 1) * HD_N)
        bid = jnp.zeros((1, nk), I32)
        for n in range(N_TOP):
            bid = jnp.where(lane // PAGE_SIZE == n, idx_ref[bi, gi, n], bid)
        dist = past - ((bid // bpp) * PAGE_SIZE + tok)
        valid = (bid != new_blk) & (tok // L_SEL == bid % bpp) & (dist >= 0)
        kt = jnp.concatenate([blk_ref[gi, n, 0] for n in range(N_TOP)], axis=1)
        vt = jnp.concatenate([blk_ref[gi, n, 1] for n in range(N_TOP)], axis=1)
        logits = _dot(q, kt) + _bias_rows(dist, rbt)
        og = with_new(logits, valid, lambda e, vt=vt: _dot_nt(e, vt), new_ref[0:1, ksl], new_ref[0:1, vsl])
        o_sel = jnp.where(hrow // HPG == gi, og, o_sel)
        logits = _dot_nt(q, win_ref[:, ksl]) + wbias
        og = with_new(logits, wvalid, lambda e, vsl=vsl: _dot(e, win_ref[:, vsl]), new_ref[1:2, ksl],
                      new_ref[1:2, vsl])
        o_win = jnp.where(hrow // HPG == gi, og, o_win)
    o_ref[...] = gates[:, 0:1] * oc_ref[...] + gates[:, 1:2] * o_sel + gates[:, 2:3] * o_win


def _sample_swa(idx, q, blocks, win, new_kv, rel_bias, gates, o_cmp, past):
    b = q.shape[0]
    w = win.shape[1]
    grid_spec = pltpu.PrefetchScalarGridSpec(
        num_scalar_prefetch=1, grid=(b,),
        in_specs=[pl.BlockSpec((None, H_N, HD_N), lambda i, ix: (i, 0, 0)),
                  pl.BlockSpec((None, N_KVG, N_TOP, 2, HD_N, PAGE_SIZE), lambda i, ix: (i, 0, 0, 0, 0, 0)),
                  pl.BlockSpec((None, w, KV_COLS), lambda i, ix: (i, 0, 0)),
                  pl.BlockSpec((None, 2, KV_COLS), lambda i, ix: (i, 0, 0)),
                  pl.BlockSpec((H_N, N_BUCKETS), lambda i, ix: (0, 0)),
                  pl.BlockSpec((None, H_N, 3), lambda i, ix: (i, 0, 0)),
                  pl.BlockSpec((None, H_N, HD_N), lambda i, ix: (i, 0, 0))],
        out_specs=pl.BlockSpec((None, H_N, HD_N), lambda i, ix: (i, 0, 0)))
    return pl.pallas_call(
        functools.partial(_sample_swa_kernel, past=past),
        grid_spec=grid_spec,
        out_shape=jax.ShapeDtypeStruct((b, H_N, HD_N), F32),
        compiler_params=_cparams("arbitrary"),
        name="nsa_sample_sel_win",
    )(idx, q, blocks, win, new_kv, rel_bias.T, gates, o_cmp)


ROUTER_COLS = LANES


def _router_kernel(x_ref, g_ref, sh_ref, sc_ref, w_ref, b_ref, h_ref, e_ref, wt_ref, rk_ref, cnt_ref):
    i = pl.program_id(0)
    tm = x_ref.shape[0]

    @pl.when(i == 0)
    def _():
        cnt_ref[...] = jnp.zeros_like(cnt_ref)

    h = (_rms(x_ref[...], g_ref[...]) * (1.0 + sc_ref[...]) + sh_ref[...]).astype(BF)
    h_ref[:, 0, :] = h
    logits = jnp.dot(h, w_ref[...], preferred_element_type=F32) + b_ref[...]
    lane = lax.broadcasted_iota(I32, (tm, ROUTER_COLS), 1)

    def top1(vals, ok):
        vm = jnp.where(ok, vals, -3e38)
        mx = jnp.max(vm, axis=-1, keepdims=True)
        return mx, jnp.min(jnp.where(ok & (vm == mx), lane, ROUTER_COLS), axis=-1, keepdims=True)

    isg = lane < N_EGROUPS
    pg = _softmax_rows(logits, isg)
    g_w, g_i = top1(pg, isg)
    ise = (lane >= N_EGROUPS) & ((lane - N_EGROUPS) // EXP_PER_GROUP == g_i)
    pe = _softmax_rows(logits, ise)
    w0, l0 = top1(pe, ise)
    w1, l1 = top1(pe, ise & (lane != l0))
    den = w0 + w1
    e0 = l0 - N_EGROUPS
    e1 = l1 - N_EGROUPS
    e_ref[...] = jnp.where(lane == 0, e0, jnp.where(lane == 1, e1, 0))
    wt_ref[...] = jnp.where(lane == 0, w0 / den * g_w, jnp.where(lane == 1, w1 / den * g_w, 0.0))
    oh0 = (lane == e0).astype(F32)
    oh1 = (lane == e1).astype(F32)
    cnt = oh0 + oh1
    ti = lax.broadcasted_iota(I32, (tm, tm), 0)
    si = lax.broadcasted_iota(I32, (tm, tm), 1)
    before = _dot((ti > si).astype(F32), cnt) + cnt_ref[...]
    r0 = jnp.sum(before * oh0, axis=-1, keepdims=True)
    r1 = jnp.sum(before * oh1, axis=-1, keepdims=True)
    rk_ref[...] = jnp.where(lane == 0, r0, jnp.where(lane == 1, r1, 0.0)).astype(I32)
    cnt_ref[...] = cnt_ref[...] + jnp.sum(cnt, axis=0, keepdims=True)


def _router(x, g, shift, scale, w_r, b_r, tm, rpb):
    m = x.shape[0]
    r = shift.shape[1]
    rows = lambda tn: pl.BlockSpec((tm, tn), lambda i: (i, 0))
    mods = pl.BlockSpec((None, r, D_MODEL), lambda i: ((i * tm) // rpb, 0, 0))
    small = lambda dt: jax.ShapeDtypeStruct((m, ROUTER_COLS), dt)
    return pl.pallas_call(
        _router_kernel,
        grid=(m // tm,),
        in_specs=[rows(D_MODEL), pl.BlockSpec((1, D_MODEL), lambda i: (0, 0)), mods, mods,
                  pl.BlockSpec((D_MODEL, ROUTER_COLS), lambda i: (0, 0)),
                  pl.BlockSpec((1, ROUTER_COLS), lambda i: (0, 0))],
        out_specs=[pl.BlockSpec((tm, 1, D_MODEL), lambda i: (i, 0, 0)),
                   rows(ROUTER_COLS), rows(ROUTER_COLS), rows(ROUTER_COLS),
                   pl.BlockSpec((1, ROUTER_COLS), lambda i: (0, 0))],
        out_shape=[jax.ShapeDtypeStruct((m, 1, D_MODEL), BF), small(I32), small(F32), small(I32),
                   jax.ShapeDtypeStruct((1, ROUTER_COLS), F32)],
        compiler_params=_cparams("arbitrary"),
        name="moe_router",
    )(x, g.reshape(1, D_MODEL), shift, scale, w_r, b_r)


def _row_gather_ring(src_hbm, buf, sems, n, idx_now, idx_next):
    i = pl.program_id(0)
    last = pl.num_programs(0) - 1
    slot = i % 2

    def copy(r, idx, s):
        return pltpu.make_async_copy(src_hbm.at[idx(r)], buf.at[s, r], sems.at[s])

    def start_all(idx, s):
        def issue(j, c):
            copy(2 * j, idx, s).start(priority=0)
            copy(2 * j + 1, idx, s).start(priority=1)
            return c
        lax.fori_loop(0, n // 2, issue, 0)

    @pl.when(i == 0)
    def _():
        start_all(idx_now, slot)

    @pl.when(i < last)
    def _():
        start_all(idx_next, 1 - slot)

    def drain(r, c):
        copy(r, idx_now, slot).wait()
        return c

    lax.fori_loop(0, n, drain, 0)
    return slot


def _expert_kernel(be_ref, rt_ref, rtn_ref, h_hbm, w1_ref, w3_ref, w2_ref, o_ref, xbuf, sems):
    del be_ref
    blk = xbuf.shape[1]
    slot = _row_gather_ring(h_hbm, xbuf, sems, blk, lambda r: rt_ref[0, r], lambda r: rtn_ref[0, r])
    x = xbuf[slot, :, 0, :]
    a = jnp.dot(x, w1_ref[...].astype(BF), preferred_element_type=F32)
    b = jnp.dot(x, w3_ref[...].astype(BF), preferred_element_type=F32)
    hid = a * _sigmoid(a) * b
    o_ref[:, 0, :] = _dot(hid, w2_ref[...])


def _experts(h2, row_tok, blk_exp, w1, w3, w2):
    nblk, _, blk = row_tok.shape
    idx_spec = lambda d: pl.BlockSpec((None, 1, blk), lambda i, be: (jnp.minimum(i + d, nblk - 1), 0, 0),
                                      memory_space=pltpu.SMEM)
    grid_spec = pltpu.PrefetchScalarGridSpec(
        num_scalar_prefetch=1, grid=(nblk,),
        in_specs=[idx_spec(0), idx_spec(1),
                  pl.BlockSpec(memory_space=pl.ANY),
                  pl.BlockSpec((None, D_MODEL, D_EXP), lambda i, be: (be[i], 0, 0)),
                  pl.BlockSpec((None, D_MODEL, D_EXP), lambda i, be: (be[i], 0, 0)),
                  pl.BlockSpec((None, D_EXP, D_MODEL), lambda i, be: (be[i], 0, 0))],
        out_specs=pl.BlockSpec((blk, 1, D_MODEL), lambda i, be: (i, 0, 0)),
        scratch_shapes=[pltpu.VMEM((2, blk, 1, D_MODEL), BF), pltpu.SemaphoreType.DMA((2,))])
    return pl.pallas_call(
        _expert_kernel,
        grid_spec=grid_spec,
        out_shape=jax.ShapeDtypeStruct((nblk * blk, 1, D_MODEL), F32),
        compiler_params=_cparams("arbitrary"),
        name="moe_experts",
    )(blk_exp, row_tok, row_tok, h2, w1, w3, w2)


def _final_kernel(x_ref, g_ref, dest_ref, destn_ref, ys_hbm, wt_ref, nf_ref, o_ref, ybuf, sems):
    tm = x_ref.shape[0]
    pick = lambda ref: (lambda r: ref[r // tm, r % tm])
    slot = _row_gather_ring(ys_hbm, ybuf, sems, TOP_K * tm, pick(dest_ref), pick(destn_ref))
    wt = wt_ref[...]
    moe = wt[:, 0:1] * ybuf[slot, 0:tm, 0, :] + wt[:, 1:2] * ybuf[slot, tm:2 * tm, 0, :]
    o_ref[...] = _rms(x_ref[...] + g_ref[...] * moe, nf_ref[...])


def _final(x, gate, ys, dest, wts, norm_f, tm, rpb):
    m = x.shape[0]
    r = gate.shape[1]
    nt = m // tm
    rows = lambda tn: pl.BlockSpec((tm, tn), lambda i: (i, 0))
    idx_spec = lambda d: pl.BlockSpec((None, TOP_K, tm), lambda i: (jnp.minimum(i + d, nt - 1), 0, 0),
                                      memory_space=pltpu.SMEM)
    return pl.pallas_call(
        _final_kernel,
        grid=(nt,),
        in_specs=[rows(D_MODEL), pl.BlockSpec((None, r, D_MODEL), lambda i: ((i * tm) // rpb, 0, 0)),
                  idx_spec(0), idx_spec(1),
                  pl.BlockSpec(memory_space=pl.ANY),
                  rows(ROUTER_COLS), pl.BlockSpec((1, D_MODEL), lambda i: (0, 0))],
        out_specs=rows(D_MODEL),
        out_shape=jax.ShapeDtypeStruct((m, D_MODEL), F32),
        scratch_shapes=[pltpu.VMEM((2, TOP_K * tm, 1, D_MODEL), F32), pltpu.SemaphoreType.DMA((2,))],
        compiler_params=_cparams("arbitrary"),
        name="moe_combine_final_norm",
    )(x, gate, dest, dest, ys, wts, norm_f.reshape(1, D_MODEL))


def _moe_and_final(x1, g2, shift, scale, gate, w_r, b_r, exp_w1, exp_w3, exp_w2, norm_f, tm, rpb, blk):
    m = x1.shape[0]
    h2, eid, wts, rank, counts = _router(x1, g2, shift, scale, w_r, b_r, tm, rpb)
    counts = counts[0, :N_EXP].astype(I32)
    padded = (counts + blk - 1) // blk * blk
    pend = jnp.cumsum(padded)
    pstart = pend - padded
    n_blocks = -(-(m * TOP_K) // blk) + N_EXP
    starts = jnp.arange(n_blocks, dtype=I32)[:, None] * blk
    blk_exp = jnp.minimum(jnp.sum((pend[None, :] <= starts).astype(I32), axis=1), N_EXP - 1)
    e = eid[:, :TOP_K]
    dest = pstart[e] + rank[:, :TOP_K]
    tok = jnp.broadcast_to(jnp.arange(m, dtype=I32)[:, None], (m, TOP_K))
    row_tok = jnp.zeros((n_blocks * blk,), I32).at[dest.reshape(-1)].set(tok.reshape(-1))
    ys = _experts(h2, row_tok.reshape(n_blocks, 1, blk), blk_exp, exp_w1, exp_w3, exp_w2)
    dest_t = jnp.transpose(dest.reshape(m // tm, tm, TOP_K), (0, 2, 1))
    return _final(x1, gate, ys, dest_t, wts, norm_f, tm, rpb)


def _pack_in_proj(w_in):
    o = C_RIN
    w_r = w_in[:, :o]
    w_q = w_in[:, o:o + C_N + 3 * KV_COLS]
    o += C_N + 3 * KV_COLS
    w_gn = w_in[:, o:o + 3 * H_N].reshape(D_MODEL, N_KVG, 3 * HPG)
    w_gn = jnp.pad(w_gn, ((0, 0), (0, 0), (0, GN_GROUP_COLS - 3 * HPG))).reshape(D_MODEL, N_KVG * GN_GROUP_COLS)
    o += 3 * H_N
    w_gm = w_in[:, o:]
    return w_r.astype(BF), jnp.concatenate([w_q, w_gn], axis=1).astype(BF), w_gm.astype(BF)


def _prompt_bias_tables(rel_bias, t):
    tq, tk = ATT_TILE, ATT_TK
    i = np.arange(tq)[None, :]
    j = np.arange(tk)[:, None]
    dist = np.stack([tk * (o - (ATT_R - 1)) + i - j for o in range(N_WIN_OFFS)]).astype(np.int32)
    raw = _bias_table(jnp.asarray(dist.reshape(N_WIN_OFFS * tk, tq)), rel_bias, tk)
    raw = jnp.transpose(raw.reshape(H_N, N_WIN_OFFS, tk, tq), (1, 0, 2, 3))
    ok_w = jnp.asarray((dist >= 0) & (dist <= WINDOW))[:, None]
    ok_s = jnp.asarray(dist[:N_SEL_OFFS] >= 0)[:, None]
    tab_w = jnp.where(ok_w, raw, NEG_INF).reshape(N_WIN_OFFS, H_N * tk, tq)
    tab_s = jnp.where(ok_s, raw[:N_SEL_OFFS], NEG_INF).reshape(N_SEL_OFFS, H_N * tk, tq)
    nc = (t - L_CMP) // STRIDE + 1
    ncp = nc + 1
    dc = (np.arange(t)[:, None] - (np.arange(ncp)[None, :] * STRIDE + L_CMP - 1)).astype(np.int32)
    return tab_s, tab_w, _bias_table(jnp.asarray(dc), rel_bias, tq)


def kernel(x_prompt, x_sample, c_prompt, c_sample, cache_cmp_kv, cache_sel_kv, state_win_kv, state_rwkv_shift,
           state_rwkv_wkv, page_table, rel_bias, norm_f, norm1, norm2, w_ada, b_ada, w_in, rwkv_mu, rwkv_w0, rwkv_w2,
           rwkv_a0, rwkv_a2, rwkv_g2, rwkv_kk, rwkv_ka, rwkv_rk, rwkv_ln_g, rwkv_ln_b, cmp_pos, cmp_w1, cmp_w2,
           w_o_rwkv, w_o_nsa, w_out, router_wg, router_bg, router_we, router_be, exp_w1, exp_w3, exp_w2):
    bp, t, _ = x_prompt.shape
    bs = x_sample.shape[0]
    mp = bp * t
    past = page_table.shape[1] * PAGE_SIZE

    nrow = -(-(bp + bs) // 8) * 8
    c_all = jnp.concatenate([c_prompt, c_sample, jnp.zeros((nrow - bp - bs, D_MODEL), F32)], axis=0)
    mod = _ada(c_all, w_ada[0], b_ada[0]).reshape(nrow, 6, D_MODEL)
    mod_p = [mod[:bp, i][:, None, :] for i in range(6)]
    mod_s = [mod[bp:bp + bs, i][None] for i in range(6)]

    w_r, w_n, w_gm = _pack_in_proj(w_in[0])
    rw = _rwkv_weights(rwkv_mu[0], rwkv_w0[0], rwkv_w2[0], rwkv_a0[0], rwkv_a2[0], rwkv_g2[0], rwkv_kk[0],
                       rwkv_ka[0], rwkv_rk[0], rwkv_ln_g[0], rwkv_ln_b[0])
    wbd = _cmp_weights(cmp_w1[0])
    cpos = _cmp_partial_rows(_cmp_pos_rows(cmp_pos[0]), wbd, 8)
    wo_r, wo_n, wo = w_o_rwkv[0].astype(BF), w_o_nsa[0].astype(BF), w_out[0].astype(BF)
    w_router = jnp.pad(jnp.concatenate([router_wg[0], router_we[0]], axis=1),
                       ((0, 0), (0, ROUTER_COLS - N_EGROUPS - N_EXP))).astype(BF)
    b_router = jnp.pad(jnp.concatenate([router_bg[0], router_be[0]]), (0, ROUTER_COLS - N_EGROUPS - N_EXP))[None]

    tm = 512
    xp = x_prompt.reshape(mp, D_MODEL)
    h = _norm_mod(xp, norm1[0], mod_p[0], mod_p[1], tm, t)
    p_r = _matmul(h, w_r, tm, C_RIN // 2)
    p_n = _matmul(h, w_n, tm, NP_COLS // 2)
    p_g = _matmul(h, w_gm, tm, 2048)
    o_r, shift_p, wkv_p = _rwkv_prompt(p_r.reshape(bp, t, C_RIN), rw)
    kvc = p_n[:, NP_KVC:NP_KVC + KV_COLS]
    kvs = p_n[:, NP_KVS:NP_KVS + KV_COLS]
    kvw = p_n[:, NP_KVW:NP_KVW + KV_COLS]
    nch = t // STRIDE
    c_part = _cmp_partial_rows(kvc.reshape(bp * nch, STRIDE * KV_COLS), wbd, nch)
    kv_cmp = _cmp_finish(c_part.reshape(bp, nch, -1), cpos, cmp_w2[0])
    tab_s, tab_w, bias_c = _prompt_bias_tables(rel_bias, t)
    o_cmp, sel = _cmp_attn_prompt(p_n, kv_cmp, bias_c, t)
    o_n = _swa_prompt(p_n, *_swa_operands(p_n, sel, bp, t), tab_s, tab_w, o_cmp, t)
    y = _merge(o_r.reshape(mp, C_R), o_n, wo_r, wo_n, p_g, tm)
    x1 = _proj_residual(y, wo, xp, mod_p[2], tm, t)
    y_prompt = _moe_and_final(x1, norm2[0], mod_p[3], mod_p[4], mod_p[5], w_router, b_router, exp_w1[0], exp_w3[0],
                              exp_w2[0], norm_f, tm, t, 128).reshape(bp, t, D_MODEL)
    kv_shape = (1, bp, t, 2, N_KVG, HD_N)
    wlen = min(WINDOW, t)
    win_p = kvw.reshape(bp, t, KV_COLS)[:, t - wlen:].reshape(1, bp, wlen, 2, N_KVG, HD_N)

    xs = x_sample.reshape(bs, D_MODEL)
    hs = _norm_mod(xs, norm1[0], mod_s[0], mod_s[1], bs, bs)
    ps_r = _matmul(hs, w_r, bs, C_RIN // 2)
    ps_n = _matmul(hs, w_n, bs, NP_COLS // 2)
    ps_g = _matmul(hs, w_gm, bs, 2048)
    os_r, wkv_s = _rwkv_step(ps_r, state_rwkv_shift[0], state_rwkv_wkv[0], rw)
    kvc_s = ps_n[:, NP_KVC:NP_KVC + KV_COLS]
    kvs_s = ps_n[:, NP_KVS:NP_KVS + KV_COLS]
    kvw_s = ps_n[:, NP_KVW:NP_KVW + KV_COLS]
    n_pool = cache_cmp_kv.shape[1]
    cpp = PAGE_SIZE // STRIDE
    cs_part = _cmp_partial_paged(cache_cmp_kv[0].reshape(n_pool, cpp, STRIDE * KV_COLS), page_table, wbd)
    kv_cmp_s = _cmp_finish(cs_part, cpos, cmp_w2[0])
    q_s = ps_n[:, :C_N].reshape(bs, H_N, HD_N)
    o_cmp_s, picks = _sample_cmp(q_s, kv_cmp_s, rel_bias, past)
    idx = picks[:, :N_KVG, :N_TOP]
    bpp = PAGE_SIZE // L_SEL
    npb = past // L_SEL
    idc = jnp.minimum(idx, npb - 1)
    page = jnp.take_along_axis(page_table, (idc // bpp).reshape(bs, -1), axis=1).reshape(-1)
    cache_t = jnp.transpose(cache_sel_kv[0], (0, 2, 3, 4, 1))
    grp = jnp.broadcast_to(jnp.arange(N_KVG, dtype=I32)[None, :, None], idx.shape).reshape(-1)
    blocks = _gather_sel_pages(cache_t, page, grp).reshape(bs, N_KVG, N_TOP, 2, HD_N, PAGE_SIZE)
    win_buf = state_win_kv[0].reshape(bs, -1, KV_COLS)
    gates_s = ps_n[:, NP_GN:].reshape(bs, N_KVG, GN_GROUP_COLS)[:, :, :3 * HPG].reshape(bs, H_N, 3)
    new_kv = jnp.stack([kvs_s, kvw_s], axis=1)
    os_n = _sample_swa(idx, q_s, blocks, win_buf, new_kv, rel_bias, gates_s, o_cmp_s, past)
    ys = _merge(os_r, os_n.reshape(bs, C_N).astype(BF), wo_r, wo_n, ps_g, bs)
    xs1 = _proj_residual(ys, wo, xs, mod_s[2], bs, bs)
    y_sample = _moe_and_final(xs1, norm2[0], mod_s[3], mod_s[4], mod_s[5], w_router, b_router, exp_w1[0], exp_w3[0],
                              exp_w2[0], norm_f, bs, bs, 16).reshape(bs, 1, D_MODEL)
    kv1 = (1, bs, 1, 2, N_KVG, HD_N)
    wbuf = win_buf.shape[1]
    win_s = jnp.concatenate([win_buf, kvw_s[:, None, :]], axis=1)[:, -wbuf:].reshape(1, bs, wbuf, 2, N_KVG, HD_N)

    return (y_prompt, y_sample,
            kvc.reshape(kv_shape), kvc_s.reshape(kv1),
            kvs.reshape(kv_shape), kvs_s.reshape(kv1),
            win_p, win_s,
            shift_p.reshape(1, bp, C_RIN), ps_r.reshape(1, bs, C_RIN),
            wkv_p[None], wkv_s[None])
```

```python
import functools
import math

import numpy as np
import jax
import jax.numpy as jnp
from jax import lax
from jax.experimental import pallas as pl
from jax.experimental.pallas import tpu as pltpu

D_MODEL = 2048
PAGE_SIZE = 128
H_R, HD_R = 16, 64
C_R = H_R * HD_R
LORA_W, LORA_A, LORA_G = 64, 64, 128
C_RIN = 3 * C_R + LORA_W + LORA_A + LORA_G
LN_X_EPS = 64e-5
H_N, HD_N, N_KVG = 16, 64, 4
HPG = H_N // N_KVG
C_N = H_N * HD_N
KV_COLS = 2 * N_KVG * HD_N
L_CMP, STRIDE, CMP_HID = 32, 16, 64
L_SEL, N_TOP, WINDOW = 64, 16, 512
N_BUCKETS, MAX_DIST = 32, 128
N_EGROUPS, EXP_PER_GROUP = 4, 8
N_EXP = N_EGROUPS * EXP_PER_GROUP
TOP_K, D_EXP = 2, 512
RMS_EPS = 1e-6
NEG_INF = -1e30
FORCE = 1e9

BF = jnp.bfloat16
F32 = jnp.float32
I32 = jnp.int32

VMEM_LIMIT_BYTES = 56 * 1024 * 1024
LANES = 128
RW_CHUNK = 32
ATT_TILE = 256
GN_GROUP_COLS = 128
NP_Q, NP_KVC, NP_KVS, NP_KVW, NP_GN = 0, C_N, C_N + KV_COLS, C_N + 2 * KV_COLS, C_N + 3 * KV_COLS
NP_COLS = NP_GN + N_KVG * GN_GROUP_COLS


def _cparams(*sem):
    return pltpu.CompilerParams(dimension_semantics=sem, vmem_limit_bytes=VMEM_LIMIT_BYTES)


def _dot(a, b):
    return jnp.dot(a.astype(BF), b.astype(BF), preferred_element_type=F32)


def _dot_nt(a, b):
    return lax.dot_general(a.astype(BF), b.astype(BF), (((1,), (1,)), ((), ())), preferred_element_type=F32)


def _dot_tn(a, b):
    return lax.dot_general(a.astype(BF), b.astype(BF), (((0,), (0,)), ((), ())), preferred_element_type=F32)


def _softplus(x):
    return jnp.maximum(x, 0.0) + jnp.log1p(jnp.exp(-jnp.abs(x)))


def _sigmoid(x):
    return 1.0 / (1.0 + jnp.exp(-x))


def _gelu_tanh(x):
    return 0.5 * x * (1.0 + jnp.tanh(math.sqrt(2.0 / math.pi) * (x + 0.044715 * x * x * x)))


def _t5_bucket(dist):
    n = jnp.maximum(dist, 0)
    max_exact = N_BUCKETS // 2
    nf = jnp.maximum(n, 1).astype(F32)
    large = max_exact + (jnp.log(nf / max_exact) / math.log(MAX_DIST / max_exact)
                         * (N_BUCKETS - max_exact)).astype(I32)
    large = jnp.minimum(large, N_BUCKETS - 1)
    return jnp.where(n < max_exact, n, large)


def _bias_rows(dist, rbt):
    bucket = _t5_bucket(dist)
    out = jnp.zeros((rbt.shape[0], dist.shape[1]), F32)
    for b in range(N_BUCKETS):
        out = jnp.where(bucket == b, rbt[:, b:b + 1], out)
    return out


def _ada_kernel(c_ref, w_ref, b_ref, o_ref):
    o_ref[...] = _dot(c_ref[...], w_ref[...]) + b_ref[...]


def _ada(c, w_ada, b_ada):
    r = c.shape[0]
    n = w_ada.shape[1]
    tn = 1024
    return pl.pallas_call(
        _ada_kernel,
        grid=(n // tn,),
        in_specs=[pl.BlockSpec((r, D_MODEL), lambda j: (0, 0)),
                  pl.BlockSpec((D_MODEL, tn), lambda j: (0, j)),
                  pl.BlockSpec((1, tn), lambda j: (0, j))],
        out_specs=pl.BlockSpec((r, tn), lambda j: (0, j)),
        out_shape=jax.ShapeDtypeStruct((r, n), F32),
        compiler_params=_cparams("arbitrary"),
        name="ada_mod",
    )(c, w_ada, b_ada.reshape(1, n))


def _rms(x, g):
    return x * lax.rsqrt(jnp.mean(x * x, axis=-1, keepdims=True) + RMS_EPS) * g


def _norm_mod_kernel(x_ref, g_ref, sh_ref, sc_ref, o_ref):
    o_ref[...] = (_rms(x_ref[...], g_ref[...]) * (1.0 + sc_ref[...]) + sh_ref[...]).astype(o_ref.dtype)


def _row_specs(m, tm, rpb):
    del m
    return (lambda tn: pl.BlockSpec((tm, tn), lambda i, j: (i, j)),
            lambda r, tn: pl.BlockSpec((None, r, tn), lambda i, j: ((i * tm) // rpb, 0, j)))


def _norm_mod(x, g, shift, scale, tm, rpb):
    m = x.shape[0]
    r = shift.shape[1]
    rows, mods = _row_specs(m, tm, rpb)
    return pl.pallas_call(
        _norm_mod_kernel,
        grid=(m // tm, 1),
        in_specs=[rows(D_MODEL), pl.BlockSpec((1, D_MODEL), lambda i, j: (0, 0)), mods(r, D_MODEL), mods(r, D_MODEL)],
        out_specs=rows(D_MODEL),
        out_shape=jax.ShapeDtypeStruct((m, D_MODEL), BF),
        compiler_params=_cparams("arbitrary", "arbitrary"),
        name="norm_mod",
    )(x, g.reshape(1, D_MODEL), shift, scale)


def _mm_kernel(a_ref, w_ref, o_ref):
    o_ref[...] = jnp.dot(a_ref[...], w_ref[...], preferred_element_type=F32).astype(o_ref.dtype)


def _matmul(a, w, tm, tn, out_dtype=F32):
    m, k = a.shape
    n = w.shape[1]
    return pl.pallas_call(
        _mm_kernel,
        grid=(m // tm, n // tn),
        in_specs=[pl.BlockSpec((tm, k), lambda i, j: (i, 0)), pl.BlockSpec((k, tn), lambda i, j: (0, j))],
        out_specs=pl.BlockSpec((tm, tn), lambda i, j: (i, j)),
        out_shape=jax.ShapeDtypeStruct((m, n), out_dtype),
        compiler_params=_cparams("arbitrary", "arbitrary"),
        name="matmul",
    )(a, w)


def _merge_kernel(or_ref, on_ref, wr_ref, wn_ref, g0_ref, g1_ref, o_ref):
    yr = jnp.dot(or_ref[...], wr_ref[...], preferred_element_type=F32)
    yn = jnp.dot(on_ref[...], wn_ref[...], preferred_element_type=F32)
    o_ref[...] = (_sigmoid(g0_ref[...]) * yr + _sigmoid(g1_ref[...]) * yn).astype(o_ref.dtype)


def _merge(o_r, o_n, w_r, w_n, p_g, tm):
    m = o_r.shape[0]
    tn = 1024
    nb = D_MODEL // tn
    return pl.pallas_call(
        _merge_kernel,
        grid=(m // tm, nb),
        in_specs=[pl.BlockSpec((tm, C_R), lambda i, j: (i, 0)), pl.BlockSpec((tm, C_N), lambda i, j: (i, 0)),
                  pl.BlockSpec((C_R, tn), lambda i, j: (0, j)), pl.BlockSpec((C_N, tn), lambda i, j: (0, j)),
                  pl.BlockSpec((tm, tn), lambda i, j: (i, j)), pl.BlockSpec((tm, tn), lambda i, j: (i, j + nb))],
        out_specs=pl.BlockSpec((tm, tn), lambda i, j: (i, j)),
        out_shape=jax.ShapeDtypeStruct((m, D_MODEL), BF),
        compiler_params=_cparams("arbitrary", "arbitrary"),
        name="merge_branches",
    )(o_r, o_n, w_r, w_n, p_g, p_g)


def _proj_res_kernel(y_ref, w_ref, x_ref, g_ref, o_ref):
    o_ref[...] = x_ref[...] + g_ref[...] * jnp.dot(y_ref[...], w_ref[...], preferred_element_type=F32)


def _proj_residual(y, w, x, gate, tm, rpb):
    m = y.shape[0]
    tn = 1024
    r = gate.shape[1]
    rows, mods = _row_specs(m, tm, rpb)
    return pl.pallas_call(
        _proj_res_kernel,
        grid=(m // tm, D_MODEL // tn),
        in_specs=[pl.BlockSpec((tm, D_MODEL), lambda i, j: (i, 0)), pl.BlockSpec((D_MODEL, tn), lambda i, j: (0, j)),
                  rows(tn), mods(r, tn)],
        out_specs=rows(tn),
        out_shape=jax.ShapeDtypeStruct((m, D_MODEL), F32),
        compiler_params=_cparams("arbitrary", "arbitrary"),
        name="out_proj_residual",
    )(y, w, x, gate)


def _rwkv_features(p, p_prev, mu, w0, w2, a0, a2, g2, k_k, k_a):
    xm = p + (p_prev - p) * mu
    r = xm[:, :C_R]
    k = xm[:, C_R:2 * C_R]
    v = xm[:, 2 * C_R:3 * C_R]
    o = 3 * C_R
    wd = xm[:, o:o + LORA_W]
    ad = xm[:, o + LORA_W:o + LORA_W + LORA_A]
    gd = xm[:, o + LORA_W + LORA_A:]
    w_log = -_softplus(-(w0 + _dot(jnp.tanh(wd), w2))) - 0.5
    lw = -jnp.exp(w_log)
    a = _sigmoid(a0 + _dot(ad, a2))
    g = _dot(_sigmoid(gd), g2)
    kk = k * k_k
    k = k * (1.0 + (a - 1.0) * k_a)
    return r, k, v, lw, a, g, kk


def _rwkv_head_out(y, r_h, k_h, v_h, g_h, rk_h, lng_h, lnb_h):
    mean = jnp.mean(y, axis=-1, keepdims=True)
    yc = y - mean
    var = jnp.mean(yc * yc, axis=-1, keepdims=True)
    yn = yc * lax.rsqrt(var + LN_X_EPS) * lng_h + lnb_h
    bonus = jnp.sum(r_h * k_h * rk_h, axis=-1, keepdims=True) * v_h
    return (yn + bonus) * g_h


def _rwkv_chunk_kernel(pr_ref, mu_ref, w0_ref, w2_ref, a0_ref, a2_ref, g2_ref, kk_ref, ka_ref, rk_ref,
                       lng_ref, lnb_ref, o_ref, shift_ref, state_ref):
    c = pl.program_id(1)
    C = RW_CHUNK

    @pl.when(c == 0)
    def _():
        shift_ref[...] = jnp.zeros_like(shift_ref)
        state_ref[...] = jnp.zeros_like(state_ref)

    p = pr_ref[...]
    row = lax.broadcasted_iota(I32, (C, 1), 0)
    p_prev = jnp.where(row == 0, shift_ref[...], pltpu.roll(p, 1, axis=0))
    shift_ref[...] = p[C - 1:C, :]
    r, k, v, lw, a, g, kk_all = _rwkv_features(p, p_prev, mu_ref[...], w0_ref[...], w2_ref[...], a0_ref[...],
                                               a2_ref[...], g2_ref[...], kk_ref[...], ka_ref[...])
    cl = lw
    s = 1
    while s < C:
        cl = cl + jnp.where(row >= s, pltpu.roll(cl, s, axis=0), 0.0)
        s *= 2
    ti = lax.broadcasted_iota(I32, (C, C), 0)
    si = lax.broadcasted_iota(I32, (C, C), 1)
    strict = ti > si
    incl = ti >= si
    eye = (ti == si).astype(F32)
    heads = range(H_R)
    sls = [slice(h * HD_R, (h + 1) * HD_R) for h in heads]
    kkt, kh, bh, rt, kbar, bbar, vb, g_end = [], [], [], [], [], [], [], []
    for sl in sls:
        kk_h = kk_all[:, sl]
        nrm = jnp.sqrt(jnp.sum(kk_h * kk_h, axis=-1, keepdims=True))
        kk_h = kk_h / jnp.maximum(nrm, 1e-12)
        b_h = kk_h * a[:, sl]
        cl_h = cl[:, sl]
        cl_end = cl_h[C - 1:C, :]
        e_neg = jnp.exp(-cl_h)
        e_end = jnp.exp(cl_end - cl_h)
        kkt.append((kk_h * jnp.exp(cl_h - lw[:, sl])).astype(BF))
        kh.append((k[:, sl] * e_neg).astype(BF))
        bh.append((b_h * e_neg).astype(BF))
        rt.append((r[:, sl] * jnp.exp(cl_h)).astype(BF))
        kbar.append((k[:, sl] * e_end).astype(BF))
        bbar.append((b_h * e_end).astype(BF))
        vb.append(v[:, sl].astype(BF))
        g_end.append(jnp.exp(cl_end))
    lkk = [jnp.where(strict, _dot_nt(kkt[h], kh[h]), 0.0).astype(BF) for h in heads]
    nil = [jnp.where(strict, -_dot_nt(kkt[h], bh[h]), 0.0) for h in heads]
    grk = [jnp.where(incl, _dot_nt(rt[h], kh[h]), 0.0).astype(BF) for h in heads]
    grb = [jnp.where(incl, _dot_nt(rt[h], bh[h]), 0.0).astype(BF) for h in heads]
    tinv = [eye + n for n in nil]
    m = 2
    while m < C:
        nil = [_dot(n, n) for n in nil]
        tinv = [t + _dot(t, n) for t, n in zip(tinv, nil)]
        m *= 2
    s0 = [state_ref[h] for h in heads]
    x = [_dot_nt(kkt[h], s0[h]) + _dot(lkk[h], vb[h]) for h in heads]
    u = [_dot(tinv[h], x[h]).astype(BF) for h in heads]
    y = [_dot_nt(rt[h], s0[h]) + _dot(grk[h], vb[h]) - _dot(grb[h], u[h]) for h in heads]
    for h in heads:
        state_ref[h] = s0[h] * g_end[h] + _dot_tn(vb[h], kbar[h]) - _dot_tn(u[h], bbar[h])
    for h, sl in enumerate(sls):
        o_ref[:, sl] = _rwkv_head_out(y[h], r[:, sl], k[:, sl], v[:, sl], g[:, sl], rk_ref[:, sl], lng_ref[:, sl],
                                      lnb_ref[:, sl]).astype(o_ref.dtype)


def _rwkv_weights(mu, w0, w2, a0, a2, g2, k_k, k_a, r_k, ln_g, ln_b):
    row = lambda z: z.reshape(1, -1).astype(F32)
    return (row(mu), row(w0), w2.astype(BF), row(a0), a2.astype(BF), g2.astype(BF), row(k_k), row(k_a), row(r_k),
            row(ln_g), row(ln_b))


_RWKV_W_SHAPES = ((1, C_RIN), (1, C_R), (LORA_W, C_R), (1, C_R), (LORA_A, C_R), (LORA_G, C_R), (1, C_R), (1, C_R),
                  (1, C_R), (1, C_R), (1, C_R))


def _rwkv_prompt(pr, rw):
    b, t, _ = pr.shape
    C = RW_CHUNK
    full = lambda shp: pl.BlockSpec(shp, lambda i, j: (0,) * len(shp))
    return pl.pallas_call(
        _rwkv_chunk_kernel,
        grid=(b, t // C),
        in_specs=[pl.BlockSpec((None, C, C_RIN), lambda i, j: (i, j, 0))] + [full(s) for s in _RWKV_W_SHAPES],
        out_specs=[pl.BlockSpec((None, C, C_R), lambda i, j: (i, j, 0)),
                   pl.BlockSpec((None, 1, C_RIN), lambda i, j: (i, 0, 0)),
                   pl.BlockSpec((None, H_R, HD_R, HD_R), lambda i, j: (i, 0, 0, 0))],
        out_shape=[jax.ShapeDtypeStruct((b, t, C_R), BF),
                   jax.ShapeDtypeStruct((b, 1, C_RIN), F32),
                   jax.ShapeDtypeStruct((b, H_R, HD_R, HD_R), F32)],
        compiler_params=_cparams("arbitrary", "arbitrary"),
        name="rwkv_chunk",
    )(pr, *rw)


def _rwkv_step_kernel(pr_ref, prev_ref, s0_ref, mu_ref, w0_ref, w2_ref, a0_ref, a2_ref, g2_ref, kk_ref, ka_ref,
                      rk_ref, lng_ref, lnb_ref, o_ref, state_ref):
    nb = pr_ref.shape[0]
    r, k, v, lw, a, g, kk_all = _rwkv_features(pr_ref[...], prev_ref[...], mu_ref[...], w0_ref[...], w2_ref[...],
                                               a0_ref[...], a2_ref[...], g2_ref[...], kk_ref[...], ka_ref[...])
    decay = jnp.exp(lw)
    ii = lax.broadcasted_iota(I32, (HD_R, HD_R), 0)
    jj = lax.broadcasted_iota(I32, (HD_R, HD_R), 1)
    eye = ii == jj
    col = lambda z: jnp.sum(jnp.where(eye, z, 0.0), axis=1, keepdims=True)
    for bi in range(nb):
        for h in range(H_R):
            sl = slice(h * HD_R, (h + 1) * HD_R)
            rows = lambda z: z[bi:bi + 1, sl]
            r_h, k_h, v_h, a_h, w_h = rows(r), rows(k), rows(v), rows(a), rows(decay)
            kk_h = rows(kk_all)
            kk_h = kk_h / jnp.maximum(jnp.sqrt(jnp.sum(kk_h * kk_h, axis=-1, keepdims=True)), 1e-12)
            b_h = kk_h * a_h
            s0 = s0_ref[bi, h]
            sa = jnp.sum(s0 * (-kk_h), axis=1, keepdims=True)
            s1 = s0 * w_h + sa * b_h + col(v_h) * k_h
            state_ref[bi, h] = s1
            y_col = jnp.sum(s1 * r_h, axis=1, keepdims=True)
            y = jnp.sum(jnp.where(eye, y_col, 0.0), axis=0, keepdims=True)
            o_ref[bi:bi + 1, sl] = _rwkv_head_out(y, r_h, k_h, v_h, rows(g), rk_ref[:, sl], lng_ref[:, sl],
                                                  lnb_ref[:, sl]).astype(o_ref.dtype)


def _rwkv_step(pr, prev, s0, rw):
    b = pr.shape[0]
    return pl.pallas_call(
        _rwkv_step_kernel,
        out_shape=[jax.ShapeDtypeStruct((b, C_R), BF), jax.ShapeDtypeStruct((b, H_R, HD_R, HD_R), F32)],
        compiler_params=pltpu.CompilerParams(vmem_limit_bytes=VMEM_LIMIT_BYTES),
        name="rwkv_step",
    )(pr, prev, s0, *rw)


def _cmp_partial_kernel(*refs):
    x_refs, w_ref, o_ref = refs[:-2], refs[-2], refs[-1]
    x = x_refs[0][...] if len(x_refs) == 1 else jnp.concatenate([r[...] for r in x_refs], axis=0)
    half = N_KVG * HD_N
    for s in range(2):
        acc = jnp.zeros((x.shape[0], N_KVG * 2 * CMP_HID), F32)
        for p in range(STRIDE):
            o = p * KV_COLS + s * half
            acc = acc + _dot(x[:, o:o + half], w_ref[p, s])
        o_ref[:, s * N_KVG * 2 * CMP_HID:(s + 1) * N_KVG * 2 * CMP_HID] = acc


def _cmp_partial_paged_kernel(pt_ref, *refs):
    del pt_ref
    _cmp_partial_kernel(*refs)


def _cmp_weights(cmp_w1):
    w1r = cmp_w1.reshape(2, 2, STRIDE, HD_N, CMP_HID)
    w = jnp.transpose(w1r, (2, 0, 3, 1, 4))
    w = w.reshape(STRIDE, 2, 1, HD_N, 1, 2 * CMP_HID)
    eye = jnp.eye(N_KVG, dtype=w.dtype).reshape(1, 1, N_KVG, 1, N_KVG, 1)
    wbd = eye * w
    return wbd.reshape(STRIDE, 2, N_KVG * HD_N, N_KVG * 2 * CMP_HID).astype(BF)


def _cmp_partial_rows(x, wbd, tr):
    r = x.shape[0]
    n = 2 * N_KVG * 2 * CMP_HID
    return pl.pallas_call(
        _cmp_partial_kernel,
        grid=(r // tr,),
        in_specs=[pl.BlockSpec((tr, STRIDE * KV_COLS), lambda i: (i, 0)),
                  pl.BlockSpec(wbd.shape, lambda i: (0, 0, 0, 0))],
        out_specs=pl.BlockSpec((tr, n), lambda i: (i, 0)),
        out_shape=jax.ShapeDtypeStruct((r, n), F32),
        compiler_params=_cparams("arbitrary"),
        name="cmp_partial",
    )(x, wbd)


PAGES_PER_STEP = 8


def _cmp_partial_paged(cache, page_table, wbd):
    b, npg = page_table.shape
    cpp = PAGE_SIZE // STRIDE
    n = 2 * N_KVG * 2 * CMP_HID
    steps = npg // PAGES_PER_STEP

    def page_spec(kpg):
        return pl.BlockSpec((None, cpp, STRIDE * KV_COLS), lambda i, j, pt: (pt[i, j * PAGES_PER_STEP + kpg], 0, 0))

    grid_spec = pltpu.PrefetchScalarGridSpec(
        num_scalar_prefetch=1,
        grid=(b, steps),
        in_specs=[page_spec(kpg) for kpg in range(PAGES_PER_STEP)]
        + [pl.BlockSpec(wbd.shape, lambda i, j, pt: (0, 0, 0, 0))],
        out_specs=pl.BlockSpec((None, PAGES_PER_STEP * cpp, n), lambda i, j, pt: (i, j, 0)),
    )
    return pl.pallas_call(
        _cmp_partial_paged_kernel,
        grid_spec=grid_spec,
        out_shape=jax.ShapeDtypeStruct((b, npg * cpp, n), F32),
        compiler_params=_cparams("arbitrary", "arbitrary"),
        name="cmp_partial_paged",
    )(page_table, *([cache] * PAGES_PER_STEP), wbd)


def _cmp_finish_kernel(c_ref, cpos_ref, w2_ref, o_ref):
    c = c_ref[...]
    nrow = c.shape[0]
    c_next = pltpu.roll(c, nrow - 1, axis=0)
    for s in range(2):
        for gi in range(N_KVG):
            o = (s * N_KVG + gi) * 2 * CMP_HID
            hid = (c[:, o:o + CMP_HID] + cpos_ref[0:1, o:o + CMP_HID]
                   + c_next[:, o + CMP_HID:o + 2 * CMP_HID] + cpos_ref[1:2, o + CMP_HID:o + 2 * CMP_HID])
            oo = (s * N_KVG + gi) * HD_N
            o_ref[:, oo:oo + HD_N] = _dot(_gelu_tanh(hid), w2_ref[s])


def _cmp_finish(c, cpos, w2):
    b, nch, n = c.shape
    return pl.pallas_call(
        _cmp_finish_kernel,
        grid=(b,),
        in_specs=[pl.BlockSpec((None, nch, n), lambda i: (i, 0, 0)), pl.BlockSpec(cpos.shape, lambda i: (0, 0)),
                  pl.BlockSpec(w2.shape, lambda i: (0, 0, 0))],
        out_specs=pl.BlockSpec((None, nch, KV_COLS), lambda i: (i, 0, 0)),
        out_shape=jax.ShapeDtypeStruct((b, nch, KV_COLS), F32),
        compiler_params=_cparams("arbitrary"),
        name="cmp_finish",
    )(c, cpos, w2.astype(BF))


def _cmp_pos_rows(cmp_pos):
    pos = cmp_pos.reshape(2, STRIDE, 1, 1, HD_N)
    rows = jnp.broadcast_to(pos, (2, STRIDE, 2, N_KVG, HD_N)).reshape(2, STRIDE * KV_COLS)
    return jnp.concatenate([rows, jnp.zeros((6, STRIDE * KV_COLS), F32)], axis=0)


def _bias_table_kernel(dist_ref, rb_ref, o_ref):
    bucket = _t5_bucket(dist_ref[...])
    for h in range(H_N):
        out = jnp.zeros(bucket.shape, F32)
        for b in range(N_BUCKETS):
            out = jnp.where(bucket == b, rb_ref[b, h], out)
        o_ref[h] = out


def _bias_table(dist, rel_bias, tr):
    r, n = dist.shape
    return pl.pallas_call(
        _bias_table_kernel,
        grid=(r // tr,),
        in_specs=[pl.BlockSpec((tr, n), lambda i: (i, 0)),
                  pl.BlockSpec(memory_space=pltpu.SMEM)],
        out_specs=pl.BlockSpec((H_N, tr, n), lambda i: (0, i, 0)),
        out_shape=jax.ShapeDtypeStruct((H_N, r, n), F32),
        compiler_params=_cparams("arbitrary"),
        name="bias_table",
    )(dist, rel_bias)


def _softmax_rows(logits, valid):
    lm = jnp.where(valid, logits, NEG_INF)
    e = jnp.exp(lm - jnp.max(lm, axis=-1, keepdims=True))
    return e / jnp.sum(e, axis=-1, keepdims=True)


def _cmp_attn_kernel(q_ref, kv_ref, bias_ref, ovt_ref, o_ref, sel_ref):
    tq = q_ref.shape[0]
    ncp = kv_ref.shape[0]
    nsb = ovt_ref.shape[0]
    q0 = pl.program_id(1) * tq
    qpos = q0 + lax.broadcasted_iota(I32, (tq, 1), 0)
    cend = lax.broadcasted_iota(I32, (1, ncp), 1) * STRIDE + (L_CMP - 1)
    valid = (qpos >= cend) & (lax.broadcasted_iota(I32, (1, ncp), 1) < ncp - 1)
    validf = valid.astype(F32)
    q = q_ref[...] * (HD_N ** -0.5)
    blk = lax.broadcasted_iota(I32, (nsb, tq), 0)
    cur = (q0 + lax.broadcasted_iota(I32, (1, tq), 1)) // L_SEL
    forced = (blk == 0) | (blk == cur) | (blk == cur - 1)
    future = blk > cur
    for gi in range(N_KVG):
        kc = kv_ref[:, gi * HD_N:(gi + 1) * HD_N]
        vc = kv_ref[:, (N_KVG + gi) * HD_N:(N_KVG + gi + 1) * HD_N]
        pcs = jnp.zeros((tq, ncp), F32)
        for hl in range(HPG):
            h = gi * HPG + hl
            sl = slice(h * HD_N, (h + 1) * HD_N)
            pc = _softmax_rows(_dot_nt(q[:, sl], kc) + bias_ref[h], valid) * validf
            pcs = pcs + pc
            o_ref[:, sl] = _dot(pc, vc)
        imp = _dot_nt(ovt_ref[...], pcs)
        score = jnp.where(forced, FORCE, jnp.where(future, -FORCE, imp))
        rank = jnp.zeros((nsb, tq), F32)
        for i in range(nsb):
            si = score[i:i + 1, :]
            rank = rank + ((si > score) | ((si == score) & (i < blk))).astype(F32)
        sel_ref[gi] = (rank < N_TOP).astype(sel_ref.dtype)


def _cmp_sel_overlap_t(nc, ncp, nsb):
    s = np.arange(nc)[None, :] * STRIDE
    j = np.arange(nsb)[:, None] * L_SEL
    ov = np.clip(np.minimum(s + L_CMP, j + L_SEL) - np.maximum(s, j), 0, None) / L_CMP
    return np.pad(ov, ((0, 0), (0, ncp - nc))).astype(np.float32)


def _cmp_attn_prompt(p_n, kv_cmp, bias_c, t):
    b = kv_cmp.shape[0]
    ncp = kv_cmp.shape[1]
    nsb = t // L_SEL
    tq = ATT_TILE
    nqt = t // tq
    ovt = jnp.asarray(_cmp_sel_overlap_t(ncp - 1, ncp, nsb), BF)
    return pl.pallas_call(
        _cmp_attn_kernel,
        grid=(b, nqt),
        in_specs=[pl.BlockSpec((tq, C_N), lambda i, j: (i * nqt + j, 0)),
                  pl.BlockSpec((None, ncp, KV_COLS), lambda i, j: (i, 0, 0)),
                  pl.BlockSpec((H_N, tq, ncp), lambda i, j: (0, j, 0)),
                  pl.BlockSpec((nsb, ncp), lambda i, j: (0, 0))],
        out_specs=[pl.BlockSpec((tq, C_N), lambda i, j: (i * nqt + j, 0)),
                   pl.BlockSpec((None, N_KVG, nsb, tq), lambda i, j: (i, 0, 0, j))],
        out_shape=[jax.ShapeDtypeStruct((b * t, C_N), F32), jax.ShapeDtypeStruct((b, N_KVG, nsb, t), BF)],
        compiler_params=_cparams("arbitrary", "arbitrary"),
        name="nsa_cmp_select",
    )(p_n, kv_cmp, bias_c, ovt)


ATT_TK = 128
ATT_R = ATT_TILE // ATT_TK
N_SEL_OFFS = ATT_R + (MAX_DIST + ATT_TK - 1) // ATT_TK + 1
N_WIN_OFFS = ATT_R + WINDOW // ATT_TK
QA_COLS = HD_N + 32


def _swa_kernel(qa_ref, ks_ref, vs_ref, kw_ref, vw_ref, tabs_ref, tabw_ref, gn_ref, oc_ref, o_ref):
    tq = qa_ref.shape[1]
    qt = pl.program_id(2)
    top = ATT_R * qt + ATT_R - 1

    heads = range(HPG)
    tiles = range(ATT_R)

    def attend(k_ref, v_ref, tab_ref, lo, n_off):
        def body(kp, carry):
            ki = [ATT_R * kp + j for j in tiles]
            kt = [k_ref[i] for i in ki]
            vt = [v_ref[i] for i in ki]
            off = [jnp.minimum(top - i, n_off - 1) for i in ki]
            s = [[lax.dot_general(kt[j], qa_ref[hl], (((1,), (1,)), ((), ())), preferred_element_type=F32)
                  + tab_ref[off[j], hl * ATT_TK:(hl + 1) * ATT_TK, :] for j in tiles] for hl in heads]
            m_new = [functools.reduce(jnp.maximum, [carry[hl][0]] + [jnp.max(s[hl][j], axis=0, keepdims=True)
                                                                    for j in tiles]) for hl in heads]
            alpha = [jnp.exp(carry[hl][0] - m_new[hl]) for hl in heads]
            p = [[jnp.exp(s[hl][j] - m_new[hl]) for j in tiles] for hl in heads]
            l = [alpha[hl] * carry[hl][1] + sum(jnp.sum(p[hl][j], axis=0, keepdims=True) for j in tiles)
                 for hl in heads]
            acc = [alpha[hl] * carry[hl][2] + sum(jnp.dot(vt[j], p[hl][j].astype(BF), preferred_element_type=F32)
                                                  for j in tiles) for hl in heads]
            return tuple((m_new[hl], l[hl], acc[hl]) for hl in heads)

        init = tuple((jnp.full((1, tq), NEG_INF, F32), jnp.zeros((1, tq), F32), jnp.zeros((HD_N, tq), F32))
                     for _ in heads)
        res = lax.fori_loop(lo // ATT_R, qt + 1, body, init)
        return [acc / l for _, l, acc in res]

    o_sel = attend(ks_ref, vs_ref, tabs_ref, 0, N_SEL_OFFS)
    o_win = attend(kw_ref, vw_ref, tabw_ref, jnp.maximum(top + 1 - N_WIN_OFFS, 0), N_WIN_OFFS)
    gates = _sigmoid(gn_ref[...])
    gates_t = gates.T
    for hl in heads:
        sl = slice(hl * HD_N, (hl + 1) * HD_N)
        o_t = gates_t[3 * hl + 1:3 * hl + 2, :] * o_sel[hl] + gates_t[3 * hl + 2:3 * hl + 3, :] * o_win[hl]
        o_ref[:, sl] = (gates[:, 3 * hl:3 * hl + 1] * oc_ref[:, sl] + o_t.T).astype(o_ref.dtype)


def _swa_prompt(p_n, qa, ks, vs, kw, vw, tab_s, tab_w, o_cmp, t):
    b = qa.shape[0]
    tq = ATT_TILE
    nqt = t // tq
    nkt = t // ATT_TK
    gw = HPG * HD_N
    k_spec = pl.BlockSpec((None, None, nkt, ATT_TK, QA_COLS), lambda i, g, j: (i, g, 0, 0, 0))
    v_spec = pl.BlockSpec((None, None, nkt, HD_N, ATT_TK), lambda i, g, j: (i, g, 0, 0, 0))
    tab_spec = lambda n: pl.BlockSpec((n, HPG * ATT_TK, tq), lambda i, g, j: (0, g, 0))
    return pl.pallas_call(
        _swa_kernel,
        grid=(b, N_KVG, nqt),
        in_specs=[pl.BlockSpec((None, HPG, tq, QA_COLS), lambda i, g, j: (i, g, j, 0)),
                  k_spec, v_spec, k_spec, v_spec, tab_spec(N_SEL_OFFS), tab_spec(N_WIN_OFFS),
                  pl.BlockSpec((tq, GN_GROUP_COLS), lambda i, g, j: (i * nqt + j, NP_GN // GN_GROUP_COLS + g)),
                  pl.BlockSpec((tq, gw), lambda i, g, j: (i * nqt + j, g))],
        out_specs=pl.BlockSpec((tq, gw), lambda i, g, j: (i * nqt + j, g)),
        out_shape=jax.ShapeDtypeStruct((b * t, C_N), BF),
        compiler_params=_cparams("arbitrary", "arbitrary", "arbitrary"),
        name="nsa_sel_win",
    )(qa, ks, vs, kw, vw, tab_s, tab_w, p_n, o_cmp)


def _swa_operands(p_n, sel, b, t):
    nsb = t // L_SEL
    nkt = t // ATT_TK
    q = (p_n[:, :C_N] * (HD_N ** -0.5)).reshape(b, t, H_N, HD_N).transpose(0, 2, 1, 3)
    pen = jnp.where(jnp.transpose(sel, (0, 1, 3, 2)) > 0.5, 0.0, NEG_INF).astype(F32)
    pen = jnp.broadcast_to(pen[:, :, None], (b, N_KVG, HPG, t, nsb)).reshape(b, H_N, t, nsb)
    qa = jnp.concatenate([q, pen], axis=-1).astype(BF)

    def split(cols):
        kv = p_n[:, cols:cols + KV_COLS].reshape(b, t, 2, N_KVG, HD_N)
        k = jnp.transpose(kv[:, :, 0], (0, 2, 1, 3))
        v = jnp.transpose(kv[:, :, 1], (0, 2, 3, 1)).reshape(b, N_KVG, HD_N, nkt, ATT_TK)
        return k, jnp.transpose(v, (0, 1, 3, 2, 4)).astype(BF)

    onehot = jnp.asarray(np.arange(t)[:, None] // L_SEL == np.arange(nsb)[None, :], F32)
    k_s, v_s = split(NP_KVS)
    k_w, v_w = split(NP_KVW)
    ext = lambda k, e: jnp.concatenate([k, jnp.broadcast_to(e, (b, N_KVG, t, nsb))], axis=-1).astype(BF) \
        .reshape(b, N_KVG, nkt, ATT_TK, QA_COLS)
    return qa, ext(k_s, onehot), v_s, ext(k_w, jnp.zeros((t, nsb), F32)), v_w


def _sample_cmp_kernel(q_ref, kv_ref, rbt_ref, ov_ref, o_ref, idx_ref, *, past):
    ncp = kv_ref.shape[0]
    nsbp = ov_ref.shape[1]
    nsb = -(-(past + 1) // L_SEL)
    q = q_ref[...] * (HD_N ** -0.5)
    hrow = lax.broadcasted_iota(I32, (H_N, 1), 0)
    nidx = lax.broadcasted_iota(I32, (1, ncp), 1)
    valid = nidx < ncp - 1
    bias = _bias_rows(past - (nidx * STRIDE + (L_CMP - 1)), rbt_ref[...])
    logits = jnp.zeros((H_N, ncp), F32)
    for gi in range(N_KVG):
        lg = _dot_nt(q, kv_ref[:, gi * HD_N:(gi + 1) * HD_N])
        logits = jnp.where(hrow // HPG == gi, lg, logits)
    pc = _softmax_rows(logits + bias, valid) * valid.astype(F32)
    o = jnp.zeros((H_N, HD_N), F32)
    for gi in range(N_KVG):
        og = _dot(pc, kv_ref[:, (N_KVG + gi) * HD_N:(N_KVG + gi + 1) * HD_N])
        o = jnp.where(hrow // HPG == gi, og, o)
    o_ref[...] = o
    imp_h = _dot(pc, ov_ref[...])
    blk = lax.broadcasted_iota(I32, (8, nsbp), 1)
    grow = lax.broadcasted_iota(I32, (8, 1), 0)
    cur = past // L_SEL
    score = jnp.full((8, nsbp), -3e38, F32)
    for gi in range(N_KVG):
        imp = jnp.sum(jnp.where(hrow // HPG == gi, imp_h, 0.0), axis=0, keepdims=True)
        score = jnp.where(grow == gi, imp, score)
    forced = (blk == 0) | (blk == cur) | (blk == cur - 1)
    score = jnp.where(forced, FORCE, jnp.where(blk > cur, -FORCE, score))
    score = jnp.where((blk < nsb) & (grow < N_KVG), score, -3e38)
    lane = lax.broadcasted_iota(I32, (8, LANES), 1)
    picks = jnp.zeros((8, LANES), I32)
    for it in range(N_TOP):
        mx = jnp.max(score, axis=-1, keepdims=True)
        pick = jnp.min(jnp.where(score == mx, blk, nsbp), axis=-1, keepdims=True)
        picks = jnp.where(lane == it, pick, picks)
        score = jnp.where(blk == pick, -3e38, score)
    idx_ref[...] = picks


def _sample_cmp(q, kv_cmp, rel_bias, past):
    b, ncp, _ = kv_cmp.shape
    nc = ncp - 1
    nsb = -(-(past + 1) // L_SEL)
    nsbp = -(-nsb // LANES) * LANES
    s = np.arange(nc)[:, None] * STRIDE
    j = np.arange(nsb)[None, :] * L_SEL
    ov = np.clip(np.minimum(s + L_CMP, j + L_SEL) - np.maximum(s, j), 0, None) / L_CMP
    ov = np.pad(ov, ((0, ncp - nc), (0, nsbp - nsb))).astype(np.float32)
    return pl.pallas_call(
        functools.partial(_sample_cmp_kernel, past=past),
        grid=(b,),
        in_specs=[pl.BlockSpec((None, H_N, HD_N), lambda i: (i, 0, 0)),
                  pl.BlockSpec((None, ncp, KV_COLS), lambda i: (i, 0, 0)),
                  pl.BlockSpec((H_N, N_BUCKETS), lambda i: (0, 0)),
                  pl.BlockSpec((ncp, nsbp), lambda i: (0, 0))],
        out_specs=[pl.BlockSpec((None, H_N, HD_N), lambda i: (i, 0, 0)),
                   pl.BlockSpec((None, 8, LANES), lambda i: (i, 0, 0))],
        out_shape=[jax.ShapeDtypeStruct((b, H_N, HD_N), F32), jax.ShapeDtypeStruct((b, 8, LANES), I32)],
        compiler_params=_cparams("arbitrary"),
        name="nsa_sample_cmp_select",
    )(q, kv_cmp, rel_bias.T, jnp.asarray(ov, BF))


def _block_copy_kernel(pg_ref, *refs):
    del pg_ref
    x_refs, o_ref = refs[:-1], refs[-1]
    for n, x_ref in enumerate(x_refs):
        o_ref[n] = x_ref[...]


def _gather_sel_pages(cache_t, page):
    rows, nslot = page.shape

    def slot_spec(n):
        return pl.BlockSpec((None, 2, None, HD_N, PAGE_SIZE), lambda i, pg: (pg[i, n], 0, i % N_KVG, 0, 0))

    return pl.pallas_call(
        _block_copy_kernel,
        grid_spec=pltpu.PrefetchScalarGridSpec(
            num_scalar_prefetch=1, grid=(rows,),
            in_specs=[slot_spec(n) for n in range(nslot)],
            out_specs=pl.BlockSpec((None, nslot, 2, HD_N, PAGE_SIZE), lambda i, pg: (i, 0, 0, 0, 0))),
        out_shape=jax.ShapeDtypeStruct((rows, nslot, 2, HD_N, PAGE_SIZE), cache_t.dtype),
        compiler_params=_cparams("arbitrary"),
        name="gather_sel_pages",
    )(page, *([cache_t] * nslot))


def _sample_swa_kernel(idx_ref, q_ref, blk_ref, win_ref, new_ref, rbt_ref, gate_ref, oc_ref, o_ref, *, past):
    bi = pl.program_id(0)
    q = q_ref[...] * (HD_N ** -0.5)
    rbt = rbt_ref[...]
    hrow = lax.broadcasted_iota(I32, (H_N, 1), 0)
    nk = N_TOP * PAGE_SIZE
    lane = lax.broadcasted_iota(I32, (1, nk), 1)
    new_blk = past // L_SEL
    bias_new = rbt[:, 0:1]
    gates = _sigmoid(gate_ref[...])
    nwin = win_ref.shape[0]
    wdist = nwin - lax.broadcasted_iota(I32, (1, nwin), 1)
    wbias = _bias_rows(wdist, rbt)
    wvalid = (wdist >= 0) & (wdist <= WINDOW)

    def with_new(logits, valid, weigh, k_new, v_new):
        l_new = jnp.sum(q * k_new, axis=-1, keepdims=True) + bias_new
        lm = jnp.where(valid, logits, NEG_INF)
        m = jnp.maximum(jnp.max(lm, axis=-1, keepdims=True), l_new)
        e = jnp.where(valid, jnp.exp(lm - m), 0.0)
        e_new = jnp.exp(l_new - m)
        den = jnp.sum(e, axis=-1, keepdims=True) + e_new
        return (weigh(e) + e_new * v_new) / den

    o_sel = jnp.zeros((H_N, HD_N), F32)
    o_win = jnp.zeros((H_N, HD_N), F32)
    bpp = PAGE_SIZE // L_SEL
    tok = lane % PAGE_SIZE
    for gi in range(N_KVG):
        ksl = slice(gi * HD_N, (gi + 1) * HD_N)
        vsl = slice((N_KVG + gi) * HD_N, (N_KVG + gi + 1) * HD_N)
        bid = jnp.zeros((1, nk), I32)
        for n in range(N_TOP):
            bid = jnp.where(lane // PAGE_SIZE == n, idx_ref[bi, gi, n], bid)
        dist = past - ((bid // bpp) * PAGE_SIZE + tok)
        valid = (bid != new_blk) & (tok // L_SEL == bid % bpp) & (dist >= 0)
        kt = jnp.concatenate([blk_ref[gi, n, 0] for n in range(N_TOP)], axis=1)
        vt = jnp.concatenate([blk_ref[gi, n, 1] for n in range(N_TOP)], axis=1)
        logits = _dot(q, kt) + _bias_rows(dist, rbt)
        og = with_new(logits, valid, lambda e, vt=vt: _dot_nt(e, vt), new_ref[0:1, ksl], new_ref[0:1, vsl])
        o_sel = jnp.where(hrow // HPG == gi, og, o_sel)
        logits = _dot_nt(q, win_ref[:, ksl]) + wbias
        og = with_new(logits, wvalid, lambda e, vsl=vsl: _dot(e, win_ref[:, vsl]), new_ref[1:2, ksl],
                      new_ref[1:2, vsl])
        o_win = jnp.where(hrow // HPG == gi, og, o_win)
    o_ref[...] = gates[:, 0:1] * oc_ref[...] + gates[:, 1:2] * o_sel + gates[:, 2:3] * o_win


def _sample_swa(idx, q, blocks, win, new_kv, rel_bias, gates, o_cmp, past):
    b = q.shape[0]
    w = win.shape[1]
    grid_spec = pltpu.PrefetchScalarGridSpec(
        num_scalar_prefetch=1, grid=(b,),
        in_specs=[pl.BlockSpec((None, H_N, HD_N), lambda i, ix: (i, 0, 0)),
                  pl.BlockSpec((None, N_KVG, N_TOP, 2, HD_N, PAGE_SIZE), lambda i, ix: (i, 0, 0, 0, 0, 0)),
                  pl.BlockSpec((None, w, KV_COLS), lambda i, ix: (i, 0, 0)),
                  pl.BlockSpec((None, 2, KV_COLS), lambda i, ix: (i, 0, 0)),
                  pl.BlockSpec((H_N, N_BUCKETS), lambda i, ix: (0, 0)),
                  pl.BlockSpec((None, H_N, 3), lambda i, ix: (i, 0, 0)),
                  pl.BlockSpec((None, H_N, HD_N), lambda i, ix: (i, 0, 0))],
        out_specs=pl.BlockSpec((None, H_N, HD_N), lambda i, ix: (i, 0, 0)))
    return pl.pallas_call(
        functools.partial(_sample_swa_kernel, past=past),
        grid_spec=grid_spec,
        out_shape=jax.ShapeDtypeStruct((b, H_N, HD_N), F32),
        compiler_params=_cparams("arbitrary"),
        name="nsa_sample_sel_win",
    )(idx, q, blocks, win, new_kv, rel_bias.T, gates, o_cmp)


ROUTER_COLS = LANES


def _router_kernel(x_ref, g_ref, sh_ref, sc_ref, w_ref, b_ref, h_ref, e_ref, wt_ref, rk_ref, cnt_ref):
    i = pl.program_id(0)
    tm = x_ref.shape[0]

    @pl.when(i == 0)
    def _():
        cnt_ref[...] = jnp.zeros_like(cnt_ref)

    h = (_rms(x_ref[...], g_ref[...]) * (1.0 + sc_ref[...]) + sh_ref[...]).astype(BF)
    h_ref[:, 0, :] = h
    logits = jnp.dot(h, w_ref[...], preferred_element_type=F32) + b_ref[...]
    lane = lax.broadcasted_iota(I32, (tm, ROUTER_COLS), 1)

    def top1(vals, ok):
        vm = jnp.where(ok, vals, -3e38)
        mx = jnp.max(vm, axis=-1, keepdims=True)
        return mx, jnp.min(jnp.where(ok & (vm == mx), lane, ROUTER_COLS), axis=-1, keepdims=True)

    isg = lane < N_EGROUPS
    pg = _softmax_rows(logits, isg)
    g_w, g_i = top1(pg, isg)
    ise = (lane >= N_EGROUPS) & ((lane - N_EGROUPS) // EXP_PER_GROUP == g_i)
    pe = _softmax_rows(logits, ise)
    w0, l0 = top1(pe, ise)
    w1, l1 = top1(pe, ise & (lane != l0))
    den = w0 + w1
    e0 = l0 - N_EGROUPS
    e1 = l1 - N_EGROUPS
    e_ref[...] = jnp.where(lane == 0, e0, jnp.where(lane == 1, e1, 0))
    wt_ref[...] = jnp.where(lane == 0, w0 / den * g_w, jnp.where(lane == 1, w1 / den * g_w, 0.0))
    oh0 = (lane == e0).astype(F32)
    oh1 = (lane == e1).astype(F32)
    cnt = oh0 + oh1
    ti = lax.broadcasted_iota(I32, (tm, tm), 0)
    si = lax.broadcasted_iota(I32, (tm, tm), 1)
    before = _dot((ti > si).astype(F32), cnt) + cnt_ref[...]
    r0 = jnp.sum(before * oh0, axis=-1, keepdims=True)
    r1 = jnp.sum(before * oh1, axis=-1, keepdims=True)
    rk_ref[...] = jnp.where(lane == 0, r0, jnp.where(lane == 1, r1, 0.0)).astype(I32)
    cnt_ref[...] = cnt_ref[...] + jnp.sum(cnt, axis=0, keepdims=True)


def _router(x, g, shift, scale, w_r, b_r, tm, rpb):
    m = x.shape[0]
    r = shift.shape[1]
    rows = lambda tn: pl.BlockSpec((tm, tn), lambda i: (i, 0))
    mods = pl.BlockSpec((None, r, D_MODEL), lambda i: ((i * tm) // rpb, 0, 0))
    small = lambda dt: jax.ShapeDtypeStruct((m, ROUTER_COLS), dt)
    return pl.pallas_call(
        _router_kernel,
        grid=(m // tm,),
        in_specs=[rows(D_MODEL), pl.BlockSpec((1, D_MODEL), lambda i: (0, 0)), mods, mods,
                  pl.BlockSpec((D_MODEL, ROUTER_COLS), lambda i: (0, 0)),
                  pl.BlockSpec((1, ROUTER_COLS), lambda i: (0, 0))],
        out_specs=[pl.BlockSpec((tm, 1, D_MODEL), lambda i: (i, 0, 0)),
                   rows(ROUTER_COLS), rows(ROUTER_COLS), rows(ROUTER_COLS),
                   pl.BlockSpec((1, ROUTER_COLS), lambda i: (0, 0))],
        out_shape=[jax.ShapeDtypeStruct((m, 1, D_MODEL), BF), small(I32), small(F32), small(I32),
                   jax.ShapeDtypeStruct((1, ROUTER_COLS), F32)],
        compiler_params=_cparams("arbitrary"),
        name="moe_router",
    )(x, g.reshape(1, D_MODEL), shift, scale, w_r, b_r)


def _row_gather_ring(src_hbm, buf, sems, groups, idx_now, idx_next):
    i = pl.program_id(0)
    last = pl.num_programs(0) - 1
    slot = i % 2
    total = sum(cnt for _, cnt, _ in groups)

    def start_all(idx, s):
        for first, cnt, k in groups:
            def issue(j, c):
                for u in range(2):
                    r = 2 * j + u
                    pltpu.make_async_copy(src_hbm.at[idx(k, r)], buf.at[s, first + r], sems.at[s]).start(priority=u)
                return c
            lax.fori_loop(0, cnt // 2, issue, 0)

    @pl.when(i == 0)
    def _():
        start_all(idx_now, slot)

    @pl.when(i < last)
    def _():
        start_all(idx_next, 1 - slot)

    assert total == buf.shape[1]
    pltpu.make_async_copy(buf.at[slot], buf.at[slot], sems.at[slot]).wait()
    return slot


def _expert_kernel(be_ref, rt_ref, rtn_ref, h_hbm, w1_ref, w3_ref, w2_ref, o_ref, xbuf, sems, w1b, w3b, w2b):
    i = pl.program_id(0)
    blk = xbuf.shape[1]
    slot = _row_gather_ring(h_hbm, xbuf, sems, ((0, blk, 0),), lambda k, r: rt_ref[0, r], lambda k, r: rtn_ref[0, r])

    @pl.when((i == 0) | (be_ref[i] != be_ref[jnp.maximum(i - 1, 0)]))
    def _():
        w1b[...] = w1_ref[...].astype(BF)
        w3b[...] = w3_ref[...].astype(BF)
        w2b[...] = w2_ref[...].astype(BF)

    x = xbuf[slot, :, 0, :]
    a = jnp.dot(x, w1b[...], preferred_element_type=F32)
    b = jnp.dot(x, w3b[...], preferred_element_type=F32)
    hid = a * _sigmoid(a) * b
    o_ref[:, 0, :] = jnp.dot(hid.astype(BF), w2b[...], preferred_element_type=F32)


def _experts(h2, row_tok, blk_exp, w1, w3, w2):
    nblk, _, blk = row_tok.shape
    idx_spec = lambda d: pl.BlockSpec((None, 1, blk), lambda i, be: (jnp.minimum(i + d, nblk - 1), 0, 0),
                                      memory_space=pltpu.SMEM)
    grid_spec = pltpu.PrefetchScalarGridSpec(
        num_scalar_prefetch=1, grid=(nblk,),
        in_specs=[idx_spec(0), idx_spec(1),
                  pl.BlockSpec(memory_space=pl.ANY),
                  pl.BlockSpec((None, D_MODEL, D_EXP), lambda i, be: (be[i], 0, 0)),
                  pl.BlockSpec((None, D_MODEL, D_EXP), lambda i, be: (be[i], 0, 0)),
                  pl.BlockSpec((None, D_EXP, D_MODEL), lambda i, be: (be[i], 0, 0))],
        out_specs=pl.BlockSpec((blk, 1, D_MODEL), lambda i, be: (i, 0, 0)),
        scratch_shapes=[pltpu.VMEM((2, blk, 1, D_MODEL), BF), pltpu.SemaphoreType.DMA((2,)),
                        pltpu.VMEM((D_MODEL, D_EXP), BF), pltpu.VMEM((D_MODEL, D_EXP), BF),
                        pltpu.VMEM((D_EXP, D_MODEL), BF)])
    return pl.pallas_call(
        _expert_kernel,
        grid_spec=grid_spec,
        out_shape=jax.ShapeDtypeStruct((nblk * blk, 1, D_MODEL), F32),
        compiler_params=_cparams("arbitrary"),
        name="moe_experts",
    )(blk_exp, row_tok, row_tok, h2, w1, w3, w2)


def _final_kernel(x_ref, g_ref, dest_ref, destn_ref, ys_hbm, wt_ref, nf_ref, o_ref, ybuf, sems):
    tm = x_ref.shape[0]
    groups = tuple((k * tm, tm, k) for k in range(TOP_K))
    slot = _row_gather_ring(ys_hbm, ybuf, sems, groups, lambda k, r: dest_ref[k, r], lambda k, r: destn_ref[k, r])
    wt = wt_ref[...]
    moe = wt[:, 0:1] * ybuf[slot, 0:tm, 0, :] + wt[:, 1:2] * ybuf[slot, tm:2 * tm, 0, :]
    o_ref[...] = _rms(x_ref[...] + g_ref[...] * moe, nf_ref[...])


def _final(x, gate, ys, dest, wts, norm_f, tm, rpb):
    m = x.shape[0]
    r = gate.shape[1]
    nt = m // tm
    rows = lambda tn: pl.BlockSpec((tm, tn), lambda i: (i, 0))
    idx_spec = lambda d: pl.BlockSpec((None, TOP_K, tm), lambda i: (jnp.minimum(i + d, nt - 1), 0, 0),
                                      memory_space=pltpu.SMEM)
    return pl.pallas_call(
        _final_kernel,
        grid=(nt,),
        in_specs=[rows(D_MODEL), pl.BlockSpec((None, r, D_MODEL), lambda i: ((i * tm) // rpb, 0, 0)),
                  idx_spec(0), idx_spec(1),
                  pl.BlockSpec(memory_space=pl.ANY),
                  rows(ROUTER_COLS), pl.BlockSpec((1, D_MODEL), lambda i: (0, 0))],
        out_specs=rows(D_MODEL),
        out_shape=jax.ShapeDtypeStruct((m, D_MODEL), F32),
        scratch_shapes=[pltpu.VMEM((2, TOP_K * tm, 1, D_MODEL), F32), pltpu.SemaphoreType.DMA((2,))],
        compiler_params=_cparams("arbitrary"),
        name="moe_combine_final_norm",
    )(x, gate, dest, dest, ys, wts, norm_f.reshape(1, D_MODEL))


def _moe_and_final(x1, g2, shift, scale, gate, w_r, b_r, exp_w1, exp_w3, exp_w2, norm_f, tm, rpb, blk):
    m = x1.shape[0]
    h2, eid, wts, rank, counts = _router(x1, g2, shift, scale, w_r, b_r, tm, rpb)
    counts = counts[0, :N_EXP].astype(I32)
    padded = (counts + blk - 1) // blk * blk
    pend = jnp.cumsum(padded)
    pstart = pend - padded
    n_blocks = -(-(m * TOP_K) // blk) + N_EXP
    starts = jnp.arange(n_blocks, dtype=I32)[:, None] * blk
    blk_exp = jnp.minimum(jnp.sum((pend[None, :] <= starts).astype(I32), axis=1), N_EXP - 1)
    e = eid[:, :TOP_K]
    dest = pstart[e] + rank[:, :TOP_K]
    tok = jnp.broadcast_to(jnp.arange(m, dtype=I32)[:, None], (m, TOP_K))
    row_tok = jnp.zeros((n_blocks * blk,), I32).at[dest.reshape(-1)].set(tok.reshape(-1))
    ys = _experts(h2, row_tok.reshape(n_blocks, 1, blk), blk_exp, exp_w1, exp_w3, exp_w2)
    dest_t = jnp.transpose(dest.reshape(m // tm, tm, TOP_K), (0, 2, 1))
    return _final(x1, gate, ys, dest_t, wts, norm_f, tm, rpb)


def _pack_in_proj(w_in):
    o = C_RIN
    w_r = w_in[:, :o]
    w_q = w_in[:, o:o + C_N + 3 * KV_COLS]
    o += C_N + 3 * KV_COLS
    w_gn = w_in[:, o:o + 3 * H_N].reshape(D_MODEL, N_KVG, 3 * HPG)
    w_gn = jnp.pad(w_gn, ((0, 0), (0, 0), (0, GN_GROUP_COLS - 3 * HPG))).reshape(D_MODEL, N_KVG * GN_GROUP_COLS)
    o += 3 * H_N
    w_gm = w_in[:, o:]
    return w_r.astype(BF), jnp.concatenate([w_q, w_gn], axis=1).astype(BF), w_gm.astype(BF)


def _prompt_bias_tables(rel_bias, t):
    tq, tk = ATT_TILE, ATT_TK
    i = np.arange(tq)[None, :]
    j = np.arange(tk)[:, None]
    dist = np.stack([tk * (o - (ATT_R - 1)) + i - j for o in range(N_WIN_OFFS)]).astype(np.int32)
    raw = _bias_table(jnp.asarray(dist.reshape(N_WIN_OFFS * tk, tq)), rel_bias, tk)
    raw = jnp.transpose(raw.reshape(H_N, N_WIN_OFFS, tk, tq), (1, 0, 2, 3))
    ok_w = jnp.asarray((dist >= 0) & (dist <= WINDOW))[:, None]
    ok_s = jnp.asarray(dist[:N_SEL_OFFS] >= 0)[:, None]
    tab_w = jnp.where(ok_w, raw, NEG_INF).reshape(N_WIN_OFFS, H_N * tk, tq)
    tab_s = jnp.where(ok_s, raw[:N_SEL_OFFS], NEG_INF).reshape(N_SEL_OFFS, H_N * tk, tq)
    nc = (t - L_CMP) // STRIDE + 1
    ncp = nc + 1
    dc = (np.arange(t)[:, None] - (np.arange(ncp)[None, :] * STRIDE + L_CMP - 1)).astype(np.int32)
    return tab_s, tab_w, _bias_table(jnp.asarray(dc), rel_bias, tq)


def kernel(x_prompt, x_sample, c_prompt, c_sample, cache_cmp_kv, cache_sel_kv, state_win_kv, state_rwkv_shift,
           state_rwkv_wkv, page_table, rel_bias, norm_f, norm1, norm2, w_ada, b_ada, w_in, rwkv_mu, rwkv_w0, rwkv_w2,
           rwkv_a0, rwkv_a2, rwkv_g2, rwkv_kk, rwkv_ka, rwkv_rk, rwkv_ln_g, rwkv_ln_b, cmp_pos, cmp_w1, cmp_w2,
           w_o_rwkv, w_o_nsa, w_out, router_wg, router_bg, router_we, router_be, exp_w1, exp_w3, exp_w2):
    bp, t, _ = x_prompt.shape
    bs = x_sample.shape[0]
    mp = bp * t
    past = page_table.shape[1] * PAGE_SIZE

    nrow = -(-(bp + bs) // 8) * 8
    c_all = jnp.concatenate([c_prompt, c_sample, jnp.zeros((nrow - bp - bs, D_MODEL), F32)], axis=0)
    mod = _ada(c_all, w_ada[0], b_ada[0]).reshape(nrow, 6, D_MODEL)
    mod_p = [mod[:bp, i][:, None, :] for i in range(6)]
    mod_s = [mod[bp:bp + bs, i][None] for i in range(6)]

    w_r, w_n, w_gm = _pack_in_proj(w_in[0])
    rw = _rwkv_weights(rwkv_mu[0], rwkv_w0[0], rwkv_w2[0], rwkv_a0[0], rwkv_a2[0], rwkv_g2[0], rwkv_kk[0],
                       rwkv_ka[0], rwkv_rk[0], rwkv_ln_g[0], rwkv_ln_b[0])
    wbd = _cmp_weights(cmp_w1[0])
    cpos = _cmp_partial_rows(_cmp_pos_rows(cmp_pos[0]), wbd, 8)
    wo_r, wo_n, wo = w_o_rwkv[0].astype(BF), w_o_nsa[0].astype(BF), w_out[0].astype(BF)
    w_router = jnp.pad(jnp.concatenate([router_wg[0], router_we[0]], axis=1),
                       ((0, 0), (0, ROUTER_COLS - N_EGROUPS - N_EXP))).astype(BF)
    b_router = jnp.pad(jnp.concatenate([router_bg[0], router_be[0]]), (0, ROUTER_COLS - N_EGROUPS - N_EXP))[None]

    tm = 512
    xp = x_prompt.reshape(mp, D_MODEL)
    h = _norm_mod(xp, norm1[0], mod_p[0], mod_p[1], tm, t)
    p_r = _matmul(h, w_r, tm, C_RIN // 2)
    p_n = _matmul(h, w_n, tm, NP_COLS // 2)
    p_g = _matmul(h, w_gm, tm, 2048)
    o_r, shift_p, wkv_p = _rwkv_prompt(p_r.reshape(bp, t, C_RIN), rw)
    kvc = p_n[:, NP_KVC:NP_KVC + KV_COLS]
    kvs = p_n[:, NP_KVS:NP_KVS + KV_COLS]
    kvw = p_n[:, NP_KVW:NP_KVW + KV_COLS]
    nch = t // STRIDE
    c_part = _cmp_partial_rows(kvc.reshape(bp * nch, STRIDE * KV_COLS), wbd, nch)
    kv_cmp = _cmp_finish(c_part.reshape(bp, nch, -1), cpos, cmp_w2[0])
    tab_s, tab_w, bias_c = _prompt_bias_tables(rel_bias, t)
    o_cmp, sel = _cmp_attn_prompt(p_n, kv_cmp, bias_c, t)
    o_n = _swa_prompt(p_n, *_swa_operands(p_n, sel, bp, t), tab_s, tab_w, o_cmp, t)
    y = _merge(o_r.reshape(mp, C_R), o_n, wo_r, wo_n, p_g, tm)
    x1 = _proj_residual(y, wo, xp, mod_p[2], tm, t)
    y_prompt = _moe_and_final(x1, norm2[0], mod_p[3], mod_p[4], mod_p[5], w_router, b_router, exp_w1[0], exp_w3[0],
                              exp_w2[0], norm_f, tm, t, 128).reshape(bp, t, D_MODEL)
    kv_shape = (1, bp, t, 2, N_KVG, HD_N)
    wlen = min(WINDOW, t)
    win_p = kvw.reshape(bp, t, KV_COLS)[:, t - wlen:].reshape(1, bp, wlen, 2, N_KVG, HD_N)

    xs = x_sample.reshape(bs, D_MODEL)
    hs = _norm_mod(xs, norm1[0], mod_s[0], mod_s[1], bs, bs)
    ps_r = _matmul(hs, w_r, bs, C_RIN // 2)
    ps_n = _matmul(hs, w_n, bs, NP_COLS // 2)
    ps_g = _matmul(hs, w_gm, bs, 2048)
    os_r, wkv_s = _rwkv_step(ps_r, state_rwkv_shift[0], state_rwkv_wkv[0], rw)
    kvc_s = ps_n[:, NP_KVC:NP_KVC + KV_COLS]
    kvs_s = ps_n[:, NP_KVS:NP_KVS + KV_COLS]
    kvw_s = ps_n[:, NP_KVW:NP_KVW + KV_COLS]
    n_pool = cache_cmp_kv.shape[1]
    cpp = PAGE_SIZE // STRIDE
    cs_part = _cmp_partial_paged(cache_cmp_kv[0].reshape(n_pool, cpp, STRIDE * KV_COLS), page_table, wbd)
    kv_cmp_s = _cmp_finish(cs_part, cpos, cmp_w2[0])
    q_s = ps_n[:, :C_N].reshape(bs, H_N, HD_N)
    o_cmp_s, picks = _sample_cmp(q_s, kv_cmp_s, rel_bias, past)
    idx = picks[:, :N_KVG, :N_TOP]
    bpp = PAGE_SIZE // L_SEL
    npb = past // L_SEL
    idc = jnp.minimum(idx, npb - 1)
    page = jnp.take_along_axis(page_table, (idc // bpp).reshape(bs, -1), axis=1).reshape(bs * N_KVG, N_TOP)
    cache_t = jnp.transpose(cache_sel_kv[0], (0, 2, 3, 4, 1))
    blocks = _gather_sel_pages(cache_t, page).reshape(bs, N_KVG, N_TOP, 2, HD_N, PAGE_SIZE)
    win_buf = state_win_kv[0].reshape(bs, -1, KV_COLS)
    gates_s = ps_n[:, NP_GN:].reshape(bs, N_KVG, GN_GROUP_COLS)[:, :, :3 * HPG].reshape(bs, H_N, 3)
    new_kv = jnp.stack([kvs_s, kvw_s], axis=1)
    os_n = _sample_swa(idx, q_s, blocks, win_buf, new_kv, rel_bias, gates_s, o_cmp_s, past)
    ys = _merge(os_r, os_n.reshape(bs, C_N).astype(BF), wo_r, wo_n, ps_g, bs)
    xs1 = _proj_residual(ys, wo, xs, mod_s[2], bs, bs)
    y_sample = _moe_and_final(xs1, norm2[0], mod_s[3], mod_s[4], mod_s[5], w_router, b_router, exp_w1[0], exp_w3[0],
                              exp_w2[0], norm_f, bs, bs, 16).reshape(bs, 1, D_MODEL)
    kv1 = (1, bs, 1, 2, N_KVG, HD_N)
    wbuf = win_buf.shape[1]
    win_s = jnp.concatenate([win_buf, kvw_s[:, None, :]], axis=1)[:, -wbuf:].reshape(1, bs, wbuf, 2, N_KVG, HD_N)

    return (y_prompt, y_sample,
            kvc.reshape(kv_shape), kvc_s.reshape(kv1),
            kvs.reshape(kv_shape), kvs_s.reshape(kv1),
            win_p, win_s,
            shift_p.reshape(1, bp, C_RIN), ps_r.reshape(1, bs, C_RIN),
            wkv_p[None], wkv_s[None])
```

```python
import functools
import math

import numpy as np
import jax
import jax.numpy as jnp
from jax import lax
from jax.experimental import pallas as pl
from jax.experimental.pallas import tpu as pltpu

D_MODEL = 2048
PAGE_SIZE = 128
H_R, HD_R = 16, 64
C_R = H_R * HD_R
LORA_W, LORA_A, LORA_G = 64, 64, 128
C_RIN = 3 * C_R + LORA_W + LORA_A + LORA_G
LN_X_EPS = 64e-5
H_N, HD_N, N_KVG = 16, 64, 4
HPG = H_N // N_KVG
C_N = H_N * HD_N
KV_COLS = 2 * N_KVG * HD_N
L_CMP, STRIDE, CMP_HID = 32, 16, 64
L_SEL, N_TOP, WINDOW = 64, 16, 512
N_BUCKETS, MAX_DIST = 32, 128
N_EGROUPS, EXP_PER_GROUP = 4, 8
N_EXP = N_EGROUPS * EXP_PER_GROUP
TOP_K, D_EXP = 2, 512
RMS_EPS = 1e-6
NEG_INF = -1e30
FORCE = 1e9

BF = jnp.bfloat16
F32 = jnp.float32
I32 = jnp.int32

VMEM_LIMIT_BYTES = 56 * 1024 * 1024
LANES = 128
RW_CHUNK = 32
ATT_TILE = 256
GN_GROUP_COLS = 128
NP_Q, NP_KVC, NP_KVS, NP_KVW, NP_GN = 0, C_N, C_N + KV_COLS, C_N + 2 * KV_COLS, C_N + 3 * KV_COLS
NP_COLS = NP_GN + N_KVG * GN_GROUP_COLS


def _cparams(*sem):
    return pltpu.CompilerParams(dimension_semantics=sem, vmem_limit_bytes=VMEM_LIMIT_BYTES)


def _dot(a, b):
    return jnp.dot(a.astype(BF), b.astype(BF), preferred_element_type=F32)


def _dot_nt(a, b):
    return lax.dot_general(a.astype(BF), b.astype(BF), (((1,), (1,)), ((), ())), preferred_element_type=F32)


def _dot_tn(a, b):
    return lax.dot_general(a.astype(BF), b.astype(BF), (((0,), (0,)), ((), ())), preferred_element_type=F32)


def _softplus(x):
    return jnp.maximum(x, 0.0) + jnp.log1p(jnp.exp(-jnp.abs(x)))


def _sigmoid(x):
    return 1.0 / (1.0 + jnp.exp(-x))


def _gelu_tanh(x):
    return 0.5 * x * (1.0 + jnp.tanh(math.sqrt(2.0 / math.pi) * (x + 0.044715 * x * x * x)))


def _t5_bucket(dist):
    n = jnp.maximum(dist, 0)
    max_exact = N_BUCKETS // 2
    nf = jnp.maximum(n, 1).astype(F32)
    large = max_exact + (jnp.log(nf / max_exact) / math.log(MAX_DIST / max_exact)
                         * (N_BUCKETS - max_exact)).astype(I32)
    large = jnp.minimum(large, N_BUCKETS - 1)
    return jnp.where(n < max_exact, n, large)


def _bias_rows(dist, rbt):
    bucket = _t5_bucket(dist)
    out = jnp.zeros((rbt.shape[0], dist.shape[1]), F32)
    for b in range(N_BUCKETS):
        out = jnp.where(bucket == b, rbt[:, b:b + 1], out)
    return out


def _ada_kernel(c_ref, w_ref, b_ref, o_ref):
    o_ref[...] = _dot(c_ref[...], w_ref[...]) + b_ref[...]


def _ada(c, w_ada, b_ada):
    r = c.shape[0]
    n = w_ada.shape[1]
    tn = 1024
    return pl.pallas_call(
        _ada_kernel,
        grid=(n // tn,),
        in_specs=[pl.BlockSpec((r, D_MODEL), lambda j: (0, 0)),
                  pl.BlockSpec((D_MODEL, tn), lambda j: (0, j)),
                  pl.BlockSpec((1, tn), lambda j: (0, j))],
        out_specs=pl.BlockSpec((r, tn), lambda j: (0, j)),
        out_shape=jax.ShapeDtypeStruct((r, n), F32),
        compiler_params=_cparams("arbitrary"),
        name="ada_mod",
    )(c, w_ada, b_ada.reshape(1, n))


def _rms(x, g):
    return x * lax.rsqrt(jnp.mean(x * x, axis=-1, keepdims=True) + RMS_EPS) * g


def _norm_mod_kernel(x_ref, g_ref, sh_ref, sc_ref, o_ref):
    o_ref[...] = (_rms(x_ref[...], g_ref[...]) * (1.0 + sc_ref[...]) + sh_ref[...]).astype(o_ref.dtype)


def _row_specs(m, tm, rpb):
    del m
    return (lambda tn: pl.BlockSpec((tm, tn), lambda i, j: (i, j)),
            lambda r, tn: pl.BlockSpec((None, r, tn), lambda i, j: ((i * tm) // rpb, 0, j)))


def _norm_mod(x, g, shift, scale, tm, rpb):
    m = x.shape[0]
    r = shift.shape[1]
    rows, mods = _row_specs(m, tm, rpb)
    return pl.pallas_call(
        _norm_mod_kernel,
        grid=(m // tm, 1),
        in_specs=[rows(D_MODEL), pl.BlockSpec((1, D_MODEL), lambda i, j: (0, 0)), mods(r, D_MODEL), mods(r, D_MODEL)],
        out_specs=rows(D_MODEL),
        out_shape=jax.ShapeDtypeStruct((m, D_MODEL), BF),
        compiler_params=_cparams("arbitrary", "arbitrary"),
        name="norm_mod",
    )(x, g.reshape(1, D_MODEL), shift, scale)


def _mm_kernel(a_ref, w_ref, o_ref):
    o_ref[...] = jnp.dot(a_ref[...], w_ref[...], preferred_element_type=F32).astype(o_ref.dtype)


def _matmul(a, w, tm, tn, out_dtype=F32):
    m, k = a.shape
    n = w.shape[1]
    return pl.pallas_call(
        _mm_kernel,
        grid=(m // tm, n // tn),
        in_specs=[pl.BlockSpec((tm, k), lambda i, j: (i, 0)), pl.BlockSpec((k, tn), lambda i, j: (0, j))],
        out_specs=pl.BlockSpec((tm, tn), lambda i, j: (i, j)),
        out_shape=jax.ShapeDtypeStruct((m, n), out_dtype),
        compiler_params=_cparams("arbitrary", "arbitrary"),
        name="matmul",
    )(a, w)


def _merge_kernel(or_ref, on_ref, wr_ref, wn_ref, g0_ref, g1_ref, o_ref):
    yr = jnp.dot(or_ref[...], wr_ref[...], preferred_element_type=F32)
    yn = jnp.dot(on_ref[...], wn_ref[...], preferred_element_type=F32)
    o_ref[...] = (_sigmoid(g0_ref[...]) * yr + _sigmoid(g1_ref[...]) * yn).astype(o_ref.dtype)


def _merge(o_r, o_n, w_r, w_n, p_g, tm):
    m = o_r.shape[0]
    tn = 1024
    nb = D_MODEL // tn
    return pl.pallas_call(
        _merge_kernel,
        grid=(m // tm, nb),
        in_specs=[pl.BlockSpec((tm, C_R), lambda i, j: (i, 0)), pl.BlockSpec((tm, C_N), lambda i, j: (i, 0)),
                  pl.BlockSpec((C_R, tn), lambda i, j: (0, j)), pl.BlockSpec((C_N, tn), lambda i, j: (0, j)),
                  pl.BlockSpec((tm, tn), lambda i, j: (i, j)), pl.BlockSpec((tm, tn), lambda i, j: (i, j + nb))],
        out_specs=pl.BlockSpec((tm, tn), lambda i, j: (i, j)),
        out_shape=jax.ShapeDtypeStruct((m, D_MODEL), BF),
        compiler_params=_cparams("arbitrary", "arbitrary"),
        name="merge_branches",
    )(o_r, o_n, w_r, w_n, p_g, p_g)


def _proj_res_kernel(y_ref, w_ref, x_ref, g_ref, o_ref):
    o_ref[...] = x_ref[...] + g_ref[...] * jnp.dot(y_ref[...], w_ref[...], preferred_element_type=F32)


def _proj_residual(y, w, x, gate, tm, rpb):
    m = y.shape[0]
    tn = 1024
    r = gate.shape[1]
    rows, mods = _row_specs(m, tm, rpb)
    return pl.pallas_call(
        _proj_res_kernel,
        grid=(m // tm, D_MODEL // tn),
        in_specs=[pl.BlockSpec((tm, D_MODEL), lambda i, j: (i, 0)), pl.BlockSpec((D_MODEL, tn), lambda i, j: (0, j)),
                  rows(tn), mods(r, tn)],
        out_specs=rows(tn),
        out_shape=jax.ShapeDtypeStruct((m, D_MODEL), F32),
        compiler_params=_cparams("arbitrary", "arbitrary"),
        name="out_proj_residual",
    )(y, w, x, gate)


def _rwkv_features(p, p_prev, mu, w0, w2, a0, a2, g2, k_k, k_a):
    xm = p + (p_prev - p) * mu
    r = xm[:, :C_R]
    k = xm[:, C_R:2 * C_R]
    v = xm[:, 2 * C_R:3 * C_R]
    o = 3 * C_R
    wd = xm[:, o:o + LORA_W]
    ad = xm[:, o + LORA_W:o + LORA_W + LORA_A]
    gd = xm[:, o + LORA_W + LORA_A:]
    w_log = -_softplus(-(w0 + _dot(jnp.tanh(wd), w2))) - 0.5
    lw = -jnp.exp(w_log)
    a = _sigmoid(a0 + _dot(ad, a2))
    g = _dot(_sigmoid(gd), g2)
    kk = k * k_k
    k = k * (1.0 + (a - 1.0) * k_a)
    return r, k, v, lw, a, g, kk


def _rwkv_head_out(y, r_h, k_h, v_h, g_h, rk_h, lng_h, lnb_h):
    mean = jnp.mean(y, axis=-1, keepdims=True)
    yc = y - mean
    var = jnp.mean(yc * yc, axis=-1, keepdims=True)
    yn = yc * lax.rsqrt(var + LN_X_EPS) * lng_h + lnb_h
    bonus = jnp.sum(r_h * k_h * rk_h, axis=-1, keepdims=True) * v_h
    return (yn + bonus) * g_h


def _head_sums(x, ones2):
    nt = x.shape[1] // LANES
    xs = jnp.concatenate([x[:, j * LANES:(j + 1) * LANES] for j in range(nt)], axis=0)
    hi = xs.astype(BF)
    lo = (xs - hi.astype(F32)).astype(BF)
    s = jnp.dot(hi, ones2, preferred_element_type=F32) + jnp.dot(lo, ones2, preferred_element_type=F32)
    r = x.shape[0]
    return jnp.concatenate([s[j * r:(j + 1) * r] for j in range(nt)], axis=1)


def _rwkv_chunk_kernel(pr_ref, mu_ref, w0_ref, w2_ref, a0_ref, a2_ref, g2_ref, kk_ref, ka_ref, rk_ref,
                       lng_ref, lnb_ref, ones_ref, o_ref, shift_ref, state_ref, y_ref):
    c = pl.program_id(1)
    C = RW_CHUNK

    @pl.when(c == 0)
    def _():
        shift_ref[...] = jnp.zeros_like(shift_ref)
        state_ref[...] = jnp.zeros_like(state_ref)

    p = pr_ref[...]
    row = lax.broadcasted_iota(I32, (C, 1), 0)
    p_prev = jnp.where(row == 0, shift_ref[...], pltpu.roll(p, 1, axis=0))
    shift_ref[...] = p[C - 1:C, :]
    r, k, v, lw, a, g, kk_all = _rwkv_features(p, p_prev, mu_ref[...], w0_ref[...], w2_ref[...], a0_ref[...],
                                               a2_ref[...], g2_ref[...], kk_ref[...], ka_ref[...])
    cl = lw
    s = 1
    while s < C:
        cl = cl + jnp.where(row >= s, pltpu.roll(cl, s, axis=0), 0.0)
        s *= 2
    ti = lax.broadcasted_iota(I32, (C, C), 0)
    si = lax.broadcasted_iota(I32, (C, C), 1)
    strict = ti > si
    incl = ti >= si
    eye = (ti == si).astype(F32)
    heads = range(H_R)
    sls = [slice(h * HD_R, (h + 1) * HD_R) for h in heads]
    ones2 = ones_ref[...]
    kk_n = kk_all / jnp.maximum(jnp.sqrt(_head_sums(kk_all * kk_all, ones2)), 1e-12)
    b_all = kk_n * a
    cl_end = cl[C - 1:C, :]
    e_neg = jnp.exp(-cl)
    e_end = jnp.exp(cl_end - cl)
    g_end_all = jnp.exp(cl_end)
    per_head = lambda z: [z[:, sl] for sl in sls]
    kkt = per_head((kk_n * jnp.exp(cl - lw)).astype(BF))
    kh = per_head((k * e_neg).astype(BF))
    bh = per_head((b_all * e_neg).astype(BF))
    rt = per_head((r * jnp.exp(cl)).astype(BF))
    kbar = per_head((k * e_end).astype(BF))
    bbar = per_head((b_all * e_end).astype(BF))
    vb = per_head(v.astype(BF))
    g_end = per_head(g_end_all)
    lkk = [jnp.where(strict, _dot_nt(kkt[h], kh[h]), 0.0).astype(BF) for h in heads]
    nil = [jnp.where(strict, -_dot_nt(kkt[h], bh[h]), 0.0) for h in heads]
    grk = [jnp.where(incl, _dot_nt(rt[h], kh[h]), 0.0).astype(BF) for h in heads]
    grb = [jnp.where(incl, _dot_nt(rt[h], bh[h]), 0.0).astype(BF) for h in heads]
    tinv = [eye + n for n in nil]
    m = 2
    while m < C:
        nil = [_dot(n, n) for n in nil]
        tinv = [t + _dot(t, n) for t, n in zip(tinv, nil)]
        m *= 2
    s0 = [state_ref[h] for h in heads]
    x = [_dot_nt(kkt[h], s0[h]) + _dot(lkk[h], vb[h]) for h in heads]
    u = [_dot(tinv[h], x[h]).astype(BF) for h in heads]
    y = [_dot_nt(rt[h], s0[h]) + _dot(grk[h], vb[h]) - _dot(grb[h], u[h]) for h in heads]
    for h in heads:
        state_ref[h] = s0[h] * g_end[h] + _dot_tn(vb[h], kbar[h]) - _dot_tn(u[h], bbar[h])
    for h, sl in enumerate(sls):
        y_ref[:, sl] = y[h]
    y_all = y_ref[...]
    yc = y_all - _head_sums(y_all, ones2) * (1.0 / HD_R)
    var = _head_sums(yc * yc, ones2) * (1.0 / HD_R)
    yn = yc * lax.rsqrt(var + LN_X_EPS) * lng_ref[...] + lnb_ref[...]
    bonus = _head_sums(r * k * rk_ref[...], ones2) * v
    o_ref[...] = ((yn + bonus) * g).astype(o_ref.dtype)


def _rwkv_weights(mu, w0, w2, a0, a2, g2, k_k, k_a, r_k, ln_g, ln_b):
    row = lambda z: z.reshape(1, -1).astype(F32)
    return (row(mu), row(w0), w2.astype(BF), row(a0), a2.astype(BF), g2.astype(BF), row(k_k), row(k_a), row(r_k),
            row(ln_g), row(ln_b))


_RWKV_W_SHAPES = ((1, C_RIN), (1, C_R), (LORA_W, C_R), (1, C_R), (LORA_A, C_R), (LORA_G, C_R), (1, C_R), (1, C_R),
                  (1, C_R), (1, C_R), (1, C_R))


def _rwkv_prompt(pr, rw):
    b, t, _ = pr.shape
    C = RW_CHUNK
    full = lambda shp: pl.BlockSpec(shp, lambda i, j: (0,) * len(shp))
    lane_head = np.arange(LANES) // HD_R
    ones2 = jnp.asarray(lane_head[:, None] == lane_head[None, :], BF)
    return pl.pallas_call(
        _rwkv_chunk_kernel,
        grid=(b, t // C),
        in_specs=[pl.BlockSpec((None, C, C_RIN), lambda i, j: (i, j, 0))] + [full(s) for s in _RWKV_W_SHAPES]
        + [full((LANES, LANES))],
        out_specs=[pl.BlockSpec((None, C, C_R), lambda i, j: (i, j, 0)),
                   pl.BlockSpec((None, 1, C_RIN), lambda i, j: (i, 0, 0)),
                   pl.BlockSpec((None, H_R, HD_R, HD_R), lambda i, j: (i, 0, 0, 0))],
        out_shape=[jax.ShapeDtypeStruct((b, t, C_R), BF),
                   jax.ShapeDtypeStruct((b, 1, C_RIN), F32),
                   jax.ShapeDtypeStruct((b, H_R, HD_R, HD_R), F32)],
        scratch_shapes=[pltpu.VMEM((C, C_R), F32)],
        compiler_params=_cparams("arbitrary", "arbitrary"),
        name="rwkv_chunk",
    )(pr, *rw, ones2)


def _rwkv_step_kernel(pr_ref, prev_ref, s0_ref, mu_ref, w0_ref, w2_ref, a0_ref, a2_ref, g2_ref, kk_ref, ka_ref,
                      rk_ref, lng_ref, lnb_ref, o_ref, state_ref):
    nb = pr_ref.shape[0]
    r, k, v, lw, a, g, kk_all = _rwkv_features(pr_ref[...], prev_ref[...], mu_ref[...], w0_ref[...], w2_ref[...],
                                               a0_ref[...], a2_ref[...], g2_ref[...], kk_ref[...], ka_ref[...])
    decay = jnp.exp(lw)
    ii = lax.broadcasted_iota(I32, (HD_R, HD_R), 0)
    jj = lax.broadcasted_iota(I32, (HD_R, HD_R), 1)
    eye = ii == jj
    col = lambda z: jnp.sum(jnp.where(eye, z, 0.0), axis=1, keepdims=True)
    for bi in range(nb):
        for h in range(H_R):
            sl = slice(h * HD_R, (h + 1) * HD_R)
            rows = lambda z: z[bi:bi + 1, sl]
            r_h, k_h, v_h, a_h, w_h = rows(r), rows(k), rows(v), rows(a), rows(decay)
            kk_h = rows(kk_all)
            kk_h = kk_h / jnp.maximum(jnp.sqrt(jnp.sum(kk_h * kk_h, axis=-1, keepdims=True)), 1e-12)
            b_h = kk_h * a_h
            s0 = s0_ref[bi, h]
            sa = jnp.sum(s0 * (-kk_h), axis=1, keepdims=True)
            s1 = s0 * w_h + sa * b_h + col(v_h) * k_h
            state_ref[bi, h] = s1
            y_col = jnp.sum(s1 * r_h, axis=1, keepdims=True)
            y = jnp.sum(jnp.where(eye, y_col, 0.0), axis=0, keepdims=True)
            o_ref[bi:bi + 1, sl] = _rwkv_head_out(y, r_h, k_h, v_h, rows(g), rk_ref[:, sl], lng_ref[:, sl],
                                                  lnb_ref[:, sl]).astype(o_ref.dtype)


def _rwkv_step(pr, prev, s0, rw):
    b = pr.shape[0]
    return pl.pallas_call(
        _rwkv_step_kernel,
        out_shape=[jax.ShapeDtypeStruct((b, C_R), BF), jax.ShapeDtypeStruct((b, H_R, HD_R, HD_R), F32)],
        compiler_params=pltpu.CompilerParams(vmem_limit_bytes=VMEM_LIMIT_BYTES),
        name="rwkv_step",
    )(pr, prev, s0, *rw)


def _cmp_partial_kernel(*refs):
    x_refs, w_ref, o_ref = refs[:-2], refs[-2], refs[-1]
    x = x_refs[0][...] if len(x_refs) == 1 else jnp.concatenate([r[...] for r in x_refs], axis=0)
    half = N_KVG * HD_N
    for s in range(2):
        acc = jnp.zeros((x.shape[0], N_KVG * 2 * CMP_HID), F32)
        for p in range(STRIDE):
            o = p * KV_COLS + s * half
            acc = acc + _dot(x[:, o:o + half], w_ref[p, s])
        o_ref[:, s * N_KVG * 2 * CMP_HID:(s + 1) * N_KVG * 2 * CMP_HID] = acc


CMP_PAIR = 2


def _cmp_partial_paged_kernel(pt_ref, *refs):
    del pt_ref
    x_refs, perm_ref, w_ref, o_ref = refs[:-3], refs[-3], refs[-2], refs[-1]
    cpp = PAGE_SIZE // STRIDE
    width = CMP_PAIR * 2 * CMP_HID
    perm = perm_ref[...]
    for s in range(2):
        for gp in range(N_KVG // CMP_PAIR):
            rows = []
            for x_ref in x_refs:
                tile = jnp.concatenate([x_ref[s, CMP_PAIR * gp + j] for j in range(CMP_PAIR)], axis=0)
                rows.append(_dot_nt(perm, tile))
            acc = jnp.zeros((len(x_refs) * cpp, width), F32)
            for p in range(STRIDE):
                lhs = jnp.concatenate([r[p * cpp:(p + 1) * cpp] for r in rows], axis=0)
                acc = acc + _dot(lhs, w_ref[p, s])
            o = (s * (N_KVG // CMP_PAIR) + gp) * width
            o_ref[:, o:o + width] = acc


def _cmp_weights(cmp_w1, groups):
    w1r = cmp_w1.reshape(2, 2, STRIDE, HD_N, CMP_HID)
    w = jnp.transpose(w1r, (2, 0, 3, 1, 4))
    w = w.reshape(STRIDE, 2, 1, HD_N, 1, 2 * CMP_HID)
    eye = jnp.eye(groups, dtype=w.dtype).reshape(1, 1, groups, 1, groups, 1)
    wbd = eye * w
    return wbd.reshape(STRIDE, 2, groups * HD_N, groups * 2 * CMP_HID).astype(BF)


def _cmp_partial_rows(x, wbd, tr):
    r = x.shape[0]
    n = 2 * N_KVG * 2 * CMP_HID
    return pl.pallas_call(
        _cmp_partial_kernel,
        grid=(r // tr,),
        in_specs=[pl.BlockSpec((tr, STRIDE * KV_COLS), lambda i: (i, 0)),
                  pl.BlockSpec(wbd.shape, lambda i: (0, 0, 0, 0))],
        out_specs=pl.BlockSpec((tr, n), lambda i: (i, 0)),
        out_shape=jax.ShapeDtypeStruct((r, n), F32),
        compiler_params=_cparams("arbitrary"),
        name="cmp_partial",
    )(x, wbd)


PAGES_PER_STEP = 8


def _cmp_partial_paged(cache_t, page_table, wpair):
    b, npg = page_table.shape
    cpp = PAGE_SIZE // STRIDE
    n = 2 * N_KVG * 2 * CMP_HID
    steps = npg // PAGES_PER_STEP
    perm = np.zeros((PAGE_SIZE, PAGE_SIZE), np.float32)
    tok = np.arange(PAGE_SIZE)
    perm[(tok % STRIDE) * cpp + tok // STRIDE, tok] = 1.0

    def page_spec(kpg):
        return pl.BlockSpec((None, 2, N_KVG, HD_N, PAGE_SIZE),
                            lambda i, j, pt: (pt[i, j * PAGES_PER_STEP + kpg], 0, 0, 0, 0))

    grid_spec = pltpu.PrefetchScalarGridSpec(
        num_scalar_prefetch=1,
        grid=(b, steps),
        in_specs=[page_spec(kpg) for kpg in range(PAGES_PER_STEP)]
        + [pl.BlockSpec((PAGE_SIZE, PAGE_SIZE), lambda i, j, pt: (0, 0)),
           pl.BlockSpec(wpair.shape, lambda i, j, pt: (0, 0, 0, 0))],
        out_specs=pl.BlockSpec((None, PAGES_PER_STEP * cpp, n), lambda i, j, pt: (i, j, 0)),
    )
    return pl.pallas_call(
        _cmp_partial_paged_kernel,
        grid_spec=grid_spec,
        out_shape=jax.ShapeDtypeStruct((b, npg * cpp, n), F32),
        compiler_params=_cparams("arbitrary", "arbitrary"),
        name="cmp_partial_paged",
    )(page_table, *([cache_t] * PAGES_PER_STEP), jnp.asarray(perm, BF), wpair)


def _cmp_finish_kernel(c_ref, cpos_ref, w2_ref, o_ref):
    c = c_ref[...]
    nrow = c.shape[0]
    c_next = pltpu.roll(c, nrow - 1, axis=0)
    for s in range(2):
        for gi in range(N_KVG):
            o = (s * N_KVG + gi) * 2 * CMP_HID
            hid = (c[:, o:o + CMP_HID] + cpos_ref[0:1, o:o + CMP_HID]
                   + c_next[:, o + CMP_HID:o + 2 * CMP_HID] + cpos_ref[1:2, o + CMP_HID:o + 2 * CMP_HID])
            oo = (s * N_KVG + gi) * HD_N
            o_ref[:, oo:oo + HD_N] = _dot(_gelu_tanh(hid), w2_ref[s])


def _cmp_finish(c, cpos, w2):
    b, nch, n = c.shape
    return pl.pallas_call(
        _cmp_finish_kernel,
        grid=(b,),
        in_specs=[pl.BlockSpec((None, nch, n), lambda i: (i, 0, 0)), pl.BlockSpec(cpos.shape, lambda i: (0, 0)),
                  pl.BlockSpec(w2.shape, lambda i: (0, 0, 0))],
        out_specs=pl.BlockSpec((None, nch, KV_COLS), lambda i: (i, 0, 0)),
        out_shape=jax.ShapeDtypeStruct((b, nch, KV_COLS), F32),
        compiler_params=_cparams("arbitrary"),
        name="cmp_finish",
    )(c, cpos, w2.astype(BF))


def _cmp_pos_rows(cmp_pos):
    pos = cmp_pos.reshape(2, STRIDE, 1, 1, HD_N)
    rows = jnp.broadcast_to(pos, (2, STRIDE, 2, N_KVG, HD_N)).reshape(2, STRIDE * KV_COLS)
    return jnp.concatenate([rows, jnp.zeros((6, STRIDE * KV_COLS), F32)], axis=0)


def _bias_table_kernel(dist_ref, rb_ref, o_ref):
    bucket = _t5_bucket(dist_ref[...])
    for h in range(H_N):
        out = jnp.zeros(bucket.shape, F32)
        for b in range(N_BUCKETS):
            out = jnp.where(bucket == b, rb_ref[b, h], out)
        o_ref[h] = out


def _bias_table(dist, rel_bias, tr):
    r, n = dist.shape
    return pl.pallas_call(
        _bias_table_kernel,
        grid=(r // tr,),
        in_specs=[pl.BlockSpec((tr, n), lambda i: (i, 0)),
                  pl.BlockSpec(memory_space=pltpu.SMEM)],
        out_specs=pl.BlockSpec((H_N, tr, n), lambda i: (0, i, 0)),
        out_shape=jax.ShapeDtypeStruct((H_N, r, n), F32),
        compiler_params=_cparams("arbitrary"),
        name="bias_table",
    )(dist, rel_bias)


def _softmax_rows(logits, valid):
    lm = jnp.where(valid, logits, NEG_INF)
    e = jnp.exp(lm - jnp.max(lm, axis=-1, keepdims=True))
    return e / jnp.sum(e, axis=-1, keepdims=True)


def _cmp_attn_kernel(q_ref, kv_ref, bias_ref, ovt_ref, o_ref, sel_ref):
    tq = q_ref.shape[0]
    ncp = kv_ref.shape[0]
    nsb = ovt_ref.shape[0]
    q0 = pl.program_id(1) * tq
    qpos = q0 + lax.broadcasted_iota(I32, (tq, 1), 0)
    cend = lax.broadcasted_iota(I32, (1, ncp), 1) * STRIDE + (L_CMP - 1)
    valid = (qpos >= cend) & (lax.broadcasted_iota(I32, (1, ncp), 1) < ncp - 1)
    validf = valid.astype(F32)
    q = q_ref[...] * (HD_N ** -0.5)
    blk = lax.broadcasted_iota(I32, (nsb, tq), 0)
    cur = (q0 + lax.broadcasted_iota(I32, (1, tq), 1)) // L_SEL
    forced = (blk == 0) | (blk == cur) | (blk == cur - 1)
    future = blk > cur
    for gi in range(N_KVG):
        kc = kv_ref[:, gi * HD_N:(gi + 1) * HD_N]
        vc = kv_ref[:, (N_KVG + gi) * HD_N:(N_KVG + gi + 1) * HD_N]
        pcs = jnp.zeros((tq, ncp), F32)
        for hl in range(HPG):
            h = gi * HPG + hl
            sl = slice(h * HD_N, (h + 1) * HD_N)
            pc = _softmax_rows(_dot_nt(q[:, sl], kc) + bias_ref[h], valid) * validf
            pcs = pcs + pc
            o_ref[:, sl] = _dot(pc, vc)
        imp = _dot_nt(ovt_ref[...], pcs)
        score = jnp.where(forced, FORCE, jnp.where(future, -FORCE, imp))
        rank = jnp.zeros((nsb, tq), F32)
        for i in range(nsb):
            si = score[i:i + 1, :]
            rank = rank + ((si > score) | ((si == score) & (i < blk))).astype(F32)
        sel_ref[gi] = (rank < N_TOP).astype(sel_ref.dtype)


def _cmp_sel_overlap_t(nc, ncp, nsb):
    s = np.arange(nc)[None, :] * STRIDE
    j = np.arange(nsb)[:, None] * L_SEL
    ov = np.clip(np.minimum(s + L_CMP, j + L_SEL) - np.maximum(s, j), 0, None) / L_CMP
    return np.pad(ov, ((0, 0), (0, ncp - nc))).astype(np.float32)


def _cmp_attn_prompt(p_n, kv_cmp, bias_c, t):
    b = kv_cmp.shape[0]
    ncp = kv_cmp.shape[1]
    nsb = t // L_SEL
    tq = ATT_TILE
    nqt = t // tq
    ovt = jnp.asarray(_cmp_sel_overlap_t(ncp - 1, ncp, nsb), BF)
    return pl.pallas_call(
        _cmp_attn_kernel,
        grid=(b, nqt),
        in_specs=[pl.BlockSpec((tq, C_N), lambda i, j: (i * nqt + j, 0)),
                  pl.BlockSpec((None, ncp, KV_COLS), lambda i, j: (i, 0, 0)),
                  pl.BlockSpec((H_N, tq, ncp), lambda i, j: (0, j, 0)),
                  pl.BlockSpec((nsb, ncp), lambda i, j: (0, 0))],
        out_specs=[pl.BlockSpec((tq, C_N), lambda i, j: (i * nqt + j, 0)),
                   pl.BlockSpec((None, N_KVG, nsb, tq), lambda i, j: (i, 0, 0, j))],
        out_shape=[jax.ShapeDtypeStruct((b * t, C_N), F32), jax.ShapeDtypeStruct((b, N_KVG, nsb, t), BF)],
        compiler_params=_cparams("arbitrary", "arbitrary"),
        name="nsa_cmp_select",
    )(p_n, kv_cmp, bias_c, ovt)


ATT_TK = 128
ATT_R = ATT_TILE // ATT_TK
N_SEL_OFFS = ATT_R + (MAX_DIST + ATT_TK - 1) // ATT_TK + 1
N_WIN_OFFS = ATT_R + WINDOW // ATT_TK
QA_COLS = HD_N + 32


def _swa_kernel(qa_ref, ks_ref, vs_ref, kw_ref, vw_ref, tabs_ref, tabw_ref, gn_ref, oc_ref, o_ref):
    tq = qa_ref.shape[1]
    qt = pl.program_id(2)
    top = ATT_R * qt + ATT_R - 1

    heads = range(HPG)
    tiles = range(ATT_R)

    def attend(k_ref, v_ref, tab_ref, lo, n_off):
        def body(kp, carry):
            ki = [ATT_R * kp + j for j in tiles]
            kt = [k_ref[i] for i in ki]
            vt = [v_ref[i] for i in ki]
            off = [jnp.minimum(top - i, n_off - 1) for i in ki]
            s = [[lax.dot_general(kt[j], qa_ref[hl], (((1,), (1,)), ((), ())), preferred_element_type=F32)
                  + tab_ref[off[j], hl * ATT_TK:(hl + 1) * ATT_TK, :] for j in tiles] for hl in heads]
            m_new = [functools.reduce(jnp.maximum, [carry[hl][0]] + [jnp.max(s[hl][j], axis=0, keepdims=True)
                                                                    for j in tiles]) for hl in heads]
            alpha = [jnp.exp(carry[hl][0] - m_new[hl]) for hl in heads]
            p = [[jnp.exp(s[hl][j] - m_new[hl]) for j in tiles] for hl in heads]
            l = [alpha[hl] * carry[hl][1] + sum(jnp.sum(p[hl][j], axis=0, keepdims=True) for j in tiles)
                 for hl in heads]
            acc = [alpha[hl] * carry[hl][2] + sum(jnp.dot(vt[j], p[hl][j].astype(BF), preferred_element_type=F32)
                                                  for j in tiles) for hl in heads]
            return tuple((m_new[hl], l[hl], acc[hl]) for hl in heads)

        init = tuple((jnp.full((1, tq), NEG_INF, F32), jnp.zeros((1, tq), F32), jnp.zeros((HD_N, tq), F32))
                     for _ in heads)
        res = lax.fori_loop(lo // ATT_R, qt + 1, body, init)
        return [acc / l for _, l, acc in res]

    o_sel = attend(ks_ref, vs_ref, tabs_ref, 0, N_SEL_OFFS)
    o_win = attend(kw_ref, vw_ref, tabw_ref, jnp.maximum(top + 1 - N_WIN_OFFS, 0), N_WIN_OFFS)
    gates = _sigmoid(gn_ref[...])
    gates_t = gates.T
    for hl in heads:
        sl = slice(hl * HD_N, (hl + 1) * HD_N)
        o_t = gates_t[3 * hl + 1:3 * hl + 2, :] * o_sel[hl] + gates_t[3 * hl + 2:3 * hl + 3, :] * o_win[hl]
        o_ref[:, sl] = (gates[:, 3 * hl:3 * hl + 1] * oc_ref[:, sl] + o_t.T).astype(o_ref.dtype)


def _swa_prompt(p_n, qa, ks, vs, kw, vw, tab_s, tab_w, o_cmp, t):
    b = qa.shape[0]
    tq = ATT_TILE
    nqt = t // tq
    nkt = t // ATT_TK
    gw = HPG * HD_N
    k_spec = pl.BlockSpec((None, None, nkt, ATT_TK, QA_COLS), lambda i, g, j: (i, g, 0, 0, 0))
    v_spec = pl.BlockSpec((None, None, nkt, HD_N, ATT_TK), lambda i, g, j: (i, g, 0, 0, 0))
    tab_spec = lambda n: pl.BlockSpec((n, HPG * ATT_TK, tq), lambda i, g, j: (0, g, 0))
    return pl.pallas_call(
        _swa_kernel,
        grid=(b, N_KVG, nqt),
        in_specs=[pl.BlockSpec((None, HPG, tq, QA_COLS), lambda i, g, j: (i, g, j, 0)),
                  k_spec, v_spec, k_spec, v_spec, tab_spec(N_SEL_OFFS), tab_spec(N_WIN_OFFS),
                  pl.BlockSpec((tq, GN_GROUP_COLS), lambda i, g, j: (i * nqt + j, NP_GN // GN_GROUP_COLS + g)),
                  pl.BlockSpec((tq, gw), lambda i, g, j: (i * nqt + j, g))],
        out_specs=pl.BlockSpec((tq, gw), lambda i, g, j: (i * nqt + j, g)),
        out_shape=jax.ShapeDtypeStruct((b * t, C_N), BF),
        compiler_params=_cparams("arbitrary", "arbitrary", "arbitrary"),
        name="nsa_sel_win",
    )(qa, ks, vs, kw, vw, tab_s, tab_w, p_n, o_cmp)


def _swa_operands(p_n, sel, b, t):
    nsb = t // L_SEL
    nkt = t // ATT_TK
    q = (p_n[:, :C_N] * (HD_N ** -0.5)).reshape(b, t, H_N, HD_N).transpose(0, 2, 1, 3)
    pen = jnp.where(jnp.transpose(sel, (0, 1, 3, 2)) > 0.5, 0.0, NEG_INF).astype(F32)
    pen = jnp.broadcast_to(pen[:, :, None], (b, N_KVG, HPG, t, nsb)).reshape(b, H_N, t, nsb)
    qa = jnp.concatenate([q, pen], axis=-1).astype(BF)

    def split(cols):
        kv = p_n[:, cols:cols + KV_COLS].reshape(b, t, 2, N_KVG, HD_N)
        k = jnp.transpose(kv[:, :, 0], (0, 2, 1, 3))
        v = jnp.transpose(kv[:, :, 1], (0, 2, 3, 1)).reshape(b, N_KVG, HD_N, nkt, ATT_TK)
        return k, jnp.transpose(v, (0, 1, 3, 2, 4)).astype(BF)

    onehot = jnp.asarray(np.arange(t)[:, None] // L_SEL == np.arange(nsb)[None, :], F32)
    k_s, v_s = split(NP_KVS)
    k_w, v_w = split(NP_KVW)
    ext = lambda k, e: jnp.concatenate([k, jnp.broadcast_to(e, (b, N_KVG, t, nsb))], axis=-1).astype(BF) \
        .reshape(b, N_KVG, nkt, ATT_TK, QA_COLS)
    return qa, ext(k_s, onehot), v_s, ext(k_w, jnp.zeros((t, nsb), F32)), v_w


def _sample_cmp_kernel(q_ref, kv_ref, rbt_ref, ov_ref, o_ref, idx_ref, *, past):
    ncp = kv_ref.shape[0]
    nsbp = ov_ref.shape[1]
    nsb = -(-(past + 1) // L_SEL)
    q = q_ref[...] * (HD_N ** -0.5)
    hrow = lax.broadcasted_iota(I32, (H_N, 1), 0)
    nidx = lax.broadcasted_iota(I32, (1, ncp), 1)
    valid = nidx < ncp - 1
    bias = _bias_rows(past - (nidx * STRIDE + (L_CMP - 1)), rbt_ref[...])
    logits = jnp.zeros((H_N, ncp), F32)
    for gi in range(N_KVG):
        lg = _dot_nt(q, kv_ref[:, gi * HD_N:(gi + 1) * HD_N])
        logits = jnp.where(hrow // HPG == gi, lg, logits)
    pc = _softmax_rows(logits + bias, valid) * valid.astype(F32)
    o = jnp.zeros((H_N, HD_N), F32)
    for gi in range(N_KVG):
        og = _dot(pc, kv_ref[:, (N_KVG + gi) * HD_N:(N_KVG + gi + 1) * HD_N])
        o = jnp.where(hrow // HPG == gi, og, o)
    o_ref[...] = o
    imp_h = _dot(pc, ov_ref[...])
    blk = lax.broadcasted_iota(I32, (8, nsbp), 1)
    grow = lax.broadcasted_iota(I32, (8, 1), 0)
    cur = past // L_SEL
    score = jnp.full((8, nsbp), -3e38, F32)
    for gi in range(N_KVG):
        imp = jnp.sum(jnp.where(hrow // HPG == gi, imp_h, 0.0), axis=0, keepdims=True)
        score = jnp.where(grow == gi, imp, score)
    forced = (blk == 0) | (blk == cur) | (blk == cur - 1)
    score = jnp.where(forced, FORCE, jnp.where(blk > cur, -FORCE, score))
    score = jnp.where((blk < nsb) & (grow < N_KVG), score, -3e38)
    lane = lax.broadcasted_iota(I32, (8, LANES), 1)
    picks = jnp.zeros((8, LANES), I32)
    for it in range(N_TOP):
        mx = jnp.max(score, axis=-1, keepdims=True)
        pick = jnp.min(jnp.where(score == mx, blk, nsbp), axis=-1, keepdims=True)
        picks = jnp.where(lane == it, pick, picks)
        score = jnp.where(blk == pick, -3e38, score)
    idx_ref[...] = picks


def _sample_cmp(q, kv_cmp, rel_bias, past):
    b, ncp, _ = kv_cmp.shape
    nc = ncp - 1
    nsb = -(-(past + 1) // L_SEL)
    nsbp = -(-nsb // LANES) * LANES
    s = np.arange(nc)[:, None] * STRIDE
    j = np.arange(nsb)[None, :] * L_SEL
    ov = np.clip(np.minimum(s + L_CMP, j + L_SEL) - np.maximum(s, j), 0, None) / L_CMP
    ov = np.pad(ov, ((0, ncp - nc), (0, nsbp - nsb))).astype(np.float32)
    return pl.pallas_call(
        functools.partial(_sample_cmp_kernel, past=past),
        grid=(b,),
        in_specs=[pl.BlockSpec((None, H_N, HD_N), lambda i: (i, 0, 0)),
                  pl.BlockSpec((None, ncp, KV_COLS), lambda i: (i, 0, 0)),
                  pl.BlockSpec((H_N, N_BUCKETS), lambda i: (0, 0)),
                  pl.BlockSpec((ncp, nsbp), lambda i: (0, 0))],
        out_specs=[pl.BlockSpec((None, H_N, HD_N), lambda i: (i, 0, 0)),
                   pl.BlockSpec((None, 8, LANES), lambda i: (i, 0, 0))],
        out_shape=[jax.ShapeDtypeStruct((b, H_N, HD_N), F32), jax.ShapeDtypeStruct((b, 8, LANES), I32)],
        compiler_params=_cparams("arbitrary"),
        name="nsa_sample_cmp_select",
    )(q, kv_cmp, rel_bias.T, jnp.asarray(ov, BF))


def _block_copy_kernel(pg_ref, *refs):
    del pg_ref
    x_refs, o_ref = refs[:-1], refs[-1]
    for n, x_ref in enumerate(x_refs):
        o_ref[n] = x_ref[...]


def _gather_sel_pages(cache_t, page):
    rows, nslot = page.shape

    def slot_spec(n):
        return pl.BlockSpec((None, 2, None, HD_N, PAGE_SIZE), lambda i, pg: (pg[i, n], 0, i % N_KVG, 0, 0))

    return pl.pallas_call(
        _block_copy_kernel,
        grid_spec=pltpu.PrefetchScalarGridSpec(
            num_scalar_prefetch=1, grid=(rows,),
            in_specs=[slot_spec(n) for n in range(nslot)],
            out_specs=pl.BlockSpec((None, nslot, 2, HD_N, PAGE_SIZE), lambda i, pg: (i, 0, 0, 0, 0))),
        out_shape=jax.ShapeDtypeStruct((rows, nslot, 2, HD_N, PAGE_SIZE), cache_t.dtype),
        compiler_params=_cparams("arbitrary"),
        name="gather_sel_pages",
    )(page, *([cache_t] * nslot))


def _sample_swa_kernel(idx_ref, q_ref, blk_ref, win_ref, new_ref, rbt_ref, gate_ref, oc_ref, o_ref, *, past):
    bi = pl.program_id(0)
    q = q_ref[...] * (HD_N ** -0.5)
    rbt = rbt_ref[...]
    hrow = lax.broadcasted_iota(I32, (H_N, 1), 0)
    nk = N_TOP * PAGE_SIZE
    lane = lax.broadcasted_iota(I32, (1, nk), 1)
    new_blk = past // L_SEL
    bias_new = rbt[:, 0:1]
    gates = _sigmoid(gate_ref[...])
    nwin = win_ref.shape[0]
    wdist = nwin - lax.broadcasted_iota(I32, (1, nwin), 1)
    wbias = _bias_rows(wdist, rbt)
    wvalid = (wdist >= 0) & (wdist <= WINDOW)

    def with_new(logits, valid, weigh, k_new, v_new):
        l_new = jnp.sum(q * k_new, axis=-1, keepdims=True) + bias_new
        lm = jnp.where(valid, logits, NEG_INF)
        m = jnp.maximum(jnp.max(lm, axis=-1, keepdims=True), l_new)
        e = jnp.where(valid, jnp.exp(lm - m), 0.0)
        e_new = jnp.exp(l_new - m)
        den = jnp.sum(e, axis=-1, keepdims=True) + e_new
        return (weigh(e) + e_new * v_new) / den

    o_sel = jnp.zeros((H_N, HD_N), F32)
    o_win = jnp.zeros((H_N, HD_N), F32)
    bpp = PAGE_SIZE // L_SEL
    tok = lane % PAGE_SIZE
    for gi in range(N_KVG):
        ksl = slice(gi * HD_N, (gi + 1) * HD_N)
        vsl = slice((N_KVG + gi) * HD_N, (N_KVG + gi + 1) * HD_N)
        bid = jnp.zeros((1, nk), I32)
        for n in range(N_TOP):
            bid = jnp.where(lane // PAGE_SIZE == n, idx_ref[bi, gi, n], bid)
        dist = past - ((bid // bpp) * PAGE_SIZE + tok)
        valid = (bid != new_blk) & (tok // L_SEL == bid % bpp) & (dist >= 0)
        kt = jnp.concatenate([blk_ref[gi, n, 0] for n in range(N_TOP)], axis=1)
        vt = jnp.concatenate([blk_ref[gi, n, 1] for n in range(N_TOP)], axis=1)
        logits = _dot(q, kt) + _bias_rows(dist, rbt)
        og = with_new(logits, valid, lambda e, vt=vt: _dot_nt(e, vt), new_ref[0:1, ksl], new_ref[0:1, vsl])
        o_sel = jnp.where(hrow // HPG == gi, og, o_sel)
        logits = _dot_nt(q, win_ref[:, ksl]) + wbias
        og = with_new(logits, wvalid, lambda e, vsl=vsl: _dot(e, win_ref[:, vsl]), new_ref[1:2, ksl],
                      new_ref[1:2, vsl])
        o_win = jnp.where(hrow // HPG == gi, og, o_win)
    o_ref[...] = gates[:, 0:1] * oc_ref[...] + gates[:, 1:2] * o_sel + gates[:, 2:3] * o_win


def _sample_swa(idx, q, blocks, win, new_kv, rel_bias, gates, o_cmp, past):
    b = q.shape[0]
    w = win.shape[1]
    grid_spec = pltpu.PrefetchScalarGridSpec(
        num_scalar_prefetch=1, grid=(b,),
        in_specs=[pl.BlockSpec((None, H_N, HD_N), lambda i, ix: (i, 0, 0)),
                  pl.BlockSpec((None, N_KVG, N_TOP, 2, HD_N, PAGE_SIZE), lambda i, ix: (i, 0, 0, 0, 0, 0)),
                  pl.BlockSpec((None, w, KV_COLS), lambda i, ix: (i, 0, 0)),
                  pl.BlockSpec((None, 2, KV_COLS), lambda i, ix: (i, 0, 0)),
                  pl.BlockSpec((H_N, N_BUCKETS), lambda i, ix: (0, 0)),
                  pl.BlockSpec((None, H_N, 3), lambda i, ix: (i, 0, 0)),
                  pl.BlockSpec((None, H_N, HD_N), lambda i, ix: (i, 0, 0))],
        out_specs=pl.BlockSpec((None, H_N, HD_N), lambda i, ix: (i, 0, 0)))
    return pl.pallas_call(
        functools.partial(_sample_swa_kernel, past=past),
        grid_spec=grid_spec,
        out_shape=jax.ShapeDtypeStruct((b, H_N, HD_N), F32),
        compiler_params=_cparams("arbitrary"),
        name="nsa_sample_sel_win",
    )(idx, q, blocks, win, new_kv, rel_bias.T, gates, o_cmp)


ROUTER_COLS = LANES


def _router_kernel(x_ref, g_ref, sh_ref, sc_ref, w_ref, b_ref, h_ref, e_ref, wt_ref, rk_ref, cnt_ref):
    i = pl.program_id(0)
    tm = x_ref.shape[0]

    @pl.when(i == 0)
    def _():
        cnt_ref[...] = jnp.zeros_like(cnt_ref)

    h = (_rms(x_ref[...], g_ref[...]) * (1.0 + sc_ref[...]) + sh_ref[...]).astype(BF)
    h_ref[:, 0, :] = h.astype(h_ref.dtype)
    logits = jnp.dot(h, w_ref[...], preferred_element_type=F32) + b_ref[...]
    lane = lax.broadcasted_iota(I32, (tm, ROUTER_COLS), 1)

    def top1(vals, ok):
        vm = jnp.where(ok, vals, -3e38)
        mx = jnp.max(vm, axis=-1, keepdims=True)
        return mx, jnp.min(jnp.where(ok & (vm == mx), lane, ROUTER_COLS), axis=-1, keepdims=True)

    isg = lane < N_EGROUPS
    pg = _softmax_rows(logits, isg)
    g_w, g_i = top1(pg, isg)
    ise = (lane >= N_EGROUPS) & ((lane - N_EGROUPS) // EXP_PER_GROUP == g_i)
    pe = _softmax_rows(logits, ise)
    w0, l0 = top1(pe, ise)
    w1, l1 = top1(pe, ise & (lane != l0))
    den = w0 + w1
    e0 = l0 - N_EGROUPS
    e1 = l1 - N_EGROUPS
    e_ref[...] = jnp.where(lane == 0, e0, jnp.where(lane == 1, e1, 0))
    wt_ref[...] = jnp.where(lane == 0, w0 / den * g_w, jnp.where(lane == 1, w1 / den * g_w, 0.0))
    oh0 = (lane == e0).astype(F32)
    oh1 = (lane == e1).astype(F32)
    cnt = oh0 + oh1
    ti = lax.broadcasted_iota(I32, (tm, tm), 0)
    si = lax.broadcasted_iota(I32, (tm, tm), 1)
    before = _dot((ti > si).astype(F32), cnt) + cnt_ref[...]
    r0 = jnp.sum(before * oh0, axis=-1, keepdims=True)
    r1 = jnp.sum(before * oh1, axis=-1, keepdims=True)
    rk_ref[...] = jnp.where(lane == 0, r0, jnp.where(lane == 1, r1, 0.0)).astype(I32)
    cnt_ref[...] = cnt_ref[...] + jnp.sum(cnt, axis=0, keepdims=True)


def _router(x, g, shift, scale, w_r, b_r, tm, rpb):
    m = x.shape[0]
    r = shift.shape[1]
    rows = lambda tn: pl.BlockSpec((tm, tn), lambda i: (i, 0))
    mods = pl.BlockSpec((None, r, D_MODEL), lambda i: ((i * tm) // rpb, 0, 0))
    small = lambda dt: jax.ShapeDtypeStruct((m, ROUTER_COLS), dt)
    return pl.pallas_call(
        _router_kernel,
        grid=(m // tm,),
        in_specs=[rows(D_MODEL), pl.BlockSpec((1, D_MODEL), lambda i: (0, 0)), mods, mods,
                  pl.BlockSpec((D_MODEL, ROUTER_COLS), lambda i: (0, 0)),
                  pl.BlockSpec((1, ROUTER_COLS), lambda i: (0, 0))],
        out_specs=[pl.BlockSpec((tm, 1, D_MODEL), lambda i: (i, 0, 0)),
                   rows(ROUTER_COLS), rows(ROUTER_COLS), rows(ROUTER_COLS),
                   pl.BlockSpec((1, ROUTER_COLS), lambda i: (0, 0))],
        out_shape=[jax.ShapeDtypeStruct((m, 1, D_MODEL), BF), small(I32), small(F32), small(I32),
                   jax.ShapeDtypeStruct((1, ROUTER_COLS), F32)],
        compiler_params=_cparams("arbitrary"),
        name="moe_router",
    )(x, g.reshape(1, D_MODEL), shift, scale, w_r, b_r)


def _row_gather_ring(src_hbm, buf, sems, groups, idx_now, idx_next):
    i = pl.program_id(0)
    last = pl.num_programs(0) - 1
    slot = i % 2
    total = sum(cnt for _, cnt, _ in groups)

    def start_all(idx, s):
        for first, cnt, k in groups:
            def issue(j, c):
                for u in range(2):
                    r = 2 * j + u
                    pltpu.make_async_copy(src_hbm.at[idx(k, r)], buf.at[s, first + r], sems.at[s]).start(priority=u)
                return c
            lax.fori_loop(0, cnt // 2, issue, 0)

    @pl.when(i == 0)
    def _():
        start_all(idx_now, slot)

    @pl.when(i < last)
    def _():
        start_all(idx_next, 1 - slot)

    assert total == buf.shape[1]
    pltpu.make_async_copy(buf.at[slot], buf.at[slot], sems.at[slot]).wait()
    return slot


def _expert_kernel(be_ref, rt_ref, rtn_ref, h_hbm, w1_ref, w3_ref, w2_ref, o_ref, xbuf, sems, w1b, w3b, w2b):
    i = pl.program_id(0)
    blk = xbuf.shape[1]
    slot = _row_gather_ring(h_hbm, xbuf, sems, ((0, blk, 0),), lambda k, r: rt_ref[0, r], lambda k, r: rtn_ref[0, r])

    @pl.when((i == 0) | (be_ref[i] != be_ref[jnp.maximum(i - 1, 0)]))
    def _():
        w1b[...] = w1_ref[...].astype(BF)
        w3b[...] = w3_ref[...].astype(BF)
        w2b[...] = w2_ref[...].astype(BF)

    x = xbuf[slot, :, 0, :]
    a = jnp.dot(x, w1b[...], preferred_element_type=F32)
    b = jnp.dot(x, w3b[...], preferred_element_type=F32)
    hid = a * _sigmoid(a) * b
    o_ref[:, 0, :] = jnp.dot(hid.astype(BF), w2b[...], preferred_element_type=F32)


def _experts(h2, row_tok, blk_exp, w1, w3, w2):
    nblk, _, blk = row_tok.shape
    idx_spec = lambda d: pl.BlockSpec((None, 1, blk), lambda i, be: (jnp.minimum(i + d, nblk - 1), 0, 0),
                                      memory_space=pltpu.SMEM)
    grid_spec = pltpu.PrefetchScalarGridSpec(
        num_scalar_prefetch=1, grid=(nblk,),
        in_specs=[idx_spec(0), idx_spec(1),
                  pl.BlockSpec(memory_space=pl.ANY),
                  pl.BlockSpec((None, D_MODEL, D_EXP), lambda i, be: (be[i], 0, 0)),
                  pl.BlockSpec((None, D_MODEL, D_EXP), lambda i, be: (be[i], 0, 0)),
                  pl.BlockSpec((None, D_EXP, D_MODEL), lambda i, be: (be[i], 0, 0))],
        out_specs=pl.BlockSpec((blk, 1, D_MODEL), lambda i, be: (i, 0, 0)),
        scratch_shapes=[pltpu.VMEM((2, blk, 1, D_MODEL), h2.dtype), pltpu.SemaphoreType.DMA((2,)),
                        pltpu.VMEM((D_MODEL, D_EXP), BF), pltpu.VMEM((D_MODEL, D_EXP), BF),
                        pltpu.VMEM((D_EXP, D_MODEL), BF)])
    return pl.pallas_call(
        _expert_kernel,
        grid_spec=grid_spec,
        out_shape=jax.ShapeDtypeStruct((nblk * blk, 1, D_MODEL), F32),
        compiler_params=_cparams("arbitrary"),
        name="moe_experts",
    )(blk_exp, row_tok, row_tok, h2, w1, w3, w2)


def _final_kernel(x_ref, g_ref, dest_ref, destn_ref, ys_hbm, wt_ref, nf_ref, o_ref, ybuf, sems):
    tm = x_ref.shape[0]
    groups = tuple((k * tm, tm, k) for k in range(TOP_K))
    slot = _row_gather_ring(ys_hbm, ybuf, sems, groups, lambda k, r: dest_ref[k, r], lambda k, r: destn_ref[k, r])
    wt = wt_ref[...]
    moe = wt[:, 0:1] * ybuf[slot, 0:tm, 0, :] + wt[:, 1:2] * ybuf[slot, tm:2 * tm, 0, :]
    o_ref[...] = _rms(x_ref[...] + g_ref[...] * moe, nf_ref[...])


def _final(x, gate, ys, dest, wts, norm_f, tm, rpb):
    m = x.shape[0]
    r = gate.shape[1]
    nt = m // tm
    rows = lambda tn: pl.BlockSpec((tm, tn), lambda i: (i, 0))
    idx_spec = lambda d: pl.BlockSpec((None, TOP_K, tm), lambda i: (jnp.minimum(i + d, nt - 1), 0, 0),
                                      memory_space=pltpu.SMEM)
    return pl.pallas_call(
        _final_kernel,
        grid=(nt,),
        in_specs=[rows(D_MODEL), pl.BlockSpec((None, r, D_MODEL), lambda i: ((i * tm) // rpb, 0, 0)),
                  idx_spec(0), idx_spec(1),
                  pl.BlockSpec(memory_space=pl.ANY),
                  rows(ROUTER_COLS), pl.BlockSpec((1, D_MODEL), lambda i: (0, 0))],
        out_specs=rows(D_MODEL),
        out_shape=jax.ShapeDtypeStruct((m, D_MODEL), F32),
        scratch_shapes=[pltpu.VMEM((2, TOP_K * tm, 1, D_MODEL), F32), pltpu.SemaphoreType.DMA((2,))],
        compiler_params=_cparams("arbitrary"),
        name="moe_combine_final_norm",
    )(x, gate, dest, dest, ys, wts, norm_f.reshape(1, D_MODEL))


def _moe_and_final(x1, g2, shift, scale, gate, w_r, b_r, exp_w1, exp_w3, exp_w2, norm_f, tm, rpb, blk):
    m = x1.shape[0]
    h2, eid, wts, rank, counts = _router(x1, g2, shift, scale, w_r, b_r, tm, rpb)
    counts = counts[0, :N_EXP].astype(I32)
    padded = (counts + blk - 1) // blk * blk
    pend = jnp.cumsum(padded)
    pstart = pend - padded
    n_blocks = -(-(m * TOP_K) // blk) + N_EXP
    starts = jnp.arange(n_blocks, dtype=I32)[:, None] * blk
    blk_exp = jnp.minimum(jnp.sum((pend[None, :] <= starts).astype(I32), axis=1), N_EXP - 1)
    e = eid[:, :TOP_K]
    dest = pstart[e] + rank[:, :TOP_K]
    tok = jnp.broadcast_to(jnp.arange(m, dtype=I32)[:, None], (m, TOP_K))
    row_tok = jnp.zeros((n_blocks * blk,), I32).at[dest.reshape(-1)].set(tok.reshape(-1))
    ys = _experts(h2, row_tok.reshape(n_blocks, 1, blk), blk_exp, exp_w1, exp_w3, exp_w2)
    dest_t = jnp.transpose(dest.reshape(m // tm, tm, TOP_K), (0, 2, 1))
    return _final(x1, gate, ys, dest_t, wts, norm_f, tm, rpb)


def _pack_in_proj(w_in):
    o = C_RIN
    w_r = w_in[:, :o]
    w_q = w_in[:, o:o + C_N + 3 * KV_COLS]
    o += C_N + 3 * KV_COLS
    w_gn = w_in[:, o:o + 3 * H_N].reshape(D_MODEL, N_KVG, 3 * HPG)
    w_gn = jnp.pad(w_gn, ((0, 0), (0, 0), (0, GN_GROUP_COLS - 3 * HPG))).reshape(D_MODEL, N_KVG * GN_GROUP_COLS)
    o += 3 * H_N
    w_gm = w_in[:, o:]
    return w_r.astype(BF), jnp.concatenate([w_q, w_gn], axis=1).astype(BF), w_gm.astype(BF)


def _prompt_bias_tables(rel_bias, t):
    tq, tk = ATT_TILE, ATT_TK
    i = np.arange(tq)[None, :]
    j = np.arange(tk)[:, None]
    dist = np.stack([tk * (o - (ATT_R - 1)) + i - j for o in range(N_WIN_OFFS)]).astype(np.int32)
    raw = _bias_table(jnp.asarray(dist.reshape(N_WIN_OFFS * tk, tq)), rel_bias, tk)
    raw = jnp.transpose(raw.reshape(H_N, N_WIN_OFFS, tk, tq), (1, 0, 2, 3))
    ok_w = jnp.asarray((dist >= 0) & (dist <= WINDOW))[:, None]
    ok_s = jnp.asarray(dist[:N_SEL_OFFS] >= 0)[:, None]
    tab_w = jnp.where(ok_w, raw, NEG_INF).reshape(N_WIN_OFFS, H_N * tk, tq)
    tab_s = jnp.where(ok_s, raw[:N_SEL_OFFS], NEG_INF).reshape(N_SEL_OFFS, H_N * tk, tq)
    nc = (t - L_CMP) // STRIDE + 1
    ncp = nc + 1
    dc = (np.arange(t)[:, None] - (np.arange(ncp)[None, :] * STRIDE + L_CMP - 1)).astype(np.int32)
    return tab_s, tab_w, _bias_table(jnp.asarray(dc), rel_bias, tq)


def kernel(x_prompt, x_sample, c_prompt, c_sample, cache_cmp_kv, cache_sel_kv, state_win_kv, state_rwkv_shift,
           state_rwkv_wkv, page_table, rel_bias, norm_f, norm1, norm2, w_ada, b_ada, w_in, rwkv_mu, rwkv_w0, rwkv_w2,
           rwkv_a0, rwkv_a2, rwkv_g2, rwkv_kk, rwkv_ka, rwkv_rk, rwkv_ln_g, rwkv_ln_b, cmp_pos, cmp_w1, cmp_w2,
           w_o_rwkv, w_o_nsa, w_out, router_wg, router_bg, router_we, router_be, exp_w1, exp_w3, exp_w2):
    bp, t, _ = x_prompt.shape
    bs = x_sample.shape[0]
    mp = bp * t
    past = page_table.shape[1] * PAGE_SIZE

    nrow = -(-(bp + bs) // 8) * 8
    c_all = jnp.concatenate([c_prompt, c_sample, jnp.zeros((nrow - bp - bs, D_MODEL), F32)], axis=0)
    mod = _ada(c_all, w_ada[0], b_ada[0]).reshape(nrow, 6, D_MODEL)
    mod_p = [mod[:bp, i][:, None, :] for i in range(6)]
    mod_s = [mod[bp:bp + bs, i][None] for i in range(6)]

    w_r, w_n, w_gm = _pack_in_proj(w_in[0])
    rw = _rwkv_weights(rwkv_mu[0], rwkv_w0[0], rwkv_w2[0], rwkv_a0[0], rwkv_a2[0], rwkv_g2[0], rwkv_kk[0],
                       rwkv_ka[0], rwkv_rk[0], rwkv_ln_g[0], rwkv_ln_b[0])
    wbd = _cmp_weights(cmp_w1[0], N_KVG)
    cpos = _cmp_partial_rows(_cmp_pos_rows(cmp_pos[0]), wbd, 8)
    wo_r, wo_n, wo = w_o_rwkv[0].astype(BF), w_o_nsa[0].astype(BF), w_out[0].astype(BF)
    w_router = jnp.pad(jnp.concatenate([router_wg[0], router_we[0]], axis=1),
                       ((0, 0), (0, ROUTER_COLS - N_EGROUPS - N_EXP))).astype(BF)
    b_router = jnp.pad(jnp.concatenate([router_bg[0], router_be[0]]), (0, ROUTER_COLS - N_EGROUPS - N_EXP))[None]

    tm = 512
    xp = x_prompt.reshape(mp, D_MODEL)
    h = _norm_mod(xp, norm1[0], mod_p[0], mod_p[1], tm, t)
    p_r = _matmul(h, w_r, tm, C_RIN // 2)
    p_n = _matmul(h, w_n, tm, NP_COLS // 2)
    p_g = _matmul(h, w_gm, tm, 2048)
    o_r, shift_p, wkv_p = _rwkv_prompt(p_r.reshape(bp, t, C_RIN), rw)
    kvc = p_n[:, NP_KVC:NP_KVC + KV_COLS]
    kvs = p_n[:, NP_KVS:NP_KVS + KV_COLS]
    kvw = p_n[:, NP_KVW:NP_KVW + KV_COLS]
    nch = t // STRIDE
    c_part = _cmp_partial_rows(kvc.reshape(bp * nch, STRIDE * KV_COLS), wbd, nch)
    kv_cmp = _cmp_finish(c_part.reshape(bp, nch, -1), cpos, cmp_w2[0])
    tab_s, tab_w, bias_c = _prompt_bias_tables(rel_bias, t)
    o_cmp, sel = _cmp_attn_prompt(p_n, kv_cmp, bias_c, t)
    o_n = _swa_prompt(p_n, *_swa_operands(p_n, sel, bp, t), tab_s, tab_w, o_cmp, t)
    y = _merge(o_r.reshape(mp, C_R), o_n, wo_r, wo_n, p_g, tm)
    x1 = _proj_residual(y, wo, xp, mod_p[2], tm, t)
    y_prompt = _moe_and_final(x1, norm2[0], mod_p[3], mod_p[4], mod_p[5], w_router, b_router, exp_w1[0], exp_w3[0],
                              exp_w2[0], norm_f, tm, t, 128).reshape(bp, t, D_MODEL)
    kv_shape = (1, bp, t, 2, N_KVG, HD_N)
    wlen = min(WINDOW, t)
    win_p = kvw.reshape(bp, t, KV_COLS)[:, t - wlen:].reshape(1, bp, wlen, 2, N_KVG, HD_N)

    xs = x_sample.reshape(bs, D_MODEL)
    hs = _norm_mod(xs, norm1[0], mod_s[0], mod_s[1], bs, bs)
    ps_r = _matmul(hs, w_r, bs, C_RIN // 2)
    ps_n = _matmul(hs, w_n, bs, NP_COLS // 2)
    ps_g = _matmul(hs, w_gm, bs, 2048)
    os_r, wkv_s = _rwkv_step(ps_r, state_rwkv_shift[0], state_rwkv_wkv[0], rw)
    kvc_s = ps_n[:, NP_KVC:NP_KVC + KV_COLS]
    kvs_s = ps_n[:, NP_KVS:NP_KVS + KV_COLS]
    kvw_s = ps_n[:, NP_KVW:NP_KVW + KV_COLS]
    cs_part = _cmp_partial_paged(jnp.transpose(cache_cmp_kv[0], (0, 2, 3, 4, 1)), page_table,
                                 _cmp_weights(cmp_w1[0], CMP_PAIR))
    kv_cmp_s = _cmp_finish(cs_part, cpos, cmp_w2[0])
    q_s = ps_n[:, :C_N].reshape(bs, H_N, HD_N)
    o_cmp_s, picks = _sample_cmp(q_s, kv_cmp_s, rel_bias, past)
    idx = picks[:, :N_KVG, :N_TOP]
    bpp = PAGE_SIZE // L_SEL
    npb = past // L_SEL
    idc = jnp.minimum(idx, npb - 1)
    page = jnp.take_along_axis(page_table, (idc // bpp).reshape(bs, -1), axis=1).reshape(bs * N_KVG, N_TOP)
    cache_t = jnp.transpose(cache_sel_kv[0], (0, 2, 3, 4, 1))
    blocks = _gather_sel_pages(cache_t, page).reshape(bs, N_KVG, N_TOP, 2, HD_N, PAGE_SIZE)
    win_buf = state_win_kv[0].reshape(bs, -1, KV_COLS)
    gates_s = ps_n[:, NP_GN:].reshape(bs, N_KVG, GN_GROUP_COLS)[:, :, :3 * HPG].reshape(bs, H_N, 3)
    new_kv = jnp.stack([kvs_s, kvw_s], axis=1)
    os_n = _sample_swa(idx, q_s, blocks, win_buf, new_kv, rel_bias, gates_s, o_cmp_s, past)
    ys = _merge(os_r, os_n.reshape(bs, C_N).astype(BF), wo_r, wo_n, ps_g, bs)
    xs1 = _proj_residual(ys, wo, xs, mod_s[2], bs, bs)
    y_sample = _moe_and_final(xs1, norm2[0], mod_s[3], mod_s[4], mod_s[5], w_router, b_router, exp_w1[0], exp_w3[0],
                              exp_w2[0], norm_f, bs, bs, 16).reshape(bs, 1, D_MODEL)
    kv1 = (1, bs, 1, 2, N_KVG, HD_N)
    wbuf = win_buf.shape[1]
    win_s = jnp.concatenate([win_buf, kvw_s[:, None, :]], axis=1)[:, -wbuf:].reshape(1, bs, wbuf, 2, N_KVG, HD_N)

    return (y_prompt, y_sample,
            kvc.reshape(kv_shape), kvc_s.reshape(kv1),
            kvs.reshape(kv_shape), kvs_s.reshape(kv1),
            win_p, win_s,
            shift_p.reshape(1, bp, C_RIN), ps_r.reshape(1, bs, C_RIN),
            wkv_p[None], wkv_s[None])
```

```python
import functools
import math

import numpy as np
import jax
import jax.numpy as jnp
from jax import lax
from jax.experimental import pallas as pl
from jax.experimental.pallas import tpu as pltpu

D_MODEL = 2048
PAGE_SIZE = 128
H_R, HD_R = 16, 64
C_R = H_R * HD_R
LORA_W, LORA_A, LORA_G = 64, 64, 128
C_RIN = 3 * C_R + LORA_W + LORA_A + LORA_G
LN_X_EPS = 64e-5
H_N, HD_N, N_KVG = 16, 64, 4
HPG = H_N // N_KVG
C_N = H_N * HD_N
KV_COLS = 2 * N_KVG * HD_N
L_CMP, STRIDE, CMP_HID = 32, 16, 64
L_SEL, N_TOP, WINDOW = 64, 16, 512
N_BUCKETS, MAX_DIST = 32, 128
N_EGROUPS, EXP_PER_GROUP = 4, 8
N_EXP = N_EGROUPS * EXP_PER_GROUP
TOP_K, D_EXP = 2, 512
RMS_EPS = 1e-6
NEG_INF = -1e30
FORCE = 1e9

BF = jnp.bfloat16
F32 = jnp.float32
I32 = jnp.int32

VMEM_LIMIT_BYTES = 56 * 1024 * 1024
LANES = 128
RW_CHUNK = 32
ATT_TILE = 256
GN_GROUP_COLS = 128
NP_Q, NP_KVC, NP_KVS, NP_KVW, NP_GN = 0, C_N, C_N + KV_COLS, C_N + 2 * KV_COLS, C_N + 3 * KV_COLS
NP_COLS = NP_GN + N_KVG * GN_GROUP_COLS


def _cparams(*sem):
    return pltpu.CompilerParams(dimension_semantics=sem, vmem_limit_bytes=VMEM_LIMIT_BYTES)


def _dot(a, b):
    return jnp.dot(a.astype(BF), b.astype(BF), preferred_element_type=F32)


def _dot_nt(a, b):
    return lax.dot_general(a.astype(BF), b.astype(BF), (((1,), (1,)), ((), ())), preferred_element_type=F32)


def _dot_tn(a, b):
    return lax.dot_general(a.astype(BF), b.astype(BF), (((0,), (0,)), ((), ())), preferred_element_type=F32)


def _softplus(x):
    return jnp.maximum(x, 0.0) + jnp.log1p(jnp.exp(-jnp.abs(x)))


def _sigmoid(x):
    return 1.0 / (1.0 + jnp.exp(-x))


def _gelu_tanh(x):
    return 0.5 * x * (1.0 + jnp.tanh(math.sqrt(2.0 / math.pi) * (x + 0.044715 * x * x * x)))


def _t5_bucket(dist):
    n = jnp.maximum(dist, 0)
    max_exact = N_BUCKETS // 2
    nf = jnp.maximum(n, 1).astype(F32)
    large = max_exact + (jnp.log(nf / max_exact) / math.log(MAX_DIST / max_exact)
                         * (N_BUCKETS - max_exact)).astype(I32)
    large = jnp.minimum(large, N_BUCKETS - 1)
    return jnp.where(n < max_exact, n, large)


def _bias_rows(dist, rbt):
    bucket = _t5_bucket(dist)
    out = jnp.zeros((rbt.shape[0], dist.shape[1]), F32)
    for b in range(N_BUCKETS):
        out = jnp.where(bucket == b, rbt[:, b:b + 1], out)
    return out


def _ada_kernel(c_ref, w_ref, b_ref, o_ref):
    o_ref[...] = _dot(c_ref[...], w_ref[...]) + b_ref[...]


def _ada(c, w_ada, b_ada):
    r = c.shape[0]
    n = w_ada.shape[1]
    tn = 1024
    return pl.pallas_call(
        _ada_kernel,
        grid=(n // tn,),
        in_specs=[pl.BlockSpec((r, D_MODEL), lambda j: (0, 0)),
                  pl.BlockSpec((D_MODEL, tn), lambda j: (0, j)),
                  pl.BlockSpec((1, tn), lambda j: (0, j))],
        out_specs=pl.BlockSpec((r, tn), lambda j: (0, j)),
        out_shape=jax.ShapeDtypeStruct((r, n), F32),
        compiler_params=_cparams("arbitrary"),
        name="ada_mod",
    )(c, w_ada, b_ada.reshape(1, n))


def _rms(x, g):
    return x * lax.rsqrt(jnp.mean(x * x, axis=-1, keepdims=True) + RMS_EPS) * g


def _norm_mod_kernel(x_ref, g_ref, sh_ref, sc_ref, o_ref):
    o_ref[...] = (_rms(x_ref[...], g_ref[...]) * (1.0 + sc_ref[...]) + sh_ref[...]).astype(o_ref.dtype)


def _row_specs(m, tm, rpb):
    del m
    return (lambda tn: pl.BlockSpec((tm, tn), lambda i, j: (i, j)),
            lambda r, tn: pl.BlockSpec((None, r, tn), lambda i, j: ((i * tm) // rpb, 0, j)))


def _norm_mod(x, g, shift, scale, tm, rpb):
    m = x.shape[0]
    r = shift.shape[1]
    rows, mods = _row_specs(m, tm, rpb)
    return pl.pallas_call(
        _norm_mod_kernel,
        grid=(m // tm, 1),
        in_specs=[rows(D_MODEL), pl.BlockSpec((1, D_MODEL), lambda i, j: (0, 0)), mods(r, D_MODEL), mods(r, D_MODEL)],
        out_specs=rows(D_MODEL),
        out_shape=jax.ShapeDtypeStruct((m, D_MODEL), BF),
        compiler_params=_cparams("arbitrary", "arbitrary"),
        name="norm_mod",
    )(x, g.reshape(1, D_MODEL), shift, scale)


def _mm_kernel(a_ref, w_ref, o_ref):
    o_ref[...] = jnp.dot(a_ref[...], w_ref[...], preferred_element_type=F32).astype(o_ref.dtype)


def _matmul(a, w, tm, tn, out_dtype=F32):
    m, k = a.shape
    n = w.shape[1]
    return pl.pallas_call(
        _mm_kernel,
        grid=(m // tm, n // tn),
        in_specs=[pl.BlockSpec((tm, k), lambda i, j: (i, 0)), pl.BlockSpec((k, tn), lambda i, j: (0, j))],
        out_specs=pl.BlockSpec((tm, tn), lambda i, j: (i, j)),
        out_shape=jax.ShapeDtypeStruct((m, n), out_dtype),
        compiler_params=_cparams("arbitrary", "arbitrary"),
        name="matmul",
    )(a, w)


def _merge_kernel(or_ref, on_ref, wr_ref, wn_ref, g0_ref, g1_ref, o_ref):
    yr = jnp.dot(or_ref[...], wr_ref[...], preferred_element_type=F32)
    yn = jnp.dot(on_ref[...], wn_ref[...], preferred_element_type=F32)
    o_ref[...] = (_sigmoid(g0_ref[...]) * yr + _sigmoid(g1_ref[...]) * yn).astype(o_ref.dtype)


def _merge(o_r, o_n, w_r, w_n, p_g, tm):
    m = o_r.shape[0]
    tn = 1024
    nb = D_MODEL // tn
    return pl.pallas_call(
        _merge_kernel,
        grid=(m // tm, nb),
        in_specs=[pl.BlockSpec((tm, C_R), lambda i, j: (i, 0)), pl.BlockSpec((tm, C_N), lambda i, j: (i, 0)),
                  pl.BlockSpec((C_R, tn), lambda i, j: (0, j)), pl.BlockSpec((C_N, tn), lambda i, j: (0, j)),
                  pl.BlockSpec((tm, tn), lambda i, j: (i, j)), pl.BlockSpec((tm, tn), lambda i, j: (i, j + nb))],
        out_specs=pl.BlockSpec((tm, tn), lambda i, j: (i, j)),
        out_shape=jax.ShapeDtypeStruct((m, D_MODEL), BF),
        compiler_params=_cparams("arbitrary", "arbitrary"),
        name="merge_branches",
    )(o_r, o_n, w_r, w_n, p_g, p_g)


def _proj_res_kernel(y_ref, w_ref, x_ref, g_ref, o_ref):
    o_ref[...] = x_ref[...] + g_ref[...] * jnp.dot(y_ref[...], w_ref[...], preferred_element_type=F32)


def _proj_residual(y, w, x, gate, tm, rpb):
    m = y.shape[0]
    tn = 1024
    r = gate.shape[1]
    rows, mods = _row_specs(m, tm, rpb)
    return pl.pallas_call(
        _proj_res_kernel,
        grid=(m // tm, D_MODEL // tn),
        in_specs=[pl.BlockSpec((tm, D_MODEL), lambda i, j: (i, 0)), pl.BlockSpec((D_MODEL, tn), lambda i, j: (0, j)),
                  rows(tn), mods(r, tn)],
        out_specs=rows(tn),
        out_shape=jax.ShapeDtypeStruct((m, D_MODEL), F32),
        compiler_params=_cparams("arbitrary", "arbitrary"),
        name="out_proj_residual",
    )(y, w, x, gate)


def _rwkv_features(p, p_prev, mu, w0, w2, a0, a2, g2, k_k, k_a):
    xm = p + (p_prev - p) * mu
    r = xm[:, :C_R]
    k = xm[:, C_R:2 * C_R]
    v = xm[:, 2 * C_R:3 * C_R]
    o = 3 * C_R
    wd = xm[:, o:o + LORA_W]
    ad = xm[:, o + LORA_W:o + LORA_W + LORA_A]
    gd = xm[:, o + LORA_W + LORA_A:]
    w_log = -_softplus(-(w0 + _dot(jnp.tanh(wd), w2))) - 0.5
    lw = -jnp.exp(w_log)
    a = _sigmoid(a0 + _dot(ad, a2))
    g = _dot(_sigmoid(gd), g2)
    kk = k * k_k
    k = k * (1.0 + (a - 1.0) * k_a)
    return r, k, v, lw, a, g, kk


def _rwkv_head_out(y, r_h, k_h, v_h, g_h, rk_h, lng_h, lnb_h):
    mean = jnp.mean(y, axis=-1, keepdims=True)
    yc = y - mean
    var = jnp.mean(yc * yc, axis=-1, keepdims=True)
    yn = yc * lax.rsqrt(var + LN_X_EPS) * lng_h + lnb_h
    bonus = jnp.sum(r_h * k_h * rk_h, axis=-1, keepdims=True) * v_h
    return (yn + bonus) * g_h


def _head_sums(x, ones2):
    nt = x.shape[1] // LANES
    xs = jnp.concatenate([x[:, j * LANES:(j + 1) * LANES] for j in range(nt)], axis=0)
    hi = xs.astype(BF)
    lo = (xs - hi.astype(F32)).astype(BF)
    s = jnp.dot(hi, ones2, preferred_element_type=F32) + jnp.dot(lo, ones2, preferred_element_type=F32)
    r = x.shape[0]
    return jnp.concatenate([s[j * r:(j + 1) * r] for j in range(nt)], axis=1)


def _rwkv_chunk_kernel(pr_ref, mu_ref, w0_ref, w2_ref, a0_ref, a2_ref, g2_ref, kk_ref, ka_ref, rk_ref,
                       lng_ref, lnb_ref, ones_ref, o_ref, shift_ref, state_ref, y_ref):
    c = pl.program_id(1)
    C = RW_CHUNK

    @pl.when(c == 0)
    def _():
        shift_ref[...] = jnp.zeros_like(shift_ref)
        state_ref[...] = jnp.zeros_like(state_ref)

    p = pr_ref[...]
    row = lax.broadcasted_iota(I32, (C, 1), 0)
    p_prev = jnp.where(row == 0, shift_ref[...], pltpu.roll(p, 1, axis=0))
    shift_ref[...] = p[C - 1:C, :]
    r, k, v, lw, a, g, kk_all = _rwkv_features(p, p_prev, mu_ref[...], w0_ref[...], w2_ref[...], a0_ref[...],
                                               a2_ref[...], g2_ref[...], kk_ref[...], ka_ref[...])
    cl = lw
    s = 1
    while s < C:
        cl = cl + jnp.where(row >= s, pltpu.roll(cl, s, axis=0), 0.0)
        s *= 2
    ti = lax.broadcasted_iota(I32, (C, C), 0)
    si = lax.broadcasted_iota(I32, (C, C), 1)
    strict = ti > si
    incl = ti >= si
    eye = (ti == si).astype(F32)
    heads = range(H_R)
    sls = [slice(h * HD_R, (h + 1) * HD_R) for h in heads]
    ones2 = ones_ref[...]
    kk_n = kk_all / jnp.maximum(jnp.sqrt(_head_sums(kk_all * kk_all, ones2)), 1e-12)
    b_all = kk_n * a
    cl_end = cl[C - 1:C, :]
    e_neg = jnp.exp(-cl)
    e_end = jnp.exp(cl_end - cl)
    g_end_all = jnp.exp(cl_end)
    per_head = lambda z: [z[:, sl] for sl in sls]
    kkt_all = (kk_n * jnp.exp(cl - lw)).astype(BF)
    rt_all = (r * jnp.exp(cl)).astype(BF)
    kr = per_head(jnp.concatenate([kkt_all, rt_all], axis=0))
    kh = per_head((k * e_neg).astype(BF))
    bh = per_head((b_all * e_neg).astype(BF))
    kbb = per_head(jnp.concatenate([(k * e_end).astype(BF), (b_all * e_end).astype(BF)], axis=0))
    vb = per_head(v.astype(BF))
    g_end = per_head(g_end_all)
    ak = [_dot_nt(kr[h], kh[h]) for h in heads]
    ab = [_dot_nt(kr[h], bh[h]) for h in heads]
    lg = [jnp.concatenate([jnp.where(strict, ak[h][:C], 0.0), jnp.where(incl, ak[h][C:], 0.0)], axis=0).astype(BF)
          for h in heads]
    nil = [jnp.where(strict, -ab[h][:C], 0.0) for h in heads]
    grb = [jnp.where(incl, ab[h][C:], 0.0).astype(BF) for h in heads]
    tinv = [eye + n for n in nil]
    m = 2
    while m < C:
        nil = [_dot(n, n) for n in nil]
        tinv = [t + _dot(t, n) for t, n in zip(tinv, nil)]
        m *= 2
    s0 = [state_ref[h] for h in heads]
    xy = [_dot_nt(kr[h], s0[h]) + _dot(lg[h], vb[h]) for h in heads]
    u = [_dot(tinv[h], xy[h][:C]).astype(BF) for h in heads]
    y = [xy[h][C:] - _dot(grb[h], u[h]) for h in heads]
    for h in heads:
        vu = jnp.concatenate([vb[h], -u[h]], axis=0)
        state_ref[h] = s0[h] * g_end[h] + _dot_tn(vu, kbb[h])
    for h, sl in enumerate(sls):
        y_ref[:, sl] = y[h]
    y_all = y_ref[...]
    yc = y_all - _head_sums(y_all, ones2) * (1.0 / HD_R)
    var = _head_sums(yc * yc, ones2) * (1.0 / HD_R)
    yn = yc * lax.rsqrt(var + LN_X_EPS) * lng_ref[...] + lnb_ref[...]
    bonus = _head_sums(r * k * rk_ref[...], ones2) * v
    o_ref[...] = ((yn + bonus) * g).astype(o_ref.dtype)


def _rwkv_weights(mu, w0, w2, a0, a2, g2, k_k, k_a, r_k, ln_g, ln_b):
    row = lambda z: z.reshape(1, -1).astype(F32)
    return (row(mu), row(w0), w2.astype(BF), row(a0), a2.astype(BF), g2.astype(BF), row(k_k), row(k_a), row(r_k),
            row(ln_g), row(ln_b))


_RWKV_W_SHAPES = ((1, C_RIN), (1, C_R), (LORA_W, C_R), (1, C_R), (LORA_A, C_R), (LORA_G, C_R), (1, C_R), (1, C_R),
                  (1, C_R), (1, C_R), (1, C_R))


def _rwkv_prompt(pr, rw):
    b, t, _ = pr.shape
    C = RW_CHUNK
    full = lambda shp: pl.BlockSpec(shp, lambda i, j: (0,) * len(shp))
    lane_head = np.arange(LANES) // HD_R
    ones2 = jnp.asarray(lane_head[:, None] == lane_head[None, :], BF)
    return pl.pallas_call(
        _rwkv_chunk_kernel,
        grid=(b, t // C),
        in_specs=[pl.BlockSpec((None, C, C_RIN), lambda i, j: (i, j, 0))] + [full(s) for s in _RWKV_W_SHAPES]
        + [full((LANES, LANES))],
        out_specs=[pl.BlockSpec((None, C, C_R), lambda i, j: (i, j, 0)),
                   pl.BlockSpec((None, 1, C_RIN), lambda i, j: (i, 0, 0)),
                   pl.BlockSpec((None, H_R, HD_R, HD_R), lambda i, j: (i, 0, 0, 0))],
        out_shape=[jax.ShapeDtypeStruct((b, t, C_R), BF),
                   jax.ShapeDtypeStruct((b, 1, C_RIN), F32),
                   jax.ShapeDtypeStruct((b, H_R, HD_R, HD_R), F32)],
        scratch_shapes=[pltpu.VMEM((C, C_R), F32)],
        compiler_params=_cparams("arbitrary", "arbitrary"),
        name="rwkv_chunk",
    )(pr, *rw, ones2)


def _rwkv_step_kernel(pr_ref, prev_ref, s0_ref, mu_ref, w0_ref, w2_ref, a0_ref, a2_ref, g2_ref, kk_ref, ka_ref,
                      rk_ref, lng_ref, lnb_ref, o_ref, state_ref):
    nb = pr_ref.shape[0]
    r, k, v, lw, a, g, kk_all = _rwkv_features(pr_ref[...], prev_ref[...], mu_ref[...], w0_ref[...], w2_ref[...],
                                               a0_ref[...], a2_ref[...], g2_ref[...], kk_ref[...], ka_ref[...])
    decay = jnp.exp(lw)
    ii = lax.broadcasted_iota(I32, (HD_R, HD_R), 0)
    jj = lax.broadcasted_iota(I32, (HD_R, HD_R), 1)
    eye = ii == jj
    col = lambda z: jnp.sum(jnp.where(eye, z, 0.0), axis=1, keepdims=True)
    for bi in range(nb):
        for h in range(H_R):
            sl = slice(h * HD_R, (h + 1) * HD_R)
            rows = lambda z: z[bi:bi + 1, sl]
            r_h, k_h, v_h, a_h, w_h = rows(r), rows(k), rows(v), rows(a), rows(decay)
            kk_h = rows(kk_all)
            kk_h = kk_h / jnp.maximum(jnp.sqrt(jnp.sum(kk_h * kk_h, axis=-1, keepdims=True)), 1e-12)
            b_h = kk_h * a_h
            s0 = s0_ref[bi, h]
            sa = jnp.sum(s0 * (-kk_h), axis=1, keepdims=True)
            s1 = s0 * w_h + sa * b_h + col(v_h) * k_h
            state_ref[bi, h] = s1
            y_col = jnp.sum(s1 * r_h, axis=1, keepdims=True)
            y = jnp.sum(jnp.where(eye, y_col, 0.0), axis=0, keepdims=True)
            o_ref[bi:bi + 1, sl] = _rwkv_head_out(y, r_h, k_h, v_h, rows(g), rk_ref[:, sl], lng_ref[:, sl],
                                                  lnb_ref[:, sl]).astype(o_ref.dtype)


def _rwkv_step(pr, prev, s0, rw):
    b = pr.shape[0]
    return pl.pallas_call(
        _rwkv_step_kernel,
        out_shape=[jax.ShapeDtypeStruct((b, C_R), BF), jax.ShapeDtypeStruct((b, H_R, HD_R, HD_R), F32)],
        compiler_params=pltpu.CompilerParams(vmem_limit_bytes=VMEM_LIMIT_BYTES),
        name="rwkv_step",
    )(pr, prev, s0, *rw)


def _cmp_partial_kernel(*refs):
    x_refs, w_ref, o_ref = refs[:-2], refs[-2], refs[-1]
    x = x_refs[0][...] if len(x_refs) == 1 else jnp.concatenate([r[...] for r in x_refs], axis=0)
    half = N_KVG * HD_N
    for s in range(2):
        acc = jnp.zeros((x.shape[0], N_KVG * 2 * CMP_HID), F32)
        for p in range(STRIDE):
            o = p * KV_COLS + s * half
            acc = acc + _dot(x[:, o:o + half], w_ref[p, s])
        o_ref[:, s * N_KVG * 2 * CMP_HID:(s + 1) * N_KVG * 2 * CMP_HID] = acc


CMP_PAIR = 2


def _cmp_partial_paged_kernel(pt_ref, *refs):
    del pt_ref
    x_refs, perm_ref, w_ref, o_ref = refs[:-3], refs[-3], refs[-2], refs[-1]
    cpp = PAGE_SIZE // STRIDE
    width = CMP_PAIR * 2 * CMP_HID
    perm = perm_ref[...]
    for s in range(2):
        for gp in range(N_KVG // CMP_PAIR):
            rows = []
            for x_ref in x_refs:
                tile = jnp.concatenate([x_ref[s, CMP_PAIR * gp + j] for j in range(CMP_PAIR)], axis=0)
                rows.append(_dot_nt(perm, tile))
            acc = jnp.zeros((len(x_refs) * cpp, width), F32)
            for p in range(STRIDE):
                lhs = jnp.concatenate([r[p * cpp:(p + 1) * cpp] for r in rows], axis=0)
                acc = acc + _dot(lhs, w_ref[p, s])
            o = (s * (N_KVG // CMP_PAIR) + gp) * width
            o_ref[:, o:o + width] = acc


def _cmp_weights(cmp_w1, groups):
    w1r = cmp_w1.reshape(2, 2, STRIDE, HD_N, CMP_HID)
    w = jnp.transpose(w1r, (2, 0, 3, 1, 4))
    w = w.reshape(STRIDE, 2, 1, HD_N, 1, 2 * CMP_HID)
    eye = jnp.eye(groups, dtype=w.dtype).reshape(1, 1, groups, 1, groups, 1)
    wbd = eye * w
    return wbd.reshape(STRIDE, 2, groups * HD_N, groups * 2 * CMP_HID).astype(BF)


def _cmp_partial_rows(x, wbd, tr):
    r = x.shape[0]
    n = 2 * N_KVG * 2 * CMP_HID
    return pl.pallas_call(
        _cmp_partial_kernel,
        grid=(r // tr,),
        in_specs=[pl.BlockSpec((tr, STRIDE * KV_COLS), lambda i: (i, 0)),
                  pl.BlockSpec(wbd.shape, lambda i: (0, 0, 0, 0))],
        out_specs=pl.BlockSpec((tr, n), lambda i: (i, 0)),
        out_shape=jax.ShapeDtypeStruct((r, n), F32),
        compiler_params=_cparams("arbitrary"),
        name="cmp_partial",
    )(x, wbd)


PAGES_PER_STEP = 16


def _cmp_partial_paged(cache_t, page_table, wpair):
    b, npg = page_table.shape
    cpp = PAGE_SIZE // STRIDE
    n = 2 * N_KVG * 2 * CMP_HID
    steps = npg // PAGES_PER_STEP
    perm = np.zeros((PAGE_SIZE, PAGE_SIZE), np.float32)
    tok = np.arange(PAGE_SIZE)
    perm[(tok % STRIDE) * cpp + tok // STRIDE, tok] = 1.0

    def page_spec(kpg):
        return pl.BlockSpec((None, 2, N_KVG, HD_N, PAGE_SIZE),
                            lambda i, j, pt: (pt[i, j * PAGES_PER_STEP + kpg], 0, 0, 0, 0))

    grid_spec = pltpu.PrefetchScalarGridSpec(
        num_scalar_prefetch=1,
        grid=(b, steps),
        in_specs=[page_spec(kpg) for kpg in range(PAGES_PER_STEP)]
        + [pl.BlockSpec((PAGE_SIZE, PAGE_SIZE), lambda i, j, pt: (0, 0)),
           pl.BlockSpec(wpair.shape, lambda i, j, pt: (0, 0, 0, 0))],
        out_specs=pl.BlockSpec((None, PAGES_PER_STEP * cpp, n), lambda i, j, pt: (i, j, 0)),
    )
    return pl.pallas_call(
        _cmp_partial_paged_kernel,
        grid_spec=grid_spec,
        out_shape=jax.ShapeDtypeStruct((b, npg * cpp, n), F32),
        compiler_params=_cparams("arbitrary", "arbitrary"),
        name="cmp_partial_paged",
    )(page_table, *([cache_t] * PAGES_PER_STEP), jnp.asarray(perm, BF), wpair)


def _cmp_finish_kernel(c_ref, cpos_ref, w2_ref, o_ref):
    c = c_ref[...]
    nrow = c.shape[0]
    c_next = pltpu.roll(c, nrow - 1, axis=0)
    for s in range(2):
        for gi in range(N_KVG):
            o = (s * N_KVG + gi) * 2 * CMP_HID
            hid = (c[:, o:o + CMP_HID] + cpos_ref[0:1, o:o + CMP_HID]
                   + c_next[:, o + CMP_HID:o + 2 * CMP_HID] + cpos_ref[1:2, o + CMP_HID:o + 2 * CMP_HID])
            oo = (s * N_KVG + gi) * HD_N
            o_ref[:, oo:oo + HD_N] = _dot(_gelu_tanh(hid), w2_ref[s])


def _cmp_finish(c, cpos, w2):
    b, nch, n = c.shape
    return pl.pallas_call(
        _cmp_finish_kernel,
        grid=(b,),
        in_specs=[pl.BlockSpec((None, nch, n), lambda i: (i, 0, 0)), pl.BlockSpec(cpos.shape, lambda i: (0, 0)),
                  pl.BlockSpec(w2.shape, lambda i: (0, 0, 0))],
        out_specs=pl.BlockSpec((None, nch, KV_COLS), lambda i: (i, 0, 0)),
        out_shape=jax.ShapeDtypeStruct((b, nch, KV_COLS), F32),
        compiler_params=_cparams("arbitrary"),
        name="cmp_finish",
    )(c, cpos, w2.astype(BF))


def _cmp_pos_rows(cmp_pos):
    pos = cmp_pos.reshape(2, STRIDE, 1, 1, HD_N)
    rows = jnp.broadcast_to(pos, (2, STRIDE, 2, N_KVG, HD_N)).reshape(2, STRIDE * KV_COLS)
    return jnp.concatenate([rows, jnp.zeros((6, STRIDE * KV_COLS), F32)], axis=0)


def _bias_table_kernel(dist_ref, rb_ref, o_ref):
    bucket = _t5_bucket(dist_ref[...])
    for h in range(H_N):
        out = jnp.zeros(bucket.shape, F32)
        for b in range(N_BUCKETS):
            out = jnp.where(bucket == b, rb_ref[b, h], out)
        o_ref[h] = out


def _bias_table(dist, rel_bias, tr):
    r, n = dist.shape
    return pl.pallas_call(
        _bias_table_kernel,
        grid=(r // tr,),
        in_specs=[pl.BlockSpec((tr, n), lambda i: (i, 0)),
                  pl.BlockSpec(memory_space=pltpu.SMEM)],
        out_specs=pl.BlockSpec((H_N, tr, n), lambda i: (0, i, 0)),
        out_shape=jax.ShapeDtypeStruct((H_N, r, n), F32),
        compiler_params=_cparams("arbitrary"),
        name="bias_table",
    )(dist, rel_bias)


def _softmax_rows(logits, valid):
    lm = jnp.where(valid, logits, NEG_INF)
    e = jnp.exp(lm - jnp.max(lm, axis=-1, keepdims=True))
    return e / jnp.sum(e, axis=-1, keepdims=True)


def _cmp_attn_kernel(q_ref, kv_ref, bias_ref, ovt_ref, o_ref, sel_ref):
    tq = q_ref.shape[0]
    ncp = kv_ref.shape[0]
    nsb = ovt_ref.shape[0]
    q0 = pl.program_id(1) * tq
    qpos = q0 + lax.broadcasted_iota(I32, (tq, 1), 0)
    cend = lax.broadcasted_iota(I32, (1, ncp), 1) * STRIDE + (L_CMP - 1)
    valid = (qpos >= cend) & (lax.broadcasted_iota(I32, (1, ncp), 1) < ncp - 1)
    validf = valid.astype(F32)
    q = q_ref[...] * (HD_N ** -0.5)
    blk = lax.broadcasted_iota(I32, (nsb, tq), 0)
    cur = (q0 + lax.broadcasted_iota(I32, (1, tq), 1)) // L_SEL
    forced = (blk == 0) | (blk == cur) | (blk == cur - 1)
    future = blk > cur
    for gi in range(N_KVG):
        kc = kv_ref[:, gi * HD_N:(gi + 1) * HD_N]
        vc = kv_ref[:, (N_KVG + gi) * HD_N:(N_KVG + gi + 1) * HD_N]
        pcs = jnp.zeros((tq, ncp), F32)
        for hl in range(HPG):
            h = gi * HPG + hl
            sl = slice(h * HD_N, (h + 1) * HD_N)
            pc = _softmax_rows(_dot_nt(q[:, sl], kc) + bias_ref[h], valid) * validf
            pcs = pcs + pc
            o_ref[:, sl] = _dot(pc, vc)
        imp = _dot_nt(ovt_ref[...], pcs)
        score = jnp.where(forced, FORCE, jnp.where(future, -FORCE, imp))
        rank = jnp.zeros((nsb, tq), F32)
        for i in range(nsb):
            si = score[i:i + 1, :]
            rank = rank + ((si > score) | ((si == score) & (i < blk))).astype(F32)
        sel_ref[gi] = (rank < N_TOP).astype(sel_ref.dtype)


def _cmp_sel_overlap_t(nc, ncp, nsb):
    s = np.arange(nc)[None, :] * STRIDE
    j = np.arange(nsb)[:, None] * L_SEL
    ov = np.clip(np.minimum(s + L_CMP, j + L_SEL) - np.maximum(s, j), 0, None) / L_CMP
    return np.pad(ov, ((0, 0), (0, ncp - nc))).astype(np.float32)


def _cmp_attn_prompt(p_n, kv_cmp, bias_c, t):
    b = kv_cmp.shape[0]
    ncp = kv_cmp.shape[1]
    nsb = t // L_SEL
    tq = ATT_TILE
    nqt = t // tq
    ovt = jnp.asarray(_cmp_sel_overlap_t(ncp - 1, ncp, nsb), BF)
    return pl.pallas_call(
        _cmp_attn_kernel,
        grid=(b, nqt),
        in_specs=[pl.BlockSpec((tq, C_N), lambda i, j: (i * nqt + j, 0)),
                  pl.BlockSpec((None, ncp, KV_COLS), lambda i, j: (i, 0, 0)),
                  pl.BlockSpec((H_N, tq, ncp), lambda i, j: (0, j, 0)),
                  pl.BlockSpec((nsb, ncp), lambda i, j: (0, 0))],
        out_specs=[pl.BlockSpec((tq, C_N), lambda i, j: (i * nqt + j, 0)),
                   pl.BlockSpec((None, N_KVG, nsb, tq), lambda i, j: (i, 0, 0, j))],
        out_shape=[jax.ShapeDtypeStruct((b * t, C_N), F32), jax.ShapeDtypeStruct((b, N_KVG, nsb, t), BF)],
        compiler_params=_cparams("arbitrary", "arbitrary"),
        name="nsa_cmp_select",
    )(p_n, kv_cmp, bias_c, ovt)


ATT_TK = 128
ATT_R = ATT_TILE // ATT_TK
N_SEL_OFFS = ATT_R + (MAX_DIST + ATT_TK - 1) // ATT_TK + 1
N_WIN_OFFS = ATT_R + WINDOW // ATT_TK
QA_COLS = HD_N + 32
SEL_STEP_TILES = 4
WIN_STEP_TILES = 2


def _swa_kernel(qa_ref, ks_ref, vs_ref, kw_ref, vw_ref, tabs_ref, tabw_ref, gn_ref, oc_ref, o_ref):
    tq = qa_ref.shape[1]
    qt = pl.program_id(2)
    top = ATT_R * qt + ATT_R - 1

    heads = range(HPG)

    def attend(k_ref, v_ref, tab_ref, lo, n_off, step_tiles):
        tiles = range(step_tiles)

        def body(kp, carry):
            ki = [lo + step_tiles * kp + j for j in tiles]
            kt = [k_ref[jnp.minimum(i, top)] for i in ki]
            vt = [v_ref[jnp.minimum(i, top)] for i in ki]
            off = [jnp.where(i > top, n_off, jnp.minimum(top - i, n_off - 1)) for i in ki]
            s = [[lax.dot_general(kt[j], qa_ref[hl], (((1,), (1,)), ((), ())), preferred_element_type=F32)
                  + tab_ref[off[j], hl * ATT_TK:(hl + 1) * ATT_TK, :] for j in tiles] for hl in heads]
            m_new = [functools.reduce(jnp.maximum, [carry[hl][0]] + [jnp.max(s[hl][j], axis=0, keepdims=True)
                                                                    for j in tiles]) for hl in heads]
            alpha = [jnp.exp(carry[hl][0] - m_new[hl]) for hl in heads]
            p = [[jnp.exp(s[hl][j] - m_new[hl]) for j in tiles] for hl in heads]
            l = [alpha[hl] * carry[hl][1] + sum(jnp.sum(p[hl][j], axis=0, keepdims=True) for j in tiles)
                 for hl in heads]
            acc = [alpha[hl] * carry[hl][2] + sum(jnp.dot(vt[j], p[hl][j].astype(BF), preferred_element_type=F32)
                                                  for j in tiles) for hl in heads]
            return tuple((m_new[hl], l[hl], acc[hl]) for hl in heads)

        init = tuple((jnp.full((1, tq), NEG_INF, F32), jnp.zeros((1, tq), F32), jnp.zeros((HD_N, tq), F32))
                     for _ in heads)
        res = lax.fori_loop(0, (top - lo) // step_tiles + 1, body, init)
        return [acc / l for _, l, acc in res]

    o_sel = attend(ks_ref, vs_ref, tabs_ref, 0, N_SEL_OFFS, SEL_STEP_TILES)
    o_win = attend(kw_ref, vw_ref, tabw_ref, jnp.maximum(top + 1 - N_WIN_OFFS, 0), N_WIN_OFFS, WIN_STEP_TILES)
    gates = _sigmoid(gn_ref[...])
    gates_t = gates.T
    for hl in heads:
        sl = slice(hl * HD_N, (hl + 1) * HD_N)
        o_t = gates_t[3 * hl + 1:3 * hl + 2, :] * o_sel[hl] + gates_t[3 * hl + 2:3 * hl + 3, :] * o_win[hl]
        o_ref[:, sl] = (gates[:, 3 * hl:3 * hl + 1] * oc_ref[:, sl] + o_t.T).astype(o_ref.dtype)


def _swa_prompt(p_n, qa, ks, vs, kw, vw, tab_s, tab_w, o_cmp, t):
    b = qa.shape[0]
    tq = ATT_TILE
    nqt = t // tq
    nkt = t // ATT_TK
    gw = HPG * HD_N
    k_spec = pl.BlockSpec((None, None, nkt, ATT_TK, QA_COLS), lambda i, g, j: (i, g, 0, 0, 0))
    v_spec = pl.BlockSpec((None, None, nkt, HD_N, ATT_TK), lambda i, g, j: (i, g, 0, 0, 0))
    tab_spec = lambda n: pl.BlockSpec((n + 1, HPG * ATT_TK, tq), lambda i, g, j: (0, g, 0))
    return pl.pallas_call(
        _swa_kernel,
        grid=(b, N_KVG, nqt),
        in_specs=[pl.BlockSpec((None, HPG, tq, QA_COLS), lambda i, g, j: (i, g, j, 0)),
                  k_spec, v_spec, k_spec, v_spec, tab_spec(N_SEL_OFFS), tab_spec(N_WIN_OFFS),
                  pl.BlockSpec((tq, GN_GROUP_COLS), lambda i, g, j: (i * nqt + j, NP_GN // GN_GROUP_COLS + g)),
                  pl.BlockSpec((tq, gw), lambda i, g, j: (i * nqt + j, g))],
        out_specs=pl.BlockSpec((tq, gw), lambda i, g, j: (i * nqt + j, g)),
        out_shape=jax.ShapeDtypeStruct((b * t, C_N), BF),
        compiler_params=_cparams("arbitrary", "arbitrary", "arbitrary"),
        name="nsa_sel_win",
    )(qa, ks, vs, kw, vw, tab_s, tab_w, p_n, o_cmp)


def _swa_operands(p_n, sel, b, t):
    nsb = t // L_SEL
    nkt = t // ATT_TK
    q = (p_n[:, :C_N] * (HD_N ** -0.5)).reshape(b, t, H_N, HD_N).transpose(0, 2, 1, 3)
    pen = jnp.where(jnp.transpose(sel, (0, 1, 3, 2)) > 0.5, 0.0, NEG_INF).astype(F32)
    pen = jnp.broadcast_to(pen[:, :, None], (b, N_KVG, HPG, t, nsb)).reshape(b, H_N, t, nsb)
    qa = jnp.concatenate([q, pen], axis=-1).astype(BF)

    def split(cols):
        kv = p_n[:, cols:cols + KV_COLS].reshape(b, t, 2, N_KVG, HD_N)
        k = jnp.transpose(kv[:, :, 0], (0, 2, 1, 3))
        v = jnp.transpose(kv[:, :, 1], (0, 2, 3, 1)).reshape(b, N_KVG, HD_N, nkt, ATT_TK)
        return k, jnp.transpose(v, (0, 1, 3, 2, 4)).astype(BF)

    onehot = jnp.asarray(np.arange(t)[:, None] // L_SEL == np.arange(nsb)[None, :], F32)
    k_s, v_s = split(NP_KVS)
    k_w, v_w = split(NP_KVW)
    ext = lambda k, e: jnp.concatenate([k, jnp.broadcast_to(e, (b, N_KVG, t, nsb))], axis=-1).astype(BF) \
        .reshape(b, N_KVG, nkt, ATT_TK, QA_COLS)
    return qa, ext(k_s, onehot), v_s, ext(k_w, jnp.zeros((t, nsb), F32)), v_w


def _sample_cmp_kernel(q_ref, kv_ref, rbt_ref, ov_ref, o_ref, idx_ref, *, past):
    ncp = kv_ref.shape[0]
    nsbp = ov_ref.shape[1]
    nsb = -(-(past + 1) // L_SEL)
    q = q_ref[...] * (HD_N ** -0.5)
    hrow = lax.broadcasted_iota(I32, (H_N, 1), 0)
    nidx = lax.broadcasted_iota(I32, (1, ncp), 1)
    valid = nidx < ncp - 1
    bias = _bias_rows(past - (nidx * STRIDE + (L_CMP - 1)), rbt_ref[...])
    logits = jnp.zeros((H_N, ncp), F32)
    for gi in range(N_KVG):
        lg = _dot_nt(q, kv_ref[:, gi * HD_N:(gi + 1) * HD_N])
        logits = jnp.where(hrow // HPG == gi, lg, logits)
    pc = _softmax_rows(logits + bias, valid) * valid.astype(F32)
    o = jnp.zeros((H_N, HD_N), F32)
    for gi in range(N_KVG):
        og = _dot(pc, kv_ref[:, (N_KVG + gi) * HD_N:(N_KVG + gi + 1) * HD_N])
        o = jnp.where(hrow // HPG == gi, og, o)
    o_ref[...] = o
    imp_h = _dot(pc, ov_ref[...])
    blk = lax.broadcasted_iota(I32, (8, nsbp), 1)
    grow = lax.broadcasted_iota(I32, (8, 1), 0)
    cur = past // L_SEL
    score = jnp.full((8, nsbp), -3e38, F32)
    for gi in range(N_KVG):
        imp = jnp.sum(jnp.where(hrow // HPG == gi, imp_h, 0.0), axis=0, keepdims=True)
        score = jnp.where(grow == gi, imp, score)
    forced = (blk == 0) | (blk == cur) | (blk == cur - 1)
    score = jnp.where(forced, FORCE, jnp.where(blk > cur, -FORCE, score))
    score = jnp.where((blk < nsb) & (grow < N_KVG), score, -3e38)
    lane = lax.broadcasted_iota(I32, (8, LANES), 1)
    picks = jnp.zeros((8, LANES), I32)
    for it in range(N_TOP):
        mx = jnp.max(score, axis=-1, keepdims=True)
        pick = jnp.min(jnp.where(score == mx, blk, nsbp), axis=-1, keepdims=True)
        picks = jnp.where(lane == it, pick, picks)
        score = jnp.where(blk == pick, -3e38, score)
    idx_ref[...] = picks


def _sample_cmp(q, kv_cmp, rel_bias, past):
    b, ncp, _ = kv_cmp.shape
    nc = ncp - 1
    nsb = -(-(past + 1) // L_SEL)
    nsbp = -(-nsb // LANES) * LANES
    s = np.arange(nc)[:, None] * STRIDE
    j = np.arange(nsb)[None, :] * L_SEL
    ov = np.clip(np.minimum(s + L_CMP, j + L_SEL) - np.maximum(s, j), 0, None) / L_CMP
    ov = np.pad(ov, ((0, ncp - nc), (0, nsbp - nsb))).astype(np.float32)
    return pl.pallas_call(
        functools.partial(_sample_cmp_kernel, past=past),
        grid=(b,),
        in_specs=[pl.BlockSpec((None, H_N, HD_N), lambda i: (i, 0, 0)),
                  pl.BlockSpec((None, ncp, KV_COLS), lambda i: (i, 0, 0)),
                  pl.BlockSpec((H_N, N_BUCKETS), lambda i: (0, 0)),
                  pl.BlockSpec((ncp, nsbp), lambda i: (0, 0))],
        out_specs=[pl.BlockSpec((None, H_N, HD_N), lambda i: (i, 0, 0)),
                   pl.BlockSpec((None, 8, LANES), lambda i: (i, 0, 0))],
        out_shape=[jax.ShapeDtypeStruct((b, H_N, HD_N), F32), jax.ShapeDtypeStruct((b, 8, LANES), I32)],
        compiler_params=_cparams("arbitrary"),
        name="nsa_sample_cmp_select",
    )(q, kv_cmp, rel_bias.T, jnp.asarray(ov, BF))


def _block_copy_kernel(pg_ref, *refs):
    del pg_ref
    x_refs, o_ref = refs[:-1], refs[-1]
    for n, x_ref in enumerate(x_refs):
        o_ref[n] = x_ref[...]


def _gather_sel_pages(cache_t, page):
    rows, nslot = page.shape

    def slot_spec(n):
        return pl.BlockSpec((None, 2, None, HD_N, PAGE_SIZE), lambda i, pg: (pg[i, n], 0, i % N_KVG, 0, 0))

    return pl.pallas_call(
        _block_copy_kernel,
        grid_spec=pltpu.PrefetchScalarGridSpec(
            num_scalar_prefetch=1, grid=(rows,),
            in_specs=[slot_spec(n) for n in range(nslot)],
            out_specs=pl.BlockSpec((None, nslot, 2, HD_N, PAGE_SIZE), lambda i, pg: (i, 0, 0, 0, 0))),
        out_shape=jax.ShapeDtypeStruct((rows, nslot, 2, HD_N, PAGE_SIZE), cache_t.dtype),
        compiler_params=_cparams("arbitrary"),
        name="gather_sel_pages",
    )(page, *([cache_t] * nslot))


def _sample_swa_kernel(idx_ref, q_ref, blk_ref, win_ref, new_ref, rbt_ref, gate_ref, oc_ref, o_ref, *, past):
    bi = pl.program_id(0)
    q = q_ref[...] * (HD_N ** -0.5)
    rbt = rbt_ref[...]
    hrow = lax.broadcasted_iota(I32, (H_N, 1), 0)
    nk = N_TOP * PAGE_SIZE
    lane = lax.broadcasted_iota(I32, (1, nk), 1)
    new_blk = past // L_SEL
    bias_new = rbt[:, 0:1]
    gates = _sigmoid(gate_ref[...])
    nwin = win_ref.shape[0]
    wdist = nwin - lax.broadcasted_iota(I32, (1, nwin), 1)
    wbias = _bias_rows(wdist, rbt)
    wvalid = (wdist >= 0) & (wdist <= WINDOW)

    def with_new(logits, valid, weigh, k_new, v_new):
        l_new = jnp.sum(q * k_new, axis=-1, keepdims=True) + bias_new
        lm = jnp.where(valid, logits, NEG_INF)
        m = jnp.maximum(jnp.max(lm, axis=-1, keepdims=True), l_new)
        e = jnp.where(valid, jnp.exp(lm - m), 0.0)
        e_new = jnp.exp(l_new - m)
        den = jnp.sum(e, axis=-1, keepdims=True) + e_new
        return (weigh(e) + e_new * v_new) / den

    o_sel = jnp.zeros((H_N, HD_N), F32)
    o_win = jnp.zeros((H_N, HD_N), F32)
    bpp = PAGE_SIZE // L_SEL
    tok = lane % PAGE_SIZE
    for gi in range(N_KVG):
        ksl = slice(gi * HD_N, (gi + 1) * HD_N)
        vsl = slice((N_KVG + gi) * HD_N, (N_KVG + gi + 1) * HD_N)
        bid = jnp.zeros((1, nk), I32)
        for n in range(N_TOP):
            bid = jnp.where(lane // PAGE_SIZE == n, idx_ref[bi, gi, n], bid)
        dist = past - ((bid // bpp) * PAGE_SIZE + tok)
        valid = (bid != new_blk) & (tok // L_SEL == bid % bpp) & (dist >= 0)
        kt = jnp.concatenate([blk_ref[gi, n, 0] for n in range(N_TOP)], axis=1)
        vt = jnp.concatenate([blk_ref[gi, n, 1] for n in range(N_TOP)], axis=1)
        logits = _dot(q, kt) + _bias_rows(dist, rbt)
        og = with_new(logits, valid, lambda e, vt=vt: _dot_nt(e, vt), new_ref[0:1, ksl], new_ref[0:1, vsl])
        o_sel = jnp.where(hrow // HPG == gi, og, o_sel)
        logits = _dot_nt(q, win_ref[:, ksl]) + wbias
        og = with_new(logits, wvalid, lambda e, vsl=vsl: _dot(e, win_ref[:, vsl]), new_ref[1:2, ksl],
                      new_ref[1:2, vsl])
        o_win = jnp.where(hrow // HPG == gi, og, o_win)
    o_ref[...] = gates[:, 0:1] * oc_ref[...] + gates[:, 1:2] * o_sel + gates[:, 2:3] * o_win


def _sample_swa(idx, q, blocks, win, new_kv, rel_bias, gates, o_cmp, past):
    b = q.shape[0]
    w = win.shape[1]
    grid_spec = pltpu.PrefetchScalarGridSpec(
        num_scalar_prefetch=1, grid=(b,),
        in_specs=[pl.BlockSpec((None, H_N, HD_N), lambda i, ix: (i, 0, 0)),
                  pl.BlockSpec((None, N_KVG, N_TOP, 2, HD_N, PAGE_SIZE), lambda i, ix: (i, 0, 0, 0, 0, 0)),
                  pl.BlockSpec((None, w, KV_COLS), lambda i, ix: (i, 0, 0)),
                  pl.BlockSpec((None, 2, KV_COLS), lambda i, ix: (i, 0, 0)),
                  pl.BlockSpec((H_N, N_BUCKETS), lambda i, ix: (0, 0)),
                  pl.BlockSpec((None, H_N, 3), lambda i, ix: (i, 0, 0)),
                  pl.BlockSpec((None, H_N, HD_N), lambda i, ix: (i, 0, 0))],
        out_specs=pl.BlockSpec((None, H_N, HD_N), lambda i, ix: (i, 0, 0)))
    return pl.pallas_call(
        functools.partial(_sample_swa_kernel, past=past),
        grid_spec=grid_spec,
        out_shape=jax.ShapeDtypeStruct((b, H_N, HD_N), F32),
        compiler_params=_cparams("arbitrary"),
        name="nsa_sample_sel_win",
    )(idx, q, blocks, win, new_kv, rel_bias.T, gates, o_cmp)


ROUTER_COLS = LANES


def _router_kernel(x_ref, g_ref, sh_ref, sc_ref, w_ref, b_ref, h_ref, e_ref, wt_ref, rk_ref, cnt_ref):
    i = pl.program_id(0)
    tm = x_ref.shape[0]

    @pl.when(i == 0)
    def _():
        cnt_ref[...] = jnp.zeros_like(cnt_ref)

    h = (_rms(x_ref[...], g_ref[...]) * (1.0 + sc_ref[...]) + sh_ref[...]).astype(BF)
    h_ref[:, 0, :] = h.astype(h_ref.dtype)
    logits = jnp.dot(h, w_ref[...], preferred_element_type=F32) + b_ref[...]
    lane = lax.broadcasted_iota(I32, (tm, ROUTER_COLS), 1)

    def top1(vals, ok):
        vm = jnp.where(ok, vals, -3e38)
        mx = jnp.max(vm, axis=-1, keepdims=True)
        return mx, jnp.min(jnp.where(ok & (vm == mx), lane, ROUTER_COLS), axis=-1, keepdims=True)

    isg = lane < N_EGROUPS
    pg = _softmax_rows(logits, isg)
    g_w, g_i = top1(pg, isg)
    ise = (lane >= N_EGROUPS) & ((lane - N_EGROUPS) // EXP_PER_GROUP == g_i)
    pe = _softmax_rows(logits, ise)
    w0, l0 = top1(pe, ise)
    w1, l1 = top1(pe, ise & (lane != l0))
    den = w0 + w1
    e0 = l0 - N_EGROUPS
    e1 = l1 - N_EGROUPS
    e_ref[...] = jnp.where(lane == 0, e0, jnp.where(lane == 1, e1, 0))
    wt_ref[...] = jnp.where(lane == 0, w0 / den * g_w, jnp.where(lane == 1, w1 / den * g_w, 0.0))
    oh0 = (lane == e0).astype(F32)
    oh1 = (lane == e1).astype(F32)
    cnt = oh0 + oh1
    ti = lax.broadcasted_iota(I32, (tm, tm), 0)
    si = lax.broadcasted_iota(I32, (tm, tm), 1)
    before = _dot((ti > si).astype(F32), cnt) + cnt_ref[...]
    r0 = jnp.sum(before * oh0, axis=-1, keepdims=True)
    r1 = jnp.sum(before * oh1, axis=-1, keepdims=True)
    rk_ref[...] = jnp.where(lane == 0, r0, jnp.where(lane == 1, r1, 0.0)).astype(I32)
    cnt_ref[...] = cnt_ref[...] + jnp.sum(cnt, axis=0, keepdims=True)


def _router(x, g, shift, scale, w_r, b_r, tm, rpb):
    m = x.shape[0]
    r = shift.shape[1]
    rows = lambda tn: pl.BlockSpec((tm, tn), lambda i: (i, 0))
    mods = pl.BlockSpec((None, r, D_MODEL), lambda i: ((i * tm) // rpb, 0, 0))
    small = lambda dt: jax.ShapeDtypeStruct((m, ROUTER_COLS), dt)
    return pl.pallas_call(
        _router_kernel,
        grid=(m // tm,),
        in_specs=[rows(D_MODEL), pl.BlockSpec((1, D_MODEL), lambda i: (0, 0)), mods, mods,
                  pl.BlockSpec((D_MODEL, ROUTER_COLS), lambda i: (0, 0)),
                  pl.BlockSpec((1, ROUTER_COLS), lambda i: (0, 0))],
        out_specs=[pl.BlockSpec((tm, 1, D_MODEL), lambda i: (i, 0, 0)),
                   rows(ROUTER_COLS), rows(ROUTER_COLS), rows(ROUTER_COLS),
                   pl.BlockSpec((1, ROUTER_COLS), lambda i: (0, 0))],
        out_shape=[jax.ShapeDtypeStruct((m, 1, D_MODEL), BF), small(I32), small(F32), small(I32),
                   jax.ShapeDtypeStruct((1, ROUTER_COLS), F32)],
        compiler_params=_cparams("arbitrary"),
        name="moe_router",
    )(x, g.reshape(1, D_MODEL), shift, scale, w_r, b_r)


def _row_gather_ring(src_hbm, buf, sems, groups, idx_now, idx_next):
    i = pl.program_id(0)
    last = pl.num_programs(0) - 1
    slot = i % 2
    total = sum(cnt for _, cnt, _ in groups)

    def start_all(idx, s):
        for first, cnt, k in groups:
            def issue(j, c):
                for u in range(2):
                    r = 2 * j + u
                    pltpu.make_async_copy(src_hbm.at[idx(k, r)], buf.at[s, first + r], sems.at[s]).start(priority=u)
                return c
            lax.fori_loop(0, cnt // 2, issue, 0)

    @pl.when(i == 0)
    def _():
        start_all(idx_now, slot)

    @pl.when(i < last)
    def _():
        start_all(idx_next, 1 - slot)

    assert total == buf.shape[1]
    pltpu.make_async_copy(buf.at[slot], buf.at[slot], sems.at[slot]).wait()
    return slot


def _expert_kernel(be_ref, rt_ref, rtn_ref, h_hbm, w1_ref, w3_ref, w2_ref, o_ref, xbuf, sems, w1b, w3b, w2b):
    i = pl.program_id(0)
    blk = xbuf.shape[1]
    slot = _row_gather_ring(h_hbm, xbuf, sems, ((0, blk, 0),), lambda k, r: rt_ref[0, r], lambda k, r: rtn_ref[0, r])

    @pl.when((i == 0) | (be_ref[i] != be_ref[jnp.maximum(i - 1, 0)]))
    def _():
        w1b[...] = w1_ref[...].astype(BF)
        w3b[...] = w3_ref[...].astype(BF)
        w2b[...] = w2_ref[...].astype(BF)

    x = xbuf[slot, :, 0, :]
    a = jnp.dot(x, w1b[...], preferred_element_type=F32)
    b = jnp.dot(x, w3b[...], preferred_element_type=F32)
    hid = a * _sigmoid(a) * b
    o_ref[:, 0, :] = jnp.dot(hid.astype(BF), w2b[...], preferred_element_type=F32)


def _experts(h2, row_tok, blk_exp, w1, w3, w2):
    nblk, _, blk = row_tok.shape
    idx_spec = lambda d: pl.BlockSpec((None, 1, blk), lambda i, be: (jnp.minimum(i + d, nblk - 1), 0, 0),
                                      memory_space=pltpu.SMEM)
    grid_spec = pltpu.PrefetchScalarGridSpec(
        num_scalar_prefetch=1, grid=(nblk,),
        in_specs=[idx_spec(0), idx_spec(1),
                  pl.BlockSpec(memory_space=pl.ANY),
                  pl.BlockSpec((None, D_MODEL, D_EXP), lambda i, be: (be[i], 0, 0)),
                  pl.BlockSpec((None, D_MODEL, D_EXP), lambda i, be: (be[i], 0, 0)),
                  pl.BlockSpec((None, D_EXP, D_MODEL), lambda i, be: (be[i], 0, 0))],
        out_specs=pl.BlockSpec((blk, 1, D_MODEL), lambda i, be: (i, 0, 0)),
        scratch_shapes=[pltpu.VMEM((2, blk, 1, D_MODEL), h2.dtype), pltpu.SemaphoreType.DMA((2,)),
                        pltpu.VMEM((D_MODEL, D_EXP), BF), pltpu.VMEM((D_MODEL, D_EXP), BF),
                        pltpu.VMEM((D_EXP, D_MODEL), BF)])
    return pl.pallas_call(
        _expert_kernel,
        grid_spec=grid_spec,
        out_shape=jax.ShapeDtypeStruct((nblk * blk, 1, D_MODEL), F32),
        compiler_params=_cparams("arbitrary"),
        name="moe_experts",
    )(blk_exp, row_tok, row_tok, h2, w1, w3, w2)


def _final_kernel(x_ref, g_ref, dest_ref, destn_ref, ys_hbm, wt_ref, nf_ref, o_ref, ybuf, sems):
    tm = x_ref.shape[0]
    groups = tuple((k * tm, tm, k) for k in range(TOP_K))
    slot = _row_gather_ring(ys_hbm, ybuf, sems, groups, lambda k, r: dest_ref[k, r], lambda k, r: destn_ref[k, r])
    wt = wt_ref[...]
    moe = wt[:, 0:1] * ybuf[slot, 0:tm, 0, :] + wt[:, 1:2] * ybuf[slot, tm:2 * tm, 0, :]
    o_ref[...] = _rms(x_ref[...] + g_ref[...] * moe, nf_ref[...])


def _final(x, gate, ys, dest, wts, norm_f, tm, rpb):
    m = x.shape[0]
    r = gate.shape[1]
    nt = m // tm
    rows = lambda tn: pl.BlockSpec((tm, tn), lambda i: (i, 0))
    idx_spec = lambda d: pl.BlockSpec((None, TOP_K, tm), lambda i: (jnp.minimum(i + d, nt - 1), 0, 0),
                                      memory_space=pltpu.SMEM)
    return pl.pallas_call(
        _final_kernel,
        grid=(nt,),
        in_specs=[rows(D_MODEL), pl.BlockSpec((None, r, D_MODEL), lambda i: ((i * tm) // rpb, 0, 0)),
                  idx_spec(0), idx_spec(1),
                  pl.BlockSpec(memory_space=pl.ANY),
                  rows(ROUTER_COLS), pl.BlockSpec((1, D_MODEL), lambda i: (0, 0))],
        out_specs=rows(D_MODEL),
        out_shape=jax.ShapeDtypeStruct((m, D_MODEL), F32),
        scratch_shapes=[pltpu.VMEM((2, TOP_K * tm, 1, D_MODEL), F32), pltpu.SemaphoreType.DMA((2,))],
        compiler_params=_cparams("arbitrary"),
        name="moe_combine_final_norm",
    )(x, gate, dest, dest, ys, wts, norm_f.reshape(1, D_MODEL))


def _moe_and_final(x1, g2, shift, scale, gate, w_r, b_r, exp_w1, exp_w3, exp_w2, norm_f, tm, rpb, blk):
    m = x1.shape[0]
    h2, eid, wts, rank, counts = _router(x1, g2, shift, scale, w_r, b_r, tm, rpb)
    counts = counts[0, :N_EXP].astype(I32)
    padded = (counts + blk - 1) // blk * blk
    pend = jnp.cumsum(padded)
    pstart = pend - padded
    n_blocks = -(-(m * TOP_K) // blk) + N_EXP
    starts = jnp.arange(n_blocks, dtype=I32)[:, None] * blk
    blk_exp = jnp.minimum(jnp.sum((pend[None, :] <= starts).astype(I32), axis=1), N_EXP - 1)
    e = eid[:, :TOP_K]
    dest = pstart[e] + rank[:, :TOP_K]
    tok = jnp.broadcast_to(jnp.arange(m, dtype=I32)[:, None], (m, TOP_K))
    row_tok = jnp.zeros((n_blocks * blk,), I32).at[dest.reshape(-1)].set(tok.reshape(-1))
    ys = _experts(h2, row_tok.reshape(n_blocks, 1, blk), blk_exp, exp_w1, exp_w3, exp_w2)
    dest_t = jnp.transpose(dest.reshape(m // tm, tm, TOP_K), (0, 2, 1))
    return _final(x1, gate, ys, dest_t, wts, norm_f, tm, rpb)


def _pack_in_proj(w_in):
    o = C_RIN
    w_r = w_in[:, :o]
    w_q = w_in[:, o:o + C_N + 3 * KV_COLS]
    o += C_N + 3 * KV_COLS
    w_gn = w_in[:, o:o + 3 * H_N].reshape(D_MODEL, N_KVG, 3 * HPG)
    w_gn = jnp.pad(w_gn, ((0, 0), (0, 0), (0, GN_GROUP_COLS - 3 * HPG))).reshape(D_MODEL, N_KVG * GN_GROUP_COLS)
    o += 3 * H_N
    w_gm = w_in[:, o:]
    return w_r.astype(BF), jnp.concatenate([w_q, w_gn], axis=1).astype(BF), w_gm.astype(BF)


def _prompt_bias_tables(rel_bias, t):
    tq, tk = ATT_TILE, ATT_TK
    i = np.arange(tq)[None, :]
    j = np.arange(tk)[:, None]
    dist = np.stack([tk * (o - (ATT_R - 1)) + i - j for o in range(N_WIN_OFFS)]).astype(np.int32)
    raw = _bias_table(jnp.asarray(dist.reshape(N_WIN_OFFS * tk, tq)), rel_bias, tk)
    raw = jnp.transpose(raw.reshape(H_N, N_WIN_OFFS, tk, tq), (1, 0, 2, 3))
    ok_w = jnp.asarray((dist >= 0) & (dist <= WINDOW))[:, None]
    ok_s = jnp.asarray(dist[:N_SEL_OFFS] >= 0)[:, None]
    masked = jnp.full((1, H_N * tk, tq), NEG_INF, F32)
    tab_w = jnp.concatenate([jnp.where(ok_w, raw, NEG_INF).reshape(N_WIN_OFFS, H_N * tk, tq), masked])
    tab_s = jnp.concatenate([jnp.where(ok_s, raw[:N_SEL_OFFS], NEG_INF).reshape(N_SEL_OFFS, H_N * tk, tq), masked])
    nc = (t - L_CMP) // STRIDE + 1
    ncp = nc + 1
    dc = (np.arange(t)[:, None] - (np.arange(ncp)[None, :] * STRIDE + L_CMP - 1)).astype(np.int32)
    return tab_s, tab_w, _bias_table(jnp.asarray(dc), rel_bias, tq)


def kernel(x_prompt, x_sample, c_prompt, c_sample, cache_cmp_kv, cache_sel_kv, state_win_kv, state_rwkv_shift,
           state_rwkv_wkv, page_table, rel_bias, norm_f, norm1, norm2, w_ada, b_ada, w_in, rwkv_mu, rwkv_w0, rwkv_w2,
           rwkv_a0, rwkv_a2, rwkv_g2, rwkv_kk, rwkv_ka, rwkv_rk, rwkv_ln_g, rwkv_ln_b, cmp_pos, cmp_w1, cmp_w2,
           w_o_rwkv, w_o_nsa, w_out, router_wg, router_bg, router_we, router_be, exp_w1, exp_w3, exp_w2):
    bp, t, _ = x_prompt.shape
    bs = x_sample.shape[0]
    mp = bp * t
    past = page_table.shape[1] * PAGE_SIZE

    nrow = -(-(bp + bs) // 8) * 8
    c_all = jnp.concatenate([c_prompt, c_sample, jnp.zeros((nrow - bp - bs, D_MODEL), F32)], axis=0)
    mod = _ada(c_all, w_ada[0], b_ada[0]).reshape(nrow, 6, D_MODEL)
    mod_p = [mod[:bp, i][:, None, :] for i in range(6)]
    mod_s = [mod[bp:bp + bs, i][None] for i in range(6)]

    w_r, w_n, w_gm = _pack_in_proj(w_in[0])
    rw = _rwkv_weights(rwkv_mu[0], rwkv_w0[0], rwkv_w2[0], rwkv_a0[0], rwkv_a2[0], rwkv_g2[0], rwkv_kk[0],
                       rwkv_ka[0], rwkv_rk[0], rwkv_ln_g[0], rwkv_ln_b[0])
    wbd = _cmp_weights(cmp_w1[0], N_KVG)
    cpos = _cmp_partial_rows(_cmp_pos_rows(cmp_pos[0]), wbd, 8)
    wo_r, wo_n, wo = w_o_rwkv[0].astype(BF), w_o_nsa[0].astype(BF), w_out[0].astype(BF)
    w_router = jnp.pad(jnp.concatenate([router_wg[0], router_we[0]], axis=1),
                       ((0, 0), (0, ROUTER_COLS - N_EGROUPS - N_EXP))).astype(BF)
    b_router = jnp.pad(jnp.concatenate([router_bg[0], router_be[0]]), (0, ROUTER_COLS - N_EGROUPS - N_EXP))[None]

    tm = 512
    xp = x_prompt.reshape(mp, D_MODEL)
    h = _norm_mod(xp, norm1[0], mod_p[0], mod_p[1], tm, t)
    p_r = _matmul(h, w_r, tm, C_RIN // 2)
    p_n = _matmul(h, w_n, tm, NP_COLS // 2)
    p_g = _matmul(h, w_gm, tm, 2048)
    o_r, shift_p, wkv_p = _rwkv_prompt(p_r.reshape(bp, t, C_RIN), rw)
    kvc = p_n[:, NP_KVC:NP_KVC + KV_COLS]
    kvs = p_n[:, NP_KVS:NP_KVS + KV_COLS]
    kvw = p_n[:, NP_KVW:NP_KVW + KV_COLS]
    nch = t // STRIDE
    c_part = _cmp_partial_rows(kvc.reshape(bp * nch, STRIDE * KV_COLS), wbd, nch)
    kv_cmp = _cmp_finish(c_part.reshape(bp, nch, -1), cpos, cmp_w2[0])
    tab_s, tab_w, bias_c = _prompt_bias_tables(rel_bias, t)
    o_cmp, sel = _cmp_attn_prompt(p_n, kv_cmp, bias_c, t)
    o_n = _swa_prompt(p_n, *_swa_operands(p_n, sel, bp, t), tab_s, tab_w, o_cmp, t)
    y = _merge(o_r.reshape(mp, C_R), o_n, wo_r, wo_n, p_g, tm)
    x1 = _proj_residual(y, wo, xp, mod_p[2], tm, t)
    y_prompt = _moe_and_final(x1, norm2[0], mod_p[3], mod_p[4], mod_p[5], w_router, b_router, exp_w1[0], exp_w3[0],
                              exp_w2[0], norm_f, tm, t, 128).reshape(bp, t, D_MODEL)
    kv_shape = (1, bp, t, 2, N_KVG, HD_N)
    wlen = min(WINDOW, t)
    win_p = kvw.reshape(bp, t, KV_COLS)[:, t - wlen:].reshape(1, bp, wlen, 2, N_KVG, HD_N)

    xs = x_sample.reshape(bs, D_MODEL)
    hs = _norm_mod(xs, norm1[0], mod_s[0], mod_s[1], bs, bs)
    ps_r = _matmul(hs, w_r, bs, C_RIN // 2)
    ps_n = _matmul(hs, w_n, bs, NP_COLS // 2)
    ps_g = _matmul(hs, w_gm, bs, 2048)
    os_r, wkv_s = _rwkv_step(ps_r, state_rwkv_shift[0], state_rwkv_wkv[0], rw)
    kvc_s = ps_n[:, NP_KVC:NP_KVC + KV_COLS]
    kvs_s = ps_n[:, NP_KVS:NP_KVS + KV_COLS]
    kvw_s = ps_n[:, NP_KVW:NP_KVW + KV_COLS]
    cs_part = _cmp_partial_paged(jnp.transpose(cache_cmp_kv[0], (0, 2, 3, 4, 1)), page_table,
                                 _cmp_weights(cmp_w1[0], CMP_PAIR))
    kv_cmp_s = _cmp_finish(cs_part, cpos, cmp_w2[0])
    q_s = ps_n[:, :C_N].reshape(bs, H_N, HD_N)
    o_cmp_s, picks = _sample_cmp(q_s, kv_cmp_s, rel_bias, past)
    idx = picks[:, :N_KVG, :N_TOP]
    bpp = PAGE_SIZE // L_SEL
    npb = past // L_SEL
    idc = jnp.minimum(idx, npb - 1)
    page = jnp.take_along_axis(page_table, (idc // bpp).reshape(bs, -1), axis=1).reshape(bs * N_KVG, N_TOP)
    cache_t = jnp.transpose(cache_sel_kv[0], (0, 2, 3, 4, 1))
    blocks = _gather_sel_pages(cache_t, page).reshape(bs, N_KVG, N_TOP, 2, HD_N, PAGE_SIZE)
    win_buf = state_win_kv[0].reshape(bs, -1, KV_COLS)
    gates_s = ps_n[:, NP_GN:].reshape(bs, N_KVG, GN_GROUP_COLS)[:, :, :3 * HPG].reshape(bs, H_N, 3)
    new_kv = jnp.stack([kvs_s, kvw_s], axis=1)
    os_n = _sample_swa(idx, q_s, blocks, win_buf, new_kv, rel_bias, gates_s, o_cmp_s, past)
    ys = _merge(os_r, os_n.reshape(bs, C_N).astype(BF), wo_r, wo_n, ps_g, bs)
    xs1 = _proj_residual(ys, wo, xs, mod_s[2], bs, bs)
    y_sample = _moe_and_final(xs1, norm2[0], mod_s[3], mod_s[4], mod_s[5], w_router, b_router, exp_w1[0], exp_w3[0],
                              exp_w2[0], norm_f, bs, bs, 16).reshape(bs, 1, D_MODEL)
    kv1 = (1, bs, 1, 2, N_KVG, HD_N)
    wbuf = win_buf.shape[1]
    win_s = jnp.concatenate([win_buf, kvw_s[:, None, :]], axis=1)[:, -wbuf:].reshape(1, bs, wbuf, 2, N_KVG, HD_N)

    return (y_prompt, y_sample,
            kvc.reshape(kv_shape), kvc_s.reshape(kv1),
            kvs.reshape(kv_shape), kvs_s.reshape(kv1),
            win_p, win_s,
            shift_p.reshape(1, bp, C_RIN), ps_r.reshape(1, bs, C_RIN),
            wkv_p[None], wkv_s[None])
```

```python
import functools
import math

import numpy as np
import jax
import jax.numpy as jnp
from jax import lax
from jax.experimental import pallas as pl
from jax.experimental.pallas import tpu as pltpu

D_MODEL = 2048
PAGE_SIZE = 128
H_R, HD_R = 16, 64
C_R = H_R * HD_R
LORA_W, LORA_A, LORA_G = 64, 64, 128
C_RIN = 3 * C_R + LORA_W + LORA_A + LORA_G
LN_X_EPS = 64e-5
H_N, HD_N, N_KVG = 16, 64, 4
HPG = H_N // N_KVG
C_N = H_N * HD_N
KV_COLS = 2 * N_KVG * HD_N
L_CMP, STRIDE, CMP_HID = 32, 16, 64
L_SEL, N_TOP, WINDOW = 64, 16, 512
N_BUCKETS, MAX_DIST = 32, 128
N_EGROUPS, EXP_PER_GROUP = 4, 8
N_EXP = N_EGROUPS * EXP_PER_GROUP
TOP_K, D_EXP = 2, 512
RMS_EPS = 1e-6
NEG_INF = -1e30
FORCE = 1e9

BF = jnp.bfloat16
F32 = jnp.float32
I32 = jnp.int32

VMEM_LIMIT_BYTES = 56 * 1024 * 1024
LANES = 128
RW_CHUNK = 32
ATT_TILE = 256
GN_GROUP_COLS = 128
NP_Q, NP_KVC, NP_KVS, NP_KVW, NP_GN = 0, C_N, C_N + KV_COLS, C_N + 2 * KV_COLS, C_N + 3 * KV_COLS
NP_COLS = NP_GN + N_KVG * GN_GROUP_COLS


def _cparams(*sem):
    return pltpu.CompilerParams(dimension_semantics=sem, vmem_limit_bytes=VMEM_LIMIT_BYTES)


def _dot(a, b):
    return jnp.dot(a.astype(BF), b.astype(BF), preferred_element_type=F32)


def _dot_nt(a, b):
    return lax.dot_general(a.astype(BF), b.astype(BF), (((1,), (1,)), ((), ())), preferred_element_type=F32)


def _dot_tn(a, b):
    return lax.dot_general(a.astype(BF), b.astype(BF), (((0,), (0,)), ((), ())), preferred_element_type=F32)


def _softplus(x):
    return jnp.maximum(x, 0.0) + jnp.log1p(jnp.exp(-jnp.abs(x)))


def _sigmoid(x):
    return 1.0 / (1.0 + jnp.exp(-x))


def _gelu_tanh(x):
    return 0.5 * x * (1.0 + jnp.tanh(math.sqrt(2.0 / math.pi) * (x + 0.044715 * x * x * x)))


def _t5_bucket(dist):
    n = jnp.maximum(dist, 0)
    max_exact = N_BUCKETS // 2
    nf = jnp.maximum(n, 1).astype(F32)
    large = max_exact + (jnp.log(nf / max_exact) / math.log(MAX_DIST / max_exact)
                         * (N_BUCKETS - max_exact)).astype(I32)
    large = jnp.minimum(large, N_BUCKETS - 1)
    return jnp.where(n < max_exact, n, large)


def _bias_rows(dist, rbt):
    bucket = _t5_bucket(dist)
    out = jnp.zeros((rbt.shape[0], dist.shape[1]), F32)
    for b in range(N_BUCKETS):
        out = jnp.where(bucket == b, rbt[:, b:b + 1], out)
    return out


def _ada_kernel(c_ref, w_ref, b_ref, o_ref):
    o_ref[...] = _dot(c_ref[...], w_ref[...]) + b_ref[...]


def _ada(c, w_ada, b_ada):
    r = c.shape[0]
    n = w_ada.shape[1]
    tn = 1024
    return pl.pallas_call(
        _ada_kernel,
        grid=(n // tn,),
        in_specs=[pl.BlockSpec((r, D_MODEL), lambda j: (0, 0)),
                  pl.BlockSpec((D_MODEL, tn), lambda j: (0, j)),
                  pl.BlockSpec((1, tn), lambda j: (0, j))],
        out_specs=pl.BlockSpec((r, tn), lambda j: (0, j)),
        out_shape=jax.ShapeDtypeStruct((r, n), F32),
        compiler_params=_cparams("arbitrary"),
        name="ada_mod",
    )(c, w_ada, b_ada.reshape(1, n))


def _rms(x, g):
    return x * lax.rsqrt(jnp.mean(x * x, axis=-1, keepdims=True) + RMS_EPS) * g


def _norm_mod_kernel(x_ref, g_ref, sh_ref, sc_ref, o_ref):
    o_ref[...] = (_rms(x_ref[...], g_ref[...]) * (1.0 + sc_ref[...]) + sh_ref[...]).astype(o_ref.dtype)


def _row_specs(m, tm, rpb):
    del m
    return (lambda tn: pl.BlockSpec((tm, tn), lambda i, j: (i, j)),
            lambda r, tn: pl.BlockSpec((None, r, tn), lambda i, j: ((i * tm) // rpb, 0, j)))


def _norm_mod(x, g, shift, scale, tm, rpb):
    m = x.shape[0]
    r = shift.shape[1]
    rows, mods = _row_specs(m, tm, rpb)
    return pl.pallas_call(
        _norm_mod_kernel,
        grid=(m // tm, 1),
        in_specs=[rows(D_MODEL), pl.BlockSpec((1, D_MODEL), lambda i, j: (0, 0)), mods(r, D_MODEL), mods(r, D_MODEL)],
        out_specs=rows(D_MODEL),
        out_shape=jax.ShapeDtypeStruct((m, D_MODEL), BF),
        compiler_params=_cparams("arbitrary", "arbitrary"),
        name="norm_mod",
    )(x, g.reshape(1, D_MODEL), shift, scale)


def _mm_kernel(a_ref, w_ref, o_ref):
    o_ref[...] = jnp.dot(a_ref[...], w_ref[...], preferred_element_type=F32).astype(o_ref.dtype)


def _matmul(a, w, tm, tn, out_dtype=F32):
    m, k = a.shape
    n = w.shape[1]
    return pl.pallas_call(
        _mm_kernel,
        grid=(m // tm, n // tn),
        in_specs=[pl.BlockSpec((tm, k), lambda i, j: (i, 0)), pl.BlockSpec((k, tn), lambda i, j: (0, j))],
        out_specs=pl.BlockSpec((tm, tn), lambda i, j: (i, j)),
        out_shape=jax.ShapeDtypeStruct((m, n), out_dtype),
        compiler_params=_cparams("arbitrary", "arbitrary"),
        name="matmul",
    )(a, w)


def _merge_kernel(or_ref, on_ref, wr_ref, wn_ref, g0_ref, g1_ref, o_ref):
    yr = jnp.dot(or_ref[...], wr_ref[...], preferred_element_type=F32)
    yn = jnp.dot(on_ref[...], wn_ref[...], preferred_element_type=F32)
    o_ref[...] = (_sigmoid(g0_ref[...]) * yr + _sigmoid(g1_ref[...]) * yn).astype(o_ref.dtype)


def _merge(o_r, o_n, w_r, w_n, p_g, tm):
    m = o_r.shape[0]
    tn = 1024
    nb = D_MODEL // tn
    return pl.pallas_call(
        _merge_kernel,
        grid=(m // tm, nb),
        in_specs=[pl.BlockSpec((tm, C_R), lambda i, j: (i, 0)), pl.BlockSpec((tm, C_N), lambda i, j: (i, 0)),
                  pl.BlockSpec((C_R, tn), lambda i, j: (0, j)), pl.BlockSpec((C_N, tn), lambda i, j: (0, j)),
                  pl.BlockSpec((tm, tn), lambda i, j: (i, j)), pl.BlockSpec((tm, tn), lambda i, j: (i, j + nb))],
        out_specs=pl.BlockSpec((tm, tn), lambda i, j: (i, j)),
        out_shape=jax.ShapeDtypeStruct((m, D_MODEL), BF),
        compiler_params=_cparams("arbitrary", "arbitrary"),
        name="merge_branches",
    )(o_r, o_n, w_r, w_n, p_g, p_g)


def _proj_res_kernel(y_ref, w_ref, x_ref, g_ref, o_ref):
    o_ref[...] = x_ref[...] + g_ref[...] * jnp.dot(y_ref[...], w_ref[...], preferred_element_type=F32)


def _proj_residual(y, w, x, gate, tm, rpb):
    m = y.shape[0]
    tn = 1024
    r = gate.shape[1]
    rows, mods = _row_specs(m, tm, rpb)
    return pl.pallas_call(
        _proj_res_kernel,
        grid=(m // tm, D_MODEL // tn),
        in_specs=[pl.BlockSpec((tm, D_MODEL), lambda i, j: (i, 0)), pl.BlockSpec((D_MODEL, tn), lambda i, j: (0, j)),
                  rows(tn), mods(r, tn)],
        out_specs=rows(tn),
        out_shape=jax.ShapeDtypeStruct((m, D_MODEL), F32),
        compiler_params=_cparams("arbitrary", "arbitrary"),
        name="out_proj_residual",
    )(y, w, x, gate)


def _rwkv_features(p, p_prev, mu, w0, w2, a0, a2, g2, k_k, k_a):
    xm = p + (p_prev - p) * mu
    r = xm[:, :C_R]
    k = xm[:, C_R:2 * C_R]
    v = xm[:, 2 * C_R:3 * C_R]
    o = 3 * C_R
    wd = xm[:, o:o + LORA_W]
    ad = xm[:, o + LORA_W:o + LORA_W + LORA_A]
    gd = xm[:, o + LORA_W + LORA_A:]
    w_log = -_softplus(-(w0 + _dot(jnp.tanh(wd), w2))) - 0.5
    lw = -jnp.exp(w_log)
    a = _sigmoid(a0 + _dot(ad, a2))
    g = _dot(_sigmoid(gd), g2)
    kk = k * k_k
    k = k * (1.0 + (a - 1.0) * k_a)
    return r, k, v, lw, a, g, kk


def _rwkv_head_out(y, r_h, k_h, v_h, g_h, rk_h, lng_h, lnb_h):
    mean = jnp.mean(y, axis=-1, keepdims=True)
    yc = y - mean
    var = jnp.mean(yc * yc, axis=-1, keepdims=True)
    yn = yc * lax.rsqrt(var + LN_X_EPS) * lng_h + lnb_h
    bonus = jnp.sum(r_h * k_h * rk_h, axis=-1, keepdims=True) * v_h
    return (yn + bonus) * g_h


def _head_sums(x, ones2):
    nt = x.shape[1] // LANES
    xs = jnp.concatenate([x[:, j * LANES:(j + 1) * LANES] for j in range(nt)], axis=0)
    hi = xs.astype(BF)
    lo = (xs - hi.astype(F32)).astype(BF)
    s = jnp.dot(hi, ones2, preferred_element_type=F32) + jnp.dot(lo, ones2, preferred_element_type=F32)
    r = x.shape[0]
    return jnp.concatenate([s[j * r:(j + 1) * r] for j in range(nt)], axis=1)


def _rwkv_chunk_kernel(pr_ref, mu_ref, w0_ref, w2_ref, a0_ref, a2_ref, g2_ref, kk_ref, ka_ref, rk_ref,
                       lng_ref, lnb_ref, ones_ref, o_ref, shift_ref, state_ref, y_ref):
    c = pl.program_id(1)
    C = RW_CHUNK

    @pl.when(c == 0)
    def _():
        shift_ref[...] = jnp.zeros_like(shift_ref)
        state_ref[...] = jnp.zeros_like(state_ref)

    p = pr_ref[...]
    row = lax.broadcasted_iota(I32, (C, 1), 0)
    p_prev = jnp.where(row == 0, shift_ref[...], pltpu.roll(p, 1, axis=0))
    shift_ref[...] = p[C - 1:C, :]
    r, k, v, lw, a, g, kk_all = _rwkv_features(p, p_prev, mu_ref[...], w0_ref[...], w2_ref[...], a0_ref[...],
                                               a2_ref[...], g2_ref[...], kk_ref[...], ka_ref[...])
    cl = lw
    s = 1
    while s < C:
        cl = cl + jnp.where(row >= s, pltpu.roll(cl, s, axis=0), 0.0)
        s *= 2
    ti = lax.broadcasted_iota(I32, (C, C), 0)
    si = lax.broadcasted_iota(I32, (C, C), 1)
    strict = ti > si
    incl = ti >= si
    eye = (ti == si).astype(F32)
    heads = range(H_R)
    sls = [slice(h * HD_R, (h + 1) * HD_R) for h in heads]
    ones2 = ones_ref[...]
    kk_n = kk_all / jnp.maximum(jnp.sqrt(_head_sums(kk_all * kk_all, ones2)), 1e-12)
    b_all = kk_n * a
    cl_end = cl[C - 1:C, :]
    e_neg = jnp.exp(-cl)
    e_end = jnp.exp(cl_end - cl)
    g_end_all = jnp.exp(cl_end)
    per_head = lambda z: [z[:, sl] for sl in sls]
    kkt_all = (kk_n * jnp.exp(cl - lw)).astype(BF)
    rt_all = (r * jnp.exp(cl)).astype(BF)
    kr = per_head(jnp.concatenate([kkt_all, rt_all], axis=0))
    kh = per_head((k * e_neg).astype(BF))
    bh = per_head((b_all * e_neg).astype(BF))
    kbb = per_head(jnp.concatenate([(k * e_end).astype(BF), (b_all * e_end).astype(BF)], axis=0))
    vb = per_head(v.astype(BF))
    g_end = per_head(g_end_all)
    ak = [_dot_nt(kr[h], kh[h]) for h in heads]
    ab = [_dot_nt(kr[h], bh[h]) for h in heads]
    lg = [jnp.concatenate([jnp.where(strict, ak[h][:C], 0.0), jnp.where(incl, ak[h][C:], 0.0)], axis=0).astype(BF)
          for h in heads]
    nil = [jnp.where(strict, -ab[h][:C], 0.0) for h in heads]
    grb = [jnp.where(incl, ab[h][C:], 0.0).astype(BF) for h in heads]
    tinv = [eye + n for n in nil]
    m = 2
    while m < C:
        nil = [_dot(n, n) for n in nil]
        tinv = [t + _dot(t, n) for t, n in zip(tinv, nil)]
        m *= 2
    s0 = [state_ref[h] for h in heads]
    xy = [_dot_nt(kr[h], s0[h]) + _dot(lg[h], vb[h]) for h in heads]
    u = [_dot(tinv[h], xy[h][:C]).astype(BF) for h in heads]
    y = [xy[h][C:] - _dot(grb[h], u[h]) for h in heads]
    for h in heads:
        vu = jnp.concatenate([vb[h], -u[h]], axis=0)
        state_ref[h] = s0[h] * g_end[h] + _dot_tn(vu, kbb[h])
    for h, sl in enumerate(sls):
        y_ref[:, sl] = y[h]
    y_all = y_ref[...]
    yc = y_all - _head_sums(y_all, ones2) * (1.0 / HD_R)
    var = _head_sums(yc * yc, ones2) * (1.0 / HD_R)
    yn = yc * lax.rsqrt(var + LN_X_EPS) * lng_ref[...] + lnb_ref[...]
    bonus = _head_sums(r * k * rk_ref[...], ones2) * v
    o_ref[...] = ((yn + bonus) * g).astype(o_ref.dtype)


def _rwkv_weights(mu, w0, w2, a0, a2, g2, k_k, k_a, r_k, ln_g, ln_b):
    row = lambda z: z.reshape(1, -1).astype(F32)
    return (row(mu), row(w0), w2.astype(BF), row(a0), a2.astype(BF), g2.astype(BF), row(k_k), row(k_a), row(r_k),
            row(ln_g), row(ln_b))


_RWKV_W_SHAPES = ((1, C_RIN), (1, C_R), (LORA_W, C_R), (1, C_R), (LORA_A, C_R), (LORA_G, C_R), (1, C_R), (1, C_R),
                  (1, C_R), (1, C_R), (1, C_R))


def _rwkv_prompt(pr, rw):
    b, t, _ = pr.shape
    C = RW_CHUNK
    full = lambda shp: pl.BlockSpec(shp, lambda i, j: (0,) * len(shp))
    lane_head = np.arange(LANES) // HD_R
    ones2 = jnp.asarray(lane_head[:, None] == lane_head[None, :], BF)
    return pl.pallas_call(
        _rwkv_chunk_kernel,
        grid=(b, t // C),
        in_specs=[pl.BlockSpec((None, C, C_RIN), lambda i, j: (i, j, 0))] + [full(s) for s in _RWKV_W_SHAPES]
        + [full((LANES, LANES))],
        out_specs=[pl.BlockSpec((None, C, C_R), lambda i, j: (i, j, 0)),
                   pl.BlockSpec((None, 1, C_RIN), lambda i, j: (i, 0, 0)),
                   pl.BlockSpec((None, H_R, HD_R, HD_R), lambda i, j: (i, 0, 0, 0))],
        out_shape=[jax.ShapeDtypeStruct((b, t, C_R), BF),
                   jax.ShapeDtypeStruct((b, 1, C_RIN), F32),
                   jax.ShapeDtypeStruct((b, H_R, HD_R, HD_R), F32)],
        scratch_shapes=[pltpu.VMEM((C, C_R), F32)],
        compiler_params=_cparams("arbitrary", "arbitrary"),
        name="rwkv_chunk",
    )(pr, *rw, ones2)


def _rwkv_step_kernel(pr_ref, prev_ref, s0_ref, mu_ref, w0_ref, w2_ref, a0_ref, a2_ref, g2_ref, kk_ref, ka_ref,
                      rk_ref, lng_ref, lnb_ref, o_ref, state_ref):
    nb = pr_ref.shape[0]
    r, k, v, lw, a, g, kk_all = _rwkv_features(pr_ref[...], prev_ref[...], mu_ref[...], w0_ref[...], w2_ref[...],
                                               a0_ref[...], a2_ref[...], g2_ref[...], kk_ref[...], ka_ref[...])
    decay = jnp.exp(lw)
    ii = lax.broadcasted_iota(I32, (HD_R, HD_R), 0)
    jj = lax.broadcasted_iota(I32, (HD_R, HD_R), 1)
    eye = ii == jj
    col = lambda z: jnp.sum(jnp.where(eye, z, 0.0), axis=1, keepdims=True)
    for bi in range(nb):
        for h in range(H_R):
            sl = slice(h * HD_R, (h + 1) * HD_R)
            rows = lambda z: z[bi:bi + 1, sl]
            r_h, k_h, v_h, a_h, w_h = rows(r), rows(k), rows(v), rows(a), rows(decay)
            kk_h = rows(kk_all)
            kk_h = kk_h / jnp.maximum(jnp.sqrt(jnp.sum(kk_h * kk_h, axis=-1, keepdims=True)), 1e-12)
            b_h = kk_h * a_h
            s0 = s0_ref[bi, h]
            sa = jnp.sum(s0 * (-kk_h), axis=1, keepdims=True)
            s1 = s0 * w_h + sa * b_h + col(v_h) * k_h
            state_ref[bi, h] = s1
            y_col = jnp.sum(s1 * r_h, axis=1, keepdims=True)
            y = jnp.sum(jnp.where(eye, y_col, 0.0), axis=0, keepdims=True)
            o_ref[bi:bi + 1, sl] = _rwkv_head_out(y, r_h, k_h, v_h, rows(g), rk_ref[:, sl], lng_ref[:, sl],
                                                  lnb_ref[:, sl]).astype(o_ref.dtype)


def _rwkv_step(pr, prev, s0, rw):
    b = pr.shape[0]
    return pl.pallas_call(
        _rwkv_step_kernel,
        out_shape=[jax.ShapeDtypeStruct((b, C_R), BF), jax.ShapeDtypeStruct((b, H_R, HD_R, HD_R), F32)],
        compiler_params=pltpu.CompilerParams(vmem_limit_bytes=VMEM_LIMIT_BYTES),
        name="rwkv_step",
    )(pr, prev, s0, *rw)


def _cmp_partial_kernel(*refs):
    x_refs, w_ref, o_ref = refs[:-2], refs[-2], refs[-1]
    x = x_refs[0][...] if len(x_refs) == 1 else jnp.concatenate([r[...] for r in x_refs], axis=0)
    half = N_KVG * HD_N
    for s in range(2):
        acc = jnp.zeros((x.shape[0], N_KVG * 2 * CMP_HID), F32)
        for p in range(STRIDE):
            o = p * KV_COLS + s * half
            acc = acc + _dot(x[:, o:o + half], w_ref[p, s])
        o_ref[:, s * N_KVG * 2 * CMP_HID:(s + 1) * N_KVG * 2 * CMP_HID] = acc


CMP_PAIR = 2


def _cmp_partial_paged_kernel(pt_ref, *refs):
    del pt_ref
    x_refs, perm_ref, w_ref, o_ref = refs[:-3], refs[-3], refs[-2], refs[-1]
    cpp = PAGE_SIZE // STRIDE
    width = CMP_PAIR * 2 * CMP_HID
    perm = perm_ref[...]
    for s in range(2):
        for gp in range(N_KVG // CMP_PAIR):
            rows = []
            for x_ref in x_refs:
                tile = jnp.concatenate([x_ref[s, CMP_PAIR * gp + j] for j in range(CMP_PAIR)], axis=0)
                rows.append(_dot_nt(perm, tile))
            acc = jnp.zeros((len(x_refs) * cpp, width), F32)
            for p in range(STRIDE):
                lhs = jnp.concatenate([r[p * cpp:(p + 1) * cpp] for r in rows], axis=0)
                acc = acc + _dot(lhs, w_ref[p, s])
            o = (s * (N_KVG // CMP_PAIR) + gp) * width
            o_ref[:, o:o + width] = acc


def _cmp_weights(cmp_w1, groups):
    w1r = cmp_w1.reshape(2, 2, STRIDE, HD_N, CMP_HID)
    w = jnp.transpose(w1r, (2, 0, 3, 1, 4))
    w = w.reshape(STRIDE, 2, 1, HD_N, 1, 2 * CMP_HID)
    eye = jnp.eye(groups, dtype=w.dtype).reshape(1, 1, groups, 1, groups, 1)
    wbd = eye * w
    return wbd.reshape(STRIDE, 2, groups * HD_N, groups * 2 * CMP_HID).astype(BF)


def _cmp_partial_rows(x, wbd, tr):
    r = x.shape[0]
    n = 2 * N_KVG * 2 * CMP_HID
    return pl.pallas_call(
        _cmp_partial_kernel,
        grid=(r // tr,),
        in_specs=[pl.BlockSpec((tr, STRIDE * KV_COLS), lambda i: (i, 0)),
                  pl.BlockSpec(wbd.shape, lambda i: (0, 0, 0, 0))],
        out_specs=pl.BlockSpec((tr, n), lambda i: (i, 0)),
        out_shape=jax.ShapeDtypeStruct((r, n), F32),
        compiler_params=_cparams("arbitrary"),
        name="cmp_partial",
    )(x, wbd)


PAGES_PER_STEP = 16


def _cmp_partial_paged(cache_t, page_table, wpair):
    b, npg = page_table.shape
    cpp = PAGE_SIZE // STRIDE
    n = 2 * N_KVG * 2 * CMP_HID
    steps = npg // PAGES_PER_STEP
    perm = np.zeros((PAGE_SIZE, PAGE_SIZE), np.float32)
    tok = np.arange(PAGE_SIZE)
    perm[(tok % STRIDE) * cpp + tok // STRIDE, tok] = 1.0

    def page_spec(kpg):
        return pl.BlockSpec((None, 2, N_KVG, HD_N, PAGE_SIZE),
                            lambda i, j, pt: (pt[i, j * PAGES_PER_STEP + kpg], 0, 0, 0, 0))

    grid_spec = pltpu.PrefetchScalarGridSpec(
        num_scalar_prefetch=1,
        grid=(b, steps),
        in_specs=[page_spec(kpg) for kpg in range(PAGES_PER_STEP)]
        + [pl.BlockSpec((PAGE_SIZE, PAGE_SIZE), lambda i, j, pt: (0, 0)),
           pl.BlockSpec(wpair.shape, lambda i, j, pt: (0, 0, 0, 0))],
        out_specs=pl.BlockSpec((None, PAGES_PER_STEP * cpp, n), lambda i, j, pt: (i, j, 0)),
    )
    return pl.pallas_call(
        _cmp_partial_paged_kernel,
        grid_spec=grid_spec,
        out_shape=jax.ShapeDtypeStruct((b, npg * cpp, n), F32),
        compiler_params=_cparams("arbitrary", "arbitrary"),
        name="cmp_partial_paged",
    )(page_table, *([cache_t] * PAGES_PER_STEP), jnp.asarray(perm, BF), wpair)


def _cmp_finish_kernel(c_ref, cpos_ref, w2_ref, o_ref):
    c = c_ref[...]
    nrow = c.shape[0]
    c_next = pltpu.roll(c, nrow - 1, axis=0)
    for s in range(2):
        for gi in range(N_KVG):
            o = (s * N_KVG + gi) * 2 * CMP_HID
            hid = (c[:, o:o + CMP_HID] + cpos_ref[0:1, o:o + CMP_HID]
                   + c_next[:, o + CMP_HID:o + 2 * CMP_HID] + cpos_ref[1:2, o + CMP_HID:o + 2 * CMP_HID])
            oo = (s * N_KVG + gi) * HD_N
            o_ref[:, oo:oo + HD_N] = _dot(_gelu_tanh(hid), w2_ref[s])


def _cmp_finish(c, cpos, w2):
    b, nch, n = c.shape
    return pl.pallas_call(
        _cmp_finish_kernel,
        grid=(b,),
        in_specs=[pl.BlockSpec((None, nch, n), lambda i: (i, 0, 0)), pl.BlockSpec(cpos.shape, lambda i: (0, 0)),
                  pl.BlockSpec(w2.shape, lambda i: (0, 0, 0))],
        out_specs=pl.BlockSpec((None, nch, KV_COLS), lambda i: (i, 0, 0)),
        out_shape=jax.ShapeDtypeStruct((b, nch, KV_COLS), F32),
        compiler_params=_cparams("arbitrary"),
        name="cmp_finish",
    )(c, cpos, w2.astype(BF))


def _cmp_pos_rows(cmp_pos):
    pos = cmp_pos.reshape(2, STRIDE, 1, 1, HD_N)
    rows = jnp.broadcast_to(pos, (2, STRIDE, 2, N_KVG, HD_N)).reshape(2, STRIDE * KV_COLS)
    return jnp.concatenate([rows, jnp.zeros((6, STRIDE * KV_COLS), F32)], axis=0)


def _bias_table_kernel(dist_ref, rb_ref, o_ref):
    bucket = _t5_bucket(dist_ref[...])
    for h in range(H_N):
        out = jnp.zeros(bucket.shape, F32)
        for b in range(N_BUCKETS):
            out = jnp.where(bucket == b, rb_ref[b, h], out)
        o_ref[h] = out


def _bias_table(dist, rel_bias, tr):
    r, n = dist.shape
    return pl.pallas_call(
        _bias_table_kernel,
        grid=(r // tr,),
        in_specs=[pl.BlockSpec((tr, n), lambda i: (i, 0)),
                  pl.BlockSpec(memory_space=pltpu.SMEM)],
        out_specs=pl.BlockSpec((H_N, tr, n), lambda i: (0, i, 0)),
        out_shape=jax.ShapeDtypeStruct((H_N, r, n), F32),
        compiler_params=_cparams("arbitrary"),
        name="bias_table",
    )(dist, rel_bias)


def _softmax_rows(logits, valid):
    lm = jnp.where(valid, logits, NEG_INF)
    e = jnp.exp(lm - jnp.max(lm, axis=-1, keepdims=True))
    return e / jnp.sum(e, axis=-1, keepdims=True)


def _cmp_attn_kernel(q_ref, kv_ref, bias_ref, ovt_ref, o_ref, sel_ref):
    tq = q_ref.shape[0]
    ncp = kv_ref.shape[0]
    nsb = ovt_ref.shape[0]
    q0 = pl.program_id(1) * tq
    qpos = q0 + lax.broadcasted_iota(I32, (tq, 1), 0)
    cend = lax.broadcasted_iota(I32, (1, ncp), 1) * STRIDE + (L_CMP - 1)
    valid = (qpos >= cend) & (lax.broadcasted_iota(I32, (1, ncp), 1) < ncp - 1)
    validf = valid.astype(F32)
    q = q_ref[...] * (HD_N ** -0.5)
    blk = lax.broadcasted_iota(I32, (nsb, tq), 0)
    cur = (q0 + lax.broadcasted_iota(I32, (1, tq), 1)) // L_SEL
    forced = (blk == 0) | (blk == cur) | (blk == cur - 1)
    future = blk > cur
    for gi in range(N_KVG):
        kc = kv_ref[:, gi * HD_N:(gi + 1) * HD_N]
        vc = kv_ref[:, (N_KVG + gi) * HD_N:(N_KVG + gi + 1) * HD_N]
        pcs = jnp.zeros((tq, ncp), F32)
        for hl in range(HPG):
            h = gi * HPG + hl
            sl = slice(h * HD_N, (h + 1) * HD_N)
            pc = _softmax_rows(_dot_nt(q[:, sl], kc) + bias_ref[h], valid) * validf
            pcs = pcs + pc
            o_ref[:, sl] = _dot(pc, vc)
        imp = _dot_nt(ovt_ref[...], pcs)
        score = jnp.where(forced, FORCE, jnp.where(future, -FORCE, imp))
        rank = jnp.zeros((nsb, tq), F32)
        for i in range(nsb):
            si = score[i:i + 1, :]
            rank = rank + ((si > score) | ((si == score) & (i < blk))).astype(F32)
        sel_ref[gi] = (rank < N_TOP).astype(sel_ref.dtype)


def _cmp_sel_overlap_t(nc, ncp, nsb):
    s = np.arange(nc)[None, :] * STRIDE
    j = np.arange(nsb)[:, None] * L_SEL
    ov = np.clip(np.minimum(s + L_CMP, j + L_SEL) - np.maximum(s, j), 0, None) / L_CMP
    return np.pad(ov, ((0, 0), (0, ncp - nc))).astype(np.float32)


def _cmp_attn_prompt(p_n, kv_cmp, bias_c, t):
    b = kv_cmp.shape[0]
    ncp = kv_cmp.shape[1]
    nsb = t // L_SEL
    tq = ATT_TILE
    nqt = t // tq
    ovt = jnp.asarray(_cmp_sel_overlap_t(ncp - 1, ncp, nsb), BF)
    return pl.pallas_call(
        _cmp_attn_kernel,
        grid=(b, nqt),
        in_specs=[pl.BlockSpec((tq, C_N), lambda i, j: (i * nqt + j, 0)),
                  pl.BlockSpec((None, ncp, KV_COLS), lambda i, j: (i, 0, 0)),
                  pl.BlockSpec((H_N, tq, ncp), lambda i, j: (0, j, 0)),
                  pl.BlockSpec((nsb, ncp), lambda i, j: (0, 0))],
        out_specs=[pl.BlockSpec((tq, C_N), lambda i, j: (i * nqt + j, 0)),
                   pl.BlockSpec((None, N_KVG, nsb, tq), lambda i, j: (i, 0, 0, j))],
        out_shape=[jax.ShapeDtypeStruct((b * t, C_N), F32), jax.ShapeDtypeStruct((b, N_KVG, nsb, t), BF)],
        compiler_params=_cparams("arbitrary", "arbitrary"),
        name="nsa_cmp_select",
    )(p_n, kv_cmp, bias_c, ovt)


ATT_TK = 128
ATT_R = ATT_TILE // ATT_TK
N_SEL_OFFS = ATT_R + (MAX_DIST + ATT_TK - 1) // ATT_TK + 1
N_WIN_OFFS = ATT_R + WINDOW // ATT_TK
QA_COLS = HD_N + 32
SEL_STEP_TILES = 4
WIN_STEP_TILES = 2


def _swa_kernel(qa_ref, ks_ref, vs_ref, kw_ref, vw_ref, tabs_ref, tabw_ref, gn_ref, oc_ref, o_ref):
    tq = qa_ref.shape[1]
    qt = pl.program_id(2)
    top = ATT_R * qt + ATT_R - 1

    heads = range(HPG)

    def attend(k_ref, v_ref, tab_ref, lo, n_off, step_tiles):
        tiles = range(step_tiles)

        def body(kp, carry):
            ki = [lo + step_tiles * kp + j for j in tiles]
            kt = [k_ref[jnp.minimum(i, top)] for i in ki]
            vt = [v_ref[jnp.minimum(i, top)] for i in ki]
            off = [jnp.where(i > top, n_off, jnp.minimum(top - i, n_off - 1)) for i in ki]
            s = [[lax.dot_general(kt[j], qa_ref[hl], (((1,), (1,)), ((), ())), preferred_element_type=F32)
                  + tab_ref[off[j], hl * ATT_TK:(hl + 1) * ATT_TK, :] for j in tiles] for hl in heads]
            m_new = [functools.reduce(jnp.maximum, [carry[hl][0]] + [jnp.max(s[hl][j], axis=0, keepdims=True)
                                                                    for j in tiles]) for hl in heads]
            alpha = [jnp.exp(carry[hl][0] - m_new[hl]) for hl in heads]
            p = [[jnp.exp(s[hl][j] - m_new[hl]) for j in tiles] for hl in heads]
            l = [alpha[hl] * carry[hl][1] + sum(jnp.sum(p[hl][j], axis=0, keepdims=True) for j in tiles)
                 for hl in heads]
            acc = [alpha[hl] * carry[hl][2] + sum(jnp.dot(vt[j], p[hl][j].astype(BF), preferred_element_type=F32)
                                                  for j in tiles) for hl in heads]
            return tuple((m_new[hl], l[hl], acc[hl]) for hl in heads)

        init = tuple((jnp.full((1, tq), NEG_INF, F32), jnp.zeros((1, tq), F32), jnp.zeros((HD_N, tq), F32))
                     for _ in heads)
        res = lax.fori_loop(0, (top - lo) // step_tiles + 1, body, init)
        return [acc / l for _, l, acc in res]

    o_sel = attend(ks_ref, vs_ref, tabs_ref, 0, N_SEL_OFFS, SEL_STEP_TILES)
    o_win = attend(kw_ref, vw_ref, tabw_ref, jnp.maximum(top + 1 - N_WIN_OFFS, 0), N_WIN_OFFS, WIN_STEP_TILES)
    gates = _sigmoid(gn_ref[...])
    gates_t = gates.T
    for hl in heads:
        sl = slice(hl * HD_N, (hl + 1) * HD_N)
        o_t = gates_t[3 * hl + 1:3 * hl + 2, :] * o_sel[hl] + gates_t[3 * hl + 2:3 * hl + 3, :] * o_win[hl]
        o_ref[:, sl] = (gates[:, 3 * hl:3 * hl + 1] * oc_ref[:, sl] + o_t.T).astype(o_ref.dtype)


def _swa_prompt(p_n, qa, ks, vs, kw, vw, tab_s, tab_w, o_cmp, t):
    b = qa.shape[0]
    tq = ATT_TILE
    nqt = t // tq
    nkt = t // ATT_TK
    gw = HPG * HD_N
    k_spec = pl.BlockSpec((None, None, nkt, ATT_TK, QA_COLS), lambda i, g, j: (i, g, 0, 0, 0))
    v_spec = pl.BlockSpec((None, None, nkt, HD_N, ATT_TK), lambda i, g, j: (i, g, 0, 0, 0))
    tab_spec = lambda n: pl.BlockSpec((n + 1, HPG * ATT_TK, tq), lambda i, g, j: (0, g, 0))
    return pl.pallas_call(
        _swa_kernel,
        grid=(b, N_KVG, nqt),
        in_specs=[pl.BlockSpec((None, HPG, tq, QA_COLS), lambda i, g, j: (i, g, j, 0)),
                  k_spec, v_spec, k_spec, v_spec, tab_spec(N_SEL_OFFS), tab_spec(N_WIN_OFFS),
                  pl.BlockSpec((tq, GN_GROUP_COLS), lambda i, g, j: (i * nqt + j, NP_GN // GN_GROUP_COLS + g)),
                  pl.BlockSpec((tq, gw), lambda i, g, j: (i * nqt + j, g))],
        out_specs=pl.BlockSpec((tq, gw), lambda i, g, j: (i * nqt + j, g)),
        out_shape=jax.ShapeDtypeStruct((b * t, C_N), BF),
        compiler_params=_cparams("arbitrary", "arbitrary", "arbitrary"),
        name="nsa_sel_win",
    )(qa, ks, vs, kw, vw, tab_s, tab_w, p_n, o_cmp)


def _swa_operands(p_n, sel, b, t):
    nsb = t // L_SEL
    nkt = t // ATT_TK
    q = (p_n[:, :C_N] * (HD_N ** -0.5)).reshape(b, t, H_N, HD_N).transpose(0, 2, 1, 3)
    pen = jnp.where(jnp.transpose(sel, (0, 1, 3, 2)) > 0.5, 0.0, NEG_INF).astype(F32)
    pen = jnp.broadcast_to(pen[:, :, None], (b, N_KVG, HPG, t, nsb)).reshape(b, H_N, t, nsb)
    qa = jnp.concatenate([q, pen], axis=-1).astype(BF)

    def split(cols):
        kv = p_n[:, cols:cols + KV_COLS].reshape(b, t, 2, N_KVG, HD_N)
        k = jnp.transpose(kv[:, :, 0], (0, 2, 1, 3))
        v = jnp.transpose(kv[:, :, 1], (0, 2, 3, 1)).reshape(b, N_KVG, HD_N, nkt, ATT_TK)
        return k, jnp.transpose(v, (0, 1, 3, 2, 4)).astype(BF)

    onehot = jnp.asarray(np.arange(t)[:, None] // L_SEL == np.arange(nsb)[None, :], F32)
    k_s, v_s = split(NP_KVS)
    k_w, v_w = split(NP_KVW)
    ext = lambda k, e: jnp.concatenate([k, jnp.broadcast_to(e, (b, N_KVG, t, nsb))], axis=-1).astype(BF) \
        .reshape(b, N_KVG, nkt, ATT_TK, QA_COLS)
    return qa, ext(k_s, onehot), v_s, ext(k_w, jnp.zeros((t, nsb), F32)), v_w


def _sample_cmp_kernel(q_ref, kv_ref, rbt_ref, ov_ref, o_ref, idx_ref, *, past):
    ncp = kv_ref.shape[0]
    nsbp = ov_ref.shape[1]
    nsb = -(-(past + 1) // L_SEL)
    q = q_ref[...] * (HD_N ** -0.5)
    hrow = lax.broadcasted_iota(I32, (H_N, 1), 0)
    nidx = lax.broadcasted_iota(I32, (1, ncp), 1)
    valid = nidx < ncp - 1
    bias = _bias_rows(past - (nidx * STRIDE + (L_CMP - 1)), rbt_ref[...])
    logits = jnp.zeros((H_N, ncp), F32)
    for gi in range(N_KVG):
        lg = _dot_nt(q, kv_ref[:, gi * HD_N:(gi + 1) * HD_N])
        logits = jnp.where(hrow // HPG == gi, lg, logits)
    pc = _softmax_rows(logits + bias, valid) * valid.astype(F32)
    o = jnp.zeros((H_N, HD_N), F32)
    for gi in range(N_KVG):
        og = _dot(pc, kv_ref[:, (N_KVG + gi) * HD_N:(N_KVG + gi + 1) * HD_N])
        o = jnp.where(hrow // HPG == gi, og, o)
    o_ref[...] = o
    imp_h = _dot(pc, ov_ref[...])
    blk = lax.broadcasted_iota(I32, (8, nsbp), 1)
    grow = lax.broadcasted_iota(I32, (8, 1), 0)
    cur = past // L_SEL
    score = jnp.full((8, nsbp), -3e38, F32)
    for gi in range(N_KVG):
        imp = jnp.sum(jnp.where(hrow // HPG == gi, imp_h, 0.0), axis=0, keepdims=True)
        score = jnp.where(grow == gi, imp, score)
    forced = (blk == 0) | (blk == cur) | (blk == cur - 1)
    score = jnp.where(forced, FORCE, jnp.where(blk > cur, -FORCE, score))
    score = jnp.where((blk < nsb) & (grow < N_KVG), score, -3e38)
    lane = lax.broadcasted_iota(I32, (8, LANES), 1)
    picks = jnp.zeros((8, LANES), I32)
    for it in range(N_TOP):
        mx = jnp.max(score, axis=-1, keepdims=True)
        pick = jnp.min(jnp.where(score == mx, blk, nsbp), axis=-1, keepdims=True)
        picks = jnp.where(lane == it, pick, picks)
        score = jnp.where(blk == pick, -3e38, score)
    idx_ref[...] = picks


def _sample_cmp(q, kv_cmp, rel_bias, past):
    b, ncp, _ = kv_cmp.shape
    nc = ncp - 1
    nsb = -(-(past + 1) // L_SEL)
    nsbp = -(-nsb // LANES) * LANES
    s = np.arange(nc)[:, None] * STRIDE
    j = np.arange(nsb)[None, :] * L_SEL
    ov = np.clip(np.minimum(s + L_CMP, j + L_SEL) - np.maximum(s, j), 0, None) / L_CMP
    ov = np.pad(ov, ((0, ncp - nc), (0, nsbp - nsb))).astype(np.float32)
    return pl.pallas_call(
        functools.partial(_sample_cmp_kernel, past=past),
        grid=(b,),
        in_specs=[pl.BlockSpec((None, H_N, HD_N), lambda i: (i, 0, 0)),
                  pl.BlockSpec((None, ncp, KV_COLS), lambda i: (i, 0, 0)),
                  pl.BlockSpec((H_N, N_BUCKETS), lambda i: (0, 0)),
                  pl.BlockSpec((ncp, nsbp), lambda i: (0, 0))],
        out_specs=[pl.BlockSpec((None, H_N, HD_N), lambda i: (i, 0, 0)),
                   pl.BlockSpec((None, 8, LANES), lambda i: (i, 0, 0))],
        out_shape=[jax.ShapeDtypeStruct((b, H_N, HD_N), F32), jax.ShapeDtypeStruct((b, 8, LANES), I32)],
        compiler_params=_cparams("arbitrary"),
        name="nsa_sample_cmp_select",
    )(q, kv_cmp, rel_bias.T, jnp.asarray(ov, BF))


def _block_copy_kernel(pg_ref, *refs):
    del pg_ref
    x_refs, o_ref = refs[:-1], refs[-1]
    for n, x_ref in enumerate(x_refs):
        o_ref[n] = x_ref[...]


def _gather_sel_pages(cache_t, page):
    rows, nslot = page.shape

    def slot_spec(n):
        return pl.BlockSpec((None, 2, None, HD_N, PAGE_SIZE), lambda i, pg: (pg[i, n], 0, i % N_KVG, 0, 0))

    return pl.pallas_call(
        _block_copy_kernel,
        grid_spec=pltpu.PrefetchScalarGridSpec(
            num_scalar_prefetch=1, grid=(rows,),
            in_specs=[slot_spec(n) for n in range(nslot)],
            out_specs=pl.BlockSpec((None, nslot, 2, HD_N, PAGE_SIZE), lambda i, pg: (i, 0, 0, 0, 0))),
        out_shape=jax.ShapeDtypeStruct((rows, nslot, 2, HD_N, PAGE_SIZE), cache_t.dtype),
        compiler_params=_cparams("arbitrary"),
        name="gather_sel_pages",
    )(page, *([cache_t] * nslot))


def _sample_swa_kernel(idx_ref, q_ref, blk_ref, win_ref, new_ref, rbt_ref, gate_ref, oc_ref, o_ref, *, past):
    bi = pl.program_id(0)
    q = q_ref[...] * (HD_N ** -0.5)
    rbt = rbt_ref[...]
    hrow = lax.broadcasted_iota(I32, (H_N, 1), 0)
    nk = N_TOP * PAGE_SIZE
    lane = lax.broadcasted_iota(I32, (1, nk), 1)
    new_blk = past // L_SEL
    bias_new = rbt[:, 0:1]
    gates = _sigmoid(gate_ref[...])
    nwin = win_ref.shape[0]
    wdist = nwin - lax.broadcasted_iota(I32, (1, nwin), 1)
    wbias = _bias_rows(wdist, rbt)
    wvalid = (wdist >= 0) & (wdist <= WINDOW)

    def with_new(logits, valid, weigh, k_new, v_new):
        l_new = jnp.sum(q * k_new, axis=-1, keepdims=True) + bias_new
        lm = jnp.where(valid, logits, NEG_INF)
        m = jnp.maximum(jnp.max(lm, axis=-1, keepdims=True), l_new)
        e = jnp.where(valid, jnp.exp(lm - m), 0.0)
        e_new = jnp.exp(l_new - m)
        den = jnp.sum(e, axis=-1, keepdims=True) + e_new
        return (weigh(e) + e_new * v_new) / den

    o_sel = jnp.zeros((H_N, HD_N), F32)
    o_win = jnp.zeros((H_N, HD_N), F32)
    bpp = PAGE_SIZE // L_SEL
    tok = lane % PAGE_SIZE
    for gi in range(N_KVG):
        ksl = slice(gi * HD_N, (gi + 1) * HD_N)
        vsl = slice((N_KVG + gi) * HD_N, (N_KVG + gi + 1) * HD_N)
        bid = jnp.zeros((1, nk), I32)
        for n in range(N_TOP):
            bid = jnp.where(lane // PAGE_SIZE == n, idx_ref[bi, gi, n], bid)
        dist = past - ((bid // bpp) * PAGE_SIZE + tok)
        valid = (bid != new_blk) & (tok // L_SEL == bid % bpp) & (dist >= 0)
        kt = jnp.concatenate([blk_ref[gi, n, 0] for n in range(N_TOP)], axis=1)
        vt = jnp.concatenate([blk_ref[gi, n, 1] for n in range(N_TOP)], axis=1)
        logits = _dot(q, kt) + _bias_rows(dist, rbt)
        og = with_new(logits, valid, lambda e, vt=vt: _dot_nt(e, vt), new_ref[0:1, ksl], new_ref[0:1, vsl])
        o_sel = jnp.where(hrow // HPG == gi, og, o_sel)
        logits = _dot_nt(q, win_ref[:, ksl]) + wbias
        og = with_new(logits, wvalid, lambda e, vsl=vsl: _dot(e, win_ref[:, vsl]), new_ref[1:2, ksl],
                      new_ref[1:2, vsl])
        o_win = jnp.where(hrow // HPG == gi, og, o_win)
    o_ref[...] = gates[:, 0:1] * oc_ref[...] + gates[:, 1:2] * o_sel + gates[:, 2:3] * o_win


def _sample_swa(idx, q, blocks, win, new_kv, rel_bias, gates, o_cmp, past):
    b = q.shape[0]
    w = win.shape[1]
    grid_spec = pltpu.PrefetchScalarGridSpec(
        num_scalar_prefetch=1, grid=(b,),
        in_specs=[pl.BlockSpec((None, H_N, HD_N), lambda i, ix: (i, 0, 0)),
                  pl.BlockSpec((None, N_KVG, N_TOP, 2, HD_N, PAGE_SIZE), lambda i, ix: (i, 0, 0, 0, 0, 0)),
                  pl.BlockSpec((None, w, KV_COLS), lambda i, ix: (i, 0, 0)),
                  pl.BlockSpec((None, 2, KV_COLS), lambda i, ix: (i, 0, 0)),
                  pl.BlockSpec((H_N, N_BUCKETS), lambda i, ix: (0, 0)),
                  pl.BlockSpec((None, H_N, 3), lambda i, ix: (i, 0, 0)),
                  pl.BlockSpec((None, H_N, HD_N), lambda i, ix: (i, 0, 0))],
        out_specs=pl.BlockSpec((None, H_N, HD_N), lambda i, ix: (i, 0, 0)))
    return pl.pallas_call(
        functools.partial(_sample_swa_kernel, past=past),
        grid_spec=grid_spec,
        out_shape=jax.ShapeDtypeStruct((b, H_N, HD_N), F32),
        compiler_params=_cparams("arbitrary"),
        name="nsa_sample_sel_win",
    )(idx, q, blocks, win, new_kv, rel_bias.T, gates, o_cmp)


ROUTER_COLS = LANES


def _router_kernel(x_ref, g_ref, sh_ref, sc_ref, w_ref, b_ref, h_ref, e_ref, wt_ref, rk_ref, cnt_ref):
    i = pl.program_id(0)
    tm = x_ref.shape[0]

    @pl.when(i == 0)
    def _():
        cnt_ref[...] = jnp.zeros_like(cnt_ref)

    h = (_rms(x_ref[...], g_ref[...]) * (1.0 + sc_ref[...]) + sh_ref[...]).astype(BF)
    _store_folded(h_ref, h.astype(F32))
    logits = jnp.dot(h, w_ref[...], preferred_element_type=F32) + b_ref[...]
    lane = lax.broadcasted_iota(I32, (tm, ROUTER_COLS), 1)

    def top1(vals, ok):
        vm = jnp.where(ok, vals, -3e38)
        mx = jnp.max(vm, axis=-1, keepdims=True)
        return mx, jnp.min(jnp.where(ok & (vm == mx), lane, ROUTER_COLS), axis=-1, keepdims=True)

    isg = lane < N_EGROUPS
    pg = _softmax_rows(logits, isg)
    g_w, g_i = top1(pg, isg)
    ise = (lane >= N_EGROUPS) & ((lane - N_EGROUPS) // EXP_PER_GROUP == g_i)
    pe = _softmax_rows(logits, ise)
    w0, l0 = top1(pe, ise)
    w1, l1 = top1(pe, ise & (lane != l0))
    den = w0 + w1
    e0 = l0 - N_EGROUPS
    e1 = l1 - N_EGROUPS
    e_ref[...] = jnp.where(lane == 0, e0, jnp.where(lane == 1, e1, 0))
    wt_ref[...] = jnp.where(lane == 0, w0 / den * g_w, jnp.where(lane == 1, w1 / den * g_w, 0.0))
    oh0 = (lane == e0).astype(F32)
    oh1 = (lane == e1).astype(F32)
    cnt = oh0 + oh1
    ti = lax.broadcasted_iota(I32, (tm, tm), 0)
    si = lax.broadcasted_iota(I32, (tm, tm), 1)
    before = _dot((ti > si).astype(F32), cnt) + cnt_ref[...]
    r0 = jnp.sum(before * oh0, axis=-1, keepdims=True)
    r1 = jnp.sum(before * oh1, axis=-1, keepdims=True)
    rk_ref[...] = jnp.where(lane == 0, r0, jnp.where(lane == 1, r1, 0.0)).astype(I32)
    cnt_ref[...] = cnt_ref[...] + jnp.sum(cnt, axis=0, keepdims=True)


def _router(x, g, shift, scale, w_r, b_r, tm, rpb):
    m = x.shape[0]
    r = shift.shape[1]
    rows = lambda tn: pl.BlockSpec((tm, tn), lambda i: (i, 0))
    mods = pl.BlockSpec((None, r, D_MODEL), lambda i: ((i * tm) // rpb, 0, 0))
    small = lambda dt: jax.ShapeDtypeStruct((m, ROUTER_COLS), dt)
    return pl.pallas_call(
        _router_kernel,
        grid=(m // tm,),
        in_specs=[rows(D_MODEL), pl.BlockSpec((1, D_MODEL), lambda i: (0, 0)), mods, mods,
                  pl.BlockSpec((D_MODEL, ROUTER_COLS), lambda i: (0, 0)),
                  pl.BlockSpec((1, ROUTER_COLS), lambda i: (0, 0))],
        out_specs=[pl.BlockSpec((tm * ROW_FOLD, LANES), lambda i: (i, 0)),
                   rows(ROUTER_COLS), rows(ROUTER_COLS), rows(ROUTER_COLS),
                   pl.BlockSpec((1, ROUTER_COLS), lambda i: (0, 0))],
        out_shape=[jax.ShapeDtypeStruct((m * ROW_FOLD, LANES), F32), small(I32), small(F32), small(I32),
                   jax.ShapeDtypeStruct((1, ROUTER_COLS), F32)],
        compiler_params=_cparams("arbitrary"),
        name="moe_router",
    )(x, g.reshape(1, D_MODEL), shift, scale, w_r, b_r)


ROW_FOLD = D_MODEL // LANES


def _store_folded(ref, x):
    n = x.shape[0]
    for c in range(ROW_FOLD):
        ref[pl.ds(c, n, stride=ROW_FOLD), :] = x[:, c * LANES:(c + 1) * LANES]


def _load_folded(ref, first_row, n):
    return jnp.concatenate([ref[pl.ds(first_row * ROW_FOLD + c, n, stride=ROW_FOLD), :] for c in range(ROW_FOLD)],
                           axis=1)


def _row_gather_ring(src_hbm, buf, sems, groups, idx_now, idx_next, inline_next=False):
    i = pl.program_id(0)
    last = pl.num_programs(0) - 1
    slot = i % 2
    total = sum(cnt for _, cnt, _ in groups)
    assert 2 * total * ROW_FOLD == buf.shape[0]

    def fold(row):
        return pl.ds(pl.multiple_of(row * ROW_FOLD, ROW_FOLD), ROW_FOLD)

    def start(idx, s, first, k, r, priority):
        pltpu.make_async_copy(src_hbm.at[fold(idx(k, r))], buf.at[fold(s * total + first + r)],
                              sems.at[s]).start(priority=priority)

    def start_all(idx, s):
        for first, cnt, k in groups:
            def issue(j, c):
                for u in range(2):
                    start(idx, s, first, k, 2 * j + u, u)
                return c
            lax.fori_loop(0, cnt // 2, issue, 0)

    def wait_slot(s):
        whole = buf.at[pl.ds(pl.multiple_of(s * total * ROW_FOLD, ROW_FOLD), total * ROW_FOLD)]
        pltpu.make_async_copy(whole, whole, sems.at[s]).wait()

    @pl.when(i == 0)
    def _():
        start_all(idx_now, slot)

    def finish():
        if inline_next:
            @pl.when(i == last)
            def _():
                wait_slot(1 - slot)

    if inline_next:
        wait_slot(slot)
        for first, cnt, k in groups:
            for r in range(cnt):
                start(idx_next, 1 - slot, first, k, r, r % 2)
    else:
        @pl.when(i < last)
        def _():
            start_all(idx_next, 1 - slot)

        wait_slot(slot)
    return slot * total, finish


def _expert_kernel(be_ref, rt_ref, rtn_ref, h_hbm, w1_ref, w3_ref, w2_ref, o_ref, xbuf, sems, w1b, w3b, w2b):
    i = pl.program_id(0)
    blk = rt_ref.shape[1]
    @pl.when((i == 0) | (be_ref[i] != be_ref[jnp.maximum(i - 1, 0)]))
    def _():
        w1b[...] = w1_ref[...].astype(BF)
        w3b[...] = w3_ref[...].astype(BF)
        w2b[...] = w2_ref[...].astype(BF)

    base, finish = _row_gather_ring(h_hbm, xbuf, sems, ((0, blk, 0),), lambda k, r: rt_ref[0, r],
                                    lambda k, r: rtn_ref[0, r], inline_next=True)
    x = _load_folded(xbuf, base, blk).astype(BF)
    a = jnp.dot(x, w1b[...], preferred_element_type=F32)
    b = jnp.dot(x, w3b[...], preferred_element_type=F32)
    hid = a * _sigmoid(a) * b
    _store_folded(o_ref, jnp.dot(hid.astype(BF), w2b[...], preferred_element_type=F32))
    finish()


def _experts(h2, row_tok, blk_exp, w1, w3, w2):
    nblk, _, blk = row_tok.shape
    idx_spec = lambda d: pl.BlockSpec((None, 1, blk), lambda i, be: (jnp.minimum(i + d, nblk - 1), 0, 0),
                                      memory_space=pltpu.SMEM)
    grid_spec = pltpu.PrefetchScalarGridSpec(
        num_scalar_prefetch=1, grid=(nblk,),
        in_specs=[idx_spec(0), idx_spec(1),
                  pl.BlockSpec(memory_space=pl.ANY),
                  pl.BlockSpec((None, D_MODEL, D_EXP), lambda i, be: (be[i], 0, 0)),
                  pl.BlockSpec((None, D_MODEL, D_EXP), lambda i, be: (be[i], 0, 0)),
                  pl.BlockSpec((None, D_EXP, D_MODEL), lambda i, be: (be[i], 0, 0))],
        out_specs=pl.BlockSpec((blk * ROW_FOLD, LANES), lambda i, be: (i, 0)),
        scratch_shapes=[pltpu.VMEM((2 * blk * ROW_FOLD, LANES), F32), pltpu.SemaphoreType.DMA((2,)),
                        pltpu.VMEM((D_MODEL, D_EXP), BF), pltpu.VMEM((D_MODEL, D_EXP), BF),
                        pltpu.VMEM((D_EXP, D_MODEL), BF)])
    return pl.pallas_call(
        _expert_kernel,
        grid_spec=grid_spec,
        out_shape=jax.ShapeDtypeStruct((nblk * blk * ROW_FOLD, LANES), F32),
        compiler_params=_cparams("arbitrary"),
        name="moe_experts",
    )(blk_exp, row_tok, row_tok, h2, w1, w3, w2)


def _final_kernel(x_ref, g_ref, dest_ref, destn_ref, ys_hbm, wt_ref, nf_ref, o_ref, ybuf, sems):
    tm = x_ref.shape[0]
    groups = tuple((k * tm, tm, k) for k in range(TOP_K))
    base, _ = _row_gather_ring(ys_hbm, ybuf, sems, groups, lambda k, r: dest_ref[k, r], lambda k, r: destn_ref[k, r])
    wt = wt_ref[...]
    moe = wt[:, 0:1] * _load_folded(ybuf, base, tm) + wt[:, 1:2] * _load_folded(ybuf, base + tm, tm)
    o_ref[...] = _rms(x_ref[...] + g_ref[...] * moe, nf_ref[...])


def _final(x, gate, ys, dest, wts, norm_f, tm, rpb):
    m = x.shape[0]
    r = gate.shape[1]
    nt = m // tm
    rows = lambda tn: pl.BlockSpec((tm, tn), lambda i: (i, 0))
    idx_spec = lambda d: pl.BlockSpec((None, TOP_K, tm), lambda i: (jnp.minimum(i + d, nt - 1), 0, 0),
                                      memory_space=pltpu.SMEM)
    return pl.pallas_call(
        _final_kernel,
        grid=(nt,),
        in_specs=[rows(D_MODEL), pl.BlockSpec((None, r, D_MODEL), lambda i: ((i * tm) // rpb, 0, 0)),
                  idx_spec(0), idx_spec(1),
                  pl.BlockSpec(memory_space=pl.ANY),
                  rows(ROUTER_COLS), pl.BlockSpec((1, D_MODEL), lambda i: (0, 0))],
        out_specs=rows(D_MODEL),
        out_shape=jax.ShapeDtypeStruct((m, D_MODEL), F32),
        scratch_shapes=[pltpu.VMEM((2 * TOP_K * tm * ROW_FOLD, LANES), F32), pltpu.SemaphoreType.DMA((2,))],
        compiler_params=_cparams("arbitrary"),
        name="moe_combine_final_norm",
    )(x, gate, dest, dest, ys, wts, norm_f.reshape(1, D_MODEL))


def _moe_and_final(x1, g2, shift, scale, gate, w_r, b_r, exp_w1, exp_w3, exp_w2, norm_f, tm, rpb, blk):
    m = x1.shape[0]
    h2, eid, wts, rank, counts = _router(x1, g2, shift, scale, w_r, b_r, tm, rpb)
    counts = counts[0, :N_EXP].astype(I32)
    padded = (counts + blk - 1) // blk * blk
    pend = jnp.cumsum(padded)
    pstart = pend - padded
    n_blocks = -(-(m * TOP_K) // blk) + N_EXP
    starts = jnp.arange(n_blocks, dtype=I32)[:, None] * blk
    blk_exp = jnp.minimum(jnp.sum((pend[None, :] <= starts).astype(I32), axis=1), N_EXP - 1)
    e = eid[:, :TOP_K]
    dest = pstart[e] + rank[:, :TOP_K]
    tok = jnp.broadcast_to(jnp.arange(m, dtype=I32)[:, None], (m, TOP_K))
    row_tok = jnp.zeros((n_blocks * blk,), I32).at[dest.reshape(-1)].set(tok.reshape(-1))
    ys = _experts(h2, row_tok.reshape(n_blocks, 1, blk), blk_exp, exp_w1, exp_w3, exp_w2)
    dest_t = jnp.transpose(dest.reshape(m // tm, tm, TOP_K), (0, 2, 1))
    return _final(x1, gate, ys, dest_t, wts, norm_f, tm, rpb)


def _pack_in_proj(w_in):
    o = C_RIN
    w_r = w_in[:, :o]
    w_q = w_in[:, o:o + C_N + 3 * KV_COLS]
    o += C_N + 3 * KV_COLS
    w_gn = w_in[:, o:o + 3 * H_N].reshape(D_MODEL, N_KVG, 3 * HPG)
    w_gn = jnp.pad(w_gn, ((0, 0), (0, 0), (0, GN_GROUP_COLS - 3 * HPG))).reshape(D_MODEL, N_KVG * GN_GROUP_COLS)
    o += 3 * H_N
    w_gm = w_in[:, o:]
    return w_r.astype(BF), jnp.concatenate([w_q, w_gn], axis=1).astype(BF), w_gm.astype(BF)


def _prompt_bias_tables(rel_bias, t):
    tq, tk = ATT_TILE, ATT_TK
    i = np.arange(tq)[None, :]
    j = np.arange(tk)[:, None]
    dist = np.stack([tk * (o - (ATT_R - 1)) + i - j for o in range(N_WIN_OFFS)]).astype(np.int32)
    raw = _bias_table(jnp.asarray(dist.reshape(N_WIN_OFFS * tk, tq)), rel_bias, tk)
    raw = jnp.transpose(raw.reshape(H_N, N_WIN_OFFS, tk, tq), (1, 0, 2, 3))
    ok_w = jnp.asarray((dist >= 0) & (dist <= WINDOW))[:, None]
    ok_s = jnp.asarray(dist[:N_SEL_OFFS] >= 0)[:, None]
    masked = jnp.full((1, H_N * tk, tq), NEG_INF, F32)
    tab_w = jnp.concatenate([jnp.where(ok_w, raw, NEG_INF).reshape(N_WIN_OFFS, H_N * tk, tq), masked])
    tab_s = jnp.concatenate([jnp.where(ok_s, raw[:N_SEL_OFFS], NEG_INF).reshape(N_SEL_OFFS, H_N * tk, tq), masked])
    nc = (t - L_CMP) // STRIDE + 1
    ncp = nc + 1
    dc = (np.arange(t)[:, None] - (np.arange(ncp)[None, :] * STRIDE + L_CMP - 1)).astype(np.int32)
    return tab_s, tab_w, _bias_table(jnp.asarray(dc), rel_bias, tq)


def kernel(x_prompt, x_sample, c_prompt, c_sample, cache_cmp_kv, cache_sel_kv, state_win_kv, state_rwkv_shift,
           state_rwkv_wkv, page_table, rel_bias, norm_f, norm1, norm2, w_ada, b_ada, w_in, rwkv_mu, rwkv_w0, rwkv_w2,
           rwkv_a0, rwkv_a2, rwkv_g2, rwkv_kk, rwkv_ka, rwkv_rk, rwkv_ln_g, rwkv_ln_b, cmp_pos, cmp_w1, cmp_w2,
           w_o_rwkv, w_o_nsa, w_out, router_wg, router_bg, router_we, router_be, exp_w1, exp_w3, exp_w2):
    bp, t, _ = x_prompt.shape
    bs = x_sample.shape[0]
    mp = bp * t
    past = page_table.shape[1] * PAGE_SIZE

    nrow = -(-(bp + bs) // 8) * 8
    c_all = jnp.concatenate([c_prompt, c_sample, jnp.zeros((nrow - bp - bs, D_MODEL), F32)], axis=0)
    mod = _ada(c_all, w_ada[0], b_ada[0]).reshape(nrow, 6, D_MODEL)
    mod_p = [mod[:bp, i][:, None, :] for i in range(6)]
    mod_s = [mod[bp:bp + bs, i][None] for i in range(6)]

    w_r, w_n, w_gm = _pack_in_proj(w_in[0])
    rw = _rwkv_weights(rwkv_mu[0], rwkv_w0[0], rwkv_w2[0], rwkv_a0[0], rwkv_a2[0], rwkv_g2[0], rwkv_kk[0],
                       rwkv_ka[0], rwkv_rk[0], rwkv_ln_g[0], rwkv_ln_b[0])
    wbd = _cmp_weights(cmp_w1[0], N_KVG)
    cpos = _cmp_partial_rows(_cmp_pos_rows(cmp_pos[0]), wbd, 8)
    wo_r, wo_n, wo = w_o_rwkv[0].astype(BF), w_o_nsa[0].astype(BF), w_out[0].astype(BF)
    w_router = jnp.pad(jnp.concatenate([router_wg[0], router_we[0]], axis=1),
                       ((0, 0), (0, ROUTER_COLS - N_EGROUPS - N_EXP))).astype(BF)
    b_router = jnp.pad(jnp.concatenate([router_bg[0], router_be[0]]), (0, ROUTER_COLS - N_EGROUPS - N_EXP))[None]

    tm = 512
    xp = x_prompt.reshape(mp, D_MODEL)
    h = _norm_mod(xp, norm1[0], mod_p[0], mod_p[1], tm, t)
    p_r = _matmul(h, w_r, tm, C_RIN // 2)
    p_n = _matmul(h, w_n, tm, NP_COLS // 2)
    p_g = _matmul(h, w_gm, tm, 2048)
    o_r, shift_p, wkv_p = _rwkv_prompt(p_r.reshape(bp, t, C_RIN), rw)
    kvc = p_n[:, NP_KVC:NP_KVC + KV_COLS]
    kvs = p_n[:, NP_KVS:NP_KVS + KV_COLS]
    kvw = p_n[:, NP_KVW:NP_KVW + KV_COLS]
    nch = t // STRIDE
    c_part = _cmp_partial_rows(kvc.reshape(bp * nch, STRIDE * KV_COLS), wbd, nch)
    kv_cmp = _cmp_finish(c_part.reshape(bp, nch, -1), cpos, cmp_w2[0])
    tab_s, tab_w, bias_c = _prompt_bias_tables(rel_bias, t)
    o_cmp, sel = _cmp_attn_prompt(p_n, kv_cmp, bias_c, t)
    o_n = _swa_prompt(p_n, *_swa_operands(p_n, sel, bp, t), tab_s, tab_w, o_cmp, t)
    y = _merge(o_r.reshape(mp, C_R), o_n, wo_r, wo_n, p_g, tm)
    x1 = _proj_residual(y, wo, xp, mod_p[2], tm, t)
    y_prompt = _moe_and_final(x1, norm2[0], mod_p[3], mod_p[4], mod_p[5], w_router, b_router, exp_w1[0], exp_w3[0],
                              exp_w2[0], norm_f, tm, t, 128).reshape(bp, t, D_MODEL)
    kv_shape = (1, bp, t, 2, N_KVG, HD_N)
    wlen = min(WINDOW, t)
    win_p = kvw.reshape(bp, t, KV_COLS)[:, t - wlen:].reshape(1, bp, wlen, 2, N_KVG, HD_N)

    xs = x_sample.reshape(bs, D_MODEL)
    hs = _norm_mod(xs, norm1[0], mod_s[0], mod_s[1], bs, bs)
    ps_r = _matmul(hs, w_r, bs, C_RIN // 2)
    ps_n = _matmul(hs, w_n, bs, NP_COLS // 2)
    ps_g = _matmul(hs, w_gm, bs, 2048)
    os_r, wkv_s = _rwkv_step(ps_r, state_rwkv_shift[0], state_rwkv_wkv[0], rw)
    kvc_s = ps_n[:, NP_KVC:NP_KVC + KV_COLS]
    kvs_s = ps_n[:, NP_KVS:NP_KVS + KV_COLS]
    kvw_s = ps_n[:, NP_KVW:NP_KVW + KV_COLS]
    cs_part = _cmp_partial_paged(jnp.transpose(cache_cmp_kv[0], (0, 2, 3, 4, 1)), page_table,
                                 _cmp_weights(cmp_w1[0], CMP_PAIR))
    kv_cmp_s = _cmp_finish(cs_part, cpos, cmp_w2[0])
    q_s = ps_n[:, :C_N].reshape(bs, H_N, HD_N)
    o_cmp_s, picks = _sample_cmp(q_s, kv_cmp_s, rel_bias, past)
    idx = picks[:, :N_KVG, :N_TOP]
    bpp = PAGE_SIZE // L_SEL
    npb = past // L_SEL
    idc = jnp.minimum(idx, npb - 1)
    page = jnp.take_along_axis(page_table, (idc // bpp).reshape(bs, -1), axis=1).reshape(bs * N_KVG, N_TOP)
    cache_t = jnp.transpose(cache_sel_kv[0], (0, 2, 3, 4, 1))
    blocks = _gather_sel_pages(cache_t, page).reshape(bs, N_KVG, N_TOP, 2, HD_N, PAGE_SIZE)
    win_buf = state_win_kv[0].reshape(bs, -1, KV_COLS)
    gates_s = ps_n[:, NP_GN:].reshape(bs, N_KVG, GN_GROUP_COLS)[:, :, :3 * HPG].reshape(bs, H_N, 3)
    new_kv = jnp.stack([kvs_s, kvw_s], axis=1)
    os_n = _sample_swa(idx, q_s, blocks, win_buf, new_kv, rel_bias, gates_s, o_cmp_s, past)
    ys = _merge(os_r, os_n.reshape(bs, C_N).astype(BF), wo_r, wo_n, ps_g, bs)
    xs1 = _proj_residual(ys, wo, xs, mod_s[2], bs, bs)
    y_sample = _moe_and_final(xs1, norm2[0], mod_s[3], mod_s[4], mod_s[5], w_router, b_router, exp_w1[0], exp_w3[0],
                              exp_w2[0], norm_f, bs, bs, 16).reshape(bs, 1, D_MODEL)
    kv1 = (1, bs, 1, 2, N_KVG, HD_N)
    wbuf = win_buf.shape[1]
    win_s = jnp.concatenate([win_buf, kvw_s[:, None, :]], axis=1)[:, -wbuf:].reshape(1, bs, wbuf, 2, N_KVG, HD_N)

    return (y_prompt, y_sample,
            kvc.reshape(kv_shape), kvc_s.reshape(kv1),
            kvs.reshape(kv_shape), kvs_s.reshape(kv1),
            win_p, win_s,
            shift_p.reshape(1, bp, C_RIN), ps_r.reshape(1, bs, C_RIN),
            wkv_p[None], wkv_s[None])
```

```python
import functools
import math

import numpy as np
import jax
import jax.numpy as jnp
from jax import lax
from jax.experimental import pallas as pl
from jax.experimental.pallas import tpu as pltpu

D_MODEL = 2048
PAGE_SIZE = 128
H_R, HD_R = 16, 64
C_R = H_R * HD_R
LORA_W, LORA_A, LORA_G = 64, 64, 128
C_RIN = 3 * C_R + LORA_W + LORA_A + LORA_G
LN_X_EPS = 64e-5
H_N, HD_N, N_KVG = 16, 64, 4
HPG = H_N // N_KVG
C_N = H_N * HD_N
KV_COLS = 2 * N_KVG * HD_N
L_CMP, STRIDE, CMP_HID = 32, 16, 64
L_SEL, N_TOP, WINDOW = 64, 16, 512
N_BUCKETS, MAX_DIST = 32, 128
N_EGROUPS, EXP_PER_GROUP = 4, 8
N_EXP = N_EGROUPS * EXP_PER_GROUP
TOP_K, D_EXP = 2, 512
RMS_EPS = 1e-6
NEG_INF = -1e30
FORCE = 1e9

BF = jnp.bfloat16
F32 = jnp.float32
I32 = jnp.int32

VMEM_LIMIT_BYTES = 56 * 1024 * 1024
LANES = 128
RW_CHUNK = 32
ATT_TILE = 256
GN_GROUP_COLS = 128
NP_Q, NP_KVC, NP_KVS, NP_KVW, NP_GN = 0, C_N, C_N + KV_COLS, C_N + 2 * KV_COLS, C_N + 3 * KV_COLS
NP_COLS = NP_GN + N_KVG * GN_GROUP_COLS


def _cparams(*sem):
    return pltpu.CompilerParams(dimension_semantics=sem, vmem_limit_bytes=VMEM_LIMIT_BYTES)


def _dot(a, b):
    return jnp.dot(a.astype(BF), b.astype(BF), preferred_element_type=F32)


def _dot_nt(a, b):
    return lax.dot_general(a.astype(BF), b.astype(BF), (((1,), (1,)), ((), ())), preferred_element_type=F32)


def _dot_tn(a, b):
    return lax.dot_general(a.astype(BF), b.astype(BF), (((0,), (0,)), ((), ())), preferred_element_type=F32)


def _softplus(x):
    return jnp.maximum(x, 0.0) + jnp.log1p(jnp.exp(-jnp.abs(x)))


def _sigmoid(x):
    return 1.0 / (1.0 + jnp.exp(-x))


def _gelu_tanh(x):
    return 0.5 * x * (1.0 + jnp.tanh(math.sqrt(2.0 / math.pi) * (x + 0.044715 * x * x * x)))


def _t5_bucket(dist):
    n = jnp.maximum(dist, 0)
    max_exact = N_BUCKETS // 2
    nf = jnp.maximum(n, 1).astype(F32)
    large = max_exact + (jnp.log(nf / max_exact) / math.log(MAX_DIST / max_exact)
                         * (N_BUCKETS - max_exact)).astype(I32)
    large = jnp.minimum(large, N_BUCKETS - 1)
    return jnp.where(n < max_exact, n, large)


def _bias_rows(dist, rbt):
    bucket = _t5_bucket(dist)
    out = jnp.zeros((rbt.shape[0], dist.shape[1]), F32)
    for b in range(N_BUCKETS):
        out = jnp.where(bucket == b, rbt[:, b:b + 1], out)
    return out


def _ada_kernel(c_ref, w_ref, b_ref, o_ref):
    o_ref[...] = _dot(c_ref[...], w_ref[...]) + b_ref[...]


def _ada(c, w_ada, b_ada):
    r = c.shape[0]
    n = w_ada.shape[1]
    tn = 1024
    return pl.pallas_call(
        _ada_kernel,
        grid=(n // tn,),
        in_specs=[pl.BlockSpec((r, D_MODEL), lambda j: (0, 0)),
                  pl.BlockSpec((D_MODEL, tn), lambda j: (0, j)),
                  pl.BlockSpec((1, tn), lambda j: (0, j))],
        out_specs=pl.BlockSpec((r, tn), lambda j: (0, j)),
        out_shape=jax.ShapeDtypeStruct((r, n), F32),
        compiler_params=_cparams("arbitrary"),
        name="ada_mod",
    )(c, w_ada, b_ada.reshape(1, n))


def _rms(x, g):
    return x * lax.rsqrt(jnp.mean(x * x, axis=-1, keepdims=True) + RMS_EPS) * g


def _norm_mod_kernel(x_ref, g_ref, sh_ref, sc_ref, o_ref):
    o_ref[...] = (_rms(x_ref[...], g_ref[...]) * (1.0 + sc_ref[...]) + sh_ref[...]).astype(o_ref.dtype)


def _row_specs(m, tm, rpb):
    del m
    return (lambda tn: pl.BlockSpec((tm, tn), lambda i, j: (i, j)),
            lambda r, tn: pl.BlockSpec((None, r, tn), lambda i, j: ((i * tm) // rpb, 0, j)))


def _norm_mod(x, g, shift, scale, tm, rpb):
    m = x.shape[0]
    r = shift.shape[1]
    rows, mods = _row_specs(m, tm, rpb)
    return pl.pallas_call(
        _norm_mod_kernel,
        grid=(m // tm, 1),
        in_specs=[rows(D_MODEL), pl.BlockSpec((1, D_MODEL), lambda i, j: (0, 0)), mods(r, D_MODEL), mods(r, D_MODEL)],
        out_specs=rows(D_MODEL),
        out_shape=jax.ShapeDtypeStruct((m, D_MODEL), BF),
        compiler_params=_cparams("arbitrary", "arbitrary"),
        name="norm_mod",
    )(x, g.reshape(1, D_MODEL), shift, scale)


def _mm_kernel(a_ref, w_ref, o_ref):
    o_ref[...] = jnp.dot(a_ref[...], w_ref[...], preferred_element_type=F32).astype(o_ref.dtype)


def _matmul(a, w, tm, tn, out_dtype=F32):
    m, k = a.shape
    n = w.shape[1]
    return pl.pallas_call(
        _mm_kernel,
        grid=(m // tm, n // tn),
        in_specs=[pl.BlockSpec((tm, k), lambda i, j: (i, 0)), pl.BlockSpec((k, tn), lambda i, j: (0, j))],
        out_specs=pl.BlockSpec((tm, tn), lambda i, j: (i, j)),
        out_shape=jax.ShapeDtypeStruct((m, n), out_dtype),
        compiler_params=_cparams("arbitrary", "arbitrary"),
        name="matmul",
    )(a, w)


def _merge_kernel(or_ref, on_ref, wr_ref, wn_ref, g0_ref, g1_ref, o_ref):
    yr = jnp.dot(or_ref[...], wr_ref[...], preferred_element_type=F32)
    yn = jnp.dot(on_ref[...], wn_ref[...], preferred_element_type=F32)
    o_ref[...] = (_sigmoid(g0_ref[...]) * yr + _sigmoid(g1_ref[...]) * yn).astype(o_ref.dtype)


def _merge(o_r, o_n, w_r, w_n, p_g, tm):
    m = o_r.shape[0]
    tn = 1024
    nb = D_MODEL // tn
    return pl.pallas_call(
        _merge_kernel,
        grid=(m // tm, nb),
        in_specs=[pl.BlockSpec((tm, C_R), lambda i, j: (i, 0)), pl.BlockSpec((tm, C_N), lambda i, j: (i, 0)),
                  pl.BlockSpec((C_R, tn), lambda i, j: (0, j)), pl.BlockSpec((C_N, tn), lambda i, j: (0, j)),
                  pl.BlockSpec((tm, tn), lambda i, j: (i, j)), pl.BlockSpec((tm, tn), lambda i, j: (i, j + nb))],
        out_specs=pl.BlockSpec((tm, tn), lambda i, j: (i, j)),
        out_shape=jax.ShapeDtypeStruct((m, D_MODEL), BF),
        compiler_params=_cparams("arbitrary", "arbitrary"),
        name="merge_branches",
    )(o_r, o_n, w_r, w_n, p_g, p_g)


def _proj_res_kernel(y_ref, w_ref, x_ref, g_ref, o_ref):
    o_ref[...] = x_ref[...] + g_ref[...] * jnp.dot(y_ref[...], w_ref[...], preferred_element_type=F32)


def _proj_residual(y, w, x, gate, tm, rpb):
    m = y.shape[0]
    tn = 1024
    r = gate.shape[1]
    rows, mods = _row_specs(m, tm, rpb)
    return pl.pallas_call(
        _proj_res_kernel,
        grid=(m // tm, D_MODEL // tn),
        in_specs=[pl.BlockSpec((tm, D_MODEL), lambda i, j: (i, 0)), pl.BlockSpec((D_MODEL, tn), lambda i, j: (0, j)),
                  rows(tn), mods(r, tn)],
        out_specs=rows(tn),
        out_shape=jax.ShapeDtypeStruct((m, D_MODEL), F32),
        compiler_params=_cparams("arbitrary", "arbitrary"),
        name="out_proj_residual",
    )(y, w, x, gate)


def _rwkv_features(p, p_prev, mu, w0, w2, a0, a2, g2, k_k, k_a):
    xm = p + (p_prev - p) * mu
    r = xm[:, :C_R]
    k = xm[:, C_R:2 * C_R]
    v = xm[:, 2 * C_R:3 * C_R]
    o = 3 * C_R
    wd = xm[:, o:o + LORA_W]
    ad = xm[:, o + LORA_W:o + LORA_W + LORA_A]
    gd = xm[:, o + LORA_W + LORA_A:]
    w_log = -_softplus(-(w0 + _dot(jnp.tanh(wd), w2))) - 0.5
    lw = -jnp.exp(w_log)
    a = _sigmoid(a0 + _dot(ad, a2))
    g = _dot(_sigmoid(gd), g2)
    kk = k * k_k
    k = k * (1.0 + (a - 1.0) * k_a)
    return r, k, v, lw, a, g, kk


def _head_sums(x, ones2):
    nt = x.shape[1] // LANES
    xs = jnp.concatenate([x[:, j * LANES:(j + 1) * LANES] for j in range(nt)], axis=0)
    hi = xs.astype(BF)
    lo = (xs - hi.astype(F32)).astype(BF)
    s = jnp.dot(hi, ones2, preferred_element_type=F32) + jnp.dot(lo, ones2, preferred_element_type=F32)
    r = x.shape[0]
    return jnp.concatenate([s[j * r:(j + 1) * r] for j in range(nt)], axis=1)


def _rwkv_chunk_kernel(pr_ref, mu_ref, w0_ref, w2_ref, a0_ref, a2_ref, g2_ref, kk_ref, ka_ref, rk_ref,
                       lng_ref, lnb_ref, ones_ref, o_ref, shift_ref, state_ref, y_ref):
    c = pl.program_id(1)
    C = RW_CHUNK

    @pl.when(c == 0)
    def _():
        shift_ref[...] = jnp.zeros_like(shift_ref)
        state_ref[...] = jnp.zeros_like(state_ref)

    p = pr_ref[...]
    row = lax.broadcasted_iota(I32, (C, 1), 0)
    p_prev = jnp.where(row == 0, shift_ref[...], pltpu.roll(p, 1, axis=0))
    shift_ref[...] = p[C - 1:C, :]
    r, k, v, lw, a, g, kk_all = _rwkv_features(p, p_prev, mu_ref[...], w0_ref[...], w2_ref[...], a0_ref[...],
                                               a2_ref[...], g2_ref[...], kk_ref[...], ka_ref[...])
    cl = lw
    s = 1
    while s < C:
        cl = cl + jnp.where(row >= s, pltpu.roll(cl, s, axis=0), 0.0)
        s *= 2
    ti = lax.broadcasted_iota(I32, (C, C), 0)
    si = lax.broadcasted_iota(I32, (C, C), 1)
    strict = ti > si
    incl = ti >= si
    eye = (ti == si).astype(F32)
    heads = range(H_R)
    sls = [slice(h * HD_R, (h + 1) * HD_R) for h in heads]
    ones2 = ones_ref[...]
    kk_n = kk_all / jnp.maximum(jnp.sqrt(_head_sums(kk_all * kk_all, ones2)), 1e-12)
    b_all = kk_n * a
    cl_end = cl[C - 1:C, :]
    e_neg = jnp.exp(-cl)
    e_end = jnp.exp(cl_end - cl)
    g_end_all = jnp.exp(cl_end)
    per_head = lambda z: [z[:, sl] for sl in sls]
    kkt_all = (kk_n * jnp.exp(cl - lw)).astype(BF)
    rt_all = (r * jnp.exp(cl)).astype(BF)
    kr = per_head(jnp.concatenate([kkt_all, rt_all], axis=0))
    kh = per_head((k * e_neg).astype(BF))
    bh = per_head((b_all * e_neg).astype(BF))
    kbb = per_head(jnp.concatenate([(k * e_end).astype(BF), (b_all * e_end).astype(BF)], axis=0))
    vb = per_head(v.astype(BF))
    g_end = per_head(g_end_all)
    ak = [_dot_nt(kr[h], kh[h]) for h in heads]
    ab = [_dot_nt(kr[h], bh[h]) for h in heads]
    lg = [jnp.concatenate([jnp.where(strict, ak[h][:C], 0.0), jnp.where(incl, ak[h][C:], 0.0)], axis=0).astype(BF)
          for h in heads]
    nil = [jnp.where(strict, -ab[h][:C], 0.0) for h in heads]
    grb = [jnp.where(incl, ab[h][C:], 0.0).astype(BF) for h in heads]
    tinv = [eye + n for n in nil]
    m = 2
    while m < C:
        nil = [_dot(n, n) for n in nil]
        tinv = [t + _dot(t, n) for t, n in zip(tinv, nil)]
        m *= 2
    s0 = [state_ref[h] for h in heads]
    xy = [_dot_nt(kr[h], s0[h]) + _dot(lg[h], vb[h]) for h in heads]
    u = [_dot(tinv[h], xy[h][:C]).astype(BF) for h in heads]
    y = [xy[h][C:] - _dot(grb[h], u[h]) for h in heads]
    for h in heads:
        vu = jnp.concatenate([vb[h], -u[h]], axis=0)
        state_ref[h] = s0[h] * g_end[h] + _dot_tn(vu, kbb[h])
    for h, sl in enumerate(sls):
        y_ref[:, sl] = y[h]
    y_all = y_ref[...]
    yc = y_all - _head_sums(y_all, ones2) * (1.0 / HD_R)
    var = _head_sums(yc * yc, ones2) * (1.0 / HD_R)
    yn = yc * lax.rsqrt(var + LN_X_EPS) * lng_ref[...] + lnb_ref[...]
    bonus = _head_sums(r * k * rk_ref[...], ones2) * v
    o_ref[...] = ((yn + bonus) * g).astype(o_ref.dtype)


def _rwkv_weights(mu, w0, w2, a0, a2, g2, k_k, k_a, r_k, ln_g, ln_b):
    row = lambda z: z.reshape(1, -1).astype(F32)
    return (row(mu), row(w0), w2.astype(BF), row(a0), a2.astype(BF), g2.astype(BF), row(k_k), row(k_a), row(r_k),
            row(ln_g), row(ln_b))


_RWKV_W_SHAPES = ((1, C_RIN), (1, C_R), (LORA_W, C_R), (1, C_R), (LORA_A, C_R), (LORA_G, C_R), (1, C_R), (1, C_R),
                  (1, C_R), (1, C_R), (1, C_R))


def _rwkv_prompt(pr, rw):
    b, t, _ = pr.shape
    C = RW_CHUNK
    full = lambda shp: pl.BlockSpec(shp, lambda i, j: (0,) * len(shp))
    lane_head = np.arange(LANES) // HD_R
    ones2 = jnp.asarray(lane_head[:, None] == lane_head[None, :], BF)
    return pl.pallas_call(
        _rwkv_chunk_kernel,
        grid=(b, t // C),
        in_specs=[pl.BlockSpec((None, C, C_RIN), lambda i, j: (i, j, 0))] + [full(s) for s in _RWKV_W_SHAPES]
        + [full((LANES, LANES))],
        out_specs=[pl.BlockSpec((None, C, C_R), lambda i, j: (i, j, 0)),
                   pl.BlockSpec((None, 1, C_RIN), lambda i, j: (i, 0, 0)),
                   pl.BlockSpec((None, H_R, HD_R, HD_R), lambda i, j: (i, 0, 0, 0))],
        out_shape=[jax.ShapeDtypeStruct((b, t, C_R), BF),
                   jax.ShapeDtypeStruct((b, 1, C_RIN), F32),
                   jax.ShapeDtypeStruct((b, H_R, HD_R, HD_R), F32)],
        scratch_shapes=[pltpu.VMEM((C, C_R), F32)],
        compiler_params=_cparams("arbitrary", "arbitrary"),
        name="rwkv_chunk",
    )(pr, *rw, ones2)


def _rwkv_step_kernel(pr_ref, prev_ref, s0_ref, mu_ref, w0_ref, w2_ref, a0_ref, a2_ref, g2_ref, kk_ref, ka_ref,
                      rk_ref, lng_ref, lnb_ref, o_ref, state_ref):
    nb = pr_ref.shape[0]
    r, k, v, lw, a, g, kk_all = _rwkv_features(pr_ref[...], prev_ref[...], mu_ref[...], w0_ref[...], w2_ref[...],
                                               a0_ref[...], a2_ref[...], g2_ref[...], kk_ref[...], ka_ref[...])
    decay = jnp.exp(lw)
    ii = lax.broadcasted_iota(I32, (HD_R, HD_R), 0)
    jj = lax.broadcasted_iota(I32, (HD_R, HD_R), 1)
    eye = ii == jj
    col = lambda z: jnp.sum(jnp.where(eye, z, 0.0), axis=1, keepdims=True)
    heads = range(H_R)
    sls = [slice(h * HD_R, (h + 1) * HD_R) for h in heads]
    for bi in range(nb):
        rows = lambda z: [z[bi:bi + 1, sl] for sl in sls]
        r_h, k_h, v_h, a_h, w_h, g_h, kk_h = rows(r), rows(k), rows(v), rows(a), rows(decay), rows(g), rows(kk_all)
        nrm = [jnp.maximum(jnp.sqrt(jnp.sum(z * z, axis=-1, keepdims=True)), 1e-12) for z in kk_h]
        kk_h = [kk_h[h] / nrm[h] for h in heads]
        s0 = [s0_ref[bi, h] for h in heads]
        sa = [jnp.sum(s0[h] * (-kk_h[h]), axis=1, keepdims=True) for h in heads]
        v_col = [col(z) for z in v_h]
        s1 = [s0[h] * w_h[h] + sa[h] * (kk_h[h] * a_h[h]) + v_col[h] * k_h[h] for h in heads]
        for h in heads:
            state_ref[bi, h] = s1[h]
        y_col = [jnp.sum(s1[h] * r_h[h], axis=1, keepdims=True) for h in heads]
        y = [jnp.sum(jnp.where(eye, z, 0.0), axis=0, keepdims=True) for z in y_col]
        yc = [z - jnp.mean(z, axis=-1, keepdims=True) for z in y]
        var = [jnp.mean(z * z, axis=-1, keepdims=True) for z in yc]
        bonus = [jnp.sum(r_h[h] * k_h[h] * rk_ref[:, sls[h]], axis=-1, keepdims=True) * v_h[h] for h in heads]
        for h, sl in enumerate(sls):
            yn = yc[h] * lax.rsqrt(var[h] + LN_X_EPS) * lng_ref[:, sl] + lnb_ref[:, sl]
            o_ref[bi:bi + 1, sl] = ((yn + bonus[h]) * g_h[h]).astype(o_ref.dtype)


def _rwkv_step(pr, prev, s0, rw):
    b = pr.shape[0]
    return pl.pallas_call(
        _rwkv_step_kernel,
        out_shape=[jax.ShapeDtypeStruct((b, C_R), BF), jax.ShapeDtypeStruct((b, H_R, HD_R, HD_R), F32)],
        compiler_params=pltpu.CompilerParams(vmem_limit_bytes=VMEM_LIMIT_BYTES),
        name="rwkv_step",
    )(pr, prev, s0, *rw)


def _cmp_partial_kernel(*refs):
    x_refs, w_ref, o_ref = refs[:-2], refs[-2], refs[-1]
    x = x_refs[0][...] if len(x_refs) == 1 else jnp.concatenate([r[...] for r in x_refs], axis=0)
    half = N_KVG * HD_N
    for s in range(2):
        acc = jnp.zeros((x.shape[0], N_KVG * 2 * CMP_HID), F32)
        for p in range(STRIDE):
            o = p * KV_COLS + s * half
            acc = acc + _dot(x[:, o:o + half], w_ref[p, s])
        o_ref[:, s * N_KVG * 2 * CMP_HID:(s + 1) * N_KVG * 2 * CMP_HID] = acc


CMP_PAIR = 2


def _cmp_partial_paged_kernel(pt_ref, *refs):
    del pt_ref
    x_refs, perm_ref, w_ref, o_ref = refs[:-3], refs[-3], refs[-2], refs[-1]
    cpp = PAGE_SIZE // STRIDE
    width = CMP_PAIR * 2 * CMP_HID
    perm = perm_ref[...]
    for s in range(2):
        for gp in range(N_KVG // CMP_PAIR):
            rows = []
            for x_ref in x_refs:
                tile = jnp.concatenate([x_ref[s, CMP_PAIR * gp + j] for j in range(CMP_PAIR)], axis=0)
                rows.append(_dot_nt(perm, tile))
            acc = jnp.zeros((len(x_refs) * cpp, width), F32)
            for p in range(STRIDE):
                lhs = jnp.concatenate([r[p * cpp:(p + 1) * cpp] for r in rows], axis=0)
                acc = acc + _dot(lhs, w_ref[p, s])
            o = (s * (N_KVG // CMP_PAIR) + gp) * width
            o_ref[:, o:o + width] = acc


def _cmp_weights(cmp_w1, groups):
    w1r = cmp_w1.reshape(2, 2, STRIDE, HD_N, CMP_HID)
    w = jnp.transpose(w1r, (2, 0, 3, 1, 4))
    w = w.reshape(STRIDE, 2, 1, HD_N, 1, 2 * CMP_HID)
    eye = jnp.eye(groups, dtype=w.dtype).reshape(1, 1, groups, 1, groups, 1)
    wbd = eye * w
    return wbd.reshape(STRIDE, 2, groups * HD_N, groups * 2 * CMP_HID).astype(BF)


def _cmp_partial_rows(x, wbd, tr):
    r = x.shape[0]
    n = 2 * N_KVG * 2 * CMP_HID
    return pl.pallas_call(
        _cmp_partial_kernel,
        grid=(r // tr,),
        in_specs=[pl.BlockSpec((tr, STRIDE * KV_COLS), lambda i: (i, 0)),
                  pl.BlockSpec(wbd.shape, lambda i: (0, 0, 0, 0))],
        out_specs=pl.BlockSpec((tr, n), lambda i: (i, 0)),
        out_shape=jax.ShapeDtypeStruct((r, n), F32),
        compiler_params=_cparams("arbitrary"),
        name="cmp_partial",
    )(x, wbd)


PAGES_PER_STEP = 16


def _cmp_partial_paged(cache_t, page_table, wpair):
    b, npg = page_table.shape
    cpp = PAGE_SIZE // STRIDE
    n = 2 * N_KVG * 2 * CMP_HID
    steps = npg // PAGES_PER_STEP
    perm = np.zeros((PAGE_SIZE, PAGE_SIZE), np.float32)
    tok = np.arange(PAGE_SIZE)
    perm[(tok % STRIDE) * cpp + tok // STRIDE, tok] = 1.0

    def page_spec(kpg):
        return pl.BlockSpec((None, 2, N_KVG, HD_N, PAGE_SIZE),
                            lambda i, j, pt: (pt[i, j * PAGES_PER_STEP + kpg], 0, 0, 0, 0))

    grid_spec = pltpu.PrefetchScalarGridSpec(
        num_scalar_prefetch=1,
        grid=(b, steps),
        in_specs=[page_spec(kpg) for kpg in range(PAGES_PER_STEP)]
        + [pl.BlockSpec((PAGE_SIZE, PAGE_SIZE), lambda i, j, pt: (0, 0)),
           pl.BlockSpec(wpair.shape, lambda i, j, pt: (0, 0, 0, 0))],
        out_specs=pl.BlockSpec((None, PAGES_PER_STEP * cpp, n), lambda i, j, pt: (i, j, 0)),
    )
    return pl.pallas_call(
        _cmp_partial_paged_kernel,
        grid_spec=grid_spec,
        out_shape=jax.ShapeDtypeStruct((b, npg * cpp, n), F32),
        compiler_params=_cparams("arbitrary", "arbitrary"),
        name="cmp_partial_paged",
    )(page_table, *([cache_t] * PAGES_PER_STEP), jnp.asarray(perm, BF), wpair)


def _cmp_finish_kernel(c_ref, cpos_ref, w2_ref, o_ref):
    c = c_ref[...]
    nrow = c.shape[0]
    c_next = pltpu.roll(c, nrow - 1, axis=0)
    for s in range(2):
        for gi in range(N_KVG):
            o = (s * N_KVG + gi) * 2 * CMP_HID
            hid = (c[:, o:o + CMP_HID] + cpos_ref[0:1, o:o + CMP_HID]
                   + c_next[:, o + CMP_HID:o + 2 * CMP_HID] + cpos_ref[1:2, o + CMP_HID:o + 2 * CMP_HID])
            oo = (s * N_KVG + gi) * HD_N
            o_ref[:, oo:oo + HD_N] = _dot(_gelu_tanh(hid), w2_ref[s])


def _cmp_finish(c, cpos, w2):
    b, nch, n = c.shape
    return pl.pallas_call(
        _cmp_finish_kernel,
        grid=(b,),
        in_specs=[pl.BlockSpec((None, nch, n), lambda i: (i, 0, 0)), pl.BlockSpec(cpos.shape, lambda i: (0, 0)),
                  pl.BlockSpec(w2.shape, lambda i: (0, 0, 0))],
        out_specs=pl.BlockSpec((None, nch, KV_COLS), lambda i: (i, 0, 0)),
        out_shape=jax.ShapeDtypeStruct((b, nch, KV_COLS), F32),
        compiler_params=_cparams("arbitrary"),
        name="cmp_finish",
    )(c, cpos, w2.astype(BF))


def _cmp_pos_rows(cmp_pos):
    pos = cmp_pos.reshape(2, STRIDE, 1, 1, HD_N)
    rows = jnp.broadcast_to(pos, (2, STRIDE, 2, N_KVG, HD_N)).reshape(2, STRIDE * KV_COLS)
    return jnp.concatenate([rows, jnp.zeros((6, STRIDE * KV_COLS), F32)], axis=0)


def _bias_table_kernel(dist_ref, rb_ref, o_ref):
    bucket = _t5_bucket(dist_ref[...])
    for h in range(H_N):
        out = jnp.zeros(bucket.shape, F32)
        for b in range(N_BUCKETS):
            out = jnp.where(bucket == b, rb_ref[b, h], out)
        o_ref[h] = out


def _bias_table(dist, rel_bias, tr):
    r, n = dist.shape
    return pl.pallas_call(
        _bias_table_kernel,
        grid=(r // tr,),
        in_specs=[pl.BlockSpec((tr, n), lambda i: (i, 0)),
                  pl.BlockSpec(memory_space=pltpu.SMEM)],
        out_specs=pl.BlockSpec((H_N, tr, n), lambda i: (0, i, 0)),
        out_shape=jax.ShapeDtypeStruct((H_N, r, n), F32),
        compiler_params=_cparams("arbitrary"),
        name="bias_table",
    )(dist, rel_bias)


def _softmax_rows(logits, valid):
    lm = jnp.where(valid, logits, NEG_INF)
    e = jnp.exp(lm - jnp.max(lm, axis=-1, keepdims=True))
    return e / jnp.sum(e, axis=-1, keepdims=True)


def _cmp_attn_kernel(q_ref, kv_ref, bias_ref, ovt_ref, o_ref, sel_ref):
    tq = q_ref.shape[0]
    ncp = kv_ref.shape[0]
    nsb = ovt_ref.shape[0]
    q0 = pl.program_id(1) * tq
    qpos = q0 + lax.broadcasted_iota(I32, (tq, 1), 0)
    cend = lax.broadcasted_iota(I32, (1, ncp), 1) * STRIDE + (L_CMP - 1)
    valid = (qpos >= cend) & (lax.broadcasted_iota(I32, (1, ncp), 1) < ncp - 1)
    validf = valid.astype(F32)
    q = q_ref[...] * (HD_N ** -0.5)
    blk = lax.broadcasted_iota(I32, (nsb, tq), 0)
    cur = (q0 + lax.broadcasted_iota(I32, (1, tq), 1)) // L_SEL
    forced = (blk == 0) | (blk == cur) | (blk == cur - 1)
    future = blk > cur
    for gi in range(N_KVG):
        kc = kv_ref[:, gi * HD_N:(gi + 1) * HD_N]
        vc = kv_ref[:, (N_KVG + gi) * HD_N:(N_KVG + gi + 1) * HD_N]
        pcs = jnp.zeros((tq, ncp), F32)
        for hl in range(HPG):
            h = gi * HPG + hl
            sl = slice(h * HD_N, (h + 1) * HD_N)
            pc = _softmax_rows(_dot_nt(q[:, sl], kc) + bias_ref[h], valid) * validf
            pcs = pcs + pc
            o_ref[:, sl] = _dot(pc, vc)
        imp = _dot_nt(ovt_ref[...], pcs)
        score = jnp.where(forced, FORCE, jnp.where(future, -FORCE, imp))
        rank = jnp.zeros((nsb, tq), F32)
        for i in range(nsb):
            si = score[i:i + 1, :]
            rank = rank + ((si > score) | ((si == score) & (i < blk))).astype(F32)
        sel_ref[gi] = (rank < N_TOP).astype(sel_ref.dtype)


def _cmp_sel_overlap_t(nc, ncp, nsb):
    s = np.arange(nc)[None, :] * STRIDE
    j = np.arange(nsb)[:, None] * L_SEL
    ov = np.clip(np.minimum(s + L_CMP, j + L_SEL) - np.maximum(s, j), 0, None) / L_CMP
    return np.pad(ov, ((0, 0), (0, ncp - nc))).astype(np.float32)


def _cmp_attn_prompt(p_n, kv_cmp, bias_c, t):
    b = kv_cmp.shape[0]
    ncp = kv_cmp.shape[1]
    nsb = t // L_SEL
    tq = ATT_TILE
    nqt = t // tq
    ovt = jnp.asarray(_cmp_sel_overlap_t(ncp - 1, ncp, nsb), BF)
    return pl.pallas_call(
        _cmp_attn_kernel,
        grid=(b, nqt),
        in_specs=[pl.BlockSpec((tq, C_N), lambda i, j: (i * nqt + j, 0)),
                  pl.BlockSpec((None, ncp, KV_COLS), lambda i, j: (i, 0, 0)),
                  pl.BlockSpec((H_N, tq, ncp), lambda i, j: (0, j, 0)),
                  pl.BlockSpec((nsb, ncp), lambda i, j: (0, 0))],
        out_specs=[pl.BlockSpec((tq, C_N), lambda i, j: (i * nqt + j, 0)),
                   pl.BlockSpec((None, N_KVG, nsb, tq), lambda i, j: (i, 0, 0, j))],
        out_shape=[jax.ShapeDtypeStruct((b * t, C_N), F32), jax.ShapeDtypeStruct((b, N_KVG, nsb, t), BF)],
        compiler_params=_cparams("arbitrary", "arbitrary"),
        name="nsa_cmp_select",
    )(p_n, kv_cmp, bias_c, ovt)


ATT_TK = 128
ATT_R = ATT_TILE // ATT_TK
N_SEL_OFFS = ATT_R + (MAX_DIST + ATT_TK - 1) // ATT_TK + 1
N_WIN_OFFS = ATT_R + WINDOW // ATT_TK
QA_COLS = HD_N + 32
SEL_STEP_TILES = 4
WIN_STEP_TILES = 2


def _swa_kernel(qa_ref, ks_ref, vs_ref, kw_ref, vw_ref, tabs_ref, tabw_ref, gn_ref, oc_ref, o_ref):
    tq = qa_ref.shape[1]
    qt = pl.program_id(2)
    top = ATT_R * qt + ATT_R - 1

    heads = range(HPG)

    def attend(k_ref, v_ref, tab_ref, lo, n_off, step_tiles):
        tiles = range(step_tiles)

        def body(kp, carry):
            ki = [lo + step_tiles * kp + j for j in tiles]
            kt = [k_ref[jnp.minimum(i, top)] for i in ki]
            vt = [v_ref[jnp.minimum(i, top)] for i in ki]
            off = [jnp.where(i > top, n_off, jnp.minimum(top - i, n_off - 1)) for i in ki]
            s = [[lax.dot_general(kt[j], qa_ref[hl], (((1,), (1,)), ((), ())), preferred_element_type=F32)
                  + tab_ref[off[j], hl * ATT_TK:(hl + 1) * ATT_TK, :] for j in tiles] for hl in heads]
            m_new = [functools.reduce(jnp.maximum, [carry[hl][0]] + [jnp.max(s[hl][j], axis=0, keepdims=True)
                                                                    for j in tiles]) for hl in heads]
            alpha = [jnp.exp(carry[hl][0] - m_new[hl]) for hl in heads]
            p = [[jnp.exp(s[hl][j] - m_new[hl]) for j in tiles] for hl in heads]
            l = [alpha[hl] * carry[hl][1] + sum(jnp.sum(p[hl][j], axis=0, keepdims=True) for j in tiles)
                 for hl in heads]
            acc = [alpha[hl] * carry[hl][2] + sum(jnp.dot(vt[j], p[hl][j].astype(BF), preferred_element_type=F32)
                                                  for j in tiles) for hl in heads]
            return tuple((m_new[hl], l[hl], acc[hl]) for hl in heads)

        init = tuple((jnp.full((1, tq), NEG_INF, F32), jnp.zeros((1, tq), F32), jnp.zeros((HD_N, tq), F32))
                     for _ in heads)
        res = lax.fori_loop(0, (top - lo) // step_tiles + 1, body, init)
        return [acc / l for _, l, acc in res]

    o_sel = attend(ks_ref, vs_ref, tabs_ref, 0, N_SEL_OFFS, SEL_STEP_TILES)
    o_win = attend(kw_ref, vw_ref, tabw_ref, jnp.maximum(top + 1 - N_WIN_OFFS, 0), N_WIN_OFFS, WIN_STEP_TILES)
    gates = _sigmoid(gn_ref[...])
    gates_t = gates.T
    for hl in heads:
        sl = slice(hl * HD_N, (hl + 1) * HD_N)
        o_t = gates_t[3 * hl + 1:3 * hl + 2, :] * o_sel[hl] + gates_t[3 * hl + 2:3 * hl + 3, :] * o_win[hl]
        o_ref[:, sl] = (gates[:, 3 * hl:3 * hl + 1] * oc_ref[:, sl] + o_t.T).astype(o_ref.dtype)


def _swa_prompt(p_n, qa, ks, vs, kw, vw, tab_s, tab_w, o_cmp, t):
    b = qa.shape[0]
    tq = ATT_TILE
    nqt = t // tq
    nkt = t // ATT_TK
    gw = HPG * HD_N
    k_spec = pl.BlockSpec((None, None, nkt, ATT_TK, QA_COLS), lambda i, g, j: (i, g, 0, 0, 0))
    v_spec = pl.BlockSpec((None, None, nkt, HD_N, ATT_TK), lambda i, g, j: (i, g, 0, 0, 0))
    tab_spec = lambda n: pl.BlockSpec((n + 1, HPG * ATT_TK, tq), lambda i, g, j: (0, g, 0))
    return pl.pallas_call(
        _swa_kernel,
        grid=(b, N_KVG, nqt),
        in_specs=[pl.BlockSpec((None, HPG, tq, QA_COLS), lambda i, g, j: (i, g, j, 0)),
                  k_spec, v_spec, k_spec, v_spec, tab_spec(N_SEL_OFFS), tab_spec(N_WIN_OFFS),
                  pl.BlockSpec((tq, GN_GROUP_COLS), lambda i, g, j: (i * nqt + j, NP_GN // GN_GROUP_COLS + g)),
                  pl.BlockSpec((tq, gw), lambda i, g, j: (i * nqt + j, g))],
        out_specs=pl.BlockSpec((tq, gw), lambda i, g, j: (i * nqt + j, g)),
        out_shape=jax.ShapeDtypeStruct((b * t, C_N), BF),
        compiler_params=_cparams("arbitrary", "arbitrary", "arbitrary"),
        name="nsa_sel_win",
    )(qa, ks, vs, kw, vw, tab_s, tab_w, p_n, o_cmp)


def _swa_operands(p_n, sel, b, t):
    nsb = t // L_SEL
    nkt = t // ATT_TK
    q = (p_n[:, :C_N] * (HD_N ** -0.5)).reshape(b, t, H_N, HD_N).transpose(0, 2, 1, 3)
    pen = jnp.where(jnp.transpose(sel, (0, 1, 3, 2)) > 0.5, 0.0, NEG_INF).astype(F32)
    pen = jnp.broadcast_to(pen[:, :, None], (b, N_KVG, HPG, t, nsb)).reshape(b, H_N, t, nsb)
    qa = jnp.concatenate([q, pen], axis=-1).astype(BF)

    def split(cols):
        kv = p_n[:, cols:cols + KV_COLS].reshape(b, t, 2, N_KVG, HD_N)
        k = jnp.transpose(kv[:, :, 0], (0, 2, 1, 3))
        v = jnp.transpose(kv[:, :, 1], (0, 2, 3, 1)).reshape(b, N_KVG, HD_N, nkt, ATT_TK)
        return k, jnp.transpose(v, (0, 1, 3, 2, 4)).astype(BF)

    onehot = jnp.asarray(np.arange(t)[:, None] // L_SEL == np.arange(nsb)[None, :], F32)
    k_s, v_s = split(NP_KVS)
    k_w, v_w = split(NP_KVW)
    ext = lambda k, e: jnp.concatenate([k, jnp.broadcast_to(e, (b, N_KVG, t, nsb))], axis=-1).astype(BF) \
        .reshape(b, N_KVG, nkt, ATT_TK, QA_COLS)
    return qa, ext(k_s, onehot), v_s, ext(k_w, jnp.zeros((t, nsb), F32)), v_w


def _sample_cmp_kernel(q_ref, kv_ref, rbt_ref, ov_ref, o_ref, idx_ref, *, past):
    ncp = kv_ref.shape[0]
    nsbp = ov_ref.shape[1]
    nsb = -(-(past + 1) // L_SEL)
    q = q_ref[...] * (HD_N ** -0.5)
    hrow = lax.broadcasted_iota(I32, (H_N, 1), 0)
    nidx = lax.broadcasted_iota(I32, (1, ncp), 1)
    valid = nidx < ncp - 1
    bias = _bias_rows(past - (nidx * STRIDE + (L_CMP - 1)), rbt_ref[...])
    logits = jnp.zeros((H_N, ncp), F32)
    for gi in range(N_KVG):
        lg = _dot_nt(q, kv_ref[:, gi * HD_N:(gi + 1) * HD_N])
        logits = jnp.where(hrow // HPG == gi, lg, logits)
    pc = _softmax_rows(logits + bias, valid) * valid.astype(F32)
    o = jnp.zeros((H_N, HD_N), F32)
    for gi in range(N_KVG):
        og = _dot(pc, kv_ref[:, (N_KVG + gi) * HD_N:(N_KVG + gi + 1) * HD_N])
        o = jnp.where(hrow // HPG == gi, og, o)
    o_ref[...] = o
    imp_h = _dot(pc, ov_ref[...])
    blk = lax.broadcasted_iota(I32, (8, nsbp), 1)
    grow = lax.broadcasted_iota(I32, (8, 1), 0)
    cur = past // L_SEL
    score = jnp.full((8, nsbp), -3e38, F32)
    for gi in range(N_KVG):
        imp = jnp.sum(jnp.where(hrow // HPG == gi, imp_h, 0.0), axis=0, keepdims=True)
        score = jnp.where(grow == gi, imp, score)
    forced = (blk == 0) | (blk == cur) | (blk == cur - 1)
    score = jnp.where(forced, FORCE, jnp.where(blk > cur, -FORCE, score))
    score = jnp.where((blk < nsb) & (grow < N_KVG), score, -3e38)
    lane = lax.broadcasted_iota(I32, (8, LANES), 1)
    picks = jnp.zeros((8, LANES), I32)
    for it in range(N_TOP):
        mx = jnp.max(score, axis=-1, keepdims=True)
        pick = jnp.min(jnp.where(score == mx, blk, nsbp), axis=-1, keepdims=True)
        picks = jnp.where(lane == it, pick, picks)
        score = jnp.where(blk == pick, -3e38, score)
    idx_ref[...] = picks


def _sample_cmp(q, kv_cmp, rel_bias, past):
    b, ncp, _ = kv_cmp.shape
    nc = ncp - 1
    nsb = -(-(past + 1) // L_SEL)
    nsbp = -(-nsb // LANES) * LANES
    s = np.arange(nc)[:, None] * STRIDE
    j = np.arange(nsb)[None, :] * L_SEL
    ov = np.clip(np.minimum(s + L_CMP, j + L_SEL) - np.maximum(s, j), 0, None) / L_CMP
    ov = np.pad(ov, ((0, ncp - nc), (0, nsbp - nsb))).astype(np.float32)
    return pl.pallas_call(
        functools.partial(_sample_cmp_kernel, past=past),
        grid=(b,),
        in_specs=[pl.BlockSpec((None, H_N, HD_N), lambda i: (i, 0, 0)),
                  pl.BlockSpec((None, ncp, KV_COLS), lambda i: (i, 0, 0)),
                  pl.BlockSpec((H_N, N_BUCKETS), lambda i: (0, 0)),
                  pl.BlockSpec((ncp, nsbp), lambda i: (0, 0))],
        out_specs=[pl.BlockSpec((None, H_N, HD_N), lambda i: (i, 0, 0)),
                   pl.BlockSpec((None, 8, LANES), lambda i: (i, 0, 0))],
        out_shape=[jax.ShapeDtypeStruct((b, H_N, HD_N), F32), jax.ShapeDtypeStruct((b, 8, LANES), I32)],
        compiler_params=_cparams("arbitrary"),
        name="nsa_sample_cmp_select",
    )(q, kv_cmp, rel_bias.T, jnp.asarray(ov, BF))


def _block_copy_kernel(pg_ref, *refs):
    del pg_ref
    x_refs, o_ref = refs[:-1], refs[-1]
    for n, x_ref in enumerate(x_refs):
        o_ref[n] = x_ref[...]


def _gather_sel_pages(cache_t, page):
    rows, nslot = page.shape

    def slot_spec(n):
        return pl.BlockSpec((None, 2, None, HD_N, PAGE_SIZE), lambda i, pg: (pg[i, n], 0, i % N_KVG, 0, 0))

    return pl.pallas_call(
        _block_copy_kernel,
        grid_spec=pltpu.PrefetchScalarGridSpec(
            num_scalar_prefetch=1, grid=(rows,),
            in_specs=[slot_spec(n) for n in range(nslot)],
            out_specs=pl.BlockSpec((None, nslot, 2, HD_N, PAGE_SIZE), lambda i, pg: (i, 0, 0, 0, 0))),
        out_shape=jax.ShapeDtypeStruct((rows, nslot, 2, HD_N, PAGE_SIZE), cache_t.dtype),
        compiler_params=_cparams("arbitrary"),
        name="gather_sel_pages",
    )(page, *([cache_t] * nslot))


def _sample_swa_kernel(idx_ref, q_ref, blk_ref, win_ref, new_ref, rbt_ref, gate_ref, oc_ref, o_ref, *, past):
    bi = pl.program_id(0)
    q = q_ref[...] * (HD_N ** -0.5)
    rbt = rbt_ref[...]
    hrow = lax.broadcasted_iota(I32, (H_N, 1), 0)
    nk = N_TOP * PAGE_SIZE
    lane = lax.broadcasted_iota(I32, (1, nk), 1)
    new_blk = past // L_SEL
    bias_new = rbt[:, 0:1]
    gates = _sigmoid(gate_ref[...])
    nwin = win_ref.shape[0]
    wdist = nwin - lax.broadcasted_iota(I32, (1, nwin), 1)
    wbias = _bias_rows(wdist, rbt)
    wvalid = (wdist >= 0) & (wdist <= WINDOW)

    def with_new(logits, valid, weigh, k_new, v_new):
        l_new = jnp.sum(q * k_new, axis=-1, keepdims=True) + bias_new
        lm = jnp.where(valid, logits, NEG_INF)
        m = jnp.maximum(jnp.max(lm, axis=-1, keepdims=True), l_new)
        e = jnp.where(valid, jnp.exp(lm - m), 0.0)
        e_new = jnp.exp(l_new - m)
        den = jnp.sum(e, axis=-1, keepdims=True) + e_new
        return (weigh(e) + e_new * v_new) / den

    o_sel = jnp.zeros((H_N, HD_N), F32)
    o_win = jnp.zeros((H_N, HD_N), F32)
    bpp = PAGE_SIZE // L_SEL
    tok = lane % PAGE_SIZE
    for gi in range(N_KVG):
        ksl = slice(gi * HD_N, (gi + 1) * HD_N)
        vsl = slice((N_KVG + gi) * HD_N, (N_KVG + gi + 1) * HD_N)
        bid = jnp.zeros((1, nk), I32)
        for n in range(N_TOP):
            bid = jnp.where(lane // PAGE_SIZE == n, idx_ref[bi, gi, n], bid)
        dist = past - ((bid // bpp) * PAGE_SIZE + tok)
        valid = (bid != new_blk) & (tok // L_SEL == bid % bpp) & (dist >= 0)
        kt = jnp.concatenate([blk_ref[gi, n, 0] for n in range(N_TOP)], axis=1)
        vt = jnp.concatenate([blk_ref[gi, n, 1] for n in range(N_TOP)], axis=1)
        logits = _dot(q, kt) + _bias_rows(dist, rbt)
        og = with_new(logits, valid, lambda e, vt=vt: _dot_nt(e, vt), new_ref[0:1, ksl], new_ref[0:1, vsl])
        o_sel = jnp.where(hrow // HPG == gi, og, o_sel)
        logits = _dot_nt(q, win_ref[:, ksl]) + wbias
        og = with_new(logits, wvalid, lambda e, vsl=vsl: _dot(e, win_ref[:, vsl]), new_ref[1:2, ksl],
                      new_ref[1:2, vsl])
        o_win = jnp.where(hrow // HPG == gi, og, o_win)
    o_ref[...] = gates[:, 0:1] * oc_ref[...] + gates[:, 1:2] * o_sel + gates[:, 2:3] * o_win


def _sample_swa(idx, q, blocks, win, new_kv, rel_bias, gates, o_cmp, past):
    b = q.shape[0]
    w = win.shape[1]
    grid_spec = pltpu.PrefetchScalarGridSpec(
        num_scalar_prefetch=1, grid=(b,),
        in_specs=[pl.BlockSpec((None, H_N, HD_N), lambda i, ix: (i, 0, 0)),
                  pl.BlockSpec((None, N_KVG, N_TOP, 2, HD_N, PAGE_SIZE), lambda i, ix: (i, 0, 0, 0, 0, 0)),
                  pl.BlockSpec((None, w, KV_COLS), lambda i, ix: (i, 0, 0)),
                  pl.BlockSpec((None, 2, KV_COLS), lambda i, ix: (i, 0, 0)),
                  pl.BlockSpec((H_N, N_BUCKETS), lambda i, ix: (0, 0)),
                  pl.BlockSpec((None, H_N, 3), lambda i, ix: (i, 0, 0)),
                  pl.BlockSpec((None, H_N, HD_N), lambda i, ix: (i, 0, 0))],
        out_specs=pl.BlockSpec((None, H_N, HD_N), lambda i, ix: (i, 0, 0)))
    return pl.pallas_call(
        functools.partial(_sample_swa_kernel, past=past),
        grid_spec=grid_spec,
        out_shape=jax.ShapeDtypeStruct((b, H_N, HD_N), F32),
        compiler_params=_cparams("arbitrary"),
        name="nsa_sample_sel_win",
    )(idx, q, blocks, win, new_kv, rel_bias.T, gates, o_cmp)


ROUTER_COLS = LANES


def _router_kernel(x_ref, g_ref, sh_ref, sc_ref, w_ref, b_ref, h_ref, e_ref, wt_ref, rk_ref, cnt_ref):
    i = pl.program_id(0)
    tm = x_ref.shape[0]

    @pl.when(i == 0)
    def _():
        cnt_ref[...] = jnp.zeros_like(cnt_ref)

    h = (_rms(x_ref[...], g_ref[...]) * (1.0 + sc_ref[...]) + sh_ref[...]).astype(BF)
    _store_folded(h_ref, h.astype(F32))
    logits = jnp.dot(h, w_ref[...], preferred_element_type=F32) + b_ref[...]
    lane = lax.broadcasted_iota(I32, (tm, ROUTER_COLS), 1)

    def top1(vals, ok):
        vm = jnp.where(ok, vals, -3e38)
        mx = jnp.max(vm, axis=-1, keepdims=True)
        return mx, jnp.min(jnp.where(ok & (vm == mx), lane, ROUTER_COLS), axis=-1, keepdims=True)

    isg = lane < N_EGROUPS
    pg = _softmax_rows(logits, isg)
    g_w, g_i = top1(pg, isg)
    ise = (lane >= N_EGROUPS) & ((lane - N_EGROUPS) // EXP_PER_GROUP == g_i)
    pe = _softmax_rows(logits, ise)
    w0, l0 = top1(pe, ise)
    w1, l1 = top1(pe, ise & (lane != l0))
    den = w0 + w1
    e0 = l0 - N_EGROUPS
    e1 = l1 - N_EGROUPS
    e_ref[...] = jnp.where(lane == 0, e0, jnp.where(lane == 1, e1, 0))
    wt_ref[...] = jnp.where(lane == 0, w0 / den * g_w, jnp.where(lane == 1, w1 / den * g_w, 0.0))
    oh0 = (lane == e0).astype(F32)
    oh1 = (lane == e1).astype(F32)
    cnt = oh0 + oh1
    ti = lax.broadcasted_iota(I32, (tm, tm), 0)
    si = lax.broadcasted_iota(I32, (tm, tm), 1)
    before = _dot((ti > si).astype(F32), cnt) + cnt_ref[...]
    r0 = jnp.sum(before * oh0, axis=-1, keepdims=True)
    r1 = jnp.sum(before * oh1, axis=-1, keepdims=True)
    rk_ref[...] = jnp.where(lane == 0, r0, jnp.where(lane == 1, r1, 0.0)).astype(I32)
    cnt_ref[...] = cnt_ref[...] + jnp.sum(cnt, axis=0, keepdims=True)


def _router(x, g, shift, scale, w_r, b_r, tm, rpb):
    m = x.shape[0]
    r = shift.shape[1]
    rows = lambda tn: pl.BlockSpec((tm, tn), lambda i: (i, 0))
    mods = pl.BlockSpec((None, r, D_MODEL), lambda i: ((i * tm) // rpb, 0, 0))
    small = lambda dt: jax.ShapeDtypeStruct((m, ROUTER_COLS), dt)
    return pl.pallas_call(
        _router_kernel,
        grid=(m // tm,),
        in_specs=[rows(D_MODEL), pl.BlockSpec((1, D_MODEL), lambda i: (0, 0)), mods, mods,
                  pl.BlockSpec((D_MODEL, ROUTER_COLS), lambda i: (0, 0)),
                  pl.BlockSpec((1, ROUTER_COLS), lambda i: (0, 0))],
        out_specs=[pl.BlockSpec((tm * ROW_FOLD, LANES), lambda i: (i, 0)),
                   rows(ROUTER_COLS), rows(ROUTER_COLS), rows(ROUTER_COLS),
                   pl.BlockSpec((1, ROUTER_COLS), lambda i: (0, 0))],
        out_shape=[jax.ShapeDtypeStruct((m * ROW_FOLD, LANES), F32), small(I32), small(F32), small(I32),
                   jax.ShapeDtypeStruct((1, ROUTER_COLS), F32)],
        compiler_params=_cparams("arbitrary"),
        name="moe_router",
    )(x, g.reshape(1, D_MODEL), shift, scale, w_r, b_r)


ROW_FOLD = D_MODEL // LANES


def _store_folded(ref, x):
    n = x.shape[0]
    for c in range(ROW_FOLD):
        ref[pl.ds(c, n, stride=ROW_FOLD), :] = x[:, c * LANES:(c + 1) * LANES]


def _load_folded(ref, first_row, n):
    return jnp.concatenate([ref[pl.ds(first_row * ROW_FOLD + c, n, stride=ROW_FOLD), :] for c in range(ROW_FOLD)],
                           axis=1)


def _row_gather_ring(src_hbm, buf, sems, groups, idx_now, idx_next, inline_next=False):
    i = pl.program_id(0)
    last = pl.num_programs(0) - 1
    slot = i % 2
    total = sum(cnt for _, cnt, _ in groups)
    assert 2 * total * ROW_FOLD == buf.shape[0]

    def fold(row):
        return pl.ds(pl.multiple_of(row * ROW_FOLD, ROW_FOLD), ROW_FOLD)

    def start(idx, s, first, k, r, priority):
        pltpu.make_async_copy(src_hbm.at[fold(idx(k, r))], buf.at[fold(s * total + first + r)],
                              sems.at[s]).start(priority=priority)

    def start_all(idx, s):
        for first, cnt, k in groups:
            per_trip = math.gcd(cnt, 8)

            def issue(j, c):
                for u in range(per_trip):
                    start(idx, s, first, k, per_trip * j + u, u % 2)
                return c
            lax.fori_loop(0, cnt // per_trip, issue, 0)

    def wait_slot(s):
        whole = buf.at[pl.ds(pl.multiple_of(s * total * ROW_FOLD, ROW_FOLD), total * ROW_FOLD)]
        pltpu.make_async_copy(whole, whole, sems.at[s]).wait()

    @pl.when(i == 0)
    def _():
        start_all(idx_now, slot)

    def finish():
        if inline_next:
            @pl.when(i == last)
            def _():
                wait_slot(1 - slot)

    if inline_next:
        wait_slot(slot)
        for first, cnt, k in groups:
            for r in range(cnt):
                start(idx_next, 1 - slot, first, k, r, r % 2)
    else:
        @pl.when(i < last)
        def _():
            start_all(idx_next, 1 - slot)

        wait_slot(slot)
    return slot * total, finish


def _expert_kernel(be_ref, nx_ref, rt_ref, rtn_ref, h_hbm, w1_hbm, w3_hbm, w2_hbm, o_ref, xbuf, sems, w1f, w3f, w2f,
                   wsems, w1b, w3b, w2b):
    i = pl.program_id(0)
    blk = rt_ref.shape[1]

    def fetch(e, s):
        return [pltpu.make_async_copy(src.at[e], dst.at[s], wsems.at[s])
                for src, dst in ((w1_hbm, w1f), (w3_hbm, w3f), (w2_hbm, w2f))]

    @pl.when(nx_ref[i, 0] == 1)
    def _():
        s = nx_ref[i, 1]

        @pl.when(i == 0)
        def _():
            for c in fetch(be_ref[i], s):
                c.start()

        for c in fetch(be_ref[i], s):
            c.wait()
        w1b[...] = w1f[s].astype(BF)
        w3b[...] = w3f[s].astype(BF)
        w2b[...] = w2f[s].astype(BF)

        @pl.when(nx_ref[i, 2] == 1)
        def _():
            for c in fetch(nx_ref[i, 3], 1 - s):
                c.start()

    base, finish = _row_gather_ring(h_hbm, xbuf, sems, ((0, blk, 0),), lambda k, r: rt_ref[0, r],
                                    lambda k, r: rtn_ref[0, r], inline_next=True)
    x = _load_folded(xbuf, base, blk).astype(BF)
    a = jnp.dot(x, w1b[...], preferred_element_type=F32)
    b = jnp.dot(x, w3b[...], preferred_element_type=F32)
    hid = a * _sigmoid(a) * b
    _store_folded(o_ref, jnp.dot(hid.astype(BF), w2b[...], preferred_element_type=F32))
    finish()


def _experts(h2, row_tok, blk_exp, w1, w3, w2):
    nblk, _, blk = row_tok.shape
    first = jnp.concatenate([jnp.ones((1,), I32), (blk_exp[1:] != blk_exp[:-1]).astype(I32)])
    run_end = jnp.sum((blk_exp[None, :] <= blk_exp[:, None]).astype(I32), axis=1)
    has_next = (run_end < nblk).astype(I32)
    nxt = blk_exp[jnp.minimum(run_end, nblk - 1)]
    runs = jnp.stack([first, (jnp.cumsum(first) - 1) % 2, has_next, nxt], axis=1).astype(I32)
    idx_spec = lambda d: pl.BlockSpec((None, 1, blk), lambda i, be, nx: (jnp.minimum(i + d, nblk - 1), 0, 0),
                                      memory_space=pltpu.SMEM)
    any_spec = pl.BlockSpec(memory_space=pl.ANY)
    grid_spec = pltpu.PrefetchScalarGridSpec(
        num_scalar_prefetch=2, grid=(nblk,),
        in_specs=[idx_spec(0), idx_spec(1), any_spec, any_spec, any_spec, any_spec],
        out_specs=pl.BlockSpec((blk * ROW_FOLD, LANES), lambda i, be, nx: (i, 0)),
        scratch_shapes=[pltpu.VMEM((2 * blk * ROW_FOLD, LANES), F32), pltpu.SemaphoreType.DMA((2,)),
                        pltpu.VMEM((2, D_MODEL, D_EXP), F32), pltpu.VMEM((2, D_MODEL, D_EXP), F32),
                        pltpu.VMEM((2, D_EXP, D_MODEL), F32), pltpu.SemaphoreType.DMA((2,)),
                        pltpu.VMEM((D_MODEL, D_EXP), BF), pltpu.VMEM((D_MODEL, D_EXP), BF),
                        pltpu.VMEM((D_EXP, D_MODEL), BF)])
    return pl.pallas_call(
        _expert_kernel,
        grid_spec=grid_spec,
        out_shape=jax.ShapeDtypeStruct((nblk * blk * ROW_FOLD, LANES), F32),
        compiler_params=_cparams("arbitrary"),
        name="moe_experts",
    )(blk_exp, runs, row_tok, row_tok, h2, w1, w3, w2)


def _final_kernel(x_ref, g_ref, dest_ref, destn_ref, ys_hbm, wt_ref, nf_ref, o_ref, ybuf, sems):
    tm = x_ref.shape[0]
    groups = tuple((k * tm, tm, k) for k in range(TOP_K))
    base, _ = _row_gather_ring(ys_hbm, ybuf, sems, groups, lambda k, r: dest_ref[k, r], lambda k, r: destn_ref[k, r])
    wt = wt_ref[...]
    moe = wt[:, 0:1] * _load_folded(ybuf, base, tm) + wt[:, 1:2] * _load_folded(ybuf, base + tm, tm)
    o_ref[...] = _rms(x_ref[...] + g_ref[...] * moe, nf_ref[...])


def _final(x, gate, ys, dest, wts, norm_f, tm, rpb):
    m = x.shape[0]
    r = gate.shape[1]
    nt = m // tm
    rows = lambda tn: pl.BlockSpec((tm, tn), lambda i: (i, 0))
    idx_spec = lambda d: pl.BlockSpec((None, TOP_K, tm), lambda i: (jnp.minimum(i + d, nt - 1), 0, 0),
                                      memory_space=pltpu.SMEM)
    return pl.pallas_call(
        _final_kernel,
        grid=(nt,),
        in_specs=[rows(D_MODEL), pl.BlockSpec((None, r, D_MODEL), lambda i: ((i * tm) // rpb, 0, 0)),
                  idx_spec(0), idx_spec(1),
                  pl.BlockSpec(memory_space=pl.ANY),
                  rows(ROUTER_COLS), pl.BlockSpec((1, D_MODEL), lambda i: (0, 0))],
        out_specs=rows(D_MODEL),
        out_shape=jax.ShapeDtypeStruct((m, D_MODEL), F32),
        scratch_shapes=[pltpu.VMEM((2 * TOP_K * tm * ROW_FOLD, LANES), F32), pltpu.SemaphoreType.DMA((2,))],
        compiler_params=_cparams("arbitrary"),
        name="moe_combine_final_norm",
    )(x, gate, dest, dest, ys, wts, norm_f.reshape(1, D_MODEL))


def _moe_and_final(x1, g2, shift, scale, gate, w_r, b_r, exp_w1, exp_w3, exp_w2, norm_f, tm, rpb, blk):
    m = x1.shape[0]
    h2, eid, wts, rank, counts = _router(x1, g2, shift, scale, w_r, b_r, tm, rpb)
    counts = counts[0, :N_EXP].astype(I32)
    padded = (counts + blk - 1) // blk * blk
    pend = jnp.cumsum(padded)
    pstart = pend - padded
    n_blocks = -(-(m * TOP_K) // blk) + N_EXP
    starts = jnp.arange(n_blocks, dtype=I32)[:, None] * blk
    blk_exp = jnp.minimum(jnp.sum((pend[None, :] <= starts).astype(I32), axis=1), N_EXP - 1)
    e = eid[:, :TOP_K]
    dest = pstart[e] + rank[:, :TOP_K]
    tok = jnp.broadcast_to(jnp.arange(m, dtype=I32)[:, None], (m, TOP_K))
    row_tok = jnp.zeros((n_blocks * blk,), I32).at[dest.reshape(-1)].set(tok.reshape(-1))
    ys = _experts(h2, row_tok.reshape(n_blocks, 1, blk), blk_exp, exp_w1, exp_w3, exp_w2)
    dest_t = jnp.transpose(dest.reshape(m // tm, tm, TOP_K), (0, 2, 1))
    return _final(x1, gate, ys, dest_t, wts, norm_f, tm, rpb)


def _pack_in_proj(w_in):
    o = C_RIN
    w_r = w_in[:, :o]
    w_q = w_in[:, o:o + C_N + 3 * KV_COLS]
    o += C_N + 3 * KV_COLS
    w_gn = w_in[:, o:o + 3 * H_N].reshape(D_MODEL, N_KVG, 3 * HPG)
    w_gn = jnp.pad(w_gn, ((0, 0), (0, 0), (0, GN_GROUP_COLS - 3 * HPG))).reshape(D_MODEL, N_KVG * GN_GROUP_COLS)
    o += 3 * H_N
    w_gm = w_in[:, o:]
    return w_r.astype(BF), jnp.concatenate([w_q, w_gn], axis=1).astype(BF), w_gm.astype(BF)


def _prompt_bias_tables(rel_bias, t):
    tq, tk = ATT_TILE, ATT_TK
    i = np.arange(tq)[None, :]
    j = np.arange(tk)[:, None]
    dist = np.stack([tk * (o - (ATT_R - 1)) + i - j for o in range(N_WIN_OFFS)]).astype(np.int32)
    raw = _bias_table(jnp.asarray(dist.reshape(N_WIN_OFFS * tk, tq)), rel_bias, tk)
    raw = jnp.transpose(raw.reshape(H_N, N_WIN_OFFS, tk, tq), (1, 0, 2, 3))
    ok_w = jnp.asarray((dist >= 0) & (dist <= WINDOW))[:, None]
    ok_s = jnp.asarray(dist[:N_SEL_OFFS] >= 0)[:, None]
    masked = jnp.full((1, H_N * tk, tq), NEG_INF, F32)
    tab_w = jnp.concatenate([jnp.where(ok_w, raw, NEG_INF).reshape(N_WIN_OFFS, H_N * tk, tq), masked])
    tab_s = jnp.concatenate([jnp.where(ok_s, raw[:N_SEL_OFFS], NEG_INF).reshape(N_SEL_OFFS, H_N * tk, tq), masked])
    nc = (t - L_CMP) // STRIDE + 1
    ncp = nc + 1
    dc = (np.arange(t)[:, None] - (np.arange(ncp)[None, :] * STRIDE + L_CMP - 1)).astype(np.int32)
    return tab_s, tab_w, _bias_table(jnp.asarray(dc), rel_bias, tq)


def kernel(x_prompt, x_sample, c_prompt, c_sample, cache_cmp_kv, cache_sel_kv, state_win_kv, state_rwkv_shift,
           state_rwkv_wkv, page_table, rel_bias, norm_f, norm1, norm2, w_ada, b_ada, w_in, rwkv_mu, rwkv_w0, rwkv_w2,
           rwkv_a0, rwkv_a2, rwkv_g2, rwkv_kk, rwkv_ka, rwkv_rk, rwkv_ln_g, rwkv_ln_b, cmp_pos, cmp_w1, cmp_w2,
           w_o_rwkv, w_o_nsa, w_out, router_wg, router_bg, router_we, router_be, exp_w1, exp_w3, exp_w2):
    bp, t, _ = x_prompt.shape
    bs = x_sample.shape[0]
    mp = bp * t
    past = page_table.shape[1] * PAGE_SIZE

    nrow = -(-(bp + bs) // 8) * 8
    c_all = jnp.concatenate([c_prompt, c_sample, jnp.zeros((nrow - bp - bs, D_MODEL), F32)], axis=0)
    mod = _ada(c_all, w_ada[0], b_ada[0]).reshape(nrow, 6, D_MODEL)
    mod_p = [mod[:bp, i][:, None, :] for i in range(6)]
    mod_s = [mod[bp:bp + bs, i][None] for i in range(6)]

    w_r, w_n, w_gm = _pack_in_proj(w_in[0])
    rw = _rwkv_weights(rwkv_mu[0], rwkv_w0[0], rwkv_w2[0], rwkv_a0[0], rwkv_a2[0], rwkv_g2[0], rwkv_kk[0],
                       rwkv_ka[0], rwkv_rk[0], rwkv_ln_g[0], rwkv_ln_b[0])
    wbd = _cmp_weights(cmp_w1[0], N_KVG)
    cpos = _cmp_partial_rows(_cmp_pos_rows(cmp_pos[0]), wbd, 8)
    wo_r, wo_n, wo = w_o_rwkv[0].astype(BF), w_o_nsa[0].astype(BF), w_out[0].astype(BF)
    w_router = jnp.pad(jnp.concatenate([router_wg[0], router_we[0]], axis=1),
                       ((0, 0), (0, ROUTER_COLS - N_EGROUPS - N_EXP))).astype(BF)
    b_router = jnp.pad(jnp.concatenate([router_bg[0], router_be[0]]), (0, ROUTER_COLS - N_EGROUPS - N_EXP))[None]

    tm = 512
    xp = x_prompt.reshape(mp, D_MODEL)
    h = _norm_mod(xp, norm1[0], mod_p[0], mod_p[1], tm, t)
    p_r = _matmul(h, w_r, tm, C_RIN // 2)
    p_n = _matmul(h, w_n, tm, NP_COLS // 2)
    p_g = _matmul(h, w_gm, tm, 2048)
    o_r, shift_p, wkv_p = _rwkv_prompt(p_r.reshape(bp, t, C_RIN), rw)
    kvc = p_n[:, NP_KVC:NP_KVC + KV_COLS]
    kvs = p_n[:, NP_KVS:NP_KVS + KV_COLS]
    kvw = p_n[:, NP_KVW:NP_KVW + KV_COLS]
    nch = t // STRIDE
    c_part = _cmp_partial_rows(kvc.reshape(bp * nch, STRIDE * KV_COLS), wbd, nch)
    kv_cmp = _cmp_finish(c_part.reshape(bp, nch, -1), cpos, cmp_w2[0])
    tab_s, tab_w, bias_c = _prompt_bias_tables(rel_bias, t)
    o_cmp, sel = _cmp_attn_prompt(p_n, kv_cmp, bias_c, t)
    o_n = _swa_prompt(p_n, *_swa_operands(p_n, sel, bp, t), tab_s, tab_w, o_cmp, t)
    y = _merge(o_r.reshape(mp, C_R), o_n, wo_r, wo_n, p_g, tm)
    x1 = _proj_residual(y, wo, xp, mod_p[2], tm, t)
    y_prompt = _moe_and_final(x1, norm2[0], mod_p[3], mod_p[4], mod_p[5], w_router, b_router, exp_w1[0], exp_w3[0],
                              exp_w2[0], norm_f, tm, t, 128).reshape(bp, t, D_MODEL)
    kv_shape = (1, bp, t, 2, N_KVG, HD_N)
    wlen = min(WINDOW, t)
    win_p = kvw.reshape(bp, t, KV_COLS)[:, t - wlen:].reshape(1, bp, wlen, 2, N_KVG, HD_N)

    xs = x_sample.reshape(bs, D_MODEL)
    hs = _norm_mod(xs, norm1[0], mod_s[0], mod_s[1], bs, bs)
    ps_r = _matmul(hs, w_r, bs, C_RIN // 2)
    ps_n = _matmul(hs, w_n, bs, NP_COLS // 2)
    ps_g = _matmul(hs, w_gm, bs, 2048)
    os_r, wkv_s = _rwkv_step(ps_r, state_rwkv_shift[0], state_rwkv_wkv[0], rw)
    kvc_s = ps_n[:, NP_KVC:NP_KVC + KV_COLS]
    kvs_s = ps_n[:, NP_KVS:NP_KVS + KV_COLS]
    kvw_s = ps_n[:, NP_KVW:NP_KVW + KV_COLS]
    cs_part = _cmp_partial_paged(jnp.transpose(cache_cmp_kv[0], (0, 2, 3, 4, 1)), page_table,
                                 _cmp_weights(cmp_w1[0], CMP_PAIR))
    kv_cmp_s = _cmp_finish(cs_part, cpos, cmp_w2[0])
    q_s = ps_n[:, :C_N].reshape(bs, H_N, HD_N)
    o_cmp_s, picks = _sample_cmp(q_s, kv_cmp_s, rel_bias, past)
    idx = picks[:, :N_KVG, :N_TOP]
    bpp = PAGE_SIZE // L_SEL
    npb = past // L_SEL
    idc = jnp.minimum(idx, npb - 1)
    page = jnp.take_along_axis(page_table, (idc // bpp).reshape(bs, -1), axis=1).reshape(bs * N_KVG, N_TOP)
    cache_t = jnp.transpose(cache_sel_kv[0], (0, 2, 3, 4, 1))
    blocks = _gather_sel_pages(cache_t, page).reshape(bs, N_KVG, N_TOP, 2, HD_N, PAGE_SIZE)
    win_buf = state_win_kv[0].reshape(bs, -1, KV_COLS)
    gates_s = ps_n[:, NP_GN:].reshape(bs, N_KVG, GN_GROUP_COLS)[:, :, :3 * HPG].reshape(bs, H_N, 3)
    new_kv = jnp.stack([kvs_s, kvw_s], axis=1)
    os_n = _sample_swa(idx, q_s, blocks, win_buf, new_kv, rel_bias, gates_s, o_cmp_s, past)
    ys = _merge(os_r, os_n.reshape(bs, C_N).astype(BF), wo_r, wo_n, ps_g, bs)
    xs1 = _proj_residual(ys, wo, xs, mod_s[2], bs, bs)
    y_sample = _moe_and_final(xs1, norm2[0], mod_s[3], mod_s[4], mod_s[5], w_router, b_router, exp_w1[0], exp_w3[0],
                              exp_w2[0], norm_f, bs, bs, 16).reshape(bs, 1, D_MODEL)
    kv1 = (1, bs, 1, 2, N_KVG, HD_N)
    wbuf = win_buf.shape[1]
    win_s = jnp.concatenate([win_buf, kvw_s[:, None, :]], axis=1)[:, -wbuf:].reshape(1, bs, wbuf, 2, N_KVG, HD_N)

    return (y_prompt, y_sample,
            kvc.reshape(kv_shape), kvc_s.reshape(kv1),
            kvs.reshape(kv_shape), kvs_s.reshape(kv1),
            win_p, win_s,
            shift_p.reshape(1, bp, C_RIN), ps_r.reshape(1, bs, C_RIN),
            wkv_p[None], wkv_s[None])
```

```python
import functools
import math

import numpy as np
import jax
import jax.numpy as jnp
from jax import lax
from jax.experimental import pallas as pl
from jax.experimental.pallas import tpu as pltpu

D_MODEL = 2048
PAGE_SIZE = 128
H_R, HD_R = 16, 64
C_R = H_R * HD_R
LORA_W, LORA_A, LORA_G = 64, 64, 128
C_RIN = 3 * C_R + LORA_W + LORA_A + LORA_G
LN_X_EPS = 64e-5
H_N, HD_N, N_KVG = 16, 64, 4
HPG = H_N // N_KVG
C_N = H_N * HD_N
KV_COLS = 2 * N_KVG * HD_N
L_CMP, STRIDE, CMP_HID = 32, 16, 64
L_SEL, N_TOP, WINDOW = 64, 16, 512
N_BUCKETS, MAX_DIST = 32, 128
N_EGROUPS, EXP_PER_GROUP = 4, 8
N_EXP = N_EGROUPS * EXP_PER_GROUP
TOP_K, D_EXP = 2, 512
RMS_EPS = 1e-6
NEG_INF = -1e30
FORCE = 1e9

BF = jnp.bfloat16
F32 = jnp.float32
I32 = jnp.int32

VMEM_LIMIT_BYTES = 56 * 1024 * 1024
LANES = 128
RW_CHUNK = 32
ATT_TILE = 256
GN_GROUP_COLS = 128
NP_Q, NP_KVC, NP_KVS, NP_KVW, NP_GN = 0, C_N, C_N + KV_COLS, C_N + 2 * KV_COLS, C_N + 3 * KV_COLS
NP_COLS = NP_GN + N_KVG * GN_GROUP_COLS


def _cparams(*sem):
    return pltpu.CompilerParams(dimension_semantics=sem, vmem_limit_bytes=VMEM_LIMIT_BYTES)


def _dot(a, b):
    return jnp.dot(a.astype(BF), b.astype(BF), preferred_element_type=F32)


def _dot_nt(a, b):
    return lax.dot_general(a.astype(BF), b.astype(BF), (((1,), (1,)), ((), ())), preferred_element_type=F32)


def _dot_tn(a, b):
    return lax.dot_general(a.astype(BF), b.astype(BF), (((0,), (0,)), ((), ())), preferred_element_type=F32)


def _softplus(x):
    return jnp.maximum(x, 0.0) + jnp.log1p(jnp.exp(-jnp.abs(x)))


def _sigmoid(x):
    return 1.0 / (1.0 + jnp.exp(-x))


def _gelu_tanh(x):
    return 0.5 * x * (1.0 + jnp.tanh(math.sqrt(2.0 / math.pi) * (x + 0.044715 * x * x * x)))


def _t5_bucket(dist):
    n = jnp.maximum(dist, 0)
    max_exact = N_BUCKETS // 2
    nf = jnp.maximum(n, 1).astype(F32)
    large = max_exact + (jnp.log(nf / max_exact) / math.log(MAX_DIST / max_exact)
                         * (N_BUCKETS - max_exact)).astype(I32)
    large = jnp.minimum(large, N_BUCKETS - 1)
    return jnp.where(n < max_exact, n, large)


def _bias_rows(dist, rbt):
    bucket = _t5_bucket(dist)
    out = jnp.zeros((rbt.shape[0], dist.shape[1]), F32)
    for b in range(N_BUCKETS):
        out = jnp.where(bucket == b, rbt[:, b:b + 1], out)
    return out


def _ada_kernel(c_ref, w_ref, b_ref, o_ref):
    o_ref[...] = _dot(c_ref[...], w_ref[...]) + b_ref[...]


def _ada(c, w_ada, b_ada):
    r = c.shape[0]
    n = w_ada.shape[1]
    tn = 1024
    return pl.pallas_call(
        _ada_kernel,
        grid=(n // tn,),
        in_specs=[pl.BlockSpec((r, D_MODEL), lambda j: (0, 0)),
                  pl.BlockSpec((D_MODEL, tn), lambda j: (0, j)),
                  pl.BlockSpec((1, tn), lambda j: (0, j))],
        out_specs=pl.BlockSpec((r, tn), lambda j: (0, j)),
        out_shape=jax.ShapeDtypeStruct((r, n), F32),
        compiler_params=_cparams("arbitrary"),
        name="ada_mod",
    )(c, w_ada, b_ada.reshape(1, n))


def _rms(x, g):
    return x * lax.rsqrt(jnp.mean(x * x, axis=-1, keepdims=True) + RMS_EPS) * g


def _norm_mod_kernel(x_ref, g_ref, sh_ref, sc_ref, o_ref):
    o_ref[...] = (_rms(x_ref[...], g_ref[...]) * (1.0 + sc_ref[...]) + sh_ref[...]).astype(o_ref.dtype)


def _row_specs(m, tm, rpb):
    del m
    return (lambda tn: pl.BlockSpec((tm, tn), lambda i, j: (i, j)),
            lambda r, tn: pl.BlockSpec((None, r, tn), lambda i, j: ((i * tm) // rpb, 0, j)))


def _norm_mod(x, g, shift, scale, tm, rpb):
    m = x.shape[0]
    r = shift.shape[1]
    rows, mods = _row_specs(m, tm, rpb)
    return pl.pallas_call(
        _norm_mod_kernel,
        grid=(m // tm, 1),
        in_specs=[rows(D_MODEL), pl.BlockSpec((1, D_MODEL), lambda i, j: (0, 0)), mods(r, D_MODEL), mods(r, D_MODEL)],
        out_specs=rows(D_MODEL),
        out_shape=jax.ShapeDtypeStruct((m, D_MODEL), BF),
        compiler_params=_cparams("arbitrary", "arbitrary"),
        name="norm_mod",
    )(x, g.reshape(1, D_MODEL), shift, scale)


def _mm_kernel(a_ref, w_ref, o_ref):
    o_ref[...] = jnp.dot(a_ref[...], w_ref[...], preferred_element_type=F32).astype(o_ref.dtype)


def _matmul(a, w, tm, tn, out_dtype=F32):
    m, k = a.shape
    n = w.shape[1]
    return pl.pallas_call(
        _mm_kernel,
        grid=(m // tm, n // tn),
        in_specs=[pl.BlockSpec((tm, k), lambda i, j: (i, 0)), pl.BlockSpec((k, tn), lambda i, j: (0, j))],
        out_specs=pl.BlockSpec((tm, tn), lambda i, j: (i, j)),
        out_shape=jax.ShapeDtypeStruct((m, n), out_dtype),
        compiler_params=_cparams("arbitrary", "arbitrary"),
        name="matmul",
    )(a, w)


def _merge_kernel(or_ref, on_ref, wr_ref, wn_ref, g0_ref, g1_ref, o_ref):
    yr = jnp.dot(or_ref[...], wr_ref[...], preferred_element_type=F32)
    yn = jnp.dot(on_ref[...], wn_ref[...], preferred_element_type=F32)
    o_ref[...] = (_sigmoid(g0_ref[...]) * yr + _sigmoid(g1_ref[...]) * yn).astype(o_ref.dtype)


def _merge(o_r, o_n, w_r, w_n, p_g, tm):
    m = o_r.shape[0]
    tn = 1024
    nb = D_MODEL // tn
    return pl.pallas_call(
        _merge_kernel,
        grid=(m // tm, nb),
        in_specs=[pl.BlockSpec((tm, C_R), lambda i, j: (i, 0)), pl.BlockSpec((tm, C_N), lambda i, j: (i, 0)),
                  pl.BlockSpec((C_R, tn), lambda i, j: (0, j)), pl.BlockSpec((C_N, tn), lambda i, j: (0, j)),
                  pl.BlockSpec((tm, tn), lambda i, j: (i, j)), pl.BlockSpec((tm, tn), lambda i, j: (i, j + nb))],
        out_specs=pl.BlockSpec((tm, tn), lambda i, j: (i, j)),
        out_shape=jax.ShapeDtypeStruct((m, D_MODEL), BF),
        compiler_params=_cparams("arbitrary", "arbitrary"),
        name="merge_branches",
    )(o_r, o_n, w_r, w_n, p_g, p_g)


def _proj_res_kernel(y_ref, w_ref, x_ref, g_ref, o_ref):
    o_ref[...] = x_ref[...] + g_ref[...] * jnp.dot(y_ref[...], w_ref[...], preferred_element_type=F32)


def _proj_residual(y, w, x, gate, tm, rpb):
    m = y.shape[0]
    tn = 1024
    r = gate.shape[1]
    rows, mods = _row_specs(m, tm, rpb)
    return pl.pallas_call(
        _proj_res_kernel,
        grid=(m // tm, D_MODEL // tn),
        in_specs=[pl.BlockSpec((tm, D_MODEL), lambda i, j: (i, 0)), pl.BlockSpec((D_MODEL, tn), lambda i, j: (0, j)),
                  rows(tn), mods(r, tn)],
        out_specs=rows(tn),
        out_shape=jax.ShapeDtypeStruct((m, D_MODEL), F32),
        compiler_params=_cparams("arbitrary", "arbitrary"),
        name="out_proj_residual",
    )(y, w, x, gate)


def _rwkv_features(p, p_prev, mu, w0, w2, a0, a2, g2, k_k, k_a):
    xm = p + (p_prev - p) * mu
    r = xm[:, :C_R]
    k = xm[:, C_R:2 * C_R]
    v = xm[:, 2 * C_R:3 * C_R]
    o = 3 * C_R
    wd = xm[:, o:o + LORA_W]
    ad = xm[:, o + LORA_W:o + LORA_W + LORA_A]
    gd = xm[:, o + LORA_W + LORA_A:]
    w_log = -_softplus(-(w0 + _dot(jnp.tanh(wd), w2))) - 0.5
    lw = -jnp.exp(w_log)
    a = _sigmoid(a0 + _dot(ad, a2))
    g = _dot(_sigmoid(gd), g2)
    kk = k * k_k
    k = k * (1.0 + (a - 1.0) * k_a)
    return r, k, v, lw, a, g, kk


def _head_sums(x, ones2):
    nt = x.shape[1] // LANES
    xs = jnp.concatenate([x[:, j * LANES:(j + 1) * LANES] for j in range(nt)], axis=0)
    hi = xs.astype(BF)
    lo = (xs - hi.astype(F32)).astype(BF)
    s = jnp.dot(hi, ones2, preferred_element_type=F32) + jnp.dot(lo, ones2, preferred_element_type=F32)
    r = x.shape[0]
    return jnp.concatenate([s[j * r:(j + 1) * r] for j in range(nt)], axis=1)


def _rwkv_chunk_kernel(pr_ref, mu_ref, w0_ref, w2_ref, a0_ref, a2_ref, g2_ref, kk_ref, ka_ref, rk_ref,
                       lng_ref, lnb_ref, ones_ref, o_ref, shift_ref, state_ref, y_ref):
    c = pl.program_id(1)
    C = RW_CHUNK

    @pl.when(c == 0)
    def _():
        shift_ref[...] = jnp.zeros_like(shift_ref)
        state_ref[...] = jnp.zeros_like(state_ref)

    p = pr_ref[...]
    row = lax.broadcasted_iota(I32, (C, 1), 0)
    p_prev = jnp.where(row == 0, shift_ref[...], pltpu.roll(p, 1, axis=0))
    shift_ref[...] = p[C - 1:C, :]
    r, k, v, lw, a, g, kk_all = _rwkv_features(p, p_prev, mu_ref[...], w0_ref[...], w2_ref[...], a0_ref[...],
                                               a2_ref[...], g2_ref[...], kk_ref[...], ka_ref[...])
    cl = lw
    s = 1
    while s < C:
        cl = cl + jnp.where(row >= s, pltpu.roll(cl, s, axis=0), 0.0)
        s *= 2
    ti = lax.broadcasted_iota(I32, (C, C), 0)
    si = lax.broadcasted_iota(I32, (C, C), 1)
    strict = ti > si
    incl = ti >= si
    eye = (ti == si).astype(F32)
    heads = range(H_R)
    sls = [slice(h * HD_R, (h + 1) * HD_R) for h in heads]
    ones2 = ones_ref[...]
    kk_n = kk_all / jnp.maximum(jnp.sqrt(_head_sums(kk_all * kk_all, ones2)), 1e-12)
    b_all = kk_n * a
    cl_end = cl[C - 1:C, :]
    e_neg = jnp.exp(-cl)
    e_end = jnp.exp(cl_end - cl)
    g_end_all = jnp.exp(cl_end)
    per_head = lambda z: [z[:, sl] for sl in sls]
    kkt_all = (kk_n * jnp.exp(cl - lw)).astype(BF)
    rt_all = (r * jnp.exp(cl)).astype(BF)
    kr = per_head(jnp.concatenate([kkt_all, rt_all], axis=0))
    kh = per_head((k * e_neg).astype(BF))
    bh = per_head((b_all * e_neg).astype(BF))
    kbb = per_head(jnp.concatenate([(k * e_end).astype(BF), (b_all * e_end).astype(BF)], axis=0))
    vb = per_head(v.astype(BF))
    g_end = per_head(g_end_all)
    ak = [_dot_nt(kr[h], kh[h]) for h in heads]
    ab = [_dot_nt(kr[h], bh[h]) for h in heads]
    lg = [jnp.concatenate([jnp.where(strict, ak[h][:C], 0.0), jnp.where(incl, ak[h][C:], 0.0)], axis=0).astype(BF)
          for h in heads]
    nil = [jnp.where(strict, -ab[h][:C], 0.0) for h in heads]
    grb = [jnp.where(incl, ab[h][C:], 0.0).astype(BF) for h in heads]
    tinv = [eye + n for n in nil]
    m = 2
    while m < C:
        nil = [_dot(n, n) for n in nil]
        tinv = [t + _dot(t, n) for t, n in zip(tinv, nil)]
        m *= 2
    s0 = [state_ref[h] for h in heads]
    xy = [_dot_nt(kr[h], s0[h]) + _dot(lg[h], vb[h]) for h in heads]
    u = [_dot(tinv[h], xy[h][:C]).astype(BF) for h in heads]
    y = [xy[h][C:] - _dot(grb[h], u[h]) for h in heads]
    for h in heads:
        vu = jnp.concatenate([vb[h], -u[h]], axis=0)
        state_ref[h] = s0[h] * g_end[h] + _dot_tn(vu, kbb[h])
    for h, sl in enumerate(sls):
        y_ref[:, sl] = y[h]
    y_all = y_ref[...]
    yc = y_all - _head_sums(y_all, ones2) * (1.0 / HD_R)
    var = _head_sums(yc * yc, ones2) * (1.0 / HD_R)
    yn = yc * lax.rsqrt(var + LN_X_EPS) * lng_ref[...] + lnb_ref[...]
    bonus = _head_sums(r * k * rk_ref[...], ones2) * v
    o_ref[...] = ((yn + bonus) * g).astype(o_ref.dtype)


def _rwkv_weights(mu, w0, w2, a0, a2, g2, k_k, k_a, r_k, ln_g, ln_b):
    row = lambda z: z.reshape(1, -1).astype(F32)
    return (row(mu), row(w0), w2.astype(BF), row(a0), a2.astype(BF), g2.astype(BF), row(k_k), row(k_a), row(r_k),
            row(ln_g), row(ln_b))


_RWKV_W_SHAPES = ((1, C_RIN), (1, C_R), (LORA_W, C_R), (1, C_R), (LORA_A, C_R), (LORA_G, C_R), (1, C_R), (1, C_R),
                  (1, C_R), (1, C_R), (1, C_R))


def _rwkv_prompt(pr, rw):
    b, t, _ = pr.shape
    C = RW_CHUNK
    full = lambda shp: pl.BlockSpec(shp, lambda i, j: (0,) * len(shp))
    lane_head = np.arange(LANES) // HD_R
    ones2 = jnp.asarray(lane_head[:, None] == lane_head[None, :], BF)
    return pl.pallas_call(
        _rwkv_chunk_kernel,
        grid=(b, t // C),
        in_specs=[pl.BlockSpec((None, C, C_RIN), lambda i, j: (i, j, 0))] + [full(s) for s in _RWKV_W_SHAPES]
        + [full((LANES, LANES))],
        out_specs=[pl.BlockSpec((None, C, C_R), lambda i, j: (i, j, 0)),
                   pl.BlockSpec((None, 1, C_RIN), lambda i, j: (i, 0, 0)),
                   pl.BlockSpec((None, H_R, HD_R, HD_R), lambda i, j: (i, 0, 0, 0))],
        out_shape=[jax.ShapeDtypeStruct((b, t, C_R), BF),
                   jax.ShapeDtypeStruct((b, 1, C_RIN), F32),
                   jax.ShapeDtypeStruct((b, H_R, HD_R, HD_R), F32)],
        scratch_shapes=[pltpu.VMEM((C, C_R), F32)],
        compiler_params=_cparams("arbitrary", "arbitrary"),
        name="rwkv_chunk",
    )(pr, *rw, ones2)


def _rwkv_step_kernel(pr_ref, prev_ref, s0_ref, mu_ref, w0_ref, w2_ref, a0_ref, a2_ref, g2_ref, kk_ref, ka_ref,
                      rk_ref, lng_ref, lnb_ref, o_ref, state_ref):
    nb = pr_ref.shape[0]
    r, k, v, lw, a, g, kk_all = _rwkv_features(pr_ref[...], prev_ref[...], mu_ref[...], w0_ref[...], w2_ref[...],
                                               a0_ref[...], a2_ref[...], g2_ref[...], kk_ref[...], ka_ref[...])
    decay = jnp.exp(lw)
    ii = lax.broadcasted_iota(I32, (HD_R, HD_R), 0)
    jj = lax.broadcasted_iota(I32, (HD_R, HD_R), 1)
    eye = ii == jj
    col = lambda z: jnp.sum(jnp.where(eye, z, 0.0), axis=1, keepdims=True)
    heads = range(H_R)
    sls = [slice(h * HD_R, (h + 1) * HD_R) for h in heads]
    for bi in range(nb):
        rows = lambda z: [z[bi:bi + 1, sl] for sl in sls]
        r_h, k_h, v_h, a_h, w_h, g_h, kk_h = rows(r), rows(k), rows(v), rows(a), rows(decay), rows(g), rows(kk_all)
        nrm = [jnp.maximum(jnp.sqrt(jnp.sum(z * z, axis=-1, keepdims=True)), 1e-12) for z in kk_h]
        kk_h = [kk_h[h] / nrm[h] for h in heads]
        s0 = [s0_ref[bi, h] for h in heads]
        sa = [jnp.sum(s0[h] * (-kk_h[h]), axis=1, keepdims=True) for h in heads]
        v_col = [col(z) for z in v_h]
        s1 = [s0[h] * w_h[h] + sa[h] * (kk_h[h] * a_h[h]) + v_col[h] * k_h[h] for h in heads]
        for h in heads:
            state_ref[bi, h] = s1[h]
        y_col = [jnp.sum(s1[h] * r_h[h], axis=1, keepdims=True) for h in heads]
        y = [jnp.sum(jnp.where(eye, z, 0.0), axis=0, keepdims=True) for z in y_col]
        yc = [z - jnp.mean(z, axis=-1, keepdims=True) for z in y]
        var = [jnp.mean(z * z, axis=-1, keepdims=True) for z in yc]
        bonus = [jnp.sum(r_h[h] * k_h[h] * rk_ref[:, sls[h]], axis=-1, keepdims=True) * v_h[h] for h in heads]
        for h, sl in enumerate(sls):
            yn = yc[h] * lax.rsqrt(var[h] + LN_X_EPS) * lng_ref[:, sl] + lnb_ref[:, sl]
            o_ref[bi:bi + 1, sl] = ((yn + bonus[h]) * g_h[h]).astype(o_ref.dtype)


def _rwkv_step(pr, prev, s0, rw):
    b = pr.shape[0]
    return pl.pallas_call(
        _rwkv_step_kernel,
        out_shape=[jax.ShapeDtypeStruct((b, C_R), BF), jax.ShapeDtypeStruct((b, H_R, HD_R, HD_R), F32)],
        compiler_params=pltpu.CompilerParams(vmem_limit_bytes=VMEM_LIMIT_BYTES),
        name="rwkv_step",
    )(pr, prev, s0, *rw)


def _cmp_partial_kernel(*refs):
    x_refs, w_ref, o_ref = refs[:-2], refs[-2], refs[-1]
    x = x_refs[0][...] if len(x_refs) == 1 else jnp.concatenate([r[...] for r in x_refs], axis=0)
    half = N_KVG * HD_N
    for s in range(2):
        acc = jnp.zeros((x.shape[0], N_KVG * 2 * CMP_HID), F32)
        for p in range(STRIDE):
            o = p * KV_COLS + s * half
            acc = acc + _dot(x[:, o:o + half], w_ref[p, s])
        o_ref[:, s * N_KVG * 2 * CMP_HID:(s + 1) * N_KVG * 2 * CMP_HID] = acc


CMP_PAIR = 2


def _cmp_partial_paged_kernel(pt_ref, *refs):
    del pt_ref
    x_refs, perm_ref, w_ref, o_ref = refs[:-3], refs[-3], refs[-2], refs[-1]
    cpp = PAGE_SIZE // STRIDE
    width = CMP_PAIR * 2 * CMP_HID
    perm = perm_ref[...]
    for s in range(2):
        for gp in range(N_KVG // CMP_PAIR):
            rows = []
            for x_ref in x_refs:
                tile = jnp.concatenate([x_ref[s, CMP_PAIR * gp + j] for j in range(CMP_PAIR)], axis=0)
                rows.append(_dot_nt(perm, tile))
            acc = jnp.zeros((len(x_refs) * cpp, width), F32)
            for p in range(STRIDE):
                lhs = jnp.concatenate([r[p * cpp:(p + 1) * cpp] for r in rows], axis=0)
                acc = acc + _dot(lhs, w_ref[p, s])
            o = (s * (N_KVG // CMP_PAIR) + gp) * width
            o_ref[:, o:o + width] = acc


def _cmp_weights(cmp_w1, groups):
    w1r = cmp_w1.reshape(2, 2, STRIDE, HD_N, CMP_HID)
    w = jnp.transpose(w1r, (2, 0, 3, 1, 4))
    w = w.reshape(STRIDE, 2, 1, HD_N, 1, 2 * CMP_HID)
    eye = jnp.eye(groups, dtype=w.dtype).reshape(1, 1, groups, 1, groups, 1)
    wbd = eye * w
    return wbd.reshape(STRIDE, 2, groups * HD_N, groups * 2 * CMP_HID).astype(BF)


def _cmp_partial_rows(x, wbd, tr):
    r = x.shape[0]
    n = 2 * N_KVG * 2 * CMP_HID
    return pl.pallas_call(
        _cmp_partial_kernel,
        grid=(r // tr,),
        in_specs=[pl.BlockSpec((tr, STRIDE * KV_COLS), lambda i: (i, 0)),
                  pl.BlockSpec(wbd.shape, lambda i: (0, 0, 0, 0))],
        out_specs=pl.BlockSpec((tr, n), lambda i: (i, 0)),
        out_shape=jax.ShapeDtypeStruct((r, n), F32),
        compiler_params=_cparams("arbitrary"),
        name="cmp_partial",
    )(x, wbd)


PAGES_PER_STEP = 16


def _cmp_partial_paged(cache_t, page_table, wpair):
    b, npg = page_table.shape
    cpp = PAGE_SIZE // STRIDE
    n = 2 * N_KVG * 2 * CMP_HID
    steps = npg // PAGES_PER_STEP
    perm = np.zeros((PAGE_SIZE, PAGE_SIZE), np.float32)
    tok = np.arange(PAGE_SIZE)
    perm[(tok % STRIDE) * cpp + tok // STRIDE, tok] = 1.0

    def page_spec(kpg):
        return pl.BlockSpec((None, 2, N_KVG, HD_N, PAGE_SIZE),
                            lambda i, j, pt: (pt[i, j * PAGES_PER_STEP + kpg], 0, 0, 0, 0))

    grid_spec = pltpu.PrefetchScalarGridSpec(
        num_scalar_prefetch=1,
        grid=(b, steps),
        in_specs=[page_spec(kpg) for kpg in range(PAGES_PER_STEP)]
        + [pl.BlockSpec((PAGE_SIZE, PAGE_SIZE), lambda i, j, pt: (0, 0)),
           pl.BlockSpec(wpair.shape, lambda i, j, pt: (0, 0, 0, 0))],
        out_specs=pl.BlockSpec((None, PAGES_PER_STEP * cpp, n), lambda i, j, pt: (i, j, 0)),
    )
    return pl.pallas_call(
        _cmp_partial_paged_kernel,
        grid_spec=grid_spec,
        out_shape=jax.ShapeDtypeStruct((b, npg * cpp, n), F32),
        compiler_params=_cparams("arbitrary", "arbitrary"),
        name="cmp_partial_paged",
    )(page_table, *([cache_t] * PAGES_PER_STEP), jnp.asarray(perm, BF), wpair)


def _cmp_finish_kernel(c_ref, cpos_ref, w2_ref, o_ref):
    c = c_ref[...]
    nrow = c.shape[0]
    c_next = pltpu.roll(c, nrow - 1, axis=0)
    for s in range(2):
        for gi in range(N_KVG):
            o = (s * N_KVG + gi) * 2 * CMP_HID
            hid = (c[:, o:o + CMP_HID] + cpos_ref[0:1, o:o + CMP_HID]
                   + c_next[:, o + CMP_HID:o + 2 * CMP_HID] + cpos_ref[1:2, o + CMP_HID:o + 2 * CMP_HID])
            oo = (s * N_KVG + gi) * HD_N
            o_ref[:, oo:oo + HD_N] = _dot(_gelu_tanh(hid), w2_ref[s])


def _cmp_finish(c, cpos, w2):
    b, nch, n = c.shape
    return pl.pallas_call(
        _cmp_finish_kernel,
        grid=(b,),
        in_specs=[pl.BlockSpec((None, nch, n), lambda i: (i, 0, 0)), pl.BlockSpec(cpos.shape, lambda i: (0, 0)),
                  pl.BlockSpec(w2.shape, lambda i: (0, 0, 0))],
        out_specs=pl.BlockSpec((None, nch, KV_COLS), lambda i: (i, 0, 0)),
        out_shape=jax.ShapeDtypeStruct((b, nch, KV_COLS), F32),
        compiler_params=_cparams("arbitrary"),
        name="cmp_finish",
    )(c, cpos, w2.astype(BF))


def _cmp_pos_rows(cmp_pos):
    pos = cmp_pos.reshape(2, STRIDE, 1, 1, HD_N)
    rows = jnp.broadcast_to(pos, (2, STRIDE, 2, N_KVG, HD_N)).reshape(2, STRIDE * KV_COLS)
    return jnp.concatenate([rows, jnp.zeros((6, STRIDE * KV_COLS), F32)], axis=0)


def _bias_table_kernel(dist_ref, rb_ref, o_ref):
    bucket = _t5_bucket(dist_ref[...])
    for h in range(H_N):
        out = jnp.zeros(bucket.shape, F32)
        for b in range(N_BUCKETS):
            out = jnp.where(bucket == b, rb_ref[b, h], out)
        o_ref[h] = out


def _bias_table(dist, rel_bias, tr):
    r, n = dist.shape
    return pl.pallas_call(
        _bias_table_kernel,
        grid=(r // tr,),
        in_specs=[pl.BlockSpec((tr, n), lambda i: (i, 0)),
                  pl.BlockSpec(memory_space=pltpu.SMEM)],
        out_specs=pl.BlockSpec((H_N, tr, n), lambda i: (0, i, 0)),
        out_shape=jax.ShapeDtypeStruct((H_N, r, n), F32),
        compiler_params=_cparams("arbitrary"),
        name="bias_table",
    )(dist, rel_bias)


def _softmax_rows(logits, valid):
    lm = jnp.where(valid, logits, NEG_INF)
    e = jnp.exp(lm - jnp.max(lm, axis=-1, keepdims=True))
    return e / jnp.sum(e, axis=-1, keepdims=True)


def _cmp_attn_kernel(q_ref, kv_ref, bias_ref, ovt_ref, o_ref, sel_ref):
    tq = q_ref.shape[0]
    ncp = kv_ref.shape[0]
    nsb = ovt_ref.shape[0]
    q0 = pl.program_id(1) * tq
    qpos = q0 + lax.broadcasted_iota(I32, (tq, 1), 0)
    cend = lax.broadcasted_iota(I32, (1, ncp), 1) * STRIDE + (L_CMP - 1)
    valid = (qpos >= cend) & (lax.broadcasted_iota(I32, (1, ncp), 1) < ncp - 1)
    validf = valid.astype(F32)
    q = q_ref[...] * (HD_N ** -0.5)
    blk = lax.broadcasted_iota(I32, (nsb, tq), 0)
    cur = (q0 + lax.broadcasted_iota(I32, (1, tq), 1)) // L_SEL
    forced = (blk == 0) | (blk == cur) | (blk == cur - 1)
    future = blk > cur
    for gi in range(N_KVG):
        kc = kv_ref[:, gi * HD_N:(gi + 1) * HD_N]
        vc = kv_ref[:, (N_KVG + gi) * HD_N:(N_KVG + gi + 1) * HD_N]
        pcs = jnp.zeros((tq, ncp), F32)
        for hl in range(HPG):
            h = gi * HPG + hl
            sl = slice(h * HD_N, (h + 1) * HD_N)
            pc = _softmax_rows(_dot_nt(q[:, sl], kc) + bias_ref[h], valid) * validf
            pcs = pcs + pc
            o_ref[:, sl] = _dot(pc, vc)
        imp = _dot_nt(ovt_ref[...], pcs)
        score = jnp.where(forced, FORCE, jnp.where(future, -FORCE, imp))
        rank = jnp.zeros((nsb, tq), F32)
        for i in range(nsb):
            si = score[i:i + 1, :]
            rank = rank + ((si > score) | ((si == score) & (i < blk))).astype(F32)
        sel_ref[gi] = (rank < N_TOP).astype(sel_ref.dtype)


def _cmp_sel_overlap_t(nc, ncp, nsb):
    s = np.arange(nc)[None, :] * STRIDE
    j = np.arange(nsb)[:, None] * L_SEL
    ov = np.clip(np.minimum(s + L_CMP, j + L_SEL) - np.maximum(s, j), 0, None) / L_CMP
    return np.pad(ov, ((0, 0), (0, ncp - nc))).astype(np.float32)


def _cmp_attn_prompt(p_n, kv_cmp, bias_c, t):
    b = kv_cmp.shape[0]
    ncp = kv_cmp.shape[1]
    nsb = t // L_SEL
    tq = ATT_TILE
    nqt = t // tq
    ovt = jnp.asarray(_cmp_sel_overlap_t(ncp - 1, ncp, nsb), BF)
    return pl.pallas_call(
        _cmp_attn_kernel,
        grid=(b, nqt),
        in_specs=[pl.BlockSpec((tq, C_N), lambda i, j: (i * nqt + j, 0)),
                  pl.BlockSpec((None, ncp, KV_COLS), lambda i, j: (i, 0, 0)),
                  pl.BlockSpec((H_N, tq, ncp), lambda i, j: (0, j, 0)),
                  pl.BlockSpec((nsb, ncp), lambda i, j: (0, 0))],
        out_specs=[pl.BlockSpec((tq, C_N), lambda i, j: (i * nqt + j, 0)),
                   pl.BlockSpec((None, N_KVG, nsb, tq), lambda i, j: (i, 0, 0, j))],
        out_shape=[jax.ShapeDtypeStruct((b * t, C_N), F32), jax.ShapeDtypeStruct((b, N_KVG, nsb, t), BF)],
        compiler_params=_cparams("arbitrary", "arbitrary"),
        name="nsa_cmp_select",
    )(p_n, kv_cmp, bias_c, ovt)


ATT_TK = 128
ATT_R = ATT_TILE // ATT_TK
N_SEL_OFFS = ATT_R + (MAX_DIST + ATT_TK - 1) // ATT_TK + 1
N_WIN_OFFS = ATT_R + WINDOW // ATT_TK
QA_COLS = HD_N + 32
SEL_STEP_TILES = 4
WIN_STEP_TILES = 2


def _swa_kernel(qa_ref, ks_ref, vs_ref, kw_ref, vw_ref, tabs_ref, tabw_ref, gn_ref, oc_ref, o_ref):
    tq = qa_ref.shape[1]
    qt = pl.program_id(2)
    top = ATT_R * qt + ATT_R - 1

    heads = range(HPG)

    def attend(k_ref, v_ref, tab_ref, lo, n_off, step_tiles):
        tiles = range(step_tiles)

        def body(kp, carry):
            ki = [lo + step_tiles * kp + j for j in tiles]
            kt = [k_ref[jnp.minimum(i, top)] for i in ki]
            vt = [v_ref[jnp.minimum(i, top)] for i in ki]
            off = [jnp.where(i > top, n_off, jnp.minimum(top - i, n_off - 1)) for i in ki]
            s = [[lax.dot_general(kt[j], qa_ref[hl], (((1,), (1,)), ((), ())), preferred_element_type=F32)
                  + tab_ref[off[j], hl * ATT_TK:(hl + 1) * ATT_TK, :] for j in tiles] for hl in heads]
            m_new = [functools.reduce(jnp.maximum, [carry[hl][0]] + [jnp.max(s[hl][j], axis=0, keepdims=True)
                                                                    for j in tiles]) for hl in heads]
            alpha = [jnp.exp(carry[hl][0] - m_new[hl]) for hl in heads]
            p = [[jnp.exp(s[hl][j] - m_new[hl]) for j in tiles] for hl in heads]
            l = [alpha[hl] * carry[hl][1] + sum(jnp.sum(p[hl][j], axis=0, keepdims=True) for j in tiles)
                 for hl in heads]
            acc = [alpha[hl] * carry[hl][2] + sum(jnp.dot(vt[j], p[hl][j].astype(BF), preferred_element_type=F32)
                                                  for j in tiles) for hl in heads]
            return tuple((m_new[hl], l[hl], acc[hl]) for hl in heads)

        init = tuple((jnp.full((1, tq), NEG_INF, F32), jnp.zeros((1, tq), F32), jnp.zeros((HD_N, tq), F32))
                     for _ in heads)
        res = lax.fori_loop(0, (top - lo) // step_tiles + 1, body, init)
        return [acc / l for _, l, acc in res]

    o_sel = attend(ks_ref, vs_ref, tabs_ref, 0, N_SEL_OFFS, SEL_STEP_TILES)
    o_win = attend(kw_ref, vw_ref, tabw_ref, jnp.maximum(top + 1 - N_WIN_OFFS, 0), N_WIN_OFFS, WIN_STEP_TILES)
    gates = _sigmoid(gn_ref[...])
    gates_t = gates.T
    for hl in heads:
        sl = slice(hl * HD_N, (hl + 1) * HD_N)
        o_t = gates_t[3 * hl + 1:3 * hl + 2, :] * o_sel[hl] + gates_t[3 * hl + 2:3 * hl + 3, :] * o_win[hl]
        o_ref[:, sl] = (gates[:, 3 * hl:3 * hl + 1] * oc_ref[:, sl] + o_t.T).astype(o_ref.dtype)


def _swa_prompt(p_n, qa, ks, vs, kw, vw, tab_s, tab_w, o_cmp, t):
    b = qa.shape[0]
    tq = ATT_TILE
    nqt = t // tq
    nkt = t // ATT_TK
    gw = HPG * HD_N
    k_spec = pl.BlockSpec((None, None, nkt, ATT_TK, QA_COLS), lambda i, g, j: (i, g, 0, 0, 0))
    v_spec = pl.BlockSpec((None, None, nkt, HD_N, ATT_TK), lambda i, g, j: (i, g, 0, 0, 0))
    tab_spec = lambda n: pl.BlockSpec((n + 1, HPG * ATT_TK, tq), lambda i, g, j: (0, g, 0))
    return pl.pallas_call(
        _swa_kernel,
        grid=(b, N_KVG, nqt),
        in_specs=[pl.BlockSpec((None, HPG, tq, QA_COLS), lambda i, g, j: (i, g, j, 0)),
                  k_spec, v_spec, k_spec, v_spec, tab_spec(N_SEL_OFFS), tab_spec(N_WIN_OFFS),
                  pl.BlockSpec((tq, GN_GROUP_COLS), lambda i, g, j: (i * nqt + j, NP_GN // GN_GROUP_COLS + g)),
                  pl.BlockSpec((tq, gw), lambda i, g, j: (i * nqt + j, g))],
        out_specs=pl.BlockSpec((tq, gw), lambda i, g, j: (i * nqt + j, g)),
        out_shape=jax.ShapeDtypeStruct((b * t, C_N), BF),
        compiler_params=_cparams("arbitrary", "arbitrary", "arbitrary"),
        name="nsa_sel_win",
    )(qa, ks, vs, kw, vw, tab_s, tab_w, p_n, o_cmp)


def _swa_operands(p_n, sel, b, t):
    nsb = t // L_SEL
    nkt = t // ATT_TK
    q = (p_n[:, :C_N] * (HD_N ** -0.5)).reshape(b, t, H_N, HD_N).transpose(0, 2, 1, 3)
    pen = jnp.where(jnp.transpose(sel, (0, 1, 3, 2)) > 0.5, 0.0, NEG_INF).astype(F32)
    pen = jnp.broadcast_to(pen[:, :, None], (b, N_KVG, HPG, t, nsb)).reshape(b, H_N, t, nsb)
    qa = jnp.concatenate([q, pen], axis=-1).astype(BF)

    def split(cols):
        kv = p_n[:, cols:cols + KV_COLS].reshape(b, t, 2, N_KVG, HD_N)
        k = jnp.transpose(kv[:, :, 0], (0, 2, 1, 3))
        v = jnp.transpose(kv[:, :, 1], (0, 2, 3, 1)).reshape(b, N_KVG, HD_N, nkt, ATT_TK)
        return k, jnp.transpose(v, (0, 1, 3, 2, 4)).astype(BF)

    onehot = jnp.asarray(np.arange(t)[:, None] // L_SEL == np.arange(nsb)[None, :], F32)
    k_s, v_s = split(NP_KVS)
    k_w, v_w = split(NP_KVW)
    ext = lambda k, e: jnp.concatenate([k, jnp.broadcast_to(e, (b, N_KVG, t, nsb))], axis=-1).astype(BF) \
        .reshape(b, N_KVG, nkt, ATT_TK, QA_COLS)
    return qa, ext(k_s, onehot), v_s, ext(k_w, jnp.zeros((t, nsb), F32)), v_w


def _sample_cmp_kernel(q_ref, kv_ref, rbt_ref, ov_ref, o_ref, idx_ref, *, past):
    ncp = kv_ref.shape[0]
    nsbp = ov_ref.shape[1]
    nsb = -(-(past + 1) // L_SEL)
    q = q_ref[...] * (HD_N ** -0.5)
    hrow = lax.broadcasted_iota(I32, (H_N, 1), 0)
    nidx = lax.broadcasted_iota(I32, (1, ncp), 1)
    valid = nidx < ncp - 1
    bias = _bias_rows(past - (nidx * STRIDE + (L_CMP - 1)), rbt_ref[...])
    logits = jnp.zeros((H_N, ncp), F32)
    for gi in range(N_KVG):
        lg = _dot_nt(q, kv_ref[:, gi * HD_N:(gi + 1) * HD_N])
        logits = jnp.where(hrow // HPG == gi, lg, logits)
    pc = _softmax_rows(logits + bias, valid) * valid.astype(F32)
    o = jnp.zeros((H_N, HD_N), F32)
    for gi in range(N_KVG):
        og = _dot(pc, kv_ref[:, (N_KVG + gi) * HD_N:(N_KVG + gi + 1) * HD_N])
        o = jnp.where(hrow // HPG == gi, og, o)
    o_ref[...] = o
    imp_h = _dot(pc, ov_ref[...])
    blk = lax.broadcasted_iota(I32, (8, nsbp), 1)
    grow = lax.broadcasted_iota(I32, (8, 1), 0)
    cur = past // L_SEL
    score = jnp.full((8, nsbp), -3e38, F32)
    for gi in range(N_KVG):
        imp = jnp.sum(jnp.where(hrow // HPG == gi, imp_h, 0.0), axis=0, keepdims=True)
        score = jnp.where(grow == gi, imp, score)
    forced = (blk == 0) | (blk == cur) | (blk == cur - 1)
    score = jnp.where(forced, FORCE, jnp.where(blk > cur, -FORCE, score))
    score = jnp.where((blk < nsb) & (grow < N_KVG), score, -3e38)
    lane = lax.broadcasted_iota(I32, (8, LANES), 1)
    picks = jnp.zeros((8, LANES), I32)
    for it in range(N_TOP):
        mx = jnp.max(score, axis=-1, keepdims=True)
        pick = jnp.min(jnp.where(score == mx, blk, nsbp), axis=-1, keepdims=True)
        picks = jnp.where(lane == it, pick, picks)
        score = jnp.where(blk == pick, -3e38, score)
    idx_ref[...] = picks


def _sample_cmp(q, kv_cmp, rel_bias, past):
    b, ncp, _ = kv_cmp.shape
    nc = ncp - 1
    nsb = -(-(past + 1) // L_SEL)
    nsbp = -(-nsb // LANES) * LANES
    s = np.arange(nc)[:, None] * STRIDE
    j = np.arange(nsb)[None, :] * L_SEL
    ov = np.clip(np.minimum(s + L_CMP, j + L_SEL) - np.maximum(s, j), 0, None) / L_CMP
    ov = np.pad(ov, ((0, ncp - nc), (0, nsbp - nsb))).astype(np.float32)
    return pl.pallas_call(
        functools.partial(_sample_cmp_kernel, past=past),
        grid=(b,),
        in_specs=[pl.BlockSpec((None, H_N, HD_N), lambda i: (i, 0, 0)),
                  pl.BlockSpec((None, ncp, KV_COLS), lambda i: (i, 0, 0)),
                  pl.BlockSpec((H_N, N_BUCKETS), lambda i: (0, 0)),
                  pl.BlockSpec((ncp, nsbp), lambda i: (0, 0))],
        out_specs=[pl.BlockSpec((None, H_N, HD_N), lambda i: (i, 0, 0)),
                   pl.BlockSpec((None, 8, LANES), lambda i: (i, 0, 0))],
        out_shape=[jax.ShapeDtypeStruct((b, H_N, HD_N), F32), jax.ShapeDtypeStruct((b, 8, LANES), I32)],
        compiler_params=_cparams("arbitrary"),
        name="nsa_sample_cmp_select",
    )(q, kv_cmp, rel_bias.T, jnp.asarray(ov, BF))


def _block_copy_kernel(pg_ref, *refs):
    del pg_ref
    x_refs, o_ref = refs[:-1], refs[-1]
    for n, x_ref in enumerate(x_refs):
        o_ref[n] = x_ref[...]


def _gather_sel_pages(cache_t, page):
    rows, nslot = page.shape

    def slot_spec(n):
        return pl.BlockSpec((None, 2, None, HD_N, PAGE_SIZE), lambda i, pg: (pg[i, n], 0, i % N_KVG, 0, 0))

    return pl.pallas_call(
        _block_copy_kernel,
        grid_spec=pltpu.PrefetchScalarGridSpec(
            num_scalar_prefetch=1, grid=(rows,),
            in_specs=[slot_spec(n) for n in range(nslot)],
            out_specs=pl.BlockSpec((None, nslot, 2, HD_N, PAGE_SIZE), lambda i, pg: (i, 0, 0, 0, 0))),
        out_shape=jax.ShapeDtypeStruct((rows, nslot, 2, HD_N, PAGE_SIZE), cache_t.dtype),
        compiler_params=_cparams("arbitrary"),
        name="gather_sel_pages",
    )(page, *([cache_t] * nslot))


def _sample_swa_kernel(idx_ref, q_ref, blk_ref, win_ref, new_ref, rbt_ref, gate_ref, oc_ref, o_ref, *, past):
    bi = pl.program_id(0)
    q = q_ref[...] * (HD_N ** -0.5)
    rbt = rbt_ref[...]
    hrow = lax.broadcasted_iota(I32, (H_N, 1), 0)
    nk = N_TOP * PAGE_SIZE
    lane = lax.broadcasted_iota(I32, (1, nk), 1)
    new_blk = past // L_SEL
    bias_new = rbt[:, 0:1]
    gates = _sigmoid(gate_ref[...])
    nwin = win_ref.shape[0]
    wdist = nwin - lax.broadcasted_iota(I32, (1, nwin), 1)
    wbias = _bias_rows(wdist, rbt)
    wvalid = (wdist >= 0) & (wdist <= WINDOW)

    def with_new(logits, valid, weigh, k_new, v_new):
        l_new = jnp.sum(q * k_new, axis=-1, keepdims=True) + bias_new
        lm = jnp.where(valid, logits, NEG_INF)
        m = jnp.maximum(jnp.max(lm, axis=-1, keepdims=True), l_new)
        e = jnp.where(valid, jnp.exp(lm - m), 0.0)
        e_new = jnp.exp(l_new - m)
        den = jnp.sum(e, axis=-1, keepdims=True) + e_new
        return (weigh(e) + e_new * v_new) / den

    o_sel = jnp.zeros((H_N, HD_N), F32)
    o_win = jnp.zeros((H_N, HD_N), F32)
    bpp = PAGE_SIZE // L_SEL
    tok = lane % PAGE_SIZE
    for gi in range(N_KVG):
        ksl = slice(gi * HD_N, (gi + 1) * HD_N)
        vsl = slice((N_KVG + gi) * HD_N, (N_KVG + gi + 1) * HD_N)
        bid = jnp.zeros((1, nk), I32)
        for n in range(N_TOP):
            bid = jnp.where(lane // PAGE_SIZE == n, idx_ref[bi, gi, n], bid)
        dist = past - ((bid // bpp) * PAGE_SIZE + tok)
        valid = (bid != new_blk) & (tok // L_SEL == bid % bpp) & (dist >= 0)
        kt = jnp.concatenate([blk_ref[gi, n, 0] for n in range(N_TOP)], axis=1)
        vt = jnp.concatenate([blk_ref[gi, n, 1] for n in range(N_TOP)], axis=1)
        logits = _dot(q, kt) + _bias_rows(dist, rbt)
        og = with_new(logits, valid, lambda e, vt=vt: _dot_nt(e, vt), new_ref[0:1, ksl], new_ref[0:1, vsl])
        o_sel = jnp.where(hrow // HPG == gi, og, o_sel)
        logits = _dot_nt(q, win_ref[:, ksl]) + wbias
        og = with_new(logits, wvalid, lambda e, vsl=vsl: _dot(e, win_ref[:, vsl]), new_ref[1:2, ksl],
                      new_ref[1:2, vsl])
        o_win = jnp.where(hrow // HPG == gi, og, o_win)
    o_ref[...] = gates[:, 0:1] * oc_ref[...] + gates[:, 1:2] * o_sel + gates[:, 2:3] * o_win


def _sample_swa(idx, q, blocks, win, new_kv, rel_bias, gates, o_cmp, past):
    b = q.shape[0]
    w = win.shape[1]
    grid_spec = pltpu.PrefetchScalarGridSpec(
        num_scalar_prefetch=1, grid=(b,),
        in_specs=[pl.BlockSpec((None, H_N, HD_N), lambda i, ix: (i, 0, 0)),
                  pl.BlockSpec((None, N_KVG, N_TOP, 2, HD_N, PAGE_SIZE), lambda i, ix: (i, 0, 0, 0, 0, 0)),
                  pl.BlockSpec((None, w, KV_COLS), lambda i, ix: (i, 0, 0)),
                  pl.BlockSpec((None, 2, KV_COLS), lambda i, ix: (i, 0, 0)),
                  pl.BlockSpec((H_N, N_BUCKETS), lambda i, ix: (0, 0)),
                  pl.BlockSpec((None, H_N, 3), lambda i, ix: (i, 0, 0)),
                  pl.BlockSpec((None, H_N, HD_N), lambda i, ix: (i, 0, 0))],
        out_specs=pl.BlockSpec((None, H_N, HD_N), lambda i, ix: (i, 0, 0)))
    return pl.pallas_call(
        functools.partial(_sample_swa_kernel, past=past),
        grid_spec=grid_spec,
        out_shape=jax.ShapeDtypeStruct((b, H_N, HD_N), F32),
        compiler_params=_cparams("arbitrary"),
        name="nsa_sample_sel_win",
    )(idx, q, blocks, win, new_kv, rel_bias.T, gates, o_cmp)


ROUTER_COLS = LANES
MOE_ROWS_PROMPT = 256
MOE_ROWS_SAMPLE = 16


def _router_kernel(x_ref, g_ref, sh_ref, sc_ref, w_ref, b_ref, h_ref, e_ref, wt_ref, rk_ref, cnt_ref):
    i = pl.program_id(0)
    tm = x_ref.shape[0]

    @pl.when(i == 0)
    def _():
        cnt_ref[...] = jnp.zeros_like(cnt_ref)

    h = (_rms(x_ref[...], g_ref[...]) * (1.0 + sc_ref[...]) + sh_ref[...]).astype(BF)
    _store_folded(h_ref, h.astype(F32))
    logits = jnp.dot(h, w_ref[...], preferred_element_type=F32) + b_ref[...]
    lane = lax.broadcasted_iota(I32, (tm, ROUTER_COLS), 1)

    def top1(vals, ok):
        vm = jnp.where(ok, vals, -3e38)
        mx = jnp.max(vm, axis=-1, keepdims=True)
        return mx, jnp.min(jnp.where(ok & (vm == mx), lane, ROUTER_COLS), axis=-1, keepdims=True)

    isg = lane < N_EGROUPS
    pg = _softmax_rows(logits, isg)
    g_w, g_i = top1(pg, isg)
    ise = (lane >= N_EGROUPS) & ((lane - N_EGROUPS) // EXP_PER_GROUP == g_i)
    pe = _softmax_rows(logits, ise)
    w0, l0 = top1(pe, ise)
    w1, l1 = top1(pe, ise & (lane != l0))
    den = w0 + w1
    e0 = l0 - N_EGROUPS
    e1 = l1 - N_EGROUPS
    e_ref[...] = jnp.where(lane == 0, e0, jnp.where(lane == 1, e1, 0))
    wt_ref[...] = jnp.where(lane == 0, w0 / den * g_w, jnp.where(lane == 1, w1 / den * g_w, 0.0))
    oh0 = (lane == e0).astype(F32)
    oh1 = (lane == e1).astype(F32)
    cnt = oh0 + oh1
    ti = lax.broadcasted_iota(I32, (tm, tm), 0)
    si = lax.broadcasted_iota(I32, (tm, tm), 1)
    before = _dot((ti > si).astype(F32), cnt) + cnt_ref[...]
    r0 = jnp.sum(before * oh0, axis=-1, keepdims=True)
    r1 = jnp.sum(before * oh1, axis=-1, keepdims=True)
    rk_ref[...] = jnp.where(lane == 0, r0, jnp.where(lane == 1, r1, 0.0)).astype(I32)
    cnt_ref[...] = cnt_ref[...] + jnp.sum(cnt, axis=0, keepdims=True)


def _router(x, g, shift, scale, w_r, b_r, tm, rpb):
    m = x.shape[0]
    r = shift.shape[1]
    rows = lambda tn: pl.BlockSpec((tm, tn), lambda i: (i, 0))
    mods = pl.BlockSpec((None, r, D_MODEL), lambda i: ((i * tm) // rpb, 0, 0))
    small = lambda dt: jax.ShapeDtypeStruct((m, ROUTER_COLS), dt)
    return pl.pallas_call(
        _router_kernel,
        grid=(m // tm,),
        in_specs=[rows(D_MODEL), pl.BlockSpec((1, D_MODEL), lambda i: (0, 0)), mods, mods,
                  pl.BlockSpec((D_MODEL, ROUTER_COLS), lambda i: (0, 0)),
                  pl.BlockSpec((1, ROUTER_COLS), lambda i: (0, 0))],
        out_specs=[pl.BlockSpec((tm * ROW_FOLD, LANES), lambda i: (i, 0)),
                   rows(ROUTER_COLS), rows(ROUTER_COLS), rows(ROUTER_COLS),
                   pl.BlockSpec((1, ROUTER_COLS), lambda i: (0, 0))],
        out_shape=[jax.ShapeDtypeStruct((m * ROW_FOLD, LANES), F32), small(I32), small(F32), small(I32),
                   jax.ShapeDtypeStruct((1, ROUTER_COLS), F32)],
        compiler_params=_cparams("arbitrary"),
        name="moe_router",
    )(x, g.reshape(1, D_MODEL), shift, scale, w_r, b_r)


ROW_FOLD = D_MODEL // LANES


def _store_folded(ref, x):
    n = x.shape[0]
    for c in range(ROW_FOLD):
        ref[pl.ds(c, n, stride=ROW_FOLD), :] = x[:, c * LANES:(c + 1) * LANES]


def _load_folded(ref, first_row, n):
    return jnp.concatenate([ref[pl.ds(first_row * ROW_FOLD + c, n, stride=ROW_FOLD), :] for c in range(ROW_FOLD)],
                           axis=1)


def _row_gather_ring(src_hbm, buf, sems, groups, idx_now, idx_next, inline_next=False):
    i = pl.program_id(0)
    last = pl.num_programs(0) - 1
    slot = i % 2
    total = sum(cnt for _, cnt, _ in groups)
    assert 2 * total * ROW_FOLD == buf.shape[0]

    def fold(row):
        return pl.ds(pl.multiple_of(row * ROW_FOLD, ROW_FOLD), ROW_FOLD)

    def start(idx, s, first, k, r, priority):
        pltpu.make_async_copy(src_hbm.at[fold(idx(k, r))], buf.at[fold(s * total + first + r)],
                              sems.at[s]).start(priority=priority)

    def start_all(idx, s):
        for first, cnt, k in groups:
            per_trip = math.gcd(cnt, 8)

            def issue(j, c):
                for u in range(per_trip):
                    start(idx, s, first, k, per_trip * j + u, u % 2)
                return c
            lax.fori_loop(0, cnt // per_trip, issue, 0)

    def wait_slot(s):
        whole = buf.at[pl.ds(pl.multiple_of(s * total * ROW_FOLD, ROW_FOLD), total * ROW_FOLD)]
        pltpu.make_async_copy(whole, whole, sems.at[s]).wait()

    @pl.when(i == 0)
    def _():
        start_all(idx_now, slot)

    def finish():
        if inline_next:
            @pl.when(i == last)
            def _():
                wait_slot(1 - slot)

    if inline_next:
        wait_slot(slot)
        for first, cnt, k in groups:
            for r in range(cnt):
                start(idx_next, 1 - slot, first, k, r, r % 2)
    else:
        @pl.when(i < last)
        def _():
            start_all(idx_next, 1 - slot)

        wait_slot(slot)
    return slot * total, finish


def _expert_kernel(be_ref, rt_ref, rtn_ref, h_hbm, w1_ref, w3_ref, w2_ref, o_ref, xbuf, sems, w1b, w3b, w2b):
    i = pl.program_id(0)
    blk = rt_ref.shape[1]

    @pl.when((i == 0) | (be_ref[i] != be_ref[jnp.maximum(i - 1, 0)]))
    def _():
        w1b[...] = w1_ref[...].astype(BF)
        w3b[...] = w3_ref[...].astype(BF)
        w2b[...] = w2_ref[...].astype(BF)

    base, finish = _row_gather_ring(h_hbm, xbuf, sems, ((0, blk, 0),), lambda k, r: rt_ref[0, r],
                                    lambda k, r: rtn_ref[0, r], inline_next=True)
    x = _load_folded(xbuf, base, blk).astype(BF)
    a = jnp.dot(x, w1b[...], preferred_element_type=F32)
    b = jnp.dot(x, w3b[...], preferred_element_type=F32)
    hid = a * _sigmoid(a) * b
    _store_folded(o_ref, jnp.dot(hid.astype(BF), w2b[...], preferred_element_type=F32))
    finish()


def _experts(h2, row_tok, blk_exp, w1, w3, w2):
    nblk, _, blk = row_tok.shape
    idx_spec = lambda d: pl.BlockSpec((None, 1, blk), lambda i, be: (jnp.minimum(i + d, nblk - 1), 0, 0),
                                      memory_space=pltpu.SMEM)
    grid_spec = pltpu.PrefetchScalarGridSpec(
        num_scalar_prefetch=1, grid=(nblk,),
        in_specs=[idx_spec(0), idx_spec(1),
                  pl.BlockSpec(memory_space=pl.ANY),
                  pl.BlockSpec((None, D_MODEL, D_EXP), lambda i, be: (be[i], 0, 0)),
                  pl.BlockSpec((None, D_MODEL, D_EXP), lambda i, be: (be[i], 0, 0)),
                  pl.BlockSpec((None, D_EXP, D_MODEL), lambda i, be: (be[i], 0, 0))],
        out_specs=pl.BlockSpec((blk * ROW_FOLD, LANES), lambda i, be: (i, 0)),
        scratch_shapes=[pltpu.VMEM((2 * blk * ROW_FOLD, LANES), F32), pltpu.SemaphoreType.DMA((2,)),
                        pltpu.VMEM((D_MODEL, D_EXP), BF), pltpu.VMEM((D_MODEL, D_EXP), BF),
                        pltpu.VMEM((D_EXP, D_MODEL), BF)])
    return pl.pallas_call(
        _expert_kernel,
        grid_spec=grid_spec,
        out_shape=jax.ShapeDtypeStruct((nblk * blk * ROW_FOLD, LANES), F32),
        compiler_params=_cparams("arbitrary"),
        name="moe_experts",
    )(blk_exp, row_tok, row_tok, h2, w1, w3, w2)


def _final_kernel(x_ref, g_ref, dest_ref, destn_ref, ys_hbm, wt_ref, nf_ref, o_ref, ybuf, sems):
    tm = x_ref.shape[0]
    groups = tuple((k * tm, tm, k) for k in range(TOP_K))
    base, _ = _row_gather_ring(ys_hbm, ybuf, sems, groups, lambda k, r: dest_ref[k, r], lambda k, r: destn_ref[k, r])
    wt = wt_ref[...]
    moe = wt[:, 0:1] * _load_folded(ybuf, base, tm) + wt[:, 1:2] * _load_folded(ybuf, base + tm, tm)
    o_ref[...] = _rms(x_ref[...] + g_ref[...] * moe, nf_ref[...])


def _final(x, gate, ys, dest, wts, norm_f, tm, rpb):
    m = x.shape[0]
    r = gate.shape[1]
    nt = m // tm
    rows = lambda tn: pl.BlockSpec((tm, tn), lambda i: (i, 0))
    idx_spec = lambda d: pl.BlockSpec((None, TOP_K, tm), lambda i: (jnp.minimum(i + d, nt - 1), 0, 0),
                                      memory_space=pltpu.SMEM)
    return pl.pallas_call(
        _final_kernel,
        grid=(nt,),
        in_specs=[rows(D_MODEL), pl.BlockSpec((None, r, D_MODEL), lambda i: ((i * tm) // rpb, 0, 0)),
                  idx_spec(0), idx_spec(1),
                  pl.BlockSpec(memory_space=pl.ANY),
                  rows(ROUTER_COLS), pl.BlockSpec((1, D_MODEL), lambda i: (0, 0))],
        out_specs=rows(D_MODEL),
        out_shape=jax.ShapeDtypeStruct((m, D_MODEL), F32),
        scratch_shapes=[pltpu.VMEM((2 * TOP_K * tm * ROW_FOLD, LANES), F32), pltpu.SemaphoreType.DMA((2,))],
        compiler_params=_cparams("arbitrary"),
        name="moe_combine_final_norm",
    )(x, gate, dest, dest, ys, wts, norm_f.reshape(1, D_MODEL))


def _moe_and_final(x1, g2, shift, scale, gate, w_r, b_r, exp_w1, exp_w3, exp_w2, norm_f, tm, rpb, blk):
    m = x1.shape[0]
    h2, eid, wts, rank, counts = _router(x1, g2, shift, scale, w_r, b_r, tm, rpb)
    counts = counts[0, :N_EXP].astype(I32)
    padded = (counts + blk - 1) // blk * blk
    pend = jnp.cumsum(padded)
    pstart = pend - padded
    n_blocks = -(-(m * TOP_K) // blk) + N_EXP
    starts = jnp.arange(n_blocks, dtype=I32)[:, None] * blk
    blk_exp = jnp.minimum(jnp.sum((pend[None, :] <= starts).astype(I32), axis=1), N_EXP - 1)
    e = eid[:, :TOP_K]
    dest = pstart[e] + rank[:, :TOP_K]
    tok = jnp.broadcast_to(jnp.arange(m, dtype=I32)[:, None], (m, TOP_K))
    row_tok = jnp.zeros((n_blocks * blk,), I32).at[dest.reshape(-1)].set(tok.reshape(-1))
    ys = _experts(h2, row_tok.reshape(n_blocks, 1, blk), blk_exp, exp_w1, exp_w3, exp_w2)
    dest_t = jnp.transpose(dest.reshape(m // tm, tm, TOP_K), (0, 2, 1))
    return _final(x1, gate, ys, dest_t, wts, norm_f, tm, rpb)


def _pack_in_proj(w_in):
    o = C_RIN
    w_r = w_in[:, :o]
    w_q = w_in[:, o:o + C_N + 3 * KV_COLS]
    o += C_N + 3 * KV_COLS
    w_gn = w_in[:, o:o + 3 * H_N].reshape(D_MODEL, N_KVG, 3 * HPG)
    w_gn = jnp.pad(w_gn, ((0, 0), (0, 0), (0, GN_GROUP_COLS - 3 * HPG))).reshape(D_MODEL, N_KVG * GN_GROUP_COLS)
    o += 3 * H_N
    w_gm = w_in[:, o:]
    return w_r.astype(BF), jnp.concatenate([w_q, w_gn], axis=1).astype(BF), w_gm.astype(BF)


def _prompt_bias_tables(rel_bias, t):
    tq, tk = ATT_TILE, ATT_TK
    i = np.arange(tq)[None, :]
    j = np.arange(tk)[:, None]
    dist = np.stack([tk * (o - (ATT_R - 1)) + i - j for o in range(N_WIN_OFFS)]).astype(np.int32)
    raw = _bias_table(jnp.asarray(dist.reshape(N_WIN_OFFS * tk, tq)), rel_bias, tk)
    raw = jnp.transpose(raw.reshape(H_N, N_WIN_OFFS, tk, tq), (1, 0, 2, 3))
    ok_w = jnp.asarray((dist >= 0) & (dist <= WINDOW))[:, None]
    ok_s = jnp.asarray(dist[:N_SEL_OFFS] >= 0)[:, None]
    masked = jnp.full((1, H_N * tk, tq), NEG_INF, F32)
    tab_w = jnp.concatenate([jnp.where(ok_w, raw, NEG_INF).reshape(N_WIN_OFFS, H_N * tk, tq), masked])
    tab_s = jnp.concatenate([jnp.where(ok_s, raw[:N_SEL_OFFS], NEG_INF).reshape(N_SEL_OFFS, H_N * tk, tq), masked])
    nc = (t - L_CMP) // STRIDE + 1
    ncp = nc + 1
    dc = (np.arange(t)[:, None] - (np.arange(ncp)[None, :] * STRIDE + L_CMP - 1)).astype(np.int32)
    return tab_s, tab_w, _bias_table(jnp.asarray(dc), rel_bias, tq)


def kernel(x_prompt, x_sample, c_prompt, c_sample, cache_cmp_kv, cache_sel_kv, state_win_kv, state_rwkv_shift,
           state_rwkv_wkv, page_table, rel_bias, norm_f, norm1, norm2, w_ada, b_ada, w_in, rwkv_mu, rwkv_w0, rwkv_w2,
           rwkv_a0, rwkv_a2, rwkv_g2, rwkv_kk, rwkv_ka, rwkv_rk, rwkv_ln_g, rwkv_ln_b, cmp_pos, cmp_w1, cmp_w2,
           w_o_rwkv, w_o_nsa, w_out, router_wg, router_bg, router_we, router_be, exp_w1, exp_w3, exp_w2):
    bp, t, _ = x_prompt.shape
    bs = x_sample.shape[0]
    mp = bp * t
    past = page_table.shape[1] * PAGE_SIZE

    nrow = -(-(bp + bs) // 8) * 8
    c_all = jnp.concatenate([c_prompt, c_sample, jnp.zeros((nrow - bp - bs, D_MODEL), F32)], axis=0)
    mod = _ada(c_all, w_ada[0], b_ada[0]).reshape(nrow, 6, D_MODEL)
    mod_p = [mod[:bp, i][:, None, :] for i in range(6)]
    mod_s = [mod[bp:bp + bs, i][None] for i in range(6)]

    w_r, w_n, w_gm = _pack_in_proj(w_in[0])
    rw = _rwkv_weights(rwkv_mu[0], rwkv_w0[0], rwkv_w2[0], rwkv_a0[0], rwkv_a2[0], rwkv_g2[0], rwkv_kk[0],
                       rwkv_ka[0], rwkv_rk[0], rwkv_ln_g[0], rwkv_ln_b[0])
    wbd = _cmp_weights(cmp_w1[0], N_KVG)
    cpos = _cmp_partial_rows(_cmp_pos_rows(cmp_pos[0]), wbd, 8)
    wo_r, wo_n, wo = w_o_rwkv[0].astype(BF), w_o_nsa[0].astype(BF), w_out[0].astype(BF)
    w_router = jnp.pad(jnp.concatenate([router_wg[0], router_we[0]], axis=1),
                       ((0, 0), (0, ROUTER_COLS - N_EGROUPS - N_EXP))).astype(BF)
    b_router = jnp.pad(jnp.concatenate([router_bg[0], router_be[0]]), (0, ROUTER_COLS - N_EGROUPS - N_EXP))[None]

    tm = 512
    xp = x_prompt.reshape(mp, D_MODEL)
    h = _norm_mod(xp, norm1[0], mod_p[0], mod_p[1], tm, t)
    p_r = _matmul(h, w_r, tm, C_RIN // 2)
    p_n = _matmul(h, w_n, tm, NP_COLS // 2)
    p_g = _matmul(h, w_gm, tm, 2048)
    o_r, shift_p, wkv_p = _rwkv_prompt(p_r.reshape(bp, t, C_RIN), rw)
    kvc = p_n[:, NP_KVC:NP_KVC + KV_COLS]
    kvs = p_n[:, NP_KVS:NP_KVS + KV_COLS]
    kvw = p_n[:, NP_KVW:NP_KVW + KV_COLS]
    nch = t // STRIDE
    c_part = _cmp_partial_rows(kvc.reshape(bp * nch, STRIDE * KV_COLS), wbd, nch)
    kv_cmp = _cmp_finish(c_part.reshape(bp, nch, -1), cpos, cmp_w2[0])
    tab_s, tab_w, bias_c = _prompt_bias_tables(rel_bias, t)
    o_cmp, sel = _cmp_attn_prompt(p_n, kv_cmp, bias_c, t)
    o_n = _swa_prompt(p_n, *_swa_operands(p_n, sel, bp, t), tab_s, tab_w, o_cmp, t)
    y = _merge(o_r.reshape(mp, C_R), o_n, wo_r, wo_n, p_g, tm)
    x1 = _proj_residual(y, wo, xp, mod_p[2], tm, t)
    y_prompt = _moe_and_final(x1, norm2[0], mod_p[3], mod_p[4], mod_p[5], w_router, b_router, exp_w1[0], exp_w3[0],
                              exp_w2[0], norm_f, tm, t, MOE_ROWS_PROMPT).reshape(bp, t, D_MODEL)
    kv_shape = (1, bp, t, 2, N_KVG, HD_N)
    wlen = min(WINDOW, t)
    win_p = kvw.reshape(bp, t, KV_COLS)[:, t - wlen:].reshape(1, bp, wlen, 2, N_KVG, HD_N)

    xs = x_sample.reshape(bs, D_MODEL)
    hs = _norm_mod(xs, norm1[0], mod_s[0], mod_s[1], bs, bs)
    ps_r = _matmul(hs, w_r, bs, C_RIN // 2)
    ps_n = _matmul(hs, w_n, bs, NP_COLS // 2)
    ps_g = _matmul(hs, w_gm, bs, 2048)
    os_r, wkv_s = _rwkv_step(ps_r, state_rwkv_shift[0], state_rwkv_wkv[0], rw)
    kvc_s = ps_n[:, NP_KVC:NP_KVC + KV_COLS]
    kvs_s = ps_n[:, NP_KVS:NP_KVS + KV_COLS]
    kvw_s = ps_n[:, NP_KVW:NP_KVW + KV_COLS]
    cs_part = _cmp_partial_paged(jnp.transpose(cache_cmp_kv[0], (0, 2, 3, 4, 1)), page_table,
                                 _cmp_weights(cmp_w1[0], CMP_PAIR))
    kv_cmp_s = _cmp_finish(cs_part, cpos, cmp_w2[0])
    q_s = ps_n[:, :C_N].reshape(bs, H_N, HD_N)
    o_cmp_s, picks = _sample_cmp(q_s, kv_cmp_s, rel_bias, past)
    idx = picks[:, :N_KVG, :N_TOP]
    bpp = PAGE_SIZE // L_SEL
    npb = past // L_SEL
    idc = jnp.minimum(idx, npb - 1)
    page = jnp.take_along_axis(page_table, (idc // bpp).reshape(bs, -1), axis=1).reshape(bs * N_KVG, N_TOP)
    cache_t = jnp.transpose(cache_sel_kv[0], (0, 2, 3, 4, 1))
    blocks = _gather_sel_pages(cache_t, page).reshape(bs, N_KVG, N_TOP, 2, HD_N, PAGE_SIZE)
    win_buf = state_win_kv[0].reshape(bs, -1, KV_COLS)
    gates_s = ps_n[:, NP_GN:].reshape(bs, N_KVG, GN_GROUP_COLS)[:, :, :3 * HPG].reshape(bs, H_N, 3)
    new_kv = jnp.stack([kvs_s, kvw_s], axis=1)
    os_n = _sample_swa(idx, q_s, blocks, win_buf, new_kv, rel_bias, gates_s, o_cmp_s, past)
    ys = _merge(os_r, os_n.reshape(bs, C_N).astype(BF), wo_r, wo_n, ps_g, bs)
    xs1 = _proj_residual(ys, wo, xs, mod_s[2], bs, bs)
    y_sample = _moe_and_final(xs1, norm2[0], mod_s[3], mod_s[4], mod_s[5], w_router, b_router, exp_w1[0], exp_w3[0],
                              exp_w2[0], norm_f, bs, bs, MOE_ROWS_SAMPLE).reshape(bs, 1, D_MODEL)
    kv1 = (1, bs, 1, 2, N_KVG, HD_N)
    wbuf = win_buf.shape[1]
    win_s = jnp.concatenate([win_buf, kvw_s[:, None, :]], axis=1)[:, -wbuf:].reshape(1, bs, wbuf, 2, N_KVG, HD_N)

    return (y_prompt, y_sample,
            kvc.reshape(kv_shape), kvc_s.reshape(kv1),
            kvs.reshape(kv_shape), kvs_s.reshape(kv1),
            win_p, win_s,
            shift_p.reshape(1, bp, C_RIN), ps_r.reshape(1, bs, C_RIN),
            wkv_p[None], wkv_s[None])
```

```python
import functools
import math

import numpy as np
import jax
import jax.numpy as jnp
from jax import lax
from jax.experimental import pallas as pl
from jax.experimental.pallas import tpu as pltpu

D_MODEL = 2048
PAGE_SIZE = 128
H_R, HD_R = 16, 64
C_R = H_R * HD_R
LORA_W, LORA_A, LORA_G = 64, 64, 128
C_RIN = 3 * C_R + LORA_W + LORA_A + LORA_G
LN_X_EPS = 64e-5
H_N, HD_N, N_KVG = 16, 64, 4
HPG = H_N // N_KVG
C_N = H_N * HD_N
KV_COLS = 2 * N_KVG * HD_N
L_CMP, STRIDE, CMP_HID = 32, 16, 64
L_SEL, N_TOP, WINDOW = 64, 16, 512
N_BUCKETS, MAX_DIST = 32, 128
N_EGROUPS, EXP_PER_GROUP = 4, 8
N_EXP = N_EGROUPS * EXP_PER_GROUP
TOP_K, D_EXP = 2, 512
RMS_EPS = 1e-6
NEG_INF = -1e30
FORCE = 1e9

BF = jnp.bfloat16
F32 = jnp.float32
I32 = jnp.int32

VMEM_LIMIT_BYTES = 56 * 1024 * 1024
LANES = 128
RW_CHUNK = 32
ATT_TILE = 256
GN_GROUP_COLS = 128
NP_Q, NP_KVC, NP_KVS, NP_KVW, NP_GN = 0, C_N, C_N + KV_COLS, C_N + 2 * KV_COLS, C_N + 3 * KV_COLS
NP_COLS = NP_GN + N_KVG * GN_GROUP_COLS


def _cparams(*sem):
    return pltpu.CompilerParams(dimension_semantics=sem, vmem_limit_bytes=VMEM_LIMIT_BYTES)


def _dot(a, b):
    return jnp.dot(a.astype(BF), b.astype(BF), preferred_element_type=F32)


def _dot_nt(a, b):
    return lax.dot_general(a.astype(BF), b.astype(BF), (((1,), (1,)), ((), ())), preferred_element_type=F32)


def _dot_tn(a, b):
    return lax.dot_general(a.astype(BF), b.astype(BF), (((0,), (0,)), ((), ())), preferred_element_type=F32)


def _softplus(x):
    return jnp.maximum(x, 0.0) + jnp.log1p(jnp.exp(-jnp.abs(x)))


def _sigmoid(x):
    return 1.0 / (1.0 + jnp.exp(-x))


def _gelu_tanh(x):
    return 0.5 * x * (1.0 + jnp.tanh(math.sqrt(2.0 / math.pi) * (x + 0.044715 * x * x * x)))


def _t5_bucket(dist):
    n = jnp.maximum(dist, 0)
    max_exact = N_BUCKETS // 2
    nf = jnp.maximum(n, 1).astype(F32)
    large = max_exact + (jnp.log(nf / max_exact) / math.log(MAX_DIST / max_exact)
                         * (N_BUCKETS - max_exact)).astype(I32)
    large = jnp.minimum(large, N_BUCKETS - 1)
    return jnp.where(n < max_exact, n, large)


def _bias_rows(dist, rbt):
    bucket = _t5_bucket(dist)
    out = jnp.zeros((rbt.shape[0], dist.shape[1]), F32)
    for b in range(N_BUCKETS):
        out = jnp.where(bucket == b, rbt[:, b:b + 1], out)
    return out


def _ada_kernel(c_ref, w_ref, b_ref, o_ref):
    o_ref[...] = _dot(c_ref[...], w_ref[...]) + b_ref[...]


def _ada(c, w_ada, b_ada):
    r = c.shape[0]
    n = w_ada.shape[1]
    tn = 1024
    return pl.pallas_call(
        _ada_kernel,
        grid=(n // tn,),
        in_specs=[pl.BlockSpec((r, D_MODEL), lambda j: (0, 0)),
                  pl.BlockSpec((D_MODEL, tn), lambda j: (0, j)),
                  pl.BlockSpec((1, tn), lambda j: (0, j))],
        out_specs=pl.BlockSpec((r, tn), lambda j: (0, j)),
        out_shape=jax.ShapeDtypeStruct((r, n), F32),
        compiler_params=_cparams("arbitrary"),
        name="ada_mod",
    )(c, w_ada, b_ada.reshape(1, n))


def _rms(x, g):
    return x * lax.rsqrt(jnp.mean(x * x, axis=-1, keepdims=True) + RMS_EPS) * g


def _norm_mod_kernel(x_ref, g_ref, sh_ref, sc_ref, o_ref):
    o_ref[...] = (_rms(x_ref[...], g_ref[...]) * (1.0 + sc_ref[...]) + sh_ref[...]).astype(o_ref.dtype)


def _row_specs(m, tm, rpb):
    del m
    return (lambda tn: pl.BlockSpec((tm, tn), lambda i, j: (i, j)),
            lambda r, tn: pl.BlockSpec((None, r, tn), lambda i, j: ((i * tm) // rpb, 0, j)))


def _norm_mod(x, g, shift, scale, tm, rpb):
    m = x.shape[0]
    r = shift.shape[1]
    rows, mods = _row_specs(m, tm, rpb)
    return pl.pallas_call(
        _norm_mod_kernel,
        grid=(m // tm, 1),
        in_specs=[rows(D_MODEL), pl.BlockSpec((1, D_MODEL), lambda i, j: (0, 0)), mods(r, D_MODEL), mods(r, D_MODEL)],
        out_specs=rows(D_MODEL),
        out_shape=jax.ShapeDtypeStruct((m, D_MODEL), BF),
        compiler_params=_cparams("arbitrary", "arbitrary"),
        name="norm_mod",
    )(x, g.reshape(1, D_MODEL), shift, scale)


def _mm_kernel(a_ref, w_ref, o_ref):
    o_ref[...] = jnp.dot(a_ref[...], w_ref[...], preferred_element_type=F32).astype(o_ref.dtype)


def _matmul(a, w, tm, tn, out_dtype=F32):
    m, k = a.shape
    n = w.shape[1]
    return pl.pallas_call(
        _mm_kernel,
        grid=(m // tm, n // tn),
        in_specs=[pl.BlockSpec((tm, k), lambda i, j: (i, 0)), pl.BlockSpec((k, tn), lambda i, j: (0, j))],
        out_specs=pl.BlockSpec((tm, tn), lambda i, j: (i, j)),
        out_shape=jax.ShapeDtypeStruct((m, n), out_dtype),
        compiler_params=_cparams("arbitrary", "arbitrary"),
        name="matmul",
    )(a, w)


def _merge_kernel(or_ref, on_ref, wr_ref, wn_ref, g0_ref, g1_ref, o_ref):
    yr = jnp.dot(or_ref[...], wr_ref[...], preferred_element_type=F32)
    yn = jnp.dot(on_ref[...], wn_ref[...], preferred_element_type=F32)
    o_ref[...] = (_sigmoid(g0_ref[...]) * yr + _sigmoid(g1_ref[...]) * yn).astype(o_ref.dtype)


def _merge(o_r, o_n, w_r, w_n, p_g, tm):
    m = o_r.shape[0]
    tn = 1024
    nb = D_MODEL // tn
    return pl.pallas_call(
        _merge_kernel,
        grid=(m // tm, nb),
        in_specs=[pl.BlockSpec((tm, C_R), lambda i, j: (i, 0)), pl.BlockSpec((tm, C_N), lambda i, j: (i, 0)),
                  pl.BlockSpec((C_R, tn), lambda i, j: (0, j)), pl.BlockSpec((C_N, tn), lambda i, j: (0, j)),
                  pl.BlockSpec((tm, tn), lambda i, j: (i, j)), pl.BlockSpec((tm, tn), lambda i, j: (i, j + nb))],
        out_specs=pl.BlockSpec((tm, tn), lambda i, j: (i, j)),
        out_shape=jax.ShapeDtypeStruct((m, D_MODEL), BF),
        compiler_params=_cparams("arbitrary", "arbitrary"),
        name="merge_branches",
    )(o_r, o_n, w_r, w_n, p_g, p_g)


def _proj_res_kernel(y_ref, w_ref, x_ref, g_ref, o_ref):
    o_ref[...] = x_ref[...] + g_ref[...] * jnp.dot(y_ref[...], w_ref[...], preferred_element_type=F32)


def _proj_residual(y, w, x, gate, tm, rpb):
    m = y.shape[0]
    tn = 1024
    r = gate.shape[1]
    rows, mods = _row_specs(m, tm, rpb)
    return pl.pallas_call(
        _proj_res_kernel,
        grid=(m // tm, D_MODEL // tn),
        in_specs=[pl.BlockSpec((tm, D_MODEL), lambda i, j: (i, 0)), pl.BlockSpec((D_MODEL, tn), lambda i, j: (0, j)),
                  rows(tn), mods(r, tn)],
        out_specs=rows(tn),
        out_shape=jax.ShapeDtypeStruct((m, D_MODEL), F32),
        compiler_params=_cparams("arbitrary", "arbitrary"),
        name="out_proj_residual",
    )(y, w, x, gate)


def _rwkv_features(p, p_prev, mu, w0, w2, a0, a2, g2, k_k, k_a):
    xm = p + (p_prev - p) * mu
    r = xm[:, :C_R]
    k = xm[:, C_R:2 * C_R]
    v = xm[:, 2 * C_R:3 * C_R]
    o = 3 * C_R
    wd = xm[:, o:o + LORA_W]
    ad = xm[:, o + LORA_W:o + LORA_W + LORA_A]
    gd = xm[:, o + LORA_W + LORA_A:]
    w_log = -_softplus(-(w0 + _dot(jnp.tanh(wd), w2))) - 0.5
    lw = -jnp.exp(w_log)
    a = _sigmoid(a0 + _dot(ad, a2))
    g = _dot(_sigmoid(gd), g2)
    kk = k * k_k
    k = k * (1.0 + (a - 1.0) * k_a)
    return r, k, v, lw, a, g, kk


def _head_sums(x, ones2):
    nt = x.shape[1] // LANES
    xs = jnp.concatenate([x[:, j * LANES:(j + 1) * LANES] for j in range(nt)], axis=0)
    hi = xs.astype(BF)
    lo = (xs - hi.astype(F32)).astype(BF)
    s = jnp.dot(hi, ones2, preferred_element_type=F32) + jnp.dot(lo, ones2, preferred_element_type=F32)
    r = x.shape[0]
    return jnp.concatenate([s[j * r:(j + 1) * r] for j in range(nt)], axis=1)


def _rwkv_chunk_kernel(pr_ref, mu_ref, w0_ref, w2_ref, a0_ref, a2_ref, g2_ref, kk_ref, ka_ref, rk_ref,
                       lng_ref, lnb_ref, ones_ref, o_ref, shift_ref, state_ref, y_ref):
    c = pl.program_id(1)
    C = RW_CHUNK

    @pl.when(c == 0)
    def _():
        shift_ref[...] = jnp.zeros_like(shift_ref)
        state_ref[...] = jnp.zeros_like(state_ref)

    p = pr_ref[...]
    row = lax.broadcasted_iota(I32, (C, 1), 0)
    p_prev = jnp.where(row == 0, shift_ref[...], pltpu.roll(p, 1, axis=0))
    shift_ref[...] = p[C - 1:C, :]
    r, k, v, lw, a, g, kk_all = _rwkv_features(p, p_prev, mu_ref[...], w0_ref[...], w2_ref[...], a0_ref[...],
                                               a2_ref[...], g2_ref[...], kk_ref[...], ka_ref[...])
    cl = lw
    s = 1
    while s < C:
        cl = cl + jnp.where(row >= s, pltpu.roll(cl, s, axis=0), 0.0)
        s *= 2
    ti = lax.broadcasted_iota(I32, (C, C), 0)
    si = lax.broadcasted_iota(I32, (C, C), 1)
    strict = ti > si
    incl = ti >= si
    eye = (ti == si).astype(F32)
    heads = range(H_R)
    sls = [slice(h * HD_R, (h + 1) * HD_R) for h in heads]
    ones2 = ones_ref[...]
    kk_n = kk_all / jnp.maximum(jnp.sqrt(_head_sums(kk_all * kk_all, ones2)), 1e-12)
    b_all = kk_n * a
    cl_end = cl[C - 1:C, :]
    e_neg = jnp.exp(-cl)
    e_end = jnp.exp(cl_end - cl)
    g_end_all = jnp.exp(cl_end)
    per_head = lambda z: [z[:, sl] for sl in sls]
    kkt_all = (kk_n * jnp.exp(cl - lw)).astype(BF)
    rt_all = (r * jnp.exp(cl)).astype(BF)
    kr = per_head(jnp.concatenate([kkt_all, rt_all], axis=0))
    kh = per_head((k * e_neg).astype(BF))
    bh = per_head((b_all * e_neg).astype(BF))
    kbb = per_head(jnp.concatenate([(k * e_end).astype(BF), (b_all * e_end).astype(BF)], axis=0))
    vb = per_head(v.astype(BF))
    g_end = per_head(g_end_all)
    ak = [_dot_nt(kr[h], kh[h]) for h in heads]
    ab = [_dot_nt(kr[h], bh[h]) for h in heads]
    lg = [jnp.concatenate([jnp.where(strict, ak[h][:C], 0.0), jnp.where(incl, ak[h][C:], 0.0)], axis=0).astype(BF)
          for h in heads]
    nil = [jnp.where(strict, -ab[h][:C], 0.0) for h in heads]
    grb = [jnp.where(incl, ab[h][C:], 0.0).astype(BF) for h in heads]
    tinv = [eye + n for n in nil]
    m = 2
    while m < C:
        nil = [_dot(n, n) for n in nil]
        tinv = [t + _dot(t, n) for t, n in zip(tinv, nil)]
        m *= 2
    s0 = [state_ref[h] for h in heads]
    xy = [_dot_nt(kr[h], s0[h]) + _dot(lg[h], vb[h]) for h in heads]
    u = [_dot(tinv[h], xy[h][:C]).astype(BF) for h in heads]
    y = [xy[h][C:] - _dot(grb[h], u[h]) for h in heads]
    for h in heads:
        vu = jnp.concatenate([vb[h], -u[h]], axis=0)
        state_ref[h] = s0[h] * g_end[h] + _dot_tn(vu, kbb[h])
    for h, sl in enumerate(sls):
        y_ref[:, sl] = y[h]
    y_all = y_ref[...]
    yc = y_all - _head_sums(y_all, ones2) * (1.0 / HD_R)
    var = _head_sums(yc * yc, ones2) * (1.0 / HD_R)
    yn = yc * lax.rsqrt(var + LN_X_EPS) * lng_ref[...] + lnb_ref[...]
    bonus = _head_sums(r * k * rk_ref[...], ones2) * v
    o_ref[...] = ((yn + bonus) * g).astype(o_ref.dtype)


def _rwkv_weights(mu, w0, w2, a0, a2, g2, k_k, k_a, r_k, ln_g, ln_b):
    row = lambda z: z.reshape(1, -1).astype(F32)
    return (row(mu), row(w0), w2.astype(BF), row(a0), a2.astype(BF), g2.astype(BF), row(k_k), row(k_a), row(r_k),
            row(ln_g), row(ln_b))


_RWKV_W_SHAPES = ((1, C_RIN), (1, C_R), (LORA_W, C_R), (1, C_R), (LORA_A, C_R), (LORA_G, C_R), (1, C_R), (1, C_R),
                  (1, C_R), (1, C_R), (1, C_R))


def _rwkv_prompt(pr, rw):
    b, t, _ = pr.shape
    C = RW_CHUNK
    full = lambda shp: pl.BlockSpec(shp, lambda i, j: (0,) * len(shp))
    lane_head = np.arange(LANES) // HD_R
    ones2 = jnp.asarray(lane_head[:, None] == lane_head[None, :], BF)
    return pl.pallas_call(
        _rwkv_chunk_kernel,
        grid=(b, t // C),
        in_specs=[pl.BlockSpec((None, C, C_RIN), lambda i, j: (i, j, 0))] + [full(s) for s in _RWKV_W_SHAPES]
        + [full((LANES, LANES))],
        out_specs=[pl.BlockSpec((None, C, C_R), lambda i, j: (i, j, 0)),
                   pl.BlockSpec((None, 1, C_RIN), lambda i, j: (i, 0, 0)),
                   pl.BlockSpec((None, H_R, HD_R, HD_R), lambda i, j: (i, 0, 0, 0))],
        out_shape=[jax.ShapeDtypeStruct((b, t, C_R), BF),
                   jax.ShapeDtypeStruct((b, 1, C_RIN), F32),
                   jax.ShapeDtypeStruct((b, H_R, HD_R, HD_R), F32)],
        scratch_shapes=[pltpu.VMEM((C, C_R), F32)],
        compiler_params=_cparams("arbitrary", "arbitrary"),
        name="rwkv_chunk",
    )(pr, *rw, ones2)


def _rwkv_step_kernel(pr_ref, prev_ref, s0_ref, mu_ref, w0_ref, w2_ref, a0_ref, a2_ref, g2_ref, kk_ref, ka_ref,
                      rk_ref, lng_ref, lnb_ref, o_ref, state_ref):
    nb = pr_ref.shape[0]
    r, k, v, lw, a, g, kk_all = _rwkv_features(pr_ref[...], prev_ref[...], mu_ref[...], w0_ref[...], w2_ref[...],
                                               a0_ref[...], a2_ref[...], g2_ref[...], kk_ref[...], ka_ref[...])
    decay = jnp.exp(lw)
    ii = lax.broadcasted_iota(I32, (HD_R, HD_R), 0)
    jj = lax.broadcasted_iota(I32, (HD_R, HD_R), 1)
    eye = ii == jj
    col = lambda z: jnp.sum(jnp.where(eye, z, 0.0), axis=1, keepdims=True)
    heads = range(H_R)
    sls = [slice(h * HD_R, (h + 1) * HD_R) for h in heads]
    for bi in range(nb):
        rows = lambda z: [z[bi:bi + 1, sl] for sl in sls]
        r_h, k_h, v_h, a_h, w_h, g_h, kk_h = rows(r), rows(k), rows(v), rows(a), rows(decay), rows(g), rows(kk_all)
        nrm = [jnp.maximum(jnp.sqrt(jnp.sum(z * z, axis=-1, keepdims=True)), 1e-12) for z in kk_h]
        kk_h = [kk_h[h] / nrm[h] for h in heads]
        s0 = [s0_ref[bi, h] for h in heads]
        sa = [jnp.sum(s0[h] * (-kk_h[h]), axis=1, keepdims=True) for h in heads]
        v_col = [col(z) for z in v_h]
        s1 = [s0[h] * w_h[h] + sa[h] * (kk_h[h] * a_h[h]) + v_col[h] * k_h[h] for h in heads]
        for h in heads:
            state_ref[bi, h] = s1[h]
        y_col = [jnp.sum(s1[h] * r_h[h], axis=1, keepdims=True) for h in heads]
        y = [jnp.sum(jnp.where(eye, z, 0.0), axis=0, keepdims=True) for z in y_col]
        yc = [z - jnp.mean(z, axis=-1, keepdims=True) for z in y]
        var = [jnp.mean(z * z, axis=-1, keepdims=True) for z in yc]
        bonus = [jnp.sum(r_h[h] * k_h[h] * rk_ref[:, sls[h]], axis=-1, keepdims=True) * v_h[h] for h in heads]
        for h, sl in enumerate(sls):
            yn = yc[h] * lax.rsqrt(var[h] + LN_X_EPS) * lng_ref[:, sl] + lnb_ref[:, sl]
            o_ref[bi:bi + 1, sl] = ((yn + bonus[h]) * g_h[h]).astype(o_ref.dtype)


def _rwkv_step(pr, prev, s0, rw):
    b = pr.shape[0]
    return pl.pallas_call(
        _rwkv_step_kernel,
        out_shape=[jax.ShapeDtypeStruct((b, C_R), BF), jax.ShapeDtypeStruct((b, H_R, HD_R, HD_R), F32)],
        compiler_params=pltpu.CompilerParams(vmem_limit_bytes=VMEM_LIMIT_BYTES),
        name="rwkv_step",
    )(pr, prev, s0, *rw)


def _cmp_partial_kernel(*refs):
    x_refs, w_ref, o_ref = refs[:-2], refs[-2], refs[-1]
    x = x_refs[0][...] if len(x_refs) == 1 else jnp.concatenate([r[...] for r in x_refs], axis=0)
    half = N_KVG * HD_N
    for s in range(2):
        acc = jnp.zeros((x.shape[0], N_KVG * 2 * CMP_HID), F32)
        for p in range(STRIDE):
            o = p * KV_COLS + s * half
            acc = acc + _dot(x[:, o:o + half], w_ref[p, s])
        o_ref[:, s * N_KVG * 2 * CMP_HID:(s + 1) * N_KVG * 2 * CMP_HID] = acc


CMP_PAIR = 2


def _cmp_partial_paged_kernel(pt_ref, *refs):
    del pt_ref
    x_refs, perm_ref, w_ref, o_ref = refs[:-3], refs[-3], refs[-2], refs[-1]
    cpp = PAGE_SIZE // STRIDE
    width = CMP_PAIR * 2 * CMP_HID
    perm = perm_ref[...]
    for s in range(2):
        for gp in range(N_KVG // CMP_PAIR):
            rows = []
            for x_ref in x_refs:
                tile = jnp.concatenate([x_ref[s, CMP_PAIR * gp + j] for j in range(CMP_PAIR)], axis=0)
                rows.append(_dot_nt(perm, tile))
            acc = jnp.zeros((len(x_refs) * cpp, width), F32)
            for p in range(STRIDE):
                lhs = jnp.concatenate([r[p * cpp:(p + 1) * cpp] for r in rows], axis=0)
                acc = acc + _dot(lhs, w_ref[p, s])
            o = (s * (N_KVG // CMP_PAIR) + gp) * width
            o_ref[:, o:o + width] = acc


def _cmp_weights(cmp_w1, groups):
    w1r = cmp_w1.reshape(2, 2, STRIDE, HD_N, CMP_HID)
    w = jnp.transpose(w1r, (2, 0, 3, 1, 4))
    w = w.reshape(STRIDE, 2, 1, HD_N, 1, 2 * CMP_HID)
    eye = jnp.eye(groups, dtype=w.dtype).reshape(1, 1, groups, 1, groups, 1)
    wbd = eye * w
    return wbd.reshape(STRIDE, 2, groups * HD_N, groups * 2 * CMP_HID).astype(BF)


def _cmp_partial_rows(x, wbd, tr):
    r = x.shape[0]
    n = 2 * N_KVG * 2 * CMP_HID
    return pl.pallas_call(
        _cmp_partial_kernel,
        grid=(r // tr,),
        in_specs=[pl.BlockSpec((tr, STRIDE * KV_COLS), lambda i: (i, 0)),
                  pl.BlockSpec(wbd.shape, lambda i: (0, 0, 0, 0))],
        out_specs=pl.BlockSpec((tr, n), lambda i: (i, 0)),
        out_shape=jax.ShapeDtypeStruct((r, n), F32),
        compiler_params=_cparams("arbitrary"),
        name="cmp_partial",
    )(x, wbd)


PAGES_PER_STEP = 16


def _cmp_partial_paged(cache_t, page_table, wpair):
    b, npg = page_table.shape
    cpp = PAGE_SIZE // STRIDE
    n = 2 * N_KVG * 2 * CMP_HID
    steps = npg // PAGES_PER_STEP
    perm = np.zeros((PAGE_SIZE, PAGE_SIZE), np.float32)
    tok = np.arange(PAGE_SIZE)
    perm[(tok % STRIDE) * cpp + tok // STRIDE, tok] = 1.0

    def page_spec(kpg):
        return pl.BlockSpec((None, 2, N_KVG, HD_N, PAGE_SIZE),
                            lambda i, j, pt: (pt[i, j * PAGES_PER_STEP + kpg], 0, 0, 0, 0))

    grid_spec = pltpu.PrefetchScalarGridSpec(
        num_scalar_prefetch=1,
        grid=(b, steps),
        in_specs=[page_spec(kpg) for kpg in range(PAGES_PER_STEP)]
        + [pl.BlockSpec((PAGE_SIZE, PAGE_SIZE), lambda i, j, pt: (0, 0)),
           pl.BlockSpec(wpair.shape, lambda i, j, pt: (0, 0, 0, 0))],
        out_specs=pl.BlockSpec((None, PAGES_PER_STEP * cpp, n), lambda i, j, pt: (i, j, 0)),
    )
    return pl.pallas_call(
        _cmp_partial_paged_kernel,
        grid_spec=grid_spec,
        out_shape=jax.ShapeDtypeStruct((b, npg * cpp, n), F32),
        compiler_params=_cparams("arbitrary", "arbitrary"),
        name="cmp_partial_paged",
    )(page_table, *([cache_t] * PAGES_PER_STEP), jnp.asarray(perm, BF), wpair)


def _cmp_finish_kernel(c_ref, cpos_ref, w2_ref, o_ref):
    c = c_ref[...]
    nrow = c.shape[0]
    c_next = pltpu.roll(c, nrow - 1, axis=0)
    for s in range(2):
        for gi in range(N_KVG):
            o = (s * N_KVG + gi) * 2 * CMP_HID
            hid = (c[:, o:o + CMP_HID] + cpos_ref[0:1, o:o + CMP_HID]
                   + c_next[:, o + CMP_HID:o + 2 * CMP_HID] + cpos_ref[1:2, o + CMP_HID:o + 2 * CMP_HID])
            oo = (s * N_KVG + gi) * HD_N
            o_ref[:, oo:oo + HD_N] = _dot(_gelu_tanh(hid), w2_ref[s])


def _cmp_finish(c, cpos, w2):
    b, nch, n = c.shape
    return pl.pallas_call(
        _cmp_finish_kernel,
        grid=(b,),
        in_specs=[pl.BlockSpec((None, nch, n), lambda i: (i, 0, 0)), pl.BlockSpec(cpos.shape, lambda i: (0, 0)),
                  pl.BlockSpec(w2.shape, lambda i: (0, 0, 0))],
        out_specs=pl.BlockSpec((None, nch, KV_COLS), lambda i: (i, 0, 0)),
        out_shape=jax.ShapeDtypeStruct((b, nch, KV_COLS), F32),
        compiler_params=_cparams("arbitrary"),
        name="cmp_finish",
    )(c, cpos, w2.astype(BF))


def _cmp_pos_rows(cmp_pos):
    pos = cmp_pos.reshape(2, STRIDE, 1, 1, HD_N)
    rows = jnp.broadcast_to(pos, (2, STRIDE, 2, N_KVG, HD_N)).reshape(2, STRIDE * KV_COLS)
    return jnp.concatenate([rows, jnp.zeros((6, STRIDE * KV_COLS), F32)], axis=0)


def _bias_table_kernel(dist_ref, rb_ref, o_ref):
    bucket = _t5_bucket(dist_ref[...])
    for h in range(H_N):
        out = jnp.zeros(bucket.shape, F32)
        for b in range(N_BUCKETS):
            out = jnp.where(bucket == b, rb_ref[b, h], out)
        o_ref[h] = out


def _bias_table(dist, rel_bias, tr):
    r, n = dist.shape
    return pl.pallas_call(
        _bias_table_kernel,
        grid=(r // tr,),
        in_specs=[pl.BlockSpec((tr, n), lambda i: (i, 0)),
                  pl.BlockSpec(memory_space=pltpu.SMEM)],
        out_specs=pl.BlockSpec((H_N, tr, n), lambda i: (0, i, 0)),
        out_shape=jax.ShapeDtypeStruct((H_N, r, n), F32),
        compiler_params=_cparams("arbitrary"),
        name="bias_table",
    )(dist, rel_bias)


def _softmax_rows(logits, valid):
    lm = jnp.where(valid, logits, NEG_INF)
    e = jnp.exp(lm - jnp.max(lm, axis=-1, keepdims=True))
    return e / jnp.sum(e, axis=-1, keepdims=True)


def _cmp_attn_kernel(q_ref, kv_ref, bias_ref, ovt_ref, o_ref, sel_ref):
    tq = q_ref.shape[0]
    ncp = kv_ref.shape[0]
    nsb = ovt_ref.shape[0]
    q0 = pl.program_id(1) * tq
    qpos = q0 + lax.broadcasted_iota(I32, (tq, 1), 0)
    cend = lax.broadcasted_iota(I32, (1, ncp), 1) * STRIDE + (L_CMP - 1)
    valid = (qpos >= cend) & (lax.broadcasted_iota(I32, (1, ncp), 1) < ncp - 1)
    validf = valid.astype(F32)
    q = q_ref[...] * (HD_N ** -0.5)
    blk = lax.broadcasted_iota(I32, (nsb, tq), 0)
    cur = (q0 + lax.broadcasted_iota(I32, (1, tq), 1)) // L_SEL
    forced = (blk == 0) | (blk == cur) | (blk == cur - 1)
    future = blk > cur
    for gi in range(N_KVG):
        kc = kv_ref[:, gi * HD_N:(gi + 1) * HD_N]
        vc = kv_ref[:, (N_KVG + gi) * HD_N:(N_KVG + gi + 1) * HD_N]
        pcs = jnp.zeros((tq, ncp), F32)
        for hl in range(HPG):
            h = gi * HPG + hl
            sl = slice(h * HD_N, (h + 1) * HD_N)
            pc = _softmax_rows(_dot_nt(q[:, sl], kc) + bias_ref[h], valid) * validf
            pcs = pcs + pc
            o_ref[:, sl] = _dot(pc, vc)
        imp = _dot_nt(ovt_ref[...], pcs)
        score = jnp.where(forced, FORCE, jnp.where(future, -FORCE, imp))
        rank = jnp.zeros((nsb, tq), F32)
        for i in range(nsb):
            si = score[i:i + 1, :]
            rank = rank + ((si > score) | ((si == score) & (i < blk))).astype(F32)
        sel_ref[gi] = (rank < N_TOP).astype(sel_ref.dtype)


def _cmp_sel_overlap_t(nc, ncp, nsb):
    s = np.arange(nc)[None, :] * STRIDE
    j = np.arange(nsb)[:, None] * L_SEL
    ov = np.clip(np.minimum(s + L_CMP, j + L_SEL) - np.maximum(s, j), 0, None) / L_CMP
    return np.pad(ov, ((0, 0), (0, ncp - nc))).astype(np.float32)


def _cmp_attn_prompt(p_n, kv_cmp, bias_c, t):
    b = kv_cmp.shape[0]
    ncp = kv_cmp.shape[1]
    nsb = t // L_SEL
    tq = ATT_TILE
    nqt = t // tq
    ovt = jnp.asarray(_cmp_sel_overlap_t(ncp - 1, ncp, nsb), BF)
    return pl.pallas_call(
        _cmp_attn_kernel,
        grid=(b, nqt),
        in_specs=[pl.BlockSpec((tq, C_N), lambda i, j: (i * nqt + j, 0)),
                  pl.BlockSpec((None, ncp, KV_COLS), lambda i, j: (i, 0, 0)),
                  pl.BlockSpec((H_N, tq, ncp), lambda i, j: (0, j, 0)),
                  pl.BlockSpec((nsb, ncp), lambda i, j: (0, 0))],
        out_specs=[pl.BlockSpec((tq, C_N), lambda i, j: (i * nqt + j, 0)),
                   pl.BlockSpec((None, N_KVG, nsb, tq), lambda i, j: (i, 0, 0, j))],
        out_shape=[jax.ShapeDtypeStruct((b * t, C_N), F32), jax.ShapeDtypeStruct((b, N_KVG, nsb, t), BF)],
        compiler_params=_cparams("arbitrary", "arbitrary"),
        name="nsa_cmp_select",
    )(p_n, kv_cmp, bias_c, ovt)


ATT_TK = 128
ATT_R = ATT_TILE // ATT_TK
N_SEL_OFFS = ATT_R + (MAX_DIST + ATT_TK - 1) // ATT_TK + 1
N_WIN_OFFS = ATT_R + WINDOW // ATT_TK
QA_COLS = HD_N + 32
SEL_STEP_TILES = 4
WIN_STEP_TILES = 2


def _swa_kernel(qa_ref, ks_ref, vs_ref, kw_ref, vw_ref, tabs_ref, tabw_ref, gn_ref, oc_ref, o_ref):
    tq = qa_ref.shape[1]
    qt = pl.program_id(2)
    top = ATT_R * qt + ATT_R - 1

    heads = range(HPG)

    def attend(k_ref, v_ref, tab_ref, lo, n_off, step_tiles):
        tiles = range(step_tiles)

        def body(kp, carry):
            ki = [lo + step_tiles * kp + j for j in tiles]
            kt = [k_ref[jnp.minimum(i, top)] for i in ki]
            vt = [v_ref[jnp.minimum(i, top)] for i in ki]
            off = [jnp.where(i > top, n_off, jnp.minimum(top - i, n_off - 1)) for i in ki]
            s = [[lax.dot_general(kt[j], qa_ref[hl], (((1,), (1,)), ((), ())), preferred_element_type=F32)
                  + tab_ref[off[j], hl * ATT_TK:(hl + 1) * ATT_TK, :] for j in tiles] for hl in heads]
            m_new = [functools.reduce(jnp.maximum, [carry[hl][0]] + [jnp.max(s[hl][j], axis=0, keepdims=True)
                                                                    for j in tiles]) for hl in heads]
            alpha = [jnp.exp(carry[hl][0] - m_new[hl]) for hl in heads]
            p = [[jnp.exp(s[hl][j] - m_new[hl]) for j in tiles] for hl in heads]
            l = [alpha[hl] * carry[hl][1] + sum(jnp.sum(p[hl][j], axis=0, keepdims=True) for j in tiles)
                 for hl in heads]
            acc = [alpha[hl] * carry[hl][2] + sum(jnp.dot(vt[j], p[hl][j].astype(BF), preferred_element_type=F32)
                                                  for j in tiles) for hl in heads]
            return tuple((m_new[hl], l[hl], acc[hl]) for hl in heads)

        init = tuple((jnp.full((1, tq), NEG_INF, F32), jnp.zeros((1, tq), F32), jnp.zeros((HD_N, tq), F32))
                     for _ in heads)
        res = lax.fori_loop(0, (top - lo) // step_tiles + 1, body, init)
        return [acc / l for _, l, acc in res]

    o_sel = attend(ks_ref, vs_ref, tabs_ref, 0, N_SEL_OFFS, SEL_STEP_TILES)
    o_win = attend(kw_ref, vw_ref, tabw_ref, jnp.maximum(top + 1 - N_WIN_OFFS, 0), N_WIN_OFFS, WIN_STEP_TILES)
    gates = _sigmoid(gn_ref[...])
    gates_t = gates.T
    for hl in heads:
        sl = slice(hl * HD_N, (hl + 1) * HD_N)
        o_t = gates_t[3 * hl + 1:3 * hl + 2, :] * o_sel[hl] + gates_t[3 * hl + 2:3 * hl + 3, :] * o_win[hl]
        o_ref[:, sl] = (gates[:, 3 * hl:3 * hl + 1] * oc_ref[:, sl] + o_t.T).astype(o_ref.dtype)


def _swa_prompt(p_n, qa, ks, vs, kw, vw, tab_s, tab_w, o_cmp, t):
    b = qa.shape[0]
    tq = ATT_TILE
    nqt = t // tq
    nkt = t // ATT_TK
    gw = HPG * HD_N
    k_spec = pl.BlockSpec((None, None, nkt, ATT_TK, QA_COLS), lambda i, g, j: (i, g, 0, 0, 0))
    v_spec = pl.BlockSpec((None, None, nkt, HD_N, ATT_TK), lambda i, g, j: (i, g, 0, 0, 0))
    tab_spec = lambda n: pl.BlockSpec((n + 1, HPG * ATT_TK, tq), lambda i, g, j: (0, g, 0))
    return pl.pallas_call(
        _swa_kernel,
        grid=(b, N_KVG, nqt),
        in_specs=[pl.BlockSpec((None, HPG, tq, QA_COLS), lambda i, g, j: (i, g, j, 0)),
                  k_spec, v_spec, k_spec, v_spec, tab_spec(N_SEL_OFFS), tab_spec(N_WIN_OFFS),
                  pl.BlockSpec((tq, GN_GROUP_COLS), lambda i, g, j: (i * nqt + j, NP_GN // GN_GROUP_COLS + g)),
                  pl.BlockSpec((tq, gw), lambda i, g, j: (i * nqt + j, g))],
        out_specs=pl.BlockSpec((tq, gw), lambda i, g, j: (i * nqt + j, g)),
        out_shape=jax.ShapeDtypeStruct((b * t, C_N), BF),
        compiler_params=_cparams("arbitrary", "arbitrary", "arbitrary"),
        name="nsa_sel_win",
    )(qa, ks, vs, kw, vw, tab_s, tab_w, p_n, o_cmp)


def _swa_operands(p_n, sel, b, t):
    nsb = t // L_SEL
    nkt = t // ATT_TK
    q = (p_n[:, :C_N] * (HD_N ** -0.5)).reshape(b, t, H_N, HD_N).transpose(0, 2, 1, 3)
    pen = jnp.where(jnp.transpose(sel, (0, 1, 3, 2)) > 0.5, 0.0, NEG_INF).astype(F32)
    pen = jnp.broadcast_to(pen[:, :, None], (b, N_KVG, HPG, t, nsb)).reshape(b, H_N, t, nsb)
    qa = jnp.concatenate([q, pen], axis=-1).astype(BF)

    def split(cols):
        kv = p_n[:, cols:cols + KV_COLS].reshape(b, t, 2, N_KVG, HD_N)
        k = jnp.transpose(kv[:, :, 0], (0, 2, 1, 3))
        v = jnp.transpose(kv[:, :, 1], (0, 2, 3, 1)).reshape(b, N_KVG, HD_N, nkt, ATT_TK)
        return k, jnp.transpose(v, (0, 1, 3, 2, 4)).astype(BF)

    onehot = jnp.asarray(np.arange(t)[:, None] // L_SEL == np.arange(nsb)[None, :], F32)
    k_s, v_s = split(NP_KVS)
    k_w, v_w = split(NP_KVW)
    ext = lambda k, e: jnp.concatenate([k, jnp.broadcast_to(e, (b, N_KVG, t, nsb))], axis=-1).astype(BF) \
        .reshape(b, N_KVG, nkt, ATT_TK, QA_COLS)
    return qa, ext(k_s, onehot), v_s, ext(k_w, jnp.zeros((t, nsb), F32)), v_w


def _sample_cmp_kernel(q_ref, kv_ref, rbt_ref, ov_ref, o_ref, idx_ref, *, past):
    ncp = kv_ref.shape[0]
    nsbp = ov_ref.shape[1]
    nsb = -(-(past + 1) // L_SEL)
    q = q_ref[...] * (HD_N ** -0.5)
    hrow = lax.broadcasted_iota(I32, (H_N, 1), 0)
    nidx = lax.broadcasted_iota(I32, (1, ncp), 1)
    valid = nidx < ncp - 1
    bias = _bias_rows(past - (nidx * STRIDE + (L_CMP - 1)), rbt_ref[...])
    logits = jnp.zeros((H_N, ncp), F32)
    for gi in range(N_KVG):
        lg = _dot_nt(q, kv_ref[:, gi * HD_N:(gi + 1) * HD_N])
        logits = jnp.where(hrow // HPG == gi, lg, logits)
    pc = _softmax_rows(logits + bias, valid) * valid.astype(F32)
    o = jnp.zeros((H_N, HD_N), F32)
    for gi in range(N_KVG):
        og = _dot(pc, kv_ref[:, (N_KVG + gi) * HD_N:(N_KVG + gi + 1) * HD_N])
        o = jnp.where(hrow // HPG == gi, og, o)
    o_ref[...] = o
    imp_h = _dot(pc, ov_ref[...])
    blk = lax.broadcasted_iota(I32, (8, nsbp), 1)
    grow = lax.broadcasted_iota(I32, (8, 1), 0)
    cur = past // L_SEL
    score = jnp.full((8, nsbp), -3e38, F32)
    for gi in range(N_KVG):
        imp = jnp.sum(jnp.where(hrow // HPG == gi, imp_h, 0.0), axis=0, keepdims=True)
        score = jnp.where(grow == gi, imp, score)
    forced = (blk == 0) | (blk == cur) | (blk == cur - 1)
    score = jnp.where(forced, FORCE, jnp.where(blk > cur, -FORCE, score))
    score = jnp.where((blk < nsb) & (grow < N_KVG), score, -3e38)
    lane = lax.broadcasted_iota(I32, (8, LANES), 1)
    picks = jnp.zeros((8, LANES), I32)
    for it in range(N_TOP):
        mx = jnp.max(score, axis=-1, keepdims=True)
        pick = jnp.min(jnp.where(score == mx, blk, nsbp), axis=-1, keepdims=True)
        picks = jnp.where(lane == it, pick, picks)
        score = jnp.where(blk == pick, -3e38, score)
    idx_ref[...] = picks


def _sample_cmp(q, kv_cmp, rel_bias, past):
    b, ncp, _ = kv_cmp.shape
    nc = ncp - 1
    nsb = -(-(past + 1) // L_SEL)
    nsbp = -(-nsb // LANES) * LANES
    s = np.arange(nc)[:, None] * STRIDE
    j = np.arange(nsb)[None, :] * L_SEL
    ov = np.clip(np.minimum(s + L_CMP, j + L_SEL) - np.maximum(s, j), 0, None) / L_CMP
    ov = np.pad(ov, ((0, ncp - nc), (0, nsbp - nsb))).astype(np.float32)
    return pl.pallas_call(
        functools.partial(_sample_cmp_kernel, past=past),
        grid=(b,),
        in_specs=[pl.BlockSpec((None, H_N, HD_N), lambda i: (i, 0, 0)),
                  pl.BlockSpec((None, ncp, KV_COLS), lambda i: (i, 0, 0)),
                  pl.BlockSpec((H_N, N_BUCKETS), lambda i: (0, 0)),
                  pl.BlockSpec((ncp, nsbp), lambda i: (0, 0))],
        out_specs=[pl.BlockSpec((None, H_N, HD_N), lambda i: (i, 0, 0)),
                   pl.BlockSpec((None, 8, LANES), lambda i: (i, 0, 0))],
        out_shape=[jax.ShapeDtypeStruct((b, H_N, HD_N), F32), jax.ShapeDtypeStruct((b, 8, LANES), I32)],
        compiler_params=_cparams("arbitrary"),
        name="nsa_sample_cmp_select",
    )(q, kv_cmp, rel_bias.T, jnp.asarray(ov, BF))


def _block_copy_kernel(pg_ref, *refs):
    del pg_ref
    x_refs, o_ref = refs[:-1], refs[-1]
    for n, x_ref in enumerate(x_refs):
        o_ref[n] = x_ref[...]


def _gather_sel_pages(cache_t, page):
    rows, nslot = page.shape

    def slot_spec(n):
        return pl.BlockSpec((None, 2, None, HD_N, PAGE_SIZE), lambda i, pg: (pg[i, n], 0, i % N_KVG, 0, 0))

    return pl.pallas_call(
        _block_copy_kernel,
        grid_spec=pltpu.PrefetchScalarGridSpec(
            num_scalar_prefetch=1, grid=(rows,),
            in_specs=[slot_spec(n) for n in range(nslot)],
            out_specs=pl.BlockSpec((None, nslot, 2, HD_N, PAGE_SIZE), lambda i, pg: (i, 0, 0, 0, 0))),
        out_shape=jax.ShapeDtypeStruct((rows, nslot, 2, HD_N, PAGE_SIZE), cache_t.dtype),
        compiler_params=_cparams("arbitrary"),
        name="gather_sel_pages",
    )(page, *([cache_t] * nslot))


def _sample_swa_kernel(idx_ref, q_ref, blk_ref, win_ref, new_ref, rbt_ref, gate_ref, oc_ref, o_ref, *, past):
    bi = pl.program_id(0)
    q = q_ref[...] * (HD_N ** -0.5)
    rbt = rbt_ref[...]
    hrow = lax.broadcasted_iota(I32, (H_N, 1), 0)
    nk = N_TOP * PAGE_SIZE
    lane = lax.broadcasted_iota(I32, (1, nk), 1)
    new_blk = past // L_SEL
    bias_new = rbt[:, 0:1]
    gates = _sigmoid(gate_ref[...])
    nwin = win_ref.shape[0]
    wdist = nwin - lax.broadcasted_iota(I32, (1, nwin), 1)
    wbias = _bias_rows(wdist, rbt)
    wvalid = (wdist >= 0) & (wdist <= WINDOW)

    def with_new(logits, valid, weigh, k_new, v_new):
        l_new = jnp.sum(q * k_new, axis=-1, keepdims=True) + bias_new
        lm = jnp.where(valid, logits, NEG_INF)
        m = jnp.maximum(jnp.max(lm, axis=-1, keepdims=True), l_new)
        e = jnp.where(valid, jnp.exp(lm - m), 0.0)
        e_new = jnp.exp(l_new - m)
        den = jnp.sum(e, axis=-1, keepdims=True) + e_new
        return (weigh(e) + e_new * v_new) / den

    o_sel = jnp.zeros((H_N, HD_N), F32)
    o_win = jnp.zeros((H_N, HD_N), F32)
    bpp = PAGE_SIZE // L_SEL
    tok = lane % PAGE_SIZE
    for gi in range(N_KVG):
        ksl = slice(gi * HD_N, (gi + 1) * HD_N)
        vsl = slice((N_KVG + gi) * HD_N, (N_KVG + gi + 1) * HD_N)
        bid = jnp.zeros((1, nk), I32)
        for n in range(N_TOP):
            bid = jnp.where(lane // PAGE_SIZE == n, idx_ref[bi, gi, n], bid)
        dist = past - ((bid // bpp) * PAGE_SIZE + tok)
        valid = (bid != new_blk) & (tok // L_SEL == bid % bpp) & (dist >= 0)
        kt = jnp.concatenate([blk_ref[gi, n, 0] for n in range(N_TOP)], axis=1)
        vt = jnp.concatenate([blk_ref[gi, n, 1] for n in range(N_TOP)], axis=1)
        logits = _dot(q, kt) + _bias_rows(dist, rbt)
        og = with_new(logits, valid, lambda e, vt=vt: _dot_nt(e, vt), new_ref[0:1, ksl], new_ref[0:1, vsl])
        o_sel = jnp.where(hrow // HPG == gi, og, o_sel)
        logits = _dot_nt(q, win_ref[:, ksl]) + wbias
        og = with_new(logits, wvalid, lambda e, vsl=vsl: _dot(e, win_ref[:, vsl]), new_ref[1:2, ksl],
                      new_ref[1:2, vsl])
        o_win = jnp.where(hrow // HPG == gi, og, o_win)
    o_ref[...] = gates[:, 0:1] * oc_ref[...] + gates[:, 1:2] * o_sel + gates[:, 2:3] * o_win


def _sample_swa(idx, q, blocks, win, new_kv, rel_bias, gates, o_cmp, past):
    b = q.shape[0]
    w = win.shape[1]
    grid_spec = pltpu.PrefetchScalarGridSpec(
        num_scalar_prefetch=1, grid=(b,),
        in_specs=[pl.BlockSpec((None, H_N, HD_N), lambda i, ix: (i, 0, 0)),
                  pl.BlockSpec((None, N_KVG, N_TOP, 2, HD_N, PAGE_SIZE), lambda i, ix: (i, 0, 0, 0, 0, 0)),
                  pl.BlockSpec((None, w, KV_COLS), lambda i, ix: (i, 0, 0)),
                  pl.BlockSpec((None, 2, KV_COLS), lambda i, ix: (i, 0, 0)),
                  pl.BlockSpec((H_N, N_BUCKETS), lambda i, ix: (0, 0)),
                  pl.BlockSpec((None, H_N, 3), lambda i, ix: (i, 0, 0)),
                  pl.BlockSpec((None, H_N, HD_N), lambda i, ix: (i, 0, 0))],
        out_specs=pl.BlockSpec((None, H_N, HD_N), lambda i, ix: (i, 0, 0)))
    return pl.pallas_call(
        functools.partial(_sample_swa_kernel, past=past),
        grid_spec=grid_spec,
        out_shape=jax.ShapeDtypeStruct((b, H_N, HD_N), F32),
        compiler_params=_cparams("arbitrary"),
        name="nsa_sample_sel_win",
    )(idx, q, blocks, win, new_kv, rel_bias.T, gates, o_cmp)


ROUTER_COLS = LANES
MOE_ROWS_PROMPT = 128
MOE_ROWS_SAMPLE = 16


def _router_kernel(x_ref, g_ref, sh_ref, sc_ref, w_ref, b_ref, h_ref, e_ref, wt_ref, rk_ref, cnt_ref):
    i = pl.program_id(0)
    tm = x_ref.shape[0]

    @pl.when(i == 0)
    def _():
        cnt_ref[...] = jnp.zeros_like(cnt_ref)

    h = (_rms(x_ref[...], g_ref[...]) * (1.0 + sc_ref[...]) + sh_ref[...]).astype(BF)
    _store_folded(h_ref, h.astype(F32))
    logits = jnp.dot(h, w_ref[...], preferred_element_type=F32) + b_ref[...]
    lane = lax.broadcasted_iota(I32, (tm, ROUTER_COLS), 1)

    def top1(vals, ok):
        vm = jnp.where(ok, vals, -3e38)
        mx = jnp.max(vm, axis=-1, keepdims=True)
        return mx, jnp.min(jnp.where(ok & (vm == mx), lane, ROUTER_COLS), axis=-1, keepdims=True)

    isg = lane < N_EGROUPS
    pg = _softmax_rows(logits, isg)
    g_w, g_i = top1(pg, isg)
    ise = (lane >= N_EGROUPS) & ((lane - N_EGROUPS) // EXP_PER_GROUP == g_i)
    pe = _softmax_rows(logits, ise)
    w0, l0 = top1(pe, ise)
    w1, l1 = top1(pe, ise & (lane != l0))
    den = w0 + w1
    e0 = l0 - N_EGROUPS
    e1 = l1 - N_EGROUPS
    e_ref[...] = jnp.where(lane == 0, e0, jnp.where(lane == 1, e1, 0))
    wt_ref[...] = jnp.where(lane == 0, w0 / den * g_w, jnp.where(lane == 1, w1 / den * g_w, 0.0))
    oh0 = (lane == e0).astype(F32)
    oh1 = (lane == e1).astype(F32)
    cnt = oh0 + oh1
    ti = lax.broadcasted_iota(I32, (tm, tm), 0)
    si = lax.broadcasted_iota(I32, (tm, tm), 1)
    before = _dot((ti > si).astype(F32), cnt) + cnt_ref[...]
    r0 = jnp.sum(before * oh0, axis=-1, keepdims=True)
    r1 = jnp.sum(before * oh1, axis=-1, keepdims=True)
    rk_ref[...] = jnp.where(lane == 0, r0, jnp.where(lane == 1, r1, 0.0)).astype(I32)
    cnt_ref[...] = cnt_ref[...] + jnp.sum(cnt, axis=0, keepdims=True)


def _router(x, g, shift, scale, w_r, b_r, tm, rpb):
    m = x.shape[0]
    r = shift.shape[1]
    rows = lambda tn: pl.BlockSpec((tm, tn), lambda i: (i, 0))
    mods = pl.BlockSpec((None, r, D_MODEL), lambda i: ((i * tm) // rpb, 0, 0))
    small = lambda dt: jax.ShapeDtypeStruct((m, ROUTER_COLS), dt)
    return pl.pallas_call(
        _router_kernel,
        grid=(m // tm,),
        in_specs=[rows(D_MODEL), pl.BlockSpec((1, D_MODEL), lambda i: (0, 0)), mods, mods,
                  pl.BlockSpec((D_MODEL, ROUTER_COLS), lambda i: (0, 0)),
                  pl.BlockSpec((1, ROUTER_COLS), lambda i: (0, 0))],
        out_specs=[pl.BlockSpec((tm * ROW_FOLD, LANES), lambda i: (i, 0)),
                   rows(ROUTER_COLS), rows(ROUTER_COLS), rows(ROUTER_COLS),
                   pl.BlockSpec((1, ROUTER_COLS), lambda i: (0, 0))],
        out_shape=[jax.ShapeDtypeStruct((m * ROW_FOLD, LANES), F32), small(I32), small(F32), small(I32),
                   jax.ShapeDtypeStruct((1, ROUTER_COLS), F32)],
        compiler_params=_cparams("arbitrary"),
        name="moe_router",
    )(x, g.reshape(1, D_MODEL), shift, scale, w_r, b_r)


ROW_FOLD = D_MODEL // LANES
ROW_PITCH = ROW_FOLD + 8


def _store_folded(ref, x):
    n = x.shape[0]
    for c in range(ROW_FOLD):
        ref[pl.ds(c, n, stride=ROW_FOLD), :] = x[:, c * LANES:(c + 1) * LANES]


def _load_folded(ref, first_row, n):
    return jnp.concatenate([ref[pl.ds(first_row * ROW_PITCH + c, n, stride=ROW_PITCH), :] for c in range(ROW_FOLD)],
                           axis=1)


def _row_gather_ring(src_hbm, buf, sems, groups, idx_now, idx_next, inline_next=False):
    i = pl.program_id(0)
    last = pl.num_programs(0) - 1
    slot = i % 2
    total = sum(cnt for _, cnt, _ in groups)
    assert 2 * total * ROW_PITCH == buf.shape[0]

    def rows_at(row, pitch):
        return pl.ds(pl.multiple_of(row * pitch, 8), ROW_FOLD)

    def start(idx, s, first, k, r, priority):
        pltpu.make_async_copy(src_hbm.at[rows_at(idx(k, r), ROW_FOLD)],
                              buf.at[rows_at(s * total + first + r, ROW_PITCH)], sems.at[s]).start(priority=priority)

    def start_all(idx, s):
        for first, cnt, k in groups:
            per_trip = math.gcd(cnt, 8)

            def issue(j, c):
                for u in range(per_trip):
                    start(idx, s, first, k, per_trip * j + u, u % 2)
                return c
            lax.fori_loop(0, cnt // per_trip, issue, 0)

    def wait_slot(s):
        span = buf.at[pl.ds(pl.multiple_of(s * total * ROW_PITCH, 8), total * ROW_FOLD)]
        pltpu.make_async_copy(span, span, sems.at[s]).wait()

    @pl.when(i == 0)
    def _():
        start_all(idx_now, slot)

    def finish():
        if inline_next:
            @pl.when(i == last)
            def _():
                wait_slot(1 - slot)

    if inline_next:
        wait_slot(slot)
        for first, cnt, k in groups:
            for r in range(cnt):
                start(idx_next, 1 - slot, first, k, r, r % 2)
    else:
        @pl.when(i < last)
        def _():
            start_all(idx_next, 1 - slot)

        wait_slot(slot)
    return slot * total, finish


def _expert_kernel(be_ref, rt_ref, rtn_ref, h_hbm, w1_ref, w3_ref, w2_ref, o_ref, xbuf, sems, w1b, w3b, w2b):
    i = pl.program_id(0)
    blk = rt_ref.shape[1]

    @pl.when((i == 0) | (be_ref[i] != be_ref[jnp.maximum(i - 1, 0)]))
    def _():
        w1b[...] = w1_ref[...].astype(BF)
        w3b[...] = w3_ref[...].astype(BF)
        w2b[...] = w2_ref[...].astype(BF)

    base, finish = _row_gather_ring(h_hbm, xbuf, sems, ((0, blk, 0),), lambda k, r: rt_ref[0, r],
                                    lambda k, r: rtn_ref[0, r], inline_next=True)
    x = _load_folded(xbuf, base, blk).astype(BF)
    a = jnp.dot(x, w1b[...], preferred_element_type=F32)
    b = jnp.dot(x, w3b[...], preferred_element_type=F32)
    hid = a * _sigmoid(a) * b
    _store_folded(o_ref, jnp.dot(hid.astype(BF), w2b[...], preferred_element_type=F32))
    finish()


def _experts(h2, row_tok, blk_exp, w1, w3, w2):
    nblk, _, blk = row_tok.shape
    idx_spec = lambda d: pl.BlockSpec((None, 1, blk), lambda i, be: (jnp.minimum(i + d, nblk - 1), 0, 0),
                                      memory_space=pltpu.SMEM)
    grid_spec = pltpu.PrefetchScalarGridSpec(
        num_scalar_prefetch=1, grid=(nblk,),
        in_specs=[idx_spec(0), idx_spec(1),
                  pl.BlockSpec(memory_space=pl.ANY),
                  pl.BlockSpec((None, D_MODEL, D_EXP), lambda i, be: (be[i], 0, 0)),
                  pl.BlockSpec((None, D_MODEL, D_EXP), lambda i, be: (be[i], 0, 0)),
                  pl.BlockSpec((None, D_EXP, D_MODEL), lambda i, be: (be[i], 0, 0))],
        out_specs=pl.BlockSpec((blk * ROW_FOLD, LANES), lambda i, be: (i, 0)),
        scratch_shapes=[pltpu.VMEM((2 * blk * ROW_PITCH, LANES), F32), pltpu.SemaphoreType.DMA((2,)),
                        pltpu.VMEM((D_MODEL, D_EXP), BF), pltpu.VMEM((D_MODEL, D_EXP), BF),
                        pltpu.VMEM((D_EXP, D_MODEL), BF)])
    return pl.pallas_call(
        _expert_kernel,
        grid_spec=grid_spec,
        out_shape=jax.ShapeDtypeStruct((nblk * blk * ROW_FOLD, LANES), F32),
        compiler_params=_cparams("arbitrary"),
        name="moe_experts",
    )(blk_exp, row_tok, row_tok, h2, w1, w3, w2)


def _final_kernel(x_ref, g_ref, dest_ref, destn_ref, ys_hbm, wt_ref, nf_ref, o_ref, ybuf, sems):
    tm = x_ref.shape[0]
    groups = tuple((k * tm, tm, k) for k in range(TOP_K))
    base, _ = _row_gather_ring(ys_hbm, ybuf, sems, groups, lambda k, r: dest_ref[k, r], lambda k, r: destn_ref[k, r])
    wt = wt_ref[...]
    moe = wt[:, 0:1] * _load_folded(ybuf, base, tm) + wt[:, 1:2] * _load_folded(ybuf, base + tm, tm)
    o_ref[...] = _rms(x_ref[...] + g_ref[...] * moe, nf_ref[...])


def _final(x, gate, ys, dest, wts, norm_f, tm, rpb):
    m = x.shape[0]
    r = gate.shape[1]
    nt = m // tm
    rows = lambda tn: pl.BlockSpec((tm, tn), lambda i: (i, 0))
    idx_spec = lambda d: pl.BlockSpec((None, TOP_K, tm), lambda i: (jnp.minimum(i + d, nt - 1), 0, 0),
                                      memory_space=pltpu.SMEM)
    return pl.pallas_call(
        _final_kernel,
        grid=(nt,),
        in_specs=[rows(D_MODEL), pl.BlockSpec((None, r, D_MODEL), lambda i: ((i * tm) // rpb, 0, 0)),
                  idx_spec(0), idx_spec(1),
                  pl.BlockSpec(memory_space=pl.ANY),
                  rows(ROUTER_COLS), pl.BlockSpec((1, D_MODEL), lambda i: (0, 0))],
        out_specs=rows(D_MODEL),
        out_shape=jax.ShapeDtypeStruct((m, D_MODEL), F32),
        scratch_shapes=[pltpu.VMEM((2 * TOP_K * tm * ROW_PITCH, LANES), F32), pltpu.SemaphoreType.DMA((2,))],
        compiler_params=_cparams("arbitrary"),
        name="moe_combine_final_norm",
    )(x, gate, dest, dest, ys, wts, norm_f.reshape(1, D_MODEL))


def _moe_and_final(x1, g2, shift, scale, gate, w_r, b_r, exp_w1, exp_w3, exp_w2, norm_f, tm, rpb, blk):
    m = x1.shape[0]
    h2, eid, wts, rank, counts = _router(x1, g2, shift, scale, w_r, b_r, tm, rpb)
    counts = counts[0, :N_EXP].astype(I32)
    padded = (counts + blk - 1) // blk * blk
    pend = jnp.cumsum(padded)
    pstart = pend - padded
    n_blocks = -(-(m * TOP_K) // blk) + N_EXP
    starts = jnp.arange(n_blocks, dtype=I32)[:, None] * blk
    blk_exp = jnp.minimum(jnp.sum((pend[None, :] <= starts).astype(I32), axis=1), N_EXP - 1)
    e = eid[:, :TOP_K]
    dest = pstart[e] + rank[:, :TOP_K]
    tok = jnp.broadcast_to(jnp.arange(m, dtype=I32)[:, None], (m, TOP_K))
    row_tok = jnp.zeros((n_blocks * blk,), I32).at[dest.reshape(-1)].set(tok.reshape(-1))
    ys = _experts(h2, row_tok.reshape(n_blocks, 1, blk), blk_exp, exp_w1, exp_w3, exp_w2)
    dest_t = jnp.transpose(dest.reshape(m // tm, tm, TOP_K), (0, 2, 1))
    return _final(x1, gate, ys, dest_t, wts, norm_f, tm, rpb)


def _pack_in_proj(w_in):
    o = C_RIN
    w_r = w_in[:, :o]
    w_q = w_in[:, o:o + C_N + 3 * KV_COLS]
    o += C_N + 3 * KV_COLS
    w_gn = w_in[:, o:o + 3 * H_N].reshape(D_MODEL, N_KVG, 3 * HPG)
    w_gn = jnp.pad(w_gn, ((0, 0), (0, 0), (0, GN_GROUP_COLS - 3 * HPG))).reshape(D_MODEL, N_KVG * GN_GROUP_COLS)
    o += 3 * H_N
    w_gm = w_in[:, o:]
    return w_r.astype(BF), jnp.concatenate([w_q, w_gn], axis=1).astype(BF), w_gm.astype(BF)


def _prompt_bias_tables(rel_bias, t):
    tq, tk = ATT_TILE, ATT_TK
    i = np.arange(tq)[None, :]
    j = np.arange(tk)[:, None]
    dist = np.stack([tk * (o - (ATT_R - 1)) + i - j for o in range(N_WIN_OFFS)]).astype(np.int32)
    raw = _bias_table(jnp.asarray(dist.reshape(N_WIN_OFFS * tk, tq)), rel_bias, tk)
    raw = jnp.transpose(raw.reshape(H_N, N_WIN_OFFS, tk, tq), (1, 0, 2, 3))
    ok_w = jnp.asarray((dist >= 0) & (dist <= WINDOW))[:, None]
    ok_s = jnp.asarray(dist[:N_SEL_OFFS] >= 0)[:, None]
    masked = jnp.full((1, H_N * tk, tq), NEG_INF, F32)
    tab_w = jnp.concatenate([jnp.where(ok_w, raw, NEG_INF).reshape(N_WIN_OFFS, H_N * tk, tq), masked])
    tab_s = jnp.concatenate([jnp.where(ok_s, raw[:N_SEL_OFFS], NEG_INF).reshape(N_SEL_OFFS, H_N * tk, tq), masked])
    nc = (t - L_CMP) // STRIDE + 1
    ncp = nc + 1
    dc = (np.arange(t)[:, None] - (np.arange(ncp)[None, :] * STRIDE + L_CMP - 1)).astype(np.int32)
    return tab_s, tab_w, _bias_table(jnp.asarray(dc), rel_bias, tq)


def kernel(x_prompt, x_sample, c_prompt, c_sample, cache_cmp_kv, cache_sel_kv, state_win_kv, state_rwkv_shift,
           state_rwkv_wkv, page_table, rel_bias, norm_f, norm1, norm2, w_ada, b_ada, w_in, rwkv_mu, rwkv_w0, rwkv_w2,
           rwkv_a0, rwkv_a2, rwkv_g2, rwkv_kk, rwkv_ka, rwkv_rk, rwkv_ln_g, rwkv_ln_b, cmp_pos, cmp_w1, cmp_w2,
           w_o_rwkv, w_o_nsa, w_out, router_wg, router_bg, router_we, router_be, exp_w1, exp_w3, exp_w2):
    bp, t, _ = x_prompt.shape
    bs = x_sample.shape[0]
    mp = bp * t
    past = page_table.shape[1] * PAGE_SIZE

    nrow = -(-(bp + bs) // 8) * 8
    c_all = jnp.concatenate([c_prompt, c_sample, jnp.zeros((nrow - bp - bs, D_MODEL), F32)], axis=0)
    mod = _ada(c_all, w_ada[0], b_ada[0]).reshape(nrow, 6, D_MODEL)
    mod_p = [mod[:bp, i][:, None, :] for i in range(6)]
    mod_s = [mod[bp:bp + bs, i][None] for i in range(6)]

    w_r, w_n, w_gm = _pack_in_proj(w_in[0])
    rw = _rwkv_weights(rwkv_mu[0], rwkv_w0[0], rwkv_w2[0], rwkv_a0[0], rwkv_a2[0], rwkv_g2[0], rwkv_kk[0],
                       rwkv_ka[0], rwkv_rk[0], rwkv_ln_g[0], rwkv_ln_b[0])
    wbd = _cmp_weights(cmp_w1[0], N_KVG)
    cpos = _cmp_partial_rows(_cmp_pos_rows(cmp_pos[0]), wbd, 8)
    wo_r, wo_n, wo = w_o_rwkv[0].astype(BF), w_o_nsa[0].astype(BF), w_out[0].astype(BF)
    w_router = jnp.pad(jnp.concatenate([router_wg[0], router_we[0]], axis=1),
                       ((0, 0), (0, ROUTER_COLS - N_EGROUPS - N_EXP))).astype(BF)
    b_router = jnp.pad(jnp.concatenate([router_bg[0], router_be[0]]), (0, ROUTER_COLS - N_EGROUPS - N_EXP))[None]

    tm = 512
    xp = x_prompt.reshape(mp, D_MODEL)
    h = _norm_mod(xp, norm1[0], mod_p[0], mod_p[1], tm, t)
    p_r = _matmul(h, w_r, tm, C_RIN // 2)
    p_n = _matmul(h, w_n, tm, NP_COLS // 2)
    p_g = _matmul(h, w_gm, tm, 2048)
    o_r, shift_p, wkv_p = _rwkv_prompt(p_r.reshape(bp, t, C_RIN), rw)
    kvc = p_n[:, NP_KVC:NP_KVC + KV_COLS]
    kvs = p_n[:, NP_KVS:NP_KVS + KV_COLS]
    kvw = p_n[:, NP_KVW:NP_KVW + KV_COLS]
    nch = t // STRIDE
    c_part = _cmp_partial_rows(kvc.reshape(bp * nch, STRIDE * KV_COLS), wbd, nch)
    kv_cmp = _cmp_finish(c_part.reshape(bp, nch, -1), cpos, cmp_w2[0])
    tab_s, tab_w, bias_c = _prompt_bias_tables(rel_bias, t)
    o_cmp, sel = _cmp_attn_prompt(p_n, kv_cmp, bias_c, t)
    o_n = _swa_prompt(p_n, *_swa_operands(p_n, sel, bp, t), tab_s, tab_w, o_cmp, t)
    y = _merge(o_r.reshape(mp, C_R), o_n, wo_r, wo_n, p_g, tm)
    x1 = _proj_residual(y, wo, xp, mod_p[2], tm, t)
    y_prompt = _moe_and_final(x1, norm2[0], mod_p[3], mod_p[4], mod_p[5], w_router, b_router, exp_w1[0], exp_w3[0],
                              exp_w2[0], norm_f, tm, t, MOE_ROWS_PROMPT).reshape(bp, t, D_MODEL)
    kv_shape = (1, bp, t, 2, N_KVG, HD_N)
    wlen = min(WINDOW, t)
    win_p = kvw.reshape(bp, t, KV_COLS)[:, t - wlen:].reshape(1, bp, wlen, 2, N_KVG, HD_N)

    xs = x_sample.reshape(bs, D_MODEL)
    hs = _norm_mod(xs, norm1[0], mod_s[0], mod_s[1], bs, bs)
    ps_r = _matmul(hs, w_r, bs, C_RIN // 2)
    ps_n = _matmul(hs, w_n, bs, NP_COLS // 2)
    ps_g = _matmul(hs, w_gm, bs, 2048)
    os_r, wkv_s = _rwkv_step(ps_r, state_rwkv_shift[0], state_rwkv_wkv[0], rw)
    kvc_s = ps_n[:, NP_KVC:NP_KVC + KV_COLS]
    kvs_s = ps_n[:, NP_KVS:NP_KVS + KV_COLS]
    kvw_s = ps_n[:, NP_KVW:NP_KVW + KV_COLS]
    cs_part = _cmp_partial_paged(jnp.transpose(cache_cmp_kv[0], (0, 2, 3, 4, 1)), page_table,
                                 _cmp_weights(cmp_w1[0], CMP_PAIR))
    kv_cmp_s = _cmp_finish(cs_part, cpos, cmp_w2[0])
    q_s = ps_n[:, :C_N].reshape(bs, H_N, HD_N)
    o_cmp_s, picks = _sample_cmp(q_s, kv_cmp_s, rel_bias, past)
    idx = picks[:, :N_KVG, :N_TOP]
    bpp = PAGE_SIZE // L_SEL
    npb = past // L_SEL
    idc = jnp.minimum(idx, npb - 1)
    page = jnp.take_along_axis(page_table, (idc // bpp).reshape(bs, -1), axis=1).reshape(bs * N_KVG, N_TOP)
    cache_t = jnp.transpose(cache_sel_kv[0], (0, 2, 3, 4, 1))
    blocks = _gather_sel_pages(cache_t, page).reshape(bs, N_KVG, N_TOP, 2, HD_N, PAGE_SIZE)
    win_buf = state_win_kv[0].reshape(bs, -1, KV_COLS)
    gates_s = ps_n[:, NP_GN:].reshape(bs, N_KVG, GN_GROUP_COLS)[:, :, :3 * HPG].reshape(bs, H_N, 3)
    new_kv = jnp.stack([kvs_s, kvw_s], axis=1)
    os_n = _sample_swa(idx, q_s, blocks, win_buf, new_kv, rel_bias, gates_s, o_cmp_s, past)
    ys = _merge(os_r, os_n.reshape(bs, C_N).astype(BF), wo_r, wo_n, ps_g, bs)
    xs1 = _proj_residual(ys, wo, xs, mod_s[2], bs, bs)
    y_sample = _moe_and_final(xs1, norm2[0], mod_s[3], mod_s[4], mod_s[5], w_router, b_router, exp_w1[0], exp_w3[0],
                              exp_w2[0], norm_f, bs, bs, MOE_ROWS_SAMPLE).reshape(bs, 1, D_MODEL)
    kv1 = (1, bs, 1, 2, N_KVG, HD_N)
    wbuf = win_buf.shape[1]
    win_s = jnp.concatenate([win_buf, kvw_s[:, None, :]], axis=1)[:, -wbuf:].reshape(1, bs, wbuf, 2, N_KVG, HD_N)

    return (y_prompt, y_sample,
            kvc.reshape(kv_shape), kvc_s.reshape(kv1),
            kvs.reshape(kv_shape), kvs_s.reshape(kv1),
            win_p, win_s,
            shift_p.reshape(1, bp, C_RIN), ps_r.reshape(1, bs, C_RIN),
            wkv_p[None], wkv_s[None])
```

```python
import functools
import math

import numpy as np
import jax
import jax.numpy as jnp
from jax import lax
from jax.experimental import pallas as pl
from jax.experimental.pallas import tpu as pltpu

D_MODEL = 2048
PAGE_SIZE = 128
H_R, HD_R = 16, 64
C_R = H_R * HD_R
LORA_W, LORA_A, LORA_G = 64, 64, 128
C_RIN = 3 * C_R + LORA_W + LORA_A + LORA_G
LN_X_EPS = 64e-5
H_N, HD_N, N_KVG = 16, 64, 4
HPG = H_N // N_KVG
C_N = H_N * HD_N
KV_COLS = 2 * N_KVG * HD_N
L_CMP, STRIDE, CMP_HID = 32, 16, 64
L_SEL, N_TOP, WINDOW = 64, 16, 512
N_BUCKETS, MAX_DIST = 32, 128
N_EGROUPS, EXP_PER_GROUP = 4, 8
N_EXP = N_EGROUPS * EXP_PER_GROUP
TOP_K, D_EXP = 2, 512
RMS_EPS = 1e-6
NEG_INF = -1e30
FORCE = 1e9

BF = jnp.bfloat16
F32 = jnp.float32
I32 = jnp.int32

VMEM_LIMIT_BYTES = 56 * 1024 * 1024
LANES = 128
RW_CHUNK = 32
ATT_TILE = 256
GN_GROUP_COLS = 128
NP_Q, NP_KVC, NP_KVS, NP_KVW, NP_GN = 0, C_N, C_N + KV_COLS, C_N + 2 * KV_COLS, C_N + 3 * KV_COLS
NP_COLS = NP_GN + N_KVG * GN_GROUP_COLS


def _cparams(*sem):
    return pltpu.CompilerParams(dimension_semantics=sem, vmem_limit_bytes=VMEM_LIMIT_BYTES)


def _dot(a, b):
    return jnp.dot(a.astype(BF), b.astype(BF), preferred_element_type=F32)


def _dot_nt(a, b):
    return lax.dot_general(a.astype(BF), b.astype(BF), (((1,), (1,)), ((), ())), preferred_element_type=F32)


def _dot_tn(a, b):
    return lax.dot_general(a.astype(BF), b.astype(BF), (((0,), (0,)), ((), ())), preferred_element_type=F32)


def _softplus(x):
    return jnp.maximum(x, 0.0) + jnp.log1p(jnp.exp(-jnp.abs(x)))


def _sigmoid(x):
    return 1.0 / (1.0 + jnp.exp(-x))


def _gelu_tanh(x):
    return 0.5 * x * (1.0 + jnp.tanh(math.sqrt(2.0 / math.pi) * (x + 0.044715 * x * x * x)))


def _t5_bucket(dist):
    n = jnp.maximum(dist, 0)
    max_exact = N_BUCKETS // 2
    nf = jnp.maximum(n, 1).astype(F32)
    large = max_exact + (jnp.log(nf / max_exact) / math.log(MAX_DIST / max_exact)
                         * (N_BUCKETS - max_exact)).astype(I32)
    large = jnp.minimum(large, N_BUCKETS - 1)
    return jnp.where(n < max_exact, n, large)


def _bias_rows(dist, rbt):
    bucket = _t5_bucket(dist)
    out = jnp.zeros((rbt.shape[0], dist.shape[1]), F32)
    for b in range(N_BUCKETS):
        out = jnp.where(bucket == b, rbt[:, b:b + 1], out)
    return out


def _ada_kernel(c_ref, w_ref, b_ref, o_ref):
    o_ref[...] = _dot(c_ref[...], w_ref[...]) + b_ref[...]


def _ada(c, w_ada, b_ada):
    r = c.shape[0]
    n = w_ada.shape[1]
    tn = 1024
    return pl.pallas_call(
        _ada_kernel,
        grid=(n // tn,),
        in_specs=[pl.BlockSpec((r, D_MODEL), lambda j: (0, 0)),
                  pl.BlockSpec((D_MODEL, tn), lambda j: (0, j)),
                  pl.BlockSpec((1, tn), lambda j: (0, j))],
        out_specs=pl.BlockSpec((r, tn), lambda j: (0, j)),
        out_shape=jax.ShapeDtypeStruct((r, n), F32),
        compiler_params=_cparams("arbitrary"),
        name="ada_mod",
    )(c, w_ada, b_ada.reshape(1, n))


def _rms(x, g):
    return x * lax.rsqrt(jnp.mean(x * x, axis=-1, keepdims=True) + RMS_EPS) * g


def _norm_mod_kernel(x_ref, g_ref, sh_ref, sc_ref, o_ref):
    o_ref[...] = (_rms(x_ref[...], g_ref[...]) * (1.0 + sc_ref[...]) + sh_ref[...]).astype(o_ref.dtype)


def _row_specs(m, tm, rpb):
    del m
    return (lambda tn: pl.BlockSpec((tm, tn), lambda i, j: (i, j)),
            lambda r, tn: pl.BlockSpec((None, r, tn), lambda i, j: ((i * tm) // rpb, 0, j)))


def _norm_mod(x, g, shift, scale, tm, rpb):
    m = x.shape[0]
    r = shift.shape[1]
    rows, mods = _row_specs(m, tm, rpb)
    return pl.pallas_call(
        _norm_mod_kernel,
        grid=(m // tm, 1),
        in_specs=[rows(D_MODEL), pl.BlockSpec((1, D_MODEL), lambda i, j: (0, 0)), mods(r, D_MODEL), mods(r, D_MODEL)],
        out_specs=rows(D_MODEL),
        out_shape=jax.ShapeDtypeStruct((m, D_MODEL), BF),
        compiler_params=_cparams("arbitrary", "arbitrary"),
        name="norm_mod",
    )(x, g.reshape(1, D_MODEL), shift, scale)


def _mm_kernel(a_ref, w_ref, o_ref):
    o_ref[...] = jnp.dot(a_ref[...], w_ref[...], preferred_element_type=F32).astype(o_ref.dtype)


def _matmul(a, w, tm, tn, out_dtype=F32):
    m, k = a.shape
    n = w.shape[1]
    return pl.pallas_call(
        _mm_kernel,
        grid=(m // tm, n // tn),
        in_specs=[pl.BlockSpec((tm, k), lambda i, j: (i, 0)), pl.BlockSpec((k, tn), lambda i, j: (0, j))],
        out_specs=pl.BlockSpec((tm, tn), lambda i, j: (i, j)),
        out_shape=jax.ShapeDtypeStruct((m, n), out_dtype),
        compiler_params=_cparams("arbitrary", "arbitrary"),
        name="matmul",
    )(a, w)


def _merge_kernel(or_ref, on_ref, wr_ref, wn_ref, g0_ref, g1_ref, o_ref):
    yr = jnp.dot(or_ref[...], wr_ref[...], preferred_element_type=F32)
    yn = jnp.dot(on_ref[...], wn_ref[...], preferred_element_type=F32)
    o_ref[...] = (_sigmoid(g0_ref[...]) * yr + _sigmoid(g1_ref[...]) * yn).astype(o_ref.dtype)


def _merge(o_r, o_n, w_r, w_n, p_g, tm):
    m = o_r.shape[0]
    tn = 1024
    nb = D_MODEL // tn
    return pl.pallas_call(
        _merge_kernel,
        grid=(m // tm, nb),
        in_specs=[pl.BlockSpec((tm, C_R), lambda i, j: (i, 0)), pl.BlockSpec((tm, C_N), lambda i, j: (i, 0)),
                  pl.BlockSpec((C_R, tn), lambda i, j: (0, j)), pl.BlockSpec((C_N, tn), lambda i, j: (0, j)),
                  pl.BlockSpec((tm, tn), lambda i, j: (i, j)), pl.BlockSpec((tm, tn), lambda i, j: (i, j + nb))],
        out_specs=pl.BlockSpec((tm, tn), lambda i, j: (i, j)),
        out_shape=jax.ShapeDtypeStruct((m, D_MODEL), BF),
        compiler_params=_cparams("arbitrary", "arbitrary"),
        name="merge_branches",
    )(o_r, o_n, w_r, w_n, p_g, p_g)


def _proj_res_kernel(y_ref, w_ref, x_ref, g_ref, o_ref):
    o_ref[...] = x_ref[...] + g_ref[...] * jnp.dot(y_ref[...], w_ref[...], preferred_element_type=F32)


def _proj_residual(y, w, x, gate, tm, rpb):
    m = y.shape[0]
    tn = 1024
    r = gate.shape[1]
    rows, mods = _row_specs(m, tm, rpb)
    return pl.pallas_call(
        _proj_res_kernel,
        grid=(m // tm, D_MODEL // tn),
        in_specs=[pl.BlockSpec((tm, D_MODEL), lambda i, j: (i, 0)), pl.BlockSpec((D_MODEL, tn), lambda i, j: (0, j)),
                  rows(tn), mods(r, tn)],
        out_specs=rows(tn),
        out_shape=jax.ShapeDtypeStruct((m, D_MODEL), F32),
        compiler_params=_cparams("arbitrary", "arbitrary"),
        name="out_proj_residual",
    )(y, w, x, gate)


def _rwkv_features(p, p_prev, mu, w0, w2, a0, a2, g2, k_k, k_a):
    xm = p + (p_prev - p) * mu
    r = xm[:, :C_R]
    k = xm[:, C_R:2 * C_R]
    v = xm[:, 2 * C_R:3 * C_R]
    o = 3 * C_R
    wd = xm[:, o:o + LORA_W]
    ad = xm[:, o + LORA_W:o + LORA_W + LORA_A]
    gd = xm[:, o + LORA_W + LORA_A:]
    w_log = -_softplus(-(w0 + _dot(jnp.tanh(wd), w2))) - 0.5
    lw = -jnp.exp(w_log)
    a = _sigmoid(a0 + _dot(ad, a2))
    g = _dot(_sigmoid(gd), g2)
    kk = k * k_k
    k = k * (1.0 + (a - 1.0) * k_a)
    return r, k, v, lw, a, g, kk


def _head_sums(x, ones2):
    nt = x.shape[1] // LANES
    xs = jnp.concatenate([x[:, j * LANES:(j + 1) * LANES] for j in range(nt)], axis=0)
    hi = xs.astype(BF)
    lo = (xs - hi.astype(F32)).astype(BF)
    s = jnp.dot(hi, ones2, preferred_element_type=F32) + jnp.dot(lo, ones2, preferred_element_type=F32)
    r = x.shape[0]
    return jnp.concatenate([s[j * r:(j + 1) * r] for j in range(nt)], axis=1)


def _rwkv_chunk_kernel(pr_ref, mu_ref, w0_ref, w2_ref, a0_ref, a2_ref, g2_ref, kk_ref, ka_ref, rk_ref,
                       lng_ref, lnb_ref, ones_ref, o_ref, shift_ref, state_ref, y_ref):
    c = pl.program_id(1)
    C = RW_CHUNK

    @pl.when(c == 0)
    def _():
        shift_ref[...] = jnp.zeros_like(shift_ref)
        state_ref[...] = jnp.zeros_like(state_ref)

    p = pr_ref[...]
    row = lax.broadcasted_iota(I32, (C, 1), 0)
    p_prev = jnp.where(row == 0, shift_ref[...], pltpu.roll(p, 1, axis=0))
    shift_ref[...] = p[C - 1:C, :]
    r, k, v, lw, a, g, kk_all = _rwkv_features(p, p_prev, mu_ref[...], w0_ref[...], w2_ref[...], a0_ref[...],
                                               a2_ref[...], g2_ref[...], kk_ref[...], ka_ref[...])
    cl = lw
    s = 1
    while s < C:
        cl = cl + jnp.where(row >= s, pltpu.roll(cl, s, axis=0), 0.0)
        s *= 2
    ti = lax.broadcasted_iota(I32, (C, C), 0)
    si = lax.broadcasted_iota(I32, (C, C), 1)
    strict = ti > si
    incl = ti >= si
    eye = (ti == si).astype(F32)
    heads = range(H_R)
    sls = [slice(h * HD_R, (h + 1) * HD_R) for h in heads]
    ones2 = ones_ref[...]
    kk_n = kk_all / jnp.maximum(jnp.sqrt(_head_sums(kk_all * kk_all, ones2)), 1e-12)
    b_all = kk_n * a
    cl_end = cl[C - 1:C, :]
    e_neg = jnp.exp(-cl)
    e_end = jnp.exp(cl_end - cl)
    g_end_all = jnp.exp(cl_end)
    per_head = lambda z: [z[:, sl] for sl in sls]
    kkt_all = (kk_n * jnp.exp(cl - lw)).astype(BF)
    rt_all = (r * jnp.exp(cl)).astype(BF)
    kr = per_head(jnp.concatenate([kkt_all, rt_all], axis=0))
    kh = per_head((k * e_neg).astype(BF))
    bh = per_head((b_all * e_neg).astype(BF))
    kbb = per_head(jnp.concatenate([(k * e_end).astype(BF), (b_all * e_end).astype(BF)], axis=0))
    vb = per_head(v.astype(BF))
    g_end = per_head(g_end_all)
    ak = [_dot_nt(kr[h], kh[h]) for h in heads]
    ab = [_dot_nt(kr[h], bh[h]) for h in heads]
    lg = [jnp.concatenate([jnp.where(strict, ak[h][:C], 0.0), jnp.where(incl, ak[h][C:], 0.0)], axis=0).astype(BF)
          for h in heads]
    nil = [jnp.where(strict, -ab[h][:C], 0.0) for h in heads]
    grb = [jnp.where(incl, ab[h][C:], 0.0).astype(BF) for h in heads]
    tinv = [eye + n for n in nil]
    m = 2
    while m < C:
        nil = [_dot(n, n) for n in nil]
        tinv = [t + _dot(t, n) for t, n in zip(tinv, nil)]
        m *= 2
    s0 = [state_ref[h] for h in heads]
    xy = [_dot_nt(kr[h], s0[h]) + _dot(lg[h], vb[h]) for h in heads]
    u = [_dot(tinv[h], xy[h][:C]).astype(BF) for h in heads]
    y = [xy[h][C:] - _dot(grb[h], u[h]) for h in heads]
    for h in heads:
        vu = jnp.concatenate([vb[h], -u[h]], axis=0)
        state_ref[h] = s0[h] * g_end[h] + _dot_tn(vu, kbb[h])
    for h, sl in enumerate(sls):
        y_ref[:, sl] = y[h]
    y_all = y_ref[...]
    yc = y_all - _head_sums(y_all, ones2) * (1.0 / HD_R)
    var = _head_sums(yc * yc, ones2) * (1.0 / HD_R)
    yn = yc * lax.rsqrt(var + LN_X_EPS) * lng_ref[...] + lnb_ref[...]
    bonus = _head_sums(r * k * rk_ref[...], ones2) * v
    o_ref[...] = ((yn + bonus) * g).astype(o_ref.dtype)


def _rwkv_weights(mu, w0, w2, a0, a2, g2, k_k, k_a, r_k, ln_g, ln_b):
    row = lambda z: z.reshape(1, -1).astype(F32)
    return (row(mu), row(w0), w2.astype(BF), row(a0), a2.astype(BF), g2.astype(BF), row(k_k), row(k_a), row(r_k),
            row(ln_g), row(ln_b))


_RWKV_W_SHAPES = ((1, C_RIN), (1, C_R), (LORA_W, C_R), (1, C_R), (LORA_A, C_R), (LORA_G, C_R), (1, C_R), (1, C_R),
                  (1, C_R), (1, C_R), (1, C_R))


def _rwkv_prompt(pr, rw):
    b, t, _ = pr.shape
    C = RW_CHUNK
    full = lambda shp: pl.BlockSpec(shp, lambda i, j: (0,) * len(shp))
    lane_head = np.arange(LANES) // HD_R
    ones2 = jnp.asarray(lane_head[:, None] == lane_head[None, :], BF)
    return pl.pallas_call(
        _rwkv_chunk_kernel,
        grid=(b, t // C),
        in_specs=[pl.BlockSpec((None, C, C_RIN), lambda i, j: (i, j, 0))] + [full(s) for s in _RWKV_W_SHAPES]
        + [full((LANES, LANES))],
        out_specs=[pl.BlockSpec((None, C, C_R), lambda i, j: (i, j, 0)),
                   pl.BlockSpec((None, 1, C_RIN), lambda i, j: (i, 0, 0)),
                   pl.BlockSpec((None, H_R, HD_R, HD_R), lambda i, j: (i, 0, 0, 0))],
        out_shape=[jax.ShapeDtypeStruct((b, t, C_R), BF),
                   jax.ShapeDtypeStruct((b, 1, C_RIN), F32),
                   jax.ShapeDtypeStruct((b, H_R, HD_R, HD_R), F32)],
        scratch_shapes=[pltpu.VMEM((C, C_R), F32)],
        compiler_params=_cparams("arbitrary", "arbitrary"),
        name="rwkv_chunk",
    )(pr, *rw, ones2)


def _rwkv_step_kernel(pr_ref, prev_ref, s0_ref, mu_ref, w0_ref, w2_ref, a0_ref, a2_ref, g2_ref, kk_ref, ka_ref,
                      rk_ref, lng_ref, lnb_ref, o_ref, state_ref):
    nb = pr_ref.shape[0]
    r, k, v, lw, a, g, kk_all = _rwkv_features(pr_ref[...], prev_ref[...], mu_ref[...], w0_ref[...], w2_ref[...],
                                               a0_ref[...], a2_ref[...], g2_ref[...], kk_ref[...], ka_ref[...])
    decay = jnp.exp(lw)
    ii = lax.broadcasted_iota(I32, (HD_R, HD_R), 0)
    jj = lax.broadcasted_iota(I32, (HD_R, HD_R), 1)
    eye = ii == jj
    col = lambda z: jnp.sum(jnp.where(eye, z, 0.0), axis=1, keepdims=True)
    heads = range(H_R)
    sls = [slice(h * HD_R, (h + 1) * HD_R) for h in heads]
    for bi in range(nb):
        rows = lambda z: [z[bi:bi + 1, sl] for sl in sls]
        r_h, k_h, v_h, a_h, w_h, g_h, kk_h = rows(r), rows(k), rows(v), rows(a), rows(decay), rows(g), rows(kk_all)
        nrm = [jnp.maximum(jnp.sqrt(jnp.sum(z * z, axis=-1, keepdims=True)), 1e-12) for z in kk_h]
        kk_h = [kk_h[h] / nrm[h] for h in heads]
        s0 = [s0_ref[bi, h] for h in heads]
        sa = [jnp.sum(s0[h] * (-kk_h[h]), axis=1, keepdims=True) for h in heads]
        v_col = [col(z) for z in v_h]
        s1 = [s0[h] * w_h[h] + sa[h] * (kk_h[h] * a_h[h]) + v_col[h] * k_h[h] for h in heads]
        for h in heads:
            state_ref[bi, h] = s1[h]
        y_col = [jnp.sum(s1[h] * r_h[h], axis=1, keepdims=True) for h in heads]
        y = [jnp.sum(jnp.where(eye, z, 0.0), axis=0, keepdims=True) for z in y_col]
        yc = [z - jnp.mean(z, axis=-1, keepdims=True) for z in y]
        var = [jnp.mean(z * z, axis=-1, keepdims=True) for z in yc]
        bonus = [jnp.sum(r_h[h] * k_h[h] * rk_ref[:, sls[h]], axis=-1, keepdims=True) * v_h[h] for h in heads]
        for h, sl in enumerate(sls):
            yn = yc[h] * lax.rsqrt(var[h] + LN_X_EPS) * lng_ref[:, sl] + lnb_ref[:, sl]
            o_ref[bi:bi + 1, sl] = ((yn + bonus[h]) * g_h[h]).astype(o_ref.dtype)


def _rwkv_step(pr, prev, s0, rw):
    b = pr.shape[0]
    return pl.pallas_call(
        _rwkv_step_kernel,
        out_shape=[jax.ShapeDtypeStruct((b, C_R), BF), jax.ShapeDtypeStruct((b, H_R, HD_R, HD_R), F32)],
        compiler_params=pltpu.CompilerParams(vmem_limit_bytes=VMEM_LIMIT_BYTES),
        name="rwkv_step",
    )(pr, prev, s0, *rw)


def _cmp_partial_kernel(*refs):
    x_refs, w_ref, o_ref = refs[:-2], refs[-2], refs[-1]
    x = x_refs[0][...] if len(x_refs) == 1 else jnp.concatenate([r[...] for r in x_refs], axis=0)
    half = N_KVG * HD_N
    for s in range(2):
        acc = jnp.zeros((x.shape[0], N_KVG * 2 * CMP_HID), F32)
        for p in range(STRIDE):
            o = p * KV_COLS + s * half
            acc = acc + _dot(x[:, o:o + half], w_ref[p, s])
        o_ref[:, s * N_KVG * 2 * CMP_HID:(s + 1) * N_KVG * 2 * CMP_HID] = acc


CMP_PAIR = 2


def _cmp_partial_paged_kernel(pt_ref, *refs):
    del pt_ref
    x_refs, perm_ref, w_ref, o_ref = refs[:-3], refs[-3], refs[-2], refs[-1]
    cpp = PAGE_SIZE // STRIDE
    width = CMP_PAIR * 2 * CMP_HID
    perm = perm_ref[...]
    for s in range(2):
        for gp in range(N_KVG // CMP_PAIR):
            rows = []
            for x_ref in x_refs:
                tile = jnp.concatenate([x_ref[s, CMP_PAIR * gp + j] for j in range(CMP_PAIR)], axis=0)
                rows.append(_dot_nt(perm, tile))
            acc = jnp.zeros((len(x_refs) * cpp, width), F32)
            for p in range(STRIDE):
                lhs = jnp.concatenate([r[p * cpp:(p + 1) * cpp] for r in rows], axis=0)
                acc = acc + _dot(lhs, w_ref[p, s])
            o = (s * (N_KVG // CMP_PAIR) + gp) * width
            o_ref[:, o:o + width] = acc


def _cmp_weights(cmp_w1, groups):
    w1r = cmp_w1.reshape(2, 2, STRIDE, HD_N, CMP_HID)
    w = jnp.transpose(w1r, (2, 0, 3, 1, 4))
    w = w.reshape(STRIDE, 2, 1, HD_N, 1, 2 * CMP_HID)
    eye = jnp.eye(groups, dtype=w.dtype).reshape(1, 1, groups, 1, groups, 1)
    wbd = eye * w
    return wbd.reshape(STRIDE, 2, groups * HD_N, groups * 2 * CMP_HID).astype(BF)


def _cmp_partial_rows(x, wbd, tr):
    r = x.shape[0]
    n = 2 * N_KVG * 2 * CMP_HID
    return pl.pallas_call(
        _cmp_partial_kernel,
        grid=(r // tr,),
        in_specs=[pl.BlockSpec((tr, STRIDE * KV_COLS), lambda i: (i, 0)),
                  pl.BlockSpec(wbd.shape, lambda i: (0, 0, 0, 0))],
        out_specs=pl.BlockSpec((tr, n), lambda i: (i, 0)),
        out_shape=jax.ShapeDtypeStruct((r, n), F32),
        compiler_params=_cparams("arbitrary"),
        name="cmp_partial",
    )(x, wbd)


PAGES_PER_STEP = 16


def _cmp_partial_paged(cache_t, page_table, wpair):
    b, npg = page_table.shape
    cpp = PAGE_SIZE // STRIDE
    n = 2 * N_KVG * 2 * CMP_HID
    steps = npg // PAGES_PER_STEP
    perm = np.zeros((PAGE_SIZE, PAGE_SIZE), np.float32)
    tok = np.arange(PAGE_SIZE)
    perm[(tok % STRIDE) * cpp + tok // STRIDE, tok] = 1.0

    def page_spec(kpg):
        return pl.BlockSpec((None, 2, N_KVG, HD_N, PAGE_SIZE),
                            lambda i, j, pt: (pt[i, j * PAGES_PER_STEP + kpg], 0, 0, 0, 0))

    grid_spec = pltpu.PrefetchScalarGridSpec(
        num_scalar_prefetch=1,
        grid=(b, steps),
        in_specs=[page_spec(kpg) for kpg in range(PAGES_PER_STEP)]
        + [pl.BlockSpec((PAGE_SIZE, PAGE_SIZE), lambda i, j, pt: (0, 0)),
           pl.BlockSpec(wpair.shape, lambda i, j, pt: (0, 0, 0, 0))],
        out_specs=pl.BlockSpec((None, PAGES_PER_STEP * cpp, n), lambda i, j, pt: (i, j, 0)),
    )
    return pl.pallas_call(
        _cmp_partial_paged_kernel,
        grid_spec=grid_spec,
        out_shape=jax.ShapeDtypeStruct((b, npg * cpp, n), F32),
        compiler_params=_cparams("arbitrary", "arbitrary"),
        name="cmp_partial_paged",
    )(page_table, *([cache_t] * PAGES_PER_STEP), jnp.asarray(perm, BF), wpair)


def _cmp_finish_kernel(c_ref, cpos_ref, w2_ref, o_ref):
    c = c_ref[...]
    nrow = c.shape[0]
    c_next = pltpu.roll(c, nrow - 1, axis=0)
    for s in range(2):
        for gi in range(N_KVG):
            o = (s * N_KVG + gi) * 2 * CMP_HID
            hid = (c[:, o:o + CMP_HID] + cpos_ref[0:1, o:o + CMP_HID]
                   + c_next[:, o + CMP_HID:o + 2 * CMP_HID] + cpos_ref[1:2, o + CMP_HID:o + 2 * CMP_HID])
            oo = (s * N_KVG + gi) * HD_N
            o_ref[:, oo:oo + HD_N] = _dot(_gelu_tanh(hid), w2_ref[s])


def _cmp_finish(c, cpos, w2):
    b, nch, n = c.shape
    return pl.pallas_call(
        _cmp_finish_kernel,
        grid=(b,),
        in_specs=[pl.BlockSpec((None, nch, n), lambda i: (i, 0, 0)), pl.BlockSpec(cpos.shape, lambda i: (0, 0)),
                  pl.BlockSpec(w2.shape, lambda i: (0, 0, 0))],
        out_specs=pl.BlockSpec((None, nch, KV_COLS), lambda i: (i, 0, 0)),
        out_shape=jax.ShapeDtypeStruct((b, nch, KV_COLS), F32),
        compiler_params=_cparams("arbitrary"),
        name="cmp_finish",
    )(c, cpos, w2.astype(BF))


def _cmp_pos_rows(cmp_pos):
    pos = cmp_pos.reshape(2, STRIDE, 1, 1, HD_N)
    rows = jnp.broadcast_to(pos, (2, STRIDE, 2, N_KVG, HD_N)).reshape(2, STRIDE * KV_COLS)
    return jnp.concatenate([rows, jnp.zeros((6, STRIDE * KV_COLS), F32)], axis=0)


def _bias_table_kernel(dist_ref, rb_ref, o_ref):
    bucket = _t5_bucket(dist_ref[...])
    for h in range(H_N):
        out = jnp.zeros(bucket.shape, F32)
        for b in range(N_BUCKETS):
            out = jnp.where(bucket == b, rb_ref[b, h], out)
        o_ref[h] = out


def _bias_table(dist, rel_bias, tr):
    r, n = dist.shape
    return pl.pallas_call(
        _bias_table_kernel,
        grid=(r // tr,),
        in_specs=[pl.BlockSpec((tr, n), lambda i: (i, 0)),
                  pl.BlockSpec(memory_space=pltpu.SMEM)],
        out_specs=pl.BlockSpec((H_N, tr, n), lambda i: (0, i, 0)),
        out_shape=jax.ShapeDtypeStruct((H_N, r, n), F32),
        compiler_params=_cparams("arbitrary"),
        name="bias_table",
    )(dist, rel_bias)


def _softmax_rows(logits, valid):
    lm = jnp.where(valid, logits, NEG_INF)
    e = jnp.exp(lm - jnp.max(lm, axis=-1, keepdims=True))
    return e / jnp.sum(e, axis=-1, keepdims=True)


def _cmp_attn_kernel(q_ref, kv_ref, bias_ref, ovt_ref, o_ref, sel_ref):
    tq = q_ref.shape[0]
    ncp = kv_ref.shape[0]
    nsb = ovt_ref.shape[0]
    q0 = pl.program_id(1) * tq
    qpos = q0 + lax.broadcasted_iota(I32, (tq, 1), 0)
    cend = lax.broadcasted_iota(I32, (1, ncp), 1) * STRIDE + (L_CMP - 1)
    valid = (qpos >= cend) & (lax.broadcasted_iota(I32, (1, ncp), 1) < ncp - 1)
    validf = valid.astype(F32)
    q = q_ref[...] * (HD_N ** -0.5)
    blk = lax.broadcasted_iota(I32, (nsb, tq), 0)
    cur = (q0 + lax.broadcasted_iota(I32, (1, tq), 1)) // L_SEL
    forced = (blk == 0) | (blk == cur) | (blk == cur - 1)
    future = blk > cur
    for gi in range(N_KVG):
        kc = kv_ref[:, gi * HD_N:(gi + 1) * HD_N]
        vc = kv_ref[:, (N_KVG + gi) * HD_N:(N_KVG + gi + 1) * HD_N]
        pcs = jnp.zeros((tq, ncp), F32)
        for hl in range(HPG):
            h = gi * HPG + hl
            sl = slice(h * HD_N, (h + 1) * HD_N)
            pc = _softmax_rows(_dot_nt(q[:, sl], kc) + bias_ref[h], valid) * validf
            pcs = pcs + pc
            o_ref[:, sl] = _dot(pc, vc)
        imp = _dot_nt(ovt_ref[...], pcs)
        score = jnp.where(forced, FORCE, jnp.where(future, -FORCE, imp))
        rank = jnp.zeros((nsb, tq), F32)
        for i in range(nsb):
            si = score[i:i + 1, :]
            rank = rank + ((si > score) | ((si == score) & (i < blk))).astype(F32)
        sel_ref[gi] = (rank < N_TOP).astype(sel_ref.dtype)


def _cmp_sel_overlap_t(nc, ncp, nsb):
    s = np.arange(nc)[None, :] * STRIDE
    j = np.arange(nsb)[:, None] * L_SEL
    ov = np.clip(np.minimum(s + L_CMP, j + L_SEL) - np.maximum(s, j), 0, None) / L_CMP
    return np.pad(ov, ((0, 0), (0, ncp - nc))).astype(np.float32)


def _cmp_attn_prompt(p_n, kv_cmp, bias_c, t):
    b = kv_cmp.shape[0]
    ncp = kv_cmp.shape[1]
    nsb = t // L_SEL
    tq = ATT_TILE
    nqt = t // tq
    ovt = jnp.asarray(_cmp_sel_overlap_t(ncp - 1, ncp, nsb), BF)
    return pl.pallas_call(
        _cmp_attn_kernel,
        grid=(b, nqt),
        in_specs=[pl.BlockSpec((tq, C_N), lambda i, j: (i * nqt + j, 0)),
                  pl.BlockSpec((None, ncp, KV_COLS), lambda i, j: (i, 0, 0)),
                  pl.BlockSpec((H_N, tq, ncp), lambda i, j: (0, j, 0)),
                  pl.BlockSpec((nsb, ncp), lambda i, j: (0, 0))],
        out_specs=[pl.BlockSpec((tq, C_N), lambda i, j: (i * nqt + j, 0)),
                   pl.BlockSpec((None, N_KVG, nsb, tq), lambda i, j: (i, 0, 0, j))],
        out_shape=[jax.ShapeDtypeStruct((b * t, C_N), F32), jax.ShapeDtypeStruct((b, N_KVG, nsb, t), BF)],
        compiler_params=_cparams("arbitrary", "arbitrary"),
        name="nsa_cmp_select",
    )(p_n, kv_cmp, bias_c, ovt)


ATT_TK = 128
ATT_R = ATT_TILE // ATT_TK
N_SEL_OFFS = ATT_R + (MAX_DIST + ATT_TK - 1) // ATT_TK + 1
N_WIN_OFFS = ATT_R + WINDOW // ATT_TK
QA_COLS = HD_N + 32
SEL_STEP_TILES = 4
WIN_STEP_TILES = 2


def _swa_kernel(qa_ref, ks_ref, vs_ref, kw_ref, vw_ref, tabs_ref, tabw_ref, gn_ref, oc_ref, o_ref):
    tq = qa_ref.shape[1]
    qt = pl.program_id(2)
    top = ATT_R * qt + ATT_R - 1

    heads = range(HPG)

    def attend(k_ref, v_ref, tab_ref, lo, n_off, step_tiles):
        tiles = range(step_tiles)

        def body(kp, carry):
            ki = [lo + step_tiles * kp + j for j in tiles]
            kt = [k_ref[jnp.minimum(i, top)] for i in ki]
            vt = [v_ref[jnp.minimum(i, top)] for i in ki]
            off = [jnp.where(i > top, n_off, jnp.minimum(top - i, n_off - 1)) for i in ki]
            s = [[lax.dot_general(kt[j], qa_ref[hl], (((1,), (1,)), ((), ())), preferred_element_type=F32)
                  + tab_ref[off[j], hl * ATT_TK:(hl + 1) * ATT_TK, :] for j in tiles] for hl in heads]
            m_new = [functools.reduce(jnp.maximum, [carry[hl][0]] + [jnp.max(s[hl][j], axis=0, keepdims=True)
                                                                    for j in tiles]) for hl in heads]
            alpha = [jnp.exp(carry[hl][0] - m_new[hl]) for hl in heads]
            p = [[jnp.exp(s[hl][j] - m_new[hl]) for j in tiles] for hl in heads]
            l = [alpha[hl] * carry[hl][1] + sum(jnp.sum(p[hl][j], axis=0, keepdims=True) for j in tiles)
                 for hl in heads]
            acc = [alpha[hl] * carry[hl][2] + sum(jnp.dot(vt[j], p[hl][j].astype(BF), preferred_element_type=F32)
                                                  for j in tiles) for hl in heads]
            return tuple((m_new[hl], l[hl], acc[hl]) for hl in heads)

        init = tuple((jnp.full((1, tq), NEG_INF, F32), jnp.zeros((1, tq), F32), jnp.zeros((HD_N, tq), F32))
                     for _ in heads)
        res = lax.fori_loop(0, (top - lo) // step_tiles + 1, body, init)
        return [acc / l for _, l, acc in res]

    o_sel = attend(ks_ref, vs_ref, tabs_ref, 0, N_SEL_OFFS, SEL_STEP_TILES)
    o_win = attend(kw_ref, vw_ref, tabw_ref, jnp.maximum(top + 1 - N_WIN_OFFS, 0), N_WIN_OFFS, WIN_STEP_TILES)
    gates = _sigmoid(gn_ref[...])
    gates_t = gates.T
    for hl in heads:
        sl = slice(hl * HD_N, (hl + 1) * HD_N)
        o_t = gates_t[3 * hl + 1:3 * hl + 2, :] * o_sel[hl] + gates_t[3 * hl + 2:3 * hl + 3, :] * o_win[hl]
        o_ref[:, sl] = (gates[:, 3 * hl:3 * hl + 1] * oc_ref[:, sl] + o_t.T).astype(o_ref.dtype)


def _swa_prompt(p_n, qa, ks, vs, kw, vw, tab_s, tab_w, o_cmp, t):
    b = qa.shape[0]
    tq = ATT_TILE
    nqt = t // tq
    nkt = t // ATT_TK
    gw = HPG * HD_N
    k_spec = pl.BlockSpec((None, None, nkt, ATT_TK, QA_COLS), lambda i, g, j: (i, g, 0, 0, 0))
    v_spec = pl.BlockSpec((None, None, nkt, HD_N, ATT_TK), lambda i, g, j: (i, g, 0, 0, 0))
    tab_spec = lambda n: pl.BlockSpec((n + 1, HPG * ATT_TK, tq), lambda i, g, j: (0, g, 0))
    return pl.pallas_call(
        _swa_kernel,
        grid=(b, N_KVG, nqt),
        in_specs=[pl.BlockSpec((None, HPG, tq, QA_COLS), lambda i, g, j: (i, g, j, 0)),
                  k_spec, v_spec, k_spec, v_spec, tab_spec(N_SEL_OFFS), tab_spec(N_WIN_OFFS),
                  pl.BlockSpec((tq, GN_GROUP_COLS), lambda i, g, j: (i * nqt + j, NP_GN // GN_GROUP_COLS + g)),
                  pl.BlockSpec((tq, gw), lambda i, g, j: (i * nqt + j, g))],
        out_specs=pl.BlockSpec((tq, gw), lambda i, g, j: (i * nqt + j, g)),
        out_shape=jax.ShapeDtypeStruct((b * t, C_N), BF),
        compiler_params=_cparams("arbitrary", "arbitrary", "arbitrary"),
        name="nsa_sel_win",
    )(qa, ks, vs, kw, vw, tab_s, tab_w, p_n, o_cmp)


def _swa_operands(p_n, sel, b, t):
    nsb = t // L_SEL
    nkt = t // ATT_TK
    q = (p_n[:, :C_N] * (HD_N ** -0.5)).reshape(b, t, H_N, HD_N).transpose(0, 2, 1, 3)
    pen = jnp.where(jnp.transpose(sel, (0, 1, 3, 2)) > 0.5, 0.0, NEG_INF).astype(F32)
    pen = jnp.broadcast_to(pen[:, :, None], (b, N_KVG, HPG, t, nsb)).reshape(b, H_N, t, nsb)
    qa = jnp.concatenate([q, pen], axis=-1).astype(BF)

    def split(cols):
        kv = p_n[:, cols:cols + KV_COLS].reshape(b, t, 2, N_KVG, HD_N)
        k = jnp.transpose(kv[:, :, 0], (0, 2, 1, 3))
        v = jnp.transpose(kv[:, :, 1], (0, 2, 3, 1)).reshape(b, N_KVG, HD_N, nkt, ATT_TK)
        return k, jnp.transpose(v, (0, 1, 3, 2, 4)).astype(BF)

    onehot = jnp.asarray(np.arange(t)[:, None] // L_SEL == np.arange(nsb)[None, :], F32)
    k_s, v_s = split(NP_KVS)
    k_w, v_w = split(NP_KVW)
    ext = lambda k, e: jnp.concatenate([k, jnp.broadcast_to(e, (b, N_KVG, t, nsb))], axis=-1).astype(BF) \
        .reshape(b, N_KVG, nkt, ATT_TK, QA_COLS)
    return qa, ext(k_s, onehot), v_s, ext(k_w, jnp.zeros((t, nsb), F32)), v_w


def _sample_cmp_kernel(q_ref, kv_ref, rbt_ref, ov_ref, o_ref, idx_ref, *, past):
    ncp = kv_ref.shape[0]
    nsbp = ov_ref.shape[1]
    nsb = -(-(past + 1) // L_SEL)
    q = q_ref[...] * (HD_N ** -0.5)
    hrow = lax.broadcasted_iota(I32, (H_N, 1), 0)
    nidx = lax.broadcasted_iota(I32, (1, ncp), 1)
    valid = nidx < ncp - 1
    bias = _bias_rows(past - (nidx * STRIDE + (L_CMP - 1)), rbt_ref[...])
    logits = jnp.zeros((H_N, ncp), F32)
    for gi in range(N_KVG):
        lg = _dot_nt(q, kv_ref[:, gi * HD_N:(gi + 1) * HD_N])
        logits = jnp.where(hrow // HPG == gi, lg, logits)
    pc = _softmax_rows(logits + bias, valid) * valid.astype(F32)
    o = jnp.zeros((H_N, HD_N), F32)
    for gi in range(N_KVG):
        og = _dot(pc, kv_ref[:, (N_KVG + gi) * HD_N:(N_KVG + gi + 1) * HD_N])
        o = jnp.where(hrow // HPG == gi, og, o)
    o_ref[...] = o
    imp_h = _dot(pc, ov_ref[...])
    blk = lax.broadcasted_iota(I32, (8, nsbp), 1)
    grow = lax.broadcasted_iota(I32, (8, 1), 0)
    cur = past // L_SEL
    score = jnp.full((8, nsbp), -3e38, F32)
    for gi in range(N_KVG):
        imp = jnp.sum(jnp.where(hrow // HPG == gi, imp_h, 0.0), axis=0, keepdims=True)
        score = jnp.where(grow == gi, imp, score)
    forced = (blk == 0) | (blk == cur) | (blk == cur - 1)
    score = jnp.where(forced, FORCE, jnp.where(blk > cur, -FORCE, score))
    score = jnp.where((blk < nsb) & (grow < N_KVG), score, -3e38)
    lane = lax.broadcasted_iota(I32, (8, LANES), 1)
    picks = jnp.zeros((8, LANES), I32)
    for it in range(N_TOP):
        mx = jnp.max(score, axis=-1, keepdims=True)
        pick = jnp.min(jnp.where(score == mx, blk, nsbp), axis=-1, keepdims=True)
        picks = jnp.where(lane == it, pick, picks)
        score = jnp.where(blk == pick, -3e38, score)
    idx_ref[...] = picks


def _sample_cmp(q, kv_cmp, rel_bias, past):
    b, ncp, _ = kv_cmp.shape
    nc = ncp - 1
    nsb = -(-(past + 1) // L_SEL)
    nsbp = -(-nsb // LANES) * LANES
    s = np.arange(nc)[:, None] * STRIDE
    j = np.arange(nsb)[None, :] * L_SEL
    ov = np.clip(np.minimum(s + L_CMP, j + L_SEL) - np.maximum(s, j), 0, None) / L_CMP
    ov = np.pad(ov, ((0, ncp - nc), (0, nsbp - nsb))).astype(np.float32)
    return pl.pallas_call(
        functools.partial(_sample_cmp_kernel, past=past),
        grid=(b,),
        in_specs=[pl.BlockSpec((None, H_N, HD_N), lambda i: (i, 0, 0)),
                  pl.BlockSpec((None, ncp, KV_COLS), lambda i: (i, 0, 0)),
                  pl.BlockSpec((H_N, N_BUCKETS), lambda i: (0, 0)),
                  pl.BlockSpec((ncp, nsbp), lambda i: (0, 0))],
        out_specs=[pl.BlockSpec((None, H_N, HD_N), lambda i: (i, 0, 0)),
                   pl.BlockSpec((None, 8, LANES), lambda i: (i, 0, 0))],
        out_shape=[jax.ShapeDtypeStruct((b, H_N, HD_N), F32), jax.ShapeDtypeStruct((b, 8, LANES), I32)],
        compiler_params=_cparams("arbitrary"),
        name="nsa_sample_cmp_select",
    )(q, kv_cmp, rel_bias.T, jnp.asarray(ov, BF))


def _block_copy_kernel(pg_ref, *refs):
    del pg_ref
    x_refs, o_ref = refs[:-1], refs[-1]
    for n, x_ref in enumerate(x_refs):
        o_ref[n] = x_ref[...]


def _gather_sel_pages(cache_t, page):
    rows, nslot = page.shape

    def slot_spec(n):
        return pl.BlockSpec((None, 2, None, HD_N, PAGE_SIZE), lambda i, pg: (pg[i, n], 0, i % N_KVG, 0, 0))

    return pl.pallas_call(
        _block_copy_kernel,
        grid_spec=pltpu.PrefetchScalarGridSpec(
            num_scalar_prefetch=1, grid=(rows,),
            in_specs=[slot_spec(n) for n in range(nslot)],
            out_specs=pl.BlockSpec((None, nslot, 2, HD_N, PAGE_SIZE), lambda i, pg: (i, 0, 0, 0, 0))),
        out_shape=jax.ShapeDtypeStruct((rows, nslot, 2, HD_N, PAGE_SIZE), cache_t.dtype),
        compiler_params=_cparams("arbitrary"),
        name="gather_sel_pages",
    )(page, *([cache_t] * nslot))


def _sample_swa_kernel(idx_ref, q_ref, blk_ref, win_ref, new_ref, rbt_ref, gate_ref, oc_ref, o_ref, *, past):
    bi = pl.program_id(0)
    q = q_ref[...] * (HD_N ** -0.5)
    rbt = rbt_ref[...]
    hrow = lax.broadcasted_iota(I32, (H_N, 1), 0)
    nk = N_TOP * PAGE_SIZE
    lane = lax.broadcasted_iota(I32, (1, nk), 1)
    new_blk = past // L_SEL
    bias_new = rbt[:, 0:1]
    gates = _sigmoid(gate_ref[...])
    nwin = win_ref.shape[0]
    wdist = nwin - lax.broadcasted_iota(I32, (1, nwin), 1)
    wbias = _bias_rows(wdist, rbt)
    wvalid = (wdist >= 0) & (wdist <= WINDOW)

    def with_new(logits, valid, weigh, k_new, v_new):
        l_new = jnp.sum(q * k_new, axis=-1, keepdims=True) + bias_new
        lm = jnp.where(valid, logits, NEG_INF)
        m = jnp.maximum(jnp.max(lm, axis=-1, keepdims=True), l_new)
        e = jnp.where(valid, jnp.exp(lm - m), 0.0)
        e_new = jnp.exp(l_new - m)
        den = jnp.sum(e, axis=-1, keepdims=True) + e_new
        return (weigh(e) + e_new * v_new) / den

    o_sel = jnp.zeros((H_N, HD_N), F32)
    o_win = jnp.zeros((H_N, HD_N), F32)
    bpp = PAGE_SIZE // L_SEL
    tok = lane % PAGE_SIZE
    for gi in range(N_KVG):
        ksl = slice(gi * HD_N, (gi + 1) * HD_N)
        vsl = slice((N_KVG + gi) * HD_N, (N_KVG + gi + 1) * HD_N)
        bid = jnp.zeros((1, nk), I32)
        for n in range(N_TOP):
            bid = jnp.where(lane // PAGE_SIZE == n, idx_ref[bi, gi, n], bid)
        dist = past - ((bid // bpp) * PAGE_SIZE + tok)
        valid = (bid != new_blk) & (tok // L_SEL == bid % bpp) & (dist >= 0)
        kt = jnp.concatenate([blk_ref[gi, n, 0] for n in range(N_TOP)], axis=1)
        vt = jnp.concatenate([blk_ref[gi, n, 1] for n in range(N_TOP)], axis=1)
        logits = _dot(q, kt) + _bias_rows(dist, rbt)
        og = with_new(logits, valid, lambda e, vt=vt: _dot_nt(e, vt), new_ref[0:1, ksl], new_ref[0:1, vsl])
        o_sel = jnp.where(hrow // HPG == gi, og, o_sel)
        logits = _dot_nt(q, win_ref[:, ksl]) + wbias
        og = with_new(logits, wvalid, lambda e, vsl=vsl: _dot(e, win_ref[:, vsl]), new_ref[1:2, ksl],
                      new_ref[1:2, vsl])
        o_win = jnp.where(hrow // HPG == gi, og, o_win)
    o_ref[...] = gates[:, 0:1] * oc_ref[...] + gates[:, 1:2] * o_sel + gates[:, 2:3] * o_win


def _sample_swa(idx, q, blocks, win, new_kv, rel_bias, gates, o_cmp, past):
    b = q.shape[0]
    w = win.shape[1]
    grid_spec = pltpu.PrefetchScalarGridSpec(
        num_scalar_prefetch=1, grid=(b,),
        in_specs=[pl.BlockSpec((None, H_N, HD_N), lambda i, ix: (i, 0, 0)),
                  pl.BlockSpec((None, N_KVG, N_TOP, 2, HD_N, PAGE_SIZE), lambda i, ix: (i, 0, 0, 0, 0, 0)),
                  pl.BlockSpec((None, w, KV_COLS), lambda i, ix: (i, 0, 0)),
                  pl.BlockSpec((None, 2, KV_COLS), lambda i, ix: (i, 0, 0)),
                  pl.BlockSpec((H_N, N_BUCKETS), lambda i, ix: (0, 0)),
                  pl.BlockSpec((None, H_N, 3), lambda i, ix: (i, 0, 0)),
                  pl.BlockSpec((None, H_N, HD_N), lambda i, ix: (i, 0, 0))],
        out_specs=pl.BlockSpec((None, H_N, HD_N), lambda i, ix: (i, 0, 0)))
    return pl.pallas_call(
        functools.partial(_sample_swa_kernel, past=past),
        grid_spec=grid_spec,
        out_shape=jax.ShapeDtypeStruct((b, H_N, HD_N), F32),
        compiler_params=_cparams("arbitrary"),
        name="nsa_sample_sel_win",
    )(idx, q, blocks, win, new_kv, rel_bias.T, gates, o_cmp)


ROUTER_COLS = LANES
MOE_ROWS_PROMPT = 128
MOE_ROWS_SAMPLE = 16


def _router_kernel(x_ref, g_ref, sh_ref, sc_ref, w_ref, b_ref, h_ref, e_ref, wt_ref, rk_ref, cnt_ref):
    i = pl.program_id(0)
    tm = x_ref.shape[0]

    @pl.when(i == 0)
    def _():
        cnt_ref[...] = jnp.zeros_like(cnt_ref)

    h = (_rms(x_ref[...], g_ref[...]) * (1.0 + sc_ref[...]) + sh_ref[...]).astype(BF)
    _store_folded(h_ref, h.astype(F32))
    logits = jnp.dot(h, w_ref[...], preferred_element_type=F32) + b_ref[...]
    lane = lax.broadcasted_iota(I32, (tm, ROUTER_COLS), 1)

    def top1(vals, ok):
        vm = jnp.where(ok, vals, -3e38)
        mx = jnp.max(vm, axis=-1, keepdims=True)
        return mx, jnp.min(jnp.where(ok & (vm == mx), lane, ROUTER_COLS), axis=-1, keepdims=True)

    isg = lane < N_EGROUPS
    pg = _softmax_rows(logits, isg)
    g_w, g_i = top1(pg, isg)
    ise = (lane >= N_EGROUPS) & ((lane - N_EGROUPS) // EXP_PER_GROUP == g_i)
    pe = _softmax_rows(logits, ise)
    w0, l0 = top1(pe, ise)
    w1, l1 = top1(pe, ise & (lane != l0))
    den = w0 + w1
    e0 = l0 - N_EGROUPS
    e1 = l1 - N_EGROUPS
    e_ref[...] = jnp.where(lane == 0, e0, jnp.where(lane == 1, e1, 0))
    wt_ref[...] = jnp.where(lane == 0, w0 / den * g_w, jnp.where(lane == 1, w1 / den * g_w, 0.0))
    oh0 = (lane == e0).astype(F32)
    oh1 = (lane == e1).astype(F32)
    cnt = oh0 + oh1
    ti = lax.broadcasted_iota(I32, (tm, tm), 0)
    si = lax.broadcasted_iota(I32, (tm, tm), 1)
    before = _dot((ti > si).astype(F32), cnt) + cnt_ref[...]
    r0 = jnp.sum(before * oh0, axis=-1, keepdims=True)
    r1 = jnp.sum(before * oh1, axis=-1, keepdims=True)
    rk_ref[...] = jnp.where(lane == 0, r0, jnp.where(lane == 1, r1, 0.0)).astype(I32)
    cnt_ref[...] = cnt_ref[...] + jnp.sum(cnt, axis=0, keepdims=True)


def _router(x, g, shift, scale, w_r, b_r, tm, rpb):
    m = x.shape[0]
    r = shift.shape[1]
    rows = lambda tn: pl.BlockSpec((tm, tn), lambda i: (i, 0))
    mods = pl.BlockSpec((None, r, D_MODEL), lambda i: ((i * tm) // rpb, 0, 0))
    small = lambda dt: jax.ShapeDtypeStruct((m, ROUTER_COLS), dt)
    return pl.pallas_call(
        _router_kernel,
        grid=(m // tm,),
        in_specs=[rows(D_MODEL), pl.BlockSpec((1, D_MODEL), lambda i: (0, 0)), mods, mods,
                  pl.BlockSpec((D_MODEL, ROUTER_COLS), lambda i: (0, 0)),
                  pl.BlockSpec((1, ROUTER_COLS), lambda i: (0, 0))],
        out_specs=[pl.BlockSpec((tm * ROW_FOLD, LANES), lambda i: (i, 0)),
                   rows(ROUTER_COLS), rows(ROUTER_COLS), rows(ROUTER_COLS),
                   pl.BlockSpec((1, ROUTER_COLS), lambda i: (0, 0))],
        out_shape=[jax.ShapeDtypeStruct((m * ROW_FOLD, LANES), F32), small(I32), small(F32), small(I32),
                   jax.ShapeDtypeStruct((1, ROUTER_COLS), F32)],
        compiler_params=_cparams("arbitrary"),
        name="moe_router",
    )(x, g.reshape(1, D_MODEL), shift, scale, w_r, b_r)


ROW_FOLD = D_MODEL // LANES
ROW_PITCH = ROW_FOLD + 8


def _store_folded(ref, x):
    n = x.shape[0]
    for c in range(ROW_FOLD):
        ref[pl.ds(c, n, stride=ROW_FOLD), :] = x[:, c * LANES:(c + 1) * LANES]


def _load_folded(ref, first_row, n):
    return jnp.concatenate([ref[pl.ds(first_row * ROW_PITCH + c, n, stride=ROW_PITCH), :] for c in range(ROW_FOLD)],
                           axis=1)


def _row_gather_ring(src_hbm, buf, sems, groups, idx_now, idx_next, inline_next=False, priorities=(0, 1)):
    i = pl.program_id(0)
    last = pl.num_programs(0) - 1
    slot = i % 2
    total = sum(cnt for _, cnt, _ in groups)
    assert 2 * total * ROW_PITCH == buf.shape[0]

    def rows_at(row, pitch):
        return pl.ds(pl.multiple_of(row * pitch, 8), ROW_FOLD)

    def start(idx, s, first, k, r, priority):
        pltpu.make_async_copy(src_hbm.at[rows_at(idx(k, r), ROW_FOLD)],
                              buf.at[rows_at(s * total + first + r, ROW_PITCH)], sems.at[s]).start(priority=priority)

    def start_all(idx, s):
        for first, cnt, k in groups:
            per_trip = math.gcd(cnt, 8)

            def issue(j, c):
                for u in range(per_trip):
                    start(idx, s, first, k, per_trip * j + u, priorities[u % 2])
                return c
            lax.fori_loop(0, cnt // per_trip, issue, 0)

    def wait_slot(s):
        span = buf.at[pl.ds(pl.multiple_of(s * total * ROW_PITCH, 8), total * ROW_FOLD)]
        pltpu.make_async_copy(span, span, sems.at[s]).wait()

    @pl.when(i == 0)
    def _():
        start_all(idx_now, slot)

    def finish():
        if inline_next:
            @pl.when(i == last)
            def _():
                wait_slot(1 - slot)

    if inline_next:
        wait_slot(slot)
        for first, cnt, k in groups:
            for r in range(cnt):
                start(idx_next, 1 - slot, first, k, r, priorities[r % 2])
    else:
        @pl.when(i < last)
        def _():
            start_all(idx_next, 1 - slot)

        wait_slot(slot)
    return slot * total, finish


def _expert_kernel(be_ref, rt_ref, rtn_ref, h_hbm, w1_ref, w3_ref, w2_ref, o_ref, xbuf, sems, w1b, w3b, w2b):
    i = pl.program_id(0)
    blk = rt_ref.shape[1]

    @pl.when((i == 0) | (be_ref[i] != be_ref[jnp.maximum(i - 1, 0)]))
    def _():
        w1b[...] = w1_ref[...].astype(BF)
        w3b[...] = w3_ref[...].astype(BF)
        w2b[...] = w2_ref[...].astype(BF)

    base, finish = _row_gather_ring(h_hbm, xbuf, sems, ((0, blk, 0),), lambda k, r: rt_ref[0, r],
                                    lambda k, r: rtn_ref[0, r], inline_next=True, priorities=(1, 1))
    x = _load_folded(xbuf, base, blk).astype(BF)
    a = jnp.dot(x, w1b[...], preferred_element_type=F32)
    b = jnp.dot(x, w3b[...], preferred_element_type=F32)
    hid = a * _sigmoid(a) * b
    _store_folded(o_ref, jnp.dot(hid.astype(BF), w2b[...], preferred_element_type=F32))
    finish()


def _experts(h2, row_tok, blk_exp, w1, w3, w2):
    nblk, _, blk = row_tok.shape
    idx_spec = lambda d: pl.BlockSpec((None, 1, blk), lambda i, be: (jnp.minimum(i + d, nblk - 1), 0, 0),
                                      memory_space=pltpu.SMEM)
    grid_spec = pltpu.PrefetchScalarGridSpec(
        num_scalar_prefetch=1, grid=(nblk,),
        in_specs=[idx_spec(0), idx_spec(1),
                  pl.BlockSpec(memory_space=pl.ANY),
                  pl.BlockSpec((None, D_MODEL, D_EXP), lambda i, be: (be[i], 0, 0)),
                  pl.BlockSpec((None, D_MODEL, D_EXP), lambda i, be: (be[i], 0, 0)),
                  pl.BlockSpec((None, D_EXP, D_MODEL), lambda i, be: (be[i], 0, 0))],
        out_specs=pl.BlockSpec((blk * ROW_FOLD, LANES), lambda i, be: (i, 0)),
        scratch_shapes=[pltpu.VMEM((2 * blk * ROW_PITCH, LANES), F32), pltpu.SemaphoreType.DMA((2,)),
                        pltpu.VMEM((D_MODEL, D_EXP), BF), pltpu.VMEM((D_MODEL, D_EXP), BF),
                        pltpu.VMEM((D_EXP, D_MODEL), BF)])
    return pl.pallas_call(
        _expert_kernel,
        grid_spec=grid_spec,
        out_shape=jax.ShapeDtypeStruct((nblk * blk * ROW_FOLD, LANES), F32),
        compiler_params=_cparams("arbitrary"),
        name="moe_experts",
    )(blk_exp, row_tok, row_tok, h2, w1, w3, w2)


def _final_kernel(x_ref, g_ref, dest_ref, destn_ref, ys_hbm, wt_ref, nf_ref, o_ref, ybuf, sems):
    tm = x_ref.shape[0]
    groups = tuple((k * tm, tm, k) for k in range(TOP_K))
    base, _ = _row_gather_ring(ys_hbm, ybuf, sems, groups, lambda k, r: dest_ref[k, r], lambda k, r: destn_ref[k, r])
    wt = wt_ref[...]
    moe = wt[:, 0:1] * _load_folded(ybuf, base, tm) + wt[:, 1:2] * _load_folded(ybuf, base + tm, tm)
    o_ref[...] = _rms(x_ref[...] + g_ref[...] * moe, nf_ref[...])


def _final(x, gate, ys, dest, wts, norm_f, tm, rpb):
    m = x.shape[0]
    r = gate.shape[1]
    nt = m // tm
    rows = lambda tn: pl.BlockSpec((tm, tn), lambda i: (i, 0))
    idx_spec = lambda d: pl.BlockSpec((None, TOP_K, tm), lambda i: (jnp.minimum(i + d, nt - 1), 0, 0),
                                      memory_space=pltpu.SMEM)
    return pl.pallas_call(
        _final_kernel,
        grid=(nt,),
        in_specs=[rows(D_MODEL), pl.BlockSpec((None, r, D_MODEL), lambda i: ((i * tm) // rpb, 0, 0)),
                  idx_spec(0), idx_spec(1),
                  pl.BlockSpec(memory_space=pl.ANY),
                  rows(ROUTER_COLS), pl.BlockSpec((1, D_MODEL), lambda i: (0, 0))],
        out_specs=rows(D_MODEL),
        out_shape=jax.ShapeDtypeStruct((m, D_MODEL), F32),
        scratch_shapes=[pltpu.VMEM((2 * TOP_K * tm * ROW_PITCH, LANES), F32), pltpu.SemaphoreType.DMA((2,))],
        compiler_params=_cparams("arbitrary"),
        name="moe_combine_final_norm",
    )(x, gate, dest, dest, ys, wts, norm_f.reshape(1, D_MODEL))


def _moe_and_final(x1, g2, shift, scale, gate, w_r, b_r, exp_w1, exp_w3, exp_w2, norm_f, tm, rpb, blk):
    m = x1.shape[0]
    h2, eid, wts, rank, counts = _router(x1, g2, shift, scale, w_r, b_r, tm, rpb)
    counts = counts[0, :N_EXP].astype(I32)
    padded = (counts + blk - 1) // blk * blk
    pend = jnp.cumsum(padded)
    pstart = pend - padded
    n_blocks = -(-(m * TOP_K) // blk) + N_EXP
    starts = jnp.arange(n_blocks, dtype=I32)[:, None] * blk
    blk_exp = jnp.minimum(jnp.sum((pend[None, :] <= starts).astype(I32), axis=1), N_EXP - 1)
    e = eid[:, :TOP_K]
    dest = pstart[e] + rank[:, :TOP_K]
    tok = jnp.broadcast_to(jnp.arange(m, dtype=I32)[:, None], (m, TOP_K))
    row_tok = jnp.zeros((n_blocks * blk,), I32).at[dest.reshape(-1)].set(tok.reshape(-1))
    ys = _experts(h2, row_tok.reshape(n_blocks, 1, blk), blk_exp, exp_w1, exp_w3, exp_w2)
    dest_t = jnp.transpose(dest.reshape(m // tm, tm, TOP_K), (0, 2, 1))
    return _final(x1, gate, ys, dest_t, wts, norm_f, tm, rpb)


def _pack_in_proj(w_in):
    o = C_RIN
    w_r = w_in[:, :o]
    w_q = w_in[:, o:o + C_N + 3 * KV_COLS]
    o += C_N + 3 * KV_COLS
    w_gn = w_in[:, o:o + 3 * H_N].reshape(D_MODEL, N_KVG, 3 * HPG)
    w_gn = jnp.pad(w_gn, ((0, 0), (0, 0), (0, GN_GROUP_COLS - 3 * HPG))).reshape(D_MODEL, N_KVG * GN_GROUP_COLS)
    o += 3 * H_N
    w_gm = w_in[:, o:]
    return w_r.astype(BF), jnp.concatenate([w_q, w_gn], axis=1).astype(BF), w_gm.astype(BF)


def _prompt_bias_tables(rel_bias, t):
    tq, tk = ATT_TILE, ATT_TK
    i = np.arange(tq)[None, :]
    j = np.arange(tk)[:, None]
    dist = np.stack([tk * (o - (ATT_R - 1)) + i - j for o in range(N_WIN_OFFS)]).astype(np.int32)
    raw = _bias_table(jnp.asarray(dist.reshape(N_WIN_OFFS * tk, tq)), rel_bias, tk)
    raw = jnp.transpose(raw.reshape(H_N, N_WIN_OFFS, tk, tq), (1, 0, 2, 3))
    ok_w = jnp.asarray((dist >= 0) & (dist <= WINDOW))[:, None]
    ok_s = jnp.asarray(dist[:N_SEL_OFFS] >= 0)[:, None]
    masked = jnp.full((1, H_N * tk, tq), NEG_INF, F32)
    tab_w = jnp.concatenate([jnp.where(ok_w, raw, NEG_INF).reshape(N_WIN_OFFS, H_N * tk, tq), masked])
    tab_s = jnp.concatenate([jnp.where(ok_s, raw[:N_SEL_OFFS], NEG_INF).reshape(N_SEL_OFFS, H_N * tk, tq), masked])
    nc = (t - L_CMP) // STRIDE + 1
    ncp = nc + 1
    dc = (np.arange(t)[:, None] - (np.arange(ncp)[None, :] * STRIDE + L_CMP - 1)).astype(np.int32)
    return tab_s, tab_w, _bias_table(jnp.asarray(dc), rel_bias, tq)


def kernel(x_prompt, x_sample, c_prompt, c_sample, cache_cmp_kv, cache_sel_kv, state_win_kv, state_rwkv_shift,
           state_rwkv_wkv, page_table, rel_bias, norm_f, norm1, norm2, w_ada, b_ada, w_in, rwkv_mu, rwkv_w0, rwkv_w2,
           rwkv_a0, rwkv_a2, rwkv_g2, rwkv_kk, rwkv_ka, rwkv_rk, rwkv_ln_g, rwkv_ln_b, cmp_pos, cmp_w1, cmp_w2,
           w_o_rwkv, w_o_nsa, w_out, router_wg, router_bg, router_we, router_be, exp_w1, exp_w3, exp_w2):
    bp, t, _ = x_prompt.shape
    bs = x_sample.shape[0]
    mp = bp * t
    past = page_table.shape[1] * PAGE_SIZE

    nrow = -(-(bp + bs) // 8) * 8
    c_all = jnp.concatenate([c_prompt, c_sample, jnp.zeros((nrow - bp - bs, D_MODEL), F32)], axis=0)
    mod = _ada(c_all, w_ada[0], b_ada[0]).reshape(nrow, 6, D_MODEL)
    mod_p = [mod[:bp, i][:, None, :] for i in range(6)]
    mod_s = [mod[bp:bp + bs, i][None] for i in range(6)]

    w_r, w_n, w_gm = _pack_in_proj(w_in[0])
    rw = _rwkv_weights(rwkv_mu[0], rwkv_w0[0], rwkv_w2[0], rwkv_a0[0], rwkv_a2[0], rwkv_g2[0], rwkv_kk[0],
                       rwkv_ka[0], rwkv_rk[0], rwkv_ln_g[0], rwkv_ln_b[0])
    wbd = _cmp_weights(cmp_w1[0], N_KVG)
    cpos = _cmp_partial_rows(_cmp_pos_rows(cmp_pos[0]), wbd, 8)
    wo_r, wo_n, wo = w_o_rwkv[0].astype(BF), w_o_nsa[0].astype(BF), w_out[0].astype(BF)
    w_router = jnp.pad(jnp.concatenate([router_wg[0], router_we[0]], axis=1),
                       ((0, 0), (0, ROUTER_COLS - N_EGROUPS - N_EXP))).astype(BF)
    b_router = jnp.pad(jnp.concatenate([router_bg[0], router_be[0]]), (0, ROUTER_COLS - N_EGROUPS - N_EXP))[None]

    tm = 512
    xp = x_prompt.reshape(mp, D_MODEL)
    h = _norm_mod(xp, norm1[0], mod_p[0], mod_p[1], tm, t)
    p_r = _matmul(h, w_r, tm, C_RIN // 2)
    p_n = _matmul(h, w_n, tm, NP_COLS // 2)
    p_g = _matmul(h, w_gm, tm, 2048)
    o_r, shift_p, wkv_p = _rwkv_prompt(p_r.reshape(bp, t, C_RIN), rw)
    kvc = p_n[:, NP_KVC:NP_KVC + KV_COLS]
    kvs = p_n[:, NP_KVS:NP_KVS + KV_COLS]
    kvw = p_n[:, NP_KVW:NP_KVW + KV_COLS]
    nch = t // STRIDE
    c_part = _cmp_partial_rows(kvc.reshape(bp * nch, STRIDE * KV_COLS), wbd, nch)
    kv_cmp = _cmp_finish(c_part.reshape(bp, nch, -1), cpos, cmp_w2[0])
    tab_s, tab_w, bias_c = _prompt_bias_tables(rel_bias, t)
    o_cmp, sel = _cmp_attn_prompt(p_n, kv_cmp, bias_c, t)
    o_n = _swa_prompt(p_n, *_swa_operands(p_n, sel, bp, t), tab_s, tab_w, o_cmp, t)
    y = _merge(o_r.reshape(mp, C_R), o_n, wo_r, wo_n, p_g, tm)
    x1 = _proj_residual(y, wo, xp, mod_p[2], tm, t)
    y_prompt = _moe_and_final(x1, norm2[0], mod_p[3], mod_p[4], mod_p[5], w_router, b_router, exp_w1[0], exp_w3[0],
                              exp_w2[0], norm_f, tm, t, MOE_ROWS_PROMPT).reshape(bp, t, D_MODEL)
    kv_shape = (1, bp, t, 2, N_KVG, HD_N)
    wlen = min(WINDOW, t)
    win_p = kvw.reshape(bp, t, KV_COLS)[:, t - wlen:].reshape(1, bp, wlen, 2, N_KVG, HD_N)

    xs = x_sample.reshape(bs, D_MODEL)
    hs = _norm_mod(xs, norm1[0], mod_s[0], mod_s[1], bs, bs)
    ps_r = _matmul(hs, w_r, bs, C_RIN // 2)
    ps_n = _matmul(hs, w_n, bs, NP_COLS // 2)
    ps_g = _matmul(hs, w_gm, bs, 2048)
    os_r, wkv_s = _rwkv_step(ps_r, state_rwkv_shift[0], state_rwkv_wkv[0], rw)
    kvc_s = ps_n[:, NP_KVC:NP_KVC + KV_COLS]
    kvs_s = ps_n[:, NP_KVS:NP_KVS + KV_COLS]
    kvw_s = ps_n[:, NP_KVW:NP_KVW + KV_COLS]
    cs_part = _cmp_partial_paged(jnp.transpose(cache_cmp_kv[0], (0, 2, 3, 4, 1)), page_table,
                                 _cmp_weights(cmp_w1[0], CMP_PAIR))
    kv_cmp_s = _cmp_finish(cs_part, cpos, cmp_w2[0])
    q_s = ps_n[:, :C_N].reshape(bs, H_N, HD_N)
    o_cmp_s, picks = _sample_cmp(q_s, kv_cmp_s, rel_bias, past)
    idx = picks[:, :N_KVG, :N_TOP]
    bpp = PAGE_SIZE // L_SEL
    npb = past // L_SEL
    idc = jnp.minimum(idx, npb - 1)
    page = jnp.take_along_axis(page_table, (idc // bpp).reshape(bs, -1), axis=1).reshape(bs * N_KVG, N_TOP)
    cache_t = jnp.transpose(cache_sel_kv[0], (0, 2, 3, 4, 1))
    blocks = _gather_sel_pages(cache_t, page).reshape(bs, N_KVG, N_TOP, 2, HD_N, PAGE_SIZE)
    win_buf = state_win_kv[0].reshape(bs, -1, KV_COLS)
    gates_s = ps_n[:, NP_GN:].reshape(bs, N_KVG, GN_GROUP_COLS)[:, :, :3 * HPG].reshape(bs, H_N, 3)
    new_kv = jnp.stack([kvs_s, kvw_s], axis=1)
    os_n = _sample_swa(idx, q_s, blocks, win_buf, new_kv, rel_bias, gates_s, o_cmp_s, past)
    ys = _merge(os_r, os_n.reshape(bs, C_N).astype(BF), wo_r, wo_n, ps_g, bs)
    xs1 = _proj_residual(ys, wo, xs, mod_s[2], bs, bs)
    y_sample = _moe_and_final(xs1, norm2[0], mod_s[3], mod_s[4], mod_s[5], w_router, b_router, exp_w1[0], exp_w3[0],
                              exp_w2[0], norm_f, bs, bs, MOE_ROWS_SAMPLE).reshape(bs, 1, D_MODEL)
    kv1 = (1, bs, 1, 2, N_KVG, HD_N)
    wbuf = win_buf.shape[1]
    win_s = jnp.concatenate([win_buf, kvw_s[:, None, :]], axis=1)[:, -wbuf:].reshape(1, bs, wbuf, 2, N_KVG, HD_N)

    return (y_prompt, y_sample,
            kvc.reshape(kv_shape), kvc_s.reshape(kv1),
            kvs.reshape(kv_shape), kvs_s.reshape(kv1),
            win_p, win_s,
            shift_p.reshape(1, bp, C_RIN), ps_r.reshape(1, bs, C_RIN),
            wkv_p[None], wkv_s[None])
```

```python
import functools
import math

import numpy as np
import jax
import jax.numpy as jnp
from jax import lax
from jax.experimental import pallas as pl
from jax.experimental.pallas import tpu as pltpu

D_MODEL = 2048
PAGE_SIZE = 128
H_R, HD_R = 16, 64
C_R = H_R * HD_R
LORA_W, LORA_A, LORA_G = 64, 64, 128
C_RIN = 3 * C_R + LORA_W + LORA_A + LORA_G
LN_X_EPS = 64e-5
H_N, HD_N, N_KVG = 16, 64, 4
HPG = H_N // N_KVG
C_N = H_N * HD_N
KV_COLS = 2 * N_KVG * HD_N
L_CMP, STRIDE, CMP_HID = 32, 16, 64
L_SEL, N_TOP, WINDOW = 64, 16, 512
N_BUCKETS, MAX_DIST = 32, 128
N_EGROUPS, EXP_PER_GROUP = 4, 8
N_EXP = N_EGROUPS * EXP_PER_GROUP
TOP_K, D_EXP = 2, 512
RMS_EPS = 1e-6
NEG_INF = -1e30
FORCE = 1e9

BF = jnp.bfloat16
F32 = jnp.float32
I32 = jnp.int32

VMEM_LIMIT_BYTES = 56 * 1024 * 1024
LANES = 128
RW_CHUNK = 32
RW_SEQS_PER_STEP = 2
ATT_TILE = 256
GN_GROUP_COLS = 128
NP_Q, NP_KVC, NP_KVS, NP_KVW, NP_GN = 0, C_N, C_N + KV_COLS, C_N + 2 * KV_COLS, C_N + 3 * KV_COLS
NP_COLS = NP_GN + N_KVG * GN_GROUP_COLS


def _cparams(*sem):
    return pltpu.CompilerParams(dimension_semantics=sem, vmem_limit_bytes=VMEM_LIMIT_BYTES)


def _dot(a, b):
    return jnp.dot(a.astype(BF), b.astype(BF), preferred_element_type=F32)


def _dot_nt(a, b):
    return lax.dot_general(a.astype(BF), b.astype(BF), (((1,), (1,)), ((), ())), preferred_element_type=F32)


def _dot_tn(a, b):
    return lax.dot_general(a.astype(BF), b.astype(BF), (((0,), (0,)), ((), ())), preferred_element_type=F32)


def _softplus(x):
    return jnp.maximum(x, 0.0) + jnp.log1p(jnp.exp(-jnp.abs(x)))


def _sigmoid(x):
    return 1.0 / (1.0 + jnp.exp(-x))


def _gelu_tanh(x):
    return 0.5 * x * (1.0 + jnp.tanh(math.sqrt(2.0 / math.pi) * (x + 0.044715 * x * x * x)))


def _t5_bucket(dist):
    n = jnp.maximum(dist, 0)
    max_exact = N_BUCKETS // 2
    nf = jnp.maximum(n, 1).astype(F32)
    large = max_exact + (jnp.log(nf / max_exact) / math.log(MAX_DIST / max_exact)
                         * (N_BUCKETS - max_exact)).astype(I32)
    large = jnp.minimum(large, N_BUCKETS - 1)
    return jnp.where(n < max_exact, n, large)


def _bias_rows(dist, rbt):
    bucket = _t5_bucket(dist)
    out = jnp.zeros((rbt.shape[0], dist.shape[1]), F32)
    for b in range(N_BUCKETS):
        out = jnp.where(bucket == b, rbt[:, b:b + 1], out)
    return out


def _ada_kernel(c_ref, w_ref, b_ref, o_ref):
    o_ref[...] = _dot(c_ref[...], w_ref[...]) + b_ref[...]


def _ada(c, w_ada, b_ada):
    r = c.shape[0]
    n = w_ada.shape[1]
    tn = 1024
    return pl.pallas_call(
        _ada_kernel,
        grid=(n // tn,),
        in_specs=[pl.BlockSpec((r, D_MODEL), lambda j: (0, 0)),
                  pl.BlockSpec((D_MODEL, tn), lambda j: (0, j)),
                  pl.BlockSpec((1, tn), lambda j: (0, j))],
        out_specs=pl.BlockSpec((r, tn), lambda j: (0, j)),
        out_shape=jax.ShapeDtypeStruct((r, n), F32),
        compiler_params=_cparams("arbitrary"),
        name="ada_mod",
    )(c, w_ada, b_ada.reshape(1, n))


def _rms(x, g):
    return x * lax.rsqrt(jnp.mean(x * x, axis=-1, keepdims=True) + RMS_EPS) * g


def _norm_mod_kernel(x_ref, g_ref, sh_ref, sc_ref, o_ref):
    o_ref[...] = (_rms(x_ref[...], g_ref[...]) * (1.0 + sc_ref[...]) + sh_ref[...]).astype(o_ref.dtype)


def _row_specs(m, tm, rpb):
    del m
    return (lambda tn: pl.BlockSpec((tm, tn), lambda i, j: (i, j)),
            lambda r, tn: pl.BlockSpec((None, r, tn), lambda i, j: ((i * tm) // rpb, 0, j)))


def _norm_mod(x, g, shift, scale, tm, rpb):
    m = x.shape[0]
    r = shift.shape[1]
    rows, mods = _row_specs(m, tm, rpb)
    return pl.pallas_call(
        _norm_mod_kernel,
        grid=(m // tm, 1),
        in_specs=[rows(D_MODEL), pl.BlockSpec((1, D_MODEL), lambda i, j: (0, 0)), mods(r, D_MODEL), mods(r, D_MODEL)],
        out_specs=rows(D_MODEL),
        out_shape=jax.ShapeDtypeStruct((m, D_MODEL), BF),
        compiler_params=_cparams("arbitrary", "arbitrary"),
        name="norm_mod",
    )(x, g.reshape(1, D_MODEL), shift, scale)


def _mm_kernel(a_ref, w_ref, o_ref):
    o_ref[...] = jnp.dot(a_ref[...], w_ref[...], preferred_element_type=F32).astype(o_ref.dtype)


def _matmul(a, w, tm, tn, out_dtype=F32):
    m, k = a.shape
    n = w.shape[1]
    return pl.pallas_call(
        _mm_kernel,
        grid=(m // tm, n // tn),
        in_specs=[pl.BlockSpec((tm, k), lambda i, j: (i, 0)), pl.BlockSpec((k, tn), lambda i, j: (0, j))],
        out_specs=pl.BlockSpec((tm, tn), lambda i, j: (i, j)),
        out_shape=jax.ShapeDtypeStruct((m, n), out_dtype),
        compiler_params=_cparams("arbitrary", "arbitrary"),
        name="matmul",
    )(a, w)


def _merge_kernel(or_ref, on_ref, wr_ref, wn_ref, g0_ref, g1_ref, o_ref):
    yr = jnp.dot(or_ref[...], wr_ref[...], preferred_element_type=F32)
    yn = jnp.dot(on_ref[...], wn_ref[...], preferred_element_type=F32)
    o_ref[...] = (_sigmoid(g0_ref[...]) * yr + _sigmoid(g1_ref[...]) * yn).astype(o_ref.dtype)


def _merge(o_r, o_n, w_r, w_n, p_g, tm):
    m = o_r.shape[0]
    tn = 1024
    nb = D_MODEL // tn
    return pl.pallas_call(
        _merge_kernel,
        grid=(m // tm, nb),
        in_specs=[pl.BlockSpec((tm, C_R), lambda i, j: (i, 0)), pl.BlockSpec((tm, C_N), lambda i, j: (i, 0)),
                  pl.BlockSpec((C_R, tn), lambda i, j: (0, j)), pl.BlockSpec((C_N, tn), lambda i, j: (0, j)),
                  pl.BlockSpec((tm, tn), lambda i, j: (i, j)), pl.BlockSpec((tm, tn), lambda i, j: (i, j + nb))],
        out_specs=pl.BlockSpec((tm, tn), lambda i, j: (i, j)),
        out_shape=jax.ShapeDtypeStruct((m, D_MODEL), BF),
        compiler_params=_cparams("arbitrary", "arbitrary"),
        name="merge_branches",
    )(o_r, o_n, w_r, w_n, p_g, p_g)


def _proj_res_kernel(y_ref, w_ref, x_ref, g_ref, o_ref):
    o_ref[...] = x_ref[...] + g_ref[...] * jnp.dot(y_ref[...], w_ref[...], preferred_element_type=F32)


def _proj_residual(y, w, x, gate, tm, rpb):
    m = y.shape[0]
    tn = 1024
    r = gate.shape[1]
    rows, mods = _row_specs(m, tm, rpb)
    return pl.pallas_call(
        _proj_res_kernel,
        grid=(m // tm, D_MODEL // tn),
        in_specs=[pl.BlockSpec((tm, D_MODEL), lambda i, j: (i, 0)), pl.BlockSpec((D_MODEL, tn), lambda i, j: (0, j)),
                  rows(tn), mods(r, tn)],
        out_specs=rows(tn),
        out_shape=jax.ShapeDtypeStruct((m, D_MODEL), F32),
        compiler_params=_cparams("arbitrary", "arbitrary"),
        name="out_proj_residual",
    )(y, w, x, gate)


def _rwkv_features(p, p_prev, mu, w0, w2, a0, a2, g2, k_k, k_a):
    xm = p + (p_prev - p) * mu
    r = xm[:, :C_R]
    k = xm[:, C_R:2 * C_R]
    v = xm[:, 2 * C_R:3 * C_R]
    o = 3 * C_R
    wd = xm[:, o:o + LORA_W]
    ad = xm[:, o + LORA_W:o + LORA_W + LORA_A]
    gd = xm[:, o + LORA_W + LORA_A:]
    w_log = -_softplus(-(w0 + _dot(jnp.tanh(wd), w2))) - 0.5
    lw = -jnp.exp(w_log)
    a = _sigmoid(a0 + _dot(ad, a2))
    g = _dot(_sigmoid(gd), g2)
    kk = k * k_k
    k = k * (1.0 + (a - 1.0) * k_a)
    return r, k, v, lw, a, g, kk


def _head_sums(x, ones2):
    nt = x.shape[1] // LANES
    xs = jnp.concatenate([x[:, j * LANES:(j + 1) * LANES] for j in range(nt)], axis=0)
    hi = xs.astype(BF)
    lo = (xs - hi.astype(F32)).astype(BF)
    s = jnp.dot(hi, ones2, preferred_element_type=F32) + jnp.dot(lo, ones2, preferred_element_type=F32)
    r = x.shape[0]
    return jnp.concatenate([s[j * r:(j + 1) * r] for j in range(nt)], axis=1)


def _rwkv_chunk_kernel(pr_ref, mu_ref, w0_ref, w2_ref, a0_ref, a2_ref, g2_ref, kk_ref, ka_ref, rk_ref,
                       lng_ref, lnb_ref, ones_ref, o_ref, shift_ref, state_ref, y_ref):
    c = pl.program_id(1)

    @pl.when(c == 0)
    def _():
        shift_ref[...] = jnp.zeros_like(shift_ref)
        state_ref[...] = jnp.zeros_like(state_ref)

    for bb in range(pr_ref.shape[0]):
        _rwkv_chunk_body(pr_ref.at[bb], mu_ref, w0_ref, w2_ref, a0_ref, a2_ref, g2_ref, kk_ref, ka_ref, rk_ref,
                         lng_ref, lnb_ref, ones_ref, o_ref.at[bb], shift_ref.at[bb], state_ref.at[bb], y_ref.at[bb])


def _rwkv_chunk_body(pr_ref, mu_ref, w0_ref, w2_ref, a0_ref, a2_ref, g2_ref, kk_ref, ka_ref, rk_ref,
                     lng_ref, lnb_ref, ones_ref, o_ref, shift_ref, state_ref, y_ref):
    C = RW_CHUNK
    p = pr_ref[...]
    row = lax.broadcasted_iota(I32, (C, 1), 0)
    p_prev = jnp.where(row == 0, shift_ref[...], pltpu.roll(p, 1, axis=0))
    shift_ref[...] = p[C - 1:C, :]
    r, k, v, lw, a, g, kk_all = _rwkv_features(p, p_prev, mu_ref[...], w0_ref[...], w2_ref[...], a0_ref[...],
                                               a2_ref[...], g2_ref[...], kk_ref[...], ka_ref[...])
    cl = lw
    s = 1
    while s < C:
        cl = cl + jnp.where(row >= s, pltpu.roll(cl, s, axis=0), 0.0)
        s *= 2
    ti = lax.broadcasted_iota(I32, (C, C), 0)
    si = lax.broadcasted_iota(I32, (C, C), 1)
    strict = ti > si
    incl = ti >= si
    eye = (ti == si).astype(F32)
    heads = range(H_R)
    sls = [slice(h * HD_R, (h + 1) * HD_R) for h in heads]
    ones2 = ones_ref[...]
    kk_n = kk_all / jnp.maximum(jnp.sqrt(_head_sums(kk_all * kk_all, ones2)), 1e-12)
    b_all = kk_n * a
    cl_end = cl[C - 1:C, :]
    e_neg = jnp.exp(-cl)
    e_end = jnp.exp(cl_end - cl)
    g_end_all = jnp.exp(cl_end)
    per_head = lambda z: [z[:, sl] for sl in sls]
    kkt_all = (kk_n * jnp.exp(cl - lw)).astype(BF)
    rt_all = (r * jnp.exp(cl)).astype(BF)
    kr = per_head(jnp.concatenate([kkt_all, rt_all], axis=0))
    kh = per_head((k * e_neg).astype(BF))
    bh = per_head((b_all * e_neg).astype(BF))
    kbb = per_head(jnp.concatenate([(k * e_end).astype(BF), (b_all * e_end).astype(BF)], axis=0))
    vb = per_head(v.astype(BF))
    g_end = per_head(g_end_all)
    ak = [_dot_nt(kr[h], kh[h]) for h in heads]
    ab = [_dot_nt(kr[h], bh[h]) for h in heads]
    lg = [jnp.concatenate([jnp.where(strict, ak[h][:C], 0.0), jnp.where(incl, ak[h][C:], 0.0)], axis=0).astype(BF)
          for h in heads]
    nil = [jnp.where(strict, -ab[h][:C], 0.0) for h in heads]
    grb = [jnp.where(incl, ab[h][C:], 0.0).astype(BF) for h in heads]
    tinv = [eye + n for n in nil]
    m = 2
    while m < C:
        nil = [_dot(n, n) for n in nil]
        tinv = [t + _dot(t, n) for t, n in zip(tinv, nil)]
        m *= 2
    s0 = [state_ref[h] for h in heads]
    xy = [_dot_nt(kr[h], s0[h]) + _dot(lg[h], vb[h]) for h in heads]
    u = [_dot(tinv[h], xy[h][:C]).astype(BF) for h in heads]
    y = [xy[h][C:] - _dot(grb[h], u[h]) for h in heads]
    for h in heads:
        vu = jnp.concatenate([vb[h], -u[h]], axis=0)
        state_ref[h] = s0[h] * g_end[h] + _dot_tn(vu, kbb[h])
    for h, sl in enumerate(sls):
        y_ref[:, sl] = y[h]
    y_all = y_ref[...]
    yc = y_all - _head_sums(y_all, ones2) * (1.0 / HD_R)
    var = _head_sums(yc * yc, ones2) * (1.0 / HD_R)
    yn = yc * lax.rsqrt(var + LN_X_EPS) * lng_ref[...] + lnb_ref[...]
    bonus = _head_sums(r * k * rk_ref[...], ones2) * v
    o_ref[...] = ((yn + bonus) * g).astype(o_ref.dtype)


def _rwkv_weights(mu, w0, w2, a0, a2, g2, k_k, k_a, r_k, ln_g, ln_b):
    row = lambda z: z.reshape(1, -1).astype(F32)
    return (row(mu), row(w0), w2.astype(BF), row(a0), a2.astype(BF), g2.astype(BF), row(k_k), row(k_a), row(r_k),
            row(ln_g), row(ln_b))


_RWKV_W_SHAPES = ((1, C_RIN), (1, C_R), (LORA_W, C_R), (1, C_R), (LORA_A, C_R), (LORA_G, C_R), (1, C_R), (1, C_R),
                  (1, C_R), (1, C_R), (1, C_R))


def _rwkv_prompt(pr, rw):
    b, t, _ = pr.shape
    C = RW_CHUNK
    full = lambda shp: pl.BlockSpec(shp, lambda i, j: (0,) * len(shp))
    lane_head = np.arange(LANES) // HD_R
    ones2 = jnp.asarray(lane_head[:, None] == lane_head[None, :], BF)
    nb = math.gcd(b, RW_SEQS_PER_STEP)
    return pl.pallas_call(
        _rwkv_chunk_kernel,
        grid=(b // nb, t // C),
        in_specs=[pl.BlockSpec((nb, C, C_RIN), lambda i, j: (i, j, 0))] + [full(s) for s in _RWKV_W_SHAPES]
        + [full((LANES, LANES))],
        out_specs=[pl.BlockSpec((nb, C, C_R), lambda i, j: (i, j, 0)),
                   pl.BlockSpec((nb, 1, C_RIN), lambda i, j: (i, 0, 0)),
                   pl.BlockSpec((nb, H_R, HD_R, HD_R), lambda i, j: (i, 0, 0, 0))],
        out_shape=[jax.ShapeDtypeStruct((b, t, C_R), BF),
                   jax.ShapeDtypeStruct((b, 1, C_RIN), F32),
                   jax.ShapeDtypeStruct((b, H_R, HD_R, HD_R), F32)],
        scratch_shapes=[pltpu.VMEM((nb, C, C_R), F32)],
        compiler_params=_cparams("arbitrary", "arbitrary"),
        name="rwkv_chunk",
    )(pr, *rw, ones2)


def _rwkv_step_kernel(pr_ref, prev_ref, s0_ref, mu_ref, w0_ref, w2_ref, a0_ref, a2_ref, g2_ref, kk_ref, ka_ref,
                      rk_ref, lng_ref, lnb_ref, o_ref, state_ref):
    nb = pr_ref.shape[0]
    r, k, v, lw, a, g, kk_all = _rwkv_features(pr_ref[...], prev_ref[...], mu_ref[...], w0_ref[...], w2_ref[...],
                                               a0_ref[...], a2_ref[...], g2_ref[...], kk_ref[...], ka_ref[...])
    decay = jnp.exp(lw)
    ii = lax.broadcasted_iota(I32, (HD_R, HD_R), 0)
    jj = lax.broadcasted_iota(I32, (HD_R, HD_R), 1)
    eye = ii == jj
    col = lambda z: jnp.sum(jnp.where(eye, z, 0.0), axis=1, keepdims=True)
    heads = range(H_R)
    sls = [slice(h * HD_R, (h + 1) * HD_R) for h in heads]
    for bi in range(nb):
        rows = lambda z: [z[bi:bi + 1, sl] for sl in sls]
        r_h, k_h, v_h, a_h, w_h, g_h, kk_h = rows(r), rows(k), rows(v), rows(a), rows(decay), rows(g), rows(kk_all)
        nrm = [jnp.maximum(jnp.sqrt(jnp.sum(z * z, axis=-1, keepdims=True)), 1e-12) for z in kk_h]
        kk_h = [kk_h[h] / nrm[h] for h in heads]
        s0 = [s0_ref[bi, h] for h in heads]
        sa = [jnp.sum(s0[h] * (-kk_h[h]), axis=1, keepdims=True) for h in heads]
        v_col = [col(z) for z in v_h]
        s1 = [s0[h] * w_h[h] + sa[h] * (kk_h[h] * a_h[h]) + v_col[h] * k_h[h] for h in heads]
        for h in heads:
            state_ref[bi, h] = s1[h]
        y_col = [jnp.sum(s1[h] * r_h[h], axis=1, keepdims=True) for h in heads]
        y = [jnp.sum(jnp.where(eye, z, 0.0), axis=0, keepdims=True) for z in y_col]
        yc = [z - jnp.mean(z, axis=-1, keepdims=True) for z in y]
        var = [jnp.mean(z * z, axis=-1, keepdims=True) for z in yc]
        bonus = [jnp.sum(r_h[h] * k_h[h] * rk_ref[:, sls[h]], axis=-1, keepdims=True) * v_h[h] for h in heads]
        for h, sl in enumerate(sls):
            yn = yc[h] * lax.rsqrt(var[h] + LN_X_EPS) * lng_ref[:, sl] + lnb_ref[:, sl]
            o_ref[bi:bi + 1, sl] = ((yn + bonus[h]) * g_h[h]).astype(o_ref.dtype)


def _rwkv_step(pr, prev, s0, rw):
    b = pr.shape[0]
    return pl.pallas_call(
        _rwkv_step_kernel,
        out_shape=[jax.ShapeDtypeStruct((b, C_R), BF), jax.ShapeDtypeStruct((b, H_R, HD_R, HD_R), F32)],
        compiler_params=pltpu.CompilerParams(vmem_limit_bytes=VMEM_LIMIT_BYTES),
        name="rwkv_step",
    )(pr, prev, s0, *rw)


def _cmp_partial_kernel(*refs):
    x_refs, w_ref, o_ref = refs[:-2], refs[-2], refs[-1]
    x = x_refs[0][...] if len(x_refs) == 1 else jnp.concatenate([r[...] for r in x_refs], axis=0)
    half = N_KVG * HD_N
    for s in range(2):
        acc = jnp.zeros((x.shape[0], N_KVG * 2 * CMP_HID), F32)
        for p in range(STRIDE):
            o = p * KV_COLS + s * half
            acc = acc + _dot(x[:, o:o + half], w_ref[p, s])
        o_ref[:, s * N_KVG * 2 * CMP_HID:(s + 1) * N_KVG * 2 * CMP_HID] = acc


CMP_PAIR = 2


def _cmp_partial_paged_kernel(pt_ref, *refs):
    del pt_ref
    x_refs, perm_ref, w_ref, o_ref = refs[:-3], refs[-3], refs[-2], refs[-1]
    cpp = PAGE_SIZE // STRIDE
    width = CMP_PAIR * 2 * CMP_HID
    perm = perm_ref[...]
    for s in range(2):
        for gp in range(N_KVG // CMP_PAIR):
            rows = []
            for x_ref in x_refs:
                tile = jnp.concatenate([x_ref[s, CMP_PAIR * gp + j] for j in range(CMP_PAIR)], axis=0)
                rows.append(_dot_nt(perm, tile))
            acc = jnp.zeros((len(x_refs) * cpp, width), F32)
            for p in range(STRIDE):
                lhs = jnp.concatenate([r[p * cpp:(p + 1) * cpp] for r in rows], axis=0)
                acc = acc + _dot(lhs, w_ref[p, s])
            o = (s * (N_KVG // CMP_PAIR) + gp) * width
            o_ref[:, o:o + width] = acc


def _cmp_weights(cmp_w1, groups):
    w1r = cmp_w1.reshape(2, 2, STRIDE, HD_N, CMP_HID)
    w = jnp.transpose(w1r, (2, 0, 3, 1, 4))
    w = w.reshape(STRIDE, 2, 1, HD_N, 1, 2 * CMP_HID)
    eye = jnp.eye(groups, dtype=w.dtype).reshape(1, 1, groups, 1, groups, 1)
    wbd = eye * w
    return wbd.reshape(STRIDE, 2, groups * HD_N, groups * 2 * CMP_HID).astype(BF)


def _cmp_partial_rows(x, wbd, tr):
    r = x.shape[0]
    n = 2 * N_KVG * 2 * CMP_HID
    return pl.pallas_call(
        _cmp_partial_kernel,
        grid=(r // tr,),
        in_specs=[pl.BlockSpec((tr, STRIDE * KV_COLS), lambda i: (i, 0)),
                  pl.BlockSpec(wbd.shape, lambda i: (0, 0, 0, 0))],
        out_specs=pl.BlockSpec((tr, n), lambda i: (i, 0)),
        out_shape=jax.ShapeDtypeStruct((r, n), F32),
        compiler_params=_cparams("arbitrary"),
        name="cmp_partial",
    )(x, wbd)


PAGES_PER_STEP = 16


def _cmp_partial_paged(cache_t, page_table, wpair):
    b, npg = page_table.shape
    cpp = PAGE_SIZE // STRIDE
    n = 2 * N_KVG * 2 * CMP_HID
    steps = npg // PAGES_PER_STEP
    perm = np.zeros((PAGE_SIZE, PAGE_SIZE), np.float32)
    tok = np.arange(PAGE_SIZE)
    perm[(tok % STRIDE) * cpp + tok // STRIDE, tok] = 1.0

    def page_spec(kpg):
        return pl.BlockSpec((None, 2, N_KVG, HD_N, PAGE_SIZE),
                            lambda i, j, pt: (pt[i, j * PAGES_PER_STEP + kpg], 0, 0, 0, 0))

    grid_spec = pltpu.PrefetchScalarGridSpec(
        num_scalar_prefetch=1,
        grid=(b, steps),
        in_specs=[page_spec(kpg) for kpg in range(PAGES_PER_STEP)]
        + [pl.BlockSpec((PAGE_SIZE, PAGE_SIZE), lambda i, j, pt: (0, 0)),
           pl.BlockSpec(wpair.shape, lambda i, j, pt: (0, 0, 0, 0))],
        out_specs=pl.BlockSpec((None, PAGES_PER_STEP * cpp, n), lambda i, j, pt: (i, j, 0)),
    )
    return pl.pallas_call(
        _cmp_partial_paged_kernel,
        grid_spec=grid_spec,
        out_shape=jax.ShapeDtypeStruct((b, npg * cpp, n), F32),
        compiler_params=_cparams("arbitrary", "arbitrary"),
        name="cmp_partial_paged",
    )(page_table, *([cache_t] * PAGES_PER_STEP), jnp.asarray(perm, BF), wpair)


def _cmp_finish_kernel(c_ref, cpos_ref, w2_ref, o_ref):
    c = c_ref[...]
    nrow = c.shape[0]
    c_next = pltpu.roll(c, nrow - 1, axis=0)
    for s in range(2):
        for gi in range(N_KVG):
            o = (s * N_KVG + gi) * 2 * CMP_HID
            hid = (c[:, o:o + CMP_HID] + cpos_ref[0:1, o:o + CMP_HID]
                   + c_next[:, o + CMP_HID:o + 2 * CMP_HID] + cpos_ref[1:2, o + CMP_HID:o + 2 * CMP_HID])
            oo = (s * N_KVG + gi) * HD_N
            o_ref[:, oo:oo + HD_N] = _dot(_gelu_tanh(hid), w2_ref[s])


def _cmp_finish(c, cpos, w2):
    b, nch, n = c.shape
    return pl.pallas_call(
        _cmp_finish_kernel,
        grid=(b,),
        in_specs=[pl.BlockSpec((None, nch, n), lambda i: (i, 0, 0)), pl.BlockSpec(cpos.shape, lambda i: (0, 0)),
                  pl.BlockSpec(w2.shape, lambda i: (0, 0, 0))],
        out_specs=pl.BlockSpec((None, nch, KV_COLS), lambda i: (i, 0, 0)),
        out_shape=jax.ShapeDtypeStruct((b, nch, KV_COLS), F32),
        compiler_params=_cparams("arbitrary"),
        name="cmp_finish",
    )(c, cpos, w2.astype(BF))


def _cmp_pos_rows(cmp_pos):
    pos = cmp_pos.reshape(2, STRIDE, 1, 1, HD_N)
    rows = jnp.broadcast_to(pos, (2, STRIDE, 2, N_KVG, HD_N)).reshape(2, STRIDE * KV_COLS)
    return jnp.concatenate([rows, jnp.zeros((6, STRIDE * KV_COLS), F32)], axis=0)


def _bias_table_kernel(dist_ref, rb_ref, o_ref):
    bucket = _t5_bucket(dist_ref[...])
    for h in range(H_N):
        out = jnp.zeros(bucket.shape, F32)
        for b in range(N_BUCKETS):
            out = jnp.where(bucket == b, rb_ref[b, h], out)
        o_ref[h] = out


def _bias_table(dist, rel_bias, tr):
    r, n = dist.shape
    return pl.pallas_call(
        _bias_table_kernel,
        grid=(r // tr,),
        in_specs=[pl.BlockSpec((tr, n), lambda i: (i, 0)),
                  pl.BlockSpec(memory_space=pltpu.SMEM)],
        out_specs=pl.BlockSpec((H_N, tr, n), lambda i: (0, i, 0)),
        out_shape=jax.ShapeDtypeStruct((H_N, r, n), F32),
        compiler_params=_cparams("arbitrary"),
        name="bias_table",
    )(dist, rel_bias)


def _softmax_rows(logits, valid):
    lm = jnp.where(valid, logits, NEG_INF)
    e = jnp.exp(lm - jnp.max(lm, axis=-1, keepdims=True))
    return e / jnp.sum(e, axis=-1, keepdims=True)


def _cmp_attn_kernel(q_ref, kv_ref, bias_ref, ovt_ref, o_ref, sel_ref):
    tq = q_ref.shape[0]
    ncp = kv_ref.shape[0]
    nsb = ovt_ref.shape[0]
    q0 = pl.program_id(1) * tq
    qpos = q0 + lax.broadcasted_iota(I32, (tq, 1), 0)
    cend = lax.broadcasted_iota(I32, (1, ncp), 1) * STRIDE + (L_CMP - 1)
    valid = (qpos >= cend) & (lax.broadcasted_iota(I32, (1, ncp), 1) < ncp - 1)
    validf = valid.astype(F32)
    q = q_ref[...] * (HD_N ** -0.5)
    blk = lax.broadcasted_iota(I32, (nsb, tq), 0)
    cur = (q0 + lax.broadcasted_iota(I32, (1, tq), 1)) // L_SEL
    forced = (blk == 0) | (blk == cur) | (blk == cur - 1)
    future = blk > cur
    for gi in range(N_KVG):
        kc = kv_ref[:, gi * HD_N:(gi + 1) * HD_N]
        vc = kv_ref[:, (N_KVG + gi) * HD_N:(N_KVG + gi + 1) * HD_N]
        pcs = jnp.zeros((tq, ncp), F32)
        for hl in range(HPG):
            h = gi * HPG + hl
            sl = slice(h * HD_N, (h + 1) * HD_N)
            pc = _softmax_rows(_dot_nt(q[:, sl], kc) + bias_ref[h], valid) * validf
            pcs = pcs + pc
            o_ref[:, sl] = _dot(pc, vc)
        imp = _dot_nt(ovt_ref[...], pcs)
        score = jnp.where(forced, FORCE, jnp.where(future, -FORCE, imp))
        rank = jnp.zeros((nsb, tq), F32)
        for i in range(nsb):
            si = score[i:i + 1, :]
            rank = rank + ((si > score) | ((si == score) & (i < blk))).astype(F32)
        sel_ref[gi] = (rank < N_TOP).astype(sel_ref.dtype)


def _cmp_sel_overlap_t(nc, ncp, nsb):
    s = np.arange(nc)[None, :] * STRIDE
    j = np.arange(nsb)[:, None] * L_SEL
    ov = np.clip(np.minimum(s + L_CMP, j + L_SEL) - np.maximum(s, j), 0, None) / L_CMP
    return np.pad(ov, ((0, 0), (0, ncp - nc))).astype(np.float32)


def _cmp_attn_prompt(p_n, kv_cmp, bias_c, t):
    b = kv_cmp.shape[0]
    ncp = kv_cmp.shape[1]
    nsb = t // L_SEL
    tq = ATT_TILE
    nqt = t // tq
    ovt = jnp.asarray(_cmp_sel_overlap_t(ncp - 1, ncp, nsb), BF)
    return pl.pallas_call(
        _cmp_attn_kernel,
        grid=(b, nqt),
        in_specs=[pl.BlockSpec((tq, C_N), lambda i, j: (i * nqt + j, 0)),
                  pl.BlockSpec((None, ncp, KV_COLS), lambda i, j: (i, 0, 0)),
                  pl.BlockSpec((H_N, tq, ncp), lambda i, j: (0, j, 0)),
                  pl.BlockSpec((nsb, ncp), lambda i, j: (0, 0))],
        out_specs=[pl.BlockSpec((tq, C_N), lambda i, j: (i * nqt + j, 0)),
                   pl.BlockSpec((None, N_KVG, nsb, tq), lambda i, j: (i, 0, 0, j))],
        out_shape=[jax.ShapeDtypeStruct((b * t, C_N), F32), jax.ShapeDtypeStruct((b, N_KVG, nsb, t), BF)],
        compiler_params=_cparams("arbitrary", "arbitrary"),
        name="nsa_cmp_select",
    )(p_n, kv_cmp, bias_c, ovt)


ATT_TK = 128
ATT_R = ATT_TILE // ATT_TK
N_SEL_OFFS = ATT_R + (MAX_DIST + ATT_TK - 1) // ATT_TK + 1
N_WIN_OFFS = ATT_R + WINDOW // ATT_TK
QA_COLS = HD_N + 32
SEL_STEP_TILES = 4
WIN_STEP_TILES = 2


def _swa_kernel(qa_ref, ks_ref, vs_ref, kw_ref, vw_ref, tabs_ref, tabw_ref, gn_ref, oc_ref, o_ref):
    tq = qa_ref.shape[1]
    qt = pl.program_id(2)
    top = ATT_R * qt + ATT_R - 1

    heads = range(HPG)

    def attend(k_ref, v_ref, tab_ref, lo, n_off, step_tiles):
        tiles = range(step_tiles)

        def body(kp, carry):
            ki = [lo + step_tiles * kp + j for j in tiles]
            kt = [k_ref[jnp.minimum(i, top)] for i in ki]
            vt = [v_ref[jnp.minimum(i, top)] for i in ki]
            off = [jnp.where(i > top, n_off, jnp.minimum(top - i, n_off - 1)) for i in ki]
            s = [[lax.dot_general(kt[j], qa_ref[hl], (((1,), (1,)), ((), ())), preferred_element_type=F32)
                  + tab_ref[off[j], hl * ATT_TK:(hl + 1) * ATT_TK, :] for j in tiles] for hl in heads]
            m_new = [functools.reduce(jnp.maximum, [carry[hl][0]] + [jnp.max(s[hl][j], axis=0, keepdims=True)
                                                                    for j in tiles]) for hl in heads]
            alpha = [jnp.exp(carry[hl][0] - m_new[hl]) for hl in heads]
            p = [[jnp.exp(s[hl][j] - m_new[hl]) for j in tiles] for hl in heads]
            l = [alpha[hl] * carry[hl][1] + sum(jnp.sum(p[hl][j], axis=0, keepdims=True) for j in tiles)
                 for hl in heads]
            acc = [alpha[hl] * carry[hl][2] + sum(jnp.dot(vt[j], p[hl][j].astype(BF), preferred_element_type=F32)
                                                  for j in tiles) for hl in heads]
            return tuple((m_new[hl], l[hl], acc[hl]) for hl in heads)

        init = tuple((jnp.full((1, tq), NEG_INF, F32), jnp.zeros((1, tq), F32), jnp.zeros((HD_N, tq), F32))
                     for _ in heads)
        res = lax.fori_loop(0, (top - lo) // step_tiles + 1, body, init)
        return [acc / l for _, l, acc in res]

    o_sel = attend(ks_ref, vs_ref, tabs_ref, 0, N_SEL_OFFS, SEL_STEP_TILES)
    o_win = attend(kw_ref, vw_ref, tabw_ref, jnp.maximum(top + 1 - N_WIN_OFFS, 0), N_WIN_OFFS, WIN_STEP_TILES)
    gates = _sigmoid(gn_ref[...])
    gates_t = gates.T
    for hl in heads:
        sl = slice(hl * HD_N, (hl + 1) * HD_N)
        o_t = gates_t[3 * hl + 1:3 * hl + 2, :] * o_sel[hl] + gates_t[3 * hl + 2:3 * hl + 3, :] * o_win[hl]
        o_ref[:, sl] = (gates[:, 3 * hl:3 * hl + 1] * oc_ref[:, sl] + o_t.T).astype(o_ref.dtype)


def _swa_prompt(p_n, qa, ks, vs, kw, vw, tab_s, tab_w, o_cmp, t):
    b = qa.shape[0]
    tq = ATT_TILE
    nqt = t // tq
    nkt = t // ATT_TK
    gw = HPG * HD_N
    k_spec = pl.BlockSpec((None, None, nkt, ATT_TK, QA_COLS), lambda i, g, j: (i, g, 0, 0, 0))
    v_spec = pl.BlockSpec((None, None, nkt, HD_N, ATT_TK), lambda i, g, j: (i, g, 0, 0, 0))
    tab_spec = lambda n: pl.BlockSpec((n + 1, HPG * ATT_TK, tq), lambda i, g, j: (0, g, 0))
    return pl.pallas_call(
        _swa_kernel,
        grid=(b, N_KVG, nqt),
        in_specs=[pl.BlockSpec((None, HPG, tq, QA_COLS), lambda i, g, j: (i, g, j, 0)),
                  k_spec, v_spec, k_spec, v_spec, tab_spec(N_SEL_OFFS), tab_spec(N_WIN_OFFS),
                  pl.BlockSpec((tq, GN_GROUP_COLS), lambda i, g, j: (i * nqt + j, NP_GN // GN_GROUP_COLS + g)),
                  pl.BlockSpec((tq, gw), lambda i, g, j: (i * nqt + j, g))],
        out_specs=pl.BlockSpec((tq, gw), lambda i, g, j: (i * nqt + j, g)),
        out_shape=jax.ShapeDtypeStruct((b * t, C_N), BF),
        compiler_params=_cparams("arbitrary", "arbitrary", "arbitrary"),
        name="nsa_sel_win",
    )(qa, ks, vs, kw, vw, tab_s, tab_w, p_n, o_cmp)


def _swa_operands(p_n, sel, b, t):
    nsb = t // L_SEL
    nkt = t // ATT_TK
    q = (p_n[:, :C_N] * (HD_N ** -0.5)).reshape(b, t, H_N, HD_N).transpose(0, 2, 1, 3)
    pen = jnp.where(jnp.transpose(sel, (0, 1, 3, 2)) > 0.5, 0.0, NEG_INF).astype(F32)
    pen = jnp.broadcast_to(pen[:, :, None], (b, N_KVG, HPG, t, nsb)).reshape(b, H_N, t, nsb)
    qa = jnp.concatenate([q, pen], axis=-1).astype(BF)

    def split(cols):
        kv = p_n[:, cols:cols + KV_COLS].reshape(b, t, 2, N_KVG, HD_N)
        k = jnp.transpose(kv[:, :, 0], (0, 2, 1, 3))
        v = jnp.transpose(kv[:, :, 1], (0, 2, 3, 1)).reshape(b, N_KVG, HD_N, nkt, ATT_TK)
        return k, jnp.transpose(v, (0, 1, 3, 2, 4)).astype(BF)

    onehot = jnp.asarray(np.arange(t)[:, None] // L_SEL == np.arange(nsb)[None, :], F32)
    k_s, v_s = split(NP_KVS)
    k_w, v_w = split(NP_KVW)
    ext = lambda k, e: jnp.concatenate([k, jnp.broadcast_to(e, (b, N_KVG, t, nsb))], axis=-1).astype(BF) \
        .reshape(b, N_KVG, nkt, ATT_TK, QA_COLS)
    return qa, ext(k_s, onehot), v_s, ext(k_w, jnp.zeros((t, nsb), F32)), v_w


def _sample_cmp_kernel(q_ref, kv_ref, rbt_ref, ov_ref, o_ref, idx_ref, *, past):
    ncp = kv_ref.shape[0]
    nsbp = ov_ref.shape[1]
    nsb = -(-(past + 1) // L_SEL)
    q = q_ref[...] * (HD_N ** -0.5)
    hrow = lax.broadcasted_iota(I32, (H_N, 1), 0)
    nidx = lax.broadcasted_iota(I32, (1, ncp), 1)
    valid = nidx < ncp - 1
    bias = _bias_rows(past - (nidx * STRIDE + (L_CMP - 1)), rbt_ref[...])
    logits = jnp.zeros((H_N, ncp), F32)
    for gi in range(N_KVG):
        lg = _dot_nt(q, kv_ref[:, gi * HD_N:(gi + 1) * HD_N])
        logits = jnp.where(hrow // HPG == gi, lg, logits)
    pc = _softmax_rows(logits + bias, valid) * valid.astype(F32)
    o = jnp.zeros((H_N, HD_N), F32)
    for gi in range(N_KVG):
        og = _dot(pc, kv_ref[:, (N_KVG + gi) * HD_N:(N_KVG + gi + 1) * HD_N])
        o = jnp.where(hrow // HPG == gi, og, o)
    o_ref[...] = o
    imp_h = _dot(pc, ov_ref[...])
    blk = lax.broadcasted_iota(I32, (8, nsbp), 1)
    grow = lax.broadcasted_iota(I32, (8, 1), 0)
    cur = past // L_SEL
    score = jnp.full((8, nsbp), -3e38, F32)
    for gi in range(N_KVG):
        imp = jnp.sum(jnp.where(hrow // HPG == gi, imp_h, 0.0), axis=0, keepdims=True)
        score = jnp.where(grow == gi, imp, score)
    forced = (blk == 0) | (blk == cur) | (blk == cur - 1)
    score = jnp.where(forced, FORCE, jnp.where(blk > cur, -FORCE, score))
    score = jnp.where((blk < nsb) & (grow < N_KVG), score, -3e38)
    lane = lax.broadcasted_iota(I32, (8, LANES), 1)
    picks = jnp.zeros((8, LANES), I32)
    for it in range(N_TOP):
        mx = jnp.max(score, axis=-1, keepdims=True)
        pick = jnp.min(jnp.where(score == mx, blk, nsbp), axis=-1, keepdims=True)
        picks = jnp.where(lane == it, pick, picks)
        score = jnp.where(blk == pick, -3e38, score)
    idx_ref[...] = picks


def _sample_cmp(q, kv_cmp, rel_bias, past):
    b, ncp, _ = kv_cmp.shape
    nc = ncp - 1
    nsb = -(-(past + 1) // L_SEL)
    nsbp = -(-nsb // LANES) * LANES
    s = np.arange(nc)[:, None] * STRIDE
    j = np.arange(nsb)[None, :] * L_SEL
    ov = np.clip(np.minimum(s + L_CMP, j + L_SEL) - np.maximum(s, j), 0, None) / L_CMP
    ov = np.pad(ov, ((0, ncp - nc), (0, nsbp - nsb))).astype(np.float32)
    return pl.pallas_call(
        functools.partial(_sample_cmp_kernel, past=past),
        grid=(b,),
        in_specs=[pl.BlockSpec((None, H_N, HD_N), lambda i: (i, 0, 0)),
                  pl.BlockSpec((None, ncp, KV_COLS), lambda i: (i, 0, 0)),
                  pl.BlockSpec((H_N, N_BUCKETS), lambda i: (0, 0)),
                  pl.BlockSpec((ncp, nsbp), lambda i: (0, 0))],
        out_specs=[pl.BlockSpec((None, H_N, HD_N), lambda i: (i, 0, 0)),
                   pl.BlockSpec((None, 8, LANES), lambda i: (i, 0, 0))],
        out_shape=[jax.ShapeDtypeStruct((b, H_N, HD_N), F32), jax.ShapeDtypeStruct((b, 8, LANES), I32)],
        compiler_params=_cparams("arbitrary"),
        name="nsa_sample_cmp_select",
    )(q, kv_cmp, rel_bias.T, jnp.asarray(ov, BF))


def _block_copy_kernel(pg_ref, *refs):
    del pg_ref
    x_refs, o_ref = refs[:-1], refs[-1]
    for n, x_ref in enumerate(x_refs):
        o_ref[n] = x_ref[...]


def _gather_sel_pages(cache_t, page):
    rows, nslot = page.shape

    def slot_spec(n):
        return pl.BlockSpec((None, 2, None, HD_N, PAGE_SIZE), lambda i, pg: (pg[i, n], 0, i % N_KVG, 0, 0))

    return pl.pallas_call(
        _block_copy_kernel,
        grid_spec=pltpu.PrefetchScalarGridSpec(
            num_scalar_prefetch=1, grid=(rows,),
            in_specs=[slot_spec(n) for n in range(nslot)],
            out_specs=pl.BlockSpec((None, nslot, 2, HD_N, PAGE_SIZE), lambda i, pg: (i, 0, 0, 0, 0))),
        out_shape=jax.ShapeDtypeStruct((rows, nslot, 2, HD_N, PAGE_SIZE), cache_t.dtype),
        compiler_params=_cparams("arbitrary"),
        name="gather_sel_pages",
    )(page, *([cache_t] * nslot))


def _sample_swa_kernel(idx_ref, q_ref, blk_ref, win_ref, new_ref, rbt_ref, gate_ref, oc_ref, o_ref, *, past):
    bi = pl.program_id(0)
    q = q_ref[...] * (HD_N ** -0.5)
    rbt = rbt_ref[...]
    hrow = lax.broadcasted_iota(I32, (H_N, 1), 0)
    nk = N_TOP * PAGE_SIZE
    lane = lax.broadcasted_iota(I32, (1, nk), 1)
    new_blk = past // L_SEL
    bias_new = rbt[:, 0:1]
    gates = _sigmoid(gate_ref[...])
    nwin = win_ref.shape[0]
    wdist = nwin - lax.broadcasted_iota(I32, (1, nwin), 1)
    wbias = _bias_rows(wdist, rbt)
    wvalid = (wdist >= 0) & (wdist <= WINDOW)

    def with_new(logits, valid, weigh, k_new, v_new):
        l_new = jnp.sum(q * k_new, axis=-1, keepdims=True) + bias_new
        lm = jnp.where(valid, logits, NEG_INF)
        m = jnp.maximum(jnp.max(lm, axis=-1, keepdims=True), l_new)
        e = jnp.where(valid, jnp.exp(lm - m), 0.0)
        e_new = jnp.exp(l_new - m)
        den = jnp.sum(e, axis=-1, keepdims=True) + e_new
        return (weigh(e) + e_new * v_new) / den

    o_sel = jnp.zeros((H_N, HD_N), F32)
    o_win = jnp.zeros((H_N, HD_N), F32)
    bpp = PAGE_SIZE // L_SEL
    tok = lane % PAGE_SIZE
    for gi in range(N_KVG):
        ksl = slice(gi * HD_N, (gi + 1) * HD_N)
        vsl = slice((N_KVG + gi) * HD_N, (N_KVG + gi + 1) * HD_N)
        bid = jnp.zeros((1, nk), I32)
        for n in range(N_TOP):
            bid = jnp.where(lane // PAGE_SIZE == n, idx_ref[bi, gi, n], bid)
        dist = past - ((bid // bpp) * PAGE_SIZE + tok)
        valid = (bid != new_blk) & (tok // L_SEL == bid % bpp) & (dist >= 0)
        kt = jnp.concatenate([blk_ref[gi, n, 0] for n in range(N_TOP)], axis=1)
        vt = jnp.concatenate([blk_ref[gi, n, 1] for n in range(N_TOP)], axis=1)
        logits = _dot(q, kt) + _bias_rows(dist, rbt)
        og = with_new(logits, valid, lambda e, vt=vt: _dot_nt(e, vt), new_ref[0:1, ksl], new_ref[0:1, vsl])
        o_sel = jnp.where(hrow // HPG == gi, og, o_sel)
        logits = _dot_nt(q, win_ref[:, ksl]) + wbias
        og = with_new(logits, wvalid, lambda e, vsl=vsl: _dot(e, win_ref[:, vsl]), new_ref[1:2, ksl],
                      new_ref[1:2, vsl])
        o_win = jnp.where(hrow // HPG == gi, og, o_win)
    o_ref[...] = gates[:, 0:1] * oc_ref[...] + gates[:, 1:2] * o_sel + gates[:, 2:3] * o_win


def _sample_swa(idx, q, blocks, win, new_kv, rel_bias, gates, o_cmp, past):
    b = q.shape[0]
    w = win.shape[1]
    grid_spec = pltpu.PrefetchScalarGridSpec(
        num_scalar_prefetch=1, grid=(b,),
        in_specs=[pl.BlockSpec((None, H_N, HD_N), lambda i, ix: (i, 0, 0)),
                  pl.BlockSpec((None, N_KVG, N_TOP, 2, HD_N, PAGE_SIZE), lambda i, ix: (i, 0, 0, 0, 0, 0)),
                  pl.BlockSpec((None, w, KV_COLS), lambda i, ix: (i, 0, 0)),
                  pl.BlockSpec((None, 2, KV_COLS), lambda i, ix: (i, 0, 0)),
                  pl.BlockSpec((H_N, N_BUCKETS), lambda i, ix: (0, 0)),
                  pl.BlockSpec((None, H_N, 3), lambda i, ix: (i, 0, 0)),
                  pl.BlockSpec((None, H_N, HD_N), lambda i, ix: (i, 0, 0))],
        out_specs=pl.BlockSpec((None, H_N, HD_N), lambda i, ix: (i, 0, 0)))
    return pl.pallas_call(
        functools.partial(_sample_swa_kernel, past=past),
        grid_spec=grid_spec,
        out_shape=jax.ShapeDtypeStruct((b, H_N, HD_N), F32),
        compiler_params=_cparams("arbitrary"),
        name="nsa_sample_sel_win",
    )(idx, q, blocks, win, new_kv, rel_bias.T, gates, o_cmp)


ROUTER_COLS = LANES
MOE_ROWS_PROMPT = 128
MOE_ROWS_SAMPLE = 16


def _router_kernel(x_ref, g_ref, sh_ref, sc_ref, w_ref, b_ref, h_ref, e_ref, wt_ref, rk_ref, cnt_ref):
    i = pl.program_id(0)
    tm = x_ref.shape[0]

    @pl.when(i == 0)
    def _():
        cnt_ref[...] = jnp.zeros_like(cnt_ref)

    h = (_rms(x_ref[...], g_ref[...]) * (1.0 + sc_ref[...]) + sh_ref[...]).astype(BF)
    _store_folded(h_ref, h.astype(F32))
    logits = jnp.dot(h, w_ref[...], preferred_element_type=F32) + b_ref[...]
    lane = lax.broadcasted_iota(I32, (tm, ROUTER_COLS), 1)

    def top1(vals, ok):
        vm = jnp.where(ok, vals, -3e38)
        mx = jnp.max(vm, axis=-1, keepdims=True)
        return mx, jnp.min(jnp.where(ok & (vm == mx), lane, ROUTER_COLS), axis=-1, keepdims=True)

    isg = lane < N_EGROUPS
    pg = _softmax_rows(logits, isg)
    g_w, g_i = top1(pg, isg)
    ise = (lane >= N_EGROUPS) & ((lane - N_EGROUPS) // EXP_PER_GROUP == g_i)
    pe = _softmax_rows(logits, ise)
    w0, l0 = top1(pe, ise)
    w1, l1 = top1(pe, ise & (lane != l0))
    den = w0 + w1
    e0 = l0 - N_EGROUPS
    e1 = l1 - N_EGROUPS
    e_ref[...] = jnp.where(lane == 0, e0, jnp.where(lane == 1, e1, 0))
    wt_ref[...] = jnp.where(lane == 0, w0 / den * g_w, jnp.where(lane == 1, w1 / den * g_w, 0.0))
    oh0 = (lane == e0).astype(F32)
    oh1 = (lane == e1).astype(F32)
    cnt = oh0 + oh1
    ti = lax.broadcasted_iota(I32, (tm, tm), 0)
    si = lax.broadcasted_iota(I32, (tm, tm), 1)
    before = _dot((ti > si).astype(F32), cnt) + cnt_ref[...]
    r0 = jnp.sum(before * oh0, axis=-1, keepdims=True)
    r1 = jnp.sum(before * oh1, axis=-1, keepdims=True)
    rk_ref[...] = jnp.where(lane == 0, r0, jnp.where(lane == 1, r1, 0.0)).astype(I32)
    cnt_ref[...] = cnt_ref[...] + jnp.sum(cnt, axis=0, keepdims=True)


def _router(x, g, shift, scale, w_r, b_r, tm, rpb):
    m = x.shape[0]
    r = shift.shape[1]
    rows = lambda tn: pl.BlockSpec((tm, tn), lambda i: (i, 0))
    mods = pl.BlockSpec((None, r, D_MODEL), lambda i: ((i * tm) // rpb, 0, 0))
    small = lambda dt: jax.ShapeDtypeStruct((m, ROUTER_COLS), dt)
    return pl.pallas_call(
        _router_kernel,
        grid=(m // tm,),
        in_specs=[rows(D_MODEL), pl.BlockSpec((1, D_MODEL), lambda i: (0, 0)), mods, mods,
                  pl.BlockSpec((D_MODEL, ROUTER_COLS), lambda i: (0, 0)),
                  pl.BlockSpec((1, ROUTER_COLS), lambda i: (0, 0))],
        out_specs=[pl.BlockSpec((tm * ROW_FOLD, LANES), lambda i: (i, 0)),
                   rows(ROUTER_COLS), rows(ROUTER_COLS), rows(ROUTER_COLS),
                   pl.BlockSpec((1, ROUTER_COLS), lambda i: (0, 0))],
        out_shape=[jax.ShapeDtypeStruct((m * ROW_FOLD, LANES), F32), small(I32), small(F32), small(I32),
                   jax.ShapeDtypeStruct((1, ROUTER_COLS), F32)],
        compiler_params=_cparams("arbitrary"),
        name="moe_router",
    )(x, g.reshape(1, D_MODEL), shift, scale, w_r, b_r)


ROW_FOLD = D_MODEL // LANES
ROW_PITCH = ROW_FOLD + 8


def _store_folded(ref, x):
    n = x.shape[0]
    for c in range(ROW_FOLD):
        ref[pl.ds(c, n, stride=ROW_FOLD), :] = x[:, c * LANES:(c + 1) * LANES]


def _load_folded(ref, first_row, n):
    return jnp.concatenate([ref[pl.ds(first_row * ROW_PITCH + c, n, stride=ROW_PITCH), :] for c in range(ROW_FOLD)],
                           axis=1)


def _row_gather_ring(src_hbm, buf, sems, groups, idx_now, idx_next, inline_next=False, priorities=(0, 1)):
    i = pl.program_id(0)
    last = pl.num_programs(0) - 1
    slot = i % 2
    total = sum(cnt for _, cnt, _ in groups)
    assert 2 * total * ROW_PITCH == buf.shape[0]

    def rows_at(row, pitch):
        return pl.ds(pl.multiple_of(row * pitch, 8), ROW_FOLD)

    def start(idx, s, first, k, r, priority):
        pltpu.make_async_copy(src_hbm.at[rows_at(idx(k, r), ROW_FOLD)],
                              buf.at[rows_at(s * total + first + r, ROW_PITCH)], sems.at[s]).start(priority=priority)

    def start_all(idx, s):
        for first, cnt, k in groups:
            per_trip = math.gcd(cnt, 8)

            def issue(j, c):
                for u in range(per_trip):
                    start(idx, s, first, k, per_trip * j + u, priorities[u % 2])
                return c
            lax.fori_loop(0, cnt // per_trip, issue, 0)

    def wait_slot(s):
        span = buf.at[pl.ds(pl.multiple_of(s * total * ROW_PITCH, 8), total * ROW_FOLD)]
        pltpu.make_async_copy(span, span, sems.at[s]).wait()

    @pl.when(i == 0)
    def _():
        start_all(idx_now, slot)

    def finish():
        if inline_next:
            @pl.when(i == last)
            def _():
                wait_slot(1 - slot)

    if inline_next:
        wait_slot(slot)
        for first, cnt, k in groups:
            for r in range(cnt):
                start(idx_next, 1 - slot, first, k, r, priorities[r % 2])
    else:
        @pl.when(i < last)
        def _():
            start_all(idx_next, 1 - slot)

        wait_slot(slot)
    return slot * total, finish


def _expert_kernel(be_ref, rt_ref, rtn_ref, h_hbm, w1_ref, w3_ref, w2_ref, o_ref, xbuf, sems, w1b, w3b, w2b):
    i = pl.program_id(0)
    blk = rt_ref.shape[1]

    @pl.when((i == 0) | (be_ref[i] != be_ref[jnp.maximum(i - 1, 0)]))
    def _():
        w1b[...] = w1_ref[...].astype(BF)
        w3b[...] = w3_ref[...].astype(BF)
        w2b[...] = w2_ref[...].astype(BF)

    base, finish = _row_gather_ring(h_hbm, xbuf, sems, ((0, blk, 0),), lambda k, r: rt_ref[0, r],
                                    lambda k, r: rtn_ref[0, r], inline_next=True, priorities=(1, 1))
    x = _load_folded(xbuf, base, blk).astype(BF)
    a = jnp.dot(x, w1b[...], preferred_element_type=F32)
    b = jnp.dot(x, w3b[...], preferred_element_type=F32)
    hid = a * _sigmoid(a) * b
    _store_folded(o_ref, jnp.dot(hid.astype(BF), w2b[...], preferred_element_type=F32))
    finish()


def _experts(h2, row_tok, blk_exp, w1, w3, w2):
    nblk, _, blk = row_tok.shape
    idx_spec = lambda d: pl.BlockSpec((None, 1, blk), lambda i, be: (jnp.minimum(i + d, nblk - 1), 0, 0),
                                      memory_space=pltpu.SMEM)
    grid_spec = pltpu.PrefetchScalarGridSpec(
        num_scalar_prefetch=1, grid=(nblk,),
        in_specs=[idx_spec(0), idx_spec(1),
                  pl.BlockSpec(memory_space=pl.ANY),
                  pl.BlockSpec((None, D_MODEL, D_EXP), lambda i, be: (be[i], 0, 0)),
                  pl.BlockSpec((None, D_MODEL, D_EXP), lambda i, be: (be[i], 0, 0)),
                  pl.BlockSpec((None, D_EXP, D_MODEL), lambda i, be: (be[i], 0, 0))],
        out_specs=pl.BlockSpec((blk * ROW_FOLD, LANES), lambda i, be: (i, 0)),
        scratch_shapes=[pltpu.VMEM((2 * blk * ROW_PITCH, LANES), F32), pltpu.SemaphoreType.DMA((2,)),
                        pltpu.VMEM((D_MODEL, D_EXP), BF), pltpu.VMEM((D_MODEL, D_EXP), BF),
                        pltpu.VMEM((D_EXP, D_MODEL), BF)])
    return pl.pallas_call(
        _expert_kernel,
        grid_spec=grid_spec,
        out_shape=jax.ShapeDtypeStruct((nblk * blk * ROW_FOLD, LANES), F32),
        compiler_params=_cparams("arbitrary"),
        name="moe_experts",
    )(blk_exp, row_tok, row_tok, h2, w1, w3, w2)


def _final_kernel(x_ref, g_ref, dest_ref, destn_ref, ys_hbm, wt_ref, nf_ref, o_ref, ybuf, sems):
    tm = x_ref.shape[0]
    groups = tuple((k * tm, tm, k) for k in range(TOP_K))
    base, _ = _row_gather_ring(ys_hbm, ybuf, sems, groups, lambda k, r: dest_ref[k, r], lambda k, r: destn_ref[k, r])
    wt = wt_ref[...]
    moe = wt[:, 0:1] * _load_folded(ybuf, base, tm) + wt[:, 1:2] * _load_folded(ybuf, base + tm, tm)
    o_ref[...] = _rms(x_ref[...] + g_ref[...] * moe, nf_ref[...])


def _final(x, gate, ys, dest, wts, norm_f, tm, rpb):
    m = x.shape[0]
    r = gate.shape[1]
    nt = m // tm
    rows = lambda tn: pl.BlockSpec((tm, tn), lambda i: (i, 0))
    idx_spec = lambda d: pl.BlockSpec((None, TOP_K, tm), lambda i: (jnp.minimum(i + d, nt - 1), 0, 0),
                                      memory_space=pltpu.SMEM)
    return pl.pallas_call(
        _final_kernel,
        grid=(nt,),
        in_specs=[rows(D_MODEL), pl.BlockSpec((None, r, D_MODEL), lambda i: ((i * tm) // rpb, 0, 0)),
                  idx_spec(0), idx_spec(1),
                  pl.BlockSpec(memory_space=pl.ANY),
                  rows(ROUTER_COLS), pl.BlockSpec((1, D_MODEL), lambda i: (0, 0))],
        out_specs=rows(D_MODEL),
        out_shape=jax.ShapeDtypeStruct((m, D_MODEL), F32),
        scratch_shapes=[pltpu.VMEM((2 * TOP_K * tm * ROW_PITCH, LANES), F32), pltpu.SemaphoreType.DMA((2,))],
        compiler_params=_cparams("arbitrary"),
        name="moe_combine_final_norm",
    )(x, gate, dest, dest, ys, wts, norm_f.reshape(1, D_MODEL))


def _moe_and_final(x1, g2, shift, scale, gate, w_r, b_r, exp_w1, exp_w3, exp_w2, norm_f, tm, rpb, blk):
    m = x1.shape[0]
    h2, eid, wts, rank, counts = _router(x1, g2, shift, scale, w_r, b_r, tm, rpb)
    counts = counts[0, :N_EXP].astype(I32)
    padded = (counts + blk - 1) // blk * blk
    pend = jnp.cumsum(padded)
    pstart = pend - padded
    n_blocks = -(-(m * TOP_K) // blk) + N_EXP
    starts = jnp.arange(n_blocks, dtype=I32)[:, None] * blk
    blk_exp = jnp.minimum(jnp.sum((pend[None, :] <= starts).astype(I32), axis=1), N_EXP - 1)
    e = eid[:, :TOP_K]
    dest = pstart[e] + rank[:, :TOP_K]
    tok = jnp.broadcast_to(jnp.arange(m, dtype=I32)[:, None], (m, TOP_K))
    row_tok = jnp.zeros((n_blocks * blk,), I32).at[dest.reshape(-1)].set(tok.reshape(-1))
    ys = _experts(h2, row_tok.reshape(n_blocks, 1, blk), blk_exp, exp_w1, exp_w3, exp_w2)
    dest_t = jnp.transpose(dest.reshape(m // tm, tm, TOP_K), (0, 2, 1))
    return _final(x1, gate, ys, dest_t, wts, norm_f, tm, rpb)


def _pack_in_proj(w_in):
    o = C_RIN
    w_r = w_in[:, :o]
    w_q = w_in[:, o:o + C_N + 3 * KV_COLS]
    o += C_N + 3 * KV_COLS
    w_gn = w_in[:, o:o + 3 * H_N].reshape(D_MODEL, N_KVG, 3 * HPG)
    w_gn = jnp.pad(w_gn, ((0, 0), (0, 0), (0, GN_GROUP_COLS - 3 * HPG))).reshape(D_MODEL, N_KVG * GN_GROUP_COLS)
    o += 3 * H_N
    w_gm = w_in[:, o:]
    return w_r.astype(BF), jnp.concatenate([w_q, w_gn], axis=1).astype(BF), w_gm.astype(BF)


def _prompt_bias_tables(rel_bias, t):
    tq, tk = ATT_TILE, ATT_TK
    i = np.arange(tq)[None, :]
    j = np.arange(tk)[:, None]
    dist = np.stack([tk * (o - (ATT_R - 1)) + i - j for o in range(N_WIN_OFFS)]).astype(np.int32)
    raw = _bias_table(jnp.asarray(dist.reshape(N_WIN_OFFS * tk, tq)), rel_bias, tk)
    raw = jnp.transpose(raw.reshape(H_N, N_WIN_OFFS, tk, tq), (1, 0, 2, 3))
    ok_w = jnp.asarray((dist >= 0) & (dist <= WINDOW))[:, None]
    ok_s = jnp.asarray(dist[:N_SEL_OFFS] >= 0)[:, None]
    masked = jnp.full((1, H_N * tk, tq), NEG_INF, F32)
    tab_w = jnp.concatenate([jnp.where(ok_w, raw, NEG_INF).reshape(N_WIN_OFFS, H_N * tk, tq), masked])
    tab_s = jnp.concatenate([jnp.where(ok_s, raw[:N_SEL_OFFS], NEG_INF).reshape(N_SEL_OFFS, H_N * tk, tq), masked])
    nc = (t - L_CMP) // STRIDE + 1
    ncp = nc + 1
    dc = (np.arange(t)[:, None] - (np.arange(ncp)[None, :] * STRIDE + L_CMP - 1)).astype(np.int32)
    return tab_s, tab_w, _bias_table(jnp.asarray(dc), rel_bias, tq)


def kernel(x_prompt, x_sample, c_prompt, c_sample, cache_cmp_kv, cache_sel_kv, state_win_kv, state_rwkv_shift,
           state_rwkv_wkv, page_table, rel_bias, norm_f, norm1, norm2, w_ada, b_ada, w_in, rwkv_mu, rwkv_w0, rwkv_w2,
           rwkv_a0, rwkv_a2, rwkv_g2, rwkv_kk, rwkv_ka, rwkv_rk, rwkv_ln_g, rwkv_ln_b, cmp_pos, cmp_w1, cmp_w2,
           w_o_rwkv, w_o_nsa, w_out, router_wg, router_bg, router_we, router_be, exp_w1, exp_w3, exp_w2):
    bp, t, _ = x_prompt.shape
    bs = x_sample.shape[0]
    mp = bp * t
    past = page_table.shape[1] * PAGE_SIZE

    nrow = -(-(bp + bs) // 8) * 8
    c_all = jnp.concatenate([c_prompt, c_sample, jnp.zeros((nrow - bp - bs, D_MODEL), F32)], axis=0)
    mod = _ada(c_all, w_ada[0], b_ada[0]).reshape(nrow, 6, D_MODEL)
    mod_p = [mod[:bp, i][:, None, :] for i in range(6)]
    mod_s = [mod[bp:bp + bs, i][None] for i in range(6)]

    w_r, w_n, w_gm = _pack_in_proj(w_in[0])
    rw = _rwkv_weights(rwkv_mu[0], rwkv_w0[0], rwkv_w2[0], rwkv_a0[0], rwkv_a2[0], rwkv_g2[0], rwkv_kk[0],
                       rwkv_ka[0], rwkv_rk[0], rwkv_ln_g[0], rwkv_ln_b[0])
    wbd = _cmp_weights(cmp_w1[0], N_KVG)
    cpos = _cmp_partial_rows(_cmp_pos_rows(cmp_pos[0]), wbd, 8)
    wo_r, wo_n, wo = w_o_rwkv[0].astype(BF), w_o_nsa[0].astype(BF), w_out[0].astype(BF)
    w_router = jnp.pad(jnp.concatenate([router_wg[0], router_we[0]], axis=1),
                       ((0, 0), (0, ROUTER_COLS - N_EGROUPS - N_EXP))).astype(BF)
    b_router = jnp.pad(jnp.concatenate([router_bg[0], router_be[0]]), (0, ROUTER_COLS - N_EGROUPS - N_EXP))[None]

    tm = 512
    xp = x_prompt.reshape(mp, D_MODEL)
    h = _norm_mod(xp, norm1[0], mod_p[0], mod_p[1], tm, t)
    p_r = _matmul(h, w_r, tm, C_RIN // 2)
    p_n = _matmul(h, w_n, tm, NP_COLS // 2)
    p_g = _matmul(h, w_gm, tm, 2048)
    o_r, shift_p, wkv_p = _rwkv_prompt(p_r.reshape(bp, t, C_RIN), rw)
    kvc = p_n[:, NP_KVC:NP_KVC + KV_COLS]
    kvs = p_n[:, NP_KVS:NP_KVS + KV_COLS]
    kvw = p_n[:, NP_KVW:NP_KVW + KV_COLS]
    nch = t // STRIDE
    c_part = _cmp_partial_rows(kvc.reshape(bp * nch, STRIDE * KV_COLS), wbd, nch)
    kv_cmp = _cmp_finish(c_part.reshape(bp, nch, -1), cpos, cmp_w2[0])
    tab_s, tab_w, bias_c = _prompt_bias_tables(rel_bias, t)
    o_cmp, sel = _cmp_attn_prompt(p_n, kv_cmp, bias_c, t)
    o_n = _swa_prompt(p_n, *_swa_operands(p_n, sel, bp, t), tab_s, tab_w, o_cmp, t)
    y = _merge(o_r.reshape(mp, C_R), o_n, wo_r, wo_n, p_g, tm)
    x1 = _proj_residual(y, wo, xp, mod_p[2], tm, t)
    y_prompt = _moe_and_final(x1, norm2[0], mod_p[3], mod_p[4], mod_p[5], w_router, b_router, exp_w1[0], exp_w3[0],
                              exp_w2[0], norm_f, tm, t, MOE_ROWS_PROMPT).reshape(bp, t, D_MODEL)
    kv_shape = (1, bp, t, 2, N_KVG, HD_N)
    wlen = min(WINDOW, t)
    win_p = kvw.reshape(bp, t, KV_COLS)[:, t - wlen:].reshape(1, bp, wlen, 2, N_KVG, HD_N)

    xs = x_sample.reshape(bs, D_MODEL)
    hs = _norm_mod(xs, norm1[0], mod_s[0], mod_s[1], bs, bs)
    ps_r = _matmul(hs, w_r, bs, C_RIN // 2)
    ps_n = _matmul(hs, w_n, bs, NP_COLS // 2)
    ps_g = _matmul(hs, w_gm, bs, 2048)
    os_r, wkv_s = _rwkv_step(ps_r, state_rwkv_shift[0], state_rwkv_wkv[0], rw)
    kvc_s = ps_n[:, NP_KVC:NP_KVC + KV_COLS]
    kvs_s = ps_n[:, NP_KVS:NP_KVS + KV_COLS]
    kvw_s = ps_n[:, NP_KVW:NP_KVW + KV_COLS]
    cs_part = _cmp_partial_paged(jnp.transpose(cache_cmp_kv[0], (0, 2, 3, 4, 1)), page_table,
                                 _cmp_weights(cmp_w1[0], CMP_PAIR))
    kv_cmp_s = _cmp_finish(cs_part, cpos, cmp_w2[0])
    q_s = ps_n[:, :C_N].reshape(bs, H_N, HD_N)
    o_cmp_s, picks = _sample_cmp(q_s, kv_cmp_s, rel_bias, past)
    idx = picks[:, :N_KVG, :N_TOP]
    bpp = PAGE_SIZE // L_SEL
    npb = past // L_SEL
    idc = jnp.minimum(idx, npb - 1)
    page = jnp.take_along_axis(page_table, (idc // bpp).reshape(bs, -1), axis=1).reshape(bs * N_KVG, N_TOP)
    cache_t = jnp.transpose(cache_sel_kv[0], (0, 2, 3, 4, 1))
    blocks = _gather_sel_pages(cache_t, page).reshape(bs, N_KVG, N_TOP, 2, HD_N, PAGE_SIZE)
    win_buf = state_win_kv[0].reshape(bs, -1, KV_COLS)
    gates_s = ps_n[:, NP_GN:].reshape(bs, N_KVG, GN_GROUP_COLS)[:, :, :3 * HPG].reshape(bs, H_N, 3)
    new_kv = jnp.stack([kvs_s, kvw_s], axis=1)
    os_n = _sample_swa(idx, q_s, blocks, win_buf, new_kv, rel_bias, gates_s, o_cmp_s, past)
    ys = _merge(os_r, os_n.reshape(bs, C_N).astype(BF), wo_r, wo_n, ps_g, bs)
    xs1 = _proj_residual(ys, wo, xs, mod_s[2], bs, bs)
    y_sample = _moe_and_final(xs1, norm2[0], mod_s[3], mod_s[4], mod_s[5], w_router, b_router, exp_w1[0], exp_w3[0],
                              exp_w2[0], norm_f, bs, bs, MOE_ROWS_SAMPLE).reshape(bs, 1, D_MODEL)
    kv1 = (1, bs, 1, 2, N_KVG, HD_N)
    wbuf = win_buf.shape[1]
    win_s = jnp.concatenate([win_buf, kvw_s[:, None, :]], axis=1)[:, -wbuf:].reshape(1, bs, wbuf, 2, N_KVG, HD_N)

    return (y_prompt, y_sample,
            kvc.reshape(kv_shape), kvc_s.reshape(kv1),
            kvs.reshape(kv_shape), kvs_s.reshape(kv1),
            win_p, win_s,
            shift_p.reshape(1, bp, C_RIN), ps_r.reshape(1, bs, C_RIN),
            wkv_p[None], wkv_s[None])
```

```python
import functools
import math

import numpy as np
import jax
import jax.numpy as jnp
from jax import lax
from jax.experimental import pallas as pl
from jax.experimental.pallas import tpu as pltpu

D_MODEL = 2048
PAGE_SIZE = 128
H_R, HD_R = 16, 64
C_R = H_R * HD_R
LORA_W, LORA_A, LORA_G = 64, 64, 128
C_RIN = 3 * C_R + LORA_W + LORA_A + LORA_G
LN_X_EPS = 64e-5
H_N, HD_N, N_KVG = 16, 64, 4
HPG = H_N // N_KVG
C_N = H_N * HD_N
KV_COLS = 2 * N_KVG * HD_N
L_CMP, STRIDE, CMP_HID = 32, 16, 64
L_SEL, N_TOP, WINDOW = 64, 16, 512
N_BUCKETS, MAX_DIST = 32, 128
N_EGROUPS, EXP_PER_GROUP = 4, 8
N_EXP = N_EGROUPS * EXP_PER_GROUP
TOP_K, D_EXP = 2, 512
RMS_EPS = 1e-6
NEG_INF = -1e30
FORCE = 1e9

BF = jnp.bfloat16
F32 = jnp.float32
I32 = jnp.int32

VMEM_LIMIT_BYTES = 56 * 1024 * 1024
LANES = 128
RW_CHUNK = 32
RW_SEQS_PER_STEP = 4
ATT_TILE = 256
GN_GROUP_COLS = 128
NP_Q, NP_KVC, NP_KVS, NP_KVW, NP_GN = 0, C_N, C_N + KV_COLS, C_N + 2 * KV_COLS, C_N + 3 * KV_COLS
NP_COLS = NP_GN + N_KVG * GN_GROUP_COLS


def _cparams(*sem):
    return pltpu.CompilerParams(dimension_semantics=sem, vmem_limit_bytes=VMEM_LIMIT_BYTES)


def _dot(a, b):
    return jnp.dot(a.astype(BF), b.astype(BF), preferred_element_type=F32)


def _dot_nt(a, b):
    return lax.dot_general(a.astype(BF), b.astype(BF), (((1,), (1,)), ((), ())), preferred_element_type=F32)


def _dot_tn(a, b):
    return lax.dot_general(a.astype(BF), b.astype(BF), (((0,), (0,)), ((), ())), preferred_element_type=F32)


def _softplus(x):
    return jnp.maximum(x, 0.0) + jnp.log1p(jnp.exp(-jnp.abs(x)))


def _sigmoid(x):
    return 1.0 / (1.0 + jnp.exp(-x))


def _gelu_tanh(x):
    return 0.5 * x * (1.0 + jnp.tanh(math.sqrt(2.0 / math.pi) * (x + 0.044715 * x * x * x)))


def _t5_bucket(dist):
    n = jnp.maximum(dist, 0)
    max_exact = N_BUCKETS // 2
    nf = jnp.maximum(n, 1).astype(F32)
    large = max_exact + (jnp.log(nf / max_exact) / math.log(MAX_DIST / max_exact)
                         * (N_BUCKETS - max_exact)).astype(I32)
    large = jnp.minimum(large, N_BUCKETS - 1)
    return jnp.where(n < max_exact, n, large)


def _bias_rows(dist, rbt):
    bucket = _t5_bucket(dist)
    out = jnp.zeros((rbt.shape[0], dist.shape[1]), F32)
    for b in range(N_BUCKETS):
        out = jnp.where(bucket == b, rbt[:, b:b + 1], out)
    return out


def _ada_kernel(c_ref, w_ref, b_ref, o_ref):
    o_ref[...] = _dot(c_ref[...], w_ref[...]) + b_ref[...]


def _ada(c, w_ada, b_ada):
    r = c.shape[0]
    n = w_ada.shape[1]
    tn = 1024
    return pl.pallas_call(
        _ada_kernel,
        grid=(n // tn,),
        in_specs=[pl.BlockSpec((r, D_MODEL), lambda j: (0, 0)),
                  pl.BlockSpec((D_MODEL, tn), lambda j: (0, j)),
                  pl.BlockSpec((1, tn), lambda j: (0, j))],
        out_specs=pl.BlockSpec((r, tn), lambda j: (0, j)),
        out_shape=jax.ShapeDtypeStruct((r, n), F32),
        compiler_params=_cparams("arbitrary"),
        name="ada_mod",
    )(c, w_ada, b_ada.reshape(1, n))


def _rms(x, g):
    return x * lax.rsqrt(jnp.mean(x * x, axis=-1, keepdims=True) + RMS_EPS) * g


def _norm_mod_kernel(x_ref, g_ref, sh_ref, sc_ref, o_ref):
    o_ref[...] = (_rms(x_ref[...], g_ref[...]) * (1.0 + sc_ref[...]) + sh_ref[...]).astype(o_ref.dtype)


def _row_specs(m, tm, rpb):
    del m
    return (lambda tn: pl.BlockSpec((tm, tn), lambda i, j: (i, j)),
            lambda r, tn: pl.BlockSpec((None, r, tn), lambda i, j: ((i * tm) // rpb, 0, j)))


def _norm_mod(x, g, shift, scale, tm, rpb):
    m = x.shape[0]
    r = shift.shape[1]
    rows, mods = _row_specs(m, tm, rpb)
    return pl.pallas_call(
        _norm_mod_kernel,
        grid=(m // tm, 1),
        in_specs=[rows(D_MODEL), pl.BlockSpec((1, D_MODEL), lambda i, j: (0, 0)), mods(r, D_MODEL), mods(r, D_MODEL)],
        out_specs=rows(D_MODEL),
        out_shape=jax.ShapeDtypeStruct((m, D_MODEL), BF),
        compiler_params=_cparams("arbitrary", "arbitrary"),
        name="norm_mod",
    )(x, g.reshape(1, D_MODEL), shift, scale)


def _mm_kernel(a_ref, w_ref, o_ref):
    o_ref[...] = jnp.dot(a_ref[...], w_ref[...], preferred_element_type=F32).astype(o_ref.dtype)


def _matmul(a, w, tm, tn, out_dtype=F32):
    m, k = a.shape
    n = w.shape[1]
    return pl.pallas_call(
        _mm_kernel,
        grid=(m // tm, n // tn),
        in_specs=[pl.BlockSpec((tm, k), lambda i, j: (i, 0)), pl.BlockSpec((k, tn), lambda i, j: (0, j))],
        out_specs=pl.BlockSpec((tm, tn), lambda i, j: (i, j)),
        out_shape=jax.ShapeDtypeStruct((m, n), out_dtype),
        compiler_params=_cparams("arbitrary", "arbitrary"),
        name="matmul",
    )(a, w)


def _merge_kernel(or_ref, on_ref, wr_ref, wn_ref, g0_ref, g1_ref, o_ref):
    yr = jnp.dot(or_ref[...], wr_ref[...], preferred_element_type=F32)
    yn = jnp.dot(on_ref[...], wn_ref[...], preferred_element_type=F32)
    o_ref[...] = (_sigmoid(g0_ref[...]) * yr + _sigmoid(g1_ref[...]) * yn).astype(o_ref.dtype)


def _merge(o_r, o_n, w_r, w_n, p_g, tm):
    m = o_r.shape[0]
    tn = 1024
    nb = D_MODEL // tn
    return pl.pallas_call(
        _merge_kernel,
        grid=(m // tm, nb),
        in_specs=[pl.BlockSpec((tm, C_R), lambda i, j: (i, 0)), pl.BlockSpec((tm, C_N), lambda i, j: (i, 0)),
                  pl.BlockSpec((C_R, tn), lambda i, j: (0, j)), pl.BlockSpec((C_N, tn), lambda i, j: (0, j)),
                  pl.BlockSpec((tm, tn), lambda i, j: (i, j)), pl.BlockSpec((tm, tn), lambda i, j: (i, j + nb))],
        out_specs=pl.BlockSpec((tm, tn), lambda i, j: (i, j)),
        out_shape=jax.ShapeDtypeStruct((m, D_MODEL), BF),
        compiler_params=_cparams("arbitrary", "arbitrary"),
        name="merge_branches",
    )(o_r, o_n, w_r, w_n, p_g, p_g)


def _proj_res_kernel(y_ref, w_ref, x_ref, g_ref, o_ref):
    o_ref[...] = x_ref[...] + g_ref[...] * jnp.dot(y_ref[...], w_ref[...], preferred_element_type=F32)


def _proj_residual(y, w, x, gate, tm, rpb):
    m = y.shape[0]
    tn = 1024
    r = gate.shape[1]
    rows, mods = _row_specs(m, tm, rpb)
    return pl.pallas_call(
        _proj_res_kernel,
        grid=(m // tm, D_MODEL // tn),
        in_specs=[pl.BlockSpec((tm, D_MODEL), lambda i, j: (i, 0)), pl.BlockSpec((D_MODEL, tn), lambda i, j: (0, j)),
                  rows(tn), mods(r, tn)],
        out_specs=rows(tn),
        out_shape=jax.ShapeDtypeStruct((m, D_MODEL), F32),
        compiler_params=_cparams("arbitrary", "arbitrary"),
        name="out_proj_residual",
    )(y, w, x, gate)


def _rwkv_features(p, p_prev, mu, w0, w2, a0, a2, g2, k_k, k_a):
    xm = p + (p_prev - p) * mu
    r = xm[:, :C_R]
    k = xm[:, C_R:2 * C_R]
    v = xm[:, 2 * C_R:3 * C_R]
    o = 3 * C_R
    wd = xm[:, o:o + LORA_W]
    ad = xm[:, o + LORA_W:o + LORA_W + LORA_A]
    gd = xm[:, o + LORA_W + LORA_A:]
    w_log = -_softplus(-(w0 + _dot(jnp.tanh(wd), w2))) - 0.5
    lw = -jnp.exp(w_log)
    a = _sigmoid(a0 + _dot(ad, a2))
    g = _dot(_sigmoid(gd), g2)
    kk = k * k_k
    k = k * (1.0 + (a - 1.0) * k_a)
    return r, k, v, lw, a, g, kk


def _head_sums(x, ones2):
    nt = x.shape[1] // LANES
    xs = jnp.concatenate([x[:, j * LANES:(j + 1) * LANES] for j in range(nt)], axis=0)
    hi = xs.astype(BF)
    lo = (xs - hi.astype(F32)).astype(BF)
    s = jnp.dot(hi, ones2, preferred_element_type=F32) + jnp.dot(lo, ones2, preferred_element_type=F32)
    r = x.shape[0]
    return jnp.concatenate([s[j * r:(j + 1) * r] for j in range(nt)], axis=1)


def _rwkv_chunk_kernel(pr_ref, mu_ref, w0_ref, w2_ref, a0_ref, a2_ref, g2_ref, kk_ref, ka_ref, rk_ref,
                       lng_ref, lnb_ref, ones_ref, o_ref, shift_ref, state_ref, y_ref):
    c = pl.program_id(1)

    @pl.when(c == 0)
    def _():
        shift_ref[...] = jnp.zeros_like(shift_ref)
        state_ref[...] = jnp.zeros_like(state_ref)

    for bb in range(pr_ref.shape[0]):
        _rwkv_chunk_body(pr_ref.at[bb], mu_ref, w0_ref, w2_ref, a0_ref, a2_ref, g2_ref, kk_ref, ka_ref, rk_ref,
                         lng_ref, lnb_ref, ones_ref, o_ref.at[bb], shift_ref.at[bb], state_ref.at[bb], y_ref.at[bb])


def _rwkv_chunk_body(pr_ref, mu_ref, w0_ref, w2_ref, a0_ref, a2_ref, g2_ref, kk_ref, ka_ref, rk_ref,
                     lng_ref, lnb_ref, ones_ref, o_ref, shift_ref, state_ref, y_ref):
    C = RW_CHUNK
    p = pr_ref[...]
    row = lax.broadcasted_iota(I32, (C, 1), 0)
    p_prev = jnp.where(row == 0, shift_ref[...], pltpu.roll(p, 1, axis=0))
    shift_ref[...] = p[C - 1:C, :]
    r, k, v, lw, a, g, kk_all = _rwkv_features(p, p_prev, mu_ref[...], w0_ref[...], w2_ref[...], a0_ref[...],
                                               a2_ref[...], g2_ref[...], kk_ref[...], ka_ref[...])
    cl = lw
    s = 1
    while s < C:
        cl = cl + jnp.where(row >= s, pltpu.roll(cl, s, axis=0), 0.0)
        s *= 2
    ti = lax.broadcasted_iota(I32, (C, C), 0)
    si = lax.broadcasted_iota(I32, (C, C), 1)
    strict = ti > si
    incl = ti >= si
    eye = (ti == si).astype(F32)
    heads = range(H_R)
    sls = [slice(h * HD_R, (h + 1) * HD_R) for h in heads]
    ones2 = ones_ref[...]
    kk_n = kk_all / jnp.maximum(jnp.sqrt(_head_sums(kk_all * kk_all, ones2)), 1e-12)
    b_all = kk_n * a
    cl_end = cl[C - 1:C, :]
    e_neg = jnp.exp(-cl)
    e_end = jnp.exp(cl_end - cl)
    g_end_all = jnp.exp(cl_end)
    per_head = lambda z: [z[:, sl] for sl in sls]
    kkt_all = (kk_n * jnp.exp(cl - lw)).astype(BF)
    rt_all = (r * jnp.exp(cl)).astype(BF)
    kr = per_head(jnp.concatenate([kkt_all, rt_all], axis=0))
    kh = per_head((k * e_neg).astype(BF))
    bh = per_head((b_all * e_neg).astype(BF))
    kbb = per_head(jnp.concatenate([(k * e_end).astype(BF), (b_all * e_end).astype(BF)], axis=0))
    vb = per_head(v.astype(BF))
    g_end = per_head(g_end_all)
    ak = [_dot_nt(kr[h], kh[h]) for h in heads]
    ab = [_dot_nt(kr[h], bh[h]) for h in heads]
    lg = [jnp.concatenate([jnp.where(strict, ak[h][:C], 0.0), jnp.where(incl, ak[h][C:], 0.0)], axis=0).astype(BF)
          for h in heads]
    nil = [jnp.where(strict, -ab[h][:C], 0.0) for h in heads]
    grb = [jnp.where(incl, ab[h][C:], 0.0).astype(BF) for h in heads]
    tinv = [eye + n for n in nil]
    m = 2
    while m < C:
        nil = [_dot(n, n) for n in nil]
        tinv = [t + _dot(t, n) for t, n in zip(tinv, nil)]
        m *= 2
    s0 = [state_ref[h] for h in heads]
    xy = [_dot_nt(kr[h], s0[h]) + _dot(lg[h], vb[h]) for h in heads]
    u = [_dot(tinv[h], xy[h][:C]).astype(BF) for h in heads]
    y = [xy[h][C:] - _dot(grb[h], u[h]) for h in heads]
    for h in heads:
        vu = jnp.concatenate([vb[h], -u[h]], axis=0)
        state_ref[h] = s0[h] * g_end[h] + _dot_tn(vu, kbb[h])
    for h, sl in enumerate(sls):
        y_ref[:, sl] = y[h]
    y_all = y_ref[...]
    yc = y_all - _head_sums(y_all, ones2) * (1.0 / HD_R)
    var = _head_sums(yc * yc, ones2) * (1.0 / HD_R)
    yn = yc * lax.rsqrt(var + LN_X_EPS) * lng_ref[...] + lnb_ref[...]
    bonus = _head_sums(r * k * rk_ref[...], ones2) * v
    o_ref[...] = ((yn + bonus) * g).astype(o_ref.dtype)


def _rwkv_weights(mu, w0, w2, a0, a2, g2, k_k, k_a, r_k, ln_g, ln_b):
    row = lambda z: z.reshape(1, -1).astype(F32)
    return (row(mu), row(w0), w2.astype(BF), row(a0), a2.astype(BF), g2.astype(BF), row(k_k), row(k_a), row(r_k),
            row(ln_g), row(ln_b))


_RWKV_W_SHAPES = ((1, C_RIN), (1, C_R), (LORA_W, C_R), (1, C_R), (LORA_A, C_R), (LORA_G, C_R), (1, C_R), (1, C_R),
                  (1, C_R), (1, C_R), (1, C_R))


def _rwkv_prompt(pr, rw):
    b, t, _ = pr.shape
    C = RW_CHUNK
    full = lambda shp: pl.BlockSpec(shp, lambda i, j: (0,) * len(shp))
    lane_head = np.arange(LANES) // HD_R
    ones2 = jnp.asarray(lane_head[:, None] == lane_head[None, :], BF)
    nb = math.gcd(b, RW_SEQS_PER_STEP)
    return pl.pallas_call(
        _rwkv_chunk_kernel,
        grid=(b // nb, t // C),
        in_specs=[pl.BlockSpec((nb, C, C_RIN), lambda i, j: (i, j, 0))] + [full(s) for s in _RWKV_W_SHAPES]
        + [full((LANES, LANES))],
        out_specs=[pl.BlockSpec((nb, C, C_R), lambda i, j: (i, j, 0)),
                   pl.BlockSpec((nb, 1, C_RIN), lambda i, j: (i, 0, 0)),
                   pl.BlockSpec((nb, H_R, HD_R, HD_R), lambda i, j: (i, 0, 0, 0))],
        out_shape=[jax.ShapeDtypeStruct((b, t, C_R), BF),
                   jax.ShapeDtypeStruct((b, 1, C_RIN), F32),
                   jax.ShapeDtypeStruct((b, H_R, HD_R, HD_R), F32)],
        scratch_shapes=[pltpu.VMEM((nb, C, C_R), F32)],
        compiler_params=_cparams("arbitrary", "arbitrary"),
        name="rwkv_chunk",
    )(pr, *rw, ones2)


def _rwkv_step_kernel(pr_ref, prev_ref, s0_ref, mu_ref, w0_ref, w2_ref, a0_ref, a2_ref, g2_ref, kk_ref, ka_ref,
                      rk_ref, lng_ref, lnb_ref, o_ref, state_ref):
    nb = pr_ref.shape[0]
    r, k, v, lw, a, g, kk_all = _rwkv_features(pr_ref[...], prev_ref[...], mu_ref[...], w0_ref[...], w2_ref[...],
                                               a0_ref[...], a2_ref[...], g2_ref[...], kk_ref[...], ka_ref[...])
    decay = jnp.exp(lw)
    ii = lax.broadcasted_iota(I32, (HD_R, HD_R), 0)
    jj = lax.broadcasted_iota(I32, (HD_R, HD_R), 1)
    eye = ii == jj
    col = lambda z: jnp.sum(jnp.where(eye, z, 0.0), axis=1, keepdims=True)
    heads = range(H_R)
    sls = [slice(h * HD_R, (h + 1) * HD_R) for h in heads]
    for bi in range(nb):
        rows = lambda z: [z[bi:bi + 1, sl] for sl in sls]
        r_h, k_h, v_h, a_h, w_h, g_h, kk_h = rows(r), rows(k), rows(v), rows(a), rows(decay), rows(g), rows(kk_all)
        nrm = [jnp.maximum(jnp.sqrt(jnp.sum(z * z, axis=-1, keepdims=True)), 1e-12) for z in kk_h]
        kk_h = [kk_h[h] / nrm[h] for h in heads]
        s0 = [s0_ref[bi, h] for h in heads]
        sa = [jnp.sum(s0[h] * (-kk_h[h]), axis=1, keepdims=True) for h in heads]
        v_col = [col(z) for z in v_h]
        s1 = [s0[h] * w_h[h] + sa[h] * (kk_h[h] * a_h[h]) + v_col[h] * k_h[h] for h in heads]
        for h in heads:
            state_ref[bi, h] = s1[h]
        y_col = [jnp.sum(s1[h] * r_h[h], axis=1, keepdims=True) for h in heads]
        y = [jnp.sum(jnp.where(eye, z, 0.0), axis=0, keepdims=True) for z in y_col]
        yc = [z - jnp.mean(z, axis=-1, keepdims=True) for z in y]
        var = [jnp.mean(z * z, axis=-1, keepdims=True) for z in yc]
        bonus = [jnp.sum(r_h[h] * k_h[h] * rk_ref[:, sls[h]], axis=-1, keepdims=True) * v_h[h] for h in heads]
        for h, sl in enumerate(sls):
            yn = yc[h] * lax.rsqrt(var[h] + LN_X_EPS) * lng_ref[:, sl] + lnb_ref[:, sl]
            o_ref[bi:bi + 1, sl] = ((yn + bonus[h]) * g_h[h]).astype(o_ref.dtype)


def _rwkv_step(pr, prev, s0, rw):
    b = pr.shape[0]
    return pl.pallas_call(
        _rwkv_step_kernel,
        out_shape=[jax.ShapeDtypeStruct((b, C_R), BF), jax.ShapeDtypeStruct((b, H_R, HD_R, HD_R), F32)],
        compiler_params=pltpu.CompilerParams(vmem_limit_bytes=VMEM_LIMIT_BYTES),
        name="rwkv_step",
    )(pr, prev, s0, *rw)


def _cmp_partial_kernel(*refs):
    x_refs, w_ref, o_ref = refs[:-2], refs[-2], refs[-1]
    x = x_refs[0][...] if len(x_refs) == 1 else jnp.concatenate([r[...] for r in x_refs], axis=0)
    half = N_KVG * HD_N
    for s in range(2):
        acc = jnp.zeros((x.shape[0], N_KVG * 2 * CMP_HID), F32)
        for p in range(STRIDE):
            o = p * KV_COLS + s * half
            acc = acc + _dot(x[:, o:o + half], w_ref[p, s])
        o_ref[:, s * N_KVG * 2 * CMP_HID:(s + 1) * N_KVG * 2 * CMP_HID] = acc


CMP_PAIR = 2


def _cmp_partial_paged_kernel(pt_ref, *refs):
    del pt_ref
    x_refs, perm_ref, w_ref, o_ref = refs[:-3], refs[-3], refs[-2], refs[-1]
    cpp = PAGE_SIZE // STRIDE
    width = CMP_PAIR * 2 * CMP_HID
    perm = perm_ref[...]
    for s in range(2):
        for gp in range(N_KVG // CMP_PAIR):
            rows = []
            for x_ref in x_refs:
                tile = jnp.concatenate([x_ref[s, CMP_PAIR * gp + j] for j in range(CMP_PAIR)], axis=0)
                rows.append(_dot_nt(perm, tile))
            acc = jnp.zeros((len(x_refs) * cpp, width), F32)
            for p in range(STRIDE):
                lhs = jnp.concatenate([r[p * cpp:(p + 1) * cpp] for r in rows], axis=0)
                acc = acc + _dot(lhs, w_ref[p, s])
            o = (s * (N_KVG // CMP_PAIR) + gp) * width
            o_ref[:, o:o + width] = acc


def _cmp_weights(cmp_w1, groups):
    w1r = cmp_w1.reshape(2, 2, STRIDE, HD_N, CMP_HID)
    w = jnp.transpose(w1r, (2, 0, 3, 1, 4))
    w = w.reshape(STRIDE, 2, 1, HD_N, 1, 2 * CMP_HID)
    eye = jnp.eye(groups, dtype=w.dtype).reshape(1, 1, groups, 1, groups, 1)
    wbd = eye * w
    return wbd.reshape(STRIDE, 2, groups * HD_N, groups * 2 * CMP_HID).astype(BF)


def _cmp_partial_rows(x, wbd, tr):
    r = x.shape[0]
    n = 2 * N_KVG * 2 * CMP_HID
    return pl.pallas_call(
        _cmp_partial_kernel,
        grid=(r // tr,),
        in_specs=[pl.BlockSpec((tr, STRIDE * KV_COLS), lambda i: (i, 0)),
                  pl.BlockSpec(wbd.shape, lambda i: (0, 0, 0, 0))],
        out_specs=pl.BlockSpec((tr, n), lambda i: (i, 0)),
        out_shape=jax.ShapeDtypeStruct((r, n), F32),
        compiler_params=_cparams("arbitrary"),
        name="cmp_partial",
    )(x, wbd)


PAGES_PER_STEP = 16


def _cmp_partial_paged(cache_t, page_table, wpair):
    b, npg = page_table.shape
    cpp = PAGE_SIZE // STRIDE
    n = 2 * N_KVG * 2 * CMP_HID
    steps = npg // PAGES_PER_STEP
    perm = np.zeros((PAGE_SIZE, PAGE_SIZE), np.float32)
    tok = np.arange(PAGE_SIZE)
    perm[(tok % STRIDE) * cpp + tok // STRIDE, tok] = 1.0

    def page_spec(kpg):
        return pl.BlockSpec((None, 2, N_KVG, HD_N, PAGE_SIZE),
                            lambda i, j, pt: (pt[i, j * PAGES_PER_STEP + kpg], 0, 0, 0, 0))

    grid_spec = pltpu.PrefetchScalarGridSpec(
        num_scalar_prefetch=1,
        grid=(b, steps),
        in_specs=[page_spec(kpg) for kpg in range(PAGES_PER_STEP)]
        + [pl.BlockSpec((PAGE_SIZE, PAGE_SIZE), lambda i, j, pt: (0, 0)),
           pl.BlockSpec(wpair.shape, lambda i, j, pt: (0, 0, 0, 0))],
        out_specs=pl.BlockSpec((None, PAGES_PER_STEP * cpp, n), lambda i, j, pt: (i, j, 0)),
    )
    return pl.pallas_call(
        _cmp_partial_paged_kernel,
        grid_spec=grid_spec,
        out_shape=jax.ShapeDtypeStruct((b, npg * cpp, n), F32),
        compiler_params=_cparams("arbitrary", "arbitrary"),
        name="cmp_partial_paged",
    )(page_table, *([cache_t] * PAGES_PER_STEP), jnp.asarray(perm, BF), wpair)


def _cmp_finish_kernel(c_ref, cpos_ref, w2_ref, o_ref):
    c = c_ref[...]
    nrow = c.shape[0]
    c_next = pltpu.roll(c, nrow - 1, axis=0)
    for s in range(2):
        for gi in range(N_KVG):
            o = (s * N_KVG + gi) * 2 * CMP_HID
            hid = (c[:, o:o + CMP_HID] + cpos_ref[0:1, o:o + CMP_HID]
                   + c_next[:, o + CMP_HID:o + 2 * CMP_HID] + cpos_ref[1:2, o + CMP_HID:o + 2 * CMP_HID])
            oo = (s * N_KVG + gi) * HD_N
            o_ref[:, oo:oo + HD_N] = _dot(_gelu_tanh(hid), w2_ref[s])


def _cmp_finish(c, cpos, w2):
    b, nch, n = c.shape
    return pl.pallas_call(
        _cmp_finish_kernel,
        grid=(b,),
        in_specs=[pl.BlockSpec((None, nch, n), lambda i: (i, 0, 0)), pl.BlockSpec(cpos.shape, lambda i: (0, 0)),
                  pl.BlockSpec(w2.shape, lambda i: (0, 0, 0))],
        out_specs=pl.BlockSpec((None, nch, KV_COLS), lambda i: (i, 0, 0)),
        out_shape=jax.ShapeDtypeStruct((b, nch, KV_COLS), F32),
        compiler_params=_cparams("arbitrary"),
        name="cmp_finish",
    )(c, cpos, w2.astype(BF))


def _cmp_pos_rows(cmp_pos):
    pos = cmp_pos.reshape(2, STRIDE, 1, 1, HD_N)
    rows = jnp.broadcast_to(pos, (2, STRIDE, 2, N_KVG, HD_N)).reshape(2, STRIDE * KV_COLS)
    return jnp.concatenate([rows, jnp.zeros((6, STRIDE * KV_COLS), F32)], axis=0)


def _bias_table_kernel(dist_ref, rb_ref, o_ref):
    bucket = _t5_bucket(dist_ref[...])
    for h in range(H_N):
        out = jnp.zeros(bucket.shape, F32)
        for b in range(N_BUCKETS):
            out = jnp.where(bucket == b, rb_ref[b, h], out)
        o_ref[h] = out


def _bias_table(dist, rel_bias, tr):
    r, n = dist.shape
    return pl.pallas_call(
        _bias_table_kernel,
        grid=(r // tr,),
        in_specs=[pl.BlockSpec((tr, n), lambda i: (i, 0)),
                  pl.BlockSpec(memory_space=pltpu.SMEM)],
        out_specs=pl.BlockSpec((H_N, tr, n), lambda i: (0, i, 0)),
        out_shape=jax.ShapeDtypeStruct((H_N, r, n), F32),
        compiler_params=_cparams("arbitrary"),
        name="bias_table",
    )(dist, rel_bias)


def _softmax_rows(logits, valid):
    lm = jnp.where(valid, logits, NEG_INF)
    e = jnp.exp(lm - jnp.max(lm, axis=-1, keepdims=True))
    return e / jnp.sum(e, axis=-1, keepdims=True)


def _cmp_attn_kernel(q_ref, kv_ref, bias_ref, ovt_ref, o_ref, sel_ref):
    tq = q_ref.shape[0]
    ncp = kv_ref.shape[0]
    nsb = ovt_ref.shape[0]
    q0 = pl.program_id(1) * tq
    qpos = q0 + lax.broadcasted_iota(I32, (tq, 1), 0)
    cend = lax.broadcasted_iota(I32, (1, ncp), 1) * STRIDE + (L_CMP - 1)
    valid = (qpos >= cend) & (lax.broadcasted_iota(I32, (1, ncp), 1) < ncp - 1)
    validf = valid.astype(F32)
    q = q_ref[...] * (HD_N ** -0.5)
    blk = lax.broadcasted_iota(I32, (nsb, tq), 0)
    cur = (q0 + lax.broadcasted_iota(I32, (1, tq), 1)) // L_SEL
    forced = (blk == 0) | (blk == cur) | (blk == cur - 1)
    future = blk > cur
    for gi in range(N_KVG):
        kc = kv_ref[:, gi * HD_N:(gi + 1) * HD_N]
        vc = kv_ref[:, (N_KVG + gi) * HD_N:(N_KVG + gi + 1) * HD_N]
        pcs = jnp.zeros((tq, ncp), F32)
        for hl in range(HPG):
            h = gi * HPG + hl
            sl = slice(h * HD_N, (h + 1) * HD_N)
            pc = _softmax_rows(_dot_nt(q[:, sl], kc) + bias_ref[h], valid) * validf
            pcs = pcs + pc
            o_ref[:, sl] = _dot(pc, vc)
        imp = _dot_nt(ovt_ref[...], pcs)
        score = jnp.where(forced, FORCE, jnp.where(future, -FORCE, imp))
        rank = jnp.zeros((nsb, tq), F32)
        for i in range(nsb):
            si = score[i:i + 1, :]
            rank = rank + ((si > score) | ((si == score) & (i < blk))).astype(F32)
        sel_ref[gi] = (rank < N_TOP).astype(sel_ref.dtype)


def _cmp_sel_overlap_t(nc, ncp, nsb):
    s = np.arange(nc)[None, :] * STRIDE
    j = np.arange(nsb)[:, None] * L_SEL
    ov = np.clip(np.minimum(s + L_CMP, j + L_SEL) - np.maximum(s, j), 0, None) / L_CMP
    return np.pad(ov, ((0, 0), (0, ncp - nc))).astype(np.float32)


def _cmp_attn_prompt(p_n, kv_cmp, bias_c, t):
    b = kv_cmp.shape[0]
    ncp = kv_cmp.shape[1]
    nsb = t // L_SEL
    tq = ATT_TILE
    nqt = t // tq
    ovt = jnp.asarray(_cmp_sel_overlap_t(ncp - 1, ncp, nsb), BF)
    return pl.pallas_call(
        _cmp_attn_kernel,
        grid=(b, nqt),
        in_specs=[pl.BlockSpec((tq, C_N), lambda i, j: (i * nqt + j, 0)),
                  pl.BlockSpec((None, ncp, KV_COLS), lambda i, j: (i, 0, 0)),
                  pl.BlockSpec((H_N, tq, ncp), lambda i, j: (0, j, 0)),
                  pl.BlockSpec((nsb, ncp), lambda i, j: (0, 0))],
        out_specs=[pl.BlockSpec((tq, C_N), lambda i, j: (i * nqt + j, 0)),
                   pl.BlockSpec((None, N_KVG, nsb, tq), lambda i, j: (i, 0, 0, j))],
        out_shape=[jax.ShapeDtypeStruct((b * t, C_N), F32), jax.ShapeDtypeStruct((b, N_KVG, nsb, t), BF)],
        compiler_params=_cparams("arbitrary", "arbitrary"),
        name="nsa_cmp_select",
    )(p_n, kv_cmp, bias_c, ovt)


ATT_TK = 128
ATT_R = ATT_TILE // ATT_TK
N_SEL_OFFS = ATT_R + (MAX_DIST + ATT_TK - 1) // ATT_TK + 1
N_WIN_OFFS = ATT_R + WINDOW // ATT_TK
QA_COLS = HD_N + 32
SEL_STEP_TILES = 4
WIN_STEP_TILES = 2


def _swa_kernel(qa_ref, ks_ref, vs_ref, kw_ref, vw_ref, tabs_ref, tabw_ref, gn_ref, oc_ref, o_ref):
    tq = qa_ref.shape[1]
    qt = pl.program_id(2)
    top = ATT_R * qt + ATT_R - 1

    heads = range(HPG)

    def attend(k_ref, v_ref, tab_ref, lo, n_off, step_tiles):
        tiles = range(step_tiles)

        def body(kp, carry):
            ki = [lo + step_tiles * kp + j for j in tiles]
            kt = [k_ref[jnp.minimum(i, top)] for i in ki]
            vt = [v_ref[jnp.minimum(i, top)] for i in ki]
            off = [jnp.where(i > top, n_off, jnp.minimum(top - i, n_off - 1)) for i in ki]
            s = [[lax.dot_general(kt[j], qa_ref[hl], (((1,), (1,)), ((), ())), preferred_element_type=F32)
                  + tab_ref[off[j], hl * ATT_TK:(hl + 1) * ATT_TK, :] for j in tiles] for hl in heads]
            m_new = [functools.reduce(jnp.maximum, [carry[hl][0]] + [jnp.max(s[hl][j], axis=0, keepdims=True)
                                                                    for j in tiles]) for hl in heads]
            alpha = [jnp.exp(carry[hl][0] - m_new[hl]) for hl in heads]
            p = [[jnp.exp(s[hl][j] - m_new[hl]) for j in tiles] for hl in heads]
            l = [alpha[hl] * carry[hl][1] + sum(jnp.sum(p[hl][j], axis=0, keepdims=True) for j in tiles)
                 for hl in heads]
            acc = [alpha[hl] * carry[hl][2] + sum(jnp.dot(vt[j], p[hl][j].astype(BF), preferred_element_type=F32)
                                                  for j in tiles) for hl in heads]
            return tuple((m_new[hl], l[hl], acc[hl]) for hl in heads)

        init = tuple((jnp.full((1, tq), NEG_INF, F32), jnp.zeros((1, tq), F32), jnp.zeros((HD_N, tq), F32))
                     for _ in heads)
        res = lax.fori_loop(0, (top - lo) // step_tiles + 1, body, init)
        return [acc / l for _, l, acc in res]

    o_sel = attend(ks_ref, vs_ref, tabs_ref, 0, N_SEL_OFFS, SEL_STEP_TILES)
    o_win = attend(kw_ref, vw_ref, tabw_ref, jnp.maximum(top + 1 - N_WIN_OFFS, 0), N_WIN_OFFS, WIN_STEP_TILES)
    gates = _sigmoid(gn_ref[...])
    gates_t = gates.T
    for hl in heads:
        sl = slice(hl * HD_N, (hl + 1) * HD_N)
        o_t = gates_t[3 * hl + 1:3 * hl + 2, :] * o_sel[hl] + gates_t[3 * hl + 2:3 * hl + 3, :] * o_win[hl]
        o_ref[:, sl] = (gates[:, 3 * hl:3 * hl + 1] * oc_ref[:, sl] + o_t.T).astype(o_ref.dtype)


def _swa_prompt(p_n, qa, ks, vs, kw, vw, tab_s, tab_w, o_cmp, t):
    b = qa.shape[0]
    tq = ATT_TILE
    nqt = t // tq
    nkt = t // ATT_TK
    gw = HPG * HD_N
    k_spec = pl.BlockSpec((None, None, nkt, ATT_TK, QA_COLS), lambda i, g, j: (i, g, 0, 0, 0))
    v_spec = pl.BlockSpec((None, None, nkt, HD_N, ATT_TK), lambda i, g, j: (i, g, 0, 0, 0))
    tab_spec = lambda n: pl.BlockSpec((n + 1, HPG * ATT_TK, tq), lambda i, g, j: (0, g, 0))
    return pl.pallas_call(
        _swa_kernel,
        grid=(b, N_KVG, nqt),
        in_specs=[pl.BlockSpec((None, HPG, tq, QA_COLS), lambda i, g, j: (i, g, j, 0)),
                  k_spec, v_spec, k_spec, v_spec, tab_spec(N_SEL_OFFS), tab_spec(N_WIN_OFFS),
                  pl.BlockSpec((tq, GN_GROUP_COLS), lambda i, g, j: (i * nqt + j, NP_GN // GN_GROUP_COLS + g)),
                  pl.BlockSpec((tq, gw), lambda i, g, j: (i * nqt + j, g))],
        out_specs=pl.BlockSpec((tq, gw), lambda i, g, j: (i * nqt + j, g)),
        out_shape=jax.ShapeDtypeStruct((b * t, C_N), BF),
        compiler_params=_cparams("arbitrary", "arbitrary", "arbitrary"),
        name="nsa_sel_win",
    )(qa, ks, vs, kw, vw, tab_s, tab_w, p_n, o_cmp)


def _swa_operands(p_n, sel, b, t):
    nsb = t // L_SEL
    nkt = t // ATT_TK
    q = (p_n[:, :C_N] * (HD_N ** -0.5)).reshape(b, t, H_N, HD_N).transpose(0, 2, 1, 3)
    pen = jnp.where(jnp.transpose(sel, (0, 1, 3, 2)) > 0.5, 0.0, NEG_INF).astype(F32)
    pen = jnp.broadcast_to(pen[:, :, None], (b, N_KVG, HPG, t, nsb)).reshape(b, H_N, t, nsb)
    qa = jnp.concatenate([q, pen], axis=-1).astype(BF)

    def split(cols):
        kv = p_n[:, cols:cols + KV_COLS].reshape(b, t, 2, N_KVG, HD_N)
        k = jnp.transpose(kv[:, :, 0], (0, 2, 1, 3))
        v = jnp.transpose(kv[:, :, 1], (0, 2, 3, 1)).reshape(b, N_KVG, HD_N, nkt, ATT_TK)
        return k, jnp.transpose(v, (0, 1, 3, 2, 4)).astype(BF)

    onehot = jnp.asarray(np.arange(t)[:, None] // L_SEL == np.arange(nsb)[None, :], F32)
    k_s, v_s = split(NP_KVS)
    k_w, v_w = split(NP_KVW)
    ext = lambda k, e: jnp.concatenate([k, jnp.broadcast_to(e, (b, N_KVG, t, nsb))], axis=-1).astype(BF) \
        .reshape(b, N_KVG, nkt, ATT_TK, QA_COLS)
    return qa, ext(k_s, onehot), v_s, ext(k_w, jnp.zeros((t, nsb), F32)), v_w


def _sample_cmp_kernel(q_ref, kv_ref, rbt_ref, ov_ref, o_ref, idx_ref, *, past):
    ncp = kv_ref.shape[0]
    nsbp = ov_ref.shape[1]
    nsb = -(-(past + 1) // L_SEL)
    q = q_ref[...] * (HD_N ** -0.5)
    hrow = lax.broadcasted_iota(I32, (H_N, 1), 0)
    nidx = lax.broadcasted_iota(I32, (1, ncp), 1)
    valid = nidx < ncp - 1
    bias = _bias_rows(past - (nidx * STRIDE + (L_CMP - 1)), rbt_ref[...])
    logits = jnp.zeros((H_N, ncp), F32)
    for gi in range(N_KVG):
        lg = _dot_nt(q, kv_ref[:, gi * HD_N:(gi + 1) * HD_N])
        logits = jnp.where(hrow // HPG == gi, lg, logits)
    pc = _softmax_rows(logits + bias, valid) * valid.astype(F32)
    o = jnp.zeros((H_N, HD_N), F32)
    for gi in range(N_KVG):
        og = _dot(pc, kv_ref[:, (N_KVG + gi) * HD_N:(N_KVG + gi + 1) * HD_N])
        o = jnp.where(hrow // HPG == gi, og, o)
    o_ref[...] = o
    imp_h = _dot(pc, ov_ref[...])
    blk = lax.broadcasted_iota(I32, (8, nsbp), 1)
    grow = lax.broadcasted_iota(I32, (8, 1), 0)
    cur = past // L_SEL
    score = jnp.full((8, nsbp), -3e38, F32)
    for gi in range(N_KVG):
        imp = jnp.sum(jnp.where(hrow // HPG == gi, imp_h, 0.0), axis=0, keepdims=True)
        score = jnp.where(grow == gi, imp, score)
    forced = (blk == 0) | (blk == cur) | (blk == cur - 1)
    score = jnp.where(forced, FORCE, jnp.where(blk > cur, -FORCE, score))
    score = jnp.where((blk < nsb) & (grow < N_KVG), score, -3e38)
    lane = lax.broadcasted_iota(I32, (8, LANES), 1)
    picks = jnp.zeros((8, LANES), I32)
    for it in range(N_TOP):
        mx = jnp.max(score, axis=-1, keepdims=True)
        pick = jnp.min(jnp.where(score == mx, blk, nsbp), axis=-1, keepdims=True)
        picks = jnp.where(lane == it, pick, picks)
        score = jnp.where(blk == pick, -3e38, score)
    idx_ref[...] = picks


def _sample_cmp(q, kv_cmp, rel_bias, past):
    b, ncp, _ = kv_cmp.shape
    nc = ncp - 1
    nsb = -(-(past + 1) // L_SEL)
    nsbp = -(-nsb // LANES) * LANES
    s = np.arange(nc)[:, None] * STRIDE
    j = np.arange(nsb)[None, :] * L_SEL
    ov = np.clip(np.minimum(s + L_CMP, j + L_SEL) - np.maximum(s, j), 0, None) / L_CMP
    ov = np.pad(ov, ((0, ncp - nc), (0, nsbp - nsb))).astype(np.float32)
    return pl.pallas_call(
        functools.partial(_sample_cmp_kernel, past=past),
        grid=(b,),
        in_specs=[pl.BlockSpec((None, H_N, HD_N), lambda i: (i, 0, 0)),
                  pl.BlockSpec((None, ncp, KV_COLS), lambda i: (i, 0, 0)),
                  pl.BlockSpec((H_N, N_BUCKETS), lambda i: (0, 0)),
                  pl.BlockSpec((ncp, nsbp), lambda i: (0, 0))],
        out_specs=[pl.BlockSpec((None, H_N, HD_N), lambda i: (i, 0, 0)),
                   pl.BlockSpec((None, 8, LANES), lambda i: (i, 0, 0))],
        out_shape=[jax.ShapeDtypeStruct((b, H_N, HD_N), F32), jax.ShapeDtypeStruct((b, 8, LANES), I32)],
        compiler_params=_cparams("arbitrary"),
        name="nsa_sample_cmp_select",
    )(q, kv_cmp, rel_bias.T, jnp.asarray(ov, BF))


def _block_copy_kernel(pg_ref, *refs):
    del pg_ref
    x_refs, o_ref = refs[:-1], refs[-1]
    for n, x_ref in enumerate(x_refs):
        o_ref[n] = x_ref[...]


def _gather_sel_pages(cache_t, page):
    rows, nslot = page.shape

    def slot_spec(n):
        return pl.BlockSpec((None, 2, None, HD_N, PAGE_SIZE), lambda i, pg: (pg[i, n], 0, i % N_KVG, 0, 0))

    return pl.pallas_call(
        _block_copy_kernel,
        grid_spec=pltpu.PrefetchScalarGridSpec(
            num_scalar_prefetch=1, grid=(rows,),
            in_specs=[slot_spec(n) for n in range(nslot)],
            out_specs=pl.BlockSpec((None, nslot, 2, HD_N, PAGE_SIZE), lambda i, pg: (i, 0, 0, 0, 0))),
        out_shape=jax.ShapeDtypeStruct((rows, nslot, 2, HD_N, PAGE_SIZE), cache_t.dtype),
        compiler_params=_cparams("arbitrary"),
        name="gather_sel_pages",
    )(page, *([cache_t] * nslot))


def _sample_swa_kernel(idx_ref, q_ref, blk_ref, win_ref, new_ref, rbt_ref, gate_ref, oc_ref, o_ref, *, past):
    bi = pl.program_id(0)
    q = q_ref[...] * (HD_N ** -0.5)
    rbt = rbt_ref[...]
    hrow = lax.broadcasted_iota(I32, (H_N, 1), 0)
    nk = N_TOP * PAGE_SIZE
    lane = lax.broadcasted_iota(I32, (1, nk), 1)
    new_blk = past // L_SEL
    bias_new = rbt[:, 0:1]
    gates = _sigmoid(gate_ref[...])
    nwin = win_ref.shape[0]
    wdist = nwin - lax.broadcasted_iota(I32, (1, nwin), 1)
    wbias = _bias_rows(wdist, rbt)
    wvalid = (wdist >= 0) & (wdist <= WINDOW)

    def with_new(logits, valid, weigh, k_new, v_new):
        l_new = jnp.sum(q * k_new, axis=-1, keepdims=True) + bias_new
        lm = jnp.where(valid, logits, NEG_INF)
        m = jnp.maximum(jnp.max(lm, axis=-1, keepdims=True), l_new)
        e = jnp.where(valid, jnp.exp(lm - m), 0.0)
        e_new = jnp.exp(l_new - m)
        den = jnp.sum(e, axis=-1, keepdims=True) + e_new
        return (weigh(e) + e_new * v_new) / den

    o_sel = jnp.zeros((H_N, HD_N), F32)
    o_win = jnp.zeros((H_N, HD_N), F32)
    bpp = PAGE_SIZE // L_SEL
    tok = lane % PAGE_SIZE
    for gi in range(N_KVG):
        ksl = slice(gi * HD_N, (gi + 1) * HD_N)
        vsl = slice((N_KVG + gi) * HD_N, (N_KVG + gi + 1) * HD_N)
        bid = jnp.zeros((1, nk), I32)
        for n in range(N_TOP):
            bid = jnp.where(lane // PAGE_SIZE == n, idx_ref[bi, gi, n], bid)
        dist = past - ((bid // bpp) * PAGE_SIZE + tok)
        valid = (bid != new_blk) & (tok // L_SEL == bid % bpp) & (dist >= 0)
        kt = jnp.concatenate([blk_ref[gi, n, 0] for n in range(N_TOP)], axis=1)
        vt = jnp.concatenate([blk_ref[gi, n, 1] for n in range(N_TOP)], axis=1)
        logits = _dot(q, kt) + _bias_rows(dist, rbt)
        og = with_new(logits, valid, lambda e, vt=vt: _dot_nt(e, vt), new_ref[0:1, ksl], new_ref[0:1, vsl])
        o_sel = jnp.where(hrow // HPG == gi, og, o_sel)
        logits = _dot_nt(q, win_ref[:, ksl]) + wbias
        og = with_new(logits, wvalid, lambda e, vsl=vsl: _dot(e, win_ref[:, vsl]), new_ref[1:2, ksl],
                      new_ref[1:2, vsl])
        o_win = jnp.where(hrow // HPG == gi, og, o_win)
    o_ref[...] = gates[:, 0:1] * oc_ref[...] + gates[:, 1:2] * o_sel + gates[:, 2:3] * o_win


def _sample_swa(idx, q, blocks, win, new_kv, rel_bias, gates, o_cmp, past):
    b = q.shape[0]
    w = win.shape[1]
    grid_spec = pltpu.PrefetchScalarGridSpec(
        num_scalar_prefetch=1, grid=(b,),
        in_specs=[pl.BlockSpec((None, H_N, HD_N), lambda i, ix: (i, 0, 0)),
                  pl.BlockSpec((None, N_KVG, N_TOP, 2, HD_N, PAGE_SIZE), lambda i, ix: (i, 0, 0, 0, 0, 0)),
                  pl.BlockSpec((None, w, KV_COLS), lambda i, ix: (i, 0, 0)),
                  pl.BlockSpec((None, 2, KV_COLS), lambda i, ix: (i, 0, 0)),
                  pl.BlockSpec((H_N, N_BUCKETS), lambda i, ix: (0, 0)),
                  pl.BlockSpec((None, H_N, 3), lambda i, ix: (i, 0, 0)),
                  pl.BlockSpec((None, H_N, HD_N), lambda i, ix: (i, 0, 0))],
        out_specs=pl.BlockSpec((None, H_N, HD_N), lambda i, ix: (i, 0, 0)))
    return pl.pallas_call(
        functools.partial(_sample_swa_kernel, past=past),
        grid_spec=grid_spec,
        out_shape=jax.ShapeDtypeStruct((b, H_N, HD_N), F32),
        compiler_params=_cparams("arbitrary"),
        name="nsa_sample_sel_win",
    )(idx, q, blocks, win, new_kv, rel_bias.T, gates, o_cmp)


ROUTER_COLS = LANES
MOE_ROWS_PROMPT = 128
MOE_ROWS_SAMPLE = 16


def _router_kernel(x_ref, g_ref, sh_ref, sc_ref, w_ref, b_ref, h_ref, e_ref, wt_ref, rk_ref, cnt_ref):
    i = pl.program_id(0)
    tm = x_ref.shape[0]

    @pl.when(i == 0)
    def _():
        cnt_ref[...] = jnp.zeros_like(cnt_ref)

    h = (_rms(x_ref[...], g_ref[...]) * (1.0 + sc_ref[...]) + sh_ref[...]).astype(BF)
    _store_folded(h_ref, h.astype(F32))
    logits = jnp.dot(h, w_ref[...], preferred_element_type=F32) + b_ref[...]
    lane = lax.broadcasted_iota(I32, (tm, ROUTER_COLS), 1)

    def top1(vals, ok):
        vm = jnp.where(ok, vals, -3e38)
        mx = jnp.max(vm, axis=-1, keepdims=True)
        return mx, jnp.min(jnp.where(ok & (vm == mx), lane, ROUTER_COLS), axis=-1, keepdims=True)

    isg = lane < N_EGROUPS
    pg = _softmax_rows(logits, isg)
    g_w, g_i = top1(pg, isg)
    ise = (lane >= N_EGROUPS) & ((lane - N_EGROUPS) // EXP_PER_GROUP == g_i)
    pe = _softmax_rows(logits, ise)
    w0, l0 = top1(pe, ise)
    w1, l1 = top1(pe, ise & (lane != l0))
    den = w0 + w1
    e0 = l0 - N_EGROUPS
    e1 = l1 - N_EGROUPS
    e_ref[...] = jnp.where(lane == 0, e0, jnp.where(lane == 1, e1, 0))
    wt_ref[...] = jnp.where(lane == 0, w0 / den * g_w, jnp.where(lane == 1, w1 / den * g_w, 0.0))
    oh0 = (lane == e0).astype(F32)
    oh1 = (lane == e1).astype(F32)
    cnt = oh0 + oh1
    ti = lax.broadcasted_iota(I32, (tm, tm), 0)
    si = lax.broadcasted_iota(I32, (tm, tm), 1)
    before = _dot((ti > si).astype(F32), cnt) + cnt_ref[...]
    r0 = jnp.sum(before * oh0, axis=-1, keepdims=True)
    r1 = jnp.sum(before * oh1, axis=-1, keepdims=True)
    rk_ref[...] = jnp.where(lane == 0, r0, jnp.where(lane == 1, r1, 0.0)).astype(I32)
    cnt_ref[...] = cnt_ref[...] + jnp.sum(cnt, axis=0, keepdims=True)


def _router(x, g, shift, scale, w_r, b_r, tm, rpb):
    m = x.shape[0]
    r = shift.shape[1]
    rows = lambda tn: pl.BlockSpec((tm, tn), lambda i: (i, 0))
    mods = pl.BlockSpec((None, r, D_MODEL), lambda i: ((i * tm) // rpb, 0, 0))
    small = lambda dt: jax.ShapeDtypeStruct((m, ROUTER_COLS), dt)
    return pl.pallas_call(
        _router_kernel,
        grid=(m // tm,),
        in_specs=[rows(D_MODEL), pl.BlockSpec((1, D_MODEL), lambda i: (0, 0)), mods, mods,
                  pl.BlockSpec((D_MODEL, ROUTER_COLS), lambda i: (0, 0)),
                  pl.BlockSpec((1, ROUTER_COLS), lambda i: (0, 0))],
        out_specs=[pl.BlockSpec((tm * ROW_FOLD, LANES), lambda i: (i, 0)),
                   rows(ROUTER_COLS), rows(ROUTER_COLS), rows(ROUTER_COLS),
                   pl.BlockSpec((1, ROUTER_COLS), lambda i: (0, 0))],
        out_shape=[jax.ShapeDtypeStruct((m * ROW_FOLD, LANES), F32), small(I32), small(F32), small(I32),
                   jax.ShapeDtypeStruct((1, ROUTER_COLS), F32)],
        compiler_params=_cparams("arbitrary"),
        name="moe_router",
    )(x, g.reshape(1, D_MODEL), shift, scale, w_r, b_r)


ROW_FOLD = D_MODEL // LANES
ROW_PITCH = ROW_FOLD + 8


def _store_folded(ref, x):
    n = x.shape[0]
    for c in range(ROW_FOLD):
        ref[pl.ds(c, n, stride=ROW_FOLD), :] = x[:, c * LANES:(c + 1) * LANES]


def _load_folded(ref, first_row, n):
    return jnp.concatenate([ref[pl.ds(first_row * ROW_PITCH + c, n, stride=ROW_PITCH), :] for c in range(ROW_FOLD)],
                           axis=1)


def _row_gather_ring(src_hbm, buf, sems, groups, idx_now, idx_next, inline_next=False, priorities=(0, 1)):
    i = pl.program_id(0)
    last = pl.num_programs(0) - 1
    slot = i % 2
    total = sum(cnt for _, cnt, _ in groups)
    assert 2 * total * ROW_PITCH == buf.shape[0]

    def rows_at(row, pitch):
        return pl.ds(pl.multiple_of(row * pitch, 8), ROW_FOLD)

    def start(idx, s, first, k, r, priority):
        pltpu.make_async_copy(src_hbm.at[rows_at(idx(k, r), ROW_FOLD)],
                              buf.at[rows_at(s * total + first + r, ROW_PITCH)], sems.at[s]).start(priority=priority)

    def start_all(idx, s):
        for first, cnt, k in groups:
            per_trip = math.gcd(cnt, 8)

            def issue(j, c):
                for u in range(per_trip):
                    start(idx, s, first, k, per_trip * j + u, priorities[u % 2])
                return c
            lax.fori_loop(0, cnt // per_trip, issue, 0)

    def wait_slot(s):
        span = buf.at[pl.ds(pl.multiple_of(s * total * ROW_PITCH, 8), total * ROW_FOLD)]
        pltpu.make_async_copy(span, span, sems.at[s]).wait()

    @pl.when(i == 0)
    def _():
        start_all(idx_now, slot)

    def finish():
        if inline_next:
            @pl.when(i == last)
            def _():
                wait_slot(1 - slot)

    if inline_next:
        wait_slot(slot)
        for first, cnt, k in groups:
            for r in range(cnt):
                start(idx_next, 1 - slot, first, k, r, priorities[r % 2])
    else:
        @pl.when(i < last)
        def _():
            start_all(idx_next, 1 - slot)

        wait_slot(slot)
    return slot * total, finish


def _expert_kernel(be_ref, rt_ref, rtn_ref, h_hbm, w1_ref, w3_ref, w2_ref, o_ref, xbuf, sems, w1b, w3b, w2b):
    i = pl.program_id(0)
    blk = rt_ref.shape[1]

    @pl.when((i == 0) | (be_ref[i] != be_ref[jnp.maximum(i - 1, 0)]))
    def _():
        w1b[...] = w1_ref[...].astype(BF)
        w3b[...] = w3_ref[...].astype(BF)
        w2b[...] = w2_ref[...].astype(BF)

    base, finish = _row_gather_ring(h_hbm, xbuf, sems, ((0, blk, 0),), lambda k, r: rt_ref[0, r],
                                    lambda k, r: rtn_ref[0, r], inline_next=True, priorities=(1, 1))
    x = _load_folded(xbuf, base, blk).astype(BF)
    a = jnp.dot(x, w1b[...], preferred_element_type=F32)
    b = jnp.dot(x, w3b[...], preferred_element_type=F32)
    hid = a * _sigmoid(a) * b
    _store_folded(o_ref, jnp.dot(hid.astype(BF), w2b[...], preferred_element_type=F32))
    finish()


def _experts(h2, row_tok, blk_exp, w1, w3, w2):
    nblk, _, blk = row_tok.shape
    idx_spec = lambda d: pl.BlockSpec((None, 1, blk), lambda i, be: (jnp.minimum(i + d, nblk - 1), 0, 0),
                                      memory_space=pltpu.SMEM)
    grid_spec = pltpu.PrefetchScalarGridSpec(
        num_scalar_prefetch=1, grid=(nblk,),
        in_specs=[idx_spec(0), idx_spec(1),
                  pl.BlockSpec(memory_space=pl.ANY),
                  pl.BlockSpec((None, D_MODEL, D_EXP), lambda i, be: (be[i], 0, 0)),
                  pl.BlockSpec((None, D_MODEL, D_EXP), lambda i, be: (be[i], 0, 0)),
                  pl.BlockSpec((None, D_EXP, D_MODEL), lambda i, be: (be[i], 0, 0))],
        out_specs=pl.BlockSpec((blk * ROW_FOLD, LANES), lambda i, be: (i, 0)),
        scratch_shapes=[pltpu.VMEM((2 * blk * ROW_PITCH, LANES), F32), pltpu.SemaphoreType.DMA((2,)),
                        pltpu.VMEM((D_MODEL, D_EXP), BF), pltpu.VMEM((D_MODEL, D_EXP), BF),
                        pltpu.VMEM((D_EXP, D_MODEL), BF)])
    return pl.pallas_call(
        _expert_kernel,
        grid_spec=grid_spec,
        out_shape=jax.ShapeDtypeStruct((nblk * blk * ROW_FOLD, LANES), F32),
        compiler_params=_cparams("arbitrary"),
        name="moe_experts",
    )(blk_exp, row_tok, row_tok, h2, w1, w3, w2)


def _final_kernel(x_ref, g_ref, dest_ref, destn_ref, ys_hbm, wt_ref, nf_ref, o_ref, ybuf, sems):
    tm = x_ref.shape[0]
    groups = tuple((k * tm, tm, k) for k in range(TOP_K))
    base, _ = _row_gather_ring(ys_hbm, ybuf, sems, groups, lambda k, r: dest_ref[k, r], lambda k, r: destn_ref[k, r])
    wt = wt_ref[...]
    moe = wt[:, 0:1] * _load_folded(ybuf, base, tm) + wt[:, 1:2] * _load_folded(ybuf, base + tm, tm)
    o_ref[...] = _rms(x_ref[...] + g_ref[...] * moe, nf_ref[...])


def _final(x, gate, ys, dest, wts, norm_f, tm, rpb):
    m = x.shape[0]
    r = gate.shape[1]
    nt = m // tm
    rows = lambda tn: pl.BlockSpec((tm, tn), lambda i: (i, 0))
    idx_spec = lambda d: pl.BlockSpec((None, TOP_K, tm), lambda i: (jnp.minimum(i + d, nt - 1), 0, 0),
                                      memory_space=pltpu.SMEM)
    return pl.pallas_call(
        _final_kernel,
        grid=(nt,),
        in_specs=[rows(D_MODEL), pl.BlockSpec((None, r, D_MODEL), lambda i: ((i * tm) // rpb, 0, 0)),
                  idx_spec(0), idx_spec(1),
                  pl.BlockSpec(memory_space=pl.ANY),
                  rows(ROUTER_COLS), pl.BlockSpec((1, D_MODEL), lambda i: (0, 0))],
        out_specs=rows(D_MODEL),
        out_shape=jax.ShapeDtypeStruct((m, D_MODEL), F32),
        scratch_shapes=[pltpu.VMEM((2 * TOP_K * tm * ROW_PITCH, LANES), F32), pltpu.SemaphoreType.DMA((2,))],
        compiler_params=_cparams("arbitrary"),
        name="moe_combine_final_norm",
    )(x, gate, dest, dest, ys, wts, norm_f.reshape(1, D_MODEL))


def _moe_and_final(x1, g2, shift, scale, gate, w_r, b_r, exp_w1, exp_w3, exp_w2, norm_f, tm, rpb, blk):
    m = x1.shape[0]
    h2, eid, wts, rank, counts = _router(x1, g2, shift, scale, w_r, b_r, tm, rpb)
    counts = counts[0, :N_EXP].astype(I32)
    padded = (counts + blk - 1) // blk * blk
    pend = jnp.cumsum(padded)
    pstart = pend - padded
    n_blocks = -(-(m * TOP_K) // blk) + N_EXP
    starts = jnp.arange(n_blocks, dtype=I32)[:, None] * blk
    blk_exp = jnp.minimum(jnp.sum((pend[None, :] <= starts).astype(I32), axis=1), N_EXP - 1)
    e = eid[:, :TOP_K]
    dest = pstart[e] + rank[:, :TOP_K]
    tok = jnp.broadcast_to(jnp.arange(m, dtype=I32)[:, None], (m, TOP_K))
    row_tok = jnp.zeros((n_blocks * blk,), I32).at[dest.reshape(-1)].set(tok.reshape(-1))
    ys = _experts(h2, row_tok.reshape(n_blocks, 1, blk), blk_exp, exp_w1, exp_w3, exp_w2)
    dest_t = jnp.transpose(dest.reshape(m // tm, tm, TOP_K), (0, 2, 1))
    return _final(x1, gate, ys, dest_t, wts, norm_f, tm, rpb)


def _pack_in_proj(w_in):
    o = C_RIN
    w_r = w_in[:, :o]
    w_q = w_in[:, o:o + C_N + 3 * KV_COLS]
    o += C_N + 3 * KV_COLS
    w_gn = w_in[:, o:o + 3 * H_N].reshape(D_MODEL, N_KVG, 3 * HPG)
    w_gn = jnp.pad(w_gn, ((0, 0), (0, 0), (0, GN_GROUP_COLS - 3 * HPG))).reshape(D_MODEL, N_KVG * GN_GROUP_COLS)
    o += 3 * H_N
    w_gm = w_in[:, o:]
    return w_r.astype(BF), jnp.concatenate([w_q, w_gn], axis=1).astype(BF), w_gm.astype(BF)


def _prompt_bias_tables(rel_bias, t):
    tq, tk = ATT_TILE, ATT_TK
    i = np.arange(tq)[None, :]
    j = np.arange(tk)[:, None]
    dist = np.stack([tk * (o - (ATT_R - 1)) + i - j for o in range(N_WIN_OFFS)]).astype(np.int32)
    raw = _bias_table(jnp.asarray(dist.reshape(N_WIN_OFFS * tk, tq)), rel_bias, tk)
    raw = jnp.transpose(raw.reshape(H_N, N_WIN_OFFS, tk, tq), (1, 0, 2, 3))
    ok_w = jnp.asarray((dist >= 0) & (dist <= WINDOW))[:, None]
    ok_s = jnp.asarray(dist[:N_SEL_OFFS] >= 0)[:, None]
    masked = jnp.full((1, H_N * tk, tq), NEG_INF, F32)
    tab_w = jnp.concatenate([jnp.where(ok_w, raw, NEG_INF).reshape(N_WIN_OFFS, H_N * tk, tq), masked])
    tab_s = jnp.concatenate([jnp.where(ok_s, raw[:N_SEL_OFFS], NEG_INF).reshape(N_SEL_OFFS, H_N * tk, tq), masked])
    nc = (t - L_CMP) // STRIDE + 1
    ncp = nc + 1
    dc = (np.arange(t)[:, None] - (np.arange(ncp)[None, :] * STRIDE + L_CMP - 1)).astype(np.int32)
    return tab_s, tab_w, _bias_table(jnp.asarray(dc), rel_bias, tq)


def kernel(x_prompt, x_sample, c_prompt, c_sample, cache_cmp_kv, cache_sel_kv, state_win_kv, state_rwkv_shift,
           state_rwkv_wkv, page_table, rel_bias, norm_f, norm1, norm2, w_ada, b_ada, w_in, rwkv_mu, rwkv_w0, rwkv_w2,
           rwkv_a0, rwkv_a2, rwkv_g2, rwkv_kk, rwkv_ka, rwkv_rk, rwkv_ln_g, rwkv_ln_b, cmp_pos, cmp_w1, cmp_w2,
           w_o_rwkv, w_o_nsa, w_out, router_wg, router_bg, router_we, router_be, exp_w1, exp_w3, exp_w2):
    bp, t, _ = x_prompt.shape
    bs = x_sample.shape[0]
    mp = bp * t
    past = page_table.shape[1] * PAGE_SIZE

    nrow = -(-(bp + bs) // 8) * 8
    c_all = jnp.concatenate([c_prompt, c_sample, jnp.zeros((nrow - bp - bs, D_MODEL), F32)], axis=0)
    mod = _ada(c_all, w_ada[0], b_ada[0]).reshape(nrow, 6, D_MODEL)
    mod_p = [mod[:bp, i][:, None, :] for i in range(6)]
    mod_s = [mod[bp:bp + bs, i][None] for i in range(6)]

    w_r, w_n, w_gm = _pack_in_proj(w_in[0])
    rw = _rwkv_weights(rwkv_mu[0], rwkv_w0[0], rwkv_w2[0], rwkv_a0[0], rwkv_a2[0], rwkv_g2[0], rwkv_kk[0],
                       rwkv_ka[0], rwkv_rk[0], rwkv_ln_g[0], rwkv_ln_b[0])
    wbd = _cmp_weights(cmp_w1[0], N_KVG)
    cpos = _cmp_partial_rows(_cmp_pos_rows(cmp_pos[0]), wbd, 8)
    wo_r, wo_n, wo = w_o_rwkv[0].astype(BF), w_o_nsa[0].astype(BF), w_out[0].astype(BF)
    w_router = jnp.pad(jnp.concatenate([router_wg[0], router_we[0]], axis=1),
                       ((0, 0), (0, ROUTER_COLS - N_EGROUPS - N_EXP))).astype(BF)
    b_router = jnp.pad(jnp.concatenate([router_bg[0], router_be[0]]), (0, ROUTER_COLS - N_EGROUPS - N_EXP))[None]

    tm = 512
    xp = x_prompt.reshape(mp, D_MODEL)
    h = _norm_mod(xp, norm1[0], mod_p[0], mod_p[1], tm, t)
    p_r = _matmul(h, w_r, tm, C_RIN // 2)
    p_n = _matmul(h, w_n, tm, NP_COLS // 2)
    p_g = _matmul(h, w_gm, tm, 2048)
    o_r, shift_p, wkv_p = _rwkv_prompt(p_r.reshape(bp, t, C_RIN), rw)
    kvc = p_n[:, NP_KVC:NP_KVC + KV_COLS]
    kvs = p_n[:, NP_KVS:NP_KVS + KV_COLS]
    kvw = p_n[:, NP_KVW:NP_KVW + KV_COLS]
    nch = t // STRIDE
    c_part = _cmp_partial_rows(kvc.reshape(bp * nch, STRIDE * KV_COLS), wbd, nch)
    kv_cmp = _cmp_finish(c_part.reshape(bp, nch, -1), cpos, cmp_w2[0])
    tab_s, tab_w, bias_c = _prompt_bias_tables(rel_bias, t)
    o_cmp, sel = _cmp_attn_prompt(p_n, kv_cmp, bias_c, t)
    o_n = _swa_prompt(p_n, *_swa_operands(p_n, sel, bp, t), tab_s, tab_w, o_cmp, t)
    y = _merge(o_r.reshape(mp, C_R), o_n, wo_r, wo_n, p_g, tm)
    x1 = _proj_residual(y, wo, xp, mod_p[2], tm, t)
    y_prompt = _moe_and_final(x1, norm2[0], mod_p[3], mod_p[4], mod_p[5], w_router, b_router, exp_w1[0], exp_w3[0],
                              exp_w2[0], norm_f, tm, t, MOE_ROWS_PROMPT).reshape(bp, t, D_MODEL)
    kv_shape = (1, bp, t, 2, N_KVG, HD_N)
    wlen = min(WINDOW, t)
    win_p = kvw.reshape(bp, t, KV_COLS)[:, t - wlen:].reshape(1, bp, wlen, 2, N_KVG, HD_N)

    xs = x_sample.reshape(bs, D_MODEL)
    hs = _norm_mod(xs, norm1[0], mod_s[0], mod_s[1], bs, bs)
    ps_r = _matmul(hs, w_r, bs, C_RIN // 2)
    ps_n = _matmul(hs, w_n, bs, NP_COLS // 2)
    ps_g = _matmul(hs, w_gm, bs, 2048)
    os_r, wkv_s = _rwkv_step(ps_r, state_rwkv_shift[0], state_rwkv_wkv[0], rw)
    kvc_s = ps_n[:, NP_KVC:NP_KVC + KV_COLS]
    kvs_s = ps_n[:, NP_KVS:NP_KVS + KV_COLS]
    kvw_s = ps_n[:, NP_KVW:NP_KVW + KV_COLS]
    cs_part = _cmp_partial_paged(jnp.transpose(cache_cmp_kv[0], (0, 2, 3, 4, 1)), page_table,
                                 _cmp_weights(cmp_w1[0], CMP_PAIR))
    kv_cmp_s = _cmp_finish(cs_part, cpos, cmp_w2[0])
    q_s = ps_n[:, :C_N].reshape(bs, H_N, HD_N)
    o_cmp_s, picks = _sample_cmp(q_s, kv_cmp_s, rel_bias, past)
    idx = picks[:, :N_KVG, :N_TOP]
    bpp = PAGE_SIZE // L_SEL
    npb = past // L_SEL
    idc = jnp.minimum(idx, npb - 1)
    page = jnp.take_along_axis(page_table, (idc // bpp).reshape(bs, -1), axis=1).reshape(bs * N_KVG, N_TOP)
    cache_t = jnp.transpose(cache_sel_kv[0], (0, 2, 3, 4, 1))
    blocks = _gather_sel_pages(cache_t, page).reshape(bs, N_KVG, N_TOP, 2, HD_N, PAGE_SIZE)
    win_buf = state_win_kv[0].reshape(bs, -1, KV_COLS)
    gates_s = ps_n[:, NP_GN:].reshape(bs, N_KVG, GN_GROUP_COLS)[:, :, :3 * HPG].reshape(bs, H_N, 3)
    new_kv = jnp.stack([kvs_s, kvw_s], axis=1)
    os_n = _sample_swa(idx, q_s, blocks, win_buf, new_kv, rel_bias, gates_s, o_cmp_s, past)
    ys = _merge(os_r, os_n.reshape(bs, C_N).astype(BF), wo_r, wo_n, ps_g, bs)
    xs1 = _proj_residual(ys, wo, xs, mod_s[2], bs, bs)
    y_sample = _moe_and_final(xs1, norm2[0], mod_s[3], mod_s[4], mod_s[5], w_router, b_router, exp_w1[0], exp_w3[0],
                              exp_w2[0], norm_f, bs, bs, MOE_ROWS_SAMPLE).reshape(bs, 1, D_MODEL)
    kv1 = (1, bs, 1, 2, N_KVG, HD_N)
    wbuf = win_buf.shape[1]
    win_s = jnp.concatenate([win_buf, kvw_s[:, None, :]], axis=1)[:, -wbuf:].reshape(1, bs, wbuf, 2, N_KVG, HD_N)

    return (y_prompt, y_sample,
            kvc.reshape(kv_shape), kvc_s.reshape(kv1),
            kvs.reshape(kv_shape), kvs_s.reshape(kv1),
            win_p, win_s,
            shift_p.reshape(1, bp, C_RIN), ps_r.reshape(1, bs, C_RIN),
            wkv_p[None], wkv_s[None])
```
